```python
import math
import jax
import jax.numpy as jnp
from jax import lax
import numpy as np

D_MODEL = 2048
BATCH = 4
SEQ = 2048
DEPTH = 2

CTX_LEN = 256
GRID_W = 64
BLOCK = 128
NORM_EPS = 1e-6
ROPE_THETA = 10000.0

MLA_HEADS = 8
MLA_NOPE = 128
MLA_ROPE = 64
MLA_V = 128
MLA_Q_LORA = 768
MLA_KV_LORA = 512

HY_WIDTH = 512
HY_SHORT = 3
HY_BANDS = 16
HY_EMB = 1 + 2 * HY_BANDS
HY_FILTER_HIDDEN = 64
HY_DECAY_TARGET = 1e-2
HY_QUICK_DECAY_PCT = 0.3
HY_SLOW_DECAY_PCT = 1.5
HY_FILTER_OUT_SCALE = 0.02

SWA_HEADS = 8
SWA_KV_HEADS = 2
SWA_HEAD_DIM = 64
SWA_WINDOW = 128

IN_SIZES = (MLA_Q_LORA, MLA_KV_LORA, MLA_ROPE, 3 * HY_WIDTH, SWA_HEADS * SWA_HEAD_DIM, SWA_KV_HEADS * SWA_HEAD_DIM, SWA_KV_HEADS * SWA_HEAD_DIM)
IN_WIDTH = sum(IN_SIZES)
MIX_WIDTH = MLA_HEADS * MLA_V + HY_WIDTH + SWA_HEADS * SWA_HEAD_DIM

D_FF = 5632
N_EXPERTS = 8
TOP_K = 2
D_FF_EXPERT = 7168

kernel_name = "hybrid_mla_hyena_swa_moe_dit_block"


def rms_norm(x, g):
    xf = x.astype(jnp.float32)
    y = xf * lax.rsqrt(jnp.mean(xf * xf, axis=-1, keepdims=True) + NORM_EPS)
    return (y * g.astype(jnp.float32)).astype(x.dtype)


def modulate(h, shift, scale):
    return h * (1.0 + scale) + shift


def axial_rope(L, dim):
    rows = L // GRID_W
    row = jnp.repeat(jnp.arange(rows), GRID_W).astype(jnp.float32)
    col = jnp.tile(jnp.arange(GRID_W), rows).astype(jnp.float32)
    quarter = dim // 4
    freqs = ROPE_THETA ** (-jnp.arange(quarter, dtype=jnp.float32) / quarter)
    ang = jnp.concatenate([row[:, None] * freqs[None], col[:, None] * freqs[None]], axis=-1)
    return jnp.cos(ang), jnp.sin(ang)


def apply_rope(x, cos, sin):
    half = x.shape[-1] // 2
    x1, x2 = x[..., :half], x[..., half:]
    c = cos[None, :, None, :].astype(x.dtype)
    s = sin[None, :, None, :].astype(x.dtype)
    return jnp.concatenate([x1 * c - x2 * s, x1 * s + x2 * c], axis=-1)


def split_in(u):
    offs = [int(o) for o in np.cumsum(IN_SIZES)[:-1]]
    return jnp.split(u, offs, axis=-1)


def mla_queries(cq, g_q, w_q_up, rope):
    B, L, _ = cq.shape
    q = (rms_norm(cq, g_q) @ w_q_up).reshape(B, L, MLA_HEADS, MLA_NOPE + MLA_ROPE)
    q_nope, q_pe = q[..., :MLA_NOPE], q[..., MLA_NOPE:]
    if rope is not None:
        q_pe = apply_rope(q_pe, *rope)
    return jnp.concatenate([q_nope, q_pe], axis=-1)


def mla_keys_values(ckv, k_pe, g_kv, w_kv_up, rope):
    B, L, _ = ckv.shape
    kv = (rms_norm(ckv, g_kv) @ w_kv_up).reshape(B, L, MLA_HEADS, MLA_NOPE + MLA_V)
    k_nope, v = kv[..., :MLA_NOPE], kv[..., MLA_NOPE:]
    k_pe = k_pe[:, :, None, :]
    if rope is not None:
        k_pe = apply_rope(k_pe, *rope)
    k = jnp.concatenate([k_nope, jnp.broadcast_to(k_pe, k_nope.shape[:-1] + (MLA_ROPE,))], axis=-1)
    return k, v


def full_attention(q, k, v, scale):
    B, Lq, H, _ = q.shape
    s = jnp.einsum('bqhd,bkhd->bhqk', q, k).astype(jnp.float32) * scale
    p = jax.nn.softmax(s, axis=-1).astype(v.dtype)
    return jnp.einsum('bhqk,bkhd->bqhd', p, v).reshape(B, Lq, H * v.shape[-1])


def blockwise_attention(q, k, v, scale):
    B, L, H, dk = q.shape
    nb = L // BLOCK
    qb = jnp.moveaxis(q.reshape(B, nb, BLOCK, H, dk), 1, 0)

    def one_block(qi):
        s = jnp.einsum('bqhd,bkhd->bhqk', qi, k).astype(jnp.float32) * scale
        p = jax.nn.softmax(s, axis=-1).astype(v.dtype)
        return jnp.einsum('bhqk,bkhd->bqhd', p, v)

    o = lax.map(one_block, qb)
    return jnp.moveaxis(o, 0, 1).reshape(B, L, H * v.shape[-1])


def short_conv(u, w, b):
    ch = u.shape[-1]
    y = lax.conv_general_dilated(u, w[:, None, :].astype(u.dtype), window_strides=(1,),
                                 padding=((HY_SHORT // 2, HY_SHORT // 2),),
                                 dimension_numbers=('NWC', 'WIO', 'NWC'), feature_group_count=ch)
    return y + b


def hyena_filters(L, w1, b1, w2, b2, w3, b3, w_filt, freq):
    pos = jnp.arange(L, dtype=jnp.float32)
    t = pos / max(L - 1, 1)
    bands = jnp.linspace(1e-4, HY_BANDS - 1, HY_BANDS, dtype=jnp.float32)
    ang = (2.0 * math.pi / L) * pos[:, None] * bands[None]
    z = jnp.concatenate([t[:, None], jnp.cos(ang), -jnp.sin(ang)], axis=-1).astype(w1.dtype)
    h = jnp.sin(freq * (z @ w1 + b1))
    h = jnp.sin(freq * (h @ w2 + b2))
    h = jnp.sin(freq * (h @ w3 + b3))
    h = h @ w_filt
    deltas = jnp.linspace(math.log(HY_DECAY_TARGET) / HY_SLOW_DECAY_PCT,
                          math.log(HY_DECAY_TARGET) / HY_QUICK_DECAY_PCT, HY_WIDTH, dtype=jnp.float32)
    decay = jnp.exp(-t[:, None] * jnp.abs(deltas)[None])
    h = h * jnp.concatenate([decay, decay], axis=-1).astype(h.dtype)
    return h[:, :HY_WIDTH], h[:, HY_WIDTH:]


def bidir_long_conv(u, h_f, h_b, skip):
    L, ch = h_f.shape
    taps = jnp.concatenate([h_f, jnp.zeros((1, ch), h_f.dtype), h_b[:0:-1]], axis=0).astype(jnp.float32)
    uf = jnp.fft.rfft(u.astype(jnp.float32), n=2 * L, axis=1)
    tf = jnp.fft.rfft(taps, n=2 * L, axis=0)
    y = jnp.fft.irfft(uf * tf[None], n=2 * L, axis=1)[:, :L]
    return (y + u.astype(jnp.float32) * skip.astype(jnp.float32)).astype(u.dtype)


def hyena_mixer(u, conv_w, conv_b, filt, skip):
    u = short_conv(u, conv_w, conv_b)
    x0, x1, v = jnp.split(u, 3, axis=-1)
    v = bidir_long_conv(v * x1, filt[0], filt[1], skip)
    return x0 * v


def swa_latent(q, k, v, kc, vc, sink):
    B, L, Hq, d = q.shape
    Hk = k.shape[2]
    G = Hq // Hk
    nb = L // BLOCK
    scale = d ** -0.5
    qb = q.reshape(B, nb, BLOCK, Hk, G, d)
    padw = ((0, 0), (BLOCK, BLOCK), (0, 0), (0, 0))
    kp = jnp.pad(k, padw).reshape(B, nb + 2, BLOCK, Hk, d)
    vp = jnp.pad(v, padw).reshape(B, nb + 2, BLOCK, Hk, d)
    kw = jnp.concatenate([kp[:, :-2], kp[:, 1:-1], kp[:, 2:]], axis=2)
    vw = jnp.concatenate([vp[:, :-2], vp[:, 1:-1], vp[:, 2:]], axis=2)
    s_loc = jnp.einsum('bnqhgd,bnkhd->bnhgqk', qb, kw).astype(jnp.float32) * scale
    q_pos = jnp.arange(L).reshape(nb, BLOCK)
    k_pos = (jnp.arange(nb)[:, None] - 1) * BLOCK + jnp.arange(3 * BLOCK)[None]
    valid = ((jnp.abs(q_pos[:, :, None] - k_pos[:, None, :]) <= SWA_WINDOW)
             & (k_pos >= 0)[:, None, :] & (k_pos < L)[:, None, :])
    s_loc = jnp.where(valid[None, :, None, None], s_loc, -jnp.inf)
    s_ctx = jnp.einsum('bnqhgd,bkhd->bnhgqk', qb, kc).astype(jnp.float32) * scale
    s_sink = jnp.broadcast_to(sink.reshape(Hk, G)[None, None, :, :, None, None].astype(jnp.float32),
                              s_loc.shape[:-1] + (1,))
    p = jax.nn.softmax(jnp.concatenate([s_loc, s_ctx, s_sink], axis=-1), axis=-1)
    W = 3 * BLOCK
    C = kc.shape[1]
    p_loc = p[..., :W].astype(v.dtype)
    p_ctx = p[..., W:W + C].astype(v.dtype)
    o = (jnp.einsum('bnhgqk,bnkhd->bnqhgd', p_loc, vw)
         + jnp.einsum('bnhgqk,bkhd->bnqhgd', p_ctx, vc))
    return o.reshape(B, L, Hq * d)


def sink_attention(q, k, v, sink):
    B, L, Hq, d = q.shape
    Hk = k.shape[2]
    G = Hq // Hk
    qg = q.reshape(B, L, Hk, G, d)
    s = jnp.einsum('bqhgd,bkhd->bhgqk', qg, k).astype(jnp.float32) * (d ** -0.5)
    s_sink = jnp.broadcast_to(sink.reshape(Hk, G)[None, :, :, None, None].astype(jnp.float32), s.shape[:-1] + (1,))
    p = jax.nn.softmax(jnp.concatenate([s, s_sink], axis=-1), axis=-1)[..., :-1].astype(v.dtype)
    return jnp.einsum('bhgqk,bkhd->bqhgd', p, v).reshape(B, L, Hq * d)


def mixer(h_lat, h_ctx, lp, need_ctx):
    B, L, _ = h_lat.shape
    C = h_ctx.shape[1]
    rope_mla = axial_rope(L, MLA_ROPE)
    rope_swa = axial_rope(L, SWA_HEAD_DIM)
    mla_scale = (MLA_NOPE + MLA_ROPE) ** -0.5
    cq_l, ckv_l, kpe_l, hy_l, sq_l, sk_l, sv_l = split_in(h_lat @ lp['w_in'])
    cq_c, ckv_c, kpe_c, hy_c, sq_c, sk_c, sv_c = split_in(h_ctx @ lp['w_in'])

    k_l, v_l = mla_keys_values(ckv_l, kpe_l, lp['g_kv'], lp['w_kv_up'], rope_mla)
    k_c, v_c = mla_keys_values(ckv_c, kpe_c, lp['g_kv'], lp['w_kv_up'], None)
    q_l = mla_queries(cq_l, lp['g_q'], lp['w_q_up'], rope_mla)
    a_l = blockwise_attention(q_l, jnp.concatenate([k_l, k_c], axis=1), jnp.concatenate([v_l, v_c], axis=1), mla_scale)

    filt_l = hyena_filters(L, lp['hy_w1'], lp['hy_b1'], lp['hy_w2'], lp['hy_b2'], lp['hy_w3'], lp['hy_b3'],
                           lp['hy_w_filt'], lp['hy_freq'])
    y_l = hyena_mixer(hy_l, lp['hy_conv_w'], lp['hy_conv_b'], filt_l, lp['hy_skip'])

    sq_l = apply_rope(sq_l.reshape(B, L, SWA_HEADS, SWA_HEAD_DIM), *rope_swa)
    sk_l = apply_rope(sk_l.reshape(B, L, SWA_KV_HEADS, SWA_HEAD_DIM), *rope_swa)
    sv_l = sv_l.reshape(B, L, SWA_KV_HEADS, SWA_HEAD_DIM)
    sk_c = sk_c.reshape(B, C, SWA_KV_HEADS, SWA_HEAD_DIM)
    sv_c = sv_c.reshape(B, C, SWA_KV_HEADS, SWA_HEAD_DIM)
    s_l = swa_latent(sq_l, sk_l, sv_l, sk_c, sv_c, lp['swa_sink'])

    out_lat = jnp.concatenate([a_l, y_l, s_l], axis=-1) @ lp['w_out']
    if not need_ctx:
        return out_lat, None

    q_c = mla_queries(cq_c, lp['g_q'], lp['w_q_up'], None)
    a_c = full_attention(q_c, k_c, v_c, mla_scale)
    filt_c = hyena_filters(C, lp['hy_w1'], lp['hy_b1'], lp['hy_w2'], lp['hy_b2'], lp['hy_w3'], lp['hy_b3'],
                           lp['hy_w_filt'], lp['hy_freq'])
    y_c = hyena_mixer(hy_c, lp['hy_conv_w'], lp['hy_conv_b'], filt_c, lp['hy_skip'])
    s_c = sink_attention(sq_c.reshape(B, C, SWA_HEADS, SWA_HEAD_DIM), sk_c, sv_c, lp['swa_sink'])
    out_ctx = jnp.concatenate([a_c, y_c, s_c], axis=-1) @ lp['w_out']
    return out_lat, out_ctx


def swiglu(x, w_gate, w_up, w_down):
    return (jax.nn.silu(x @ w_gate) * (x @ w_up)) @ w_down


def moe_swiglu(x, router, w_gate, w_up, w_down):
    logits = (x @ router).astype(jnp.float32)
    top_v, top_i = lax.top_k(logits, TOP_K)
    top_w = jax.nn.softmax(top_v, axis=-1)
    gates = jnp.sum(jax.nn.one_hot(top_i, N_EXPERTS, dtype=jnp.float32) * top_w[..., None], axis=-2).astype(x.dtype)
    y = jnp.zeros_like(x)
    for e in range(N_EXPERTS):
        y = y + gates[..., e:e + 1] * swiglu(x, w_gate[e], w_up[e], w_down[e])
    return y


def setup_inputs(seed: int = 0) -> dict:
    key = jax.random.key(seed)
    ks = iter(jax.random.split(key, 48))
    f32 = jnp.float32
    D = D_MODEL
    FH = HY_FILTER_HIDDEN
    n_dense = (DEPTH + 1) // 2
    n_moe = DEPTH // 2

    def nrm(shape, scale):
        return jax.random.normal(next(ks), shape, f32) * scale

    def gain(shape):
        return 1.0 + 0.1 * jax.random.normal(next(ks), shape, f32)

    return {
        "x": nrm((BATCH, SEQ, D), 1.0),
        "c": nrm((BATCH, D), 1.0),
        "ctx": nrm((BATCH, CTX_LEN, D), 1.0),
        "c_ctx": nrm((D,), 1.0),
        "w_mod": nrm((DEPTH, D, 6 * D), 0.5 * D ** -0.5),
        "b_mod": nrm((DEPTH, 6 * D), 0.02),
        "g_mix": gain((DEPTH, D)),
        "g_ffn": gain((DEPTH, D)),
        "w_in": nrm((DEPTH, D, IN_WIDTH), D ** -0.5),
        "g_q": gain((DEPTH, MLA_Q_LORA)),
        "w_q_up": nrm((DEPTH, MLA_Q_LORA, MLA_HEADS * (MLA_NOPE + MLA_ROPE)), MLA_Q_LORA ** -0.5),
        "g_kv": gain((DEPTH, MLA_KV_LORA)),
        "w_kv_up": nrm((DEPTH, MLA_KV_LORA, MLA_HEADS * (MLA_NOPE + MLA_V)), MLA_KV_LORA ** -0.5),
        "hy_conv_w": nrm((DEPTH, HY_SHORT, 3 * HY_WIDTH), HY_SHORT ** -0.5),
        "hy_conv_b": nrm((DEPTH, 3 * HY_WIDTH), 0.02),
        "hy_w1": nrm((DEPTH, HY_EMB, FH), HY_EMB ** -0.5),
        "hy_b1": nrm((DEPTH, FH), 0.02),
        "hy_w2": nrm((DEPTH, FH, FH), FH ** -0.5),
        "hy_b2": nrm((DEPTH, FH), 0.02),
        "hy_w3": nrm((DEPTH, FH, FH), FH ** -0.5),
        "hy_b3": nrm((DEPTH, FH), 0.02),
        "hy_w_filt": nrm((DEPTH, FH, 2 * HY_WIDTH), HY_FILTER_OUT_SCALE),
        "hy_freq": gain((DEPTH, FH)),
        "hy_skip": nrm((DEPTH, HY_WIDTH), 1.0),
        "swa_sink": nrm((DEPTH, SWA_HEADS), 1.0),
        "w_out": nrm((DEPTH, MIX_WIDTH, D), MIX_WIDTH ** -0.5),
        "ffn_w_gate": nrm((n_dense, D, D_FF), D ** -0.5),
        "ffn_w_up": nrm((n_dense, D, D_FF), D ** -0.5),
        "ffn_w_down": nrm((n_dense, D_FF, D), D_FF ** -0.5),
        "moe_router": nrm((n_moe, D, N_EXPERTS), D ** -0.5),
        "moe_w_gate": nrm((n_moe, N_EXPERTS, D, D_FF_EXPERT), D ** -0.5),
        "moe_w_up": nrm((n_moe, N_EXPERTS, D, D_FF_EXPERT), D ** -0.5),
        "moe_w_down": nrm((n_moe, N_EXPERTS, D_FF_EXPERT, D), D_FF_EXPERT ** -0.5),
        "g_final": gain((D,)),
    }


def reference(x, c, ctx, c_ctx, w_mod, b_mod, g_mix, g_ffn, w_in, g_q, w_q_up, g_kv, w_kv_up,
              hy_conv_w, hy_conv_b, hy_w1, hy_b1, hy_w2, hy_b2, hy_w3, hy_b3, hy_w_filt, hy_freq, hy_skip,
              swa_sink, w_out, ffn_w_gate, ffn_w_up, ffn_w_down, moe_router, moe_w_gate, moe_w_up, moe_w_down,
              g_final):
    x_lat, x_ctx = x, ctx
    for l in range(DEPTH):
        last = l == DEPTH - 1
        sh1, sc1, g1, sh2, sc2, g2 = [m[:, None, :] for m in jnp.split(jax.nn.silu(c) @ w_mod[l] + b_mod[l], 6, axis=-1)]
        sh1c, sc1c, g1c, sh2c, sc2c, g2c = jnp.split(jax.nn.silu(c_ctx) @ w_mod[l] + b_mod[l], 6, axis=-1)
        lp = dict(w_in=w_in[l], g_q=g_q[l], w_q_up=w_q_up[l], g_kv=g_kv[l], w_kv_up=w_kv_up[l],
                  hy_conv_w=hy_conv_w[l], hy_conv_b=hy_conv_b[l], hy_w1=hy_w1[l], hy_b1=hy_b1[l],
                  hy_w2=hy_w2[l], hy_b2=hy_b2[l], hy_w3=hy_w3[l], hy_b3=hy_b3[l], hy_w_filt=hy_w_filt[l],
                  hy_freq=hy_freq[l], hy_skip=hy_skip[l], swa_sink=swa_sink[l], w_out=w_out[l])
        h_lat = modulate(rms_norm(x_lat, g_mix[l]), sh1, sc1)
        h_ctx = modulate(rms_norm(x_ctx, g_mix[l]), sh1c, sc1c)
        y_lat, y_ctx = mixer(h_lat, h_ctx, lp, not last)
        x_lat = x_lat + g1 * y_lat
        if not last:
            x_ctx = x_ctx + g1c * y_ctx

        if l % 2 == 0:
            i = l // 2
            f_lat = swiglu(modulate(rms_norm(x_lat, g_ffn[l]), sh2, sc2), ffn_w_gate[i], ffn_w_up[i], ffn_w_down[i])
            if not last:
                f_ctx = swiglu(modulate(rms_norm(x_ctx, g_ffn[l]), sh2c, sc2c), ffn_w_gate[i], ffn_w_up[i], ffn_w_down[i])
        else:
            i = l // 2
            f_lat = moe_swiglu(modulate(rms_norm(x_lat, g_ffn[l]), sh2, sc2), moe_router[i], moe_w_gate[i], moe_w_up[i], moe_w_down[i])
            if not last:
                f_ctx = moe_swiglu(modulate(rms_norm(x_ctx, g_ffn[l]), sh2c, sc2c), moe_router[i], moe_w_gate[i], moe_w_up[i], moe_w_down[i])
        x_lat = x_lat + g2 * f_lat
        if not last:
            x_ctx = x_ctx + g2c * f_ctx
    return rms_norm(x_lat, g_final)
```

```python
import functools
import math

import numpy as np
import jax
import jax.numpy as jnp
from jax import lax
from jax.experimental import pallas as pl
from jax.experimental.pallas import tpu as pltpu

F32 = jnp.float32
BF16 = jnp.bfloat16

D_MODEL = 2048
BATCH = 4
SEQ = 2048
DEPTH = 2
CTX_LEN = 256
GRID_W = 64
NORM_EPS = 1e-6
ROPE_THETA = 10000.0
MLA_HEADS = 8
MLA_NOPE = 128
MLA_ROPE = 64
MLA_V = 128
MLA_Q_LORA = 768
MLA_KV_LORA = 512
HY_WIDTH = 512
HY_BANDS = 16
HY_EMB = 1 + 2 * HY_BANDS
HY_FILTER_HIDDEN = 64
HY_DECAY_TARGET = 1e-2
HY_QUICK_DECAY_PCT = 0.3
HY_SLOW_DECAY_PCT = 1.5
SWA_HEADS = 8
SWA_KV_HEADS = 2
SWA_HEAD_DIM = 64
SWA_WINDOW = 128
N_EXPERTS = 8
D_FF = 5632
D_FF_EXPERT = 7168

LANES = 128
MXU_DIM = 256
VMEM_LIMIT_BYTES = 56 * 1024 * 1024
NEG_BIG = -1e30
MOD_ROWS = 8

_O_CQ = 0
_O_CKV = _O_CQ + MLA_Q_LORA
_O_KPE = _O_CKV + MLA_KV_LORA
_O_HY = _O_KPE + MLA_ROPE
_O_SQ = _O_HY + 3 * HY_WIDTH
_O_SK = _O_SQ + SWA_HEADS * SWA_HEAD_DIM
_O_SV = _O_SK + SWA_KV_HEADS * SWA_HEAD_DIM


def _cparams(*sem):
    return pltpu.CompilerParams(dimension_semantics=sem, vmem_limit_bytes=VMEM_LIMIT_BYTES)


def _dot(a, b):
    return jnp.dot(a, b, preferred_element_type=F32)


def _dot_nt(a, b):
    return lax.dot_general(a, b, (((1,), (1,)), ((), ())), preferred_element_type=F32)


def _split_bf16(a):
    hi = a.astype(BF16)
    lo = (a - hi.astype(F32)).astype(BF16)
    return hi, lo


def _dot3(a, b):
    ah, al = _split_bf16(a)
    bh, bl = _split_bf16(b)
    return _dot(ah, bh) + (_dot(al, bh) + _dot(ah, bl))


def _silu(x):
    return x / (1.0 + jnp.exp(-x))


def _rms(x, g):
    ms = jnp.mean(x * x, axis=-1, keepdims=True)
    return x * lax.rsqrt(ms + NORM_EPS) * g


def _const_spec(shape):
    nd = len(shape)
    return pl.BlockSpec(shape, lambda *_: (0,) * nd, pipeline_mode=pl.Buffered(1))


def _mod_kernel(c_ref, w_ref, b_ref, o_ref):
    a = _silu(c_ref[...])
    o_ref[0] = _dot3(a, w_ref[0]) + b_ref[0]


def _modulation(cvec, w_mod, b_mod):
    tn = 1024
    n = 6 * D_MODEL
    return pl.pallas_call(
        _mod_kernel,
        out_shape=jax.ShapeDtypeStruct((DEPTH, MOD_ROWS, n), F32),
        grid=(DEPTH, n // tn),
        in_specs=[
            pl.BlockSpec((MOD_ROWS, D_MODEL), lambda l, j: (0, 0)),
            pl.BlockSpec((1, D_MODEL, tn), lambda l, j: (l, 0, j)),
            pl.BlockSpec((1, 1, tn), lambda l, j: (l, 0, j)),
        ],
        out_specs=pl.BlockSpec((1, MOD_ROWS, tn), lambda l, j: (l, 0, j)),
        compiler_params=_cparams("arbitrary", "arbitrary"),
        name="adaln_mod",
    )(cvec, w_mod, b_mod.reshape(DEPTH, 1, n))


def _norm_kernel(x_ref, g_ref, m_ref, o_ref, *, si):
    y = _rms(x_ref[...], g_ref[...])
    if si is not None:
        y = y * (1.0 + m_ref[si + 1:si + 2, :]) + m_ref[si:si + 1, :]
    o_ref[...] = y.astype(o_ref.dtype)


def _norm_mod(x, g, mods, groups, row0, si, out_dtype, tm=512):
    m = x.shape[0]
    nr = m // groups // tm
    return pl.pallas_call(
        functools.partial(_norm_kernel, si=si),
        out_shape=jax.ShapeDtypeStruct((m, D_MODEL), out_dtype),
        grid=(groups, nr),
        in_specs=[
            pl.BlockSpec((tm, D_MODEL), lambda g_, i: (g_ * nr + i, 0)),
            pl.BlockSpec((1, D_MODEL), lambda g_, i: (0, 0)),
            pl.BlockSpec((None, 6, D_MODEL), lambda g_, i: (row0 + g_, 0, 0)),
        ],
        out_specs=pl.BlockSpec((tm, D_MODEL), lambda g_, i: (g_ * nr + i, 0)),
        compiler_params=_cparams("parallel", "parallel"),
        name="norm_mod",
    )(x, g.reshape(1, D_MODEL), mods)


def _mm_kernel(x_ref, w_ref, o_ref):
    o_ref[...] = _dot(x_ref[...], w_ref[...]).astype(o_ref.dtype)


def _matmul(x, w, out_dtype, tm=512):
    m, k = x.shape
    n = w.shape[1]
    return pl.pallas_call(
        _mm_kernel,
        out_shape=jax.ShapeDtypeStruct((m, n), out_dtype),
        grid=(m // tm,),
        in_specs=[pl.BlockSpec((tm, k), lambda i: (i, 0)), _const_spec((k, n))],
        out_specs=pl.BlockSpec((tm, n), lambda i: (i, 0)),
        compiler_params=_cparams("parallel"),
        name="matmul",
    )(x, w)


def _rope_halves(pe_pair, tab):
    r = pe_pair * tab
    return r + pltpu.roll(r, MLA_ROPE, axis=1)


def _qproj_kernel(h_ref, wc_ref, g_ref, wq_ref, tab_ref, o_ref):
    cq = _dot(h_ref[...], wc_ref[...])
    cqn = _rms(cq, g_ref[...]).astype(BF16)
    tab = tab_ref[...]
    for hh in range(MLA_HEADS):
        r = _dot(cqn, wq_ref[hh])
        o_ref[hh, :, 0:MLA_NOPE] = r[:, :MLA_NOPE].astype(BF16)
        o_ref[hh, :, MLA_NOPE:] = _rope_halves(r[:, MLA_NOPE:], tab).astype(BF16)


def _q_proj(h, w_cq, g_q, w_q, tab, tm=512):
    m = h.shape[0]
    nt = tab.shape[0] // tm
    return pl.pallas_call(
        _qproj_kernel,
        out_shape=jax.ShapeDtypeStruct((MLA_HEADS, m, MXU_DIM), BF16),
        grid=(m // tm,),
        in_specs=[
            pl.BlockSpec((tm, D_MODEL), lambda i: (i, 0)),
            _const_spec(w_cq.shape),
            _const_spec((1, MLA_Q_LORA)),
            _const_spec(w_q.shape),
            pl.BlockSpec((tm, LANES), lambda i: (i % nt, 0)),
        ],
        out_specs=pl.BlockSpec((MLA_HEADS, tm, MXU_DIM), lambda i: (0, i, 0)),
        compiler_params=_cparams("parallel"),
        name="mla_q_proj",
    )(h, w_cq, g_q.reshape(1, -1), w_q, tab)


def _kvproj_kernel(h_ref, wc_ref, g_ref, wkv_ref, tab_ref, k_ref, v_ref):
    t = _dot(h_ref[...], wc_ref[...])
    ckvn = _rms(t[:, :MLA_KV_LORA], g_ref[...]).astype(BF16)
    rot = _rope_halves(t[:, MLA_KV_LORA:], tab_ref[...])
    lane = lax.broadcasted_iota(jnp.int32, rot.shape, 1)
    kpe = jnp.where(lane < MLA_ROPE, rot, 0.0).astype(BF16)
    for hh in range(MLA_HEADS):
        r = _dot(ckvn, wkv_ref[hh])
        k_ref[hh, :, 0:MLA_NOPE] = r[:, :MLA_NOPE].astype(BF16)
        k_ref[hh, :, MLA_NOPE:] = kpe
        v_ref[hh] = r[:, MLA_NOPE:].astype(BF16)


def _kv_proj(h, w_ckv, g_kv, w_kv, tab, tm=512):
    m = h.shape[0]
    nt = tab.shape[0] // tm
    return pl.pallas_call(
        _kvproj_kernel,
        out_shape=(jax.ShapeDtypeStruct((MLA_HEADS, m, MXU_DIM), BF16),
                   jax.ShapeDtypeStruct((MLA_HEADS, m, MLA_V), BF16)),
        grid=(m // tm,),
        in_specs=[
            pl.BlockSpec((tm, D_MODEL), lambda i: (i, 0)),
            _const_spec(w_ckv.shape),
            _const_spec((1, MLA_KV_LORA)),
            _const_spec(w_kv.shape),
            pl.BlockSpec((tm, LANES), lambda i: (i % nt, 0)),
        ],
        out_specs=(pl.BlockSpec((MLA_HEADS, tm, MXU_DIM), lambda i: (0, i, 0)),
                   pl.BlockSpec((MLA_HEADS, tm, MLA_V), lambda i: (0, i, 0))),
        compiler_params=_cparams("parallel"),
        name="mla_kv_proj",
    )(h, w_ckv, g_kv.reshape(1, -1), w_kv, tab)


def _mla_kernel(q_ref, *refs, nseg, scale):
    o_ref = refs[2 * nseg]
    q = q_ref[...]
    ss = [_dot_nt(q, refs[2 * s][...]) * scale for s in range(nseg)]
    m = ss[0].max(axis=-1, keepdims=True)
    for s in ss[1:]:
        m = jnp.maximum(m, s.max(axis=-1, keepdims=True))
    acc = None
    den = None
    for s in range(nseg):
        p = jnp.exp(ss[s] - m)
        d = p.sum(axis=-1, keepdims=True)
        o = _dot(p.astype(BF16), refs[2 * s + 1][...])
        acc = o if acc is None else acc + o
        den = d if den is None else den + d
    o_ref[...] = (acc / den).astype(o_ref.dtype)


def _mla_attention(q, segs, lq, tq=512):
    nq = lq // tq
    in_specs = [pl.BlockSpec((None, tq, MXU_DIM), lambda b, h, i: (h, b * nq + i, 0))]
    args = [q]
    for k, v, lk in segs:
        in_specs.append(pl.BlockSpec((None, lk, MXU_DIM), lambda b, h, i: (h, b, 0)))
        in_specs.append(pl.BlockSpec((None, lk, MLA_V), lambda b, h, i: (h, b, 0)))
        args += [k, v]
    scale = (MLA_NOPE + MLA_ROPE) ** -0.5
    return pl.pallas_call(
        functools.partial(_mla_kernel, nseg=len(segs), scale=scale),
        out_shape=jax.ShapeDtypeStruct((BATCH * lq, MLA_HEADS * MLA_V), BF16),
        grid=(BATCH, MLA_HEADS, nq),
        in_specs=in_specs,
        out_specs=pl.BlockSpec((tq, MLA_V), lambda b, h, i: (b * nq + i, h)),
        compiler_params=_cparams("parallel", "parallel", "arbitrary"),
        name="mla_attention",
    )(*args)


def _dft_tables(n_len):
    n2 = 2 * n_len
    idx = np.arange(n_len, dtype=np.int64)
    ang = (2.0 * np.pi / n2) * ((idx[:, None] * idx[None, :]) % n2).astype(np.float64)
    cm = np.cos(ang)
    sf = np.sin(ang)
    sf[0, :] = 1.0 - 2.0 * (idx % 2)
    return (jnp.asarray(cm, dtype=BF16), jnp.asarray(sf, dtype=BF16),
            jnp.asarray(sf.T.copy(), dtype=BF16))


def _filter_tables(n_len):
    pos = np.arange(n_len, dtype=np.float64)
    t = pos / max(n_len - 1, 1)
    bands = np.linspace(1e-4, HY_BANDS - 1, HY_BANDS)
    ang = (2.0 * math.pi / n_len) * pos[:, None] * bands[None]
    z = np.zeros((n_len, LANES), np.float64)
    z[:, 0] = t
    z[:, 1:1 + HY_BANDS] = np.cos(ang)
    z[:, 1 + HY_BANDS:HY_EMB] = -np.sin(ang)
    deltas = np.linspace(math.log(HY_DECAY_TARGET) / HY_SLOW_DECAY_PCT,
                         math.log(HY_DECAY_TARGET) / HY_QUICK_DECAY_PCT, HY_WIDTH)
    decay = np.exp(-t[:, None] * np.abs(deltas)[None])
    return jnp.asarray(z, dtype=F32), jnp.asarray(decay, dtype=F32)


def _hy_filter_kernel(z_ref, w1_ref, b1_ref, w2_ref, b2_ref, w3_ref, b3_ref, wf_ref, wb_ref,
                      fr_ref, dec_ref, cm_ref, sf_ref, a_ref, b_ref, d_ref, *, n_len):
    fr = fr_ref[...]
    h = jnp.sin(fr * (_dot3(z_ref[...], w1_ref[...]) + b1_ref[...]))
    h = jnp.sin(fr * (_dot3(h, w2_ref[...]) + b2_ref[...]))
    h = jnp.sin(fr * (_dot3(h, w3_ref[...]) + b3_ref[...]))
    dec = dec_ref[...]
    h_f = _dot3(h, wf_ref[...]) * dec
    h_b = _dot3(h, wb_ref[...]) * dec
    row = lax.broadcasted_iota(jnp.int32, h_f.shape, 0)
    h_b = jnp.where(row == 0, 0.0, h_b)
    hs = h_f + h_b
    hd = h_f - h_b
    hs_h, hs_l = _split_bf16(hs)
    hd_h, hd_l = _split_bf16(hd)
    cm = cm_ref[...]
    sf = sf_ref[...]
    t_re = _dot(cm, hs_h) + _dot(cm, hs_l)
    g_t = _dot(sf, hd_h) + _dot(sf, hd_l)
    t_ny = (_dot(sf[0:16, :], hs_h) + _dot(sf[0:16, :], hs_l))[0:1, :]
    inv_n = 1.0 / (2 * n_len)
    first = row == 0
    a_ref[...] = jnp.where(first, inv_n, 2.0 * inv_n) * t_re
    b_ref[...] = jnp.where(first, 0.0, -2.0 * inv_n * g_t)
    d_ref[...] = jnp.where(first, inv_n * t_ny, 2.0 * inv_n * t_re)


def _hy_filters(lp, n_len, tabs):
    z, decay = tabs["filt"]
    cm, sf, _ = tabs["dft"]
    nblk = HY_WIDTH // LANES
    w1 = jnp.zeros((LANES, HY_FILTER_HIDDEN), F32).at[:HY_EMB].set(lp["hy_w1"])
    row = lambda a: a.reshape(1, -1)
    small = lambda shape: pl.BlockSpec(shape, lambda j: (0,) * len(shape))
    fh = HY_FILTER_HIDDEN
    out = jax.ShapeDtypeStruct((n_len, HY_WIDTH), F32)
    return pl.pallas_call(
        functools.partial(_hy_filter_kernel, n_len=n_len),
        out_shape=(out, out, out),
        grid=(nblk,),
        in_specs=[
            small((n_len, LANES)), small((LANES, fh)), small((1, fh)), small((fh, fh)), small((1, fh)),
            small((fh, fh)), small((1, fh)),
            pl.BlockSpec((fh, LANES), lambda j: (0, j)),
            pl.BlockSpec((fh, LANES), lambda j: (0, nblk + j)),
            small((1, fh)),
            pl.BlockSpec((n_len, LANES), lambda j: (0, j)),
            _const_spec((n_len, n_len)), _const_spec((n_len, n_len)),
        ],
        out_specs=tuple(pl.BlockSpec((n_len, LANES), lambda j: (0, j)) for _ in range(3)),
        compiler_params=_cparams("arbitrary"),
        name="hyena_filters",
    )(z, w1, row(lp["hy_b1"]), lp["hy_w2"], row(lp["hy_b2"]), lp["hy_w3"], row(lp["hy_b3"]),
      lp["hy_w_filt"], lp["hy_w_filt"], row(lp["hy_freq"]), decay, cm, sf)


def _hy_pre_kernel(u0_ref, u1_ref, u2_ref, w0_ref, w1_ref, w2_ref, b0_ref, b1_ref, b2_ref,
                   x0_ref, vx_ref, vxb_ref, *, n_len):
    def sconv(u_ref, w_ref, b_ref):
        u = u_ref[...]
        w = w_ref[...]
        row = lax.broadcasted_iota(jnp.int32, u.shape, 0)
        prev = jnp.where(row == 0, 0.0, pltpu.roll(u, 1, axis=0))
        nxt = jnp.where(row == n_len - 1, 0.0, pltpu.roll(u, n_len - 1, axis=0))
        return w[0:1] * prev + w[1:2] * u + w[2:3] * nxt + b_ref[...]

    x0_ref[...] = sconv(u0_ref, w0_ref, b0_ref)
    vx = sconv(u2_ref, w2_ref, b2_ref) * sconv(u1_ref, w1_ref, b1_ref)
    vx_ref[...] = vx
    vxb_ref[...] = vx.astype(BF16)


def _hy_pre(hy, conv_w, conv_b, n_len):
    nblk = HY_WIDTH // LANES
    uspec = lambda part: pl.BlockSpec((None, n_len, LANES), lambda b, j: (b, 0, part * nblk + j))
    wspec = lambda part: pl.BlockSpec((3, LANES), lambda b, j: (0, part * nblk + j))
    bspec = lambda part: pl.BlockSpec((1, LANES), lambda b, j: (0, part * nblk + j))
    ospec = pl.BlockSpec((None, n_len, LANES), lambda b, j: (b, 0, j))
    hy3 = hy.reshape(BATCH, n_len, 3 * HY_WIDTH)
    cb = conv_b.reshape(1, -1)
    return pl.pallas_call(
        functools.partial(_hy_pre_kernel, n_len=n_len),
        out_shape=(jax.ShapeDtypeStruct((BATCH, n_len, HY_WIDTH), F32),
                   jax.ShapeDtypeStruct((BATCH, n_len, HY_WIDTH), F32),
                   jax.ShapeDtypeStruct((BATCH, n_len, HY_WIDTH), BF16)),
        grid=(BATCH, nblk),
        in_specs=[uspec(0), uspec(1), uspec(2), wspec(0), wspec(1), wspec(2), bspec(0), bspec(1), bspec(2)],
        out_specs=(ospec, ospec, ospec),
        compiler_params=_cparams("parallel", "parallel"),
        name="hyena_short_conv",
    )(hy3, hy3, hy3, conv_w, conv_w, conv_w, cb, cb, cb)


def _hy_fwd_kernel(cm_ref, sf_ref, u_ref, a_ref, b_ref, d_ref, q_ref):
    u = u_ref[...]
    ur = _dot(cm_ref[...], u)
    g = _dot(sf_ref[...], u)
    b = b_ref[...]
    q_ref[0] = (a_ref[...] * ur + b * g).astype(BF16)
    q_ref[1] = (d_ref[...] * g - b * ur).astype(BF16)


def _hy_inv_kernel(cm_ref, si_ref, q_ref, x0_ref, vx_ref, skip_ref, o_ref):
    y = _dot(cm_ref[...], q_ref[0]) + _dot(si_ref[...], q_ref[1])
    o_ref[...] = (x0_ref[...] * (y + vx_ref[...] * skip_ref[...])).astype(o_ref.dtype)


def _hy_long_conv(x0, vx, vxb, spectra, skip, n_len, tabs):
    cm, sf, si = tabs["dft"]
    a, b, d = spectra
    tf = min(512, n_len)
    nf = n_len // tf
    mat = pl.BlockSpec((tf, n_len), lambda bb, f: (f, 0))
    spec = pl.BlockSpec((tf, HY_WIDTH), lambda bb, f: (f, 0))
    q = pl.pallas_call(
        _hy_fwd_kernel,
        out_shape=jax.ShapeDtypeStruct((BATCH, 2, n_len, HY_WIDTH), BF16),
        grid=(BATCH, nf),
        in_specs=[mat, mat, pl.BlockSpec((None, n_len, HY_WIDTH), lambda bb, f: (bb, 0, 0)), spec, spec, spec],
        out_specs=pl.BlockSpec((None, 2, tf, HY_WIDTH), lambda bb, f: (bb, 0, f, 0)),
        compiler_params=_cparams("parallel", "arbitrary"),
        name="hyena_dft_fwd",
    )(cm, sf, vxb, a, b, d)
    tile = pl.BlockSpec((None, tf, HY_WIDTH), lambda bb, f: (bb, f, 0))
    return pl.pallas_call(
        _hy_inv_kernel,
        out_shape=jax.ShapeDtypeStruct((BATCH, n_len, HY_WIDTH), BF16),
        grid=(BATCH, nf),
        in_specs=[mat, mat, pl.BlockSpec((None, 2, n_len, HY_WIDTH), lambda bb, f: (bb, 0, 0, 0)),
                  tile, tile, pl.BlockSpec((1, HY_WIDTH), lambda bb, f: (0, 0))],
        out_specs=tile,
        compiler_params=_cparams("parallel", "arbitrary"),
        name="hyena_dft_inv",
    )(cm, si, q, x0, vx, skip.reshape(1, -1))


_SWA_Q = SWA_HEADS * SWA_HEAD_DIM
_SWA_KV = SWA_KV_HEADS * SWA_HEAD_DIM


def _swaproj_kernel(h_ref, w_ref, cos_ref, sin_ref, q_ref, k_ref, v_ref, *, rope):
    t = _dot(h_ref[...], w_ref[...])
    q = t[:, :_SWA_Q]
    k = t[:, 2 * _SWA_Q:2 * _SWA_Q + _SWA_KV]
    v = t[:, 2 * _SWA_Q + 2 * _SWA_KV:]
    if rope:
        cos = cos_ref[...]
        sin = sin_ref[...]
        q = q * cos + t[:, _SWA_Q:2 * _SWA_Q] * sin
        k = k * cos[:, :_SWA_KV] + t[:, 2 * _SWA_Q + _SWA_KV:2 * _SWA_Q + 2 * _SWA_KV] * sin[:, :_SWA_KV]
    q_ref[...] = (q * (SWA_HEAD_DIM ** -0.5)).astype(BF16)
    lo = lax.broadcasted_iota(jnp.int32, k.shape, 1) < SWA_HEAD_DIM
    for src, dst in ((k, k_ref), (v, v_ref)):
        sw = pltpu.roll(src, SWA_HEAD_DIM, axis=1)
        dst[:, 0 * LANES:1 * LANES] = jnp.where(lo, src, 0.0).astype(BF16)
        dst[:, 1 * LANES:2 * LANES] = jnp.where(lo, 0.0, sw).astype(BF16)
        dst[:, 2 * LANES:3 * LANES] = jnp.where(lo, sw, 0.0).astype(BF16)
        dst[:, 3 * LANES:4 * LANES] = jnp.where(lo, 0.0, src).astype(BF16)


def _swa_proj(h, w, cos_t, sin_t, rope, tm=512):
    m = h.shape[0]
    nt = cos_t.shape[0] // tm
    o = jax.ShapeDtypeStruct((m, 4 * LANES), BF16)
    ospec = pl.BlockSpec((tm, 4 * LANES), lambda i: (i, 0))
    tspec = pl.BlockSpec((tm, _SWA_Q), lambda i: (i % nt, 0))
    return pl.pallas_call(
        functools.partial(_swaproj_kernel, rope=rope),
        out_shape=(o, o, o),
        grid=(m // tm,),
        in_specs=[pl.BlockSpec((tm, D_MODEL), lambda i: (i, 0)), _const_spec(w.shape), tspec, tspec],
        out_specs=(ospec, ospec, ospec),
        compiler_params=_cparams("parallel"),
        name="swa_proj",
    )(h, w, cos_t, sin_t)


def _swa_kernel(sink_ref, q_ref, *refs, tq, win, lk, has_lat):
    if has_lat:
        kl_ref, vl_ref, kc_ref, vc_ref, o_ref = refs
        t0 = pl.program_id(1) * tq
        start = pl.multiple_of(jnp.clip(t0 - SWA_WINDOW, 0, lk - win), SWA_WINDOW)
        qpos = t0 + lax.broadcasted_iota(jnp.int32, (tq, win), 0)
        kpos = start + lax.broadcasted_iota(jnp.int32, (tq, win), 1)
        valid = jnp.abs(qpos - kpos) <= SWA_WINDOW
    else:
        kc_ref, vc_ref, o_ref = refs
    for j in range(SWA_HEADS // 2):
        g = j // (SWA_HEADS // SWA_KV_HEADS // 2)
        qb = q_ref[:, j * LANES:(j + 1) * LANES]
        acc = None
        for par in range(2):
            c0 = (2 * g + par) * LANES
            sk = sink_ref[2 * j + par]
            s_c = _dot_nt(qb, kc_ref[:, c0:c0 + LANES])
            m = jnp.maximum(s_c.max(axis=-1, keepdims=True), sk)
            if has_lat:
                s_l = _dot_nt(qb, kl_ref[pl.ds(start, win), c0:c0 + LANES])
                s_l = jnp.where(valid, s_l, NEG_BIG)
                m = jnp.maximum(m, s_l.max(axis=-1, keepdims=True))
            p_c = jnp.exp(s_c - m)
            den = p_c.sum(axis=-1, keepdims=True) + jnp.exp(sk - m)
            if has_lat:
                p_l = jnp.exp(s_l - m)
                den = den + p_l.sum(axis=-1, keepdims=True)
            inv = 1.0 / den
            o = _dot((p_c * inv).astype(BF16), vc_ref[:, c0:c0 + LANES])
            if has_lat:
                o = o + _dot((p_l * inv).astype(BF16), vl_ref[pl.ds(start, win), c0:c0 + LANES])
            acc = o if acc is None else acc + o
        o_ref[:, j * LANES:(j + 1) * LANES] = acc.astype(o_ref.dtype)


def _swa_attention(sink, q, lat, ctx, lq, tq=256):
    nq = lq // tq
    win = tq + 2 * SWA_WINDOW
    full = lambda n: pl.BlockSpec((n, 4 * LANES), lambda b, i: (b, 0))
    in_specs = [pl.BlockSpec(memory_space=pltpu.SMEM),
                pl.BlockSpec((tq, 4 * LANES), lambda b, i: (b * nq + i, 0))]
    args = [sink, q]
    if lat is not None:
        in_specs += [full(SEQ), full(SEQ)]
        args += list(lat)
    in_specs += [full(CTX_LEN), full(CTX_LEN)]
    args += list(ctx)
    return pl.pallas_call(
        functools.partial(_swa_kernel, tq=tq, win=win, lk=SEQ, has_lat=lat is not None),
        out_shape=jax.ShapeDtypeStruct((BATCH * lq, 4 * LANES), BF16),
        grid=(BATCH, nq),
        in_specs=in_specs,
        out_specs=pl.BlockSpec((tq, 4 * LANES), lambda b, i: (b * nq + i, 0)),
        compiler_params=_cparams("parallel", "arbitrary"),
        name="swa_attention",
    )(*args)


def _out_kernel(a_ref, y_ref, s_ref, w_ref, x_ref, m_ref, g_ref, xo_ref, ho_ref):
    na = MLA_HEADS * MLA_V
    acc = _dot(a_ref[...], w_ref[0:na, :])
    acc = acc + _dot(y_ref[...], w_ref[na:na + HY_WIDTH, :])
    acc = acc + _dot(s_ref[...], w_ref[na + HY_WIDTH:, :])
    x = x_ref[...] + m_ref[2:3, :] * acc
    xo_ref[...] = x
    ho_ref[...] = (_rms(x, g_ref[...]) * (1.0 + m_ref[4:5, :]) + m_ref[3:4, :]).astype(BF16)


def _out_proj(a, y, s, w_out, x, mods, g_ffn, groups, row0, tm=512):
    m = x.shape[0]
    nr = m // groups // tm
    tile = lambda n: pl.BlockSpec((tm, n), lambda g_, i: (g_ * nr + i, 0))
    return pl.pallas_call(
        _out_kernel,
        out_shape=(jax.ShapeDtypeStruct((m, D_MODEL), F32), jax.ShapeDtypeStruct((m, D_MODEL), BF16)),
        grid=(groups, nr),
        in_specs=[tile(a.shape[1]), tile(y.shape[1]), tile(s.shape[1]), _const_spec(w_out.shape),
                  tile(D_MODEL), pl.BlockSpec((None, 6, D_MODEL), lambda g_, i: (row0 + g_, 0, 0)),
                  pl.BlockSpec((1, D_MODEL), lambda g_, i: (0, 0))],
        out_specs=(tile(D_MODEL), tile(D_MODEL)),
        compiler_params=_cparams("parallel", "parallel"),
        name="out_proj",
    )(a, y, s, w_out, x, mods, g_ffn.reshape(1, -1))


def _router_kernel(x_ref, g_ref, m_ref, w_ref, o_ref):
    h = _rms(x_ref[...], g_ref[...]) * (1.0 + m_ref[4:5, :]) + m_ref[3:4, :]
    logits = _dot3(h, w_ref[...])
    lane = lax.broadcasted_iota(jnp.int32, logits.shape, 1).astype(F32)
    logits = jnp.where(lane < N_EXPERTS, logits, NEG_BIG)
    m1 = logits.max(axis=-1, keepdims=True)
    i1 = jnp.where(logits == m1, lane, float(LANES)).min(axis=-1, keepdims=True)
    rest = jnp.where(lane == i1, NEG_BIG, logits)
    m2 = rest.max(axis=-1, keepdims=True)
    i2 = jnp.where(rest == m2, lane, float(LANES)).min(axis=-1, keepdims=True)
    e2 = jnp.exp(m2 - m1)
    w1 = 1.0 / (1.0 + e2)
    o_ref[...] = jnp.where(lane == i1, w1, 0.0) + jnp.where(lane == i2, e2 * w1, 0.0)


def _router(x, g_ffn, mods, w_router, groups, row0, tm=512):
    m = x.shape[0]
    nr = m // groups // tm
    wp = jnp.zeros((D_MODEL, LANES), F32).at[:, :N_EXPERTS].set(w_router)
    return pl.pallas_call(
        _router_kernel,
        out_shape=jax.ShapeDtypeStruct((m, LANES), F32),
        grid=(groups, nr),
        in_specs=[pl.BlockSpec((tm, D_MODEL), lambda g_, i: (g_ * nr + i, 0)),
                  pl.BlockSpec((1, D_MODEL), lambda g_, i: (0, 0)),
                  pl.BlockSpec((None, 6, D_MODEL), lambda g_, i: (row0 + g_, 0, 0)),
                  pl.BlockSpec((D_MODEL, LANES), lambda g_, i: (0, 0))],
        out_specs=pl.BlockSpec((tm, LANES), lambda g_, i: (g_ * nr + i, 0)),
        compiler_params=_cparams("parallel", "parallel"),
        name="moe_router",
    )(x, g_ffn.reshape(1, -1), mods, wp)


def _ffn_kernel(h_ref, wg_ref, wu_ref, wd_ref, gate_ref, x_ref, m_ref, o_ref, acc_ref, *, gated):
    e = pl.program_id(2)
    f = pl.program_id(3)

    @pl.when((e == 0) & (f == 0))
    def _():
        acc_ref[...] = jnp.zeros_like(acc_ref)

    h = h_ref[...]
    gg = _dot(h, wg_ref[...].astype(BF16))
    uu = _dot(h, wu_ref[...].astype(BF16))
    mid = _silu(gg) * uu
    if gated:
        gates = gate_ref[...]
        lane = lax.broadcasted_iota(jnp.int32, gates.shape, 1)
        mid = mid * jnp.where(lane == e, gates, 0.0).sum(axis=-1, keepdims=True)
    acc_ref[...] += _dot(mid.astype(BF16), wd_ref[...].astype(BF16))

    @pl.when((e == pl.num_programs(2) - 1) & (f == pl.num_programs(3) - 1))
    def _():
        o_ref[...] = x_ref[...] + m_ref[5:6, :] * acc_ref[...]


def _ffn(h, w_gate, w_up, w_down, gates, x, mods, groups, row0, tm=512, tf=512):
    m = x.shape[0]
    n_e, _, n_ff = w_gate.shape
    nr = m // groups // tm
    gated = gates is not None
    if gates is None:
        gates = jnp.zeros((m, LANES), F32)
    tile = lambda n: pl.BlockSpec((tm, n), lambda g_, i, e, f: (g_ * nr + i, 0))
    return pl.pallas_call(
        functools.partial(_ffn_kernel, gated=gated),
        out_shape=jax.ShapeDtypeStruct((m, D_MODEL), F32),
        grid=(groups, nr, n_e, n_ff // tf),
        in_specs=[tile(D_MODEL),
                  pl.BlockSpec((None, D_MODEL, tf), lambda g_, i, e, f: (e, 0, f)),
                  pl.BlockSpec((None, D_MODEL, tf), lambda g_, i, e, f: (e, 0, f)),
                  pl.BlockSpec((None, tf, D_MODEL), lambda g_, i, e, f: (e, f, 0)),
                  tile(LANES), tile(D_MODEL),
                  pl.BlockSpec((None, 6, D_MODEL), lambda g_, i, e, f: (row0 + g_, 0, 0))],
        out_specs=tile(D_MODEL),
        scratch_shapes=[pltpu.VMEM((tm, D_MODEL), F32)],
        compiler_params=_cparams("parallel", "parallel", "arbitrary", "arbitrary"),
        name="swiglu_ffn",
    )(h, w_gate, w_up, w_down, gates, x, mods)


def _swap_halves(w):
    half = w.shape[-1] // 2
    return jnp.concatenate([w[..., half:], w[..., :half]], axis=-1)


def _swap_heads(w, heads, dim):
    k = w.shape[0]
    return _swap_halves(w.reshape(k, heads, dim)).reshape(k, heads * dim)


def _rope_tables():
    rows = SEQ // GRID_W
    row = np.repeat(np.arange(rows), GRID_W).astype(np.float32)
    col = np.tile(np.arange(GRID_W), rows).astype(np.float32)
    quarter = MLA_ROPE // 4
    freqs = (np.float32(ROPE_THETA) ** (-np.arange(quarter, dtype=np.float32) / quarter)).astype(np.float32)
    ang = np.concatenate([row[:, None] * freqs[None], col[:, None] * freqs[None]], axis=-1)
    cos = np.cos(ang.astype(np.float64))
    sin = np.sin(ang.astype(np.float64))
    cos2 = np.concatenate([cos, cos], axis=-1)
    sin2 = np.concatenate([-sin, sin], axis=-1)
    tab_lat = np.concatenate([cos2, sin2], axis=-1)
    tab_ctx = np.concatenate([np.ones((CTX_LEN, MLA_ROPE)), np.zeros((CTX_LEN, MLA_ROPE))], axis=-1)
    return dict(
        tab_lat=jnp.asarray(tab_lat, F32), tab_ctx=jnp.asarray(tab_ctx, F32),
        cos8=jnp.asarray(np.tile(cos2, (1, SWA_HEADS)), F32),
        sin8=jnp.asarray(np.tile(sin2, (1, SWA_HEADS)), F32),
        ones8=jnp.ones((CTX_LEN, _SWA_Q), F32), zeros8=jnp.zeros((CTX_LEN, _SWA_Q), F32),
    )


def _layer_weights(p, l):
    w_in = p["w_in"][l]
    cq = w_in[:, _O_CQ:_O_CKV]
    ckv = w_in[:, _O_CKV:_O_KPE]
    kpe = w_in[:, _O_KPE:_O_HY]
    hy = w_in[:, _O_HY:_O_SQ]
    sq = w_in[:, _O_SQ:_O_SK]
    sk = w_in[:, _O_SK:_O_SV]
    sv = w_in[:, _O_SV:]
    dq = MLA_NOPE + MLA_ROPE
    wq = p["w_q_up"][l].reshape(MLA_Q_LORA, MLA_HEADS, dq)
    wq = jnp.concatenate([wq, _swap_halves(wq[..., MLA_NOPE:])], axis=-1)
    wkv = p["w_kv_up"][l].reshape(MLA_KV_LORA, MLA_HEADS, MLA_NOPE + MLA_V)
    return dict(
        w_cq=cq.astype(BF16),
        w_ckv=jnp.concatenate([ckv, kpe, _swap_halves(kpe)], axis=-1).astype(BF16),
        w_hy=hy.astype(BF16),
        w_swa=jnp.concatenate([sq, _swap_heads(sq, SWA_HEADS, SWA_HEAD_DIM), sk,
                               _swap_heads(sk, SWA_KV_HEADS, SWA_HEAD_DIM), sv], axis=-1).astype(BF16),
        w_q=jnp.transpose(wq, (1, 0, 2)).astype(BF16),
        w_kv=jnp.transpose(wkv, (1, 0, 2)).astype(BF16),
        w_out=p["w_out"][l].astype(BF16),
        g_q=p["g_q"][l], g_kv=p["g_kv"][l],
        hy_conv_w=p["hy_conv_w"][l], hy_conv_b=p["hy_conv_b"][l],
        hy_w1=p["hy_w1"][l], hy_b1=p["hy_b1"][l], hy_w2=p["hy_w2"][l], hy_b2=p["hy_b2"][l],
        hy_w3=p["hy_w3"][l], hy_b3=p["hy_b3"][l], hy_w_filt=p["hy_w_filt"][l],
        hy_freq=p["hy_freq"][l], hy_skip=p["hy_skip"][l], swa_sink=p["swa_sink"][l],
    )


def _hyena(h, lw, n_len, tabs):
    hy = _matmul(h, lw["w_hy"], F32)
    x0, vx, vxb = _hy_pre(hy, lw["hy_conv_w"], lw["hy_conv_b"], n_len)
    spectra = _hy_filters(lw, n_len, tabs)
    y = _hy_long_conv(x0, vx, vxb, spectra, lw["hy_skip"], n_len, tabs)
    return y.reshape(BATCH * n_len, HY_WIDTH)


def _mixer(h_lat, h_ctx, lw, rope, hy_tabs, need_ctx):
    k_l, v_l = _kv_proj(h_lat, lw["w_ckv"], lw["g_kv"], lw["w_kv"], rope["tab_lat"])
    k_c, v_c = _kv_proj(h_ctx, lw["w_ckv"], lw["g_kv"], lw["w_kv"], rope["tab_ctx"], tm=CTX_LEN)
    q_l = _q_proj(h_lat, lw["w_cq"], lw["g_q"], lw["w_q"], rope["tab_lat"])
    a_l = _mla_attention(q_l, [(k_l, v_l, SEQ), (k_c, v_c, CTX_LEN)], SEQ)
    y_l = _hyena(h_lat, lw, SEQ, hy_tabs[SEQ])
    sq_l, kk_l, vv_l = _swa_proj(h_lat, lw["w_swa"], rope["cos8"], rope["sin8"], True)
    sq_c, kk_c, vv_c = _swa_proj(h_ctx, lw["w_swa"], rope["ones8"], rope["zeros8"], False, tm=CTX_LEN)
    s_l = _swa_attention(lw["swa_sink"], sq_l, (kk_l, vv_l), (kk_c, vv_c), SEQ)
    if not need_ctx:
        return (a_l, y_l, s_l), None
    q_c = _q_proj(h_ctx, lw["w_cq"], lw["g_q"], lw["w_q"], rope["tab_ctx"], tm=CTX_LEN)
    a_c = _mla_attention(q_c, [(k_c, v_c, CTX_LEN)], CTX_LEN, tq=CTX_LEN)
    y_c = _hyena(h_ctx, lw, CTX_LEN, hy_tabs[CTX_LEN])
    s_c = _swa_attention(lw["swa_sink"], sq_c, None, (kk_c, vv_c), CTX_LEN, tq=CTX_LEN)
    return (a_l, y_l, s_l), (a_c, y_c, s_c)


def _forward(p):
    rope = _rope_tables()
    hy_tabs = {n: dict(dft=_dft_tables(n), filt=_filter_tables(n)) for n in (SEQ, CTX_LEN)}
    cvec = jnp.zeros((MOD_ROWS, D_MODEL), F32).at[:BATCH].set(p["c"]).at[BATCH].set(p["c_ctx"])
    mods_all = _modulation(cvec, p["w_mod"], p["b_mod"]).reshape(DEPTH, MOD_ROWS, 6, D_MODEL)

    x_lat = p["x"].reshape(BATCH * SEQ, D_MODEL)
    x_ctx = p["ctx"].reshape(BATCH * CTX_LEN, D_MODEL)
    for l in range(DEPTH):
        last = l == DEPTH - 1
        mods = mods_all[l]
        lw = _layer_weights(p, l)
        h_lat = _norm_mod(x_lat, p["g_mix"][l], mods, BATCH, 0, 0, BF16)
        h_ctx = _norm_mod(x_ctx, p["g_mix"][l], mods, 1, BATCH, 0, BF16)
        mix_l, mix_c = _mixer(h_lat, h_ctx, lw, rope, hy_tabs, not last)
        streams = [(x_lat, mix_l, BATCH, 0)]
        if not last:
            streams.append((x_ctx, mix_c, 1, BATCH))
        outs = []
        for x, mix, groups, row0 in streams:
            x, h2 = _out_proj(*mix, lw["w_out"], x, mods, p["g_ffn"][l], groups, row0)
            i = l // 2
            if l % 2 == 0:
                x = _ffn(h2, p["ffn_w_gate"][i].astype(BF16)[None], p["ffn_w_up"][i].astype(BF16)[None],
                         p["ffn_w_down"][i].astype(BF16)[None], None, x, mods, groups, row0)
            else:
                gates = _router(x, p["g_ffn"][l], mods, p["moe_router"][i], groups, row0)
                x = _ffn(h2, p["moe_w_gate"][i], p["moe_w_up"][i], p["moe_w_down"][i], gates, x, mods,
                         groups, row0, tf=256)
            outs.append(x)
        x_lat = outs[0]
        if not last:
            x_ctx = outs[1]
    out = _norm_mod(x_lat, p["g_final"], mods_all[0], BATCH, 0, None, F32)
    return out.reshape(BATCH, SEQ, D_MODEL)


def kernel(x, c, ctx, c_ctx, w_mod, b_mod, g_mix, g_ffn, w_in, g_q, w_q_up, g_kv, w_kv_up, hy_conv_w, hy_conv_b, hy_w1, hy_b1, hy_w2, hy_b2, hy_w3, hy_b3, hy_w_filt, hy_freq, hy_skip, swa_sink, w_out, ffn_w_gate, ffn_w_up, ffn_w_down, moe_router, moe_w_gate, moe_w_up, moe_w_down, g_final):
    return _forward(dict(
        x=x, c=c, ctx=ctx, c_ctx=c_ctx, w_mod=w_mod, b_mod=b_mod, g_mix=g_mix, g_ffn=g_ffn, w_in=w_in,
        g_q=g_q, w_q_up=w_q_up, g_kv=g_kv, w_kv_up=w_kv_up, hy_conv_w=hy_conv_w, hy_conv_b=hy_conv_b,
        hy_w1=hy_w1, hy_b1=hy_b1, hy_w2=hy_w2, hy_b2=hy_b2, hy_w3=hy_w3, hy_b3=hy_b3,
        hy_w_filt=hy_w_filt, hy_freq=hy_freq, hy_skip=hy_skip, swa_sink=swa_sink, w_out=w_out,
        ffn_w_gate=ffn_w_gate, ffn_w_up=ffn_w_up, ffn_w_down=ffn_w_down, moe_router=moe_router,
        moe_w_gate=moe_w_gate, moe_w_up=moe_w_up, moe_w_down=moe_w_down, g_final=g_final))
```

```python
import functools
import math

import numpy as np
import jax
import jax.numpy as jnp
from jax import lax
from jax.experimental import pallas as pl
from jax.experimental.pallas import tpu as pltpu

F32 = jnp.float32
BF16 = jnp.bfloat16

D_MODEL = 2048
BATCH = 4
SEQ = 2048
DEPTH = 2
CTX_LEN = 256
GRID_W = 64
NORM_EPS = 1e-6
ROPE_THETA = 10000.0
MLA_HEADS = 8
MLA_NOPE = 128
MLA_ROPE = 64
MLA_V = 128
MLA_Q_LORA = 768
MLA_KV_LORA = 512
HY_WIDTH = 512
HY_BANDS = 16
HY_EMB = 1 + 2 * HY_BANDS
HY_FILTER_HIDDEN = 64
HY_DECAY_TARGET = 1e-2
HY_QUICK_DECAY_PCT = 0.3
HY_SLOW_DECAY_PCT = 1.5
SWA_HEADS = 8
SWA_KV_HEADS = 2
SWA_HEAD_DIM = 64
SWA_WINDOW = 128
N_EXPERTS = 8
D_FF = 5632
D_FF_EXPERT = 7168

LANES = 128
MXU_DIM = 256
VMEM_LIMIT_BYTES = 56 * 1024 * 1024
NEG_BIG = -1e30
MOD_ROWS = 8

_O_CQ = 0
_O_CKV = _O_CQ + MLA_Q_LORA
_O_KPE = _O_CKV + MLA_KV_LORA
_O_HY = _O_KPE + MLA_ROPE
_O_SQ = _O_HY + 3 * HY_WIDTH
_O_SK = _O_SQ + SWA_HEADS * SWA_HEAD_DIM
_O_SV = _O_SK + SWA_KV_HEADS * SWA_HEAD_DIM


def _cparams(*sem):
    return pltpu.CompilerParams(dimension_semantics=sem, vmem_limit_bytes=VMEM_LIMIT_BYTES)


def _dot(a, b):
    return jnp.dot(a, b, preferred_element_type=F32)


def _dot_nt(a, b):
    return lax.dot_general(a, b, (((1,), (1,)), ((), ())), preferred_element_type=F32)


def _split_bf16(a):
    hi = a.astype(BF16)
    lo = (a - hi.astype(F32)).astype(BF16)
    return hi, lo


def _dot3(a, b):
    ah, al = _split_bf16(a)
    bh, bl = _split_bf16(b)
    return _dot(ah, bh) + (_dot(al, bh) + _dot(ah, bl))


def _silu(x):
    return x / (1.0 + jnp.exp(-x))


def _rms(x, g):
    ms = jnp.mean(x * x, axis=-1, keepdims=True)
    return x * lax.rsqrt(ms + NORM_EPS) * g


def _const_spec(shape):
    nd = len(shape)
    return pl.BlockSpec(shape, lambda *_: (0,) * nd, pipeline_mode=pl.Buffered(1))


def _mod_kernel(c_ref, w_ref, b_ref, o_ref):
    a = _silu(c_ref[...])
    o_ref[0] = _dot3(a, w_ref[0]) + b_ref[0]


def _modulation(cvec, w_mod, b_mod):
    tn = 1024
    n = 6 * D_MODEL
    return pl.pallas_call(
        _mod_kernel,
        out_shape=jax.ShapeDtypeStruct((DEPTH, MOD_ROWS, n), F32),
        grid=(DEPTH, n // tn),
        in_specs=[
            pl.BlockSpec((MOD_ROWS, D_MODEL), lambda l, j: (0, 0)),
            pl.BlockSpec((1, D_MODEL, tn), lambda l, j: (l, 0, j)),
            pl.BlockSpec((1, 1, tn), lambda l, j: (l, 0, j)),
        ],
        out_specs=pl.BlockSpec((1, MOD_ROWS, tn), lambda l, j: (l, 0, j)),
        compiler_params=_cparams("arbitrary", "arbitrary"),
        name="adaln_mod",
    )(cvec, w_mod, b_mod.reshape(DEPTH, 1, n))


def _norm_kernel(x_ref, g_ref, m_ref, o_ref, *, si):
    y = _rms(x_ref[...], g_ref[...])
    if si is not None:
        y = y * (1.0 + m_ref[si + 1:si + 2, :]) + m_ref[si:si + 1, :]
    o_ref[...] = y.astype(o_ref.dtype)


def _norm_mod(x, g, mods, groups, row0, si, out_dtype, tm=512):
    m = x.shape[0]
    nr = m // groups // tm
    return pl.pallas_call(
        functools.partial(_norm_kernel, si=si),
        out_shape=jax.ShapeDtypeStruct((m, D_MODEL), out_dtype),
        grid=(groups, nr),
        in_specs=[
            pl.BlockSpec((tm, D_MODEL), lambda g_, i: (g_ * nr + i, 0)),
            pl.BlockSpec((1, D_MODEL), lambda g_, i: (0, 0)),
            pl.BlockSpec((None, 6, D_MODEL), lambda g_, i: (row0 + g_, 0, 0)),
        ],
        out_specs=pl.BlockSpec((tm, D_MODEL), lambda g_, i: (g_ * nr + i, 0)),
        compiler_params=_cparams("parallel", "parallel"),
        name="norm_mod",
    )(x, g.reshape(1, D_MODEL), mods)


def _mm_kernel(x_ref, w_ref, o_ref):
    o_ref[...] = _dot(x_ref[...], w_ref[...]).astype(o_ref.dtype)


def _matmul(x, w, out_dtype, tm=512):
    m, k = x.shape
    n = w.shape[1]
    return pl.pallas_call(
        _mm_kernel,
        out_shape=jax.ShapeDtypeStruct((m, n), out_dtype),
        grid=(m // tm,),
        in_specs=[pl.BlockSpec((tm, k), lambda i: (i, 0)), _const_spec((k, n))],
        out_specs=pl.BlockSpec((tm, n), lambda i: (i, 0)),
        compiler_params=_cparams("parallel"),
        name="matmul",
    )(x, w)


def _rope_halves(pe_pair, tab):
    r = pe_pair * tab
    return r + pltpu.roll(r, MLA_ROPE, axis=1)


def _qproj_kernel(h_ref, wc_ref, g_ref, wq_ref, tab_ref, o_ref):
    cq = _dot(h_ref[...], wc_ref[...])
    cqn = _rms(cq, g_ref[...]).astype(BF16)
    tab = tab_ref[...]
    for hh in range(MLA_HEADS):
        r = _dot(cqn, wq_ref[hh])
        o_ref[hh, :, 0:MLA_NOPE] = r[:, :MLA_NOPE].astype(BF16)
        o_ref[hh, :, MLA_NOPE:] = _rope_halves(r[:, MLA_NOPE:], tab).astype(BF16)


def _q_proj(h, w_cq, g_q, w_q, tab, tm=512):
    m = h.shape[0]
    nt = tab.shape[0] // tm
    return pl.pallas_call(
        _qproj_kernel,
        out_shape=jax.ShapeDtypeStruct((MLA_HEADS, m, MXU_DIM), BF16),
        grid=(m // tm,),
        in_specs=[
            pl.BlockSpec((tm, D_MODEL), lambda i: (i, 0)),
            _const_spec(w_cq.shape),
            _const_spec((1, MLA_Q_LORA)),
            _const_spec(w_q.shape),
            pl.BlockSpec((tm, LANES), lambda i: (i % nt, 0)),
        ],
        out_specs=pl.BlockSpec((MLA_HEADS, tm, MXU_DIM), lambda i: (0, i, 0)),
        compiler_params=_cparams("parallel"),
        name="mla_q_proj",
    )(h, w_cq, g_q.reshape(1, -1), w_q, tab)


def _kvproj_kernel(h_ref, wc_ref, g_ref, wkv_ref, tab_ref, k_ref, v_ref):
    t = _dot(h_ref[...], wc_ref[...])
    ckvn = _rms(t[:, :MLA_KV_LORA], g_ref[...]).astype(BF16)
    rot = _rope_halves(t[:, MLA_KV_LORA:], tab_ref[...])
    lane = lax.broadcasted_iota(jnp.int32, rot.shape, 1)
    kpe = jnp.where(lane < MLA_ROPE, rot, 0.0).astype(BF16)
    for hh in range(MLA_HEADS):
        r = _dot(ckvn, wkv_ref[hh])
        k_ref[hh, :, 0:MLA_NOPE] = r[:, :MLA_NOPE].astype(BF16)
        k_ref[hh, :, MLA_NOPE:] = kpe
        v_ref[hh, :, 0:MLA_V] = r[:, MLA_NOPE:].astype(BF16)
        v_ref[hh, :, MLA_V:] = jnp.ones((r.shape[0], MXU_DIM - MLA_V), BF16)


def _kv_proj(h, w_ckv, g_kv, w_kv, tab, tm=512):
    m = h.shape[0]
    nt = tab.shape[0] // tm
    return pl.pallas_call(
        _kvproj_kernel,
        out_shape=(jax.ShapeDtypeStruct((MLA_HEADS, m, MXU_DIM), BF16),
                   jax.ShapeDtypeStruct((MLA_HEADS, m, MXU_DIM), BF16)),
        grid=(m // tm,),
        in_specs=[
            pl.BlockSpec((tm, D_MODEL), lambda i: (i, 0)),
            _const_spec(w_ckv.shape),
            _const_spec((1, MLA_KV_LORA)),
            _const_spec(w_kv.shape),
            pl.BlockSpec((tm, LANES), lambda i: (i % nt, 0)),
        ],
        out_specs=(pl.BlockSpec((MLA_HEADS, tm, MXU_DIM), lambda i: (0, i, 0)),
                   pl.BlockSpec((MLA_HEADS, tm, MXU_DIM), lambda i: (0, i, 0))),
        compiler_params=_cparams("parallel"),
        name="mla_kv_proj",
    )(h, w_ckv, g_kv.reshape(1, -1), w_kv, tab)


MLA_KEY_CHUNK = 512


def _mla_kernel(q_ref, *refs, seg_lens, scale):
    nseg = len(seg_lens)
    o_ref = refs[2 * nseg]
    q = q_ref[...]
    c2 = scale * math.log2(math.e)
    m = jnp.full((q.shape[0], 1), NEG_BIG, F32)
    acc = jnp.zeros((q.shape[0], MXU_DIM), F32)
    for s, lk in enumerate(seg_lens):
        k_ref, v_ref = refs[2 * s], refs[2 * s + 1]
        step = min(MLA_KEY_CHUNK, lk)
        for c0 in range(0, lk, step):
            sc = _dot_nt(q, k_ref[c0:c0 + step, :])
            m_new = jnp.maximum(m, sc.max(axis=-1, keepdims=True))
            p = jnp.exp2((sc - m_new) * c2).astype(BF16)
            acc = acc * jnp.exp2((m - m_new) * c2) + _dot(p, v_ref[c0:c0 + step, :])
            m = m_new
    o_ref[...] = (acc[:, :MLA_V] / acc[:, MLA_V:MLA_V + 1]).astype(o_ref.dtype)


def _mla_attention(q, segs, lq, tq=512):
    nq = lq // tq
    in_specs = [pl.BlockSpec((None, tq, MXU_DIM), lambda b, h, i: (h, b * nq + i, 0))]
    args = [q]
    for k, v, lk in segs:
        in_specs.append(pl.BlockSpec((None, lk, MXU_DIM), lambda b, h, i: (h, b, 0)))
        in_specs.append(pl.BlockSpec((None, lk, MXU_DIM), lambda b, h, i: (h, b, 0)))
        args += [k, v]
    scale = (MLA_NOPE + MLA_ROPE) ** -0.5
    return pl.pallas_call(
        functools.partial(_mla_kernel, seg_lens=tuple(lk for _, _, lk in segs), scale=scale),
        out_shape=jax.ShapeDtypeStruct((BATCH * lq, MLA_HEADS * MLA_V), BF16),
        grid=(BATCH, MLA_HEADS, nq),
        in_specs=in_specs,
        out_specs=pl.BlockSpec((tq, MLA_V), lambda b, h, i: (b * nq + i, h)),
        compiler_params=_cparams("parallel", "parallel", "arbitrary"),
        name="mla_attention",
    )(*args)


def _dft_tables(n_len):
    n2 = 2 * n_len
    idx = np.arange(n_len, dtype=np.int64)
    ang = (2.0 * np.pi / n2) * ((idx[:, None] * idx[None, :]) % n2).astype(np.float64)
    cm = np.cos(ang)
    sf = np.sin(ang)
    sf[0, :] = 1.0 - 2.0 * (idx % 2)
    return (jnp.asarray(cm, dtype=BF16), jnp.asarray(sf, dtype=BF16),
            jnp.asarray(sf.T.copy(), dtype=BF16))


def _filter_tables(n_len):
    pos = np.arange(n_len, dtype=np.float64)
    t = pos / max(n_len - 1, 1)
    bands = np.linspace(1e-4, HY_BANDS - 1, HY_BANDS)
    ang = (2.0 * math.pi / n_len) * pos[:, None] * bands[None]
    z = np.zeros((n_len, LANES), np.float64)
    z[:, 0] = t
    z[:, 1:1 + HY_BANDS] = np.cos(ang)
    z[:, 1 + HY_BANDS:HY_EMB] = -np.sin(ang)
    deltas = np.linspace(math.log(HY_DECAY_TARGET) / HY_SLOW_DECAY_PCT,
                         math.log(HY_DECAY_TARGET) / HY_QUICK_DECAY_PCT, HY_WIDTH)
    decay = np.exp(-t[:, None] * np.abs(deltas)[None])
    return jnp.asarray(z, dtype=F32), jnp.asarray(decay, dtype=F32)


def _hy_filter_kernel(z_ref, w1_ref, b1_ref, w2_ref, b2_ref, w3_ref, b3_ref, wf_ref, wb_ref,
                      fr_ref, dec_ref, cm_ref, sf_ref, a_ref, b_ref, d_ref, *, n_len):
    fr = fr_ref[...]
    h = jnp.sin(fr * (_dot3(z_ref[...], w1_ref[...]) + b1_ref[...]))
    h = jnp.sin(fr * (_dot3(h, w2_ref[...]) + b2_ref[...]))
    h = jnp.sin(fr * (_dot3(h, w3_ref[...]) + b3_ref[...]))
    dec = dec_ref[...]
    h_f = _dot3(h, wf_ref[...]) * dec
    h_b = _dot3(h, wb_ref[...]) * dec
    row = lax.broadcasted_iota(jnp.int32, h_f.shape, 0)
    h_b = jnp.where(row == 0, 0.0, h_b)
    hs = h_f + h_b
    hd = h_f - h_b
    hs_h, hs_l = _split_bf16(hs)
    hd_h, hd_l = _split_bf16(hd)
    cm = cm_ref[...]
    sf = sf_ref[...]
    t_re = _dot(cm, hs_h) + _dot(cm, hs_l)
    g_t = _dot(sf, hd_h) + _dot(sf, hd_l)
    t_ny = (_dot(sf[0:16, :], hs_h) + _dot(sf[0:16, :], hs_l))[0:1, :]
    inv_n = 1.0 / (2 * n_len)
    first = row == 0
    a_ref[...] = jnp.where(first, inv_n, 2.0 * inv_n) * t_re
    b_ref[...] = jnp.where(first, 0.0, -2.0 * inv_n * g_t)
    d_ref[...] = jnp.where(first, inv_n * t_ny, 2.0 * inv_n * t_re)


def _hy_filters(lp, n_len, tabs):
    z, decay = tabs["filt"]
    cm, sf, _ = tabs["dft"]
    nblk = HY_WIDTH // LANES
    w1 = jnp.zeros((LANES, HY_FILTER_HIDDEN), F32).at[:HY_EMB].set(lp["hy_w1"])
    row = lambda a: a.reshape(1, -1)
    small = lambda shape: pl.BlockSpec(shape, lambda j: (0,) * len(shape))
    fh = HY_FILTER_HIDDEN
    out = jax.ShapeDtypeStruct((n_len, HY_WIDTH), F32)
    return pl.pallas_call(
        functools.partial(_hy_filter_kernel, n_len=n_len),
        out_shape=(out, out, out),
        grid=(nblk,),
        in_specs=[
            small((n_len, LANES)), small((LANES, fh)), small((1, fh)), small((fh, fh)), small((1, fh)),
            small((fh, fh)), small((1, fh)),
            pl.BlockSpec((fh, LANES), lambda j: (0, j)),
            pl.BlockSpec((fh, LANES), lambda j: (0, nblk + j)),
            small((1, fh)),
            pl.BlockSpec((n_len, LANES), lambda j: (0, j)),
            _const_spec((n_len, n_len)), _const_spec((n_len, n_len)),
        ],
        out_specs=tuple(pl.BlockSpec((n_len, LANES), lambda j: (0, j)) for _ in range(3)),
        compiler_params=_cparams("arbitrary"),
        name="hyena_filters",
    )(z, w1, row(lp["hy_b1"]), lp["hy_w2"], row(lp["hy_b2"]), lp["hy_w3"], row(lp["hy_b3"]),
      lp["hy_w_filt"], lp["hy_w_filt"], row(lp["hy_freq"]), decay, cm, sf)


def _hy_pre_kernel(u0_ref, u1_ref, u2_ref, w0_ref, w1_ref, w2_ref, b0_ref, b1_ref, b2_ref,
                   x0_ref, vx_ref, vxb_ref, *, n_len):
    def sconv(u_ref, w_ref, b_ref):
        u = u_ref[...]
        w = w_ref[...]
        row = lax.broadcasted_iota(jnp.int32, u.shape, 0)
        prev = jnp.where(row == 0, 0.0, pltpu.roll(u, 1, axis=0))
        nxt = jnp.where(row == n_len - 1, 0.0, pltpu.roll(u, n_len - 1, axis=0))
        return w[0:1] * prev + w[1:2] * u + w[2:3] * nxt + b_ref[...]

    x0_ref[...] = sconv(u0_ref, w0_ref, b0_ref)
    vx = sconv(u2_ref, w2_ref, b2_ref) * sconv(u1_ref, w1_ref, b1_ref)
    vx_ref[...] = vx
    vxb_ref[...] = vx.astype(BF16)


def _hy_pre(hy, conv_w, conv_b, n_len):
    nblk = HY_WIDTH // LANES
    uspec = lambda part: pl.BlockSpec((None, n_len, LANES), lambda b, j: (b, 0, part * nblk + j))
    wspec = lambda part: pl.BlockSpec((3, LANES), lambda b, j: (0, part * nblk + j))
    bspec = lambda part: pl.BlockSpec((1, LANES), lambda b, j: (0, part * nblk + j))
    ospec = pl.BlockSpec((None, n_len, LANES), lambda b, j: (b, 0, j))
    hy3 = hy.reshape(BATCH, n_len, 3 * HY_WIDTH)
    cb = conv_b.reshape(1, -1)
    return pl.pallas_call(
        functools.partial(_hy_pre_kernel, n_len=n_len),
        out_shape=(jax.ShapeDtypeStruct((BATCH, n_len, HY_WIDTH), F32),
                   jax.ShapeDtypeStruct((BATCH, n_len, HY_WIDTH), F32),
                   jax.ShapeDtypeStruct((BATCH, n_len, HY_WIDTH), BF16)),
        grid=(BATCH, nblk),
        in_specs=[uspec(0), uspec(1), uspec(2), wspec(0), wspec(1), wspec(2), bspec(0), bspec(1), bspec(2)],
        out_specs=(ospec, ospec, ospec),
        compiler_params=_cparams("parallel", "parallel"),
        name="hyena_short_conv",
    )(hy3, hy3, hy3, conv_w, conv_w, conv_w, cb, cb, cb)


def _hy_fwd_kernel(cm_ref, sf_ref, u_ref, a_ref, b_ref, d_ref, q_ref):
    u = u_ref[...]
    ur = _dot(cm_ref[...], u)
    g = _dot(sf_ref[...], u)
    b = b_ref[...]
    q_ref[0] = (a_ref[...] * ur + b * g).astype(BF16)
    q_ref[1] = (d_ref[...] * g - b * ur).astype(BF16)


def _hy_inv_kernel(cm_ref, si_ref, q_ref, x0_ref, vx_ref, skip_ref, o_ref):
    y = _dot(cm_ref[...], q_ref[0]) + _dot(si_ref[...], q_ref[1])
    o_ref[...] = (x0_ref[...] * (y + vx_ref[...] * skip_ref[...])).astype(o_ref.dtype)


def _hy_long_conv(x0, vx, vxb, spectra, skip, n_len, tabs):
    cm, sf, si = tabs["dft"]
    a, b, d = spectra
    tf = min(512, n_len)
    nf = n_len // tf
    mat = pl.BlockSpec((tf, n_len), lambda bb, f: (f, 0))
    spec = pl.BlockSpec((tf, HY_WIDTH), lambda bb, f: (f, 0))
    q = pl.pallas_call(
        _hy_fwd_kernel,
        out_shape=jax.ShapeDtypeStruct((BATCH, 2, n_len, HY_WIDTH), BF16),
        grid=(BATCH, nf),
        in_specs=[mat, mat, pl.BlockSpec((None, n_len, HY_WIDTH), lambda bb, f: (bb, 0, 0)), spec, spec, spec],
        out_specs=pl.BlockSpec((None, 2, tf, HY_WIDTH), lambda bb, f: (bb, 0, f, 0)),
        compiler_params=_cparams("parallel", "arbitrary"),
        name="hyena_dft_fwd",
    )(cm, sf, vxb, a, b, d)
    tile = pl.BlockSpec((None, tf, HY_WIDTH), lambda bb, f: (bb, f, 0))
    return pl.pallas_call(
        _hy_inv_kernel,
        out_shape=jax.ShapeDtypeStruct((BATCH, n_len, HY_WIDTH), BF16),
        grid=(BATCH, nf),
        in_specs=[mat, mat, pl.BlockSpec((None, 2, n_len, HY_WIDTH), lambda bb, f: (bb, 0, 0, 0)),
                  tile, tile, pl.BlockSpec((1, HY_WIDTH), lambda bb, f: (0, 0))],
        out_specs=tile,
        compiler_params=_cparams("parallel", "arbitrary"),
        name="hyena_dft_inv",
    )(cm, si, q, x0, vx, skip.reshape(1, -1))


_SWA_Q = SWA_HEADS * SWA_HEAD_DIM
_SWA_KV = SWA_KV_HEADS * SWA_HEAD_DIM


def _swaproj_kernel(h_ref, w_ref, cos_ref, sin_ref, q_ref, k_ref, v_ref, *, rope):
    t = _dot(h_ref[...], w_ref[...])
    q = t[:, :_SWA_Q]
    k = t[:, 2 * _SWA_Q:2 * _SWA_Q + _SWA_KV]
    v = t[:, 2 * _SWA_Q + 2 * _SWA_KV:]
    if rope:
        cos = cos_ref[...]
        sin = sin_ref[...]
        q = q * cos + t[:, _SWA_Q:2 * _SWA_Q] * sin
        k = k * cos[:, :_SWA_KV] + t[:, 2 * _SWA_Q + _SWA_KV:2 * _SWA_Q + 2 * _SWA_KV] * sin[:, :_SWA_KV]
    q_ref[...] = (q * (SWA_HEAD_DIM ** -0.5)).astype(BF16)
    lo = lax.broadcasted_iota(jnp.int32, k.shape, 1) < SWA_HEAD_DIM
    for src, dst in ((k, k_ref), (v, v_ref)):
        sw = pltpu.roll(src, SWA_HEAD_DIM, axis=1)
        dst[:, 0 * LANES:1 * LANES] = jnp.where(lo, src, 0.0).astype(BF16)
        dst[:, 1 * LANES:2 * LANES] = jnp.where(lo, 0.0, sw).astype(BF16)
        dst[:, 2 * LANES:3 * LANES] = jnp.where(lo, sw, 0.0).astype(BF16)
        dst[:, 3 * LANES:4 * LANES] = jnp.where(lo, 0.0, src).astype(BF16)


def _swa_proj(h, w, cos_t, sin_t, rope, tm=512):
    m = h.shape[0]
    nt = cos_t.shape[0] // tm
    o = jax.ShapeDtypeStruct((m, 4 * LANES), BF16)
    ospec = pl.BlockSpec((tm, 4 * LANES), lambda i: (i, 0))
    tspec = pl.BlockSpec((tm, _SWA_Q), lambda i: (i % nt, 0))
    return pl.pallas_call(
        functools.partial(_swaproj_kernel, rope=rope),
        out_shape=(o, o, o),
        grid=(m // tm,),
        in_specs=[pl.BlockSpec((tm, D_MODEL), lambda i: (i, 0)), _const_spec(w.shape), tspec, tspec],
        out_specs=(ospec, ospec, ospec),
        compiler_params=_cparams("parallel"),
        name="swa_proj",
    )(h, w, cos_t, sin_t)


def _swa_kernel(sink_ref, q_ref, *refs, tq, win, lk, has_lat):
    if has_lat:
        kl_ref, vl_ref, kc_ref, vc_ref, o_ref = refs
        t0 = pl.program_id(1) * tq
        start = pl.multiple_of(jnp.clip(t0 - SWA_WINDOW, 0, lk - win), SWA_WINDOW)
        qpos = t0 + lax.broadcasted_iota(jnp.int32, (tq, win), 0)
        kpos = start + lax.broadcasted_iota(jnp.int32, (tq, win), 1)
        valid = jnp.abs(qpos - kpos) <= SWA_WINDOW
    else:
        kc_ref, vc_ref, o_ref = refs
    for j in range(SWA_HEADS // 2):
        g = j // (SWA_HEADS // SWA_KV_HEADS // 2)
        qb = q_ref[:, j * LANES:(j + 1) * LANES]
        acc = None
        for par in range(2):
            c0 = (2 * g + par) * LANES
            sk = sink_ref[2 * j + par]
            s_c = _dot_nt(qb, kc_ref[:, c0:c0 + LANES])
            m = jnp.maximum(s_c.max(axis=-1, keepdims=True), sk)
            if has_lat:
                s_l = _dot_nt(qb, kl_ref[pl.ds(start, win), c0:c0 + LANES])
                s_l = jnp.where(valid, s_l, NEG_BIG)
                m = jnp.maximum(m, s_l.max(axis=-1, keepdims=True))
            p_c = jnp.exp(s_c - m)
            den = p_c.sum(axis=-1, keepdims=True) + jnp.exp(sk - m)
            if has_lat:
                p_l = jnp.exp(s_l - m)
                den = den + p_l.sum(axis=-1, keepdims=True)
            inv = 1.0 / den
            o = _dot((p_c * inv).astype(BF16), vc_ref[:, c0:c0 + LANES])
            if has_lat:
                o = o + _dot((p_l * inv).astype(BF16), vl_ref[pl.ds(start, win), c0:c0 + LANES])
            acc = o if acc is None else acc + o
        o_ref[:, j * LANES:(j + 1) * LANES] = acc.astype(o_ref.dtype)


def _swa_attention(sink, q, lat, ctx, lq, tq=256):
    nq = lq // tq
    win = tq + 2 * SWA_WINDOW
    full = lambda n: pl.BlockSpec((n, 4 * LANES), lambda b, i: (b, 0))
    in_specs = [pl.BlockSpec(memory_space=pltpu.SMEM),
                pl.BlockSpec((tq, 4 * LANES), lambda b, i: (b * nq + i, 0))]
    args = [sink, q]
    if lat is not None:
        in_specs += [full(SEQ), full(SEQ)]
        args += list(lat)
    in_specs += [full(CTX_LEN), full(CTX_LEN)]
    args += list(ctx)
    return pl.pallas_call(
        functools.partial(_swa_kernel, tq=tq, win=win, lk=SEQ, has_lat=lat is not None),
        out_shape=jax.ShapeDtypeStruct((BATCH * lq, 4 * LANES), BF16),
        grid=(BATCH, nq),
        in_specs=in_specs,
        out_specs=pl.BlockSpec((tq, 4 * LANES), lambda b, i: (b * nq + i, 0)),
        compiler_params=_cparams("parallel", "arbitrary"),
        name="swa_attention",
    )(*args)


def _out_kernel(a_ref, y_ref, s_ref, w_ref, x_ref, m_ref, g_ref, xo_ref, ho_ref):
    na = MLA_HEADS * MLA_V
    acc = _dot(a_ref[...], w_ref[0:na, :])
    acc = acc + _dot(y_ref[...], w_ref[na:na + HY_WIDTH, :])
    acc = acc + _dot(s_ref[...], w_ref[na + HY_WIDTH:, :])
    x = x_ref[...] + m_ref[2:3, :] * acc
    xo_ref[...] = x
    ho_ref[...] = (_rms(x, g_ref[...]) * (1.0 + m_ref[4:5, :]) + m_ref[3:4, :]).astype(ho_ref.dtype)


def _out_proj(a, y, s, w_out, x, mods, g_ffn, groups, row0, h_dtype, tm=512):
    m = x.shape[0]
    nr = m // groups // tm
    tile = lambda n: pl.BlockSpec((tm, n), lambda g_, i: (g_ * nr + i, 0))
    return pl.pallas_call(
        _out_kernel,
        out_shape=(jax.ShapeDtypeStruct((m, D_MODEL), F32), jax.ShapeDtypeStruct((m, D_MODEL), h_dtype)),
        grid=(groups, nr),
        in_specs=[tile(a.shape[1]), tile(y.shape[1]), tile(s.shape[1]), _const_spec(w_out.shape),
                  tile(D_MODEL), pl.BlockSpec((None, 6, D_MODEL), lambda g_, i: (row0 + g_, 0, 0)),
                  pl.BlockSpec((1, D_MODEL), lambda g_, i: (0, 0))],
        out_specs=(tile(D_MODEL), tile(D_MODEL)),
        compiler_params=_cparams("parallel", "parallel"),
        name="out_proj",
    )(a, y, s, w_out, x, mods, g_ffn.reshape(1, -1))


_META_W1, _META_W2, _META_E1, _META_E2, _META_R1, _META_R2 = range(6)


def _router_kernel(h_ref, w_ref, meta_ref, cnt_ref, carry_ref):
    @pl.when(pl.program_id(0) == 0)
    def _():
        carry_ref[...] = jnp.zeros_like(carry_ref)

    logits = _dot3(h_ref[...], w_ref[...])
    tm = logits.shape[0]
    lane = lax.broadcasted_iota(jnp.int32, logits.shape, 1).astype(F32)
    logits = jnp.where(lane < N_EXPERTS, logits, NEG_BIG)
    m1 = logits.max(axis=-1, keepdims=True)
    i1 = jnp.where(logits == m1, lane, float(LANES)).min(axis=-1, keepdims=True)
    rest = jnp.where(lane == i1, NEG_BIG, logits)
    m2 = rest.max(axis=-1, keepdims=True)
    i2 = jnp.where(rest == m2, lane, float(LANES)).min(axis=-1, keepdims=True)
    e2 = jnp.exp(m2 - m1)
    w1 = 1.0 / (1.0 + e2)
    hot = jnp.where((lane == i1) | (lane == i2), 1.0, 0.0)
    before = (lax.broadcasted_iota(jnp.int32, (tm, tm), 0) > lax.broadcasted_iota(jnp.int32, (tm, tm), 1))
    seen = _dot(before.astype(F32).astype(BF16), hot.astype(BF16)) + carry_ref[0:1, :]
    r1 = jnp.where(lane == i1, seen, 0.0).sum(axis=-1, keepdims=True)
    r2 = jnp.where(lane == i2, seen, 0.0).sum(axis=-1, keepdims=True)
    carry_ref[...] = carry_ref[...] + hot.sum(axis=0, keepdims=True)
    cnt_ref[...] = carry_ref[...]
    rec = jnp.zeros_like(logits)
    for k, v in ((_META_W1, w1), (_META_W2, e2 * w1), (_META_E1, i1), (_META_E2, i2), (_META_R1, r1), (_META_R2, r2)):
        rec = jnp.where(lane == float(k), v, rec)
    meta_ref[...] = rec


def _router(h, w_router, tm=512):
    m = h.shape[0]
    wp = jnp.zeros((D_MODEL, LANES), F32).at[:, :N_EXPERTS].set(w_router)
    return pl.pallas_call(
        _router_kernel,
        out_shape=(jax.ShapeDtypeStruct((m, LANES), F32), jax.ShapeDtypeStruct((8, LANES), F32)),
        grid=(m // tm,),
        in_specs=[pl.BlockSpec((tm, D_MODEL), lambda i: (i, 0)),
                  pl.BlockSpec((D_MODEL, LANES), lambda i: (0, 0))],
        out_specs=(pl.BlockSpec((tm, LANES), lambda i: (i, 0)), pl.BlockSpec((8, LANES), lambda i: (0, 0))),
        scratch_shapes=[pltpu.VMEM((8, LANES), F32)],
        compiler_params=_cparams("arbitrary"),
        name="moe_router",
    )(h, wp)


MOE_TM = 1024
MOE_SUB = 512


def _moe_plan(meta, counts, m):
    n_steps = 2 * m // MOE_TM + N_EXPERTS
    cnt = counts[0, :N_EXPERTS].astype(jnp.int32)
    steps_e = (cnt + MOE_TM - 1) // MOE_TM
    ends = jnp.cumsum(steps_e)
    first = ends - steps_e
    total = ends[-1]
    e = meta[:, _META_E1:_META_E2 + 1].astype(jnp.int32)
    rank = meta[:, _META_R1:_META_R2 + 1].astype(jnp.int32)
    pos = (first * MOE_TM)[e] + rank
    s_idx = jnp.arange(n_steps, dtype=jnp.int32)
    step_e = jnp.minimum(jnp.searchsorted(ends, s_idx, side="right").astype(jnp.int32), N_EXPERTS - 1)
    valid = jnp.clip(cnt[step_e] - (s_idx - first[step_e]) * MOE_TM, 0, MOE_TM)
    valid = jnp.where(s_idx < total, valid, 0)
    blk = jnp.minimum(s_idx, total - 1)
    return pos.reshape(-1), step_e, valid, blk, n_steps


def _row_copy(src_ref, src_row, dst_ref, dst_row, sem):
    return pltpu.make_async_copy(src_ref.at[pl.ds(src_row, 1)], dst_ref.at[pl.ds(dst_row, 1)], sem)


def _dispatch_kernel(pos_ref, h_ref, xs_in_ref, xs_ref, sem, *, tm):
    del xs_in_ref
    base = pl.program_id(0) * tm

    def issue(i, c):
        t = base + i
        for slot in range(2):
            _row_copy(h_ref, t, xs_ref, pos_ref[2 * t + slot], sem).start()
        return c

    def drain(i, c):
        for slot in range(2):
            _row_copy(h_ref, 0, xs_ref, 0, sem).wait()
        return c

    lax.fori_loop(0, tm, issue, 0)
    lax.fori_loop(0, tm, drain, 0)


def _moe_dispatch(pos, h, n_rows, tm=512):
    m = h.shape[0]
    any_spec = pl.BlockSpec(memory_space=pl.ANY)
    return pl.pallas_call(
        functools.partial(_dispatch_kernel, tm=tm),
        out_shape=jax.ShapeDtypeStruct((n_rows, D_MODEL), F32),
        grid_spec=pltpu.PrefetchScalarGridSpec(
            num_scalar_prefetch=1, grid=(m // tm,), in_specs=[any_spec, any_spec], out_specs=any_spec,
            scratch_shapes=[pltpu.SemaphoreType.DMA(())]),
        input_output_aliases={2: 0},
        compiler_params=_cparams("arbitrary"),
        name="moe_dispatch",
    )(pos, h, jnp.zeros((n_rows, D_MODEL), F32))


def _moe_ffn_kernel(se_ref, sv_ref, sb_ref, x_ref, wg_ref, wu_ref, wd_ref, o_ref,
                    xb_ref, wgb_ref, wub_ref, wdb_ref):
    del se_ref, sb_ref
    f = pl.program_id(1)
    valid = sv_ref[pl.program_id(0)]

    @pl.when(valid > 0)
    def _():
        @pl.when(f == 0)
        def _():
            xb_ref[...] = x_ref[...].astype(BF16)

        wgb_ref[...] = wg_ref[...].astype(BF16)
        wub_ref[...] = wu_ref[...].astype(BF16)
        wdb_ref[...] = wd_ref[...].astype(BF16)

    for blk in range(MOE_TM // MOE_SUB):
        r0 = blk * MOE_SUB

        @pl.when(valid > r0)
        def _():
            xs = xb_ref[r0:r0 + MOE_SUB, :]
            mid = _silu(_dot(xs, wgb_ref[...])) * _dot(xs, wub_ref[...])
            y = _dot(mid.astype(BF16), wdb_ref[...])

            @pl.when(f == 0)
            def _():
                o_ref[r0:r0 + MOE_SUB, :] = y

            @pl.when(f > 0)
            def _():
                o_ref[r0:r0 + MOE_SUB, :] += y

        @pl.when((valid <= r0) & (f == 0))
        def _():
            o_ref[r0:r0 + MOE_SUB, :] = jnp.zeros((MOE_SUB, D_MODEL), F32)


def _moe_ffn(xs, w_gate, w_up, w_down, step_e, valid, blk, n_steps, tf=256):
    n_ff = w_gate.shape[-1]
    nf = n_ff // tf
    fidx = lambda s, f, sv: jnp.where(sv[s] > 0, f, nf - 1)
    return pl.pallas_call(
        _moe_ffn_kernel,
        out_shape=jax.ShapeDtypeStruct(xs.shape, F32),
        grid_spec=pltpu.PrefetchScalarGridSpec(
            num_scalar_prefetch=3, grid=(n_steps, nf),
            in_specs=[
                pl.BlockSpec((MOE_TM, D_MODEL), lambda s, f, se, sv, sb: (sb[s], 0), pipeline_mode=pl.Buffered(1)),
                pl.BlockSpec((None, D_MODEL, tf), lambda s, f, se, sv, sb: (se[s], 0, fidx(s, f, sv))),
                pl.BlockSpec((None, D_MODEL, tf), lambda s, f, se, sv, sb: (se[s], 0, fidx(s, f, sv))),
                pl.BlockSpec((None, tf, D_MODEL), lambda s, f, se, sv, sb: (se[s], fidx(s, f, sv), 0)),
            ],
            out_specs=pl.BlockSpec((MOE_TM, D_MODEL), lambda s, f, se, sv, sb: (s, 0)),
            scratch_shapes=[pltpu.VMEM((MOE_TM, D_MODEL), BF16), pltpu.VMEM((D_MODEL, tf), BF16),
                            pltpu.VMEM((D_MODEL, tf), BF16), pltpu.VMEM((tf, D_MODEL), BF16)]),
        compiler_params=_cparams("arbitrary", "arbitrary"),
        name="moe_experts",
    )(step_e, valid, blk, xs, w_gate, w_up, w_down)


def _combine_kernel(pos_ref, meta_ref, x_ref, m_ref, gf_ref, ys_ref, o_ref, buf_ref, sem, *, tm, final):
    base = pl.program_id(0) * tm

    def issue(i, c):
        for slot in range(2):
            _row_copy(ys_ref, pos_ref[2 * (base + i) + slot], buf_ref.at[slot], i, sem).start()
        return c

    def drain(i, c):
        for slot in range(2):
            _row_copy(ys_ref, 0, buf_ref.at[slot], 0, sem).wait()
        return c

    lax.fori_loop(0, tm, issue, 0)
    lax.fori_loop(0, tm, drain, 0)
    meta = meta_ref[...]
    y = meta[:, _META_W1:_META_W1 + 1] * buf_ref[0] + meta[:, _META_W2:_META_W2 + 1] * buf_ref[1]
    x = x_ref[...] + m_ref[5:6, :] * y
    if final:
        x = _rms(x, gf_ref[...])
    o_ref[...] = x


def _moe_combine(pos, meta, x, mods, ys, groups, row0, g_final, tm=512):
    m = x.shape[0]
    nr = m // groups // tm
    final = g_final is not None
    gf = (g_final if final else jnp.ones((D_MODEL,), F32)).reshape(1, D_MODEL)
    return pl.pallas_call(
        functools.partial(_combine_kernel, tm=tm, final=final),
        out_shape=jax.ShapeDtypeStruct((m, D_MODEL), F32),
        grid_spec=pltpu.PrefetchScalarGridSpec(
            num_scalar_prefetch=1, grid=(m // tm,),
            in_specs=[pl.BlockSpec((tm, LANES), lambda i, p: (i, 0)),
                      pl.BlockSpec((tm, D_MODEL), lambda i, p: (i, 0)),
                      pl.BlockSpec((None, 6, D_MODEL), lambda i, p: (row0 + i // nr, 0, 0)),
                      pl.BlockSpec((1, D_MODEL), lambda i, p: (0, 0)),
                      pl.BlockSpec(memory_space=pl.ANY)],
            out_specs=pl.BlockSpec((tm, D_MODEL), lambda i, p: (i, 0)),
            scratch_shapes=[pltpu.VMEM((2, tm, D_MODEL), F32), pltpu.SemaphoreType.DMA(())]),
        compiler_params=_cparams("arbitrary"),
        name="moe_combine",
    )(pos, meta, x, mods, gf, ys)


def _moe(h, x, mods, groups, row0, w_router, w_gate, w_up, w_down, g_final):
    m = h.shape[0]
    meta, counts = _router(h, w_router)
    pos, step_e, valid, blk, n_steps = _moe_plan(meta, counts, m)
    xs = _moe_dispatch(pos, h, n_steps * MOE_TM)
    ys = _moe_ffn(xs, w_gate, w_up, w_down, step_e, valid, blk, n_steps)
    return _moe_combine(pos, meta, x, mods, ys, groups, row0, g_final)


def _ffn_kernel(h_ref, wg_ref, wu_ref, wd_ref, x_ref, m_ref, o_ref, acc_ref):
    f = pl.program_id(2)

    @pl.when(f == 0)
    def _():
        acc_ref[...] = jnp.zeros_like(acc_ref)

    h = h_ref[...]
    mid = _silu(_dot(h, wg_ref[...])) * _dot(h, wu_ref[...])
    acc_ref[...] += _dot(mid.astype(BF16), wd_ref[...])

    @pl.when(f == pl.num_programs(2) - 1)
    def _():
        o_ref[...] = x_ref[...] + m_ref[5:6, :] * acc_ref[...]


def _ffn(h, w_gate, w_up, w_down, x, mods, groups, row0, tm=512, tf=512):
    m = x.shape[0]
    n_ff = w_gate.shape[1]
    nr = m // groups // tm
    tile = lambda n: pl.BlockSpec((tm, n), lambda g_, i, f: (g_ * nr + i, 0))
    return pl.pallas_call(
        _ffn_kernel,
        out_shape=jax.ShapeDtypeStruct((m, D_MODEL), F32),
        grid=(groups, nr, n_ff // tf),
        in_specs=[tile(D_MODEL),
                  pl.BlockSpec((D_MODEL, tf), lambda g_, i, f: (0, f)),
                  pl.BlockSpec((D_MODEL, tf), lambda g_, i, f: (0, f)),
                  pl.BlockSpec((tf, D_MODEL), lambda g_, i, f: (f, 0)),
                  tile(D_MODEL),
                  pl.BlockSpec((None, 6, D_MODEL), lambda g_, i, f: (row0 + g_, 0, 0))],
        out_specs=tile(D_MODEL),
        scratch_shapes=[pltpu.VMEM((tm, D_MODEL), F32)],
        compiler_params=_cparams("parallel", "parallel", "arbitrary"),
        name="swiglu_ffn",
    )(h, w_gate, w_up, w_down, x, mods)


def _swap_halves(w):
    half = w.shape[-1] // 2
    return jnp.concatenate([w[..., half:], w[..., :half]], axis=-1)


def _swap_heads(w, heads, dim):
    k = w.shape[0]
    return _swap_halves(w.reshape(k, heads, dim)).reshape(k, heads * dim)


def _rope_tables():
    rows = SEQ // GRID_W
    row = np.repeat(np.arange(rows), GRID_W).astype(np.float32)
    col = np.tile(np.arange(GRID_W), rows).astype(np.float32)
    quarter = MLA_ROPE // 4
    freqs = (np.float32(ROPE_THETA) ** (-np.arange(quarter, dtype=np.float32) / quarter)).astype(np.float32)
    ang = np.concatenate([row[:, None] * freqs[None], col[:, None] * freqs[None]], axis=-1)
    cos = np.cos(ang.astype(np.float64))
    sin = np.sin(ang.astype(np.float64))
    cos2 = np.concatenate([cos, cos], axis=-1)
    sin2 = np.concatenate([-sin, sin], axis=-1)
    tab_lat = np.concatenate([cos2, sin2], axis=-1)
    tab_ctx = np.concatenate([np.ones((CTX_LEN, MLA_ROPE)), np.zeros((CTX_LEN, MLA_ROPE))], axis=-1)
    return dict(
        tab_lat=jnp.asarray(tab_lat, F32), tab_ctx=jnp.asarray(tab_ctx, F32),
        cos8=jnp.asarray(np.tile(cos2, (1, SWA_HEADS)), F32),
        sin8=jnp.asarray(np.tile(sin2, (1, SWA_HEADS)), F32),
        ones8=jnp.ones((CTX_LEN, _SWA_Q), F32), zeros8=jnp.zeros((CTX_LEN, _SWA_Q), F32),
    )


def _layer_weights(p, l):
    w_in = p["w_in"][l]
    cq = w_in[:, _O_CQ:_O_CKV]
    ckv = w_in[:, _O_CKV:_O_KPE]
    kpe = w_in[:, _O_KPE:_O_HY]
    hy = w_in[:, _O_HY:_O_SQ]
    sq = w_in[:, _O_SQ:_O_SK]
    sk = w_in[:, _O_SK:_O_SV]
    sv = w_in[:, _O_SV:]
    dq = MLA_NOPE + MLA_ROPE
    wq = p["w_q_up"][l].reshape(MLA_Q_LORA, MLA_HEADS, dq)
    wq = jnp.concatenate([wq, _swap_halves(wq[..., MLA_NOPE:])], axis=-1)
    wkv = p["w_kv_up"][l].reshape(MLA_KV_LORA, MLA_HEADS, MLA_NOPE + MLA_V)
    return dict(
        w_cq=cq.astype(BF16),
        w_ckv=jnp.concatenate([ckv, kpe, _swap_halves(kpe)], axis=-1).astype(BF16),
        w_hy=hy.astype(BF16),
        w_swa=jnp.concatenate([sq, _swap_heads(sq, SWA_HEADS, SWA_HEAD_DIM), sk,
                               _swap_heads(sk, SWA_KV_HEADS, SWA_HEAD_DIM), sv], axis=-1).astype(BF16),
        w_q=jnp.transpose(wq, (1, 0, 2)).astype(BF16),
        w_kv=jnp.transpose(wkv, (1, 0, 2)).astype(BF16),
        w_out=p["w_out"][l].astype(BF16),
        g_q=p["g_q"][l], g_kv=p["g_kv"][l],
        hy_conv_w=p["hy_conv_w"][l], hy_conv_b=p["hy_conv_b"][l],
        hy_w1=p["hy_w1"][l], hy_b1=p["hy_b1"][l], hy_w2=p["hy_w2"][l], hy_b2=p["hy_b2"][l],
        hy_w3=p["hy_w3"][l], hy_b3=p["hy_b3"][l], hy_w_filt=p["hy_w_filt"][l],
        hy_freq=p["hy_freq"][l], hy_skip=p["hy_skip"][l], swa_sink=p["swa_sink"][l],
    )


def _hyena(h, lw, n_len, tabs):
    hy = _matmul(h, lw["w_hy"], F32)
    x0, vx, vxb = _hy_pre(hy, lw["hy_conv_w"], lw["hy_conv_b"], n_len)
    spectra = _hy_filters(lw, n_len, tabs)
    y = _hy_long_conv(x0, vx, vxb, spectra, lw["hy_skip"], n_len, tabs)
    return y.reshape(BATCH * n_len, HY_WIDTH)


def _mixer(h_lat, h_ctx, lw, rope, hy_tabs, need_ctx):
    k_l, v_l = _kv_proj(h_lat, lw["w_ckv"], lw["g_kv"], lw["w_kv"], rope["tab_lat"])
    k_c, v_c = _kv_proj(h_ctx, lw["w_ckv"], lw["g_kv"], lw["w_kv"], rope["tab_ctx"], tm=CTX_LEN)
    q_l = _q_proj(h_lat, lw["w_cq"], lw["g_q"], lw["w_q"], rope["tab_lat"])
    a_l = _mla_attention(q_l, [(k_l, v_l, SEQ), (k_c, v_c, CTX_LEN)], SEQ)
    y_l = _hyena(h_lat, lw, SEQ, hy_tabs[SEQ])
    sq_l, kk_l, vv_l = _swa_proj(h_lat, lw["w_swa"], rope["cos8"], rope["sin8"], True)
    sq_c, kk_c, vv_c = _swa_proj(h_ctx, lw["w_swa"], rope["ones8"], rope["zeros8"], False, tm=CTX_LEN)
    s_l = _swa_attention(lw["swa_sink"], sq_l, (kk_l, vv_l), (kk_c, vv_c), SEQ)
    if not need_ctx:
        return (a_l, y_l, s_l), None
    q_c = _q_proj(h_ctx, lw["w_cq"], lw["g_q"], lw["w_q"], rope["tab_ctx"], tm=CTX_LEN)
    a_c = _mla_attention(q_c, [(k_c, v_c, CTX_LEN)], CTX_LEN, tq=CTX_LEN)
    y_c = _hyena(h_ctx, lw, CTX_LEN, hy_tabs[CTX_LEN])
    s_c = _swa_attention(lw["swa_sink"], sq_c, None, (kk_c, vv_c), CTX_LEN, tq=CTX_LEN)
    return (a_l, y_l, s_l), (a_c, y_c, s_c)


def _forward(p):
    rope = _rope_tables()
    hy_tabs = {n: dict(dft=_dft_tables(n), filt=_filter_tables(n)) for n in (SEQ, CTX_LEN)}
    cvec = jnp.zeros((MOD_ROWS, D_MODEL), F32).at[:BATCH].set(p["c"]).at[BATCH].set(p["c_ctx"])
    mods_all = _modulation(cvec, p["w_mod"], p["b_mod"]).reshape(DEPTH, MOD_ROWS, 6, D_MODEL)

    x_lat = p["x"].reshape(BATCH * SEQ, D_MODEL)
    x_ctx = p["ctx"].reshape(BATCH * CTX_LEN, D_MODEL)
    for l in range(DEPTH):
        last = l == DEPTH - 1
        mods = mods_all[l]
        lw = _layer_weights(p, l)
        h_lat = _norm_mod(x_lat, p["g_mix"][l], mods, BATCH, 0, 0, BF16)
        h_ctx = _norm_mod(x_ctx, p["g_mix"][l], mods, 1, BATCH, 0, BF16)
        mix_l, mix_c = _mixer(h_lat, h_ctx, lw, rope, hy_tabs, not last)
        streams = [(x_lat, mix_l, BATCH, 0)]
        if not last:
            streams.append((x_ctx, mix_c, 1, BATCH))
        outs = []
        i = l // 2
        dense = l % 2 == 0
        for n, (x, mix, groups, row0) in enumerate(streams):
            x, h2 = _out_proj(*mix, lw["w_out"], x, mods, p["g_ffn"][l], groups, row0, BF16 if dense else F32)
            if dense:
                x = _ffn(h2, p["ffn_w_gate"][i].astype(BF16), p["ffn_w_up"][i].astype(BF16),
                         p["ffn_w_down"][i].astype(BF16), x, mods, groups, row0)
            else:
                g_final = p["g_final"] if (last and n == 0) else None
                x = _moe(h2, x, mods, groups, row0, p["moe_router"][i], p["moe_w_gate"][i], p["moe_w_up"][i],
                         p["moe_w_down"][i], g_final)
            outs.append(x)
        x_lat = outs[0]
        if not last:
            x_ctx = outs[1]
    if DEPTH % 2 == 1:
        x_lat = _norm_mod(x_lat, p["g_final"], mods_all[0], BATCH, 0, None, F32)
    return x_lat.reshape(BATCH, SEQ, D_MODEL)


def kernel(x, c, ctx, c_ctx, w_mod, b_mod, g_mix, g_ffn, w_in, g_q, w_q_up, g_kv, w_kv_up, hy_conv_w, hy_conv_b, hy_w1, hy_b1, hy_w2, hy_b2, hy_w3, hy_b3, hy_w_filt, hy_freq, hy_skip, swa_sink, w_out, ffn_w_gate, ffn_w_up, ffn_w_down, moe_router, moe_w_gate, moe_w_up, moe_w_down, g_final):
    return _forward(dict(
        x=x, c=c, ctx=ctx, c_ctx=c_ctx, w_mod=w_mod, b_mod=b_mod, g_mix=g_mix, g_ffn=g_ffn, w_in=w_in,
        g_q=g_q, w_q_up=w_q_up, g_kv=g_kv, w_kv_up=w_kv_up, hy_conv_w=hy_conv_w, hy_conv_b=hy_conv_b,
        hy_w1=hy_w1, hy_b1=hy_b1, hy_w2=hy_w2, hy_b2=hy_b2, hy_w3=hy_w3, hy_b3=hy_b3,
        hy_w_filt=hy_w_filt, hy_freq=hy_freq, hy_skip=hy_skip, swa_sink=swa_sink, w_out=w_out,
        ffn_w_gate=ffn_w_gate, ffn_w_up=ffn_w_up, ffn_w_down=ffn_w_down, moe_router=moe_router,
        moe_w_gate=moe_w_gate, moe_w_up=moe_w_up, moe_w_down=moe_w_down, g_final=g_final))
```

```python
import functools
import math

import numpy as np
import jax
import jax.numpy as jnp
from jax import lax
from jax.experimental import pallas as pl
from jax.experimental.pallas import tpu as pltpu

F32 = jnp.float32
BF16 = jnp.bfloat16

D_MODEL = 2048
BATCH = 4
SEQ = 2048
DEPTH = 2
CTX_LEN = 256
GRID_W = 64
NORM_EPS = 1e-6
ROPE_THETA = 10000.0
MLA_HEADS = 8
MLA_NOPE = 128
MLA_ROPE = 64
MLA_V = 128
MLA_Q_LORA = 768
MLA_KV_LORA = 512
HY_WIDTH = 512
HY_BANDS = 16
HY_EMB = 1 + 2 * HY_BANDS
HY_FILTER_HIDDEN = 64
HY_DECAY_TARGET = 1e-2
HY_QUICK_DECAY_PCT = 0.3
HY_SLOW_DECAY_PCT = 1.5
SWA_HEADS = 8
SWA_KV_HEADS = 2
SWA_HEAD_DIM = 64
SWA_WINDOW = 128
N_EXPERTS = 8
D_FF = 5632
D_FF_EXPERT = 7168

LANES = 128
MXU_DIM = 256
VMEM_LIMIT_BYTES = 56 * 1024 * 1024
NEG_BIG = -1e30
MOD_ROWS = 8

_O_CQ = 0
_O_CKV = _O_CQ + MLA_Q_LORA
_O_KPE = _O_CKV + MLA_KV_LORA
_O_HY = _O_KPE + MLA_ROPE
_O_SQ = _O_HY + 3 * HY_WIDTH
_O_SK = _O_SQ + SWA_HEADS * SWA_HEAD_DIM
_O_SV = _O_SK + SWA_KV_HEADS * SWA_HEAD_DIM


def _cparams(*sem):
    return pltpu.CompilerParams(dimension_semantics=sem, vmem_limit_bytes=VMEM_LIMIT_BYTES)


def _dot(a, b):
    return jnp.dot(a, b, preferred_element_type=F32)


def _dot_nt(a, b):
    return lax.dot_general(a, b, (((1,), (1,)), ((), ())), preferred_element_type=F32)


def _split_bf16(a):
    hi = a.astype(BF16)
    lo = (a - hi.astype(F32)).astype(BF16)
    return hi, lo


def _dot3(a, b):
    ah, al = _split_bf16(a)
    bh, bl = _split_bf16(b)
    return _dot(ah, bh) + (_dot(al, bh) + _dot(ah, bl))


def _silu(x):
    return x / (1.0 + jnp.exp(-x))


def _rms(x, g):
    ms = jnp.mean(x * x, axis=-1, keepdims=True)
    return x * lax.rsqrt(ms + NORM_EPS) * g


def _const_spec(shape):
    nd = len(shape)
    return pl.BlockSpec(shape, lambda *_: (0,) * nd, pipeline_mode=pl.Buffered(1))


def _mod_kernel(c_ref, w_ref, b_ref, o_ref):
    a = _silu(c_ref[...])
    o_ref[0] = _dot3(a, w_ref[0]) + b_ref[0]


def _modulation(cvec, w_mod, b_mod):
    tn = 1024
    n = 6 * D_MODEL
    return pl.pallas_call(
        _mod_kernel,
        out_shape=jax.ShapeDtypeStruct((DEPTH, MOD_ROWS, n), F32),
        grid=(DEPTH, n // tn),
        in_specs=[
            pl.BlockSpec((MOD_ROWS, D_MODEL), lambda l, j: (0, 0)),
            pl.BlockSpec((1, D_MODEL, tn), lambda l, j: (l, 0, j)),
            pl.BlockSpec((1, 1, tn), lambda l, j: (l, 0, j)),
        ],
        out_specs=pl.BlockSpec((1, MOD_ROWS, tn), lambda l, j: (l, 0, j)),
        compiler_params=_cparams("arbitrary", "arbitrary"),
        name="adaln_mod",
    )(cvec, w_mod, b_mod.reshape(DEPTH, 1, n))


def _norm_kernel(x_ref, g_ref, m_ref, o_ref, *, si):
    y = _rms(x_ref[...], g_ref[...])
    if si is not None:
        y = y * (1.0 + m_ref[si + 1:si + 2, :]) + m_ref[si:si + 1, :]
    o_ref[...] = y.astype(o_ref.dtype)


def _norm_mod(x, g, mods, groups, row0, si, out_dtype, tm=512):
    m = x.shape[0]
    nr = m // groups // tm
    return pl.pallas_call(
        functools.partial(_norm_kernel, si=si),
        out_shape=jax.ShapeDtypeStruct((m, D_MODEL), out_dtype),
        grid=(groups, nr),
        in_specs=[
            pl.BlockSpec((tm, D_MODEL), lambda g_, i: (g_ * nr + i, 0)),
            pl.BlockSpec((1, D_MODEL), lambda g_, i: (0, 0)),
            pl.BlockSpec((None, 6, D_MODEL), lambda g_, i: (row0 + g_, 0, 0)),
        ],
        out_specs=pl.BlockSpec((tm, D_MODEL), lambda g_, i: (g_ * nr + i, 0)),
        compiler_params=_cparams("parallel", "parallel"),
        name="norm_mod",
    )(x, g.reshape(1, D_MODEL), mods)


def _mm_kernel(x_ref, w_ref, o_ref):
    o_ref[...] = _dot(x_ref[...], w_ref[...]).astype(o_ref.dtype)


def _matmul(x, w, out_dtype, tm=512):
    m, k = x.shape
    n = w.shape[1]
    return pl.pallas_call(
        _mm_kernel,
        out_shape=jax.ShapeDtypeStruct((m, n), out_dtype),
        grid=(m // tm,),
        in_specs=[pl.BlockSpec((tm, k), lambda i: (i, 0)), _const_spec((k, n))],
        out_specs=pl.BlockSpec((tm, n), lambda i: (i, 0)),
        compiler_params=_cparams("parallel"),
        name="matmul",
    )(x, w)


def _rope_halves(pe_pair, tab):
    r = pe_pair * tab
    return r + pltpu.roll(r, MLA_ROPE, axis=1)


def _qproj_kernel(h_ref, wc_ref, g_ref, wq_ref, tab_ref, o_ref):
    cq = _dot(h_ref[...], wc_ref[...])
    cqn = _rms(cq, g_ref[...]).astype(BF16)
    tab = tab_ref[...]
    for hh in range(MLA_HEADS):
        r = _dot(cqn, wq_ref[hh])
        o_ref[hh, :, 0:MLA_NOPE] = r[:, :MLA_NOPE].astype(BF16)
        o_ref[hh, :, MLA_NOPE:] = _rope_halves(r[:, MLA_NOPE:], tab).astype(BF16)


def _q_proj(h, w_cq, g_q, w_q, tab, tm=512):
    m = h.shape[0]
    nt = tab.shape[0] // tm
    return pl.pallas_call(
        _qproj_kernel,
        out_shape=jax.ShapeDtypeStruct((MLA_HEADS, m, MXU_DIM), BF16),
        grid=(m // tm,),
        in_specs=[
            pl.BlockSpec((tm, D_MODEL), lambda i: (i, 0)),
            _const_spec(w_cq.shape),
            _const_spec((1, MLA_Q_LORA)),
            _const_spec(w_q.shape),
            pl.BlockSpec((tm, LANES), lambda i: (i % nt, 0)),
        ],
        out_specs=pl.BlockSpec((MLA_HEADS, tm, MXU_DIM), lambda i: (0, i, 0)),
        compiler_params=_cparams("parallel"),
        name="mla_q_proj",
    )(h, w_cq, g_q.reshape(1, -1), w_q, tab)


def _kvproj_kernel(h_ref, wc_ref, g_ref, wkv_ref, tab_ref, k_ref, v_ref):
    t = _dot(h_ref[...], wc_ref[...])
    ckvn = _rms(t[:, :MLA_KV_LORA], g_ref[...]).astype(BF16)
    rot = _rope_halves(t[:, MLA_KV_LORA:], tab_ref[...])
    lane = lax.broadcasted_iota(jnp.int32, rot.shape, 1)
    kpe = jnp.where(lane < MLA_ROPE, rot, 0.0).astype(BF16)
    for hh in range(MLA_HEADS):
        r = _dot(ckvn, wkv_ref[hh])
        k_ref[hh, :, 0:MLA_NOPE] = r[:, :MLA_NOPE].astype(BF16)
        k_ref[hh, :, MLA_NOPE:] = kpe
        v_ref[hh, :, 0:MLA_V] = r[:, MLA_NOPE:].astype(BF16)
        v_ref[hh, :, MLA_V:] = jnp.ones((r.shape[0], MXU_DIM - MLA_V), BF16)


def _kv_proj(h, w_ckv, g_kv, w_kv, tab, tm=512):
    m = h.shape[0]
    nt = tab.shape[0] // tm
    return pl.pallas_call(
        _kvproj_kernel,
        out_shape=(jax.ShapeDtypeStruct((MLA_HEADS, m, MXU_DIM), BF16),
                   jax.ShapeDtypeStruct((MLA_HEADS, m, MXU_DIM), BF16)),
        grid=(m // tm,),
        in_specs=[
            pl.BlockSpec((tm, D_MODEL), lambda i: (i, 0)),
            _const_spec(w_ckv.shape),
            _const_spec((1, MLA_KV_LORA)),
            _const_spec(w_kv.shape),
            pl.BlockSpec((tm, LANES), lambda i: (i % nt, 0)),
        ],
        out_specs=(pl.BlockSpec((MLA_HEADS, tm, MXU_DIM), lambda i: (0, i, 0)),
                   pl.BlockSpec((MLA_HEADS, tm, MXU_DIM), lambda i: (0, i, 0))),
        compiler_params=_cparams("parallel"),
        name="mla_kv_proj",
    )(h, w_ckv, g_kv.reshape(1, -1), w_kv, tab)


MLA_KEY_CHUNK = 512


def _mla_kernel(q_ref, *refs, seg_lens, scale):
    nseg = len(seg_lens)
    o_ref = refs[2 * nseg]
    q = q_ref[...]
    c2 = scale * math.log2(math.e)
    m = jnp.full((q.shape[0], 1), NEG_BIG, F32)
    acc = jnp.zeros((q.shape[0], MXU_DIM), F32)
    for s, lk in enumerate(seg_lens):
        k_ref, v_ref = refs[2 * s], refs[2 * s + 1]
        step = min(MLA_KEY_CHUNK, lk)
        for c0 in range(0, lk, step):
            sc = _dot_nt(q, k_ref[c0:c0 + step, :])
            m_new = jnp.maximum(m, sc.max(axis=-1, keepdims=True))
            p = jnp.exp2((sc - m_new) * c2).astype(BF16)
            acc = acc * jnp.exp2((m - m_new) * c2) + _dot(p, v_ref[c0:c0 + step, :])
            m = m_new
    o_ref[...] = (acc[:, :MLA_V] / acc[:, MLA_V:MLA_V + 1]).astype(o_ref.dtype)


def _mla_attention(q, segs, lq, tq=512):
    nq = lq // tq
    in_specs = [pl.BlockSpec((None, tq, MXU_DIM), lambda b, h, i: (h, b * nq + i, 0))]
    args = [q]
    for k, v, lk in segs:
        in_specs.append(pl.BlockSpec((None, lk, MXU_DIM), lambda b, h, i: (h, b, 0)))
        in_specs.append(pl.BlockSpec((None, lk, MXU_DIM), lambda b, h, i: (h, b, 0)))
        args += [k, v]
    scale = (MLA_NOPE + MLA_ROPE) ** -0.5
    return pl.pallas_call(
        functools.partial(_mla_kernel, seg_lens=tuple(lk for _, _, lk in segs), scale=scale),
        out_shape=jax.ShapeDtypeStruct((BATCH * lq, MLA_HEADS * MLA_V), BF16),
        grid=(BATCH, MLA_HEADS, nq),
        in_specs=in_specs,
        out_specs=pl.BlockSpec((tq, MLA_V), lambda b, h, i: (b * nq + i, h)),
        compiler_params=_cparams("parallel", "parallel", "arbitrary"),
        name="mla_attention",
    )(*args)


def _dft_tables(n_len):
    n2 = 2 * n_len
    idx = np.arange(n_len, dtype=np.int64)
    ang = (2.0 * np.pi / n2) * ((idx[:, None] * idx[None, :]) % n2).astype(np.float64)
    cm = np.cos(ang)
    sf = np.sin(ang)
    sf[0, :] = 1.0 - 2.0 * (idx % 2)
    return (jnp.asarray(cm, dtype=BF16), jnp.asarray(sf, dtype=BF16),
            jnp.asarray(sf.T.copy(), dtype=BF16))


def _filter_tables(n_len):
    pos = np.arange(n_len, dtype=np.float64)
    t = pos / max(n_len - 1, 1)
    bands = np.linspace(1e-4, HY_BANDS - 1, HY_BANDS)
    ang = (2.0 * math.pi / n_len) * pos[:, None] * bands[None]
    z = np.zeros((n_len, LANES), np.float64)
    z[:, 0] = t
    z[:, 1:1 + HY_BANDS] = np.cos(ang)
    z[:, 1 + HY_BANDS:HY_EMB] = -np.sin(ang)
    deltas = np.linspace(math.log(HY_DECAY_TARGET) / HY_SLOW_DECAY_PCT,
                         math.log(HY_DECAY_TARGET) / HY_QUICK_DECAY_PCT, HY_WIDTH)
    decay = np.exp(-t[:, None] * np.abs(deltas)[None])
    return jnp.asarray(z, dtype=F32), jnp.asarray(decay, dtype=F32)


def _hy_filter_kernel(z_ref, w1_ref, b1_ref, w2_ref, b2_ref, w3_ref, b3_ref, wf_ref, wb_ref,
                      fr_ref, dec_ref, cm_ref, sf_ref, a_ref, b_ref, d_ref, hid_ref, *, n_len):
    @pl.when(pl.program_id(0) == 0)
    def _():
        fr = fr_ref[...]
        h = jnp.sin(fr * (_dot3(z_ref[...], w1_ref[...]) + b1_ref[...]))
        h = jnp.sin(fr * (_dot3(h, w2_ref[...]) + b2_ref[...]))
        hid_ref[...] = jnp.sin(fr * (_dot3(h, w3_ref[...]) + b3_ref[...]))

    h = hid_ref[...]
    dec = dec_ref[...]
    h_f = _dot3(h, wf_ref[...]) * dec
    h_b = _dot3(h, wb_ref[...]) * dec
    row = lax.broadcasted_iota(jnp.int32, h_f.shape, 0)
    h_b = jnp.where(row == 0, 0.0, h_b)
    hs = h_f + h_b
    hd = h_f - h_b
    hs_h, hs_l = _split_bf16(hs)
    hd_h, hd_l = _split_bf16(hd)
    cm = cm_ref[...]
    sf = sf_ref[...]
    t_re = _dot(cm, hs_h) + _dot(cm, hs_l)
    g_t = _dot(sf, hd_h) + _dot(sf, hd_l)
    t_ny = (_dot(sf[0:16, :], hs_h) + _dot(sf[0:16, :], hs_l))[0:1, :]
    inv_n = 1.0 / (2 * n_len)
    first = row == 0
    a_ref[...] = jnp.where(first, inv_n, 2.0 * inv_n) * t_re
    b_ref[...] = jnp.where(first, 0.0, -2.0 * inv_n * g_t)
    d_ref[...] = jnp.where(first, inv_n * t_ny, 2.0 * inv_n * t_re)


def _hy_filters(lp, n_len, tabs):
    z, decay = tabs["filt"]
    cm, sf, _ = tabs["dft"]
    nblk = HY_WIDTH // LANES
    w1 = jnp.zeros((LANES, HY_FILTER_HIDDEN), F32).at[:HY_EMB].set(lp["hy_w1"])
    row = lambda a: a.reshape(1, -1)
    small = lambda shape: pl.BlockSpec(shape, lambda j: (0,) * len(shape))
    fh = HY_FILTER_HIDDEN
    out = jax.ShapeDtypeStruct((n_len, HY_WIDTH), F32)
    return pl.pallas_call(
        functools.partial(_hy_filter_kernel, n_len=n_len),
        out_shape=(out, out, out),
        grid=(nblk,),
        in_specs=[
            small((n_len, LANES)), small((LANES, fh)), small((1, fh)), small((fh, fh)), small((1, fh)),
            small((fh, fh)), small((1, fh)),
            pl.BlockSpec((fh, LANES), lambda j: (0, j)),
            pl.BlockSpec((fh, LANES), lambda j: (0, nblk + j)),
            small((1, fh)),
            pl.BlockSpec((n_len, LANES), lambda j: (0, j)),
            _const_spec((n_len, n_len)), _const_spec((n_len, n_len)),
        ],
        out_specs=tuple(pl.BlockSpec((n_len, LANES), lambda j: (0, j)) for _ in range(3)),
        scratch_shapes=[pltpu.VMEM((n_len, fh), F32)],
        compiler_params=_cparams("arbitrary"),
        name="hyena_filters",
    )(z, w1, row(lp["hy_b1"]), lp["hy_w2"], row(lp["hy_b2"]), lp["hy_w3"], row(lp["hy_b3"]),
      lp["hy_w_filt"], lp["hy_w_filt"], row(lp["hy_freq"]), decay, cm, sf)


def _hy_pre_kernel(u0_ref, u1_ref, u2_ref, w0_ref, w1_ref, w2_ref, b0_ref, b1_ref, b2_ref,
                   x0_ref, vx_ref, vxb_ref, *, n_len):
    def sconv(u_ref, w_ref, b_ref):
        u = u_ref[...]
        w = w_ref[...]
        row = lax.broadcasted_iota(jnp.int32, u.shape, 0)
        prev = jnp.where(row == 0, 0.0, pltpu.roll(u, 1, axis=0))
        nxt = jnp.where(row == n_len - 1, 0.0, pltpu.roll(u, n_len - 1, axis=0))
        return w[0:1] * prev + w[1:2] * u + w[2:3] * nxt + b_ref[...]

    x0_ref[...] = sconv(u0_ref, w0_ref, b0_ref)
    vx = sconv(u2_ref, w2_ref, b2_ref) * sconv(u1_ref, w1_ref, b1_ref)
    vx_ref[...] = vx
    vxb_ref[...] = vx.astype(BF16)


def _hy_pre(hy, conv_w, conv_b, n_len):
    nblk = HY_WIDTH // LANES
    uspec = lambda part: pl.BlockSpec((None, n_len, LANES), lambda b, j: (b, 0, part * nblk + j))
    wspec = lambda part: pl.BlockSpec((3, LANES), lambda b, j: (0, part * nblk + j))
    bspec = lambda part: pl.BlockSpec((1, LANES), lambda b, j: (0, part * nblk + j))
    ospec = pl.BlockSpec((None, n_len, LANES), lambda b, j: (b, 0, j))
    hy3 = hy.reshape(BATCH, n_len, 3 * HY_WIDTH)
    cb = conv_b.reshape(1, -1)
    return pl.pallas_call(
        functools.partial(_hy_pre_kernel, n_len=n_len),
        out_shape=(jax.ShapeDtypeStruct((BATCH, n_len, HY_WIDTH), F32),
                   jax.ShapeDtypeStruct((BATCH, n_len, HY_WIDTH), F32),
                   jax.ShapeDtypeStruct((BATCH, n_len, HY_WIDTH), BF16)),
        grid=(BATCH, nblk),
        in_specs=[uspec(0), uspec(1), uspec(2), wspec(0), wspec(1), wspec(2), bspec(0), bspec(1), bspec(2)],
        out_specs=(ospec, ospec, ospec),
        compiler_params=_cparams("parallel", "parallel"),
        name="hyena_short_conv",
    )(hy3, hy3, hy3, conv_w, conv_w, conv_w, cb, cb, cb)


def _hy_fwd_kernel(cm_ref, sf_ref, u_ref, a_ref, b_ref, d_ref, q_ref):
    u = u_ref[...]
    ur = _dot(cm_ref[...], u)
    g = _dot(sf_ref[...], u)
    b = b_ref[...]
    q_ref[0] = (a_ref[...] * ur + b * g).astype(BF16)
    q_ref[1] = (d_ref[...] * g - b * ur).astype(BF16)


def _hy_inv_kernel(cm_ref, si_ref, q_ref, x0_ref, vx_ref, skip_ref, o_ref):
    y = _dot(cm_ref[...], q_ref[0]) + _dot(si_ref[...], q_ref[1])
    o_ref[...] = (x0_ref[...] * (y + vx_ref[...] * skip_ref[...])).astype(o_ref.dtype)


def _hy_long_conv(x0, vx, vxb, spectra, skip, n_len, tabs):
    cm, sf, si = tabs["dft"]
    a, b, d = spectra
    tf = min(512, n_len)
    nf = n_len // tf
    mat = pl.BlockSpec((tf, n_len), lambda bb, f: (f, 0))
    spec = pl.BlockSpec((tf, HY_WIDTH), lambda bb, f: (f, 0))
    q = pl.pallas_call(
        _hy_fwd_kernel,
        out_shape=jax.ShapeDtypeStruct((BATCH, 2, n_len, HY_WIDTH), BF16),
        grid=(BATCH, nf),
        in_specs=[mat, mat, pl.BlockSpec((None, n_len, HY_WIDTH), lambda bb, f: (bb, 0, 0)), spec, spec, spec],
        out_specs=pl.BlockSpec((None, 2, tf, HY_WIDTH), lambda bb, f: (bb, 0, f, 0)),
        compiler_params=_cparams("parallel", "arbitrary"),
        name="hyena_dft_fwd",
    )(cm, sf, vxb, a, b, d)
    tile = pl.BlockSpec((None, tf, HY_WIDTH), lambda bb, f: (bb, f, 0))
    return pl.pallas_call(
        _hy_inv_kernel,
        out_shape=jax.ShapeDtypeStruct((BATCH, n_len, HY_WIDTH), BF16),
        grid=(BATCH, nf),
        in_specs=[mat, mat, pl.BlockSpec((None, 2, n_len, HY_WIDTH), lambda bb, f: (bb, 0, 0, 0)),
                  tile, tile, pl.BlockSpec((1, HY_WIDTH), lambda bb, f: (0, 0))],
        out_specs=tile,
        compiler_params=_cparams("parallel", "arbitrary"),
        name="hyena_dft_inv",
    )(cm, si, q, x0, vx, skip.reshape(1, -1))


_SWA_Q = SWA_HEADS * SWA_HEAD_DIM
_SWA_KV = SWA_KV_HEADS * SWA_HEAD_DIM


def _swaproj_kernel(h_ref, w_ref, cos_ref, sin_ref, q_ref, k_ref, v_ref, *, rope):
    t = _dot(h_ref[...], w_ref[...])
    q = t[:, :_SWA_Q]
    k = t[:, 2 * _SWA_Q:2 * _SWA_Q + _SWA_KV]
    v = t[:, 2 * _SWA_Q + 2 * _SWA_KV:]
    if rope:
        cos = cos_ref[...]
        sin = sin_ref[...]
        q = q * cos + t[:, _SWA_Q:2 * _SWA_Q] * sin
        k = k * cos[:, :_SWA_KV] + t[:, 2 * _SWA_Q + _SWA_KV:2 * _SWA_Q + 2 * _SWA_KV] * sin[:, :_SWA_KV]
    q_ref[...] = (q * (SWA_HEAD_DIM ** -0.5)).astype(BF16)
    lo = lax.broadcasted_iota(jnp.int32, k.shape, 1) < SWA_HEAD_DIM
    for src, dst in ((k, k_ref), (v, v_ref)):
        sw = pltpu.roll(src, SWA_HEAD_DIM, axis=1)
        dst[:, 0 * LANES:1 * LANES] = jnp.where(lo, src, 0.0).astype(BF16)
        dst[:, 1 * LANES:2 * LANES] = jnp.where(lo, 0.0, sw).astype(BF16)
        dst[:, 2 * LANES:3 * LANES] = jnp.where(lo, sw, 0.0).astype(BF16)
        dst[:, 3 * LANES:4 * LANES] = jnp.where(lo, 0.0, src).astype(BF16)


def _swa_proj(h, w, cos_t, sin_t, rope, tm=512):
    m = h.shape[0]
    nt = cos_t.shape[0] // tm
    o = jax.ShapeDtypeStruct((m, 4 * LANES), BF16)
    ospec = pl.BlockSpec((tm, 4 * LANES), lambda i: (i, 0))
    tspec = pl.BlockSpec((tm, _SWA_Q), lambda i: (i % nt, 0))
    return pl.pallas_call(
        functools.partial(_swaproj_kernel, rope=rope),
        out_shape=(o, o, o),
        grid=(m // tm,),
        in_specs=[pl.BlockSpec((tm, D_MODEL), lambda i: (i, 0)), _const_spec(w.shape), tspec, tspec],
        out_specs=(ospec, ospec, ospec),
        compiler_params=_cparams("parallel"),
        name="swa_proj",
    )(h, w, cos_t, sin_t)


def _swa_kernel(sink_ref, q_ref, *refs, tq, win, lk, has_lat):
    if has_lat:
        kl_ref, vl_ref, kc_ref, vc_ref, o_ref = refs
        t0 = pl.program_id(1) * tq
        start = pl.multiple_of(jnp.clip(t0 - SWA_WINDOW, 0, lk - win), SWA_WINDOW)
        qpos = t0 + lax.broadcasted_iota(jnp.int32, (tq, win), 0)
        kpos = start + lax.broadcasted_iota(jnp.int32, (tq, win), 1)
        valid = jnp.abs(qpos - kpos) <= SWA_WINDOW
    else:
        kc_ref, vc_ref, o_ref = refs
    for j in range(SWA_HEADS // 2):
        g = j // (SWA_HEADS // SWA_KV_HEADS // 2)
        qb = q_ref[:, j * LANES:(j + 1) * LANES]
        acc = None
        for par in range(2):
            c0 = (2 * g + par) * LANES
            sk = sink_ref[2 * j + par]
            s_c = _dot_nt(qb, kc_ref[:, c0:c0 + LANES])
            m = jnp.maximum(s_c.max(axis=-1, keepdims=True), sk)
            if has_lat:
                s_l = _dot_nt(qb, kl_ref[pl.ds(start, win), c0:c0 + LANES])
                s_l = jnp.where(valid, s_l, NEG_BIG)
                m = jnp.maximum(m, s_l.max(axis=-1, keepdims=True))
            p_c = jnp.exp(s_c - m)
            den = p_c.sum(axis=-1, keepdims=True) + jnp.exp(sk - m)
            if has_lat:
                p_l = jnp.exp(s_l - m)
                den = den + p_l.sum(axis=-1, keepdims=True)
            inv = 1.0 / den
            o = _dot((p_c * inv).astype(BF16), vc_ref[:, c0:c0 + LANES])
            if has_lat:
                o = o + _dot((p_l * inv).astype(BF16), vl_ref[pl.ds(start, win), c0:c0 + LANES])
            acc = o if acc is None else acc + o
        o_ref[:, j * LANES:(j + 1) * LANES] = acc.astype(o_ref.dtype)


def _swa_attention(sink, q, lat, ctx, lq, tq=256):
    nq = lq // tq
    win = tq + 2 * SWA_WINDOW
    full = lambda n: pl.BlockSpec((n, 4 * LANES), lambda b, i: (b, 0))
    in_specs = [pl.BlockSpec(memory_space=pltpu.SMEM),
                pl.BlockSpec((tq, 4 * LANES), lambda b, i: (b * nq + i, 0))]
    args = [sink, q]
    if lat is not None:
        in_specs += [full(SEQ), full(SEQ)]
        args += list(lat)
    in_specs += [full(CTX_LEN), full(CTX_LEN)]
    args += list(ctx)
    return pl.pallas_call(
        functools.partial(_swa_kernel, tq=tq, win=win, lk=SEQ, has_lat=lat is not None),
        out_shape=jax.ShapeDtypeStruct((BATCH * lq, 4 * LANES), BF16),
        grid=(BATCH, nq),
        in_specs=in_specs,
        out_specs=pl.BlockSpec((tq, 4 * LANES), lambda b, i: (b * nq + i, 0)),
        compiler_params=_cparams("parallel", "arbitrary"),
        name="swa_attention",
    )(*args)


def _out_kernel(a_ref, y_ref, s_ref, w_ref, x_ref, m_ref, g_ref, xo_ref, ho_ref):
    na = MLA_HEADS * MLA_V
    acc = _dot(a_ref[...], w_ref[0:na, :])
    acc = acc + _dot(y_ref[...], w_ref[na:na + HY_WIDTH, :])
    acc = acc + _dot(s_ref[...], w_ref[na + HY_WIDTH:, :])
    x = x_ref[...] + m_ref[2:3, :] * acc
    xo_ref[...] = x
    ho_ref[...] = (_rms(x, g_ref[...]) * (1.0 + m_ref[4:5, :]) + m_ref[3:4, :]).astype(ho_ref.dtype)


def _out_proj(a, y, s, w_out, x, mods, g_ffn, groups, row0, h_dtype, tm=512):
    m = x.shape[0]
    nr = m // groups // tm
    tile = lambda n: pl.BlockSpec((tm, n), lambda g_, i: (g_ * nr + i, 0))
    return pl.pallas_call(
        _out_kernel,
        out_shape=(jax.ShapeDtypeStruct((m, D_MODEL), F32), jax.ShapeDtypeStruct((m, D_MODEL), h_dtype)),
        grid=(groups, nr),
        in_specs=[tile(a.shape[1]), tile(y.shape[1]), tile(s.shape[1]), _const_spec(w_out.shape),
                  tile(D_MODEL), pl.BlockSpec((None, 6, D_MODEL), lambda g_, i: (row0 + g_, 0, 0)),
                  pl.BlockSpec((1, D_MODEL), lambda g_, i: (0, 0))],
        out_specs=(tile(D_MODEL), tile(D_MODEL)),
        compiler_params=_cparams("parallel", "parallel"),
        name="out_proj",
    )(a, y, s, w_out, x, mods, g_ffn.reshape(1, -1))


_META_W1, _META_W2, _META_E1, _META_E2, _META_R1, _META_R2 = range(6)


def _router_kernel(h_ref, w_ref, meta_ref, cnt_ref, carry_ref):
    @pl.when(pl.program_id(0) == 0)
    def _():
        carry_ref[...] = jnp.zeros_like(carry_ref)

    logits = _dot3(h_ref[...], w_ref[...])
    tm = logits.shape[0]
    lane = lax.broadcasted_iota(jnp.int32, logits.shape, 1).astype(F32)
    logits = jnp.where(lane < N_EXPERTS, logits, NEG_BIG)
    m1 = logits.max(axis=-1, keepdims=True)
    i1 = jnp.where(logits == m1, lane, float(LANES)).min(axis=-1, keepdims=True)
    rest = jnp.where(lane == i1, NEG_BIG, logits)
    m2 = rest.max(axis=-1, keepdims=True)
    i2 = jnp.where(rest == m2, lane, float(LANES)).min(axis=-1, keepdims=True)
    e2 = jnp.exp(m2 - m1)
    w1 = 1.0 / (1.0 + e2)
    hot = jnp.where((lane == i1) | (lane == i2), 1.0, 0.0)
    before = (lax.broadcasted_iota(jnp.int32, (tm, tm), 0) > lax.broadcasted_iota(jnp.int32, (tm, tm), 1))
    seen = _dot(before.astype(F32).astype(BF16), hot.astype(BF16)) + carry_ref[0:1, :]
    r1 = jnp.where(lane == i1, seen, 0.0).sum(axis=-1, keepdims=True)
    r2 = jnp.where(lane == i2, seen, 0.0).sum(axis=-1, keepdims=True)
    carry_ref[...] = carry_ref[...] + hot.sum(axis=0, keepdims=True)
    cnt_ref[...] = carry_ref[...]
    rec = jnp.zeros_like(logits)
    for k, v in ((_META_W1, w1), (_META_W2, e2 * w1), (_META_E1, i1), (_META_E2, i2), (_META_R1, r1), (_META_R2, r2)):
        rec = jnp.where(lane == float(k), v, rec)
    meta_ref[...] = rec


def _router(h, w_router, tm=512):
    m = h.shape[0]
    wp = jnp.zeros((D_MODEL, LANES), F32).at[:, :N_EXPERTS].set(w_router)
    return pl.pallas_call(
        _router_kernel,
        out_shape=(jax.ShapeDtypeStruct((m, LANES), F32), jax.ShapeDtypeStruct((8, LANES), F32)),
        grid=(m // tm,),
        in_specs=[pl.BlockSpec((tm, D_MODEL), lambda i: (i, 0)),
                  pl.BlockSpec((D_MODEL, LANES), lambda i: (0, 0))],
        out_specs=(pl.BlockSpec((tm, LANES), lambda i: (i, 0)), pl.BlockSpec((8, LANES), lambda i: (0, 0))),
        scratch_shapes=[pltpu.VMEM((8, LANES), F32)],
        compiler_params=_cparams("arbitrary"),
        name="moe_router",
    )(h, wp)


MOE_TM = 1024
MOE_SUB = 512


def _moe_plan(meta, counts, m):
    n_steps = 2 * m // MOE_TM + N_EXPERTS
    cnt = counts[0, :N_EXPERTS].astype(jnp.int32)
    steps_e = (cnt + MOE_TM - 1) // MOE_TM
    ends = jnp.cumsum(steps_e)
    first = ends - steps_e
    total = ends[-1]
    e = meta[:, _META_E1:_META_E2 + 1].astype(jnp.int32)
    rank = meta[:, _META_R1:_META_R2 + 1].astype(jnp.int32)
    pos = (first * MOE_TM)[e] + rank
    s_idx = jnp.arange(n_steps, dtype=jnp.int32)
    step_e = jnp.minimum(jnp.searchsorted(ends, s_idx, side="right").astype(jnp.int32), N_EXPERTS - 1)
    valid = jnp.clip(cnt[step_e] - (s_idx - first[step_e]) * MOE_TM, 0, MOE_TM)
    valid = jnp.where(s_idx < total, valid, 0)
    return pos.reshape(-1), step_e, valid, n_steps


def _row_copy(src_ref, src_row, dst_ref, dst_row, sem):
    return pltpu.make_async_copy(src_ref.at[pl.ds(src_row, 1)], dst_ref.at[pl.ds(dst_row, 1)], sem)


def _invert_kernel(pos_ref, src_ref, *, n_rows):
    def clear(i, c):
        src_ref[i] = 0
        return c

    def place(i, c):
        src_ref[pos_ref[i]] = i // 2
        return c

    lax.fori_loop(0, n_rows, clear, 0, unroll=8)
    lax.fori_loop(0, pos_ref.shape[0], place, 0, unroll=8)


def _moe_invert(pos, n_rows):
    smem = pl.BlockSpec(memory_space=pltpu.SMEM)
    return pl.pallas_call(
        functools.partial(_invert_kernel, n_rows=n_rows),
        out_shape=jax.ShapeDtypeStruct((n_rows,), jnp.int32),
        in_specs=[smem], out_specs=smem,
        name="moe_invert",
    )(pos)


def _moe_ffn_kernel(se_ref, sv_ref, src_ref, h_ref, wg_ref, wu_ref, wd_ref, o_ref,
                    x32_ref, xb_ref, wgb_ref, wub_ref, wdb_ref, sem):
    del se_ref
    s = pl.program_id(0)
    f = pl.program_id(1)
    valid = sv_ref[s]

    def rows_used(v):
        return (v + MOE_SUB - 1) // MOE_SUB * MOE_SUB

    def gather(step):
        base = step * MOE_TM

        def issue(i, c):
            _row_copy(h_ref, src_ref[base + i], x32_ref, i, sem).start()
            return c

        lax.fori_loop(0, rows_used(sv_ref[step]), issue, 0)

    def wait_rows(n):
        def drain(i, c):
            _row_copy(h_ref, 0, x32_ref, 0, sem).wait()
            return c

        lax.fori_loop(0, n, drain, 0)

    @pl.when(f == 0)
    def _():
        @pl.when(s == 0)
        def _():
            gather(0)

        wait_rows(rows_used(valid))
        for r0 in range(0, MOE_TM, MOE_SUB):
            @pl.when(valid > r0)
            def _():
                xb_ref[r0:r0 + MOE_SUB, :] = x32_ref[r0:r0 + MOE_SUB, :].astype(BF16)

        @pl.when(s + 1 < pl.num_programs(0))
        def _():
            gather(s + 1)

    @pl.when(valid > 0)
    def _():
        wgb_ref[...] = wg_ref[...].astype(BF16)
        wub_ref[...] = wu_ref[...].astype(BF16)
        wdb_ref[...] = wd_ref[...].astype(BF16)

    def compute(rows):
        xs = xb_ref[0:rows, :]
        mid = _silu(_dot(xs, wgb_ref[...])) * _dot(xs, wub_ref[...])
        y = _dot(mid.astype(BF16), wdb_ref[...])

        @pl.when(f == 0)
        def _():
            o_ref[0:rows, :] = y

        @pl.when(f > 0)
        def _():
            o_ref[0:rows, :] += y

    @pl.when(valid > MOE_SUB)
    def _():
        compute(MOE_TM)

    @pl.when((valid > 0) & (valid <= MOE_SUB))
    def _():
        compute(MOE_SUB)

    @pl.when((valid <= MOE_SUB) & (f == 0))
    def _():
        o_ref[MOE_SUB:, :] = jnp.zeros((MOE_TM - MOE_SUB, D_MODEL), F32)

    @pl.when((valid == 0) & (f == 0))
    def _():
        o_ref[0:MOE_SUB, :] = jnp.zeros((MOE_SUB, D_MODEL), F32)


def _moe_ffn(h, src, w_gate, w_up, w_down, step_e, valid, n_steps, tf=256):
    assert MOE_TM == 2 * MOE_SUB
    n_ff = w_gate.shape[-1]
    nf = n_ff // tf
    fidx = lambda s, f, sv: jnp.where(sv[s] > 0, f, nf - 1)
    return pl.pallas_call(
        _moe_ffn_kernel,
        out_shape=jax.ShapeDtypeStruct((n_steps * MOE_TM, D_MODEL), F32),
        grid_spec=pltpu.PrefetchScalarGridSpec(
            num_scalar_prefetch=3, grid=(n_steps, nf),
            in_specs=[
                pl.BlockSpec(memory_space=pl.ANY),
                pl.BlockSpec((None, D_MODEL, tf), lambda s, f, se, sv, sr: (se[s], 0, fidx(s, f, sv))),
                pl.BlockSpec((None, D_MODEL, tf), lambda s, f, se, sv, sr: (se[s], 0, fidx(s, f, sv))),
                pl.BlockSpec((None, tf, D_MODEL), lambda s, f, se, sv, sr: (se[s], fidx(s, f, sv), 0)),
            ],
            out_specs=pl.BlockSpec((MOE_TM, D_MODEL), lambda s, f, se, sv, sr: (s, 0)),
            scratch_shapes=[pltpu.VMEM((MOE_TM, D_MODEL), F32), pltpu.VMEM((MOE_TM, D_MODEL), BF16),
                            pltpu.VMEM((D_MODEL, tf), BF16), pltpu.VMEM((D_MODEL, tf), BF16),
                            pltpu.VMEM((tf, D_MODEL), BF16), pltpu.SemaphoreType.DMA(())]),
        compiler_params=_cparams("arbitrary", "arbitrary"),
        name="moe_experts",
    )(step_e, valid, src, h, w_gate, w_up, w_down)


def _combine_kernel(pos_ref, meta_ref, x_ref, m_ref, gf_ref, ys_ref, o_ref, buf_ref, sem, *, tm, final):
    base = pl.program_id(0) * tm

    def issue(i, c):
        for slot in range(2):
            _row_copy(ys_ref, pos_ref[2 * (base + i) + slot], buf_ref.at[slot], i, sem).start()
        return c

    def drain(i, c):
        for slot in range(2):
            _row_copy(ys_ref, 0, buf_ref.at[slot], 0, sem).wait()
        return c

    lax.fori_loop(0, tm, issue, 0)
    lax.fori_loop(0, tm, drain, 0)
    meta = meta_ref[...]
    y = meta[:, _META_W1:_META_W1 + 1] * buf_ref[0] + meta[:, _META_W2:_META_W2 + 1] * buf_ref[1]
    x = x_ref[...] + m_ref[5:6, :] * y
    if final:
        x = _rms(x, gf_ref[...])
    o_ref[...] = x


def _moe_combine(pos, meta, x, mods, ys, groups, row0, g_final, tm=512):
    m = x.shape[0]
    nr = m // groups // tm
    final = g_final is not None
    gf = (g_final if final else jnp.ones((D_MODEL,), F32)).reshape(1, D_MODEL)
    return pl.pallas_call(
        functools.partial(_combine_kernel, tm=tm, final=final),
        out_shape=jax.ShapeDtypeStruct((m, D_MODEL), F32),
        grid_spec=pltpu.PrefetchScalarGridSpec(
            num_scalar_prefetch=1, grid=(m // tm,),
            in_specs=[pl.BlockSpec((tm, LANES), lambda i, p: (i, 0)),
                      pl.BlockSpec((tm, D_MODEL), lambda i, p: (i, 0)),
                      pl.BlockSpec((None, 6, D_MODEL), lambda i, p: (row0 + i // nr, 0, 0)),
                      pl.BlockSpec((1, D_MODEL), lambda i, p: (0, 0)),
                      pl.BlockSpec(memory_space=pl.ANY)],
            out_specs=pl.BlockSpec((tm, D_MODEL), lambda i, p: (i, 0)),
            scratch_shapes=[pltpu.VMEM((2, tm, D_MODEL), F32), pltpu.SemaphoreType.DMA(())]),
        compiler_params=_cparams("arbitrary"),
        name="moe_combine",
    )(pos, meta, x, mods, gf, ys)


def _moe(h, x, mods, groups, row0, w_router, w_gate, w_up, w_down, g_final):
    m = h.shape[0]
    meta, counts = _router(h, w_router)
    pos, step_e, valid, n_steps = _moe_plan(meta, counts, m)
    src = _moe_invert(pos, n_steps * MOE_TM)
    ys = _moe_ffn(h, src, w_gate, w_up, w_down, step_e, valid, n_steps)
    return _moe_combine(pos, meta, x, mods, ys, groups, row0, g_final)


def _ffn_kernel(h_ref, wg_ref, wu_ref, wd_ref, x_ref, m_ref, o_ref, acc_ref):
    f = pl.program_id(2)

    @pl.when(f == 0)
    def _():
        acc_ref[...] = jnp.zeros_like(acc_ref)

    h = h_ref[...]
    mid = _silu(_dot(h, wg_ref[...])) * _dot(h, wu_ref[...])
    acc_ref[...] += _dot(mid.astype(BF16), wd_ref[...])

    @pl.when(f == pl.num_programs(2) - 1)
    def _():
        o_ref[...] = x_ref[...] + m_ref[5:6, :] * acc_ref[...]


def _ffn(h, w_gate, w_up, w_down, x, mods, groups, row0, tm=512, tf=512):
    m = x.shape[0]
    n_ff = w_gate.shape[1]
    nr = m // groups // tm
    tile = lambda n: pl.BlockSpec((tm, n), lambda g_, i, f: (g_ * nr + i, 0))
    return pl.pallas_call(
        _ffn_kernel,
        out_shape=jax.ShapeDtypeStruct((m, D_MODEL), F32),
        grid=(groups, nr, n_ff // tf),
        in_specs=[tile(D_MODEL),
                  pl.BlockSpec((D_MODEL, tf), lambda g_, i, f: (0, f)),
                  pl.BlockSpec((D_MODEL, tf), lambda g_, i, f: (0, f)),
                  pl.BlockSpec((tf, D_MODEL), lambda g_, i, f: (f, 0)),
                  tile(D_MODEL),
                  pl.BlockSpec((None, 6, D_MODEL), lambda g_, i, f: (row0 + g_, 0, 0))],
        out_specs=tile(D_MODEL),
        scratch_shapes=[pltpu.VMEM((tm, D_MODEL), F32)],
        compiler_params=_cparams("parallel", "parallel", "arbitrary"),
        name="swiglu_ffn",
    )(h, w_gate, w_up, w_down, x, mods)


def _swap_halves(w):
    half = w.shape[-1] // 2
    return jnp.concatenate([w[..., half:], w[..., :half]], axis=-1)


def _swap_heads(w, heads, dim):
    k = w.shape[0]
    return _swap_halves(w.reshape(k, heads, dim)).reshape(k, heads * dim)


def _rope_tables():
    rows = SEQ // GRID_W
    row = np.repeat(np.arange(rows), GRID_W).astype(np.float32)
    col = np.tile(np.arange(GRID_W), rows).astype(np.float32)
    quarter = MLA_ROPE // 4
    freqs = (np.float32(ROPE_THETA) ** (-np.arange(quarter, dtype=np.float32) / quarter)).astype(np.float32)
    ang = np.concatenate([row[:, None] * freqs[None], col[:, None] * freqs[None]], axis=-1)
    cos = np.cos(ang.astype(np.float64))
    sin = np.sin(ang.astype(np.float64))
    cos2 = np.concatenate([cos, cos], axis=-1)
    sin2 = np.concatenate([-sin, sin], axis=-1)
    tab_lat = np.concatenate([cos2, sin2], axis=-1)
    tab_ctx = np.concatenate([np.ones((CTX_LEN, MLA_ROPE)), np.zeros((CTX_LEN, MLA_ROPE))], axis=-1)
    return dict(
        tab_lat=jnp.asarray(tab_lat, F32), tab_ctx=jnp.asarray(tab_ctx, F32),
        cos8=jnp.asarray(np.tile(cos2, (1, SWA_HEADS)), F32),
        sin8=jnp.asarray(np.tile(sin2, (1, SWA_HEADS)), F32),
        ones8=jnp.ones((CTX_LEN, _SWA_Q), F32), zeros8=jnp.zeros((CTX_LEN, _SWA_Q), F32),
    )


def _layer_weights(p, l):
    w_in = p["w_in"][l]
    cq = w_in[:, _O_CQ:_O_CKV]
    ckv = w_in[:, _O_CKV:_O_KPE]
    kpe = w_in[:, _O_KPE:_O_HY]
    hy = w_in[:, _O_HY:_O_SQ]
    sq = w_in[:, _O_SQ:_O_SK]
    sk = w_in[:, _O_SK:_O_SV]
    sv = w_in[:, _O_SV:]
    dq = MLA_NOPE + MLA_ROPE
    wq = p["w_q_up"][l].reshape(MLA_Q_LORA, MLA_HEADS, dq)
    wq = jnp.concatenate([wq, _swap_halves(wq[..., MLA_NOPE:])], axis=-1)
    wkv = p["w_kv_up"][l].reshape(MLA_KV_LORA, MLA_HEADS, MLA_NOPE + MLA_V)
    return dict(
        w_cq=cq.astype(BF16),
        w_ckv=jnp.concatenate([ckv, kpe, _swap_halves(kpe)], axis=-1).astype(BF16),
        w_hy=hy.astype(BF16),
        w_swa=jnp.concatenate([sq, _swap_heads(sq, SWA_HEADS, SWA_HEAD_DIM), sk,
                               _swap_heads(sk, SWA_KV_HEADS, SWA_HEAD_DIM), sv], axis=-1).astype(BF16),
        w_q=jnp.transpose(wq, (1, 0, 2)).astype(BF16),
        w_kv=jnp.transpose(wkv, (1, 0, 2)).astype(BF16),
        w_out=p["w_out"][l].astype(BF16),
        g_q=p["g_q"][l], g_kv=p["g_kv"][l],
        hy_conv_w=p["hy_conv_w"][l], hy_conv_b=p["hy_conv_b"][l],
        hy_w1=p["hy_w1"][l], hy_b1=p["hy_b1"][l], hy_w2=p["hy_w2"][l], hy_b2=p["hy_b2"][l],
        hy_w3=p["hy_w3"][l], hy_b3=p["hy_b3"][l], hy_w_filt=p["hy_w_filt"][l],
        hy_freq=p["hy_freq"][l], hy_skip=p["hy_skip"][l], swa_sink=p["swa_sink"][l],
    )


def _hyena(h, lw, n_len, tabs):
    hy = _matmul(h, lw["w_hy"], F32)
    x0, vx, vxb = _hy_pre(hy, lw["hy_conv_w"], lw["hy_conv_b"], n_len)
    spectra = _hy_filters(lw, n_len, tabs)
    y = _hy_long_conv(x0, vx, vxb, spectra, lw["hy_skip"], n_len, tabs)
    return y.reshape(BATCH * n_len, HY_WIDTH)


def _mixer(h_lat, h_ctx, lw, rope, hy_tabs, need_ctx):
    k_l, v_l = _kv_proj(h_lat, lw["w_ckv"], lw["g_kv"], lw["w_kv"], rope["tab_lat"])
    k_c, v_c = _kv_proj(h_ctx, lw["w_ckv"], lw["g_kv"], lw["w_kv"], rope["tab_ctx"], tm=CTX_LEN)
    q_l = _q_proj(h_lat, lw["w_cq"], lw["g_q"], lw["w_q"], rope["tab_lat"])
    a_l = _mla_attention(q_l, [(k_l, v_l, SEQ), (k_c, v_c, CTX_LEN)], SEQ)
    y_l = _hyena(h_lat, lw, SEQ, hy_tabs[SEQ])
    sq_l, kk_l, vv_l = _swa_proj(h_lat, lw["w_swa"], rope["cos8"], rope["sin8"], True)
    sq_c, kk_c, vv_c = _swa_proj(h_ctx, lw["w_swa"], rope["ones8"], rope["zeros8"], False, tm=CTX_LEN)
    s_l = _swa_attention(lw["swa_sink"], sq_l, (kk_l, vv_l), (kk_c, vv_c), SEQ)
    if not need_ctx:
        return (a_l, y_l, s_l), None
    q_c = _q_proj(h_ctx, lw["w_cq"], lw["g_q"], lw["w_q"], rope["tab_ctx"], tm=CTX_LEN)
    a_c = _mla_attention(q_c, [(k_c, v_c, CTX_LEN)], CTX_LEN, tq=CTX_LEN)
    y_c = _hyena(h_ctx, lw, CTX_LEN, hy_tabs[CTX_LEN])
    s_c = _swa_attention(lw["swa_sink"], sq_c, None, (kk_c, vv_c), CTX_LEN, tq=CTX_LEN)
    return (a_l, y_l, s_l), (a_c, y_c, s_c)


def _forward(p):
    rope = _rope_tables()
    hy_tabs = {n: dict(dft=_dft_tables(n), filt=_filter_tables(n)) for n in (SEQ, CTX_LEN)}
    cvec = jnp.zeros((MOD_ROWS, D_MODEL), F32).at[:BATCH].set(p["c"]).at[BATCH].set(p["c_ctx"])
    mods_all = _modulation(cvec, p["w_mod"], p["b_mod"]).reshape(DEPTH, MOD_ROWS, 6, D_MODEL)

    x_lat = p["x"].reshape(BATCH * SEQ, D_MODEL)
    x_ctx = p["ctx"].reshape(BATCH * CTX_LEN, D_MODEL)
    for l in range(DEPTH):
        last = l == DEPTH - 1
        mods = mods_all[l]
        lw = _layer_weights(p, l)
        h_lat = _norm_mod(x_lat, p["g_mix"][l], mods, BATCH, 0, 0, BF16)
        h_ctx = _norm_mod(x_ctx, p["g_mix"][l], mods, 1, BATCH, 0, BF16)
        mix_l, mix_c = _mixer(h_lat, h_ctx, lw, rope, hy_tabs, not last)
        streams = [(x_lat, mix_l, BATCH, 0)]
        if not last:
            streams.append((x_ctx, mix_c, 1, BATCH))
        outs = []
        i = l // 2
        dense = l % 2 == 0
        for n, (x, mix, groups, row0) in enumerate(streams):
            x, h2 = _out_proj(*mix, lw["w_out"], x, mods, p["g_ffn"][l], groups, row0, BF16 if dense else F32)
            if dense:
                x = _ffn(h2, p["ffn_w_gate"][i].astype(BF16), p["ffn_w_up"][i].astype(BF16),
                         p["ffn_w_down"][i].astype(BF16), x, mods, groups, row0)
            else:
                g_final = p["g_final"] if (last and n == 0) else None
                x = _moe(h2, x, mods, groups, row0, p["moe_router"][i], p["moe_w_gate"][i], p["moe_w_up"][i],
                         p["moe_w_down"][i], g_final)
            outs.append(x)
        x_lat = outs[0]
        if not last:
            x_ctx = outs[1]
    if DEPTH % 2 == 1:
        x_lat = _norm_mod(x_lat, p["g_final"], mods_all[0], BATCH, 0, None, F32)
    return x_lat.reshape(BATCH, SEQ, D_MODEL)


def kernel(x, c, ctx, c_ctx, w_mod, b_mod, g_mix, g_ffn, w_in, g_q, w_q_up, g_kv, w_kv_up, hy_conv_w, hy_conv_b, hy_w1, hy_b1, hy_w2, hy_b2, hy_w3, hy_b3, hy_w_filt, hy_freq, hy_skip, swa_sink, w_out, ffn_w_gate, ffn_w_up, ffn_w_down, moe_router, moe_w_gate, moe_w_up, moe_w_down, g_final):
    return _forward(dict(
        x=x, c=c, ctx=ctx, c_ctx=c_ctx, w_mod=w_mod, b_mod=b_mod, g_mix=g_mix, g_ffn=g_ffn, w_in=w_in,
        g_q=g_q, w_q_up=w_q_up, g_kv=g_kv, w_kv_up=w_kv_up, hy_conv_w=hy_conv_w, hy_conv_b=hy_conv_b,
        hy_w1=hy_w1, hy_b1=hy_b1, hy_w2=hy_w2, hy_b2=hy_b2, hy_w3=hy_w3, hy_b3=hy_b3,
        hy_w_filt=hy_w_filt, hy_freq=hy_freq, hy_skip=hy_skip, swa_sink=swa_sink, w_out=w_out,
        ffn_w_gate=ffn_w_gate, ffn_w_up=ffn_w_up, ffn_w_down=ffn_w_down, moe_router=moe_router,
        moe_w_gate=moe_w_gate, moe_w_up=moe_w_up, moe_w_down=moe_w_down, g_final=g_final))
```

```python
import functools
import math

import numpy as np
import jax
import jax.numpy as jnp
from jax import lax
from jax.experimental import pallas as pl
from jax.experimental.pallas import tpu as pltpu

F32 = jnp.float32
BF16 = jnp.bfloat16

D_MODEL = 2048
BATCH = 4
SEQ = 2048
DEPTH = 2
CTX_LEN = 256
GRID_W = 64
NORM_EPS = 1e-6
ROPE_THETA = 10000.0
MLA_HEADS = 8
MLA_NOPE = 128
MLA_ROPE = 64
MLA_V = 128
MLA_Q_LORA = 768
MLA_KV_LORA = 512
HY_WIDTH = 512
HY_BANDS = 16
HY_EMB = 1 + 2 * HY_BANDS
HY_FILTER_HIDDEN = 64
HY_DECAY_TARGET = 1e-2
HY_QUICK_DECAY_PCT = 0.3
HY_SLOW_DECAY_PCT = 1.5
SWA_HEADS = 8
SWA_KV_HEADS = 2
SWA_HEAD_DIM = 64
SWA_WINDOW = 128
N_EXPERTS = 8
D_FF = 5632
D_FF_EXPERT = 7168

LANES = 128
MXU_DIM = 256
VMEM_LIMIT_BYTES = 56 * 1024 * 1024
NEG_BIG = -1e30
MOD_ROWS = 8

_O_CQ = 0
_O_CKV = _O_CQ + MLA_Q_LORA
_O_KPE = _O_CKV + MLA_KV_LORA
_O_HY = _O_KPE + MLA_ROPE
_O_SQ = _O_HY + 3 * HY_WIDTH
_O_SK = _O_SQ + SWA_HEADS * SWA_HEAD_DIM
_O_SV = _O_SK + SWA_KV_HEADS * SWA_HEAD_DIM


def _cparams(*sem):
    return pltpu.CompilerParams(dimension_semantics=sem, vmem_limit_bytes=VMEM_LIMIT_BYTES)


def _dot(a, b):
    return jnp.dot(a, b, preferred_element_type=F32)


def _dot_nt(a, b):
    return lax.dot_general(a, b, (((1,), (1,)), ((), ())), preferred_element_type=F32)


def _split_bf16(a):
    hi = a.astype(BF16)
    lo = (a - hi.astype(F32)).astype(BF16)
    return hi, lo


def _dot3(a, b):
    ah, al = _split_bf16(a)
    bh, bl = _split_bf16(b)
    return _dot(ah, bh) + (_dot(al, bh) + _dot(ah, bl))


def _silu(x):
    return x / (1.0 + jnp.exp(-x))


def _rms(x, g):
    ms = jnp.mean(x * x, axis=-1, keepdims=True)
    return x * lax.rsqrt(ms + NORM_EPS) * g


def _const_spec(shape):
    nd = len(shape)
    return pl.BlockSpec(shape, lambda *_: (0,) * nd, pipeline_mode=pl.Buffered(1))


def _mod_kernel(c_ref, w_ref, b_ref, o_ref):
    a = _silu(c_ref[...])
    o_ref[0] = _dot3(a, w_ref[0]) + b_ref[0]


def _modulation(cvec, w_mod, b_mod):
    tn = 1024
    n = 6 * D_MODEL
    return pl.pallas_call(
        _mod_kernel,
        out_shape=jax.ShapeDtypeStruct((DEPTH, MOD_ROWS, n), F32),
        grid=(DEPTH, n // tn),
        in_specs=[
            pl.BlockSpec((MOD_ROWS, D_MODEL), lambda l, j: (0, 0)),
            pl.BlockSpec((1, D_MODEL, tn), lambda l, j: (l, 0, j)),
            pl.BlockSpec((1, 1, tn), lambda l, j: (l, 0, j)),
        ],
        out_specs=pl.BlockSpec((1, MOD_ROWS, tn), lambda l, j: (l, 0, j)),
        compiler_params=_cparams("arbitrary", "arbitrary"),
        name="adaln_mod",
    )(cvec, w_mod, b_mod.reshape(DEPTH, 1, n))


def _norm_kernel(x_ref, g_ref, m_ref, o_ref, *, si):
    y = _rms(x_ref[...], g_ref[...])
    if si is not None:
        y = y * (1.0 + m_ref[si + 1:si + 2, :]) + m_ref[si:si + 1, :]
    o_ref[...] = y.astype(o_ref.dtype)


def _norm_mod(x, g, mods, groups, row0, si, out_dtype, tm=512):
    m = x.shape[0]
    nr = m // groups // tm
    return pl.pallas_call(
        functools.partial(_norm_kernel, si=si),
        out_shape=jax.ShapeDtypeStruct((m, D_MODEL), out_dtype),
        grid=(groups, nr),
        in_specs=[
            pl.BlockSpec((tm, D_MODEL), lambda g_, i: (g_ * nr + i, 0)),
            pl.BlockSpec((1, D_MODEL), lambda g_, i: (0, 0)),
            pl.BlockSpec((None, 6, D_MODEL), lambda g_, i: (row0 + g_, 0, 0)),
        ],
        out_specs=pl.BlockSpec((tm, D_MODEL), lambda g_, i: (g_ * nr + i, 0)),
        compiler_params=_cparams("parallel", "parallel"),
        name="norm_mod",
    )(x, g.reshape(1, D_MODEL), mods)


def _mm_kernel(x_ref, w_ref, o_ref):
    o_ref[...] = _dot(x_ref[...], w_ref[...]).astype(o_ref.dtype)


def _matmul(x, w, out_dtype, tm=512):
    m, k = x.shape
    n = w.shape[1]
    return pl.pallas_call(
        _mm_kernel,
        out_shape=jax.ShapeDtypeStruct((m, n), out_dtype),
        grid=(m // tm,),
        in_specs=[pl.BlockSpec((tm, k), lambda i: (i, 0)), _const_spec((k, n))],
        out_specs=pl.BlockSpec((tm, n), lambda i: (i, 0)),
        compiler_params=_cparams("parallel"),
        name="matmul",
    )(x, w)


def _rope_halves(pe_pair, tab):
    r = pe_pair * tab
    return r + pltpu.roll(r, MLA_ROPE, axis=1)


def _qproj_kernel(h_ref, wc_ref, g_ref, wq_ref, tab_ref, o_ref):
    cq = _dot(h_ref[...], wc_ref[...])
    cqn = _rms(cq, g_ref[...]).astype(BF16)
    tab = tab_ref[...]
    for hh in range(MLA_HEADS):
        r = _dot(cqn, wq_ref[hh])
        o_ref[hh, :, 0:MLA_NOPE] = r[:, :MLA_NOPE].astype(BF16)
        o_ref[hh, :, MLA_NOPE:] = _rope_halves(r[:, MLA_NOPE:], tab).astype(BF16)


def _q_proj(h, w_cq, g_q, w_q, tab, tm=512):
    m = h.shape[0]
    nt = tab.shape[0] // tm
    return pl.pallas_call(
        _qproj_kernel,
        out_shape=jax.ShapeDtypeStruct((MLA_HEADS, m, MXU_DIM), BF16),
        grid=(m // tm,),
        in_specs=[
            pl.BlockSpec((tm, D_MODEL), lambda i: (i, 0)),
            _const_spec(w_cq.shape),
            _const_spec((1, MLA_Q_LORA)),
            _const_spec(w_q.shape),
            pl.BlockSpec((tm, LANES), lambda i: (i % nt, 0)),
        ],
        out_specs=pl.BlockSpec((MLA_HEADS, tm, MXU_DIM), lambda i: (0, i, 0)),
        compiler_params=_cparams("parallel"),
        name="mla_q_proj",
    )(h, w_cq, g_q.reshape(1, -1), w_q, tab)


def _kvproj_kernel(h_ref, wc_ref, g_ref, wkv_ref, tab_ref, k_ref, v_ref):
    t = _dot(h_ref[...], wc_ref[...])
    ckvn = _rms(t[:, :MLA_KV_LORA], g_ref[...]).astype(BF16)
    rot = _rope_halves(t[:, MLA_KV_LORA:], tab_ref[...])
    lane = lax.broadcasted_iota(jnp.int32, rot.shape, 1)
    kpe = jnp.where(lane < MLA_ROPE, rot, 0.0).astype(BF16)
    for hh in range(MLA_HEADS):
        r = _dot(ckvn, wkv_ref[hh])
        k_ref[hh, :, 0:MLA_NOPE] = r[:, :MLA_NOPE].astype(BF16)
        k_ref[hh, :, MLA_NOPE:] = kpe
        v_ref[hh, :, 0:MLA_V] = r[:, MLA_NOPE:].astype(BF16)
        v_ref[hh, :, MLA_V:] = jnp.ones((r.shape[0], MXU_DIM - MLA_V), BF16)


def _kv_proj(h, w_ckv, g_kv, w_kv, tab, tm=512):
    m = h.shape[0]
    nt = tab.shape[0] // tm
    return pl.pallas_call(
        _kvproj_kernel,
        out_shape=(jax.ShapeDtypeStruct((MLA_HEADS, m, MXU_DIM), BF16),
                   jax.ShapeDtypeStruct((MLA_HEADS, m, MXU_DIM), BF16)),
        grid=(m // tm,),
        in_specs=[
            pl.BlockSpec((tm, D_MODEL), lambda i: (i, 0)),
            _const_spec(w_ckv.shape),
            _const_spec((1, MLA_KV_LORA)),
            _const_spec(w_kv.shape),
            pl.BlockSpec((tm, LANES), lambda i: (i % nt, 0)),
        ],
        out_specs=(pl.BlockSpec((MLA_HEADS, tm, MXU_DIM), lambda i: (0, i, 0)),
                   pl.BlockSpec((MLA_HEADS, tm, MXU_DIM), lambda i: (0, i, 0))),
        compiler_params=_cparams("parallel"),
        name="mla_kv_proj",
    )(h, w_ckv, g_kv.reshape(1, -1), w_kv, tab)


MLA_KEY_CHUNK = 512


def _mla_kernel(q_ref, *refs, seg_lens, scale):
    nseg = len(seg_lens)
    o_ref = refs[2 * nseg]
    q = q_ref[...]
    c2 = scale * math.log2(math.e)
    m = jnp.full((q.shape[0], 1), NEG_BIG, F32)
    acc = jnp.zeros((q.shape[0], MXU_DIM), F32)
    for s, lk in enumerate(seg_lens):
        k_ref, v_ref = refs[2 * s], refs[2 * s + 1]
        step = min(MLA_KEY_CHUNK, lk)
        for c0 in range(0, lk, step):
            sc = _dot_nt(q, k_ref[c0:c0 + step, :])
            m_new = jnp.maximum(m, sc.max(axis=-1, keepdims=True))
            p = jnp.exp2((sc - m_new) * c2).astype(BF16)
            acc = acc * jnp.exp2((m - m_new) * c2) + _dot(p, v_ref[c0:c0 + step, :])
            m = m_new
    o_ref[...] = (acc[:, :MLA_V] / acc[:, MLA_V:MLA_V + 1]).astype(o_ref.dtype)


def _mla_attention(q, segs, lq, tq=512):
    nq = lq // tq
    in_specs = [pl.BlockSpec((None, tq, MXU_DIM), lambda b, h, i: (h, b * nq + i, 0))]
    args = [q]
    for k, v, lk in segs:
        in_specs.append(pl.BlockSpec((None, lk, MXU_DIM), lambda b, h, i: (h, b, 0)))
        in_specs.append(pl.BlockSpec((None, lk, MXU_DIM), lambda b, h, i: (h, b, 0)))
        args += [k, v]
    scale = (MLA_NOPE + MLA_ROPE) ** -0.5
    return pl.pallas_call(
        functools.partial(_mla_kernel, seg_lens=tuple(lk for _, _, lk in segs), scale=scale),
        out_shape=jax.ShapeDtypeStruct((BATCH * lq, MLA_HEADS * MLA_V), BF16),
        grid=(BATCH, MLA_HEADS, nq),
        in_specs=in_specs,
        out_specs=pl.BlockSpec((tq, MLA_V), lambda b, h, i: (b * nq + i, h)),
        compiler_params=_cparams("parallel", "parallel", "arbitrary"),
        name="mla_attention",
    )(*args)


def _dft_tables(n_len):
    n2 = 2 * n_len
    idx = np.arange(n_len, dtype=np.int64)
    ang = (2.0 * np.pi / n2) * ((idx[:, None] * idx[None, :]) % n2).astype(np.float64)
    cm = np.cos(ang)
    sf = np.sin(ang)
    sf[0, :] = 1.0 - 2.0 * (idx % 2)
    return (jnp.asarray(cm, dtype=BF16), jnp.asarray(sf, dtype=BF16),
            jnp.asarray(sf.T.copy(), dtype=BF16))


def _filter_tables(n_len):
    pos = np.arange(n_len, dtype=np.float64)
    t = pos / max(n_len - 1, 1)
    bands = np.linspace(1e-4, HY_BANDS - 1, HY_BANDS)
    ang = (2.0 * math.pi / n_len) * pos[:, None] * bands[None]
    z = np.zeros((n_len, LANES), np.float64)
    z[:, 0] = t
    z[:, 1:1 + HY_BANDS] = np.cos(ang)
    z[:, 1 + HY_BANDS:HY_EMB] = -np.sin(ang)
    deltas = np.linspace(math.log(HY_DECAY_TARGET) / HY_SLOW_DECAY_PCT,
                         math.log(HY_DECAY_TARGET) / HY_QUICK_DECAY_PCT, HY_WIDTH)
    decay = np.exp(-t[:, None] * np.abs(deltas)[None])
    return jnp.asarray(z, dtype=F32), jnp.asarray(decay, dtype=F32)


def _hy_filter_kernel(z_ref, w1_ref, b1_ref, w2_ref, b2_ref, w3_ref, b3_ref, wf_ref, wb_ref,
                      fr_ref, dec_ref, cm_ref, sf_ref, a_ref, b_ref, d_ref, hid_ref, *, n_len):
    @pl.when(pl.program_id(0) == 0)
    def _():
        fr = fr_ref[...]
        h = jnp.sin(fr * (_dot3(z_ref[...], w1_ref[...]) + b1_ref[...]))
        h = jnp.sin(fr * (_dot3(h, w2_ref[...]) + b2_ref[...]))
        hid_ref[...] = jnp.sin(fr * (_dot3(h, w3_ref[...]) + b3_ref[...]))

    h = hid_ref[...]
    dec = dec_ref[...]
    h_f = _dot3(h, wf_ref[...]) * dec
    h_b = _dot3(h, wb_ref[...]) * dec
    row = lax.broadcasted_iota(jnp.int32, h_f.shape, 0)
    h_b = jnp.where(row == 0, 0.0, h_b)
    hs = h_f + h_b
    hd = h_f - h_b
    hs_h, hs_l = _split_bf16(hs)
    hd_h, hd_l = _split_bf16(hd)
    cm = cm_ref[...]
    sf = sf_ref[...]
    t_re = _dot(cm, hs_h) + _dot(cm, hs_l)
    g_t = _dot(sf, hd_h) + _dot(sf, hd_l)
    t_ny = (_dot(sf[0:16, :], hs_h) + _dot(sf[0:16, :], hs_l))[0:1, :]
    inv_n = 1.0 / (2 * n_len)
    first = row == 0
    a_ref[...] = jnp.where(first, inv_n, 2.0 * inv_n) * t_re
    b_ref[...] = jnp.where(first, 0.0, -2.0 * inv_n * g_t)
    d_ref[...] = jnp.where(first, inv_n * t_ny, 2.0 * inv_n * t_re)


def _hy_filters(lp, n_len, tabs):
    z, decay = tabs["filt"]
    cm, sf, _ = tabs["dft"]
    nblk = HY_WIDTH // LANES
    w1 = jnp.zeros((LANES, HY_FILTER_HIDDEN), F32).at[:HY_EMB].set(lp["hy_w1"])
    row = lambda a: a.reshape(1, -1)
    small = lambda shape: pl.BlockSpec(shape, lambda j: (0,) * len(shape))
    fh = HY_FILTER_HIDDEN
    out = jax.ShapeDtypeStruct((n_len, HY_WIDTH), F32)
    return pl.pallas_call(
        functools.partial(_hy_filter_kernel, n_len=n_len),
        out_shape=(out, out, out),
        grid=(nblk,),
        in_specs=[
            small((n_len, LANES)), small((LANES, fh)), small((1, fh)), small((fh, fh)), small((1, fh)),
            small((fh, fh)), small((1, fh)),
            pl.BlockSpec((fh, LANES), lambda j: (0, j)),
            pl.BlockSpec((fh, LANES), lambda j: (0, nblk + j)),
            small((1, fh)),
            pl.BlockSpec((n_len, LANES), lambda j: (0, j)),
            _const_spec((n_len, n_len)), _const_spec((n_len, n_len)),
        ],
        out_specs=tuple(pl.BlockSpec((n_len, LANES), lambda j: (0, j)) for _ in range(3)),
        scratch_shapes=[pltpu.VMEM((n_len, fh), F32)],
        compiler_params=_cparams("arbitrary"),
        name="hyena_filters",
    )(z, w1, row(lp["hy_b1"]), lp["hy_w2"], row(lp["hy_b2"]), lp["hy_w3"], row(lp["hy_b3"]),
      lp["hy_w_filt"], lp["hy_w_filt"], row(lp["hy_freq"]), decay, cm, sf)


def _hy_pre_kernel(u0_ref, u1_ref, u2_ref, w0_ref, w1_ref, w2_ref, b0_ref, b1_ref, b2_ref,
                   x0_ref, vx_ref, vxb_ref, *, n_len):
    def sconv(u_ref, w_ref, b_ref):
        u = u_ref[...]
        w = w_ref[...]
        row = lax.broadcasted_iota(jnp.int32, u.shape, 0)
        prev = jnp.where(row == 0, 0.0, pltpu.roll(u, 1, axis=0))
        nxt = jnp.where(row == n_len - 1, 0.0, pltpu.roll(u, n_len - 1, axis=0))
        return w[0:1] * prev + w[1:2] * u + w[2:3] * nxt + b_ref[...]

    x0_ref[...] = sconv(u0_ref, w0_ref, b0_ref)
    vx = sconv(u2_ref, w2_ref, b2_ref) * sconv(u1_ref, w1_ref, b1_ref)
    vx_ref[...] = vx
    vxb_ref[...] = vx.astype(BF16)


def _hy_pre(hy, conv_w, conv_b, n_len):
    nblk = HY_WIDTH // LANES
    uspec = lambda part: pl.BlockSpec((None, n_len, LANES), lambda b, j: (b, 0, part * nblk + j))
    wspec = lambda part: pl.BlockSpec((3, LANES), lambda b, j: (0, part * nblk + j))
    bspec = lambda part: pl.BlockSpec((1, LANES), lambda b, j: (0, part * nblk + j))
    ospec = pl.BlockSpec((None, n_len, LANES), lambda b, j: (b, 0, j))
    hy3 = hy.reshape(BATCH, n_len, 3 * HY_WIDTH)
    cb = conv_b.reshape(1, -1)
    return pl.pallas_call(
        functools.partial(_hy_pre_kernel, n_len=n_len),
        out_shape=(jax.ShapeDtypeStruct((BATCH, n_len, HY_WIDTH), F32),
                   jax.ShapeDtypeStruct((BATCH, n_len, HY_WIDTH), F32),
                   jax.ShapeDtypeStruct((BATCH, n_len, HY_WIDTH), BF16)),
        grid=(BATCH, nblk),
        in_specs=[uspec(0), uspec(1), uspec(2), wspec(0), wspec(1), wspec(2), bspec(0), bspec(1), bspec(2)],
        out_specs=(ospec, ospec, ospec),
        compiler_params=_cparams("parallel", "parallel"),
        name="hyena_short_conv",
    )(hy3, hy3, hy3, conv_w, conv_w, conv_w, cb, cb, cb)


def _hy_fwd_kernel(cm_ref, sf_ref, u_ref, a_ref, b_ref, d_ref, q_ref):
    u = u_ref[...]
    ur = _dot(cm_ref[...], u)
    g = _dot(sf_ref[...], u)
    b = b_ref[...]
    q_ref[0] = (a_ref[...] * ur + b * g).astype(BF16)
    q_ref[1] = (d_ref[...] * g - b * ur).astype(BF16)


def _hy_inv_kernel(cm_ref, si_ref, q_ref, x0_ref, vx_ref, skip_ref, o_ref):
    y = _dot(cm_ref[...], q_ref[0]) + _dot(si_ref[...], q_ref[1])
    o_ref[...] = (x0_ref[...] * (y + vx_ref[...] * skip_ref[...])).astype(o_ref.dtype)


def _hy_long_conv(x0, vx, vxb, spectra, skip, n_len, tabs):
    cm, sf, si = tabs["dft"]
    a, b, d = spectra
    tf = min(512, n_len)
    nf = n_len // tf
    mat = pl.BlockSpec((tf, n_len), lambda bb, f: (f, 0))
    spec = pl.BlockSpec((tf, HY_WIDTH), lambda bb, f: (f, 0))
    q = pl.pallas_call(
        _hy_fwd_kernel,
        out_shape=jax.ShapeDtypeStruct((BATCH, 2, n_len, HY_WIDTH), BF16),
        grid=(BATCH, nf),
        in_specs=[mat, mat, pl.BlockSpec((None, n_len, HY_WIDTH), lambda bb, f: (bb, 0, 0)), spec, spec, spec],
        out_specs=pl.BlockSpec((None, 2, tf, HY_WIDTH), lambda bb, f: (bb, 0, f, 0)),
        compiler_params=_cparams("parallel", "arbitrary"),
        name="hyena_dft_fwd",
    )(cm, sf, vxb, a, b, d)
    tile = pl.BlockSpec((None, tf, HY_WIDTH), lambda bb, f: (bb, f, 0))
    return pl.pallas_call(
        _hy_inv_kernel,
        out_shape=jax.ShapeDtypeStruct((BATCH, n_len, HY_WIDTH), BF16),
        grid=(BATCH, nf),
        in_specs=[mat, mat, pl.BlockSpec((None, 2, n_len, HY_WIDTH), lambda bb, f: (bb, 0, 0, 0)),
                  tile, tile, pl.BlockSpec((1, HY_WIDTH), lambda bb, f: (0, 0))],
        out_specs=tile,
        compiler_params=_cparams("parallel", "arbitrary"),
        name="hyena_dft_inv",
    )(cm, si, q, x0, vx, skip.reshape(1, -1))


_SWA_Q = SWA_HEADS * SWA_HEAD_DIM
_SWA_KV = SWA_KV_HEADS * SWA_HEAD_DIM


def _swaproj_kernel(h_ref, w_ref, cos_ref, sin_ref, q_ref, k_ref, v_ref, *, rope):
    t = _dot(h_ref[...], w_ref[...])
    q = t[:, :_SWA_Q]
    k = t[:, 2 * _SWA_Q:2 * _SWA_Q + _SWA_KV]
    v = t[:, 2 * _SWA_Q + 2 * _SWA_KV:]
    if rope:
        cos = cos_ref[...]
        sin = sin_ref[...]
        q = q * cos + t[:, _SWA_Q:2 * _SWA_Q] * sin
        k = k * cos[:, :_SWA_KV] + t[:, 2 * _SWA_Q + _SWA_KV:2 * _SWA_Q + 2 * _SWA_KV] * sin[:, :_SWA_KV]
    q_ref[...] = (q * (SWA_HEAD_DIM ** -0.5)).astype(BF16)
    lo = lax.broadcasted_iota(jnp.int32, k.shape, 1) < SWA_HEAD_DIM
    for src, dst in ((k, k_ref), (v, v_ref)):
        sw = pltpu.roll(src, SWA_HEAD_DIM, axis=1)
        dst[:, 0 * LANES:1 * LANES] = jnp.where(lo, src, 0.0).astype(BF16)
        dst[:, 1 * LANES:2 * LANES] = jnp.where(lo, 0.0, sw).astype(BF16)
        dst[:, 2 * LANES:3 * LANES] = jnp.where(lo, sw, 0.0).astype(BF16)
        dst[:, 3 * LANES:4 * LANES] = jnp.where(lo, 0.0, src).astype(BF16)


def _swa_proj(h, w, cos_t, sin_t, rope, tm=512):
    m = h.shape[0]
    nt = cos_t.shape[0] // tm
    o = jax.ShapeDtypeStruct((m, 4 * LANES), BF16)
    ospec = pl.BlockSpec((tm, 4 * LANES), lambda i: (i, 0))
    tspec = pl.BlockSpec((tm, _SWA_Q), lambda i: (i % nt, 0))
    return pl.pallas_call(
        functools.partial(_swaproj_kernel, rope=rope),
        out_shape=(o, o, o),
        grid=(m // tm,),
        in_specs=[pl.BlockSpec((tm, D_MODEL), lambda i: (i, 0)), _const_spec(w.shape), tspec, tspec],
        out_specs=(ospec, ospec, ospec),
        compiler_params=_cparams("parallel"),
        name="swa_proj",
    )(h, w, cos_t, sin_t)


def _swa_kernel(sink_ref, q_ref, *refs, tq, win, lk, has_lat):
    if has_lat:
        kl_ref, vl_ref, kc_ref, vc_ref, o_ref = refs
        t0 = pl.program_id(1) * tq
        start = pl.multiple_of(jnp.clip(t0 - SWA_WINDOW, 0, lk - win), SWA_WINDOW)
        qpos = t0 + lax.broadcasted_iota(jnp.int32, (tq, win), 0)
        kpos = start + lax.broadcasted_iota(jnp.int32, (tq, win), 1)
        valid = jnp.abs(qpos - kpos) <= SWA_WINDOW
    else:
        kc_ref, vc_ref, o_ref = refs
    for j in range(SWA_HEADS // 2):
        g = j // (SWA_HEADS // SWA_KV_HEADS // 2)
        qb = q_ref[:, j * LANES:(j + 1) * LANES]
        acc = None
        for par in range(2):
            c0 = (2 * g + par) * LANES
            sk = sink_ref[2 * j + par]
            s_c = _dot_nt(qb, kc_ref[:, c0:c0 + LANES])
            m = jnp.maximum(s_c.max(axis=-1, keepdims=True), sk)
            if has_lat:
                s_l = _dot_nt(qb, kl_ref[pl.ds(start, win), c0:c0 + LANES])
                s_l = jnp.where(valid, s_l, NEG_BIG)
                m = jnp.maximum(m, s_l.max(axis=-1, keepdims=True))
            p_c = jnp.exp(s_c - m)
            den = p_c.sum(axis=-1, keepdims=True) + jnp.exp(sk - m)
            if has_lat:
                p_l = jnp.exp(s_l - m)
                den = den + p_l.sum(axis=-1, keepdims=True)
            o = _dot(p_c.astype(BF16), vc_ref[:, c0:c0 + LANES])
            if has_lat:
                o = o + _dot(p_l.astype(BF16), vl_ref[pl.ds(start, win), c0:c0 + LANES])
            o = o * (1.0 / den)
            acc = o if acc is None else acc + o
        o_ref[:, j * LANES:(j + 1) * LANES] = acc.astype(o_ref.dtype)


def _swa_attention(sink, q, lat, ctx, lq, tq=128):
    nq = lq // tq
    win = tq + 2 * SWA_WINDOW
    full = lambda n: pl.BlockSpec((n, 4 * LANES), lambda b, i: (b, 0))
    in_specs = [pl.BlockSpec(memory_space=pltpu.SMEM),
                pl.BlockSpec((tq, 4 * LANES), lambda b, i: (b * nq + i, 0))]
    args = [sink, q]
    if lat is not None:
        in_specs += [full(SEQ), full(SEQ)]
        args += list(lat)
    in_specs += [full(CTX_LEN), full(CTX_LEN)]
    args += list(ctx)
    return pl.pallas_call(
        functools.partial(_swa_kernel, tq=tq, win=win, lk=SEQ, has_lat=lat is not None),
        out_shape=jax.ShapeDtypeStruct((BATCH * lq, 4 * LANES), BF16),
        grid=(BATCH, nq),
        in_specs=in_specs,
        out_specs=pl.BlockSpec((tq, 4 * LANES), lambda b, i: (b * nq + i, 0)),
        compiler_params=_cparams("parallel", "arbitrary"),
        name="swa_attention",
    )(*args)


def _out_kernel(a_ref, y_ref, s_ref, w_ref, x_ref, m_ref, g_ref, xo_ref, ho_ref):
    na = MLA_HEADS * MLA_V
    acc = _dot(a_ref[...], w_ref[0:na, :])
    acc = acc + _dot(y_ref[...], w_ref[na:na + HY_WIDTH, :])
    acc = acc + _dot(s_ref[...], w_ref[na + HY_WIDTH:, :])
    x = x_ref[...] + m_ref[2:3, :] * acc
    xo_ref[...] = x
    ho_ref[...] = (_rms(x, g_ref[...]) * (1.0 + m_ref[4:5, :]) + m_ref[3:4, :]).astype(ho_ref.dtype)


def _out_proj(a, y, s, w_out, x, mods, g_ffn, groups, row0, h_dtype, tm=512):
    m = x.shape[0]
    nr = m // groups // tm
    tile = lambda n: pl.BlockSpec((tm, n), lambda g_, i: (g_ * nr + i, 0))
    return pl.pallas_call(
        _out_kernel,
        out_shape=(jax.ShapeDtypeStruct((m, D_MODEL), F32), jax.ShapeDtypeStruct((m, D_MODEL), h_dtype)),
        grid=(groups, nr),
        in_specs=[tile(a.shape[1]), tile(y.shape[1]), tile(s.shape[1]), _const_spec(w_out.shape),
                  tile(D_MODEL), pl.BlockSpec((None, 6, D_MODEL), lambda g_, i: (row0 + g_, 0, 0)),
                  pl.BlockSpec((1, D_MODEL), lambda g_, i: (0, 0))],
        out_specs=(tile(D_MODEL), tile(D_MODEL)),
        compiler_params=_cparams("parallel", "parallel"),
        name="out_proj",
    )(a, y, s, w_out, x, mods, g_ffn.reshape(1, -1))


_META_W1, _META_W2, _META_E1, _META_E2, _META_R1, _META_R2 = range(6)


def _router_kernel(h_ref, w_ref, meta_ref, cnt_ref, carry_ref):
    @pl.when(pl.program_id(0) == 0)
    def _():
        carry_ref[...] = jnp.zeros_like(carry_ref)

    logits = _dot3(h_ref[...], w_ref[...])
    tm = logits.shape[0]
    lane = lax.broadcasted_iota(jnp.int32, logits.shape, 1).astype(F32)
    logits = jnp.where(lane < N_EXPERTS, logits, NEG_BIG)
    m1 = logits.max(axis=-1, keepdims=True)
    i1 = jnp.where(logits == m1, lane, float(LANES)).min(axis=-1, keepdims=True)
    rest = jnp.where(lane == i1, NEG_BIG, logits)
    m2 = rest.max(axis=-1, keepdims=True)
    i2 = jnp.where(rest == m2, lane, float(LANES)).min(axis=-1, keepdims=True)
    e2 = jnp.exp(m2 - m1)
    w1 = 1.0 / (1.0 + e2)
    hot = jnp.where((lane == i1) | (lane == i2), 1.0, 0.0)
    before = (lax.broadcasted_iota(jnp.int32, (tm, tm), 0) > lax.broadcasted_iota(jnp.int32, (tm, tm), 1))
    seen = _dot(before.astype(F32).astype(BF16), hot.astype(BF16)) + carry_ref[0:1, :]
    r1 = jnp.where(lane == i1, seen, 0.0).sum(axis=-1, keepdims=True)
    r2 = jnp.where(lane == i2, seen, 0.0).sum(axis=-1, keepdims=True)
    carry_ref[...] = carry_ref[...] + hot.sum(axis=0, keepdims=True)
    cnt_ref[...] = carry_ref[...]
    rec = jnp.zeros_like(logits)
    for k, v in ((_META_W1, w1), (_META_W2, e2 * w1), (_META_E1, i1), (_META_E2, i2), (_META_R1, r1), (_META_R2, r2)):
        rec = jnp.where(lane == float(k), v, rec)
    meta_ref[...] = rec


def _router(h, w_router, tm=512):
    m = h.shape[0]
    wp = jnp.zeros((D_MODEL, LANES), F32).at[:, :N_EXPERTS].set(w_router)
    return pl.pallas_call(
        _router_kernel,
        out_shape=(jax.ShapeDtypeStruct((m, LANES), F32), jax.ShapeDtypeStruct((8, LANES), F32)),
        grid=(m // tm,),
        in_specs=[pl.BlockSpec((tm, D_MODEL), lambda i: (i, 0)),
                  pl.BlockSpec((D_MODEL, LANES), lambda i: (0, 0))],
        out_specs=(pl.BlockSpec((tm, LANES), lambda i: (i, 0)), pl.BlockSpec((8, LANES), lambda i: (0, 0))),
        scratch_shapes=[pltpu.VMEM((8, LANES), F32)],
        compiler_params=_cparams("arbitrary"),
        name="moe_router",
    )(h, wp)


MOE_TM = 1024
MOE_SUB = 512


def _moe_plan(meta, counts, m):
    n_steps = 2 * m // MOE_TM + N_EXPERTS
    cnt = counts[0, :N_EXPERTS].astype(jnp.int32)
    steps_e = (cnt + MOE_TM - 1) // MOE_TM
    ends = jnp.cumsum(steps_e)
    first = ends - steps_e
    total = ends[-1]
    e = meta[:, _META_E1:_META_E2 + 1].astype(jnp.int32)
    rank = meta[:, _META_R1:_META_R2 + 1].astype(jnp.int32)
    pos = (first * MOE_TM)[e] + rank
    s_idx = jnp.arange(n_steps, dtype=jnp.int32)
    step_e = jnp.minimum(jnp.searchsorted(ends, s_idx, side="right").astype(jnp.int32), N_EXPERTS - 1)
    valid = jnp.clip(cnt[step_e] - (s_idx - first[step_e]) * MOE_TM, 0, MOE_TM)
    valid = jnp.where(s_idx < total, valid, 0)
    return pos.reshape(-1), step_e, valid, n_steps


def _row_copy(src_ref, src_row, dst_ref, dst_row, sem):
    return pltpu.make_async_copy(src_ref.at[pl.ds(src_row, 1)], dst_ref.at[pl.ds(dst_row, 1)], sem)


def _invert_kernel(pos_ref, sv_ref, src_ref):
    def clear_step(s, c):
        def clear(i, c2):
            src_ref[s * MOE_TM + i] = 0
            return c2

        return lax.fori_loop(sv_ref[s], MOE_TM, clear, c)

    def place(i, c):
        src_ref[pos_ref[i]] = i // 2
        return c

    lax.fori_loop(0, sv_ref.shape[0], clear_step, 0)
    lax.fori_loop(0, pos_ref.shape[0], place, 0, unroll=16)


def _moe_invert(pos, valid, n_rows):
    smem = pl.BlockSpec(memory_space=pltpu.SMEM)
    return pl.pallas_call(
        _invert_kernel,
        out_shape=jax.ShapeDtypeStruct((n_rows,), jnp.int32),
        in_specs=[smem, smem], out_specs=smem,
        name="moe_invert",
    )(pos, valid)


def _block_wait(src_ref, dst_ref, rows, sem):
    pltpu.make_async_copy(src_ref.at[pl.ds(0, rows)], dst_ref.at[pl.ds(0, rows)], sem).wait()


def _moe_ffn_kernel(se_ref, sv_ref, src_ref, h_ref, wg_ref, wu_ref, wd_ref, o_ref, x32_ref, xb_ref, sem):
    del se_ref
    s = pl.program_id(0)
    f = pl.program_id(1)
    valid = sv_ref[s]

    def gather(step):
        base = step * MOE_TM
        rows = (sv_ref[step] + MOE_SUB - 1) // MOE_SUB * MOE_SUB

        def issue(i, c):
            _row_copy(h_ref, src_ref[base + i], x32_ref, i, sem).start()
            return c

        lax.fori_loop(0, rows, issue, 0)

    @pl.when(f == 0)
    def _():
        @pl.when(s == 0)
        def _():
            gather(0)

        for r0 in range(0, MOE_TM, MOE_SUB):
            @pl.when(valid > r0)
            def _():
                _block_wait(h_ref, x32_ref, MOE_SUB, sem)

        for r0 in range(0, MOE_TM, MOE_SUB):
            @pl.when(valid > r0)
            def _():
                xb_ref[r0:r0 + MOE_SUB, :] = x32_ref[r0:r0 + MOE_SUB, :].astype(BF16)

        @pl.when(s + 1 < pl.num_programs(0))
        def _():
            gather(s + 1)

        o_ref[...] = jnp.zeros_like(o_ref)

    def compute(rows):
        xs = xb_ref[0:rows, :]
        mid = _silu(_dot(xs, wg_ref[...].astype(BF16))) * _dot(xs, wu_ref[...].astype(BF16))
        o_ref[0:rows, :] += _dot(mid.astype(BF16), wd_ref[...].astype(BF16))

    @pl.when(valid > MOE_SUB)
    def _():
        compute(MOE_TM)

    @pl.when((valid > 0) & (valid <= MOE_SUB))
    def _():
        compute(MOE_SUB)


def _moe_ffn(h, src, w_gate, w_up, w_down, step_e, valid, n_steps, tf=512):
    assert MOE_TM == 2 * MOE_SUB
    n_ff = w_gate.shape[-1]
    nf = n_ff // tf
    fidx = lambda s, f, sv: jnp.where(sv[s] > 0, f, nf - 1)
    return pl.pallas_call(
        _moe_ffn_kernel,
        out_shape=jax.ShapeDtypeStruct((n_steps * MOE_TM, D_MODEL), F32),
        grid_spec=pltpu.PrefetchScalarGridSpec(
            num_scalar_prefetch=3, grid=(n_steps, nf),
            in_specs=[
                pl.BlockSpec(memory_space=pl.ANY),
                pl.BlockSpec((None, D_MODEL, tf), lambda s, f, se, sv, sr: (se[s], 0, fidx(s, f, sv))),
                pl.BlockSpec((None, D_MODEL, tf), lambda s, f, se, sv, sr: (se[s], 0, fidx(s, f, sv))),
                pl.BlockSpec((None, tf, D_MODEL), lambda s, f, se, sv, sr: (se[s], fidx(s, f, sv), 0)),
            ],
            out_specs=pl.BlockSpec((MOE_TM, D_MODEL), lambda s, f, se, sv, sr: (s, 0), pipeline_mode=pl.Buffered(1)),
            scratch_shapes=[pltpu.VMEM((MOE_TM, D_MODEL), F32), pltpu.VMEM((MOE_TM, D_MODEL), BF16),
                            pltpu.SemaphoreType.DMA(())]),
        compiler_params=_cparams("arbitrary", "arbitrary"),
        name="moe_experts",
    )(step_e, valid, src, h, w_gate, w_up, w_down)


def _combine_kernel(pos_ref, meta_ref, x_ref, m_ref, gf_ref, ys_ref, o_ref, buf_ref, sem, *, tm, final):
    base = pl.program_id(0) * tm

    def issue(i, c):
        for slot in range(2):
            _row_copy(ys_ref, pos_ref[2 * (base + i) + slot], buf_ref.at[slot], i, sem).start()
        return c

    lax.fori_loop(0, tm, issue, 0, unroll=4)
    for slot in range(2):
        _block_wait(ys_ref, buf_ref.at[slot], tm, sem)
    meta = meta_ref[...]
    y = meta[:, _META_W1:_META_W1 + 1] * buf_ref[0] + meta[:, _META_W2:_META_W2 + 1] * buf_ref[1]
    x = x_ref[...] + m_ref[5:6, :] * y
    if final:
        x = _rms(x, gf_ref[...])
    o_ref[...] = x


def _moe_combine(pos, meta, x, mods, ys, groups, row0, g_final, tm=512):
    m = x.shape[0]
    nr = m // groups // tm
    final = g_final is not None
    gf = (g_final if final else jnp.ones((D_MODEL,), F32)).reshape(1, D_MODEL)
    return pl.pallas_call(
        functools.partial(_combine_kernel, tm=tm, final=final),
        out_shape=jax.ShapeDtypeStruct((m, D_MODEL), F32),
        grid_spec=pltpu.PrefetchScalarGridSpec(
            num_scalar_prefetch=1, grid=(m // tm,),
            in_specs=[pl.BlockSpec((tm, LANES), lambda i, p: (i, 0)),
                      pl.BlockSpec((tm, D_MODEL), lambda i, p: (i, 0)),
                      pl.BlockSpec((None, 6, D_MODEL), lambda i, p: (row0 + i // nr, 0, 0)),
                      pl.BlockSpec((1, D_MODEL), lambda i, p: (0, 0)),
                      pl.BlockSpec(memory_space=pl.ANY)],
            out_specs=pl.BlockSpec((tm, D_MODEL), lambda i, p: (i, 0)),
            scratch_shapes=[pltpu.VMEM((2, tm, D_MODEL), F32), pltpu.SemaphoreType.DMA(())]),
        compiler_params=_cparams("arbitrary"),
        name="moe_combine",
    )(pos, meta, x, mods, gf, ys)


def _moe(h, x, mods, groups, row0, w_router, w_gate, w_up, w_down, g_final):
    m = h.shape[0]
    meta, counts = _router(h, w_router)
    pos, step_e, valid, n_steps = _moe_plan(meta, counts, m)
    src = _moe_invert(pos, valid, n_steps * MOE_TM)
    ys = _moe_ffn(h, src, w_gate, w_up, w_down, step_e, valid, n_steps)
    return _moe_combine(pos, meta, x, mods, ys, groups, row0, g_final)


def _ffn_kernel(h_ref, wg_ref, wu_ref, wd_ref, x_ref, m_ref, o_ref, acc_ref):
    f = pl.program_id(2)

    @pl.when(f == 0)
    def _():
        acc_ref[...] = jnp.zeros_like(acc_ref)

    h = h_ref[...]
    mid = _silu(_dot(h, wg_ref[...])) * _dot(h, wu_ref[...])
    acc_ref[...] += _dot(mid.astype(BF16), wd_ref[...])

    @pl.when(f == pl.num_programs(2) - 1)
    def _():
        o_ref[...] = x_ref[...] + m_ref[5:6, :] * acc_ref[...]


def _ffn(h, w_gate, w_up, w_down, x, mods, groups, row0, tm=512, tf=512):
    m = x.shape[0]
    n_ff = w_gate.shape[1]
    nr = m // groups // tm
    tile = lambda n: pl.BlockSpec((tm, n), lambda g_, i, f: (g_ * nr + i, 0))
    return pl.pallas_call(
        _ffn_kernel,
        out_shape=jax.ShapeDtypeStruct((m, D_MODEL), F32),
        grid=(groups, nr, n_ff // tf),
        in_specs=[tile(D_MODEL),
                  pl.BlockSpec((D_MODEL, tf), lambda g_, i, f: (0, f)),
                  pl.BlockSpec((D_MODEL, tf), lambda g_, i, f: (0, f)),
                  pl.BlockSpec((tf, D_MODEL), lambda g_, i, f: (f, 0)),
                  tile(D_MODEL),
                  pl.BlockSpec((None, 6, D_MODEL), lambda g_, i, f: (row0 + g_, 0, 0))],
        out_specs=tile(D_MODEL),
        scratch_shapes=[pltpu.VMEM((tm, D_MODEL), F32)],
        compiler_params=_cparams("parallel", "parallel", "arbitrary"),
        name="swiglu_ffn",
    )(h, w_gate, w_up, w_down, x, mods)


def _swap_halves(w):
    half = w.shape[-1] // 2
    return jnp.concatenate([w[..., half:], w[..., :half]], axis=-1)


def _swap_heads(w, heads, dim):
    k = w.shape[0]
    return _swap_halves(w.reshape(k, heads, dim)).reshape(k, heads * dim)


def _rope_tables():
    rows = SEQ // GRID_W
    row = np.repeat(np.arange(rows), GRID_W).astype(np.float32)
    col = np.tile(np.arange(GRID_W), rows).astype(np.float32)
    quarter = MLA_ROPE // 4
    freqs = (np.float32(ROPE_THETA) ** (-np.arange(quarter, dtype=np.float32) / quarter)).astype(np.float32)
    ang = np.concatenate([row[:, None] * freqs[None], col[:, None] * freqs[None]], axis=-1)
    cos = np.cos(ang.astype(np.float64))
    sin = np.sin(ang.astype(np.float64))
    cos2 = np.concatenate([cos, cos], axis=-1)
    sin2 = np.concatenate([-sin, sin], axis=-1)
    tab_lat = np.concatenate([cos2, sin2], axis=-1)
    tab_ctx = np.concatenate([np.ones((CTX_LEN, MLA_ROPE)), np.zeros((CTX_LEN, MLA_ROPE))], axis=-1)
    return dict(
        tab_lat=jnp.asarray(tab_lat, F32), tab_ctx=jnp.asarray(tab_ctx, F32),
        cos8=jnp.asarray(np.tile(cos2, (1, SWA_HEADS)), F32),
        sin8=jnp.asarray(np.tile(sin2, (1, SWA_HEADS)), F32),
        ones8=jnp.ones((CTX_LEN, _SWA_Q), F32), zeros8=jnp.zeros((CTX_LEN, _SWA_Q), F32),
    )


def _layer_weights(p, l):
    w_in = p["w_in"][l]
    cq = w_in[:, _O_CQ:_O_CKV]
    ckv = w_in[:, _O_CKV:_O_KPE]
    kpe = w_in[:, _O_KPE:_O_HY]
    hy = w_in[:, _O_HY:_O_SQ]
    sq = w_in[:, _O_SQ:_O_SK]
    sk = w_in[:, _O_SK:_O_SV]
    sv = w_in[:, _O_SV:]
    dq = MLA_NOPE + MLA_ROPE
    wq = p["w_q_up"][l].reshape(MLA_Q_LORA, MLA_HEADS, dq)
    wq = jnp.concatenate([wq, _swap_halves(wq[..., MLA_NOPE:])], axis=-1)
    wkv = p["w_kv_up"][l].reshape(MLA_KV_LORA, MLA_HEADS, MLA_NOPE + MLA_V)
    return dict(
        w_cq=cq.astype(BF16),
        w_ckv=jnp.concatenate([ckv, kpe, _swap_halves(kpe)], axis=-1).astype(BF16),
        w_hy=hy.astype(BF16),
        w_swa=jnp.concatenate([sq, _swap_heads(sq, SWA_HEADS, SWA_HEAD_DIM), sk,
                               _swap_heads(sk, SWA_KV_HEADS, SWA_HEAD_DIM), sv], axis=-1).astype(BF16),
        w_q=jnp.transpose(wq, (1, 0, 2)).astype(BF16),
        w_kv=jnp.transpose(wkv, (1, 0, 2)).astype(BF16),
        w_out=p["w_out"][l].astype(BF16),
        g_q=p["g_q"][l], g_kv=p["g_kv"][l],
        hy_conv_w=p["hy_conv_w"][l], hy_conv_b=p["hy_conv_b"][l],
        hy_w1=p["hy_w1"][l], hy_b1=p["hy_b1"][l], hy_w2=p["hy_w2"][l], hy_b2=p["hy_b2"][l],
        hy_w3=p["hy_w3"][l], hy_b3=p["hy_b3"][l], hy_w_filt=p["hy_w_filt"][l],
        hy_freq=p["hy_freq"][l], hy_skip=p["hy_skip"][l], swa_sink=p["swa_sink"][l],
    )


def _hyena(h, lw, n_len, tabs):
    hy = _matmul(h, lw["w_hy"], F32)
    x0, vx, vxb = _hy_pre(hy, lw["hy_conv_w"], lw["hy_conv_b"], n_len)
    spectra = _hy_filters(lw, n_len, tabs)
    y = _hy_long_conv(x0, vx, vxb, spectra, lw["hy_skip"], n_len, tabs)
    return y.reshape(BATCH * n_len, HY_WIDTH)


def _mixer(h_lat, h_ctx, lw, rope, hy_tabs, need_ctx):
    k_l, v_l = _kv_proj(h_lat, lw["w_ckv"], lw["g_kv"], lw["w_kv"], rope["tab_lat"])
    k_c, v_c = _kv_proj(h_ctx, lw["w_ckv"], lw["g_kv"], lw["w_kv"], rope["tab_ctx"], tm=CTX_LEN)
    q_l = _q_proj(h_lat, lw["w_cq"], lw["g_q"], lw["w_q"], rope["tab_lat"])
    a_l = _mla_attention(q_l, [(k_l, v_l, SEQ), (k_c, v_c, CTX_LEN)], SEQ)
    y_l = _hyena(h_lat, lw, SEQ, hy_tabs[SEQ])
    sq_l, kk_l, vv_l = _swa_proj(h_lat, lw["w_swa"], rope["cos8"], rope["sin8"], True)
    sq_c, kk_c, vv_c = _swa_proj(h_ctx, lw["w_swa"], rope["ones8"], rope["zeros8"], False, tm=CTX_LEN)
    s_l = _swa_attention(lw["swa_sink"], sq_l, (kk_l, vv_l), (kk_c, vv_c), SEQ)
    if not need_ctx:
        return (a_l, y_l, s_l), None
    q_c = _q_proj(h_ctx, lw["w_cq"], lw["g_q"], lw["w_q"], rope["tab_ctx"], tm=CTX_LEN)
    a_c = _mla_attention(q_c, [(k_c, v_c, CTX_LEN)], CTX_LEN, tq=CTX_LEN)
    y_c = _hyena(h_ctx, lw, CTX_LEN, hy_tabs[CTX_LEN])
    s_c = _swa_attention(lw["swa_sink"], sq_c, None, (kk_c, vv_c), CTX_LEN, tq=CTX_LEN)
    return (a_l, y_l, s_l), (a_c, y_c, s_c)


def _forward(p):
    rope = _rope_tables()
    hy_tabs = {n: dict(dft=_dft_tables(n), filt=_filter_tables(n)) for n in (SEQ, CTX_LEN)}
    cvec = jnp.zeros((MOD_ROWS, D_MODEL), F32).at[:BATCH].set(p["c"]).at[BATCH].set(p["c_ctx"])
    mods_all = _modulation(cvec, p["w_mod"], p["b_mod"]).reshape(DEPTH, MOD_ROWS, 6, D_MODEL)

    x_lat = p["x"].reshape(BATCH * SEQ, D_MODEL)
    x_ctx = p["ctx"].reshape(BATCH * CTX_LEN, D_MODEL)
    for l in range(DEPTH):
        last = l == DEPTH - 1
        mods = mods_all[l]
        lw = _layer_weights(p, l)
        h_lat = _norm_mod(x_lat, p["g_mix"][l], mods, BATCH, 0, 0, BF16)
        h_ctx = _norm_mod(x_ctx, p["g_mix"][l], mods, 1, BATCH, 0, BF16)
        mix_l, mix_c = _mixer(h_lat, h_ctx, lw, rope, hy_tabs, not last)
        streams = [(x_lat, mix_l, BATCH, 0)]
        if not last:
            streams.append((x_ctx, mix_c, 1, BATCH))
        outs = []
        i = l // 2
        dense = l % 2 == 0
        for n, (x, mix, groups, row0) in enumerate(streams):
            x, h2 = _out_proj(*mix, lw["w_out"], x, mods, p["g_ffn"][l], groups, row0, BF16 if dense else F32)
            if dense:
                x = _ffn(h2, p["ffn_w_gate"][i].astype(BF16), p["ffn_w_up"][i].astype(BF16),
                         p["ffn_w_down"][i].astype(BF16), x, mods, groups, row0)
            else:
                g_final = p["g_final"] if (last and n == 0) else None
                x = _moe(h2, x, mods, groups, row0, p["moe_router"][i], p["moe_w_gate"][i], p["moe_w_up"][i],
                         p["moe_w_down"][i], g_final)
            outs.append(x)
        x_lat = outs[0]
        if not last:
            x_ctx = outs[1]
    if DEPTH % 2 == 1:
        x_lat = _norm_mod(x_lat, p["g_final"], mods_all[0], BATCH, 0, None, F32)
    return x_lat.reshape(BATCH, SEQ, D_MODEL)


def kernel(x, c, ctx, c_ctx, w_mod, b_mod, g_mix, g_ffn, w_in, g_q, w_q_up, g_kv, w_kv_up, hy_conv_w, hy_conv_b, hy_w1, hy_b1, hy_w2, hy_b2, hy_w3, hy_b3, hy_w_filt, hy_freq, hy_skip, swa_sink, w_out, ffn_w_gate, ffn_w_up, ffn_w_down, moe_router, moe_w_gate, moe_w_up, moe_w_down, g_final):
    return _forward(dict(
        x=x, c=c, ctx=ctx, c_ctx=c_ctx, w_mod=w_mod, b_mod=b_mod, g_mix=g_mix, g_ffn=g_ffn, w_in=w_in,
        g_q=g_q, w_q_up=w_q_up, g_kv=g_kv, w_kv_up=w_kv_up, hy_conv_w=hy_conv_w, hy_conv_b=hy_conv_b,
        hy_w1=hy_w1, hy_b1=hy_b1, hy_w2=hy_w2, hy_b2=hy_b2, hy_w3=hy_w3, hy_b3=hy_b3,
        hy_w_filt=hy_w_filt, hy_freq=hy_freq, hy_skip=hy_skip, swa_sink=swa_sink, w_out=w_out,
        ffn_w_gate=ffn_w_gate, ffn_w_up=ffn_w_up, ffn_w_down=ffn_w_down, moe_router=moe_router,
        moe_w_gate=moe_w_gate, moe_w_up=moe_w_up, moe_w_down=moe_w_down, g_final=g_final))
```

```python
import functools
import math

import numpy as np
import jax
import jax.numpy as jnp
from jax import lax
from jax.experimental import pallas as pl
from jax.experimental.pallas import tpu as pltpu

F32 = jnp.float32
BF16 = jnp.bfloat16

D_MODEL = 2048
BATCH = 4
SEQ = 2048
DEPTH = 2
CTX_LEN = 256
GRID_W = 64
NORM_EPS = 1e-6
ROPE_THETA = 10000.0
MLA_HEADS = 8
MLA_NOPE = 128
MLA_ROPE = 64
MLA_V = 128
MLA_Q_LORA = 768
MLA_KV_LORA = 512
HY_WIDTH = 512
HY_BANDS = 16
HY_EMB = 1 + 2 * HY_BANDS
HY_FILTER_HIDDEN = 64
HY_DECAY_TARGET = 1e-2
HY_QUICK_DECAY_PCT = 0.3
HY_SLOW_DECAY_PCT = 1.5
SWA_HEADS = 8
SWA_KV_HEADS = 2
SWA_HEAD_DIM = 64
SWA_WINDOW = 128
N_EXPERTS = 8
D_FF = 5632
D_FF_EXPERT = 7168

LANES = 128
MXU_DIM = 256
VMEM_LIMIT_BYTES = 56 * 1024 * 1024
NEG_BIG = -1e30
MOD_ROWS = 8

_O_CQ = 0
_O_CKV = _O_CQ + MLA_Q_LORA
_O_KPE = _O_CKV + MLA_KV_LORA
_O_HY = _O_KPE + MLA_ROPE
_O_SQ = _O_HY + 3 * HY_WIDTH
_O_SK = _O_SQ + SWA_HEADS * SWA_HEAD_DIM
_O_SV = _O_SK + SWA_KV_HEADS * SWA_HEAD_DIM


def _cparams(*sem):
    return pltpu.CompilerParams(dimension_semantics=sem, vmem_limit_bytes=VMEM_LIMIT_BYTES)


def _dot(a, b):
    return jnp.dot(a, b, preferred_element_type=F32)


def _dot_nt(a, b):
    return lax.dot_general(a, b, (((1,), (1,)), ((), ())), preferred_element_type=F32)


def _split_bf16(a):
    hi = a.astype(BF16)
    lo = (a - hi.astype(F32)).astype(BF16)
    return hi, lo


def _dot3(a, b):
    ah, al = _split_bf16(a)
    bh, bl = _split_bf16(b)
    return _dot(ah, bh) + (_dot(al, bh) + _dot(ah, bl))


def _silu(x):
    return x / (1.0 + jnp.exp(-x))


def _rms(x, g):
    ms = jnp.mean(x * x, axis=-1, keepdims=True)
    return x * lax.rsqrt(ms + NORM_EPS) * g


def _const_spec(shape):
    nd = len(shape)
    return pl.BlockSpec(shape, lambda *_: (0,) * nd, pipeline_mode=pl.Buffered(1))


def _mod_kernel(c_ref, w_ref, b_ref, o_ref):
    a = _silu(c_ref[...])
    o_ref[0] = _dot3(a, w_ref[0]) + b_ref[0]


def _modulation(cvec, w_mod, b_mod):
    tn = 1024
    n = 6 * D_MODEL
    return pl.pallas_call(
        _mod_kernel,
        out_shape=jax.ShapeDtypeStruct((DEPTH, MOD_ROWS, n), F32),
        grid=(DEPTH, n // tn),
        in_specs=[
            pl.BlockSpec((MOD_ROWS, D_MODEL), lambda l, j: (0, 0)),
            pl.BlockSpec((1, D_MODEL, tn), lambda l, j: (l, 0, j)),
            pl.BlockSpec((1, 1, tn), lambda l, j: (l, 0, j)),
        ],
        out_specs=pl.BlockSpec((1, MOD_ROWS, tn), lambda l, j: (l, 0, j)),
        compiler_params=_cparams("arbitrary", "arbitrary"),
        name="adaln_mod",
    )(cvec, w_mod, b_mod.reshape(DEPTH, 1, n))


def _norm_kernel(x_ref, g_ref, m_ref, o_ref, *, si):
    y = _rms(x_ref[...], g_ref[...])
    if si is not None:
        y = y * (1.0 + m_ref[si + 1:si + 2, :]) + m_ref[si:si + 1, :]
    o_ref[...] = y.astype(o_ref.dtype)


def _norm_mod(x, g, mods, groups, row0, si, out_dtype, tm=512):
    m = x.shape[0]
    nr = m // groups // tm
    return pl.pallas_call(
        functools.partial(_norm_kernel, si=si),
        out_shape=jax.ShapeDtypeStruct((m, D_MODEL), out_dtype),
        grid=(groups, nr),
        in_specs=[
            pl.BlockSpec((tm, D_MODEL), lambda g_, i: (g_ * nr + i, 0)),
            pl.BlockSpec((1, D_MODEL), lambda g_, i: (0, 0)),
            pl.BlockSpec((None, 6, D_MODEL), lambda g_, i: (row0 + g_, 0, 0)),
        ],
        out_specs=pl.BlockSpec((tm, D_MODEL), lambda g_, i: (g_ * nr + i, 0)),
        compiler_params=_cparams("parallel", "parallel"),
        name="norm_mod",
    )(x, g.reshape(1, D_MODEL), mods)


def _mm_kernel(x_ref, w_ref, o_ref):
    o_ref[...] = _dot(x_ref[...], w_ref[...]).astype(o_ref.dtype)


def _matmul(x, w, out_dtype, tm=512):
    m, k = x.shape
    n = w.shape[1]
    return pl.pallas_call(
        _mm_kernel,
        out_shape=jax.ShapeDtypeStruct((m, n), out_dtype),
        grid=(m // tm,),
        in_specs=[pl.BlockSpec((tm, k), lambda i: (i, 0)), _const_spec((k, n))],
        out_specs=pl.BlockSpec((tm, n), lambda i: (i, 0)),
        compiler_params=_cparams("parallel"),
        name="matmul",
    )(x, w)


def _rope_halves(pe_pair, tab):
    r = pe_pair * tab
    return r + pltpu.roll(r, MLA_ROPE, axis=1)


def _qproj_kernel(h_ref, wc_ref, g_ref, wq_ref, tab_ref, o_ref):
    cq = _dot(h_ref[...], wc_ref[...])
    cqn = _rms(cq, g_ref[...]).astype(BF16)
    tab = tab_ref[...]
    for hh in range(MLA_HEADS):
        r = _dot(cqn, wq_ref[hh])
        o_ref[hh, :, 0:MLA_NOPE] = r[:, :MLA_NOPE].astype(BF16)
        o_ref[hh, :, MLA_NOPE:] = _rope_halves(r[:, MLA_NOPE:], tab).astype(BF16)


def _q_proj(h, w_cq, g_q, w_q, tab, tm=512):
    m = h.shape[0]
    nt = tab.shape[0] // tm
    return pl.pallas_call(
        _qproj_kernel,
        out_shape=jax.ShapeDtypeStruct((MLA_HEADS, m, MXU_DIM), BF16),
        grid=(m // tm,),
        in_specs=[
            pl.BlockSpec((tm, D_MODEL), lambda i: (i, 0)),
            _const_spec(w_cq.shape),
            _const_spec((1, MLA_Q_LORA)),
            _const_spec(w_q.shape),
            pl.BlockSpec((tm, LANES), lambda i: (i % nt, 0)),
        ],
        out_specs=pl.BlockSpec((MLA_HEADS, tm, MXU_DIM), lambda i: (0, i, 0)),
        compiler_params=_cparams("parallel"),
        name="mla_q_proj",
    )(h, w_cq, g_q.reshape(1, -1), w_q, tab)


def _kvproj_kernel(h_ref, wc_ref, g_ref, wkv_ref, tab_ref, k_ref, v_ref):
    t = _dot(h_ref[...], wc_ref[...])
    ckvn = _rms(t[:, :MLA_KV_LORA], g_ref[...]).astype(BF16)
    rot = _rope_halves(t[:, MLA_KV_LORA:], tab_ref[...])
    lane = lax.broadcasted_iota(jnp.int32, rot.shape, 1)
    kpe = jnp.where(lane < MLA_ROPE, rot, 0.0).astype(BF16)
    for hh in range(MLA_HEADS):
        r = _dot(ckvn, wkv_ref[hh])
        k_ref[hh, :, 0:MLA_NOPE] = r[:, :MLA_NOPE].astype(BF16)
        k_ref[hh, :, MLA_NOPE:] = kpe
        v_ref[hh, :, 0:MLA_V] = r[:, MLA_NOPE:].astype(BF16)
        v_ref[hh, :, MLA_V:] = jnp.ones((r.shape[0], MXU_DIM - MLA_V), BF16)


def _kv_proj(h, w_ckv, g_kv, w_kv, tab, tm=512):
    m = h.shape[0]
    nt = tab.shape[0] // tm
    return pl.pallas_call(
        _kvproj_kernel,
        out_shape=(jax.ShapeDtypeStruct((MLA_HEADS, m, MXU_DIM), BF16),
                   jax.ShapeDtypeStruct((MLA_HEADS, m, MXU_DIM), BF16)),
        grid=(m // tm,),
        in_specs=[
            pl.BlockSpec((tm, D_MODEL), lambda i: (i, 0)),
            _const_spec(w_ckv.shape),
            _const_spec((1, MLA_KV_LORA)),
            _const_spec(w_kv.shape),
            pl.BlockSpec((tm, LANES), lambda i: (i % nt, 0)),
        ],
        out_specs=(pl.BlockSpec((MLA_HEADS, tm, MXU_DIM), lambda i: (0, i, 0)),
                   pl.BlockSpec((MLA_HEADS, tm, MXU_DIM), lambda i: (0, i, 0))),
        compiler_params=_cparams("parallel"),
        name="mla_kv_proj",
    )(h, w_ckv, g_kv.reshape(1, -1), w_kv, tab)


MLA_KEY_CHUNK = 512


def _mla_kernel(q_ref, *refs, seg_lens, scale):
    nseg = len(seg_lens)
    o_ref = refs[2 * nseg]
    q = q_ref[...]
    c2 = scale * math.log2(math.e)
    m = jnp.full((q.shape[0], 1), NEG_BIG, F32)
    acc = jnp.zeros((q.shape[0], MXU_DIM), F32)
    for s, lk in enumerate(seg_lens):
        k_ref, v_ref = refs[2 * s], refs[2 * s + 1]
        step = min(MLA_KEY_CHUNK, lk)
        for c0 in range(0, lk, step):
            sc = _dot_nt(q, k_ref[c0:c0 + step, :])
            m_new = jnp.maximum(m, sc.max(axis=-1, keepdims=True))
            p = jnp.exp2((sc - m_new) * c2).astype(BF16)
            acc = acc * jnp.exp2((m - m_new) * c2) + _dot(p, v_ref[c0:c0 + step, :])
            m = m_new
    o_ref[...] = (acc[:, :MLA_V] / acc[:, MLA_V:MLA_V + 1]).astype(o_ref.dtype)


def _mla_attention(q, segs, lq, tq=512):
    nq = lq // tq
    in_specs = [pl.BlockSpec((None, tq, MXU_DIM), lambda b, h, i: (h, b * nq + i, 0))]
    args = [q]
    for k, v, lk in segs:
        in_specs.append(pl.BlockSpec((None, lk, MXU_DIM), lambda b, h, i: (h, b, 0)))
        in_specs.append(pl.BlockSpec((None, lk, MXU_DIM), lambda b, h, i: (h, b, 0)))
        args += [k, v]
    scale = (MLA_NOPE + MLA_ROPE) ** -0.5
    return pl.pallas_call(
        functools.partial(_mla_kernel, seg_lens=tuple(lk for _, _, lk in segs), scale=scale),
        out_shape=jax.ShapeDtypeStruct((BATCH * lq, MLA_HEADS * MLA_V), BF16),
        grid=(BATCH, MLA_HEADS, nq),
        in_specs=in_specs,
        out_specs=pl.BlockSpec((tq, MLA_V), lambda b, h, i: (b * nq + i, h)),
        compiler_params=_cparams("parallel", "parallel", "arbitrary"),
        name="mla_attention",
    )(*args)


def _dft_tables(n_len):
    n2 = 2 * n_len
    idx = np.arange(n_len, dtype=np.int64)
    ang = (2.0 * np.pi / n2) * ((idx[:, None] * idx[None, :]) % n2).astype(np.float64)
    cm = np.cos(ang)
    sf = np.sin(ang)
    sf[0, :] = 1.0 - 2.0 * (idx % 2)
    return (jnp.asarray(cm, dtype=BF16), jnp.asarray(sf, dtype=BF16),
            jnp.asarray(sf.T.copy(), dtype=BF16))


def _filter_tables(n_len):
    pos = np.arange(n_len, dtype=np.float64)
    t = pos / max(n_len - 1, 1)
    bands = np.linspace(1e-4, HY_BANDS - 1, HY_BANDS)
    ang = (2.0 * math.pi / n_len) * pos[:, None] * bands[None]
    z = np.zeros((n_len, LANES), np.float64)
    z[:, 0] = t
    z[:, 1:1 + HY_BANDS] = np.cos(ang)
    z[:, 1 + HY_BANDS:HY_EMB] = -np.sin(ang)
    deltas = np.linspace(math.log(HY_DECAY_TARGET) / HY_SLOW_DECAY_PCT,
                         math.log(HY_DECAY_TARGET) / HY_QUICK_DECAY_PCT, HY_WIDTH)
    decay = np.exp(-t[:, None] * np.abs(deltas)[None])
    return jnp.asarray(z, dtype=F32), jnp.asarray(decay, dtype=F32)


def _hy_filter_kernel(z_ref, w1_ref, b1_ref, w2_ref, b2_ref, w3_ref, b3_ref, wf_ref, wb_ref,
                      fr_ref, dec_ref, cm_ref, sf_ref, a_ref, b_ref, d_ref, hid_ref, *, n_len):
    @pl.when(pl.program_id(0) == 0)
    def _():
        fr = fr_ref[...]
        h = jnp.sin(fr * (_dot3(z_ref[...], w1_ref[...]) + b1_ref[...]))
        h = jnp.sin(fr * (_dot3(h, w2_ref[...]) + b2_ref[...]))
        hid_ref[...] = jnp.sin(fr * (_dot3(h, w3_ref[...]) + b3_ref[...]))

    h = hid_ref[...]
    dec = dec_ref[...]
    h_f = _dot3(h, wf_ref[...]) * dec
    h_b = _dot3(h, wb_ref[...]) * dec
    row = lax.broadcasted_iota(jnp.int32, h_f.shape, 0)
    h_b = jnp.where(row == 0, 0.0, h_b)
    hs = h_f + h_b
    hd = h_f - h_b
    hs_h, hs_l = _split_bf16(hs)
    hd_h, hd_l = _split_bf16(hd)
    cm = cm_ref[...]
    sf = sf_ref[...]
    t_re = _dot(cm, hs_h) + _dot(cm, hs_l)
    g_t = _dot(sf, hd_h) + _dot(sf, hd_l)
    t_ny = (_dot(sf[0:16, :], hs_h) + _dot(sf[0:16, :], hs_l))[0:1, :]
    inv_n = 1.0 / (2 * n_len)
    first = row == 0
    a_ref[...] = jnp.where(first, inv_n, 2.0 * inv_n) * t_re
    b_ref[...] = jnp.where(first, 0.0, -2.0 * inv_n * g_t)
    d_ref[...] = jnp.where(first, inv_n * t_ny, 2.0 * inv_n * t_re)


def _hy_filters(lp, n_len, tabs):
    z, decay = tabs["filt"]
    cm, sf, _ = tabs["dft"]
    nblk = HY_WIDTH // LANES
    w1 = jnp.zeros((LANES, HY_FILTER_HIDDEN), F32).at[:HY_EMB].set(lp["hy_w1"])
    row = lambda a: a.reshape(1, -1)
    small = lambda shape: pl.BlockSpec(shape, lambda j: (0,) * len(shape))
    fh = HY_FILTER_HIDDEN
    out = jax.ShapeDtypeStruct((n_len, HY_WIDTH), F32)
    return pl.pallas_call(
        functools.partial(_hy_filter_kernel, n_len=n_len),
        out_shape=(out, out, out),
        grid=(nblk,),
        in_specs=[
            small((n_len, LANES)), small((LANES, fh)), small((1, fh)), small((fh, fh)), small((1, fh)),
            small((fh, fh)), small((1, fh)),
            pl.BlockSpec((fh, LANES), lambda j: (0, j)),
            pl.BlockSpec((fh, LANES), lambda j: (0, nblk + j)),
            small((1, fh)),
            pl.BlockSpec((n_len, LANES), lambda j: (0, j)),
            _const_spec((n_len, n_len)), _const_spec((n_len, n_len)),
        ],
        out_specs=tuple(pl.BlockSpec((n_len, LANES), lambda j: (0, j)) for _ in range(3)),
        scratch_shapes=[pltpu.VMEM((n_len, fh), F32)],
        compiler_params=_cparams("arbitrary"),
        name="hyena_filters",
    )(z, w1, row(lp["hy_b1"]), lp["hy_w2"], row(lp["hy_b2"]), lp["hy_w3"], row(lp["hy_b3"]),
      lp["hy_w_filt"], lp["hy_w_filt"], row(lp["hy_freq"]), decay, cm, sf)


def _hy_pre_kernel(u0_ref, u1_ref, u2_ref, w0_ref, w1_ref, w2_ref, b0_ref, b1_ref, b2_ref,
                   x0_ref, vx_ref, vxb_ref, *, n_len):
    def sconv(u_ref, w_ref, b_ref):
        u = u_ref[...]
        w = w_ref[...]
        row = lax.broadcasted_iota(jnp.int32, u.shape, 0)
        prev = jnp.where(row == 0, 0.0, pltpu.roll(u, 1, axis=0))
        nxt = jnp.where(row == n_len - 1, 0.0, pltpu.roll(u, n_len - 1, axis=0))
        return w[0:1] * prev + w[1:2] * u + w[2:3] * nxt + b_ref[...]

    x0_ref[...] = sconv(u0_ref, w0_ref, b0_ref)
    vx = sconv(u2_ref, w2_ref, b2_ref) * sconv(u1_ref, w1_ref, b1_ref)
    vx_ref[...] = vx
    vxb_ref[...] = vx.astype(BF16)


def _hy_pre(hy, conv_w, conv_b, n_len):
    nblk = HY_WIDTH // LANES
    uspec = lambda part: pl.BlockSpec((None, n_len, LANES), lambda b, j: (b, 0, part * nblk + j))
    wspec = lambda part: pl.BlockSpec((3, LANES), lambda b, j: (0, part * nblk + j))
    bspec = lambda part: pl.BlockSpec((1, LANES), lambda b, j: (0, part * nblk + j))
    ospec = pl.BlockSpec((None, n_len, LANES), lambda b, j: (b, 0, j))
    hy3 = hy.reshape(BATCH, n_len, 3 * HY_WIDTH)
    cb = conv_b.reshape(1, -1)
    return pl.pallas_call(
        functools.partial(_hy_pre_kernel, n_len=n_len),
        out_shape=(jax.ShapeDtypeStruct((BATCH, n_len, HY_WIDTH), F32),
                   jax.ShapeDtypeStruct((BATCH, n_len, HY_WIDTH), F32),
                   jax.ShapeDtypeStruct((BATCH, n_len, HY_WIDTH), BF16)),
        grid=(BATCH, nblk),
        in_specs=[uspec(0), uspec(1), uspec(2), wspec(0), wspec(1), wspec(2), bspec(0), bspec(1), bspec(2)],
        out_specs=(ospec, ospec, ospec),
        compiler_params=_cparams("parallel", "parallel"),
        name="hyena_short_conv",
    )(hy3, hy3, hy3, conv_w, conv_w, conv_w, cb, cb, cb)


def _hy_fwd_kernel(cm_ref, sf_ref, u_ref, a_ref, b_ref, d_ref, q_ref):
    u = u_ref[...]
    ur = _dot(cm_ref[...], u)
    g = _dot(sf_ref[...], u)
    b = b_ref[...]
    q_ref[0] = (a_ref[...] * ur + b * g).astype(BF16)
    q_ref[1] = (d_ref[...] * g - b * ur).astype(BF16)


def _hy_inv_kernel(cm_ref, si_ref, q_ref, x0_ref, vx_ref, skip_ref, o_ref):
    y = _dot(cm_ref[...], q_ref[0]) + _dot(si_ref[...], q_ref[1])
    o_ref[...] = (x0_ref[...] * (y + vx_ref[...] * skip_ref[...])).astype(o_ref.dtype)


def _hy_long_conv(x0, vx, vxb, spectra, skip, n_len, tabs):
    cm, sf, si = tabs["dft"]
    a, b, d = spectra
    tf = min(512, n_len)
    nf = n_len // tf
    mat = pl.BlockSpec((tf, n_len), lambda bb, f: (f, 0))
    spec = pl.BlockSpec((tf, HY_WIDTH), lambda bb, f: (f, 0))
    q = pl.pallas_call(
        _hy_fwd_kernel,
        out_shape=jax.ShapeDtypeStruct((BATCH, 2, n_len, HY_WIDTH), BF16),
        grid=(BATCH, nf),
        in_specs=[mat, mat, pl.BlockSpec((None, n_len, HY_WIDTH), lambda bb, f: (bb, 0, 0)), spec, spec, spec],
        out_specs=pl.BlockSpec((None, 2, tf, HY_WIDTH), lambda bb, f: (bb, 0, f, 0)),
        compiler_params=_cparams("parallel", "arbitrary"),
        name="hyena_dft_fwd",
    )(cm, sf, vxb, a, b, d)
    tile = pl.BlockSpec((None, tf, HY_WIDTH), lambda bb, f: (bb, f, 0))
    return pl.pallas_call(
        _hy_inv_kernel,
        out_shape=jax.ShapeDtypeStruct((BATCH, n_len, HY_WIDTH), BF16),
        grid=(BATCH, nf),
        in_specs=[mat, mat, pl.BlockSpec((None, 2, n_len, HY_WIDTH), lambda bb, f: (bb, 0, 0, 0)),
                  tile, tile, pl.BlockSpec((1, HY_WIDTH), lambda bb, f: (0, 0))],
        out_specs=tile,
        compiler_params=_cparams("parallel", "arbitrary"),
        name="hyena_dft_inv",
    )(cm, si, q, x0, vx, skip.reshape(1, -1))


_SWA_Q = SWA_HEADS * SWA_HEAD_DIM
_SWA_KV = SWA_KV_HEADS * SWA_HEAD_DIM


def _swaproj_kernel(h_ref, w_ref, cos_ref, sin_ref, q_ref, k_ref, v_ref, *, rope):
    t = _dot(h_ref[...], w_ref[...])
    q = t[:, :_SWA_Q]
    k = t[:, 2 * _SWA_Q:2 * _SWA_Q + _SWA_KV]
    v = t[:, 2 * _SWA_Q + 2 * _SWA_KV:]
    if rope:
        cos = cos_ref[...]
        sin = sin_ref[...]
        q = q * cos + t[:, _SWA_Q:2 * _SWA_Q] * sin
        k = k * cos[:, :_SWA_KV] + t[:, 2 * _SWA_Q + _SWA_KV:2 * _SWA_Q + 2 * _SWA_KV] * sin[:, :_SWA_KV]
    q_ref[...] = (q * (SWA_HEAD_DIM ** -0.5)).astype(BF16)
    lo = lax.broadcasted_iota(jnp.int32, k.shape, 1) < SWA_HEAD_DIM
    for src, dst in ((k, k_ref), (v, v_ref)):
        sw = pltpu.roll(src, SWA_HEAD_DIM, axis=1)
        dst[:, 0 * LANES:1 * LANES] = jnp.where(lo, src, 0.0).astype(BF16)
        dst[:, 1 * LANES:2 * LANES] = jnp.where(lo, 0.0, sw).astype(BF16)
        dst[:, 2 * LANES:3 * LANES] = jnp.where(lo, sw, 0.0).astype(BF16)
        dst[:, 3 * LANES:4 * LANES] = jnp.where(lo, 0.0, src).astype(BF16)


def _swa_proj(h, w, cos_t, sin_t, rope, tm=512):
    m = h.shape[0]
    nt = cos_t.shape[0] // tm
    o = jax.ShapeDtypeStruct((m, 4 * LANES), BF16)
    ospec = pl.BlockSpec((tm, 4 * LANES), lambda i: (i, 0))
    tspec = pl.BlockSpec((tm, _SWA_Q), lambda i: (i % nt, 0))
    return pl.pallas_call(
        functools.partial(_swaproj_kernel, rope=rope),
        out_shape=(o, o, o),
        grid=(m // tm,),
        in_specs=[pl.BlockSpec((tm, D_MODEL), lambda i: (i, 0)), _const_spec(w.shape), tspec, tspec],
        out_specs=(ospec, ospec, ospec),
        compiler_params=_cparams("parallel"),
        name="swa_proj",
    )(h, w, cos_t, sin_t)


def _swa_kernel(sink_ref, q_ref, *refs, tq, win, lk, has_lat):
    if has_lat:
        kl_ref, vl_ref, kc_ref, vc_ref, o_ref = refs
        t0 = pl.program_id(1) * tq
        start = pl.multiple_of(jnp.clip(t0 - SWA_WINDOW, 0, lk - win), SWA_WINDOW)
        qpos = t0 + lax.broadcasted_iota(jnp.int32, (tq, win), 0)
        kpos = start + lax.broadcasted_iota(jnp.int32, (tq, win), 1)
        valid = jnp.abs(qpos - kpos) <= SWA_WINDOW
        valid = jnp.concatenate([valid, valid], axis=0)
    else:
        kc_ref, vc_ref, o_ref = refs
    upper = lax.broadcasted_iota(jnp.int32, (2 * tq, 1), 0) < tq
    for g in range(SWA_KV_HEADS):
        j0 = 2 * g
        qb = jnp.concatenate([q_ref[:, j0 * LANES:(j0 + 1) * LANES],
                              q_ref[:, (j0 + 1) * LANES:(j0 + 2) * LANES]], axis=0)
        acc = None
        for par in range(2):
            c0 = (2 * g + par) * LANES
            sk = jnp.where(upper, sink_ref[2 * j0 + par], sink_ref[2 * j0 + 2 + par])
            s_c = _dot_nt(qb, kc_ref[:, c0:c0 + LANES])
            m = jnp.maximum(s_c.max(axis=-1, keepdims=True), sk)
            if has_lat:
                s_l = _dot_nt(qb, kl_ref[pl.ds(start, win), c0:c0 + LANES])
                s_l = jnp.where(valid, s_l, NEG_BIG)
                m = jnp.maximum(m, s_l.max(axis=-1, keepdims=True))
            p_c = jnp.exp(s_c - m)
            den = p_c.sum(axis=-1, keepdims=True) + jnp.exp(sk - m)
            if has_lat:
                p_l = jnp.exp(s_l - m)
                den = den + p_l.sum(axis=-1, keepdims=True)
            o = _dot(p_c.astype(BF16), vc_ref[:, c0:c0 + LANES])
            if has_lat:
                o = o + _dot(p_l.astype(BF16), vl_ref[pl.ds(start, win), c0:c0 + LANES])
            o = o * (1.0 / den)
            acc = o if acc is None else acc + o
        o_ref[:, j0 * LANES:(j0 + 1) * LANES] = acc[:tq].astype(o_ref.dtype)
        o_ref[:, (j0 + 1) * LANES:(j0 + 2) * LANES] = acc[tq:].astype(o_ref.dtype)


def _swa_attention(sink, q, lat, ctx, lq, tq=256):
    nq = lq // tq
    win = tq + 2 * SWA_WINDOW
    full = lambda n: pl.BlockSpec((n, 4 * LANES), lambda b, i: (b, 0))
    in_specs = [pl.BlockSpec(memory_space=pltpu.SMEM),
                pl.BlockSpec((tq, 4 * LANES), lambda b, i: (b * nq + i, 0))]
    args = [sink, q]
    if lat is not None:
        in_specs += [full(SEQ), full(SEQ)]
        args += list(lat)
    in_specs += [full(CTX_LEN), full(CTX_LEN)]
    args += list(ctx)
    return pl.pallas_call(
        functools.partial(_swa_kernel, tq=tq, win=win, lk=SEQ, has_lat=lat is not None),
        out_shape=jax.ShapeDtypeStruct((BATCH * lq, 4 * LANES), BF16),
        grid=(BATCH, nq),
        in_specs=in_specs,
        out_specs=pl.BlockSpec((tq, 4 * LANES), lambda b, i: (b * nq + i, 0)),
        compiler_params=_cparams("parallel", "arbitrary"),
        name="swa_attention",
    )(*args)


def _out_kernel(a_ref, y_ref, s_ref, w_ref, x_ref, m_ref, g_ref, xo_ref, ho_ref):
    na = MLA_HEADS * MLA_V
    acc = _dot(a_ref[...], w_ref[0:na, :])
    acc = acc + _dot(y_ref[...], w_ref[na:na + HY_WIDTH, :])
    acc = acc + _dot(s_ref[...], w_ref[na + HY_WIDTH:, :])
    x = x_ref[...] + m_ref[2:3, :] * acc
    xo_ref[...] = x
    ho_ref[...] = (_rms(x, g_ref[...]) * (1.0 + m_ref[4:5, :]) + m_ref[3:4, :]).astype(ho_ref.dtype)


def _out_proj(a, y, s, w_out, x, mods, g_ffn, groups, row0, h_dtype, tm=512):
    m = x.shape[0]
    nr = m // groups // tm
    tile = lambda n: pl.BlockSpec((tm, n), lambda g_, i: (g_ * nr + i, 0))
    return pl.pallas_call(
        _out_kernel,
        out_shape=(jax.ShapeDtypeStruct((m, D_MODEL), F32), jax.ShapeDtypeStruct((m, D_MODEL), h_dtype)),
        grid=(groups, nr),
        in_specs=[tile(a.shape[1]), tile(y.shape[1]), tile(s.shape[1]), _const_spec(w_out.shape),
                  tile(D_MODEL), pl.BlockSpec((None, 6, D_MODEL), lambda g_, i: (row0 + g_, 0, 0)),
                  pl.BlockSpec((1, D_MODEL), lambda g_, i: (0, 0))],
        out_specs=(tile(D_MODEL), tile(D_MODEL)),
        compiler_params=_cparams("parallel", "parallel"),
        name="out_proj",
    )(a, y, s, w_out, x, mods, g_ffn.reshape(1, -1))


_META_W1, _META_W2, _META_E1, _META_E2, _META_R1, _META_R2 = range(6)


def _router_kernel(h_ref, w_ref, meta_ref, cnt_ref, carry_ref):
    @pl.when(pl.program_id(0) == 0)
    def _():
        carry_ref[...] = jnp.zeros_like(carry_ref)

    logits = _dot3(h_ref[...], w_ref[...])
    tm = logits.shape[0]
    lane = lax.broadcasted_iota(jnp.int32, logits.shape, 1).astype(F32)
    logits = jnp.where(lane < N_EXPERTS, logits, NEG_BIG)
    m1 = logits.max(axis=-1, keepdims=True)
    i1 = jnp.where(logits == m1, lane, float(LANES)).min(axis=-1, keepdims=True)
    rest = jnp.where(lane == i1, NEG_BIG, logits)
    m2 = rest.max(axis=-1, keepdims=True)
    i2 = jnp.where(rest == m2, lane, float(LANES)).min(axis=-1, keepdims=True)
    e2 = jnp.exp(m2 - m1)
    w1 = 1.0 / (1.0 + e2)
    hot = jnp.where((lane == i1) | (lane == i2), 1.0, 0.0)
    before = (lax.broadcasted_iota(jnp.int32, (tm, tm), 0) > lax.broadcasted_iota(jnp.int32, (tm, tm), 1))
    seen = _dot(before.astype(F32).astype(BF16), hot.astype(BF16)) + carry_ref[0:1, :]
    r1 = jnp.where(lane == i1, seen, 0.0).sum(axis=-1, keepdims=True)
    r2 = jnp.where(lane == i2, seen, 0.0).sum(axis=-1, keepdims=True)
    carry_ref[...] = carry_ref[...] + hot.sum(axis=0, keepdims=True)
    cnt_ref[...] = carry_ref[...]
    rec = jnp.zeros_like(logits)
    for k, v in ((_META_W1, w1), (_META_W2, e2 * w1), (_META_E1, i1), (_META_E2, i2), (_META_R1, r1), (_META_R2, r2)):
        rec = jnp.where(lane == float(k), v, rec)
    meta_ref[...] = rec


def _router(h, w_router, tm=512):
    m = h.shape[0]
    wp = jnp.zeros((D_MODEL, LANES), F32).at[:, :N_EXPERTS].set(w_router)
    return pl.pallas_call(
        _router_kernel,
        out_shape=(jax.ShapeDtypeStruct((m, LANES), F32), jax.ShapeDtypeStruct((8, LANES), F32)),
        grid=(m // tm,),
        in_specs=[pl.BlockSpec((tm, D_MODEL), lambda i: (i, 0)),
                  pl.BlockSpec((D_MODEL, LANES), lambda i: (0, 0))],
        out_specs=(pl.BlockSpec((tm, LANES), lambda i: (i, 0)), pl.BlockSpec((8, LANES), lambda i: (0, 0))),
        scratch_shapes=[pltpu.VMEM((8, LANES), F32)],
        compiler_params=_cparams("arbitrary"),
        name="moe_router",
    )(h, wp)


MOE_TM = 1024
MOE_SUB = 512


def _moe_plan(meta, counts, m):
    n_steps = 2 * m // MOE_TM + N_EXPERTS
    cnt = counts[0, :N_EXPERTS].astype(jnp.int32)
    steps_e = (cnt + MOE_TM - 1) // MOE_TM
    ends = jnp.cumsum(steps_e)
    first = ends - steps_e
    total = ends[-1]
    e = meta[:, _META_E1:_META_E2 + 1].astype(jnp.int32)
    rank = meta[:, _META_R1:_META_R2 + 1].astype(jnp.int32)
    pos = (first * MOE_TM)[e] + rank
    s_idx = jnp.arange(n_steps, dtype=jnp.int32)
    step_e = jnp.minimum(jnp.searchsorted(ends, s_idx, side="right").astype(jnp.int32), N_EXPERTS - 1)
    valid = jnp.clip(cnt[step_e] - (s_idx - first[step_e]) * MOE_TM, 0, MOE_TM)
    valid = jnp.where(s_idx < total, valid, 0)
    return pos.reshape(-1), step_e, valid, n_steps


def _row_copy(src_ref, src_row, dst_ref, dst_row, sem):
    return pltpu.make_async_copy(src_ref.at[pl.ds(src_row, 1)], dst_ref.at[pl.ds(dst_row, 1)], sem)


def _invert_kernel(pos_ref, sv_ref, src_ref):
    def clear_step(s, c):
        def clear(i, c2):
            src_ref[s * MOE_TM + i] = 0
            return c2

        return lax.fori_loop(sv_ref[s], MOE_TM, clear, c)

    def place(t, c):
        p0 = pos_ref[2 * t]
        p1 = pos_ref[2 * t + 1]
        src_ref[p0] = t
        src_ref[p1] = t
        return c

    lax.fori_loop(0, sv_ref.shape[0], clear_step, 0)
    lax.fori_loop(0, pos_ref.shape[0] // 2, place, 0, unroll=8)


def _moe_invert(pos, valid, n_rows):
    smem = pl.BlockSpec(memory_space=pltpu.SMEM)
    return pl.pallas_call(
        _invert_kernel,
        out_shape=jax.ShapeDtypeStruct((n_rows,), jnp.int32),
        in_specs=[smem, smem], out_specs=smem,
        name="moe_invert",
    )(pos, valid)


def _block_wait(src_ref, dst_ref, rows, sem):
    pltpu.make_async_copy(src_ref.at[pl.ds(0, rows)], dst_ref.at[pl.ds(0, rows)], sem).wait()


def _moe_ffn_kernel(se_ref, sv_ref, src_ref, h_ref, wg_ref, wu_ref, wd_ref, o_ref, x32_ref, xb_ref, sem):
    del se_ref
    s = pl.program_id(0)
    f = pl.program_id(1)
    valid = sv_ref[s]

    def gather(step):
        base = step * MOE_TM
        rows = (sv_ref[step] + MOE_SUB - 1) // MOE_SUB * MOE_SUB

        def issue(i, c):
            _row_copy(h_ref, src_ref[base + i], x32_ref, i, sem).start()
            return c

        lax.fori_loop(0, rows, issue, 0)

    @pl.when(f == 0)
    def _():
        @pl.when(s == 0)
        def _():
            gather(0)

        for r0 in range(0, MOE_TM, MOE_SUB):
            @pl.when(valid > r0)
            def _():
                _block_wait(h_ref, x32_ref, MOE_SUB, sem)

        for r0 in range(0, MOE_TM, MOE_SUB):
            @pl.when(valid > r0)
            def _():
                xb_ref[r0:r0 + MOE_SUB, :] = x32_ref[r0:r0 + MOE_SUB, :].astype(BF16)

        @pl.when(s + 1 < pl.num_programs(0))
        def _():
            gather(s + 1)

        o_ref[...] = jnp.zeros_like(o_ref)

    def compute(rows):
        xs = xb_ref[0:rows, :]
        mid = _silu(_dot(xs, wg_ref[...].astype(BF16))) * _dot(xs, wu_ref[...].astype(BF16))
        o_ref[0:rows, :] += _dot(mid.astype(BF16), wd_ref[...].astype(BF16))

    @pl.when(valid > MOE_SUB)
    def _():
        compute(MOE_TM)

    @pl.when((valid > 0) & (valid <= MOE_SUB))
    def _():
        compute(MOE_SUB)


def _moe_ffn(h, src, w_gate, w_up, w_down, step_e, valid, n_steps, tf=512):
    assert MOE_TM == 2 * MOE_SUB
    n_ff = w_gate.shape[-1]
    nf = n_ff // tf
    fidx = lambda s, f, sv: jnp.where(sv[s] > 0, f, nf - 1)
    return pl.pallas_call(
        _moe_ffn_kernel,
        out_shape=jax.ShapeDtypeStruct((n_steps * MOE_TM, D_MODEL), F32),
        grid_spec=pltpu.PrefetchScalarGridSpec(
            num_scalar_prefetch=3, grid=(n_steps, nf),
            in_specs=[
                pl.BlockSpec(memory_space=pl.ANY),
                pl.BlockSpec((None, D_MODEL, tf), lambda s, f, se, sv, sr: (se[s], 0, fidx(s, f, sv))),
                pl.BlockSpec((None, D_MODEL, tf), lambda s, f, se, sv, sr: (se[s], 0, fidx(s, f, sv))),
                pl.BlockSpec((None, tf, D_MODEL), lambda s, f, se, sv, sr: (se[s], fidx(s, f, sv), 0)),
            ],
            out_specs=pl.BlockSpec((MOE_TM, D_MODEL), lambda s, f, se, sv, sr: (s, 0), pipeline_mode=pl.Buffered(1)),
            scratch_shapes=[pltpu.VMEM((MOE_TM, D_MODEL), F32), pltpu.VMEM((MOE_TM, D_MODEL), BF16),
                            pltpu.SemaphoreType.DMA(())]),
        compiler_params=_cparams("arbitrary", "arbitrary"),
        name="moe_experts",
    )(step_e, valid, src, h, w_gate, w_up, w_down)


def _combine_kernel(pos_ref, meta_ref, x_ref, m_ref, gf_ref, ys_ref, o_ref, buf_ref, sem, *, tm, final):
    base = pl.program_id(0) * tm

    def issue(i, c):
        for slot in range(2):
            _row_copy(ys_ref, pos_ref[2 * (base + i) + slot], buf_ref.at[slot], i, sem).start()
        return c

    lax.fori_loop(0, tm, issue, 0, unroll=4)
    for slot in range(2):
        _block_wait(ys_ref, buf_ref.at[slot], tm, sem)
    meta = meta_ref[...]
    y = meta[:, _META_W1:_META_W1 + 1] * buf_ref[0] + meta[:, _META_W2:_META_W2 + 1] * buf_ref[1]
    x = x_ref[...] + m_ref[5:6, :] * y
    if final:
        x = _rms(x, gf_ref[...])
    o_ref[...] = x


def _moe_combine(pos, meta, x, mods, ys, groups, row0, g_final, tm=512):
    m = x.shape[0]
    nr = m // groups // tm
    final = g_final is not None
    gf = (g_final if final else jnp.ones((D_MODEL,), F32)).reshape(1, D_MODEL)
    return pl.pallas_call(
        functools.partial(_combine_kernel, tm=tm, final=final),
        out_shape=jax.ShapeDtypeStruct((m, D_MODEL), F32),
        grid_spec=pltpu.PrefetchScalarGridSpec(
            num_scalar_prefetch=1, grid=(m // tm,),
            in_specs=[pl.BlockSpec((tm, LANES), lambda i, p: (i, 0)),
                      pl.BlockSpec((tm, D_MODEL), lambda i, p: (i, 0)),
                      pl.BlockSpec((None, 6, D_MODEL), lambda i, p: (row0 + i // nr, 0, 0)),
                      pl.BlockSpec((1, D_MODEL), lambda i, p: (0, 0)),
                      pl.BlockSpec(memory_space=pl.ANY)],
            out_specs=pl.BlockSpec((tm, D_MODEL), lambda i, p: (i, 0)),
            scratch_shapes=[pltpu.VMEM((2, tm, D_MODEL), F32), pltpu.SemaphoreType.DMA(())]),
        compiler_params=_cparams("arbitrary"),
        name="moe_combine",
    )(pos, meta, x, mods, gf, ys)


def _moe(h, x, mods, groups, row0, w_router, w_gate, w_up, w_down, g_final):
    m = h.shape[0]
    meta, counts = _router(h, w_router)
    pos, step_e, valid, n_steps = _moe_plan(meta, counts, m)
    src = _moe_invert(pos, valid, n_steps * MOE_TM)
    ys = _moe_ffn(h, src, w_gate, w_up, w_down, step_e, valid, n_steps)
    return _moe_combine(pos, meta, x, mods, ys, groups, row0, g_final)


def _ffn_kernel(h_ref, wg_ref, wu_ref, wd_ref, x_ref, m_ref, gn_ref, mn_ref, o_ref, *maybe_hn_ref):
    f = pl.program_id(2)

    @pl.when(f == 0)
    def _():
        o_ref[...] = jnp.zeros_like(o_ref)

    h = h_ref[...]
    mid = _silu(_dot(h, wg_ref[...])) * _dot(h, wu_ref[...])
    o_ref[...] += _dot(mid.astype(BF16), wd_ref[...])

    @pl.when(f == pl.num_programs(2) - 1)
    def _():
        x = x_ref[...] + m_ref[5:6, :] * o_ref[...]
        o_ref[...] = x
        for hn_ref in maybe_hn_ref:
            hn_ref[...] = (_rms(x, gn_ref[...]) * (1.0 + mn_ref[1:2, :]) + mn_ref[0:1, :]).astype(hn_ref.dtype)


def _ffn(h, w_gate, w_up, w_down, x, mods, groups, row0, nxt, tm, tf=512):
    m = x.shape[0]
    n_ff = w_gate.shape[1]
    nr = m // groups // tm
    g_next, mods_next = nxt if nxt is not None else (jnp.ones((D_MODEL,), F32), mods)
    once = pl.Buffered(1)
    tile = lambda n, mode=None: pl.BlockSpec((tm, n), lambda g_, i, f: (g_ * nr + i, 0), pipeline_mode=mode)
    mod_spec = pl.BlockSpec((None, 6, D_MODEL), lambda g_, i, f: (row0 + g_, 0, 0))
    out_shape = [jax.ShapeDtypeStruct((m, D_MODEL), F32)]
    out_specs = [tile(D_MODEL, once)]
    if nxt is not None:
        out_shape.append(jax.ShapeDtypeStruct((m, D_MODEL), BF16))
        out_specs.append(tile(D_MODEL, once))
    outs = pl.pallas_call(
        _ffn_kernel,
        out_shape=tuple(out_shape),
        grid=(groups, nr, n_ff // tf),
        in_specs=[tile(D_MODEL),
                  pl.BlockSpec((D_MODEL, tf), lambda g_, i, f: (0, f)),
                  pl.BlockSpec((D_MODEL, tf), lambda g_, i, f: (0, f)),
                  pl.BlockSpec((tf, D_MODEL), lambda g_, i, f: (f, 0)),
                  tile(D_MODEL, once), mod_spec,
                  pl.BlockSpec((1, D_MODEL), lambda g_, i, f: (0, 0)), mod_spec],
        out_specs=tuple(out_specs),
        compiler_params=_cparams("parallel", "parallel", "arbitrary"),
        name="swiglu_ffn",
    )(h, w_gate, w_up, w_down, x, mods, g_next.reshape(1, D_MODEL), mods_next)
    return outs if nxt is not None else (outs[0], None)


def _swap_halves(w):
    half = w.shape[-1] // 2
    return jnp.concatenate([w[..., half:], w[..., :half]], axis=-1)


def _swap_heads(w, heads, dim):
    k = w.shape[0]
    return _swap_halves(w.reshape(k, heads, dim)).reshape(k, heads * dim)


def _rope_tables():
    rows = SEQ // GRID_W
    row = np.repeat(np.arange(rows), GRID_W).astype(np.float32)
    col = np.tile(np.arange(GRID_W), rows).astype(np.float32)
    quarter = MLA_ROPE // 4
    freqs = (np.float32(ROPE_THETA) ** (-np.arange(quarter, dtype=np.float32) / quarter)).astype(np.float32)
    ang = np.concatenate([row[:, None] * freqs[None], col[:, None] * freqs[None]], axis=-1)
    cos = np.cos(ang.astype(np.float64))
    sin = np.sin(ang.astype(np.float64))
    cos2 = np.concatenate([cos, cos], axis=-1)
    sin2 = np.concatenate([-sin, sin], axis=-1)
    tab_lat = np.concatenate([cos2, sin2], axis=-1)
    tab_ctx = np.concatenate([np.ones((CTX_LEN, MLA_ROPE)), np.zeros((CTX_LEN, MLA_ROPE))], axis=-1)
    return dict(
        tab_lat=jnp.asarray(tab_lat, F32), tab_ctx=jnp.asarray(tab_ctx, F32),
        cos8=jnp.asarray(np.tile(cos2, (1, SWA_HEADS)), F32),
        sin8=jnp.asarray(np.tile(sin2, (1, SWA_HEADS)), F32),
        ones8=jnp.ones((CTX_LEN, _SWA_Q), F32), zeros8=jnp.zeros((CTX_LEN, _SWA_Q), F32),
    )


def _layer_weights(p, l):
    w_in = p["w_in"][l]
    cq = w_in[:, _O_CQ:_O_CKV]
    ckv = w_in[:, _O_CKV:_O_KPE]
    kpe = w_in[:, _O_KPE:_O_HY]
    hy = w_in[:, _O_HY:_O_SQ]
    sq = w_in[:, _O_SQ:_O_SK]
    sk = w_in[:, _O_SK:_O_SV]
    sv = w_in[:, _O_SV:]
    dq = MLA_NOPE + MLA_ROPE
    wq = p["w_q_up"][l].reshape(MLA_Q_LORA, MLA_HEADS, dq)
    wq = jnp.concatenate([wq, _swap_halves(wq[..., MLA_NOPE:])], axis=-1)
    wkv = p["w_kv_up"][l].reshape(MLA_KV_LORA, MLA_HEADS, MLA_NOPE + MLA_V)
    return dict(
        w_cq=cq.astype(BF16),
        w_ckv=jnp.concatenate([ckv, kpe, _swap_halves(kpe)], axis=-1).astype(BF16),
        w_hy=hy.astype(BF16),
        w_swa=jnp.concatenate([sq, _swap_heads(sq, SWA_HEADS, SWA_HEAD_DIM), sk,
                               _swap_heads(sk, SWA_KV_HEADS, SWA_HEAD_DIM), sv], axis=-1).astype(BF16),
        w_q=jnp.transpose(wq, (1, 0, 2)).astype(BF16),
        w_kv=jnp.transpose(wkv, (1, 0, 2)).astype(BF16),
        w_out=p["w_out"][l].astype(BF16),
        g_q=p["g_q"][l], g_kv=p["g_kv"][l],
        hy_conv_w=p["hy_conv_w"][l], hy_conv_b=p["hy_conv_b"][l],
        hy_w1=p["hy_w1"][l], hy_b1=p["hy_b1"][l], hy_w2=p["hy_w2"][l], hy_b2=p["hy_b2"][l],
        hy_w3=p["hy_w3"][l], hy_b3=p["hy_b3"][l], hy_w_filt=p["hy_w_filt"][l],
        hy_freq=p["hy_freq"][l], hy_skip=p["hy_skip"][l], swa_sink=p["swa_sink"][l],
    )


def _hyena(h, lw, n_len, tabs):
    hy = _matmul(h, lw["w_hy"], F32)
    x0, vx, vxb = _hy_pre(hy, lw["hy_conv_w"], lw["hy_conv_b"], n_len)
    spectra = _hy_filters(lw, n_len, tabs)
    y = _hy_long_conv(x0, vx, vxb, spectra, lw["hy_skip"], n_len, tabs)
    return y.reshape(BATCH * n_len, HY_WIDTH)


def _mixer(h_lat, h_ctx, lw, rope, hy_tabs, need_ctx):
    k_l, v_l = _kv_proj(h_lat, lw["w_ckv"], lw["g_kv"], lw["w_kv"], rope["tab_lat"])
    k_c, v_c = _kv_proj(h_ctx, lw["w_ckv"], lw["g_kv"], lw["w_kv"], rope["tab_ctx"], tm=CTX_LEN)
    q_l = _q_proj(h_lat, lw["w_cq"], lw["g_q"], lw["w_q"], rope["tab_lat"])
    a_l = _mla_attention(q_l, [(k_l, v_l, SEQ), (k_c, v_c, CTX_LEN)], SEQ)
    y_l = _hyena(h_lat, lw, SEQ, hy_tabs[SEQ])
    sq_l, kk_l, vv_l = _swa_proj(h_lat, lw["w_swa"], rope["cos8"], rope["sin8"], True)
    sq_c, kk_c, vv_c = _swa_proj(h_ctx, lw["w_swa"], rope["ones8"], rope["zeros8"], False, tm=CTX_LEN)
    s_l = _swa_attention(lw["swa_sink"], sq_l, (kk_l, vv_l), (kk_c, vv_c), SEQ)
    if not need_ctx:
        return (a_l, y_l, s_l), None
    q_c = _q_proj(h_ctx, lw["w_cq"], lw["g_q"], lw["w_q"], rope["tab_ctx"], tm=CTX_LEN)
    a_c = _mla_attention(q_c, [(k_c, v_c, CTX_LEN)], CTX_LEN, tq=CTX_LEN)
    y_c = _hyena(h_ctx, lw, CTX_LEN, hy_tabs[CTX_LEN])
    s_c = _swa_attention(lw["swa_sink"], sq_c, None, (kk_c, vv_c), CTX_LEN, tq=CTX_LEN)
    return (a_l, y_l, s_l), (a_c, y_c, s_c)


def _forward(p):
    rope = _rope_tables()
    hy_tabs = {n: dict(dft=_dft_tables(n), filt=_filter_tables(n)) for n in (SEQ, CTX_LEN)}
    cvec = jnp.zeros((MOD_ROWS, D_MODEL), F32).at[:BATCH].set(p["c"]).at[BATCH].set(p["c_ctx"])
    mods_all = _modulation(cvec, p["w_mod"], p["b_mod"]).reshape(DEPTH, MOD_ROWS, 6, D_MODEL)

    x_lat = p["x"].reshape(BATCH * SEQ, D_MODEL)
    x_ctx = p["ctx"].reshape(BATCH * CTX_LEN, D_MODEL)
    h_in = [None, None]
    for l in range(DEPTH):
        last = l == DEPTH - 1
        mods = mods_all[l]
        lw = _layer_weights(p, l)
        h_lat = h_in[0] if h_in[0] is not None else _norm_mod(x_lat, p["g_mix"][l], mods, BATCH, 0, 0, BF16)
        h_ctx = h_in[1] if h_in[1] is not None else _norm_mod(x_ctx, p["g_mix"][l], mods, 1, BATCH, 0, BF16)
        mix_l, mix_c = _mixer(h_lat, h_ctx, lw, rope, hy_tabs, not last)
        streams = [(x_lat, mix_l, BATCH, 0, 1024)]
        if not last:
            streams.append((x_ctx, mix_c, 1, BATCH, 512))
        outs = []
        h_in = [None, None]
        i = l // 2
        dense = l % 2 == 0
        for n, (x, mix, groups, row0, ffn_tm) in enumerate(streams):
            x, h2 = _out_proj(*mix, lw["w_out"], x, mods, p["g_ffn"][l], groups, row0, BF16 if dense else F32)
            if dense:
                nxt = None if last else (p["g_mix"][l + 1], mods_all[l + 1])
                x, h_in[n] = _ffn(h2, p["ffn_w_gate"][i].astype(BF16), p["ffn_w_up"][i].astype(BF16),
                                  p["ffn_w_down"][i].astype(BF16), x, mods, groups, row0, nxt, ffn_tm)
            else:
                g_final = p["g_final"] if (last and n == 0) else None
                x = _moe(h2, x, mods, groups, row0, p["moe_router"][i], p["moe_w_gate"][i], p["moe_w_up"][i],
                         p["moe_w_down"][i], g_final)
            outs.append(x)
        x_lat = outs[0]
        if not last:
            x_ctx = outs[1]
    if DEPTH % 2 == 1:
        x_lat = _norm_mod(x_lat, p["g_final"], mods_all[0], BATCH, 0, None, F32)
    return x_lat.reshape(BATCH, SEQ, D_MODEL)


def kernel(x, c, ctx, c_ctx, w_mod, b_mod, g_mix, g_ffn, w_in, g_q, w_q_up, g_kv, w_kv_up, hy_conv_w, hy_conv_b, hy_w1, hy_b1, hy_w2, hy_b2, hy_w3, hy_b3, hy_w_filt, hy_freq, hy_skip, swa_sink, w_out, ffn_w_gate, ffn_w_up, ffn_w_down, moe_router, moe_w_gate, moe_w_up, moe_w_down, g_final):
    return _forward(dict(
        x=x, c=c, ctx=ctx, c_ctx=c_ctx, w_mod=w_mod, b_mod=b_mod, g_mix=g_mix, g_ffn=g_ffn, w_in=w_in,
        g_q=g_q, w_q_up=w_q_up, g_kv=g_kv, w_kv_up=w_kv_up, hy_conv_w=hy_conv_w, hy_conv_b=hy_conv_b,
        hy_w1=hy_w1, hy_b1=hy_b1, hy_w2=hy_w2, hy_b2=hy_b2, hy_w3=hy_w3, hy_b3=hy_b3,
        hy_w_filt=hy_w_filt, hy_freq=hy_freq, hy_skip=hy_skip, swa_sink=swa_sink, w_out=w_out,
        ffn_w_gate=ffn_w_gate, ffn_w_up=ffn_w_up, ffn_w_down=ffn_w_down, moe_router=moe_router,
        moe_w_gate=moe_w_gate, moe_w_up=moe_w_up, moe_w_down=moe_w_down, g_final=g_final))
```

```python
import functools
import math

import numpy as np
import jax
import jax.numpy as jnp
from jax import lax
from jax.experimental import pallas as pl
from jax.experimental.pallas import tpu as pltpu

F32 = jnp.float32
BF16 = jnp.bfloat16

D_MODEL = 2048
BATCH = 4
SEQ = 2048
DEPTH = 2
CTX_LEN = 256
GRID_W = 64
NORM_EPS = 1e-6
ROPE_THETA = 10000.0
MLA_HEADS = 8
MLA_NOPE = 128
MLA_ROPE = 64
MLA_V = 128
MLA_Q_LORA = 768
MLA_KV_LORA = 512
HY_WIDTH = 512
HY_BANDS = 16
HY_EMB = 1 + 2 * HY_BANDS
HY_FILTER_HIDDEN = 64
HY_DECAY_TARGET = 1e-2
HY_QUICK_DECAY_PCT = 0.3
HY_SLOW_DECAY_PCT = 1.5
SWA_HEADS = 8
SWA_KV_HEADS = 2
SWA_HEAD_DIM = 64
SWA_WINDOW = 128
N_EXPERTS = 8
D_FF = 5632
D_FF_EXPERT = 7168

LANES = 128
MXU_DIM = 256
VMEM_LIMIT_BYTES = 56 * 1024 * 1024
NEG_BIG = -1e30
MOD_ROWS = 8

_O_CQ = 0
_O_CKV = _O_CQ + MLA_Q_LORA
_O_KPE = _O_CKV + MLA_KV_LORA
_O_HY = _O_KPE + MLA_ROPE
_O_SQ = _O_HY + 3 * HY_WIDTH
_O_SK = _O_SQ + SWA_HEADS * SWA_HEAD_DIM
_O_SV = _O_SK + SWA_KV_HEADS * SWA_HEAD_DIM


def _cparams(*sem):
    return pltpu.CompilerParams(dimension_semantics=sem, vmem_limit_bytes=VMEM_LIMIT_BYTES)


def _dot(a, b):
    return jnp.dot(a, b, preferred_element_type=F32)


def _dot_nt(a, b):
    return lax.dot_general(a, b, (((1,), (1,)), ((), ())), preferred_element_type=F32)


def _split_bf16(a):
    hi = a.astype(BF16)
    lo = (a - hi.astype(F32)).astype(BF16)
    return hi, lo


def _dot3(a, b):
    ah, al = _split_bf16(a)
    bh, bl = _split_bf16(b)
    return _dot(ah, bh) + (_dot(al, bh) + _dot(ah, bl))


def _silu(x):
    return x / (1.0 + jnp.exp(-x))


def _rms(x, g):
    ms = jnp.mean(x * x, axis=-1, keepdims=True)
    return x * lax.rsqrt(ms + NORM_EPS) * g


def _const_spec(shape):
    nd = len(shape)
    return pl.BlockSpec(shape, lambda *_: (0,) * nd, pipeline_mode=pl.Buffered(1))


def _mod_kernel(c_ref, w_ref, b_ref, o_ref):
    a = _silu(c_ref[...])
    o_ref[0] = _dot3(a, w_ref[0]) + b_ref[0]


def _modulation(cvec, w_mod, b_mod):
    tn = 1024
    n = 6 * D_MODEL
    return pl.pallas_call(
        _mod_kernel,
        out_shape=jax.ShapeDtypeStruct((DEPTH, MOD_ROWS, n), F32),
        grid=(DEPTH, n // tn),
        in_specs=[
            pl.BlockSpec((MOD_ROWS, D_MODEL), lambda l, j: (0, 0)),
            pl.BlockSpec((1, D_MODEL, tn), lambda l, j: (l, 0, j)),
            pl.BlockSpec((1, 1, tn), lambda l, j: (l, 0, j)),
        ],
        out_specs=pl.BlockSpec((1, MOD_ROWS, tn), lambda l, j: (l, 0, j)),
        compiler_params=_cparams("arbitrary", "arbitrary"),
        name="adaln_mod",
    )(cvec, w_mod, b_mod.reshape(DEPTH, 1, n))


def _norm_kernel(x_ref, g_ref, m_ref, o_ref, *, si):
    y = _rms(x_ref[...], g_ref[...])
    if si is not None:
        y = y * (1.0 + m_ref[si + 1:si + 2, :]) + m_ref[si:si + 1, :]
    o_ref[...] = y.astype(o_ref.dtype)


def _norm_mod(x, g, mods, groups, row0, si, out_dtype, tm=512):
    m = x.shape[0]
    nr = m // groups // tm
    return pl.pallas_call(
        functools.partial(_norm_kernel, si=si),
        out_shape=jax.ShapeDtypeStruct((m, D_MODEL), out_dtype),
        grid=(groups, nr),
        in_specs=[
            pl.BlockSpec((tm, D_MODEL), lambda g_, i: (g_ * nr + i, 0)),
            pl.BlockSpec((1, D_MODEL), lambda g_, i: (0, 0)),
            pl.BlockSpec((None, 6, D_MODEL), lambda g_, i: (row0 + g_, 0, 0)),
        ],
        out_specs=pl.BlockSpec((tm, D_MODEL), lambda g_, i: (g_ * nr + i, 0)),
        compiler_params=_cparams("parallel", "parallel"),
        name="norm_mod",
    )(x, g.reshape(1, D_MODEL), mods)


def _mm_kernel(x_ref, w_ref, o_ref):
    o_ref[...] = _dot(x_ref[...], w_ref[...]).astype(o_ref.dtype)


def _matmul(x, w, out_dtype, tm=512):
    m, k = x.shape
    n = w.shape[1]
    return pl.pallas_call(
        _mm_kernel,
        out_shape=jax.ShapeDtypeStruct((m, n), out_dtype),
        grid=(m // tm,),
        in_specs=[pl.BlockSpec((tm, k), lambda i: (i, 0)), _const_spec((k, n))],
        out_specs=pl.BlockSpec((tm, n), lambda i: (i, 0)),
        compiler_params=_cparams("parallel"),
        name="matmul",
    )(x, w)


def _rope_halves(pe_pair, tab):
    r = pe_pair * tab
    return r + pltpu.roll(r, MLA_ROPE, axis=1)


def _qproj_kernel(h_ref, wc_ref, g_ref, wq_ref, tab_ref, o_ref):
    cq = _dot(h_ref[...], wc_ref[...])
    cqn = _rms(cq, g_ref[...]).astype(BF16)
    tab = tab_ref[...]
    for hh in range(MLA_HEADS):
        r = _dot(cqn, wq_ref[hh])
        o_ref[hh, :, 0:MLA_NOPE] = r[:, :MLA_NOPE].astype(BF16)
        o_ref[hh, :, MLA_NOPE:] = _rope_halves(r[:, MLA_NOPE:], tab).astype(BF16)


def _q_proj(h, w_cq, g_q, w_q, tab, tm=512):
    m = h.shape[0]
    nt = tab.shape[0] // tm
    return pl.pallas_call(
        _qproj_kernel,
        out_shape=jax.ShapeDtypeStruct((MLA_HEADS, m, MXU_DIM), BF16),
        grid=(m // tm,),
        in_specs=[
            pl.BlockSpec((tm, D_MODEL), lambda i: (i, 0)),
            _const_spec(w_cq.shape),
            _const_spec((1, MLA_Q_LORA)),
            _const_spec(w_q.shape),
            pl.BlockSpec((tm, LANES), lambda i: (i % nt, 0)),
        ],
        out_specs=pl.BlockSpec((MLA_HEADS, tm, MXU_DIM), lambda i: (0, i, 0)),
        compiler_params=_cparams("parallel"),
        name="mla_q_proj",
    )(h, w_cq, g_q.reshape(1, -1), w_q, tab)


def _kvproj_kernel(h_ref, wc_ref, g_ref, wkv_ref, tab_ref, k_ref, v_ref):
    t = _dot(h_ref[...], wc_ref[...])
    ckvn = _rms(t[:, :MLA_KV_LORA], g_ref[...]).astype(BF16)
    rot = _rope_halves(t[:, MLA_KV_LORA:], tab_ref[...])
    lane = lax.broadcasted_iota(jnp.int32, rot.shape, 1)
    kpe = jnp.where(lane < MLA_ROPE, rot, 0.0).astype(BF16)
    for hh in range(MLA_HEADS):
        r = _dot(ckvn, wkv_ref[hh])
        k_ref[hh, :, 0:MLA_NOPE] = r[:, :MLA_NOPE].astype(BF16)
        k_ref[hh, :, MLA_NOPE:] = kpe
        v_ref[hh, :, 0:MLA_V] = r[:, MLA_NOPE:].astype(BF16)
        v_ref[hh, :, MLA_V:] = jnp.ones((r.shape[0], MXU_DIM - MLA_V), BF16)


def _kv_proj(h, w_ckv, g_kv, w_kv, tab, tm=512):
    m = h.shape[0]
    nt = tab.shape[0] // tm
    return pl.pallas_call(
        _kvproj_kernel,
        out_shape=(jax.ShapeDtypeStruct((MLA_HEADS, m, MXU_DIM), BF16),
                   jax.ShapeDtypeStruct((MLA_HEADS, m, MXU_DIM), BF16)),
        grid=(m // tm,),
        in_specs=[
            pl.BlockSpec((tm, D_MODEL), lambda i: (i, 0)),
            _const_spec(w_ckv.shape),
            _const_spec((1, MLA_KV_LORA)),
            _const_spec(w_kv.shape),
            pl.BlockSpec((tm, LANES), lambda i: (i % nt, 0)),
        ],
        out_specs=(pl.BlockSpec((MLA_HEADS, tm, MXU_DIM), lambda i: (0, i, 0)),
                   pl.BlockSpec((MLA_HEADS, tm, MXU_DIM), lambda i: (0, i, 0))),
        compiler_params=_cparams("parallel"),
        name="mla_kv_proj",
    )(h, w_ckv, g_kv.reshape(1, -1), w_kv, tab)


MLA_KEY_CHUNK = 1024


def _mla_kernel(q_ref, *refs, seg_lens, scale):
    nseg = len(seg_lens)
    o_ref = refs[2 * nseg]
    q = q_ref[...]
    c2 = scale * math.log2(math.e)
    m = jnp.full((q.shape[0], 1), NEG_BIG, F32)
    acc = jnp.zeros((q.shape[0], MXU_DIM), F32)
    for s, lk in enumerate(seg_lens):
        k_ref, v_ref = refs[2 * s], refs[2 * s + 1]
        step = min(MLA_KEY_CHUNK, lk)
        for c0 in range(0, lk, step):
            sc = _dot_nt(q, k_ref[c0:c0 + step, :])
            m_new = jnp.maximum(m, sc.max(axis=-1, keepdims=True))
            p = jnp.exp2((sc - m_new) * c2).astype(BF16)
            acc = acc * jnp.exp2((m - m_new) * c2) + _dot(p, v_ref[c0:c0 + step, :])
            m = m_new
    o_ref[...] = (acc[:, :MLA_V] / acc[:, MLA_V:MLA_V + 1]).astype(o_ref.dtype)


def _mla_attention(q, segs, lq, tq=512):
    nq = lq // tq
    in_specs = [pl.BlockSpec((None, tq, MXU_DIM), lambda b, h, i: (h, b * nq + i, 0))]
    args = [q]
    for k, v, lk in segs:
        in_specs.append(pl.BlockSpec((None, lk, MXU_DIM), lambda b, h, i: (h, b, 0)))
        in_specs.append(pl.BlockSpec((None, lk, MXU_DIM), lambda b, h, i: (h, b, 0)))
        args += [k, v]
    scale = (MLA_NOPE + MLA_ROPE) ** -0.5
    return pl.pallas_call(
        functools.partial(_mla_kernel, seg_lens=tuple(lk for _, _, lk in segs), scale=scale),
        out_shape=jax.ShapeDtypeStruct((BATCH * lq, MLA_HEADS * MLA_V), BF16),
        grid=(BATCH, MLA_HEADS, nq),
        in_specs=in_specs,
        out_specs=pl.BlockSpec((tq, MLA_V), lambda b, h, i: (b * nq + i, h)),
        compiler_params=_cparams("parallel", "parallel", "arbitrary"),
        name="mla_attention",
    )(*args)


def _dft_tables(n_len):
    n2 = 2 * n_len
    idx = np.arange(n_len, dtype=np.int64)
    ang = (2.0 * np.pi / n2) * ((idx[:, None] * idx[None, :]) % n2).astype(np.float64)
    cm = np.cos(ang)
    sf = np.sin(ang)
    sf[0, :] = 1.0 - 2.0 * (idx % 2)
    return (jnp.asarray(cm, dtype=BF16), jnp.asarray(sf, dtype=BF16),
            jnp.asarray(sf.T.copy(), dtype=BF16))


def _filter_tables(n_len):
    pos = np.arange(n_len, dtype=np.float64)
    t = pos / max(n_len - 1, 1)
    bands = np.linspace(1e-4, HY_BANDS - 1, HY_BANDS)
    ang = (2.0 * math.pi / n_len) * pos[:, None] * bands[None]
    z = np.zeros((n_len, LANES), np.float64)
    z[:, 0] = t
    z[:, 1:1 + HY_BANDS] = np.cos(ang)
    z[:, 1 + HY_BANDS:HY_EMB] = -np.sin(ang)
    deltas = np.linspace(math.log(HY_DECAY_TARGET) / HY_SLOW_DECAY_PCT,
                         math.log(HY_DECAY_TARGET) / HY_QUICK_DECAY_PCT, HY_WIDTH)
    decay = np.exp(-t[:, None] * np.abs(deltas)[None])
    return jnp.asarray(z, dtype=F32), jnp.asarray(decay, dtype=F32)


def _hy_filter_kernel(z_ref, w1_ref, b1_ref, w2_ref, b2_ref, w3_ref, b3_ref, wf_ref, wb_ref,
                      fr_ref, dec_ref, cm_ref, sf_ref, a_ref, b_ref, d_ref, hid_ref, *, n_len):
    @pl.when(pl.program_id(0) == 0)
    def _():
        fr = fr_ref[...]
        h = jnp.sin(fr * (_dot3(z_ref[...], w1_ref[...]) + b1_ref[...]))
        h = jnp.sin(fr * (_dot3(h, w2_ref[...]) + b2_ref[...]))
        hid_ref[...] = jnp.sin(fr * (_dot3(h, w3_ref[...]) + b3_ref[...]))

    h = hid_ref[...]
    dec = dec_ref[...]
    h_f = _dot3(h, wf_ref[...]) * dec
    h_b = _dot3(h, wb_ref[...]) * dec
    row = lax.broadcasted_iota(jnp.int32, h_f.shape, 0)
    h_b = jnp.where(row == 0, 0.0, h_b)
    hs = h_f + h_b
    hd = h_f - h_b
    hs_h, hs_l = _split_bf16(hs)
    hd_h, hd_l = _split_bf16(hd)
    cm = cm_ref[...]
    sf = sf_ref[...]
    t_re = _dot(cm, hs_h) + _dot(cm, hs_l)
    g_t = _dot(sf, hd_h) + _dot(sf, hd_l)
    t_ny = (_dot(sf[0:16, :], hs_h) + _dot(sf[0:16, :], hs_l))[0:1, :]
    inv_n = 1.0 / (2 * n_len)
    first = row == 0
    a_ref[...] = jnp.where(first, inv_n, 2.0 * inv_n) * t_re
    b_ref[...] = jnp.where(first, 0.0, -2.0 * inv_n * g_t)
    d_ref[...] = jnp.where(first, inv_n * t_ny, 2.0 * inv_n * t_re)


def _hy_filters(lp, n_len, tabs):
    z, decay = tabs["filt"]
    cm, sf, _ = tabs["dft"]
    nblk = HY_WIDTH // LANES
    w1 = jnp.zeros((LANES, HY_FILTER_HIDDEN), F32).at[:HY_EMB].set(lp["hy_w1"])
    row = lambda a: a.reshape(1, -1)
    small = lambda shape: pl.BlockSpec(shape, lambda j: (0,) * len(shape))
    fh = HY_FILTER_HIDDEN
    out = jax.ShapeDtypeStruct((n_len, HY_WIDTH), F32)
    return pl.pallas_call(
        functools.partial(_hy_filter_kernel, n_len=n_len),
        out_shape=(out, out, out),
        grid=(nblk,),
        in_specs=[
            small((n_len, LANES)), small((LANES, fh)), small((1, fh)), small((fh, fh)), small((1, fh)),
            small((fh, fh)), small((1, fh)),
            pl.BlockSpec((fh, LANES), lambda j: (0, j)),
            pl.BlockSpec((fh, LANES), lambda j: (0, nblk + j)),
            small((1, fh)),
            pl.BlockSpec((n_len, LANES), lambda j: (0, j)),
            _const_spec((n_len, n_len)), _const_spec((n_len, n_len)),
        ],
        out_specs=tuple(pl.BlockSpec((n_len, LANES), lambda j: (0, j)) for _ in range(3)),
        scratch_shapes=[pltpu.VMEM((n_len, fh), F32)],
        compiler_params=_cparams("arbitrary"),
        name="hyena_filters",
    )(z, w1, row(lp["hy_b1"]), lp["hy_w2"], row(lp["hy_b2"]), lp["hy_w3"], row(lp["hy_b3"]),
      lp["hy_w_filt"], lp["hy_w_filt"], row(lp["hy_freq"]), decay, cm, sf)


def _hy_pre_kernel(u0_ref, u1_ref, u2_ref, w0_ref, w1_ref, w2_ref, b0_ref, b1_ref, b2_ref,
                   x0_ref, vx_ref, vxb_ref, *, n_len):
    def sconv(u_ref, w_ref, b_ref):
        u = u_ref[...]
        w = w_ref[...]
        row = lax.broadcasted_iota(jnp.int32, u.shape, 0)
        prev = jnp.where(row == 0, 0.0, pltpu.roll(u, 1, axis=0))
        nxt = jnp.where(row == n_len - 1, 0.0, pltpu.roll(u, n_len - 1, axis=0))
        return w[0:1] * prev + w[1:2] * u + w[2:3] * nxt + b_ref[...]

    x0_ref[...] = sconv(u0_ref, w0_ref, b0_ref)
    vx = sconv(u2_ref, w2_ref, b2_ref) * sconv(u1_ref, w1_ref, b1_ref)
    vx_ref[...] = vx
    vxb_ref[...] = vx.astype(BF16)


def _hy_pre(hy, conv_w, conv_b, n_len):
    nblk = HY_WIDTH // LANES
    uspec = lambda part: pl.BlockSpec((None, n_len, LANES), lambda b, j: (b, 0, part * nblk + j))
    wspec = lambda part: pl.BlockSpec((3, LANES), lambda b, j: (0, part * nblk + j))
    bspec = lambda part: pl.BlockSpec((1, LANES), lambda b, j: (0, part * nblk + j))
    ospec = pl.BlockSpec((None, n_len, LANES), lambda b, j: (b, 0, j))
    hy3 = hy.reshape(BATCH, n_len, 3 * HY_WIDTH)
    cb = conv_b.reshape(1, -1)
    return pl.pallas_call(
        functools.partial(_hy_pre_kernel, n_len=n_len),
        out_shape=(jax.ShapeDtypeStruct((BATCH, n_len, HY_WIDTH), F32),
                   jax.ShapeDtypeStruct((BATCH, n_len, HY_WIDTH), F32),
                   jax.ShapeDtypeStruct((BATCH, n_len, HY_WIDTH), BF16)),
        grid=(BATCH, nblk),
        in_specs=[uspec(0), uspec(1), uspec(2), wspec(0), wspec(1), wspec(2), bspec(0), bspec(1), bspec(2)],
        out_specs=(ospec, ospec, ospec),
        compiler_params=_cparams("parallel", "parallel"),
        name="hyena_short_conv",
    )(hy3, hy3, hy3, conv_w, conv_w, conv_w, cb, cb, cb)


def _hy_fwd_kernel(cm_ref, sf_ref, u_ref, a_ref, b_ref, d_ref, q_ref):
    u = u_ref[...]
    ur = _dot(cm_ref[...], u)
    g = _dot(sf_ref[...], u)
    b = b_ref[...]
    q_ref[0] = (a_ref[...] * ur + b * g).astype(BF16)
    q_ref[1] = (d_ref[...] * g - b * ur).astype(BF16)


def _hy_inv_kernel(cm_ref, si_ref, q_ref, x0_ref, vx_ref, skip_ref, o_ref):
    y = _dot(cm_ref[...], q_ref[0]) + _dot(si_ref[...], q_ref[1])
    o_ref[...] = (x0_ref[...] * (y + vx_ref[...] * skip_ref[...])).astype(o_ref.dtype)


def _hy_long_conv(x0, vx, vxb, spectra, skip, n_len, tabs):
    cm, sf, si = tabs["dft"]
    a, b, d = spectra
    tf = min(512, n_len)
    nf = n_len // tf
    mat = pl.BlockSpec((tf, n_len), lambda bb, f: (f, 0))
    spec = pl.BlockSpec((tf, HY_WIDTH), lambda bb, f: (f, 0))
    q = pl.pallas_call(
        _hy_fwd_kernel,
        out_shape=jax.ShapeDtypeStruct((BATCH, 2, n_len, HY_WIDTH), BF16),
        grid=(BATCH, nf),
        in_specs=[mat, mat, pl.BlockSpec((None, n_len, HY_WIDTH), lambda bb, f: (bb, 0, 0)), spec, spec, spec],
        out_specs=pl.BlockSpec((None, 2, tf, HY_WIDTH), lambda bb, f: (bb, 0, f, 0)),
        compiler_params=_cparams("parallel", "arbitrary"),
        name="hyena_dft_fwd",
    )(cm, sf, vxb, a, b, d)
    tile = pl.BlockSpec((None, tf, HY_WIDTH), lambda bb, f: (bb, f, 0))
    return pl.pallas_call(
        _hy_inv_kernel,
        out_shape=jax.ShapeDtypeStruct((BATCH, n_len, HY_WIDTH), BF16),
        grid=(BATCH, nf),
        in_specs=[mat, mat, pl.BlockSpec((None, 2, n_len, HY_WIDTH), lambda bb, f: (bb, 0, 0, 0)),
                  tile, tile, pl.BlockSpec((1, HY_WIDTH), lambda bb, f: (0, 0))],
        out_specs=tile,
        compiler_params=_cparams("parallel", "arbitrary"),
        name="hyena_dft_inv",
    )(cm, si, q, x0, vx, skip.reshape(1, -1))


_SWA_Q = SWA_HEADS * SWA_HEAD_DIM
_SWA_KV = SWA_KV_HEADS * SWA_HEAD_DIM


def _swaproj_kernel(h_ref, w_ref, cos_ref, sin_ref, q_ref, k_ref, v_ref, *, rope):
    t = _dot(h_ref[...], w_ref[...])
    q = t[:, :_SWA_Q]
    k = t[:, 2 * _SWA_Q:2 * _SWA_Q + _SWA_KV]
    v = t[:, 2 * _SWA_Q + 2 * _SWA_KV:]
    if rope:
        cos = cos_ref[...]
        sin = sin_ref[...]
        q = q * cos + t[:, _SWA_Q:2 * _SWA_Q] * sin
        k = k * cos[:, :_SWA_KV] + t[:, 2 * _SWA_Q + _SWA_KV:2 * _SWA_Q + 2 * _SWA_KV] * sin[:, :_SWA_KV]
    q_ref[...] = (q * (SWA_HEAD_DIM ** -0.5)).astype(BF16)
    lo = lax.broadcasted_iota(jnp.int32, k.shape, 1) < SWA_HEAD_DIM
    for src, dst in ((k, k_ref), (v, v_ref)):
        sw = pltpu.roll(src, SWA_HEAD_DIM, axis=1)
        dst[:, 0 * LANES:1 * LANES] = jnp.where(lo, src, 0.0).astype(BF16)
        dst[:, 1 * LANES:2 * LANES] = jnp.where(lo, 0.0, sw).astype(BF16)
        dst[:, 2 * LANES:3 * LANES] = jnp.where(lo, sw, 0.0).astype(BF16)
        dst[:, 3 * LANES:4 * LANES] = jnp.where(lo, 0.0, src).astype(BF16)


def _swa_proj(h, w, cos_t, sin_t, rope, tm=512):
    m = h.shape[0]
    nt = cos_t.shape[0] // tm
    o = jax.ShapeDtypeStruct((m, 4 * LANES), BF16)
    ospec = pl.BlockSpec((tm, 4 * LANES), lambda i: (i, 0))
    tspec = pl.BlockSpec((tm, _SWA_Q), lambda i: (i % nt, 0))
    return pl.pallas_call(
        functools.partial(_swaproj_kernel, rope=rope),
        out_shape=(o, o, o),
        grid=(m // tm,),
        in_specs=[pl.BlockSpec((tm, D_MODEL), lambda i: (i, 0)), _const_spec(w.shape), tspec, tspec],
        out_specs=(ospec, ospec, ospec),
        compiler_params=_cparams("parallel"),
        name="swa_proj",
    )(h, w, cos_t, sin_t)


def _swa_kernel(sink_ref, q_ref, *refs, tq, win, lk, has_lat):
    if has_lat:
        kl_ref, vl_ref, kc_ref, vc_ref, o_ref = refs
        t0 = pl.program_id(1) * tq
        start = pl.multiple_of(jnp.clip(t0 - SWA_WINDOW, 0, lk - win), SWA_WINDOW)
        qpos = t0 + lax.broadcasted_iota(jnp.int32, (tq, win), 0)
        kpos = start + lax.broadcasted_iota(jnp.int32, (tq, win), 1)
        valid = jnp.abs(qpos - kpos) <= SWA_WINDOW
        valid = jnp.concatenate([valid, valid], axis=0)
    else:
        kc_ref, vc_ref, o_ref = refs
    upper = lax.broadcasted_iota(jnp.int32, (2 * tq, 1), 0) < tq
    for g in range(SWA_KV_HEADS):
        j0 = 2 * g
        qb = jnp.concatenate([q_ref[:, j0 * LANES:(j0 + 1) * LANES],
                              q_ref[:, (j0 + 1) * LANES:(j0 + 2) * LANES]], axis=0)
        acc = None
        for par in range(2):
            c0 = (2 * g + par) * LANES
            sk = jnp.where(upper, sink_ref[2 * j0 + par], sink_ref[2 * j0 + 2 + par])
            s_c = _dot_nt(qb, kc_ref[:, c0:c0 + LANES])
            m = jnp.maximum(s_c.max(axis=-1, keepdims=True), sk)
            if has_lat:
                s_l = _dot_nt(qb, kl_ref[pl.ds(start, win), c0:c0 + LANES])
                s_l = jnp.where(valid, s_l, NEG_BIG)
                m = jnp.maximum(m, s_l.max(axis=-1, keepdims=True))
            p_c = jnp.exp(s_c - m)
            den = p_c.sum(axis=-1, keepdims=True) + jnp.exp(sk - m)
            if has_lat:
                p_l = jnp.exp(s_l - m)
                den = den + p_l.sum(axis=-1, keepdims=True)
            o = _dot(p_c.astype(BF16), vc_ref[:, c0:c0 + LANES])
            if has_lat:
                o = o + _dot(p_l.astype(BF16), vl_ref[pl.ds(start, win), c0:c0 + LANES])
            o = o * (1.0 / den)
            acc = o if acc is None else acc + o
        o_ref[:, j0 * LANES:(j0 + 1) * LANES] = acc[:tq].astype(o_ref.dtype)
        o_ref[:, (j0 + 1) * LANES:(j0 + 2) * LANES] = acc[tq:].astype(o_ref.dtype)


def _swa_attention(sink, q, lat, ctx, lq, tq=256):
    nq = lq // tq
    win = tq + 2 * SWA_WINDOW
    full = lambda n: pl.BlockSpec((n, 4 * LANES), lambda b, i: (b, 0))
    in_specs = [pl.BlockSpec(memory_space=pltpu.SMEM),
                pl.BlockSpec((tq, 4 * LANES), lambda b, i: (b * nq + i, 0))]
    args = [sink, q]
    if lat is not None:
        in_specs += [full(SEQ), full(SEQ)]
        args += list(lat)
    in_specs += [full(CTX_LEN), full(CTX_LEN)]
    args += list(ctx)
    return pl.pallas_call(
        functools.partial(_swa_kernel, tq=tq, win=win, lk=SEQ, has_lat=lat is not None),
        out_shape=jax.ShapeDtypeStruct((BATCH * lq, 4 * LANES), BF16),
        grid=(BATCH, nq),
        in_specs=in_specs,
        out_specs=pl.BlockSpec((tq, 4 * LANES), lambda b, i: (b * nq + i, 0)),
        compiler_params=_cparams("parallel", "arbitrary"),
        name="swa_attention",
    )(*args)


def _out_kernel(a_ref, y_ref, s_ref, w_ref, x_ref, m_ref, g_ref, xo_ref, ho_ref):
    na = MLA_HEADS * MLA_V
    acc = _dot(a_ref[...], w_ref[0:na, :])
    acc = acc + _dot(y_ref[...], w_ref[na:na + HY_WIDTH, :])
    acc = acc + _dot(s_ref[...], w_ref[na + HY_WIDTH:, :])
    x = x_ref[...] + m_ref[2:3, :] * acc
    xo_ref[...] = x
    ho_ref[...] = (_rms(x, g_ref[...]) * (1.0 + m_ref[4:5, :]) + m_ref[3:4, :]).astype(ho_ref.dtype)


def _out_proj(a, y, s, w_out, x, mods, g_ffn, groups, row0, h_dtype, tm=512):
    m = x.shape[0]
    nr = m // groups // tm
    tile = lambda n: pl.BlockSpec((tm, n), lambda g_, i: (g_ * nr + i, 0))
    return pl.pallas_call(
        _out_kernel,
        out_shape=(jax.ShapeDtypeStruct((m, D_MODEL), F32), jax.ShapeDtypeStruct((m, D_MODEL), h_dtype)),
        grid=(groups, nr),
        in_specs=[tile(a.shape[1]), tile(y.shape[1]), tile(s.shape[1]), _const_spec(w_out.shape),
                  tile(D_MODEL), pl.BlockSpec((None, 6, D_MODEL), lambda g_, i: (row0 + g_, 0, 0)),
                  pl.BlockSpec((1, D_MODEL), lambda g_, i: (0, 0))],
        out_specs=(tile(D_MODEL), tile(D_MODEL)),
        compiler_params=_cparams("parallel", "parallel"),
        name="out_proj",
    )(a, y, s, w_out, x, mods, g_ffn.reshape(1, -1))


_META_W1, _META_W2, _META_E1, _META_E2, _META_R1, _META_R2 = range(6)


def _router_kernel(h_ref, w_ref, meta_ref, cnt_ref, carry_ref):
    @pl.when(pl.program_id(0) == 0)
    def _():
        carry_ref[...] = jnp.zeros_like(carry_ref)

    logits = _dot3(h_ref[...], w_ref[...])
    tm = logits.shape[0]
    lane = lax.broadcasted_iota(jnp.int32, logits.shape, 1).astype(F32)
    logits = jnp.where(lane < N_EXPERTS, logits, NEG_BIG)
    m1 = logits.max(axis=-1, keepdims=True)
    i1 = jnp.where(logits == m1, lane, float(LANES)).min(axis=-1, keepdims=True)
    rest = jnp.where(lane == i1, NEG_BIG, logits)
    m2 = rest.max(axis=-1, keepdims=True)
    i2 = jnp.where(rest == m2, lane, float(LANES)).min(axis=-1, keepdims=True)
    e2 = jnp.exp(m2 - m1)
    w1 = 1.0 / (1.0 + e2)
    hot = jnp.where((lane == i1) | (lane == i2), 1.0, 0.0)
    before = (lax.broadcasted_iota(jnp.int32, (tm, tm), 0) > lax.broadcasted_iota(jnp.int32, (tm, tm), 1))
    seen = _dot(before.astype(F32).astype(BF16), hot.astype(BF16)) + carry_ref[0:1, :]
    r1 = jnp.where(lane == i1, seen, 0.0).sum(axis=-1, keepdims=True)
    r2 = jnp.where(lane == i2, seen, 0.0).sum(axis=-1, keepdims=True)
    carry_ref[...] = carry_ref[...] + hot.sum(axis=0, keepdims=True)
    cnt_ref[...] = carry_ref[...]
    rec = jnp.zeros_like(logits)
    for k, v in ((_META_W1, w1), (_META_W2, e2 * w1), (_META_E1, i1), (_META_E2, i2), (_META_R1, r1), (_META_R2, r2)):
        rec = jnp.where(lane == float(k), v, rec)
    meta_ref[...] = rec


def _router(h, w_router, tm=512):
    m = h.shape[0]
    wp = jnp.zeros((D_MODEL, LANES), F32).at[:, :N_EXPERTS].set(w_router)
    return pl.pallas_call(
        _router_kernel,
        out_shape=(jax.ShapeDtypeStruct((m, LANES), F32), jax.ShapeDtypeStruct((8, LANES), F32)),
        grid=(m // tm,),
        in_specs=[pl.BlockSpec((tm, D_MODEL), lambda i: (i, 0)),
                  pl.BlockSpec((D_MODEL, LANES), lambda i: (0, 0))],
        out_specs=(pl.BlockSpec((tm, LANES), lambda i: (i, 0)), pl.BlockSpec((8, LANES), lambda i: (0, 0))),
        scratch_shapes=[pltpu.VMEM((8, LANES), F32)],
        compiler_params=_cparams("arbitrary"),
        name="moe_router",
    )(h, wp)


MOE_TM = 1152
MOE_BLK = 128
MOE_ROWS = (MOE_TM, 1024, 896)


def _moe_plan(meta, counts, m):
    n_steps = -(-2 * m // MOE_TM) + N_EXPERTS
    cnt = counts[0, :N_EXPERTS].astype(jnp.int32)
    steps_e = (cnt + MOE_TM - 1) // MOE_TM
    ends = jnp.cumsum(steps_e)
    first = ends - steps_e
    total = ends[-1]
    e = meta[:, _META_E1:_META_E2 + 1].astype(jnp.int32)
    rank = meta[:, _META_R1:_META_R2 + 1].astype(jnp.int32)
    pos = (first * MOE_TM)[e] + rank
    s_idx = jnp.arange(n_steps, dtype=jnp.int32)
    step_e = jnp.minimum(jnp.searchsorted(ends, s_idx, side="right").astype(jnp.int32), N_EXPERTS - 1)
    valid = jnp.clip(cnt[step_e] - (s_idx - first[step_e]) * MOE_TM, 0, MOE_TM)
    valid = jnp.where(s_idx < total, valid, 0)
    return pos.reshape(-1), step_e, valid, n_steps


def _row_copy(src_ref, src_row, dst_ref, dst_row, sem):
    return pltpu.make_async_copy(src_ref.at[pl.ds(src_row, 1)], dst_ref.at[pl.ds(dst_row, 1)], sem)


def _invert_kernel(pos_ref, sv_ref, src_ref):
    def clear_step(s, c):
        def clear(i, c2):
            src_ref[s * MOE_TM + i] = 0
            return c2

        return lax.fori_loop(sv_ref[s], MOE_TM, clear, c)

    def place(t, c):
        p0 = pos_ref[2 * t]
        p1 = pos_ref[2 * t + 1]
        src_ref[p0] = t
        src_ref[p1] = t
        return c

    lax.fori_loop(0, sv_ref.shape[0], clear_step, 0)
    lax.fori_loop(0, pos_ref.shape[0] // 2, place, 0, unroll=8)


def _moe_invert(pos, valid, n_rows):
    smem = pl.BlockSpec(memory_space=pltpu.SMEM)
    return pl.pallas_call(
        _invert_kernel,
        out_shape=jax.ShapeDtypeStruct((n_rows,), jnp.int32),
        in_specs=[smem, smem], out_specs=smem,
        name="moe_invert",
    )(pos, valid)


def _block_wait(src_ref, dst_ref, rows, sem):
    pltpu.make_async_copy(src_ref.at[pl.ds(0, rows)], dst_ref.at[pl.ds(0, rows)], sem).wait()


def _moe_ffn_kernel(se_ref, sv_ref, src_ref, h_ref, wg_ref, wu_ref, wd_ref, o_ref, x32_ref, xb_ref, sem):
    del se_ref
    s = pl.program_id(0)
    f = pl.program_id(1)
    valid = sv_ref[s]

    def rows_used(v):
        r = jnp.where(v > 0, MOE_ROWS[-1], 0)
        for small, big in zip(MOE_ROWS[:0:-1], MOE_ROWS[-2::-1]):
            r = jnp.where(v > small, big, r)
        return r

    def blocks(v):
        return rows_used(v) // MOE_BLK

    def gather(step):
        base = step * MOE_TM

        def issue(i, c):
            _row_copy(h_ref, src_ref[base + i], x32_ref, i, sem).start()
            return c

        lax.fori_loop(0, rows_used(sv_ref[step]), issue, 0)

    @pl.when(f == 0)
    def _():
        @pl.when(s == 0)
        def _():
            gather(0)

        def land(i, c):
            _block_wait(h_ref, x32_ref, MOE_BLK, sem)
            return c

        def to_bf16(i, c):
            r0 = pl.multiple_of(i * MOE_BLK, MOE_BLK)
            xb_ref[pl.ds(r0, MOE_BLK), :] = x32_ref[pl.ds(r0, MOE_BLK), :].astype(BF16)
            return c

        lax.fori_loop(0, blocks(valid), land, 0)
        lax.fori_loop(0, blocks(valid), to_bf16, 0)

        @pl.when(s + 1 < pl.num_programs(0))
        def _():
            gather(s + 1)

        o_ref[...] = jnp.zeros_like(o_ref)

    def compute(rows):
        xs = xb_ref[0:rows, :]
        mid = _silu(_dot(xs, wg_ref[...].astype(BF16))) * _dot(xs, wu_ref[...].astype(BF16))
        o_ref[0:rows, :] += _dot(mid.astype(BF16), wd_ref[...].astype(BF16))

    for k, rows in enumerate(MOE_ROWS):
        lower = MOE_ROWS[k + 1] if k + 1 < len(MOE_ROWS) else 0

        @pl.when((valid > lower) & (valid <= rows))
        def _():
            compute(rows)


def _moe_ffn(h, src, w_gate, w_up, w_down, step_e, valid, n_steps, tf=512):
    assert all(r % MOE_BLK == 0 for r in MOE_ROWS) and MOE_ROWS[0] == MOE_TM
    n_ff = w_gate.shape[-1]
    nf = n_ff // tf
    fidx = lambda s, f, sv: jnp.where(sv[s] > 0, f, nf - 1)
    return pl.pallas_call(
        _moe_ffn_kernel,
        out_shape=jax.ShapeDtypeStruct((n_steps * MOE_TM, D_MODEL), F32),
        grid_spec=pltpu.PrefetchScalarGridSpec(
            num_scalar_prefetch=3, grid=(n_steps, nf),
            in_specs=[
                pl.BlockSpec(memory_space=pl.ANY),
                pl.BlockSpec((None, D_MODEL, tf), lambda s, f, se, sv, sr: (se[s], 0, fidx(s, f, sv))),
                pl.BlockSpec((None, D_MODEL, tf), lambda s, f, se, sv, sr: (se[s], 0, fidx(s, f, sv))),
                pl.BlockSpec((None, tf, D_MODEL), lambda s, f, se, sv, sr: (se[s], fidx(s, f, sv), 0)),
            ],
            out_specs=pl.BlockSpec((MOE_TM, D_MODEL), lambda s, f, se, sv, sr: (s, 0), pipeline_mode=pl.Buffered(1)),
            scratch_shapes=[pltpu.VMEM((MOE_TM, D_MODEL), F32), pltpu.VMEM((MOE_TM, D_MODEL), BF16),
                            pltpu.SemaphoreType.DMA(())]),
        compiler_params=_cparams("arbitrary", "arbitrary"),
        name="moe_experts",
    )(step_e, valid, src, h, w_gate, w_up, w_down)


def _combine_kernel(pos_ref, meta_ref, x_ref, m_ref, gf_ref, ys_ref, o_ref, buf_ref, sem, *, tm, final):
    base = pl.program_id(0) * tm

    def issue(i, c):
        for slot in range(2):
            _row_copy(ys_ref, pos_ref[2 * (base + i) + slot], buf_ref.at[slot], i, sem).start()
        return c

    lax.fori_loop(0, tm, issue, 0, unroll=4)
    for slot in range(2):
        _block_wait(ys_ref, buf_ref.at[slot], tm, sem)
    meta = meta_ref[...]
    y = meta[:, _META_W1:_META_W1 + 1] * buf_ref[0] + meta[:, _META_W2:_META_W2 + 1] * buf_ref[1]
    x = x_ref[...] + m_ref[5:6, :] * y
    if final:
        x = _rms(x, gf_ref[...])
    o_ref[...] = x


def _moe_combine(pos, meta, x, mods, ys, groups, row0, g_final, tm=512):
    m = x.shape[0]
    nr = m // groups // tm
    final = g_final is not None
    gf = (g_final if final else jnp.ones((D_MODEL,), F32)).reshape(1, D_MODEL)
    return pl.pallas_call(
        functools.partial(_combine_kernel, tm=tm, final=final),
        out_shape=jax.ShapeDtypeStruct((m, D_MODEL), F32),
        grid_spec=pltpu.PrefetchScalarGridSpec(
            num_scalar_prefetch=1, grid=(m // tm,),
            in_specs=[pl.BlockSpec((tm, LANES), lambda i, p: (i, 0)),
                      pl.BlockSpec((tm, D_MODEL), lambda i, p: (i, 0)),
                      pl.BlockSpec((None, 6, D_MODEL), lambda i, p: (row0 + i // nr, 0, 0)),
                      pl.BlockSpec((1, D_MODEL), lambda i, p: (0, 0)),
                      pl.BlockSpec(memory_space=pl.ANY)],
            out_specs=pl.BlockSpec((tm, D_MODEL), lambda i, p: (i, 0)),
            scratch_shapes=[pltpu.VMEM((2, tm, D_MODEL), F32), pltpu.SemaphoreType.DMA(())]),
        compiler_params=_cparams("arbitrary"),
        name="moe_combine",
    )(pos, meta, x, mods, gf, ys)


def _moe(h, x, mods, groups, row0, w_router, w_gate, w_up, w_down, g_final):
    m = h.shape[0]
    meta, counts = _router(h, w_router)
    pos, step_e, valid, n_steps = _moe_plan(meta, counts, m)
    src = _moe_invert(pos, valid, n_steps * MOE_TM)
    ys = _moe_ffn(h, src, w_gate, w_up, w_down, step_e, valid, n_steps)
    return _moe_combine(pos, meta, x, mods, ys, groups, row0, g_final)


def _ffn_kernel(h_ref, wg_ref, wu_ref, wd_ref, x_ref, m_ref, gn_ref, mn_ref, o_ref, *maybe_hn_ref):
    f = pl.program_id(2)

    @pl.when(f == 0)
    def _():
        o_ref[...] = jnp.zeros_like(o_ref)

    h = h_ref[...]
    mid = _silu(_dot(h, wg_ref[...])) * _dot(h, wu_ref[...])
    o_ref[...] += _dot(mid.astype(BF16), wd_ref[...])

    @pl.when(f == pl.num_programs(2) - 1)
    def _():
        x = x_ref[...] + m_ref[5:6, :] * o_ref[...]
        o_ref[...] = x
        for hn_ref in maybe_hn_ref:
            hn_ref[...] = (_rms(x, gn_ref[...]) * (1.0 + mn_ref[1:2, :]) + mn_ref[0:1, :]).astype(hn_ref.dtype)


def _ffn(h, w_gate, w_up, w_down, x, mods, groups, row0, nxt, tm, tf=512):
    m = x.shape[0]
    n_ff = w_gate.shape[1]
    nr = m // groups // tm
    g_next, mods_next = nxt if nxt is not None else (jnp.ones((D_MODEL,), F32), mods)
    once = pl.Buffered(1)
    tile = lambda n, mode=None: pl.BlockSpec((tm, n), lambda g_, i, f: (g_ * nr + i, 0), pipeline_mode=mode)
    mod_spec = pl.BlockSpec((None, 6, D_MODEL), lambda g_, i, f: (row0 + g_, 0, 0))
    out_shape = [jax.ShapeDtypeStruct((m, D_MODEL), F32)]
    out_specs = [tile(D_MODEL, once)]
    if nxt is not None:
        out_shape.append(jax.ShapeDtypeStruct((m, D_MODEL), BF16))
        out_specs.append(tile(D_MODEL, once))
    outs = pl.pallas_call(
        _ffn_kernel,
        out_shape=tuple(out_shape),
        grid=(groups, nr, n_ff // tf),
        in_specs=[tile(D_MODEL),
                  pl.BlockSpec((D_MODEL, tf), lambda g_, i, f: (0, f)),
                  pl.BlockSpec((D_MODEL, tf), lambda g_, i, f: (0, f)),
                  pl.BlockSpec((tf, D_MODEL), lambda g_, i, f: (f, 0)),
                  tile(D_MODEL, once), mod_spec,
                  pl.BlockSpec((1, D_MODEL), lambda g_, i, f: (0, 0)), mod_spec],
        out_specs=tuple(out_specs),
        compiler_params=_cparams("parallel", "parallel", "arbitrary"),
        name="swiglu_ffn",
    )(h, w_gate, w_up, w_down, x, mods, g_next.reshape(1, D_MODEL), mods_next)
    return outs if nxt is not None else (outs[0], None)


def _swap_halves(w):
    half = w.shape[-1] // 2
    return jnp.concatenate([w[..., half:], w[..., :half]], axis=-1)


def _swap_heads(w, heads, dim):
    k = w.shape[0]
    return _swap_halves(w.reshape(k, heads, dim)).reshape(k, heads * dim)


def _rope_tables():
    rows = SEQ // GRID_W
    row = np.repeat(np.arange(rows), GRID_W).astype(np.float32)
    col = np.tile(np.arange(GRID_W), rows).astype(np.float32)
    quarter = MLA_ROPE // 4
    freqs = (np.float32(ROPE_THETA) ** (-np.arange(quarter, dtype=np.float32) / quarter)).astype(np.float32)
    ang = np.concatenate([row[:, None] * freqs[None], col[:, None] * freqs[None]], axis=-1)
    cos = np.cos(ang.astype(np.float64))
    sin = np.sin(ang.astype(np.float64))
    cos2 = np.concatenate([cos, cos], axis=-1)
    sin2 = np.concatenate([-sin, sin], axis=-1)
    tab_lat = np.concatenate([cos2, sin2], axis=-1)
    tab_ctx = np.concatenate([np.ones((CTX_LEN, MLA_ROPE)), np.zeros((CTX_LEN, MLA_ROPE))], axis=-1)
    return dict(
        tab_lat=jnp.asarray(tab_lat, F32), tab_ctx=jnp.asarray(tab_ctx, F32),
        cos8=jnp.asarray(np.tile(cos2, (1, SWA_HEADS)), F32),
        sin8=jnp.asarray(np.tile(sin2, (1, SWA_HEADS)), F32),
        ones8=jnp.ones((CTX_LEN, _SWA_Q), F32), zeros8=jnp.zeros((CTX_LEN, _SWA_Q), F32),
    )


def _layer_weights(p, l):
    w_in = p["w_in"][l]
    cq = w_in[:, _O_CQ:_O_CKV]
    ckv = w_in[:, _O_CKV:_O_KPE]
    kpe = w_in[:, _O_KPE:_O_HY]
    hy = w_in[:, _O_HY:_O_SQ]
    sq = w_in[:, _O_SQ:_O_SK]
    sk = w_in[:, _O_SK:_O_SV]
    sv = w_in[:, _O_SV:]
    dq = MLA_NOPE + MLA_ROPE
    wq = p["w_q_up"][l].reshape(MLA_Q_LORA, MLA_HEADS, dq)
    wq = jnp.concatenate([wq, _swap_halves(wq[..., MLA_NOPE:])], axis=-1)
    wkv = p["w_kv_up"][l].reshape(MLA_KV_LORA, MLA_HEADS, MLA_NOPE + MLA_V)
    return dict(
        w_cq=cq.astype(BF16),
        w_ckv=jnp.concatenate([ckv, kpe, _swap_halves(kpe)], axis=-1).astype(BF16),
        w_hy=hy.astype(BF16),
        w_swa=jnp.concatenate([sq, _swap_heads(sq, SWA_HEADS, SWA_HEAD_DIM), sk,
                               _swap_heads(sk, SWA_KV_HEADS, SWA_HEAD_DIM), sv], axis=-1).astype(BF16),
        w_q=jnp.transpose(wq, (1, 0, 2)).astype(BF16),
        w_kv=jnp.transpose(wkv, (1, 0, 2)).astype(BF16),
        w_out=p["w_out"][l].astype(BF16),
        g_q=p["g_q"][l], g_kv=p["g_kv"][l],
        hy_conv_w=p["hy_conv_w"][l], hy_conv_b=p["hy_conv_b"][l],
        hy_w1=p["hy_w1"][l], hy_b1=p["hy_b1"][l], hy_w2=p["hy_w2"][l], hy_b2=p["hy_b2"][l],
        hy_w3=p["hy_w3"][l], hy_b3=p["hy_b3"][l], hy_w_filt=p["hy_w_filt"][l],
        hy_freq=p["hy_freq"][l], hy_skip=p["hy_skip"][l], swa_sink=p["swa_sink"][l],
    )


def _hyena(h, lw, n_len, tabs):
    hy = _matmul(h, lw["w_hy"], F32)
    x0, vx, vxb = _hy_pre(hy, lw["hy_conv_w"], lw["hy_conv_b"], n_len)
    spectra = _hy_filters(lw, n_len, tabs)
    y = _hy_long_conv(x0, vx, vxb, spectra, lw["hy_skip"], n_len, tabs)
    return y.reshape(BATCH * n_len, HY_WIDTH)


def _mixer(h_lat, h_ctx, lw, rope, hy_tabs, need_ctx):
    k_l, v_l = _kv_proj(h_lat, lw["w_ckv"], lw["g_kv"], lw["w_kv"], rope["tab_lat"])
    k_c, v_c = _kv_proj(h_ctx, lw["w_ckv"], lw["g_kv"], lw["w_kv"], rope["tab_ctx"], tm=CTX_LEN)
    q_l = _q_proj(h_lat, lw["w_cq"], lw["g_q"], lw["w_q"], rope["tab_lat"])
    a_l = _mla_attention(q_l, [(k_l, v_l, SEQ), (k_c, v_c, CTX_LEN)], SEQ)
    y_l = _hyena(h_lat, lw, SEQ, hy_tabs[SEQ])
    sq_l, kk_l, vv_l = _swa_proj(h_lat, lw["w_swa"], rope["cos8"], rope["sin8"], True)
    sq_c, kk_c, vv_c = _swa_proj(h_ctx, lw["w_swa"], rope["ones8"], rope["zeros8"], False, tm=CTX_LEN)
    s_l = _swa_attention(lw["swa_sink"], sq_l, (kk_l, vv_l), (kk_c, vv_c), SEQ)
    if not need_ctx:
        return (a_l, y_l, s_l), None
    q_c = _q_proj(h_ctx, lw["w_cq"], lw["g_q"], lw["w_q"], rope["tab_ctx"], tm=CTX_LEN)
    a_c = _mla_attention(q_c, [(k_c, v_c, CTX_LEN)], CTX_LEN, tq=CTX_LEN)
    y_c = _hyena(h_ctx, lw, CTX_LEN, hy_tabs[CTX_LEN])
    s_c = _swa_attention(lw["swa_sink"], sq_c, None, (kk_c, vv_c), CTX_LEN, tq=CTX_LEN)
    return (a_l, y_l, s_l), (a_c, y_c, s_c)


def _forward(p):
    rope = _rope_tables()
    hy_tabs = {n: dict(dft=_dft_tables(n), filt=_filter_tables(n)) for n in (SEQ, CTX_LEN)}
    cvec = jnp.zeros((MOD_ROWS, D_MODEL), F32).at[:BATCH].set(p["c"]).at[BATCH].set(p["c_ctx"])
    mods_all = _modulation(cvec, p["w_mod"], p["b_mod"]).reshape(DEPTH, MOD_ROWS, 6, D_MODEL)

    x_lat = p["x"].reshape(BATCH * SEQ, D_MODEL)
    x_ctx = p["ctx"].reshape(BATCH * CTX_LEN, D_MODEL)
    h_in = [None, None]
    for l in range(DEPTH):
        last = l == DEPTH - 1
        mods = mods_all[l]
        lw = _layer_weights(p, l)
        h_lat = h_in[0] if h_in[0] is not None else _norm_mod(x_lat, p["g_mix"][l], mods, BATCH, 0, 0, BF16)
        h_ctx = h_in[1] if h_in[1] is not None else _norm_mod(x_ctx, p["g_mix"][l], mods, 1, BATCH, 0, BF16)
        mix_l, mix_c = _mixer(h_lat, h_ctx, lw, rope, hy_tabs, not last)
        streams = [(x_lat, mix_l, BATCH, 0, 1024)]
        if not last:
            streams.append((x_ctx, mix_c, 1, BATCH, 512))
        outs = []
        h_in = [None, None]
        i = l // 2
        dense = l % 2 == 0
        for n, (x, mix, groups, row0, ffn_tm) in enumerate(streams):
            x, h2 = _out_proj(*mix, lw["w_out"], x, mods, p["g_ffn"][l], groups, row0, BF16 if dense else F32)
            if dense:
                nxt = None if last else (p["g_mix"][l + 1], mods_all[l + 1])
                x, h_in[n] = _ffn(h2, p["ffn_w_gate"][i].astype(BF16), p["ffn_w_up"][i].astype(BF16),
                                  p["ffn_w_down"][i].astype(BF16), x, mods, groups, row0, nxt, ffn_tm)
            else:
                g_final = p["g_final"] if (last and n == 0) else None
                x = _moe(h2, x, mods, groups, row0, p["moe_router"][i], p["moe_w_gate"][i], p["moe_w_up"][i],
                         p["moe_w_down"][i], g_final)
            outs.append(x)
        x_lat = outs[0]
        if not last:
            x_ctx = outs[1]
    if DEPTH % 2 == 1:
        x_lat = _norm_mod(x_lat, p["g_final"], mods_all[0], BATCH, 0, None, F32)
    return x_lat.reshape(BATCH, SEQ, D_MODEL)


def kernel(x, c, ctx, c_ctx, w_mod, b_mod, g_mix, g_ffn, w_in, g_q, w_q_up, g_kv, w_kv_up, hy_conv_w, hy_conv_b, hy_w1, hy_b1, hy_w2, hy_b2, hy_w3, hy_b3, hy_w_filt, hy_freq, hy_skip, swa_sink, w_out, ffn_w_gate, ffn_w_up, ffn_w_down, moe_router, moe_w_gate, moe_w_up, moe_w_down, g_final):
    return _forward(dict(
        x=x, c=c, ctx=ctx, c_ctx=c_ctx, w_mod=w_mod, b_mod=b_mod, g_mix=g_mix, g_ffn=g_ffn, w_in=w_in,
        g_q=g_q, w_q_up=w_q_up, g_kv=g_kv, w_kv_up=w_kv_up, hy_conv_w=hy_conv_w, hy_conv_b=hy_conv_b,
        hy_w1=hy_w1, hy_b1=hy_b1, hy_w2=hy_w2, hy_b2=hy_b2, hy_w3=hy_w3, hy_b3=hy_b3,
        hy_w_filt=hy_w_filt, hy_freq=hy_freq, hy_skip=hy_skip, swa_sink=swa_sink, w_out=w_out,
        ffn_w_gate=ffn_w_gate, ffn_w_up=ffn_w_up, ffn_w_down=ffn_w_down, moe_router=moe_router,
        moe_w_gate=moe_w_gate, moe_w_up=moe_w_up, moe_w_down=moe_w_down, g_final=g_final))
```

```python
import functools
import math

import numpy as np
import jax
import jax.numpy as jnp
from jax import lax
from jax.experimental import pallas as pl
from jax.experimental.pallas import tpu as pltpu

F32 = jnp.float32
BF16 = jnp.bfloat16

D_MODEL = 2048
BATCH = 4
SEQ = 2048
DEPTH = 2
CTX_LEN = 256
GRID_W = 64
NORM_EPS = 1e-6
ROPE_THETA = 10000.0
MLA_HEADS = 8
MLA_NOPE = 128
MLA_ROPE = 64
MLA_V = 128
MLA_Q_LORA = 768
MLA_KV_LORA = 512
HY_WIDTH = 512
HY_BANDS = 16
HY_EMB = 1 + 2 * HY_BANDS
HY_FILTER_HIDDEN = 64
HY_DECAY_TARGET = 1e-2
HY_QUICK_DECAY_PCT = 0.3
HY_SLOW_DECAY_PCT = 1.5
SWA_HEADS = 8
SWA_KV_HEADS = 2
SWA_HEAD_DIM = 64
SWA_WINDOW = 128
N_EXPERTS = 8
D_FF = 5632
D_FF_EXPERT = 7168

LANES = 128
MXU_DIM = 256
VMEM_LIMIT_BYTES = 56 * 1024 * 1024
NEG_BIG = -1e30
MOD_ROWS = 8

_O_CQ = 0
_O_CKV = _O_CQ + MLA_Q_LORA
_O_KPE = _O_CKV + MLA_KV_LORA
_O_HY = _O_KPE + MLA_ROPE
_O_SQ = _O_HY + 3 * HY_WIDTH
_O_SK = _O_SQ + SWA_HEADS * SWA_HEAD_DIM
_O_SV = _O_SK + SWA_KV_HEADS * SWA_HEAD_DIM


def _cparams(*sem):
    return pltpu.CompilerParams(dimension_semantics=sem, vmem_limit_bytes=VMEM_LIMIT_BYTES)


def _dot(a, b):
    return jnp.dot(a, b, preferred_element_type=F32)


def _dot_nt(a, b):
    return lax.dot_general(a, b, (((1,), (1,)), ((), ())), preferred_element_type=F32)


def _split_bf16(a):
    hi = a.astype(BF16)
    lo = (a - hi.astype(F32)).astype(BF16)
    return hi, lo


def _dot3(a, b):
    ah, al = _split_bf16(a)
    bh, bl = _split_bf16(b)
    return _dot(ah, bh) + (_dot(al, bh) + _dot(ah, bl))


def _silu(x):
    return x / (1.0 + jnp.exp(-x))


def _rms(x, g):
    ms = jnp.mean(x * x, axis=-1, keepdims=True)
    return x * lax.rsqrt(ms + NORM_EPS) * g


def _const_spec(shape):
    nd = len(shape)
    return pl.BlockSpec(shape, lambda *_: (0,) * nd, pipeline_mode=pl.Buffered(1))


def _mod_kernel(c_ref, w_ref, b_ref, o_ref):
    a = _silu(c_ref[...])
    o_ref[0] = _dot3(a, w_ref[0]) + b_ref[0]


def _modulation(cvec, w_mod, b_mod):
    tn = 1024
    n = 6 * D_MODEL
    return pl.pallas_call(
        _mod_kernel,
        out_shape=jax.ShapeDtypeStruct((DEPTH, MOD_ROWS, n), F32),
        grid=(DEPTH, n // tn),
        in_specs=[
            pl.BlockSpec((MOD_ROWS, D_MODEL), lambda l, j: (0, 0)),
            pl.BlockSpec((1, D_MODEL, tn), lambda l, j: (l, 0, j)),
            pl.BlockSpec((1, 1, tn), lambda l, j: (l, 0, j)),
        ],
        out_specs=pl.BlockSpec((1, MOD_ROWS, tn), lambda l, j: (l, 0, j)),
        compiler_params=_cparams("arbitrary", "arbitrary"),
        name="adaln_mod",
    )(cvec, w_mod, b_mod.reshape(DEPTH, 1, n))


def _norm_kernel(x_ref, g_ref, m_ref, o_ref, *, si):
    y = _rms(x_ref[...], g_ref[...])
    if si is not None:
        y = y * (1.0 + m_ref[si + 1:si + 2, :]) + m_ref[si:si + 1, :]
    o_ref[...] = y.astype(o_ref.dtype)


def _norm_mod(x, g, mods, groups, row0, si, out_dtype, tm=512):
    m = x.shape[0]
    nr = m // groups // tm
    return pl.pallas_call(
        functools.partial(_norm_kernel, si=si),
        out_shape=jax.ShapeDtypeStruct((m, D_MODEL), out_dtype),
        grid=(groups, nr),
        in_specs=[
            pl.BlockSpec((tm, D_MODEL), lambda g_, i: (g_ * nr + i, 0)),
            pl.BlockSpec((1, D_MODEL), lambda g_, i: (0, 0)),
            pl.BlockSpec((None, 6, D_MODEL), lambda g_, i: (row0 + g_, 0, 0)),
        ],
        out_specs=pl.BlockSpec((tm, D_MODEL), lambda g_, i: (g_ * nr + i, 0)),
        compiler_params=_cparams("parallel", "parallel"),
        name="norm_mod",
    )(x, g.reshape(1, D_MODEL), mods)


def _rope_halves(pe_pair, tab):
    r = pe_pair * tab
    return r + pltpu.roll(r, MLA_ROPE, axis=1)


MLA_Q_SCALE = (MLA_NOPE + MLA_ROPE) ** -0.5 * math.log2(math.e)


def _qproj_body(h_ref, wc_ref, g_ref, wq_ref, tab_ref, o_ref):
    cq = _dot(h_ref[...], wc_ref[...])
    cqn = _rms(cq, g_ref[...]).astype(BF16)
    tab = tab_ref[...] * MLA_Q_SCALE
    for hh in range(MLA_HEADS):
        r = _dot(cqn, wq_ref[hh])
        o_ref[hh, :, 0:MLA_NOPE] = (r[:, :MLA_NOPE] * MLA_Q_SCALE).astype(BF16)
        o_ref[hh, :, MLA_NOPE:] = _rope_halves(r[:, MLA_NOPE:], tab).astype(BF16)


def _mla_proj_kernel(h_ref, wcq_ref, gq_ref, wq_ref, wckv_ref, gkv_ref, wkv_ref, tab_ref, q_ref, k_ref, v_ref):
    _qproj_body(h_ref, wcq_ref, gq_ref, wq_ref, tab_ref, q_ref)
    _kvproj_kernel(h_ref, wckv_ref, gkv_ref, wkv_ref, tab_ref, k_ref, v_ref)


def _mla_proj(h, lw, tab, tm=512):
    m = h.shape[0]
    nt = tab.shape[0] // tm
    head_out = jax.ShapeDtypeStruct((MLA_HEADS, m, MXU_DIM), BF16)
    head_spec = pl.BlockSpec((MLA_HEADS, tm, MXU_DIM), lambda i: (0, i, 0))
    return pl.pallas_call(
        _mla_proj_kernel,
        out_shape=(head_out, head_out, head_out),
        grid=(m // tm,),
        in_specs=[
            pl.BlockSpec((tm, D_MODEL), lambda i: (i, 0)),
            _const_spec(lw["w_cq"].shape), _const_spec((1, MLA_Q_LORA)), _const_spec(lw["w_q"].shape),
            _const_spec(lw["w_ckv"].shape), _const_spec((1, MLA_KV_LORA)), _const_spec(lw["w_kv"].shape),
            pl.BlockSpec((tm, LANES), lambda i: (i % nt, 0)),
        ],
        out_specs=(head_spec, head_spec, head_spec),
        compiler_params=_cparams("parallel"),
        name="mla_proj",
    )(h, lw["w_cq"], lw["g_q"].reshape(1, -1), lw["w_q"], lw["w_ckv"], lw["g_kv"].reshape(1, -1), lw["w_kv"], tab)


def _kvproj_kernel(h_ref, wc_ref, g_ref, wkv_ref, tab_ref, k_ref, v_ref):
    t = _dot(h_ref[...], wc_ref[...])
    ckvn = _rms(t[:, :MLA_KV_LORA], g_ref[...]).astype(BF16)
    rot = _rope_halves(t[:, MLA_KV_LORA:], tab_ref[...])
    lane = lax.broadcasted_iota(jnp.int32, rot.shape, 1)
    kpe = jnp.where(lane < MLA_ROPE, rot, 0.0).astype(BF16)
    for hh in range(MLA_HEADS):
        r = _dot(ckvn, wkv_ref[hh])
        k_ref[hh, :, 0:MLA_NOPE] = r[:, :MLA_NOPE].astype(BF16)
        k_ref[hh, :, MLA_NOPE:] = kpe
        v_ref[hh, :, 0:MLA_V] = r[:, MLA_NOPE:].astype(BF16)
        v_ref[hh, :, MLA_V:] = jnp.ones((r.shape[0], MXU_DIM - MLA_V), BF16)


def _kv_proj(h, w_ckv, g_kv, w_kv, tab, tm=512):
    m = h.shape[0]
    nt = tab.shape[0] // tm
    return pl.pallas_call(
        _kvproj_kernel,
        out_shape=(jax.ShapeDtypeStruct((MLA_HEADS, m, MXU_DIM), BF16),
                   jax.ShapeDtypeStruct((MLA_HEADS, m, MXU_DIM), BF16)),
        grid=(m // tm,),
        in_specs=[
            pl.BlockSpec((tm, D_MODEL), lambda i: (i, 0)),
            _const_spec(w_ckv.shape),
            _const_spec((1, MLA_KV_LORA)),
            _const_spec(w_kv.shape),
            pl.BlockSpec((tm, LANES), lambda i: (i % nt, 0)),
        ],
        out_specs=(pl.BlockSpec((MLA_HEADS, tm, MXU_DIM), lambda i: (0, i, 0)),
                   pl.BlockSpec((MLA_HEADS, tm, MXU_DIM), lambda i: (0, i, 0))),
        compiler_params=_cparams("parallel"),
        name="mla_kv_proj",
    )(h, w_ckv, g_kv.reshape(1, -1), w_kv, tab)


MLA_KEY_CHUNK = 512


def _mla_kernel(q_ref, *refs, seg_lens):
    nseg = len(seg_lens)
    o_ref = refs[2 * nseg]
    q = q_ref[...]
    m = jnp.full((q.shape[0], 1), NEG_BIG, F32)
    acc = jnp.zeros((q.shape[0], MXU_DIM), F32)
    for s, lk in enumerate(seg_lens):
        k_ref, v_ref = refs[2 * s], refs[2 * s + 1]
        step = min(MLA_KEY_CHUNK, lk)
        for c0 in range(0, lk, step):
            sc = _dot_nt(q, k_ref[c0:c0 + step, :])
            m_new = jnp.maximum(m, sc.max(axis=-1, keepdims=True))
            p = jnp.exp2(sc - m_new).astype(BF16)
            acc = acc * jnp.exp2(m - m_new) + _dot(p, v_ref[c0:c0 + step, :])
            m = m_new
    o_ref[...] = (acc[:, :MLA_V] / acc[:, MLA_V:MLA_V + 1]).astype(o_ref.dtype)


def _mla_attention(q, segs, lq, tq=512):
    nq = lq // tq
    in_specs = [pl.BlockSpec((None, tq, MXU_DIM), lambda b, h, i: (h, b * nq + i, 0))]
    args = [q]
    for k, v, lk in segs:
        in_specs.append(pl.BlockSpec((None, lk, MXU_DIM), lambda b, h, i: (h, b, 0)))
        in_specs.append(pl.BlockSpec((None, lk, MXU_DIM), lambda b, h, i: (h, b, 0)))
        args += [k, v]
    return pl.pallas_call(
        functools.partial(_mla_kernel, seg_lens=tuple(lk for _, _, lk in segs)),
        out_shape=jax.ShapeDtypeStruct((BATCH * lq, MLA_HEADS * MLA_V), BF16),
        grid=(BATCH, MLA_HEADS, nq),
        in_specs=in_specs,
        out_specs=pl.BlockSpec((tq, MLA_V), lambda b, h, i: (b * nq + i, h)),
        compiler_params=_cparams("parallel", "parallel", "arbitrary"),
        name="mla_attention",
    )(*args)


def _dft_tables(n_len):
    n2 = 2 * n_len
    idx = np.arange(n_len, dtype=np.int64)
    ang = (2.0 * np.pi / n2) * ((idx[:, None] * idx[None, :]) % n2).astype(np.float64)
    cm = np.cos(ang)
    sf = np.sin(ang)
    sf[0, :] = 1.0 - 2.0 * (idx % 2)
    return (jnp.asarray(cm, dtype=BF16), jnp.asarray(sf, dtype=BF16),
            jnp.asarray(sf.T.copy(), dtype=BF16))


def _filter_tables(n_len):
    pos = np.arange(n_len, dtype=np.float64)
    t = pos / max(n_len - 1, 1)
    bands = np.linspace(1e-4, HY_BANDS - 1, HY_BANDS)
    ang = (2.0 * math.pi / n_len) * pos[:, None] * bands[None]
    z = np.zeros((n_len, LANES), np.float64)
    z[:, 0] = t
    z[:, 1:1 + HY_BANDS] = np.cos(ang)
    z[:, 1 + HY_BANDS:HY_EMB] = -np.sin(ang)
    deltas = np.linspace(math.log(HY_DECAY_TARGET) / HY_SLOW_DECAY_PCT,
                         math.log(HY_DECAY_TARGET) / HY_QUICK_DECAY_PCT, HY_WIDTH)
    decay = np.exp(-t[:, None] * np.abs(deltas)[None])
    return jnp.asarray(z, dtype=F32), jnp.asarray(decay, dtype=F32)


def _hy_filter_kernel(z_ref, w1_ref, b1_ref, w2_ref, b2_ref, w3_ref, b3_ref, wf_ref, wb_ref,
                      fr_ref, dec_ref, cm_ref, sf_ref, a_ref, b_ref, d_ref, hid_ref, *, n_len):
    @pl.when(pl.program_id(0) == 0)
    def _():
        fr = fr_ref[...]
        h = jnp.sin(fr * (_dot3(z_ref[...], w1_ref[...]) + b1_ref[...]))
        h = jnp.sin(fr * (_dot3(h, w2_ref[...]) + b2_ref[...]))
        hid_ref[...] = jnp.sin(fr * (_dot3(h, w3_ref[...]) + b3_ref[...]))

    h = hid_ref[...]
    dec = dec_ref[...]
    h_f = _dot3(h, wf_ref[...]) * dec
    h_b = _dot3(h, wb_ref[...]) * dec
    row = lax.broadcasted_iota(jnp.int32, h_f.shape, 0)
    h_b = jnp.where(row == 0, 0.0, h_b)
    hs = h_f + h_b
    hd = h_f - h_b
    hs_h, hs_l = _split_bf16(hs)
    hd_h, hd_l = _split_bf16(hd)
    cm = cm_ref[...]
    sf = sf_ref[...]
    t_re = _dot(cm, hs_h) + _dot(cm, hs_l)
    g_t = _dot(sf, hd_h) + _dot(sf, hd_l)
    t_ny = (_dot(sf[0:16, :], hs_h) + _dot(sf[0:16, :], hs_l))[0:1, :]
    inv_n = 1.0 / (2 * n_len)
    first = row == 0
    a_ref[...] = jnp.where(first, inv_n, 2.0 * inv_n) * t_re
    b_ref[...] = jnp.where(first, 0.0, -2.0 * inv_n * g_t)
    d_ref[...] = jnp.where(first, inv_n * t_ny, 2.0 * inv_n * t_re)


def _hy_filters(lp, n_len, tabs):
    z, decay = tabs["filt"]
    cm, sf, _ = tabs["dft"]
    nblk = HY_WIDTH // LANES
    w1 = jnp.zeros((LANES, HY_FILTER_HIDDEN), F32).at[:HY_EMB].set(lp["hy_w1"])
    row = lambda a: a.reshape(1, -1)
    small = lambda shape: pl.BlockSpec(shape, lambda j: (0,) * len(shape))
    fh = HY_FILTER_HIDDEN
    out = jax.ShapeDtypeStruct((n_len, HY_WIDTH), F32)
    return pl.pallas_call(
        functools.partial(_hy_filter_kernel, n_len=n_len),
        out_shape=(out, out, out),
        grid=(nblk,),
        in_specs=[
            small((n_len, LANES)), small((LANES, fh)), small((1, fh)), small((fh, fh)), small((1, fh)),
            small((fh, fh)), small((1, fh)),
            pl.BlockSpec((fh, LANES), lambda j: (0, j)),
            pl.BlockSpec((fh, LANES), lambda j: (0, nblk + j)),
            small((1, fh)),
            pl.BlockSpec((n_len, LANES), lambda j: (0, j)),
            _const_spec((n_len, n_len)), _const_spec((n_len, n_len)),
        ],
        out_specs=tuple(pl.BlockSpec((n_len, LANES), lambda j: (0, j)) for _ in range(3)),
        scratch_shapes=[pltpu.VMEM((n_len, fh), F32)],
        compiler_params=_cparams("arbitrary"),
        name="hyena_filters",
    )(z, w1, row(lp["hy_b1"]), lp["hy_w2"], row(lp["hy_b2"]), lp["hy_w3"], row(lp["hy_b3"]),
      lp["hy_w_filt"], lp["hy_w_filt"], row(lp["hy_freq"]), decay, cm, sf)


def _hy_pre_kernel(u0_ref, u1_ref, u2_ref, w0_ref, w1_ref, w2_ref, b0_ref, b1_ref, b2_ref,
                   x0_ref, vx_ref, vxb_ref, *, n_len):
    def sconv(u_ref, w_ref, b_ref):
        u = u_ref[...]
        w = w_ref[...]
        row = lax.broadcasted_iota(jnp.int32, u.shape, 0)
        prev = jnp.where(row == 0, 0.0, pltpu.roll(u, 1, axis=0))
        nxt = jnp.where(row == n_len - 1, 0.0, pltpu.roll(u, n_len - 1, axis=0))
        return w[0:1] * prev + w[1:2] * u + w[2:3] * nxt + b_ref[...]

    x0_ref[...] = sconv(u0_ref, w0_ref, b0_ref)
    vx = sconv(u2_ref, w2_ref, b2_ref) * sconv(u1_ref, w1_ref, b1_ref)
    vx_ref[...] = vx
    vxb_ref[...] = vx.astype(BF16)


def _hy_pre(hy, conv_w, conv_b, n_len):
    nblk = HY_WIDTH // LANES
    uspec = lambda part: pl.BlockSpec((None, n_len, LANES), lambda b, j: (b, 0, part * nblk + j))
    wspec = lambda part: pl.BlockSpec((3, LANES), lambda b, j: (0, part * nblk + j))
    bspec = lambda part: pl.BlockSpec((1, LANES), lambda b, j: (0, part * nblk + j))
    ospec = pl.BlockSpec((None, n_len, LANES), lambda b, j: (b, 0, j))
    hy3 = hy.reshape(BATCH, n_len, 3 * HY_WIDTH)
    cb = conv_b.reshape(1, -1)
    return pl.pallas_call(
        functools.partial(_hy_pre_kernel, n_len=n_len),
        out_shape=(jax.ShapeDtypeStruct((BATCH, n_len, HY_WIDTH), F32),
                   jax.ShapeDtypeStruct((BATCH, n_len, HY_WIDTH), F32),
                   jax.ShapeDtypeStruct((BATCH, n_len, HY_WIDTH), BF16)),
        grid=(BATCH, nblk),
        in_specs=[uspec(0), uspec(1), uspec(2), wspec(0), wspec(1), wspec(2), bspec(0), bspec(1), bspec(2)],
        out_specs=(ospec, ospec, ospec),
        compiler_params=_cparams("parallel", "parallel"),
        name="hyena_short_conv",
    )(hy3, hy3, hy3, conv_w, conv_w, conv_w, cb, cb, cb)


def _hy_fwd_kernel(cm_ref, sf_ref, u_ref, a_ref, b_ref, d_ref, q_ref):
    u = u_ref[...]
    ur = _dot(cm_ref[...], u)
    g = _dot(sf_ref[...], u)
    b = b_ref[...]
    q_ref[0] = (a_ref[...] * ur + b * g).astype(BF16)
    q_ref[1] = (d_ref[...] * g - b * ur).astype(BF16)


def _hy_inv_kernel(cm_ref, si_ref, q_ref, x0_ref, vx_ref, skip_ref, o_ref):
    y = _dot(cm_ref[...], q_ref[0]) + _dot(si_ref[...], q_ref[1])
    o_ref[...] = (x0_ref[...] * (y + vx_ref[...] * skip_ref[...])).astype(o_ref.dtype)


def _hy_long_conv(x0, vx, vxb, spectra, skip, n_len, tabs):
    cm, sf, si = tabs["dft"]
    a, b, d = spectra
    tf = min(512, n_len)
    nf = n_len // tf
    mat = pl.BlockSpec((tf, n_len), lambda bb, f: (f, 0))
    spec = pl.BlockSpec((tf, HY_WIDTH), lambda bb, f: (f, 0))
    q = pl.pallas_call(
        _hy_fwd_kernel,
        out_shape=jax.ShapeDtypeStruct((BATCH, 2, n_len, HY_WIDTH), BF16),
        grid=(BATCH, nf),
        in_specs=[mat, mat, pl.BlockSpec((None, n_len, HY_WIDTH), lambda bb, f: (bb, 0, 0)), spec, spec, spec],
        out_specs=pl.BlockSpec((None, 2, tf, HY_WIDTH), lambda bb, f: (bb, 0, f, 0)),
        compiler_params=_cparams("parallel", "arbitrary"),
        name="hyena_dft_fwd",
    )(cm, sf, vxb, a, b, d)
    tile = pl.BlockSpec((None, tf, HY_WIDTH), lambda bb, f: (bb, f, 0))
    return pl.pallas_call(
        _hy_inv_kernel,
        out_shape=jax.ShapeDtypeStruct((BATCH, n_len, HY_WIDTH), BF16),
        grid=(BATCH, nf),
        in_specs=[mat, mat, pl.BlockSpec((None, 2, n_len, HY_WIDTH), lambda bb, f: (bb, 0, 0, 0)),
                  tile, tile, pl.BlockSpec((1, HY_WIDTH), lambda bb, f: (0, 0))],
        out_specs=tile,
        compiler_params=_cparams("parallel", "arbitrary"),
        name="hyena_dft_inv",
    )(cm, si, q, x0, vx, skip.reshape(1, -1))


_SWA_Q = SWA_HEADS * SWA_HEAD_DIM
_SWA_KV = SWA_KV_HEADS * SWA_HEAD_DIM


def _swaproj_kernel(h_ref, w_ref, cos_ref, sin_ref, q_ref, k_ref, v_ref, *, rope):
    t = _dot(h_ref[...], w_ref[...])
    q = t[:, :_SWA_Q]
    k = t[:, 2 * _SWA_Q:2 * _SWA_Q + _SWA_KV]
    v = t[:, 2 * _SWA_Q + 2 * _SWA_KV:]
    if rope:
        cos = cos_ref[...]
        sin = sin_ref[...]
        q = q * cos + t[:, _SWA_Q:2 * _SWA_Q] * sin
        k = k * cos[:, :_SWA_KV] + t[:, 2 * _SWA_Q + _SWA_KV:2 * _SWA_Q + 2 * _SWA_KV] * sin[:, :_SWA_KV]
    q_ref[...] = (q * (SWA_HEAD_DIM ** -0.5 * math.log2(math.e))).astype(BF16)
    lo = lax.broadcasted_iota(jnp.int32, k.shape, 1) < SWA_HEAD_DIM
    for src, dst in ((k, k_ref), (v, v_ref)):
        sw = pltpu.roll(src, SWA_HEAD_DIM, axis=1)
        dst[:, 0 * LANES:1 * LANES] = jnp.where(lo, src, 0.0).astype(BF16)
        dst[:, 1 * LANES:2 * LANES] = jnp.where(lo, 0.0, sw).astype(BF16)
        dst[:, 2 * LANES:3 * LANES] = jnp.where(lo, sw, 0.0).astype(BF16)
        dst[:, 3 * LANES:4 * LANES] = jnp.where(lo, 0.0, src).astype(BF16)


def _hy_swa_kernel(h_ref, why_ref, w_ref, cos_ref, sin_ref, hy_ref, q_ref, k_ref, v_ref, *, rope):
    hy_ref[...] = _dot(h_ref[...], why_ref[...])
    _swaproj_kernel(h_ref, w_ref, cos_ref, sin_ref, q_ref, k_ref, v_ref, rope=rope)


def _swa_proj(h, w, cos_t, sin_t, rope, w_hy=None, tm=512):
    m = h.shape[0]
    nt = cos_t.shape[0] // tm
    o = jax.ShapeDtypeStruct((m, 4 * LANES), BF16)
    ospec = pl.BlockSpec((tm, 4 * LANES), lambda i: (i, 0))
    tspec = pl.BlockSpec((tm, _SWA_Q), lambda i: (i % nt, 0))
    hspec = pl.BlockSpec((tm, D_MODEL), lambda i: (i, 0))
    if w_hy is None:
        return pl.pallas_call(
            functools.partial(_swaproj_kernel, rope=rope),
            out_shape=(o, o, o),
            grid=(m // tm,),
            in_specs=[hspec, _const_spec(w.shape), tspec, tspec],
            out_specs=(ospec, ospec, ospec),
            compiler_params=_cparams("parallel"),
            name="swa_proj",
        )(h, w, cos_t, sin_t)
    n_hy = w_hy.shape[1]
    return pl.pallas_call(
        functools.partial(_hy_swa_kernel, rope=rope),
        out_shape=(jax.ShapeDtypeStruct((m, n_hy), F32), o, o, o),
        grid=(m // tm,),
        in_specs=[hspec, _const_spec(w_hy.shape), _const_spec(w.shape), tspec, tspec],
        out_specs=(pl.BlockSpec((tm, n_hy), lambda i: (i, 0)), ospec, ospec, ospec),
        compiler_params=_cparams("parallel"),
        name="hyena_swa_proj",
    )(h, w_hy, w, cos_t, sin_t)


def _swa_kernel(sink_ref, q_ref, *refs, tq, win, lk, has_lat):
    if has_lat:
        kl_ref, vl_ref, kc_ref, vc_ref, o_ref = refs
        t0 = pl.program_id(1) * tq
        start = pl.multiple_of(jnp.clip(t0 - SWA_WINDOW, 0, lk - win), SWA_WINDOW)
        qpos = t0 + lax.broadcasted_iota(jnp.int32, (tq, win), 0)
        kpos = start + lax.broadcasted_iota(jnp.int32, (tq, win), 1)
        valid = jnp.abs(qpos - kpos) <= SWA_WINDOW
        valid = jnp.concatenate([valid, valid], axis=0)
    else:
        kc_ref, vc_ref, o_ref = refs
    upper = lax.broadcasted_iota(jnp.int32, (2 * tq, 1), 0) < tq
    for g in range(SWA_KV_HEADS):
        j0 = 2 * g
        qb = jnp.concatenate([q_ref[:, j0 * LANES:(j0 + 1) * LANES],
                              q_ref[:, (j0 + 1) * LANES:(j0 + 2) * LANES]], axis=0)
        acc = None
        for par in range(2):
            c0 = (2 * g + par) * LANES
            sk = jnp.where(upper, sink_ref[2 * j0 + par], sink_ref[2 * j0 + 2 + par]) * math.log2(math.e)
            s_c = _dot_nt(qb, kc_ref[:, c0:c0 + LANES])
            m = jnp.maximum(s_c.max(axis=-1, keepdims=True), sk)
            if has_lat:
                s_l = _dot_nt(qb, kl_ref[pl.ds(start, win), c0:c0 + LANES])
                s_l = jnp.where(valid, s_l, NEG_BIG)
                m = jnp.maximum(m, s_l.max(axis=-1, keepdims=True))
            p_c = jnp.exp2(s_c - m)
            den = p_c.sum(axis=-1, keepdims=True) + jnp.exp2(sk - m)
            if has_lat:
                p_l = jnp.exp2(s_l - m)
                den = den + p_l.sum(axis=-1, keepdims=True)
            o = _dot(p_c.astype(BF16), vc_ref[:, c0:c0 + LANES])
            if has_lat:
                o = o + _dot(p_l.astype(BF16), vl_ref[pl.ds(start, win), c0:c0 + LANES])
            o = o * (1.0 / den)
            acc = o if acc is None else acc + o
        o_ref[:, j0 * LANES:(j0 + 1) * LANES] = acc[:tq].astype(o_ref.dtype)
        o_ref[:, (j0 + 1) * LANES:(j0 + 2) * LANES] = acc[tq:].astype(o_ref.dtype)


def _swa_attention(sink, q, lat, ctx, lq, tq=256):
    nq = lq // tq
    win = tq + 2 * SWA_WINDOW
    full = lambda n: pl.BlockSpec((n, 4 * LANES), lambda b, i: (b, 0))
    in_specs = [pl.BlockSpec(memory_space=pltpu.SMEM),
                pl.BlockSpec((tq, 4 * LANES), lambda b, i: (b * nq + i, 0))]
    args = [sink, q]
    if lat is not None:
        in_specs += [full(SEQ), full(SEQ)]
        args += list(lat)
    in_specs += [full(CTX_LEN), full(CTX_LEN)]
    args += list(ctx)
    return pl.pallas_call(
        functools.partial(_swa_kernel, tq=tq, win=win, lk=SEQ, has_lat=lat is not None),
        out_shape=jax.ShapeDtypeStruct((BATCH * lq, 4 * LANES), BF16),
        grid=(BATCH, nq),
        in_specs=in_specs,
        out_specs=pl.BlockSpec((tq, 4 * LANES), lambda b, i: (b * nq + i, 0)),
        compiler_params=_cparams("parallel", "arbitrary"),
        name="swa_attention",
    )(*args)


def _out_kernel(a_ref, y_ref, s_ref, w_ref, x_ref, m_ref, g_ref, xo_ref, ho_ref):
    na = MLA_HEADS * MLA_V
    acc = _dot(a_ref[...], w_ref[0:na, :])
    acc = acc + _dot(y_ref[...], w_ref[na:na + HY_WIDTH, :])
    acc = acc + _dot(s_ref[...], w_ref[na + HY_WIDTH:, :])
    x = x_ref[...] + m_ref[2:3, :] * acc
    xo_ref[...] = x
    ho_ref[...] = (_rms(x, g_ref[...]) * (1.0 + m_ref[4:5, :]) + m_ref[3:4, :]).astype(ho_ref.dtype)


def _out_proj(a, y, s, w_out, x, mods, g_ffn, groups, row0, h_dtype, tm=512):
    m = x.shape[0]
    nr = m // groups // tm
    tile = lambda n: pl.BlockSpec((tm, n), lambda g_, i: (g_ * nr + i, 0))
    return pl.pallas_call(
        _out_kernel,
        out_shape=(jax.ShapeDtypeStruct((m, D_MODEL), F32), jax.ShapeDtypeStruct((m, D_MODEL), h_dtype)),
        grid=(groups, nr),
        in_specs=[tile(a.shape[1]), tile(y.shape[1]), tile(s.shape[1]), _const_spec(w_out.shape),
                  tile(D_MODEL), pl.BlockSpec((None, 6, D_MODEL), lambda g_, i: (row0 + g_, 0, 0)),
                  pl.BlockSpec((1, D_MODEL), lambda g_, i: (0, 0))],
        out_specs=(tile(D_MODEL), tile(D_MODEL)),
        compiler_params=_cparams("parallel", "parallel"),
        name="out_proj",
    )(a, y, s, w_out, x, mods, g_ffn.reshape(1, -1))


_META_W1, _META_W2, _META_E1, _META_E2, _META_R1, _META_R2 = range(6)


def _router_kernel(h_ref, w_ref, meta_ref, cnt_ref, carry_ref):
    @pl.when(pl.program_id(0) == 0)
    def _():
        carry_ref[...] = jnp.zeros_like(carry_ref)

    logits = _dot3(h_ref[...], w_ref[...])
    tm = logits.shape[0]
    lane = lax.broadcasted_iota(jnp.int32, logits.shape, 1).astype(F32)
    logits = jnp.where(lane < N_EXPERTS, logits, NEG_BIG)
    m1 = logits.max(axis=-1, keepdims=True)
    i1 = jnp.where(logits == m1, lane, float(LANES)).min(axis=-1, keepdims=True)
    rest = jnp.where(lane == i1, NEG_BIG, logits)
    m2 = rest.max(axis=-1, keepdims=True)
    i2 = jnp.where(rest == m2, lane, float(LANES)).min(axis=-1, keepdims=True)
    e2 = jnp.exp(m2 - m1)
    w1 = 1.0 / (1.0 + e2)
    hot = jnp.where((lane == i1) | (lane == i2), 1.0, 0.0)
    before = (lax.broadcasted_iota(jnp.int32, (tm, tm), 0) > lax.broadcasted_iota(jnp.int32, (tm, tm), 1))
    seen = _dot(before.astype(F32).astype(BF16), hot.astype(BF16)) + carry_ref[0:1, :]
    r1 = jnp.where(lane == i1, seen, 0.0).sum(axis=-1, keepdims=True)
    r2 = jnp.where(lane == i2, seen, 0.0).sum(axis=-1, keepdims=True)
    carry_ref[...] = carry_ref[...] + hot.sum(axis=0, keepdims=True)
    cnt_ref[...] = carry_ref[...]
    rec = jnp.zeros_like(logits)
    for k, v in ((_META_W1, w1), (_META_W2, e2 * w1), (_META_E1, i1), (_META_E2, i2), (_META_R1, r1), (_META_R2, r2)):
        rec = jnp.where(lane == float(k), v, rec)
    meta_ref[...] = rec


def _router(h, w_router, tm=512):
    m = h.shape[0]
    wp = jnp.zeros((D_MODEL, LANES), F32).at[:, :N_EXPERTS].set(w_router)
    return pl.pallas_call(
        _router_kernel,
        out_shape=(jax.ShapeDtypeStruct((m, LANES), F32), jax.ShapeDtypeStruct((8, LANES), F32)),
        grid=(m // tm,),
        in_specs=[pl.BlockSpec((tm, D_MODEL), lambda i: (i, 0)),
                  pl.BlockSpec((D_MODEL, LANES), lambda i: (0, 0))],
        out_specs=(pl.BlockSpec((tm, LANES), lambda i: (i, 0)), pl.BlockSpec((8, LANES), lambda i: (0, 0))),
        scratch_shapes=[pltpu.VMEM((8, LANES), F32)],
        compiler_params=_cparams("arbitrary"),
        name="moe_router",
    )(h, wp)


MOE_TM = 1024
MOE_BLK = 128
MOE_ROWS = (MOE_TM, 512)


def _moe_plan(meta, counts, m):
    n_steps = -(-2 * m // MOE_TM) + N_EXPERTS
    cnt = counts[0, :N_EXPERTS].astype(jnp.int32)
    steps_e = (cnt + MOE_TM - 1) // MOE_TM
    ends = jnp.cumsum(steps_e)
    first = ends - steps_e
    total = ends[-1]
    e = meta[:, _META_E1:_META_E2 + 1].astype(jnp.int32)
    rank = meta[:, _META_R1:_META_R2 + 1].astype(jnp.int32)
    pos = (first * MOE_TM)[e] + rank
    s_idx = jnp.arange(n_steps, dtype=jnp.int32)
    step_e = jnp.minimum(jnp.searchsorted(ends, s_idx, side="right").astype(jnp.int32), N_EXPERTS - 1)
    valid = jnp.clip(cnt[step_e] - (s_idx - first[step_e]) * MOE_TM, 0, MOE_TM)
    valid = jnp.where(s_idx < total, valid, 0)
    return pos.reshape(-1), step_e, valid, n_steps


def _row_copy(src_ref, src_row, dst_ref, dst_row, sem):
    return pltpu.make_async_copy(src_ref.at[pl.ds(src_row, 1)], dst_ref.at[pl.ds(dst_row, 1)], sem)


def _invert_kernel(pos_ref, sv_ref, src_ref):
    def clear_step(s, c):
        def clear(i, c2):
            src_ref[s * MOE_TM + i] = 0
            return c2

        return lax.fori_loop(sv_ref[s], MOE_TM, clear, c)

    def place(t, c):
        p0 = pos_ref[2 * t]
        p1 = pos_ref[2 * t + 1]
        src_ref[p0] = t
        src_ref[p1] = t
        return c

    lax.fori_loop(0, sv_ref.shape[0], clear_step, 0)
    lax.fori_loop(0, pos_ref.shape[0] // 2, place, 0, unroll=8)


def _moe_invert(pos, valid, n_rows):
    smem = pl.BlockSpec(memory_space=pltpu.SMEM)
    return pl.pallas_call(
        _invert_kernel,
        out_shape=jax.ShapeDtypeStruct((n_rows,), jnp.int32),
        in_specs=[smem, smem], out_specs=smem,
        name="moe_invert",
    )(pos, valid)


def _block_wait(src_ref, dst_ref, rows, sem):
    pltpu.make_async_copy(src_ref.at[pl.ds(0, rows)], dst_ref.at[pl.ds(0, rows)], sem).wait()


def _moe_ffn_kernel(se_ref, sv_ref, src_ref, h_ref, wg_ref, wu_ref, wd_ref, o_ref, x32_ref, xb_ref, sem):
    del se_ref
    s = pl.program_id(0)
    f = pl.program_id(1)
    valid = sv_ref[s]

    def rows_used(v):
        r = jnp.where(v > 0, MOE_ROWS[-1], 0)
        for small, big in zip(MOE_ROWS[:0:-1], MOE_ROWS[-2::-1]):
            r = jnp.where(v > small, big, r)
        return r

    def blocks(v):
        return rows_used(v) // MOE_BLK

    def gather(step):
        base = step * MOE_TM

        def issue(i, c):
            _row_copy(h_ref, src_ref[base + i], x32_ref, i, sem).start()
            return c

        lax.fori_loop(0, rows_used(sv_ref[step]), issue, 0)

    @pl.when(f == 0)
    def _():
        @pl.when(s == 0)
        def _():
            gather(0)

        def land(i, c):
            _block_wait(h_ref, x32_ref, MOE_BLK, sem)
            return c

        def to_bf16(i, c):
            r0 = pl.multiple_of(i * MOE_BLK, MOE_BLK)
            xb_ref[pl.ds(r0, MOE_BLK), :] = x32_ref[pl.ds(r0, MOE_BLK), :].astype(BF16)
            return c

        lax.fori_loop(0, blocks(valid), land, 0)
        lax.fori_loop(0, blocks(valid), to_bf16, 0)

        @pl.when(s + 1 < pl.num_programs(0))
        def _():
            gather(s + 1)

        o_ref[...] = jnp.zeros_like(o_ref)

    def compute(rows):
        xs = xb_ref[0:rows, :]
        mid = _silu(_dot(xs, wg_ref[...].astype(BF16))) * _dot(xs, wu_ref[...].astype(BF16))
        o_ref[0:rows, :] += _dot(mid.astype(BF16), wd_ref[...].astype(BF16))

    for k, rows in enumerate(MOE_ROWS):
        lower = MOE_ROWS[k + 1] if k + 1 < len(MOE_ROWS) else 0

        @pl.when((valid > lower) & (valid <= rows))
        def _():
            compute(rows)


def _moe_ffn(h, src, w_gate, w_up, w_down, step_e, valid, n_steps, tf=512):
    assert all(r % MOE_BLK == 0 for r in MOE_ROWS) and MOE_ROWS[0] == MOE_TM
    n_ff = w_gate.shape[-1]
    nf = n_ff // tf
    fidx = lambda s, f, sv: jnp.where(sv[s] > 0, f, nf - 1)
    return pl.pallas_call(
        _moe_ffn_kernel,
        out_shape=jax.ShapeDtypeStruct((n_steps * MOE_TM, D_MODEL), F32),
        grid_spec=pltpu.PrefetchScalarGridSpec(
            num_scalar_prefetch=3, grid=(n_steps, nf),
            in_specs=[
                pl.BlockSpec(memory_space=pl.ANY),
                pl.BlockSpec((None, D_MODEL, tf), lambda s, f, se, sv, sr: (se[s], 0, fidx(s, f, sv))),
                pl.BlockSpec((None, D_MODEL, tf), lambda s, f, se, sv, sr: (se[s], 0, fidx(s, f, sv))),
                pl.BlockSpec((None, tf, D_MODEL), lambda s, f, se, sv, sr: (se[s], fidx(s, f, sv), 0)),
            ],
            out_specs=pl.BlockSpec((MOE_TM, D_MODEL), lambda s, f, se, sv, sr: (s, 0), pipeline_mode=pl.Buffered(1)),
            scratch_shapes=[pltpu.VMEM((MOE_TM, D_MODEL), F32), pltpu.VMEM((MOE_TM, D_MODEL), BF16),
                            pltpu.SemaphoreType.DMA(())]),
        compiler_params=_cparams("arbitrary", "arbitrary"),
        name="moe_experts",
    )(step_e, valid, src, h, w_gate, w_up, w_down)


def _combine_kernel(pos_ref, meta_ref, x_ref, m_ref, gf_ref, ys_ref, o_ref, buf_ref, sem, *, tm, final):
    base = pl.program_id(0) * tm

    def issue(i, c):
        for slot in range(2):
            _row_copy(ys_ref, pos_ref[2 * (base + i) + slot], buf_ref.at[slot], i, sem).start()
        return c

    lax.fori_loop(0, tm, issue, 0, unroll=4)
    for slot in range(2):
        _block_wait(ys_ref, buf_ref.at[slot], tm, sem)
    meta = meta_ref[...]
    y = meta[:, _META_W1:_META_W1 + 1] * buf_ref[0] + meta[:, _META_W2:_META_W2 + 1] * buf_ref[1]
    x = x_ref[...] + m_ref[5:6, :] * y
    if final:
        x = _rms(x, gf_ref[...])
    o_ref[...] = x


def _moe_combine(pos, meta, x, mods, ys, groups, row0, g_final, tm=512):
    m = x.shape[0]
    nr = m // groups // tm
    final = g_final is not None
    gf = (g_final if final else jnp.ones((D_MODEL,), F32)).reshape(1, D_MODEL)
    return pl.pallas_call(
        functools.partial(_combine_kernel, tm=tm, final=final),
        out_shape=jax.ShapeDtypeStruct((m, D_MODEL), F32),
        grid_spec=pltpu.PrefetchScalarGridSpec(
            num_scalar_prefetch=1, grid=(m // tm,),
            in_specs=[pl.BlockSpec((tm, LANES), lambda i, p: (i, 0)),
                      pl.BlockSpec((tm, D_MODEL), lambda i, p: (i, 0)),
                      pl.BlockSpec((None, 6, D_MODEL), lambda i, p: (row0 + i // nr, 0, 0)),
                      pl.BlockSpec((1, D_MODEL), lambda i, p: (0, 0)),
                      pl.BlockSpec(memory_space=pl.ANY)],
            out_specs=pl.BlockSpec((tm, D_MODEL), lambda i, p: (i, 0)),
            scratch_shapes=[pltpu.VMEM((2, tm, D_MODEL), F32), pltpu.SemaphoreType.DMA(())]),
        compiler_params=_cparams("arbitrary"),
        name="moe_combine",
    )(pos, meta, x, mods, gf, ys)


def _moe(h, x, mods, groups, row0, w_router, w_gate, w_up, w_down, g_final):
    m = h.shape[0]
    meta, counts = _router(h, w_router)
    pos, step_e, valid, n_steps = _moe_plan(meta, counts, m)
    src = _moe_invert(pos, valid, n_steps * MOE_TM)
    ys = _moe_ffn(h, src, w_gate, w_up, w_down, step_e, valid, n_steps)
    return _moe_combine(pos, meta, x, mods, ys, groups, row0, g_final)


def _ffn_kernel(h_ref, wg_ref, wu_ref, wd_ref, x_ref, m_ref, gn_ref, mn_ref, o_ref, *maybe_hn_ref):
    f = pl.program_id(2)

    @pl.when(f == 0)
    def _():
        o_ref[...] = jnp.zeros_like(o_ref)

    h = h_ref[...]
    mid = _silu(_dot(h, wg_ref[...])) * _dot(h, wu_ref[...])
    o_ref[...] += _dot(mid.astype(BF16), wd_ref[...])

    @pl.when(f == pl.num_programs(2) - 1)
    def _():
        x = x_ref[...] + m_ref[5:6, :] * o_ref[...]
        o_ref[...] = x
        for hn_ref in maybe_hn_ref:
            hn_ref[...] = (_rms(x, gn_ref[...]) * (1.0 + mn_ref[1:2, :]) + mn_ref[0:1, :]).astype(hn_ref.dtype)


def _ffn(h, w_gate, w_up, w_down, x, mods, groups, row0, nxt, tm, tf=512):
    m = x.shape[0]
    n_ff = w_gate.shape[1]
    nr = m // groups // tm
    g_next, mods_next = nxt if nxt is not None else (jnp.ones((D_MODEL,), F32), mods)
    once = pl.Buffered(1)
    tile = lambda n, mode=None: pl.BlockSpec((tm, n), lambda g_, i, f: (g_ * nr + i, 0), pipeline_mode=mode)
    mod_spec = pl.BlockSpec((None, 6, D_MODEL), lambda g_, i, f: (row0 + g_, 0, 0))
    out_shape = [jax.ShapeDtypeStruct((m, D_MODEL), F32)]
    out_specs = [tile(D_MODEL, once)]
    if nxt is not None:
        out_shape.append(jax.ShapeDtypeStruct((m, D_MODEL), BF16))
        out_specs.append(tile(D_MODEL, once))
    outs = pl.pallas_call(
        _ffn_kernel,
        out_shape=tuple(out_shape),
        grid=(groups, nr, n_ff // tf),
        in_specs=[tile(D_MODEL),
                  pl.BlockSpec((D_MODEL, tf), lambda g_, i, f: (0, f)),
                  pl.BlockSpec((D_MODEL, tf), lambda g_, i, f: (0, f)),
                  pl.BlockSpec((tf, D_MODEL), lambda g_, i, f: (f, 0)),
                  tile(D_MODEL, once), mod_spec,
                  pl.BlockSpec((1, D_MODEL), lambda g_, i, f: (0, 0)), mod_spec],
        out_specs=tuple(out_specs),
        compiler_params=_cparams("parallel", "parallel", "arbitrary"),
        name="swiglu_ffn",
    )(h, w_gate, w_up, w_down, x, mods, g_next.reshape(1, D_MODEL), mods_next)
    return outs if nxt is not None else (outs[0], None)


def _swap_halves(w):
    half = w.shape[-1] // 2
    return jnp.concatenate([w[..., half:], w[..., :half]], axis=-1)


def _swap_heads(w, heads, dim):
    k = w.shape[0]
    return _swap_halves(w.reshape(k, heads, dim)).reshape(k, heads * dim)


def _rope_tables():
    rows = SEQ // GRID_W
    row = np.repeat(np.arange(rows), GRID_W).astype(np.float32)
    col = np.tile(np.arange(GRID_W), rows).astype(np.float32)
    quarter = MLA_ROPE // 4
    freqs = (np.float32(ROPE_THETA) ** (-np.arange(quarter, dtype=np.float32) / quarter)).astype(np.float32)
    ang = np.concatenate([row[:, None] * freqs[None], col[:, None] * freqs[None]], axis=-1)
    cos = np.cos(ang.astype(np.float64))
    sin = np.sin(ang.astype(np.float64))
    cos2 = np.concatenate([cos, cos], axis=-1)
    sin2 = np.concatenate([-sin, sin], axis=-1)
    tab_lat = np.concatenate([cos2, sin2], axis=-1)
    tab_ctx = np.concatenate([np.ones((CTX_LEN, MLA_ROPE)), np.zeros((CTX_LEN, MLA_ROPE))], axis=-1)
    return dict(
        tab_lat=jnp.asarray(tab_lat, F32), tab_ctx=jnp.asarray(tab_ctx, F32),
        cos8=jnp.asarray(np.tile(cos2, (1, SWA_HEADS)), F32),
        sin8=jnp.asarray(np.tile(sin2, (1, SWA_HEADS)), F32),
        ones8=jnp.ones((CTX_LEN, _SWA_Q), F32), zeros8=jnp.zeros((CTX_LEN, _SWA_Q), F32),
    )


def _layer_weights(p, l):
    w_in = p["w_in"][l]
    cq = w_in[:, _O_CQ:_O_CKV]
    ckv = w_in[:, _O_CKV:_O_KPE]
    kpe = w_in[:, _O_KPE:_O_HY]
    hy = w_in[:, _O_HY:_O_SQ]
    sq = w_in[:, _O_SQ:_O_SK]
    sk = w_in[:, _O_SK:_O_SV]
    sv = w_in[:, _O_SV:]
    dq = MLA_NOPE + MLA_ROPE
    wq = p["w_q_up"][l].reshape(MLA_Q_LORA, MLA_HEADS, dq)
    wq = jnp.concatenate([wq, _swap_halves(wq[..., MLA_NOPE:])], axis=-1)
    wkv = p["w_kv_up"][l].reshape(MLA_KV_LORA, MLA_HEADS, MLA_NOPE + MLA_V)
    return dict(
        w_cq=cq.astype(BF16),
        w_ckv=jnp.concatenate([ckv, kpe, _swap_halves(kpe)], axis=-1).astype(BF16),
        w_hy=hy.astype(BF16),
        w_swa=jnp.concatenate([sq, _swap_heads(sq, SWA_HEADS, SWA_HEAD_DIM), sk,
                               _swap_heads(sk, SWA_KV_HEADS, SWA_HEAD_DIM), sv], axis=-1).astype(BF16),
        w_q=jnp.transpose(wq, (1, 0, 2)).astype(BF16),
        w_kv=jnp.transpose(wkv, (1, 0, 2)).astype(BF16),
        w_out=p["w_out"][l].astype(BF16),
        g_q=p["g_q"][l], g_kv=p["g_kv"][l],
        hy_conv_w=p["hy_conv_w"][l], hy_conv_b=p["hy_conv_b"][l],
        hy_w1=p["hy_w1"][l], hy_b1=p["hy_b1"][l], hy_w2=p["hy_w2"][l], hy_b2=p["hy_b2"][l],
        hy_w3=p["hy_w3"][l], hy_b3=p["hy_b3"][l], hy_w_filt=p["hy_w_filt"][l],
        hy_freq=p["hy_freq"][l], hy_skip=p["hy_skip"][l], swa_sink=p["swa_sink"][l],
    )


def _hyena(hy, lw, n_len, tabs):
    x0, vx, vxb = _hy_pre(hy, lw["hy_conv_w"], lw["hy_conv_b"], n_len)
    spectra = _hy_filters(lw, n_len, tabs)
    y = _hy_long_conv(x0, vx, vxb, spectra, lw["hy_skip"], n_len, tabs)
    return y.reshape(BATCH * n_len, HY_WIDTH)


def _mixer(h_lat, h_ctx, lw, rope, hy_tabs, need_ctx):
    q_l, k_l, v_l = _mla_proj(h_lat, lw, rope["tab_lat"])
    if need_ctx:
        q_c, k_c, v_c = _mla_proj(h_ctx, lw, rope["tab_ctx"], tm=CTX_LEN)
    else:
        k_c, v_c = _kv_proj(h_ctx, lw["w_ckv"], lw["g_kv"], lw["w_kv"], rope["tab_ctx"], tm=CTX_LEN)
    a_l = _mla_attention(q_l, [(k_l, v_l, SEQ), (k_c, v_c, CTX_LEN)], SEQ)
    hy_l, sq_l, kk_l, vv_l = _swa_proj(h_lat, lw["w_swa"], rope["cos8"], rope["sin8"], True, w_hy=lw["w_hy"])
    y_l = _hyena(hy_l, lw, SEQ, hy_tabs[SEQ])
    if need_ctx:
        hy_c, sq_c, kk_c, vv_c = _swa_proj(h_ctx, lw["w_swa"], rope["ones8"], rope["zeros8"], False,
                                           w_hy=lw["w_hy"], tm=CTX_LEN)
    else:
        sq_c, kk_c, vv_c = _swa_proj(h_ctx, lw["w_swa"], rope["ones8"], rope["zeros8"], False, tm=CTX_LEN)
    s_l = _swa_attention(lw["swa_sink"], sq_l, (kk_l, vv_l), (kk_c, vv_c), SEQ)
    if not need_ctx:
        return (a_l, y_l, s_l), None
    a_c = _mla_attention(q_c, [(k_c, v_c, CTX_LEN)], CTX_LEN, tq=CTX_LEN)
    y_c = _hyena(hy_c, lw, CTX_LEN, hy_tabs[CTX_LEN])
    s_c = _swa_attention(lw["swa_sink"], sq_c, None, (kk_c, vv_c), CTX_LEN, tq=CTX_LEN)
    return (a_l, y_l, s_l), (a_c, y_c, s_c)


def _forward(p):
    rope = _rope_tables()
    hy_tabs = {n: dict(dft=_dft_tables(n), filt=_filter_tables(n)) for n in (SEQ, CTX_LEN)}
    cvec = jnp.zeros((MOD_ROWS, D_MODEL), F32).at[:BATCH].set(p["c"]).at[BATCH].set(p["c_ctx"])
    mods_all = _modulation(cvec, p["w_mod"], p["b_mod"]).reshape(DEPTH, MOD_ROWS, 6, D_MODEL)

    x_lat = p["x"].reshape(BATCH * SEQ, D_MODEL)
    x_ctx = p["ctx"].reshape(BATCH * CTX_LEN, D_MODEL)
    h_in = [None, None]
    for l in range(DEPTH):
        last = l == DEPTH - 1
        mods = mods_all[l]
        lw = _layer_weights(p, l)
        h_lat = h_in[0] if h_in[0] is not None else _norm_mod(x_lat, p["g_mix"][l], mods, BATCH, 0, 0, BF16)
        h_ctx = h_in[1] if h_in[1] is not None else _norm_mod(x_ctx, p["g_mix"][l], mods, 1, BATCH, 0, BF16)
        mix_l, mix_c = _mixer(h_lat, h_ctx, lw, rope, hy_tabs, not last)
        streams = [(x_lat, mix_l, BATCH, 0, 1024)]
        if not last:
            streams.append((x_ctx, mix_c, 1, BATCH, 512))
        outs = []
        h_in = [None, None]
        i = l // 2
        dense = l % 2 == 0
        for n, (x, mix, groups, row0, ffn_tm) in enumerate(streams):
            x, h2 = _out_proj(*mix, lw["w_out"], x, mods, p["g_ffn"][l], groups, row0, BF16 if dense else F32)
            if dense:
                nxt = None if last else (p["g_mix"][l + 1], mods_all[l + 1])
                x, h_in[n] = _ffn(h2, p["ffn_w_gate"][i].astype(BF16), p["ffn_w_up"][i].astype(BF16),
                                  p["ffn_w_down"][i].astype(BF16), x, mods, groups, row0, nxt, ffn_tm)
            else:
                g_final = p["g_final"] if (last and n == 0) else None
                x = _moe(h2, x, mods, groups, row0, p["moe_router"][i], p["moe_w_gate"][i], p["moe_w_up"][i],
                         p["moe_w_down"][i], g_final)
            outs.append(x)
        x_lat = outs[0]
        if not last:
            x_ctx = outs[1]
    if DEPTH % 2 == 1:
        x_lat = _norm_mod(x_lat, p["g_final"], mods_all[0], BATCH, 0, None, F32)
    return x_lat.reshape(BATCH, SEQ, D_MODEL)


def kernel(x, c, ctx, c_ctx, w_mod, b_mod, g_mix, g_ffn, w_in, g_q, w_q_up, g_kv, w_kv_up, hy_conv_w, hy_conv_b, hy_w1, hy_b1, hy_w2, hy_b2, hy_w3, hy_b3, hy_w_filt, hy_freq, hy_skip, swa_sink, w_out, ffn_w_gate, ffn_w_up, ffn_w_down, moe_router, moe_w_gate, moe_w_up, moe_w_down, g_final):
    return _forward(dict(
        x=x, c=c, ctx=ctx, c_ctx=c_ctx, w_mod=w_mod, b_mod=b_mod, g_mix=g_mix, g_ffn=g_ffn, w_in=w_in,
        g_q=g_q, w_q_up=w_q_up, g_kv=g_kv, w_kv_up=w_kv_up, hy_conv_w=hy_conv_w, hy_conv_b=hy_conv_b,
        hy_w1=hy_w1, hy_b1=hy_b1, hy_w2=hy_w2, hy_b2=hy_b2, hy_w3=hy_w3, hy_b3=hy_b3,
        hy_w_filt=hy_w_filt, hy_freq=hy_freq, hy_skip=hy_skip, swa_sink=swa_sink, w_out=w_out,
        ffn_w_gate=ffn_w_gate, ffn_w_up=ffn_w_up, ffn_w_down=ffn_w_down, moe_router=moe_router,
        moe_w_gate=moe_w_gate, moe_w_up=moe_w_up, moe_w_down=moe_w_down, g_final=g_final))
```

```python
import functools
import math

import numpy as np
import jax
import jax.numpy as jnp
from jax import lax
from jax.experimental import pallas as pl
from jax.experimental.pallas import tpu as pltpu

F32 = jnp.float32
BF16 = jnp.bfloat16

D_MODEL = 2048
BATCH = 4
SEQ = 2048
DEPTH = 2
CTX_LEN = 256
GRID_W = 64
NORM_EPS = 1e-6
ROPE_THETA = 10000.0
MLA_HEADS = 8
MLA_NOPE = 128
MLA_ROPE = 64
MLA_V = 128
MLA_Q_LORA = 768
MLA_KV_LORA = 512
HY_WIDTH = 512
HY_BANDS = 16
HY_EMB = 1 + 2 * HY_BANDS
HY_FILTER_HIDDEN = 64
HY_DECAY_TARGET = 1e-2
HY_QUICK_DECAY_PCT = 0.3
HY_SLOW_DECAY_PCT = 1.5
SWA_HEADS = 8
SWA_KV_HEADS = 2
SWA_HEAD_DIM = 64
SWA_WINDOW = 128
N_EXPERTS = 8
D_FF = 5632
D_FF_EXPERT = 7168

LANES = 128
MXU_DIM = 256
VMEM_LIMIT_BYTES = 56 * 1024 * 1024
NEG_BIG = -1e30
MOD_ROWS = 8

_O_CQ = 0
_O_CKV = _O_CQ + MLA_Q_LORA
_O_KPE = _O_CKV + MLA_KV_LORA
_O_HY = _O_KPE + MLA_ROPE
_O_SQ = _O_HY + 3 * HY_WIDTH
_O_SK = _O_SQ + SWA_HEADS * SWA_HEAD_DIM
_O_SV = _O_SK + SWA_KV_HEADS * SWA_HEAD_DIM


def _cparams(*sem):
    return pltpu.CompilerParams(dimension_semantics=sem, vmem_limit_bytes=VMEM_LIMIT_BYTES)


def _dot(a, b):
    return jnp.dot(a, b, preferred_element_type=F32)


def _dot_nt(a, b):
    return lax.dot_general(a, b, (((1,), (1,)), ((), ())), preferred_element_type=F32)


def _split_bf16(a):
    hi = a.astype(BF16)
    lo = (a - hi.astype(F32)).astype(BF16)
    return hi, lo


def _dot3(a, b):
    ah, al = _split_bf16(a)
    bh, bl = _split_bf16(b)
    return _dot(ah, bh) + (_dot(al, bh) + _dot(ah, bl))


def _silu(x):
    return x / (1.0 + jnp.exp(-x))


def _rms(x, g):
    ms = jnp.mean(x * x, axis=-1, keepdims=True)
    return x * lax.rsqrt(ms + NORM_EPS) * g


def _const_spec(shape):
    nd = len(shape)
    return pl.BlockSpec(shape, lambda *_: (0,) * nd, pipeline_mode=pl.Buffered(1))


def _mod_kernel(c_ref, w_ref, b_ref, o_ref):
    ah, al = _split_bf16(_silu(c_ref[...]))
    wh, wl = _split_bf16(w_ref[0])
    r = _dot(jnp.concatenate([ah, al], axis=0), wh)
    o_ref[0] = r[:MOD_ROWS] + (r[MOD_ROWS:] + _dot(ah, wl)) + b_ref[0]


def _modulation(cvec, w_mod, b_mod):
    tn = 1024
    n = 6 * D_MODEL
    return pl.pallas_call(
        _mod_kernel,
        out_shape=jax.ShapeDtypeStruct((DEPTH, MOD_ROWS, n), F32),
        grid=(DEPTH, n // tn),
        in_specs=[
            pl.BlockSpec((MOD_ROWS, D_MODEL), lambda l, j: (0, 0)),
            pl.BlockSpec((1, D_MODEL, tn), lambda l, j: (l, 0, j)),
            pl.BlockSpec((1, 1, tn), lambda l, j: (l, 0, j)),
        ],
        out_specs=pl.BlockSpec((1, MOD_ROWS, tn), lambda l, j: (l, 0, j)),
        compiler_params=_cparams("arbitrary", "arbitrary"),
        name="adaln_mod",
    )(cvec, w_mod, b_mod.reshape(DEPTH, 1, n))


def _norm_kernel(x_ref, g_ref, m_ref, o_ref, *, si):
    y = _rms(x_ref[...], g_ref[...])
    if si is not None:
        y = y * (1.0 + m_ref[si + 1:si + 2, :]) + m_ref[si:si + 1, :]
    o_ref[...] = y.astype(o_ref.dtype)


def _norm_mod(x, g, mods, groups, row0, si, out_dtype, tm=512):
    m = x.shape[0]
    nr = m // groups // tm
    return pl.pallas_call(
        functools.partial(_norm_kernel, si=si),
        out_shape=jax.ShapeDtypeStruct((m, D_MODEL), out_dtype),
        grid=(groups, nr),
        in_specs=[
            pl.BlockSpec((tm, D_MODEL), lambda g_, i: (g_ * nr + i, 0)),
            pl.BlockSpec((1, D_MODEL), lambda g_, i: (0, 0)),
            pl.BlockSpec((None, 6, D_MODEL), lambda g_, i: (row0 + g_, 0, 0)),
        ],
        out_specs=pl.BlockSpec((tm, D_MODEL), lambda g_, i: (g_ * nr + i, 0)),
        compiler_params=_cparams("parallel", "parallel"),
        name="norm_mod",
    )(x, g.reshape(1, D_MODEL), mods)


def _rope_halves(pe_pair, tab):
    r = pe_pair * tab
    return r + pltpu.roll(r, MLA_ROPE, axis=1)


MLA_Q_SCALE = (MLA_NOPE + MLA_ROPE) ** -0.5 * math.log2(math.e)


def _qproj_body(h_ref, wc_ref, g_ref, wq_ref, tab_ref, o_ref):
    cq = _dot(h_ref[...], wc_ref[...])
    cqn = _rms(cq, g_ref[...]).astype(BF16)
    tab = tab_ref[...] * MLA_Q_SCALE
    for hh in range(MLA_HEADS):
        r = _dot(cqn, wq_ref[hh])
        o_ref[hh, :, 0:MLA_NOPE] = (r[:, :MLA_NOPE] * MLA_Q_SCALE).astype(BF16)
        o_ref[hh, :, MLA_NOPE:] = _rope_halves(r[:, MLA_NOPE:], tab).astype(BF16)


def _mla_proj_kernel(h_ref, wcq_ref, gq_ref, wq_ref, wckv_ref, gkv_ref, wkv_ref, tab_ref, q_ref, k_ref, v_ref):
    _qproj_body(h_ref, wcq_ref, gq_ref, wq_ref, tab_ref, q_ref)
    _kvproj_kernel(h_ref, wckv_ref, gkv_ref, wkv_ref, tab_ref, k_ref, v_ref)


def _mla_proj(h, lw, tab, tm=512):
    m = h.shape[0]
    nt = tab.shape[0] // tm
    head_out = jax.ShapeDtypeStruct((MLA_HEADS, m, MXU_DIM), BF16)
    head_spec = pl.BlockSpec((MLA_HEADS, tm, MXU_DIM), lambda i: (0, i, 0))
    return pl.pallas_call(
        _mla_proj_kernel,
        out_shape=(head_out, head_out, head_out),
        grid=(m // tm,),
        in_specs=[
            pl.BlockSpec((tm, D_MODEL), lambda i: (i, 0)),
            _const_spec(lw["w_cq"].shape), _const_spec((1, MLA_Q_LORA)), _const_spec(lw["w_q"].shape),
            _const_spec(lw["w_ckv"].shape), _const_spec((1, MLA_KV_LORA)), _const_spec(lw["w_kv"].shape),
            pl.BlockSpec((tm, LANES), lambda i: (i % nt, 0)),
        ],
        out_specs=(head_spec, head_spec, head_spec),
        compiler_params=_cparams("parallel"),
        name="mla_proj",
    )(h, lw["w_cq"], lw["g_q"].reshape(1, -1), lw["w_q"], lw["w_ckv"], lw["g_kv"].reshape(1, -1), lw["w_kv"], tab)


def _kvproj_kernel(h_ref, wc_ref, g_ref, wkv_ref, tab_ref, k_ref, v_ref):
    t = _dot(h_ref[...], wc_ref[...])
    ckvn = _rms(t[:, :MLA_KV_LORA], g_ref[...]).astype(BF16)
    rot = _rope_halves(t[:, MLA_KV_LORA:], tab_ref[...])
    lane = lax.broadcasted_iota(jnp.int32, rot.shape, 1)
    kpe = jnp.where(lane < MLA_ROPE, rot, 0.0).astype(BF16)
    for hh in range(MLA_HEADS):
        r = _dot(ckvn, wkv_ref[hh])
        k_ref[hh, :, 0:MLA_NOPE] = r[:, :MLA_NOPE].astype(BF16)
        k_ref[hh, :, MLA_NOPE:] = kpe
        v_ref[hh, :, 0:MLA_V] = r[:, MLA_NOPE:].astype(BF16)
        v_ref[hh, :, MLA_V:] = jnp.ones((r.shape[0], MXU_DIM - MLA_V), BF16)


def _kv_proj(h, w_ckv, g_kv, w_kv, tab, tm=512):
    m = h.shape[0]
    nt = tab.shape[0] // tm
    return pl.pallas_call(
        _kvproj_kernel,
        out_shape=(jax.ShapeDtypeStruct((MLA_HEADS, m, MXU_DIM), BF16),
                   jax.ShapeDtypeStruct((MLA_HEADS, m, MXU_DIM), BF16)),
        grid=(m // tm,),
        in_specs=[
            pl.BlockSpec((tm, D_MODEL), lambda i: (i, 0)),
            _const_spec(w_ckv.shape),
            _const_spec((1, MLA_KV_LORA)),
            _const_spec(w_kv.shape),
            pl.BlockSpec((tm, LANES), lambda i: (i % nt, 0)),
        ],
        out_specs=(pl.BlockSpec((MLA_HEADS, tm, MXU_DIM), lambda i: (0, i, 0)),
                   pl.BlockSpec((MLA_HEADS, tm, MXU_DIM), lambda i: (0, i, 0))),
        compiler_params=_cparams("parallel"),
        name="mla_kv_proj",
    )(h, w_ckv, g_kv.reshape(1, -1), w_kv, tab)


MLA_KEY_CHUNK = 512


def _mla_kernel(q_ref, *refs, seg_lens):
    nseg = len(seg_lens)
    o_ref = refs[2 * nseg]
    q = q_ref[...]
    m = jnp.full((q.shape[0], 1), NEG_BIG, F32)
    acc = jnp.zeros((q.shape[0], MXU_DIM), F32)
    for s, lk in enumerate(seg_lens):
        k_ref, v_ref = refs[2 * s], refs[2 * s + 1]
        step = min(MLA_KEY_CHUNK, lk)
        for c0 in range(0, lk, step):
            sc = _dot_nt(q, k_ref[c0:c0 + step, :])
            m_new = jnp.maximum(m, sc.max(axis=-1, keepdims=True))
            p = jnp.exp2(sc - m_new).astype(BF16)
            acc = acc * jnp.exp2(m - m_new) + _dot(p, v_ref[c0:c0 + step, :])
            m = m_new
    o_ref[...] = (acc[:, :MLA_V] / acc[:, MLA_V:MLA_V + 1]).astype(o_ref.dtype)


def _mla_attention(q, segs, lq, tq=512):
    nq = lq // tq
    in_specs = [pl.BlockSpec((None, tq, MXU_DIM), lambda b, h, i: (h, b * nq + i, 0))]
    args = [q]
    for k, v, lk in segs:
        in_specs.append(pl.BlockSpec((None, lk, MXU_DIM), lambda b, h, i: (h, b, 0)))
        in_specs.append(pl.BlockSpec((None, lk, MXU_DIM), lambda b, h, i: (h, b, 0)))
        args += [k, v]
    return pl.pallas_call(
        functools.partial(_mla_kernel, seg_lens=tuple(lk for _, _, lk in segs)),
        out_shape=jax.ShapeDtypeStruct((BATCH * lq, MLA_HEADS * MLA_V), BF16),
        grid=(BATCH, MLA_HEADS, nq),
        in_specs=in_specs,
        out_specs=pl.BlockSpec((tq, MLA_V), lambda b, h, i: (b * nq + i, h)),
        compiler_params=_cparams("parallel", "parallel", "arbitrary"),
        name="mla_attention",
    )(*args)


def _dft_tables(n_len):
    n2 = 2 * n_len
    idx = np.arange(n_len, dtype=np.int64)
    ang = (2.0 * np.pi / n2) * ((idx[:, None] * idx[None, :]) % n2).astype(np.float64)
    cm = np.cos(ang)
    sf = np.sin(ang)
    sf[0, :] = 1.0 - 2.0 * (idx % 2)
    return (jnp.asarray(cm, dtype=BF16), jnp.asarray(sf, dtype=BF16),
            jnp.asarray(sf.T.copy(), dtype=BF16))


def _filter_tables(n_len):
    pos = np.arange(n_len, dtype=np.float64)
    t = pos / max(n_len - 1, 1)
    bands = np.linspace(1e-4, HY_BANDS - 1, HY_BANDS)
    ang = (2.0 * math.pi / n_len) * pos[:, None] * bands[None]
    z = np.zeros((n_len, LANES), np.float64)
    z[:, 0] = t
    z[:, 1:1 + HY_BANDS] = np.cos(ang)
    z[:, 1 + HY_BANDS:HY_EMB] = -np.sin(ang)
    deltas = np.linspace(math.log(HY_DECAY_TARGET) / HY_SLOW_DECAY_PCT,
                         math.log(HY_DECAY_TARGET) / HY_QUICK_DECAY_PCT, HY_WIDTH)
    decay = np.exp(-t[:, None] * np.abs(deltas)[None])
    return jnp.asarray(z, dtype=F32), jnp.asarray(decay, dtype=F32)


def _hy_filter_kernel(z_ref, w1_ref, b1_ref, w2_ref, b2_ref, w3_ref, b3_ref, wf_ref, wb_ref,
                      fr_ref, dec_ref, cm_ref, sf_ref, a_ref, b_ref, d_ref, hid_ref, *, n_len):
    @pl.when(pl.program_id(0) == 0)
    def _():
        fr = fr_ref[...]
        h = jnp.sin(fr * (_dot3(z_ref[...], w1_ref[...]) + b1_ref[...]))
        h = jnp.sin(fr * (_dot3(h, w2_ref[...]) + b2_ref[...]))
        hid_ref[...] = jnp.sin(fr * (_dot3(h, w3_ref[...]) + b3_ref[...]))

    h = hid_ref[...]
    dec = dec_ref[...]
    h_f = _dot3(h, wf_ref[...]) * dec
    h_b = _dot3(h, wb_ref[...]) * dec
    row = lax.broadcasted_iota(jnp.int32, h_f.shape, 0)
    h_b = jnp.where(row == 0, 0.0, h_b)
    hs = h_f + h_b
    hd = h_f - h_b
    hs_h, hs_l = _split_bf16(hs)
    hd_h, hd_l = _split_bf16(hd)
    cm = cm_ref[...]
    sf = sf_ref[...]
    t_re = _dot(cm, hs_h) + _dot(cm, hs_l)
    g_t = _dot(sf, hd_h) + _dot(sf, hd_l)
    t_ny = (_dot(sf[0:16, :], hs_h) + _dot(sf[0:16, :], hs_l))[0:1, :]
    inv_n = 1.0 / (2 * n_len)
    first = row == 0
    a_ref[...] = jnp.where(first, inv_n, 2.0 * inv_n) * t_re
    b_ref[...] = jnp.where(first, 0.0, -2.0 * inv_n * g_t)
    d_ref[...] = jnp.where(first, inv_n * t_ny, 2.0 * inv_n * t_re)


def _hy_filters(lp, n_len, tabs):
    z, decay = tabs["filt"]
    cm, sf, _ = tabs["dft"]
    cb = LANES
    nblk = HY_WIDTH // cb
    w1 = jnp.zeros((LANES, HY_FILTER_HIDDEN), F32).at[:HY_EMB].set(lp["hy_w1"])
    row = lambda a: a.reshape(1, -1)
    small = lambda shape: pl.BlockSpec(shape, lambda j: (0,) * len(shape))
    fh = HY_FILTER_HIDDEN
    out = jax.ShapeDtypeStruct((n_len, HY_WIDTH), F32)
    return pl.pallas_call(
        functools.partial(_hy_filter_kernel, n_len=n_len),
        out_shape=(out, out, out),
        grid=(nblk,),
        in_specs=[
            small((n_len, LANES)), small((LANES, fh)), small((1, fh)), small((fh, fh)), small((1, fh)),
            small((fh, fh)), small((1, fh)),
            pl.BlockSpec((fh, cb), lambda j: (0, j)),
            pl.BlockSpec((fh, cb), lambda j: (0, nblk + j)),
            small((1, fh)),
            pl.BlockSpec((n_len, cb), lambda j: (0, j)),
            _const_spec((n_len, n_len)), _const_spec((n_len, n_len)),
        ],
        out_specs=tuple(pl.BlockSpec((n_len, cb), lambda j: (0, j)) for _ in range(3)),
        scratch_shapes=[pltpu.VMEM((n_len, fh), F32)],
        compiler_params=_cparams("arbitrary"),
        name="hyena_filters",
    )(z, w1, row(lp["hy_b1"]), lp["hy_w2"], row(lp["hy_b2"]), lp["hy_w3"], row(lp["hy_b3"]),
      lp["hy_w_filt"], lp["hy_w_filt"], row(lp["hy_freq"]), decay, cm, sf)


def _hy_pre_kernel(u0_ref, u1_ref, u2_ref, w0_ref, w1_ref, w2_ref, b0_ref, b1_ref, b2_ref,
                   x0_ref, vx_ref, vxb_ref, *, n_len):
    def sconv(u_ref, w_ref, b_ref):
        u = u_ref[...]
        w = w_ref[...]
        row = lax.broadcasted_iota(jnp.int32, u.shape, 0)
        prev = jnp.where(row == 0, 0.0, pltpu.roll(u, 1, axis=0))
        nxt = jnp.where(row == n_len - 1, 0.0, pltpu.roll(u, n_len - 1, axis=0))
        return w[0:1] * prev + w[1:2] * u + w[2:3] * nxt + b_ref[...]

    x0_ref[...] = sconv(u0_ref, w0_ref, b0_ref)
    vx = sconv(u2_ref, w2_ref, b2_ref) * sconv(u1_ref, w1_ref, b1_ref)
    vx_ref[...] = vx
    vxb_ref[...] = vx.astype(BF16)


def _hy_pre(hy, conv_w, conv_b, n_len):
    nblk = HY_WIDTH // LANES
    uspec = lambda part: pl.BlockSpec((None, n_len, LANES), lambda b, j: (b, 0, part * nblk + j))
    wspec = lambda part: pl.BlockSpec((3, LANES), lambda b, j: (0, part * nblk + j))
    bspec = lambda part: pl.BlockSpec((1, LANES), lambda b, j: (0, part * nblk + j))
    ospec = pl.BlockSpec((None, n_len, LANES), lambda b, j: (b, 0, j))
    hy3 = hy.reshape(BATCH, n_len, 3 * HY_WIDTH)
    cb = conv_b.reshape(1, -1)
    return pl.pallas_call(
        functools.partial(_hy_pre_kernel, n_len=n_len),
        out_shape=(jax.ShapeDtypeStruct((BATCH, n_len, HY_WIDTH), F32),
                   jax.ShapeDtypeStruct((BATCH, n_len, HY_WIDTH), F32),
                   jax.ShapeDtypeStruct((BATCH, n_len, HY_WIDTH), BF16)),
        grid=(BATCH, nblk),
        in_specs=[uspec(0), uspec(1), uspec(2), wspec(0), wspec(1), wspec(2), bspec(0), bspec(1), bspec(2)],
        out_specs=(ospec, ospec, ospec),
        compiler_params=_cparams("parallel", "parallel"),
        name="hyena_short_conv",
    )(hy3, hy3, hy3, conv_w, conv_w, conv_w, cb, cb, cb)


def _hy_fwd_kernel(cm_ref, sf_ref, u_ref, a_ref, b_ref, d_ref, q_ref):
    u = u_ref[...]
    ur = _dot(cm_ref[...], u)
    g = _dot(sf_ref[...], u)
    b = b_ref[...]
    q_ref[0] = (a_ref[...] * ur + b * g).astype(BF16)
    q_ref[1] = (d_ref[...] * g - b * ur).astype(BF16)


def _hy_inv_kernel(cm_ref, si_ref, q_ref, x0_ref, vx_ref, skip_ref, o_ref):
    y = _dot(cm_ref[...], q_ref[0]) + _dot(si_ref[...], q_ref[1])
    o_ref[...] = (x0_ref[...] * (y + vx_ref[...] * skip_ref[...])).astype(o_ref.dtype)


def _hy_long_conv(x0, vx, vxb, spectra, skip, n_len, tabs):
    cm, sf, si = tabs["dft"]
    a, b, d = spectra
    tf = min(512, n_len)
    nf = n_len // tf
    mat = pl.BlockSpec((tf, n_len), lambda bb, f: (f, 0))
    spec = pl.BlockSpec((tf, HY_WIDTH), lambda bb, f: (f, 0))
    q = pl.pallas_call(
        _hy_fwd_kernel,
        out_shape=jax.ShapeDtypeStruct((BATCH, 2, n_len, HY_WIDTH), BF16),
        grid=(BATCH, nf),
        in_specs=[mat, mat, pl.BlockSpec((None, n_len, HY_WIDTH), lambda bb, f: (bb, 0, 0)), spec, spec, spec],
        out_specs=pl.BlockSpec((None, 2, tf, HY_WIDTH), lambda bb, f: (bb, 0, f, 0)),
        compiler_params=_cparams("parallel", "arbitrary"),
        name="hyena_dft_fwd",
    )(cm, sf, vxb, a, b, d)
    tile = pl.BlockSpec((None, tf, HY_WIDTH), lambda bb, f: (bb, f, 0))
    return pl.pallas_call(
        _hy_inv_kernel,
        out_shape=jax.ShapeDtypeStruct((BATCH, n_len, HY_WIDTH), BF16),
        grid=(BATCH, nf),
        in_specs=[mat, mat, pl.BlockSpec((None, 2, n_len, HY_WIDTH), lambda bb, f: (bb, 0, 0, 0)),
                  tile, tile, pl.BlockSpec((1, HY_WIDTH), lambda bb, f: (0, 0))],
        out_specs=tile,
        compiler_params=_cparams("parallel", "arbitrary"),
        name="hyena_dft_inv",
    )(cm, si, q, x0, vx, skip.reshape(1, -1))


_SWA_Q = SWA_HEADS * SWA_HEAD_DIM
_SWA_KV = SWA_KV_HEADS * SWA_HEAD_DIM


def _swaproj_kernel(h_ref, w_ref, cos_ref, sin_ref, q_ref, k_ref, v_ref, *, rope):
    t = _dot(h_ref[...], w_ref[...])
    q = t[:, :_SWA_Q]
    k = t[:, 2 * _SWA_Q:2 * _SWA_Q + _SWA_KV]
    v = t[:, 2 * _SWA_Q + 2 * _SWA_KV:]
    if rope:
        cos = cos_ref[...]
        sin = sin_ref[...]
        q = q * cos + t[:, _SWA_Q:2 * _SWA_Q] * sin
        k = k * cos[:, :_SWA_KV] + t[:, 2 * _SWA_Q + _SWA_KV:2 * _SWA_Q + 2 * _SWA_KV] * sin[:, :_SWA_KV]
    q_ref[...] = (q * (SWA_HEAD_DIM ** -0.5 * math.log2(math.e))).astype(BF16)
    lo = lax.broadcasted_iota(jnp.int32, k.shape, 1) < SWA_HEAD_DIM
    for src, dst in ((k, k_ref), (v, v_ref)):
        sw = pltpu.roll(src, SWA_HEAD_DIM, axis=1)
        dst[:, 0 * LANES:1 * LANES] = jnp.where(lo, src, 0.0).astype(BF16)
        dst[:, 1 * LANES:2 * LANES] = jnp.where(lo, 0.0, sw).astype(BF16)
        dst[:, 2 * LANES:3 * LANES] = jnp.where(lo, sw, 0.0).astype(BF16)
        dst[:, 3 * LANES:4 * LANES] = jnp.where(lo, 0.0, src).astype(BF16)


def _hy_swa_kernel(h_ref, why_ref, w_ref, cos_ref, sin_ref, hy_ref, q_ref, k_ref, v_ref, *, rope):
    hy_ref[...] = _dot(h_ref[...], why_ref[...])
    _swaproj_kernel(h_ref, w_ref, cos_ref, sin_ref, q_ref, k_ref, v_ref, rope=rope)


def _swa_proj(h, w, cos_t, sin_t, rope, w_hy=None, tm=512):
    m = h.shape[0]
    nt = cos_t.shape[0] // tm
    o = jax.ShapeDtypeStruct((m, 4 * LANES), BF16)
    ospec = pl.BlockSpec((tm, 4 * LANES), lambda i: (i, 0))
    tspec = pl.BlockSpec((tm, _SWA_Q), lambda i: (i % nt, 0))
    hspec = pl.BlockSpec((tm, D_MODEL), lambda i: (i, 0))
    if w_hy is None:
        return pl.pallas_call(
            functools.partial(_swaproj_kernel, rope=rope),
            out_shape=(o, o, o),
            grid=(m // tm,),
            in_specs=[hspec, _const_spec(w.shape), tspec, tspec],
            out_specs=(ospec, ospec, ospec),
            compiler_params=_cparams("parallel"),
            name="swa_proj",
        )(h, w, cos_t, sin_t)
    n_hy = w_hy.shape[1]
    return pl.pallas_call(
        functools.partial(_hy_swa_kernel, rope=rope),
        out_shape=(jax.ShapeDtypeStruct((m, n_hy), F32), o, o, o),
        grid=(m // tm,),
        in_specs=[hspec, _const_spec(w_hy.shape), _const_spec(w.shape), tspec, tspec],
        out_specs=(pl.BlockSpec((tm, n_hy), lambda i: (i, 0)), ospec, ospec, ospec),
        compiler_params=_cparams("parallel"),
        name="hyena_swa_proj",
    )(h, w_hy, w, cos_t, sin_t)


def _swa_kernel(sink_ref, q_ref, *refs, tq, win, lk, has_lat):
    if has_lat:
        kl_ref, vl_ref, kc_ref, vc_ref, o_ref = refs
        t0 = pl.program_id(1) * tq
        start = pl.multiple_of(jnp.clip(t0 - SWA_WINDOW, 0, lk - win), SWA_WINDOW)
        qpos = t0 + lax.broadcasted_iota(jnp.int32, (tq, win), 0)
        kpos = start + lax.broadcasted_iota(jnp.int32, (tq, win), 1)
        valid = jnp.abs(qpos - kpos) <= SWA_WINDOW
        valid = jnp.concatenate([valid, valid], axis=0)
    else:
        kc_ref, vc_ref, o_ref = refs
    upper = lax.broadcasted_iota(jnp.int32, (2 * tq, 1), 0) < tq
    for g in range(SWA_KV_HEADS):
        j0 = 2 * g
        qb = jnp.concatenate([q_ref[:, j0 * LANES:(j0 + 1) * LANES],
                              q_ref[:, (j0 + 1) * LANES:(j0 + 2) * LANES]], axis=0)
        acc = None
        for par in range(2):
            c0 = (2 * g + par) * LANES
            sk = jnp.where(upper, sink_ref[2 * j0 + par], sink_ref[2 * j0 + 2 + par]) * math.log2(math.e)
            s_c = _dot_nt(qb, kc_ref[:, c0:c0 + LANES])
            m = jnp.maximum(s_c.max(axis=-1, keepdims=True), sk)
            if has_lat:
                s_l = _dot_nt(qb, kl_ref[pl.ds(start, win), c0:c0 + LANES])
                s_l = jnp.where(valid, s_l, NEG_BIG)
                m = jnp.maximum(m, s_l.max(axis=-1, keepdims=True))
            p_c = jnp.exp2(s_c - m)
            den = p_c.sum(axis=-1, keepdims=True) + jnp.exp2(sk - m)
            if has_lat:
                p_l = jnp.exp2(s_l - m)
                den = den + p_l.sum(axis=-1, keepdims=True)
            o = _dot(p_c.astype(BF16), vc_ref[:, c0:c0 + LANES])
            if has_lat:
                o = o + _dot(p_l.astype(BF16), vl_ref[pl.ds(start, win), c0:c0 + LANES])
            o = o * (1.0 / den)
            acc = o if acc is None else acc + o
        o_ref[:, j0 * LANES:(j0 + 1) * LANES] = acc[:tq].astype(o_ref.dtype)
        o_ref[:, (j0 + 1) * LANES:(j0 + 2) * LANES] = acc[tq:].astype(o_ref.dtype)


def _swa_attention(sink, q, lat, ctx, lq, tq=256):
    nq = lq // tq
    win = tq + 2 * SWA_WINDOW
    full = lambda n: pl.BlockSpec((n, 4 * LANES), lambda b, i: (b, 0))
    in_specs = [pl.BlockSpec(memory_space=pltpu.SMEM),
                pl.BlockSpec((tq, 4 * LANES), lambda b, i: (b * nq + i, 0))]
    args = [sink, q]
    if lat is not None:
        in_specs += [full(SEQ), full(SEQ)]
        args += list(lat)
    in_specs += [full(CTX_LEN), full(CTX_LEN)]
    args += list(ctx)
    return pl.pallas_call(
        functools.partial(_swa_kernel, tq=tq, win=win, lk=SEQ, has_lat=lat is not None),
        out_shape=jax.ShapeDtypeStruct((BATCH * lq, 4 * LANES), BF16),
        grid=(BATCH, nq),
        in_specs=in_specs,
        out_specs=pl.BlockSpec((tq, 4 * LANES), lambda b, i: (b * nq + i, 0)),
        compiler_params=_cparams("parallel", "arbitrary"),
        name="swa_attention",
    )(*args)


def _out_kernel(a_ref, y_ref, s_ref, w_ref, x_ref, m_ref, g_ref, xo_ref, ho_ref):
    na = MLA_HEADS * MLA_V
    acc = _dot(a_ref[...], w_ref[0:na, :])
    acc = acc + _dot(y_ref[...], w_ref[na:na + HY_WIDTH, :])
    acc = acc + _dot(s_ref[...], w_ref[na + HY_WIDTH:, :])
    x = x_ref[...] + m_ref[2:3, :] * acc
    xo_ref[...] = x
    ho_ref[...] = (_rms(x, g_ref[...]) * (1.0 + m_ref[4:5, :]) + m_ref[3:4, :]).astype(ho_ref.dtype)


def _out_proj(a, y, s, w_out, x, mods, g_ffn, groups, row0, h_dtype, tm=512):
    m = x.shape[0]
    nr = m // groups // tm
    tile = lambda n: pl.BlockSpec((tm, n), lambda g_, i: (g_ * nr + i, 0))
    return pl.pallas_call(
        _out_kernel,
        out_shape=(jax.ShapeDtypeStruct((m, D_MODEL), F32), jax.ShapeDtypeStruct((m, D_MODEL), h_dtype)),
        grid=(groups, nr),
        in_specs=[tile(a.shape[1]), tile(y.shape[1]), tile(s.shape[1]), _const_spec(w_out.shape),
                  tile(D_MODEL), pl.BlockSpec((None, 6, D_MODEL), lambda g_, i: (row0 + g_, 0, 0)),
                  pl.BlockSpec((1, D_MODEL), lambda g_, i: (0, 0))],
        out_specs=(tile(D_MODEL), tile(D_MODEL)),
        compiler_params=_cparams("parallel", "parallel"),
        name="out_proj",
    )(a, y, s, w_out, x, mods, g_ffn.reshape(1, -1))


_META_W1, _META_W2, _META_E1, _META_E2, _META_R1, _META_R2 = range(6)


def _router_kernel(h_ref, w_ref, meta_ref, cnt_ref, carry_ref):
    @pl.when(pl.program_id(0) == 0)
    def _():
        carry_ref[...] = jnp.zeros_like(carry_ref)

    logits = _dot3(h_ref[...], w_ref[...])
    tm = logits.shape[0]
    lane = lax.broadcasted_iota(jnp.int32, logits.shape, 1).astype(F32)
    logits = jnp.where(lane < N_EXPERTS, logits, NEG_BIG)
    m1 = logits.max(axis=-1, keepdims=True)
    i1 = jnp.where(logits == m1, lane, float(LANES)).min(axis=-1, keepdims=True)
    rest = jnp.where(lane == i1, NEG_BIG, logits)
    m2 = rest.max(axis=-1, keepdims=True)
    i2 = jnp.where(rest == m2, lane, float(LANES)).min(axis=-1, keepdims=True)
    e2 = jnp.exp(m2 - m1)
    w1 = 1.0 / (1.0 + e2)
    hot = jnp.where((lane == i1) | (lane == i2), 1.0, 0.0)
    before = (lax.broadcasted_iota(jnp.int32, (tm, tm), 0) > lax.broadcasted_iota(jnp.int32, (tm, tm), 1))
    seen = _dot(before.astype(F32).astype(BF16), hot.astype(BF16)) + carry_ref[0:1, :]
    r1 = jnp.where(lane == i1, seen, 0.0).sum(axis=-1, keepdims=True)
    r2 = jnp.where(lane == i2, seen, 0.0).sum(axis=-1, keepdims=True)
    carry_ref[...] = carry_ref[...] + hot.sum(axis=0, keepdims=True)
    cnt_ref[...] = carry_ref[...]
    rec = jnp.zeros_like(logits)
    for k, v in ((_META_W1, w1), (_META_W2, e2 * w1), (_META_E1, i1), (_META_E2, i2), (_META_R1, r1), (_META_R2, r2)):
        rec = jnp.where(lane == float(k), v, rec)
    meta_ref[...] = rec


def _router(h, w_router, tm=512):
    m = h.shape[0]
    wp = jnp.zeros((D_MODEL, LANES), F32).at[:, :N_EXPERTS].set(w_router)
    return pl.pallas_call(
        _router_kernel,
        out_shape=(jax.ShapeDtypeStruct((m, LANES), F32), jax.ShapeDtypeStruct((8, LANES), F32)),
        grid=(m // tm,),
        in_specs=[pl.BlockSpec((tm, D_MODEL), lambda i: (i, 0)),
                  pl.BlockSpec((D_MODEL, LANES), lambda i: (0, 0))],
        out_specs=(pl.BlockSpec((tm, LANES), lambda i: (i, 0)), pl.BlockSpec((8, LANES), lambda i: (0, 0))),
        scratch_shapes=[pltpu.VMEM((8, LANES), F32)],
        compiler_params=_cparams("arbitrary"),
        name="moe_router",
    )(h, wp)


MOE_TM = 1024
MOE_BLK = 128
MOE_ROWS = (MOE_TM, 512, 128)


def _moe_plan(meta, counts, m):
    n_steps = -(-2 * m // MOE_TM) + N_EXPERTS
    cnt = counts[0, :N_EXPERTS].astype(jnp.int32)
    steps_e = (cnt + MOE_TM - 1) // MOE_TM
    ends = jnp.cumsum(steps_e)
    first = ends - steps_e
    total = ends[-1]
    e = meta[:, _META_E1:_META_E2 + 1].astype(jnp.int32)
    rank = meta[:, _META_R1:_META_R2 + 1].astype(jnp.int32)
    pos = (first * MOE_TM)[e] + rank
    s_idx = jnp.arange(n_steps, dtype=jnp.int32)
    step_e = jnp.minimum(jnp.searchsorted(ends, s_idx, side="right").astype(jnp.int32), N_EXPERTS - 1)
    valid = jnp.clip(cnt[step_e] - (s_idx - first[step_e]) * MOE_TM, 0, MOE_TM)
    valid = jnp.where(s_idx < total, valid, 0)
    return pos.reshape(-1), step_e, valid, n_steps


def _row_copy(src_ref, src_row, dst_ref, dst_row, sem):
    return pltpu.make_async_copy(src_ref.at[pl.ds(src_row, 1)], dst_ref.at[pl.ds(dst_row, 1)], sem)


def _invert_kernel(pos_ref, sv_ref, src_ref):
    def clear_step(s, c):
        def clear(i, c2):
            src_ref[s * MOE_TM + i] = 0
            return c2

        return lax.fori_loop(sv_ref[s], MOE_TM, clear, c)

    def place(t, c):
        p0 = pos_ref[2 * t]
        p1 = pos_ref[2 * t + 1]
        src_ref[p0] = t
        src_ref[p1] = t
        return c

    lax.fori_loop(0, sv_ref.shape[0], clear_step, 0)
    lax.fori_loop(0, pos_ref.shape[0] // 2, place, 0, unroll=8)


def _moe_invert(pos, valid, n_rows):
    smem = pl.BlockSpec(memory_space=pltpu.SMEM)
    return pl.pallas_call(
        _invert_kernel,
        out_shape=jax.ShapeDtypeStruct((n_rows,), jnp.int32),
        in_specs=[smem, smem], out_specs=smem,
        name="moe_invert",
    )(pos, valid)


def _block_wait(src_ref, dst_ref, rows, sem):
    pltpu.make_async_copy(src_ref.at[pl.ds(0, rows)], dst_ref.at[pl.ds(0, rows)], sem).wait()


def _moe_ffn_kernel(se_ref, sv_ref, src_ref, h_ref, wg_ref, wu_ref, wd_ref, o_ref, x32_ref, xb_ref, sem):
    del se_ref
    s = pl.program_id(0)
    f = pl.program_id(1)
    valid = sv_ref[s]

    def rows_used(v):
        r = jnp.where(v > 0, MOE_ROWS[-1], 0)
        for small, big in zip(MOE_ROWS[:0:-1], MOE_ROWS[-2::-1]):
            r = jnp.where(v > small, big, r)
        return r

    def blocks(v):
        return rows_used(v) // MOE_BLK

    def gather(step):
        base = step * MOE_TM

        def issue(i, c):
            _row_copy(h_ref, src_ref[base + i], x32_ref, i, sem).start()
            return c

        lax.fori_loop(0, rows_used(sv_ref[step]), issue, 0)

    @pl.when(f == 0)
    def _():
        @pl.when(s == 0)
        def _():
            gather(0)

        def land(i, c):
            _block_wait(h_ref, x32_ref, MOE_BLK, sem)
            return c

        def to_bf16(i, c):
            r0 = pl.multiple_of(i * MOE_BLK, MOE_BLK)
            xb_ref[pl.ds(r0, MOE_BLK), :] = x32_ref[pl.ds(r0, MOE_BLK), :].astype(BF16)
            return c

        lax.fori_loop(0, blocks(valid), land, 0)
        lax.fori_loop(0, blocks(valid), to_bf16, 0)

        @pl.when(s + 1 < pl.num_programs(0))
        def _():
            gather(s + 1)

        o_ref[...] = jnp.zeros_like(o_ref)

    def compute(rows):
        xs = xb_ref[0:rows, :]
        mid = _silu(_dot(xs, wg_ref[...].astype(BF16))) * _dot(xs, wu_ref[...].astype(BF16))
        o_ref[0:rows, :] += _dot(mid.astype(BF16), wd_ref[...].astype(BF16))

    for k, rows in enumerate(MOE_ROWS):
        lower = MOE_ROWS[k + 1] if k + 1 < len(MOE_ROWS) else 0

        @pl.when((valid > lower) & (valid <= rows))
        def _():
            compute(rows)


def _moe_ffn(h, src, w_gate, w_up, w_down, step_e, valid, n_steps, tf=512):
    assert all(r % MOE_BLK == 0 for r in MOE_ROWS) and MOE_ROWS[0] == MOE_TM
    n_ff = w_gate.shape[-1]
    nf = n_ff // tf
    fidx = lambda s, f, sv: jnp.where(sv[s] > 0, f, nf - 1)
    return pl.pallas_call(
        _moe_ffn_kernel,
        out_shape=jax.ShapeDtypeStruct((n_steps * MOE_TM, D_MODEL), F32),
        grid_spec=pltpu.PrefetchScalarGridSpec(
            num_scalar_prefetch=3, grid=(n_steps, nf),
            in_specs=[
                pl.BlockSpec(memory_space=pl.ANY),
                pl.BlockSpec((None, D_MODEL, tf), lambda s, f, se, sv, sr: (se[s], 0, fidx(s, f, sv))),
                pl.BlockSpec((None, D_MODEL, tf), lambda s, f, se, sv, sr: (se[s], 0, fidx(s, f, sv))),
                pl.BlockSpec((None, tf, D_MODEL), lambda s, f, se, sv, sr: (se[s], fidx(s, f, sv), 0)),
            ],
            out_specs=pl.BlockSpec((MOE_TM, D_MODEL), lambda s, f, se, sv, sr: (s, 0), pipeline_mode=pl.Buffered(1)),
            scratch_shapes=[pltpu.VMEM((MOE_TM, D_MODEL), F32), pltpu.VMEM((MOE_TM, D_MODEL), BF16),
                            pltpu.SemaphoreType.DMA(())]),
        compiler_params=_cparams("arbitrary", "arbitrary"),
        name="moe_experts",
    )(step_e, valid, src, h, w_gate, w_up, w_down)


def _combine_kernel(pos_ref, meta_ref, x_ref, m_ref, gf_ref, ys_ref, o_ref, buf_ref, sem, *, tm, final):
    base = pl.program_id(0) * tm

    def issue(i, c):
        for slot in range(2):
            _row_copy(ys_ref, pos_ref[2 * (base + i) + slot], buf_ref.at[slot], i, sem).start()
        return c

    lax.fori_loop(0, tm, issue, 0, unroll=4)
    for slot in range(2):
        _block_wait(ys_ref, buf_ref.at[slot], tm, sem)
    meta = meta_ref[...]
    y = meta[:, _META_W1:_META_W1 + 1] * buf_ref[0] + meta[:, _META_W2:_META_W2 + 1] * buf_ref[1]
    x = x_ref[...] + m_ref[5:6, :] * y
    if final:
        x = _rms(x, gf_ref[...])
    o_ref[...] = x


def _moe_combine(pos, meta, x, mods, ys, groups, row0, g_final, tm=512):
    m = x.shape[0]
    nr = m // groups // tm
    final = g_final is not None
    gf = (g_final if final else jnp.ones((D_MODEL,), F32)).reshape(1, D_MODEL)
    return pl.pallas_call(
        functools.partial(_combine_kernel, tm=tm, final=final),
        out_shape=jax.ShapeDtypeStruct((m, D_MODEL), F32),
        grid_spec=pltpu.PrefetchScalarGridSpec(
            num_scalar_prefetch=1, grid=(m // tm,),
            in_specs=[pl.BlockSpec((tm, LANES), lambda i, p: (i, 0)),
                      pl.BlockSpec((tm, D_MODEL), lambda i, p: (i, 0)),
                      pl.BlockSpec((None, 6, D_MODEL), lambda i, p: (row0 + i // nr, 0, 0)),
                      pl.BlockSpec((1, D_MODEL), lambda i, p: (0, 0)),
                      pl.BlockSpec(memory_space=pl.ANY)],
            out_specs=pl.BlockSpec((tm, D_MODEL), lambda i, p: (i, 0)),
            scratch_shapes=[pltpu.VMEM((2, tm, D_MODEL), F32), pltpu.SemaphoreType.DMA(())]),
        compiler_params=_cparams("arbitrary"),
        name="moe_combine",
    )(pos, meta, x, mods, gf, ys)


def _moe(h, x, mods, groups, row0, w_router, w_gate, w_up, w_down, g_final):
    m = h.shape[0]
    meta, counts = _router(h, w_router)
    pos, step_e, valid, n_steps = _moe_plan(meta, counts, m)
    src = _moe_invert(pos, valid, n_steps * MOE_TM)
    ys = _moe_ffn(h, src, w_gate, w_up, w_down, step_e, valid, n_steps)
    return _moe_combine(pos, meta, x, mods, ys, groups, row0, g_final)


def _ffn_kernel(h_ref, wg_ref, wu_ref, wd_ref, x_ref, m_ref, gn_ref, mn_ref, o_ref, *maybe_hn_ref):
    f = pl.program_id(2)

    @pl.when(f == 0)
    def _():
        o_ref[...] = jnp.zeros_like(o_ref)

    h = h_ref[...]
    mid = _silu(_dot(h, wg_ref[...])) * _dot(h, wu_ref[...])
    o_ref[...] += _dot(mid.astype(BF16), wd_ref[...])

    @pl.when(f == pl.num_programs(2) - 1)
    def _():
        x = x_ref[...] + m_ref[5:6, :] * o_ref[...]
        o_ref[...] = x
        for hn_ref in maybe_hn_ref:
            hn_ref[...] = (_rms(x, gn_ref[...]) * (1.0 + mn_ref[1:2, :]) + mn_ref[0:1, :]).astype(hn_ref.dtype)


def _ffn(h, w_gate, w_up, w_down, x, mods, groups, row0, nxt, tm, tf=512):
    m = x.shape[0]
    n_ff = w_gate.shape[1]
    nr = m // groups // tm
    g_next, mods_next = nxt if nxt is not None else (jnp.ones((D_MODEL,), F32), mods)
    once = pl.Buffered(1)
    tile = lambda n, mode=None: pl.BlockSpec((tm, n), lambda g_, i, f: (g_ * nr + i, 0), pipeline_mode=mode)
    mod_spec = pl.BlockSpec((None, 6, D_MODEL), lambda g_, i, f: (row0 + g_, 0, 0))
    out_shape = [jax.ShapeDtypeStruct((m, D_MODEL), F32)]
    out_specs = [tile(D_MODEL, once)]
    if nxt is not None:
        out_shape.append(jax.ShapeDtypeStruct((m, D_MODEL), BF16))
        out_specs.append(tile(D_MODEL, once))
    outs = pl.pallas_call(
        _ffn_kernel,
        out_shape=tuple(out_shape),
        grid=(groups, nr, n_ff // tf),
        in_specs=[tile(D_MODEL),
                  pl.BlockSpec((D_MODEL, tf), lambda g_, i, f: (0, f)),
                  pl.BlockSpec((D_MODEL, tf), lambda g_, i, f: (0, f)),
                  pl.BlockSpec((tf, D_MODEL), lambda g_, i, f: (f, 0)),
                  tile(D_MODEL, once), mod_spec,
                  pl.BlockSpec((1, D_MODEL), lambda g_, i, f: (0, 0)), mod_spec],
        out_specs=tuple(out_specs),
        compiler_params=_cparams("parallel", "parallel", "arbitrary"),
        name="swiglu_ffn",
    )(h, w_gate, w_up, w_down, x, mods, g_next.reshape(1, D_MODEL), mods_next)
    return outs if nxt is not None else (outs[0], None)


def _swap_halves(w):
    half = w.shape[-1] // 2
    return jnp.concatenate([w[..., half:], w[..., :half]], axis=-1)


def _swap_heads(w, heads, dim):
    k = w.shape[0]
    return _swap_halves(w.reshape(k, heads, dim)).reshape(k, heads * dim)


def _rope_tables():
    rows = SEQ // GRID_W
    row = np.repeat(np.arange(rows), GRID_W).astype(np.float32)
    col = np.tile(np.arange(GRID_W), rows).astype(np.float32)
    quarter = MLA_ROPE // 4
    freqs = (np.float32(ROPE_THETA) ** (-np.arange(quarter, dtype=np.float32) / quarter)).astype(np.float32)
    ang = np.concatenate([row[:, None] * freqs[None], col[:, None] * freqs[None]], axis=-1)
    cos = np.cos(ang.astype(np.float64))
    sin = np.sin(ang.astype(np.float64))
    cos2 = np.concatenate([cos, cos], axis=-1)
    sin2 = np.concatenate([-sin, sin], axis=-1)
    tab_lat = np.concatenate([cos2, sin2], axis=-1)
    tab_ctx = np.concatenate([np.ones((CTX_LEN, MLA_ROPE)), np.zeros((CTX_LEN, MLA_ROPE))], axis=-1)
    return dict(
        tab_lat=jnp.asarray(tab_lat, F32), tab_ctx=jnp.asarray(tab_ctx, F32),
        cos8=jnp.asarray(np.tile(cos2, (1, SWA_HEADS)), F32),
        sin8=jnp.asarray(np.tile(sin2, (1, SWA_HEADS)), F32),
        ones8=jnp.ones((CTX_LEN, _SWA_Q), F32), zeros8=jnp.zeros((CTX_LEN, _SWA_Q), F32),
    )


def _layer_weights(p, l):
    w_in = p["w_in"][l]
    cq = w_in[:, _O_CQ:_O_CKV]
    ckv = w_in[:, _O_CKV:_O_KPE]
    kpe = w_in[:, _O_KPE:_O_HY]
    hy = w_in[:, _O_HY:_O_SQ]
    sq = w_in[:, _O_SQ:_O_SK]
    sk = w_in[:, _O_SK:_O_SV]
    sv = w_in[:, _O_SV:]
    dq = MLA_NOPE + MLA_ROPE
    wq = p["w_q_up"][l].reshape(MLA_Q_LORA, MLA_HEADS, dq)
    wq = jnp.concatenate([wq, _swap_halves(wq[..., MLA_NOPE:])], axis=-1)
    wkv = p["w_kv_up"][l].reshape(MLA_KV_LORA, MLA_HEADS, MLA_NOPE + MLA_V)
    return dict(
        w_cq=cq.astype(BF16),
        w_ckv=jnp.concatenate([ckv, kpe, _swap_halves(kpe)], axis=-1).astype(BF16),
        w_hy=hy.astype(BF16),
        w_swa=jnp.concatenate([sq, _swap_heads(sq, SWA_HEADS, SWA_HEAD_DIM), sk,
                               _swap_heads(sk, SWA_KV_HEADS, SWA_HEAD_DIM), sv], axis=-1).astype(BF16),
        w_q=jnp.transpose(wq, (1, 0, 2)).astype(BF16),
        w_kv=jnp.transpose(wkv, (1, 0, 2)).astype(BF16),
        w_out=p["w_out"][l].astype(BF16),
        g_q=p["g_q"][l], g_kv=p["g_kv"][l],
        hy_conv_w=p["hy_conv_w"][l], hy_conv_b=p["hy_conv_b"][l],
        hy_w1=p["hy_w1"][l], hy_b1=p["hy_b1"][l], hy_w2=p["hy_w2"][l], hy_b2=p["hy_b2"][l],
        hy_w3=p["hy_w3"][l], hy_b3=p["hy_b3"][l], hy_w_filt=p["hy_w_filt"][l],
        hy_freq=p["hy_freq"][l], hy_skip=p["hy_skip"][l], swa_sink=p["swa_sink"][l],
    )


def _hyena(hy, lw, n_len, tabs):
    x0, vx, vxb = _hy_pre(hy, lw["hy_conv_w"], lw["hy_conv_b"], n_len)
    spectra = _hy_filters(lw, n_len, tabs)
    y = _hy_long_conv(x0, vx, vxb, spectra, lw["hy_skip"], n_len, tabs)
    return y.reshape(BATCH * n_len, HY_WIDTH)


def _mixer(h_lat, h_ctx, lw, rope, hy_tabs, need_ctx):
    q_l, k_l, v_l = _mla_proj(h_lat, lw, rope["tab_lat"])
    if need_ctx:
        q_c, k_c, v_c = _mla_proj(h_ctx, lw, rope["tab_ctx"], tm=CTX_LEN)
    else:
        k_c, v_c = _kv_proj(h_ctx, lw["w_ckv"], lw["g_kv"], lw["w_kv"], rope["tab_ctx"], tm=CTX_LEN)
    a_l = _mla_attention(q_l, [(k_l, v_l, SEQ), (k_c, v_c, CTX_LEN)], SEQ)
    hy_l, sq_l, kk_l, vv_l = _swa_proj(h_lat, lw["w_swa"], rope["cos8"], rope["sin8"], True, w_hy=lw["w_hy"])
    y_l = _hyena(hy_l, lw, SEQ, hy_tabs[SEQ])
    if need_ctx:
        hy_c, sq_c, kk_c, vv_c = _swa_proj(h_ctx, lw["w_swa"], rope["ones8"], rope["zeros8"], False,
                                           w_hy=lw["w_hy"], tm=CTX_LEN)
    else:
        sq_c, kk_c, vv_c = _swa_proj(h_ctx, lw["w_swa"], rope["ones8"], rope["zeros8"], False, tm=CTX_LEN)
    s_l = _swa_attention(lw["swa_sink"], sq_l, (kk_l, vv_l), (kk_c, vv_c), SEQ)
    if not need_ctx:
        return (a_l, y_l, s_l), None
    a_c = _mla_attention(q_c, [(k_c, v_c, CTX_LEN)], CTX_LEN, tq=CTX_LEN)
    y_c = _hyena(hy_c, lw, CTX_LEN, hy_tabs[CTX_LEN])
    s_c = _swa_attention(lw["swa_sink"], sq_c, None, (kk_c, vv_c), CTX_LEN, tq=CTX_LEN)
    return (a_l, y_l, s_l), (a_c, y_c, s_c)


def _forward(p):
    rope = _rope_tables()
    hy_tabs = {n: dict(dft=_dft_tables(n), filt=_filter_tables(n)) for n in (SEQ, CTX_LEN)}
    cvec = jnp.zeros((MOD_ROWS, D_MODEL), F32).at[:BATCH].set(p["c"]).at[BATCH].set(p["c_ctx"])
    mods_all = _modulation(cvec, p["w_mod"], p["b_mod"]).reshape(DEPTH, MOD_ROWS, 6, D_MODEL)

    x_lat = p["x"].reshape(BATCH * SEQ, D_MODEL)
    x_ctx = p["ctx"].reshape(BATCH * CTX_LEN, D_MODEL)
    h_in = [None, None]
    for l in range(DEPTH):
        last = l == DEPTH - 1
        mods = mods_all[l]
        lw = _layer_weights(p, l)
        h_lat = h_in[0] if h_in[0] is not None else _norm_mod(x_lat, p["g_mix"][l], mods, BATCH, 0, 0, BF16)
        h_ctx = h_in[1] if h_in[1] is not None else _norm_mod(x_ctx, p["g_mix"][l], mods, 1, BATCH, 0, BF16)
        mix_l, mix_c = _mixer(h_lat, h_ctx, lw, rope, hy_tabs, not last)
        streams = [(x_lat, mix_l, BATCH, 0, 1024)]
        if not last:
            streams.append((x_ctx, mix_c, 1, BATCH, 512))
        outs = []
        h_in = [None, None]
        i = l // 2
        dense = l % 2 == 0
        for n, (x, mix, groups, row0, ffn_tm) in enumerate(streams):
            x, h2 = _out_proj(*mix, lw["w_out"], x, mods, p["g_ffn"][l], groups, row0, BF16 if dense else F32)
            if dense:
                nxt = None if last else (p["g_mix"][l + 1], mods_all[l + 1])
                x, h_in[n] = _ffn(h2, p["ffn_w_gate"][i].astype(BF16), p["ffn_w_up"][i].astype(BF16),
                                  p["ffn_w_down"][i].astype(BF16), x, mods, groups, row0, nxt, ffn_tm)
            else:
                g_final = p["g_final"] if (last and n == 0) else None
                x = _moe(h2, x, mods, groups, row0, p["moe_router"][i], p["moe_w_gate"][i], p["moe_w_up"][i],
                         p["moe_w_down"][i], g_final)
            outs.append(x)
        x_lat = outs[0]
        if not last:
            x_ctx = outs[1]
    if DEPTH % 2 == 1:
        x_lat = _norm_mod(x_lat, p["g_final"], mods_all[0], BATCH, 0, None, F32)
    return x_lat.reshape(BATCH, SEQ, D_MODEL)


def kernel(x, c, ctx, c_ctx, w_mod, b_mod, g_mix, g_ffn, w_in, g_q, w_q_up, g_kv, w_kv_up, hy_conv_w, hy_conv_b, hy_w1, hy_b1, hy_w2, hy_b2, hy_w3, hy_b3, hy_w_filt, hy_freq, hy_skip, swa_sink, w_out, ffn_w_gate, ffn_w_up, ffn_w_down, moe_router, moe_w_gate, moe_w_up, moe_w_down, g_final):
    return _forward(dict(
        x=x, c=c, ctx=ctx, c_ctx=c_ctx, w_mod=w_mod, b_mod=b_mod, g_mix=g_mix, g_ffn=g_ffn, w_in=w_in,
        g_q=g_q, w_q_up=w_q_up, g_kv=g_kv, w_kv_up=w_kv_up, hy_conv_w=hy_conv_w, hy_conv_b=hy_conv_b,
        hy_w1=hy_w1, hy_b1=hy_b1, hy_w2=hy_w2, hy_b2=hy_b2, hy_w3=hy_w3, hy_b3=hy_b3,
        hy_w_filt=hy_w_filt, hy_freq=hy_freq, hy_skip=hy_skip, swa_sink=swa_sink, w_out=w_out,
        ffn_w_gate=ffn_w_gate, ffn_w_up=ffn_w_up, ffn_w_down=ffn_w_down, moe_router=moe_router,
        moe_w_gate=moe_w_gate, moe_w_up=moe_w_up, moe_w_down=moe_w_down, g_final=g_final))
```

```python
import functools
import math

import numpy as np
import jax
import jax.numpy as jnp
from jax import lax
from jax.experimental import pallas as pl
from jax.experimental.pallas import tpu as pltpu

F32 = jnp.float32
BF16 = jnp.bfloat16

D_MODEL = 2048
BATCH = 4
SEQ = 2048
DEPTH = 2
CTX_LEN = 256
GRID_W = 64
NORM_EPS = 1e-6
ROPE_THETA = 10000.0
MLA_HEADS = 8
MLA_NOPE = 128
MLA_ROPE = 64
MLA_V = 128
MLA_Q_LORA = 768
MLA_KV_LORA = 512
HY_WIDTH = 512
HY_BANDS = 16
HY_EMB = 1 + 2 * HY_BANDS
HY_FILTER_HIDDEN = 64
HY_DECAY_TARGET = 1e-2
HY_QUICK_DECAY_PCT = 0.3
HY_SLOW_DECAY_PCT = 1.5
SWA_HEADS = 8
SWA_KV_HEADS = 2
SWA_HEAD_DIM = 64
SWA_WINDOW = 128
N_EXPERTS = 8
D_FF = 5632
D_FF_EXPERT = 7168

LANES = 128
MXU_DIM = 256
VMEM_LIMIT_BYTES = 56 * 1024 * 1024
NEG_BIG = -1e30
MOD_ROWS = 8

_O_CQ = 0
_O_CKV = _O_CQ + MLA_Q_LORA
_O_KPE = _O_CKV + MLA_KV_LORA
_O_HY = _O_KPE + MLA_ROPE
_O_SQ = _O_HY + 3 * HY_WIDTH
_O_SK = _O_SQ + SWA_HEADS * SWA_HEAD_DIM
_O_SV = _O_SK + SWA_KV_HEADS * SWA_HEAD_DIM


def _cparams(*sem):
    return pltpu.CompilerParams(dimension_semantics=sem, vmem_limit_bytes=VMEM_LIMIT_BYTES)


def _dot(a, b):
    return jnp.dot(a, b, preferred_element_type=F32)


def _dot_nt(a, b):
    return lax.dot_general(a, b, (((1,), (1,)), ((), ())), preferred_element_type=F32)


def _split_bf16(a):
    hi = a.astype(BF16)
    lo = (a - hi.astype(F32)).astype(BF16)
    return hi, lo


def _dot3(a, b):
    ah, al = _split_bf16(a)
    bh, bl = _split_bf16(b)
    return _dot(ah, bh) + (_dot(al, bh) + _dot(ah, bl))


def _silu(x):
    return x / (1.0 + jnp.exp(-x))


def _rms(x, g):
    ms = jnp.mean(x * x, axis=-1, keepdims=True)
    return x * lax.rsqrt(ms + NORM_EPS) * g


def _const_spec(shape):
    nd = len(shape)
    return pl.BlockSpec(shape, lambda *_: (0,) * nd, pipeline_mode=pl.Buffered(1))


def _mod_kernel(c_ref, w_ref, b_ref, o_ref):
    ah, al = _split_bf16(_silu(c_ref[...]))
    wh, wl = _split_bf16(w_ref[0])
    r = _dot(jnp.concatenate([ah, al], axis=0), wh)
    o_ref[0] = r[:MOD_ROWS] + (r[MOD_ROWS:] + _dot(ah, wl)) + b_ref[0]


def _modulation(cvec, w_mod, b_mod):
    tn = 1024
    n = 6 * D_MODEL
    return pl.pallas_call(
        _mod_kernel,
        out_shape=jax.ShapeDtypeStruct((DEPTH, MOD_ROWS, n), F32),
        grid=(DEPTH, n // tn),
        in_specs=[
            pl.BlockSpec((MOD_ROWS, D_MODEL), lambda l, j: (0, 0)),
            pl.BlockSpec((1, D_MODEL, tn), lambda l, j: (l, 0, j)),
            pl.BlockSpec((1, 1, tn), lambda l, j: (l, 0, j)),
        ],
        out_specs=pl.BlockSpec((1, MOD_ROWS, tn), lambda l, j: (l, 0, j)),
        compiler_params=_cparams("arbitrary", "arbitrary"),
        name="adaln_mod",
    )(cvec, w_mod, b_mod.reshape(DEPTH, 1, n))


def _norm_kernel(x_ref, g_ref, m_ref, o_ref, *, si):
    y = _rms(x_ref[...], g_ref[...])
    if si is not None:
        y = y * (1.0 + m_ref[si + 1:si + 2, :]) + m_ref[si:si + 1, :]
    o_ref[...] = y.astype(o_ref.dtype)


def _norm_mod(x, g, mods, groups, row0, si, out_dtype, tm=512):
    m = x.shape[0]
    nr = m // groups // tm
    return pl.pallas_call(
        functools.partial(_norm_kernel, si=si),
        out_shape=jax.ShapeDtypeStruct((m, D_MODEL), out_dtype),
        grid=(groups, nr),
        in_specs=[
            pl.BlockSpec((tm, D_MODEL), lambda g_, i: (g_ * nr + i, 0)),
            pl.BlockSpec((1, D_MODEL), lambda g_, i: (0, 0)),
            pl.BlockSpec((None, 6, D_MODEL), lambda g_, i: (row0 + g_, 0, 0)),
        ],
        out_specs=pl.BlockSpec((tm, D_MODEL), lambda g_, i: (g_ * nr + i, 0)),
        compiler_params=_cparams("parallel", "parallel"),
        name="norm_mod",
    )(x, g.reshape(1, D_MODEL), mods)


def _rope_halves(pe_pair, tab):
    r = pe_pair * tab
    return r + pltpu.roll(r, MLA_ROPE, axis=1)


MLA_Q_SCALE = (MLA_NOPE + MLA_ROPE) ** -0.5 * math.log2(math.e)


def _qproj_body(h_ref, wc_ref, g_ref, wq_ref, tab_ref, o_ref):
    cq = _dot(h_ref[...], wc_ref[...])
    cqn = _rms(cq, g_ref[...]).astype(BF16)
    tab = tab_ref[...] * MLA_Q_SCALE
    for hh in range(MLA_HEADS):
        r = _dot(cqn, wq_ref[hh])
        o_ref[hh, :, 0:MLA_NOPE] = (r[:, :MLA_NOPE] * MLA_Q_SCALE).astype(BF16)
        o_ref[hh, :, MLA_NOPE:] = _rope_halves(r[:, MLA_NOPE:], tab).astype(BF16)


def _mla_proj_kernel(h_ref, wcq_ref, gq_ref, wq_ref, wckv_ref, gkv_ref, wkv_ref, tab_ref, q_ref, k_ref, v_ref):
    _qproj_body(h_ref, wcq_ref, gq_ref, wq_ref, tab_ref, q_ref)
    _kvproj_kernel(h_ref, wckv_ref, gkv_ref, wkv_ref, tab_ref, k_ref, v_ref)


def _mla_proj(h, lw, tab, tm=512):
    m = h.shape[0]
    nt = tab.shape[0] // tm
    head_out = jax.ShapeDtypeStruct((MLA_HEADS, m, MXU_DIM), BF16)
    head_spec = pl.BlockSpec((MLA_HEADS, tm, MXU_DIM), lambda i: (0, i, 0))
    return pl.pallas_call(
        _mla_proj_kernel,
        out_shape=(head_out, head_out, head_out),
        grid=(m // tm,),
        in_specs=[
            pl.BlockSpec((tm, D_MODEL), lambda i: (i, 0)),
            _const_spec(lw["w_cq"].shape), _const_spec((1, MLA_Q_LORA)), _const_spec(lw["w_q"].shape),
            _const_spec(lw["w_ckv"].shape), _const_spec((1, MLA_KV_LORA)), _const_spec(lw["w_kv"].shape),
            pl.BlockSpec((tm, LANES), lambda i: (i % nt, 0)),
        ],
        out_specs=(head_spec, head_spec, head_spec),
        compiler_params=_cparams("parallel"),
        name="mla_proj",
    )(h, lw["w_cq"], lw["g_q"].reshape(1, -1), lw["w_q"], lw["w_ckv"], lw["g_kv"].reshape(1, -1), lw["w_kv"], tab)


def _kvproj_kernel(h_ref, wc_ref, g_ref, wkv_ref, tab_ref, k_ref, v_ref):
    t = _dot(h_ref[...], wc_ref[...])
    ckvn = _rms(t[:, :MLA_KV_LORA], g_ref[...]).astype(BF16)
    rot = _rope_halves(t[:, MLA_KV_LORA:], tab_ref[...])
    lane = lax.broadcasted_iota(jnp.int32, rot.shape, 1)
    kpe = jnp.where(lane < MLA_ROPE, rot, 0.0).astype(BF16)
    for hh in range(MLA_HEADS):
        r = _dot(ckvn, wkv_ref[hh])
        k_ref[hh, :, 0:MLA_NOPE] = r[:, :MLA_NOPE].astype(BF16)
        k_ref[hh, :, MLA_NOPE:] = kpe
        v_ref[hh, :, 0:MLA_V] = r[:, MLA_NOPE:].astype(BF16)
        v_ref[hh, :, MLA_V:] = jnp.ones((r.shape[0], MXU_DIM - MLA_V), BF16)


def _kv_proj(h, w_ckv, g_kv, w_kv, tab, tm=512):
    m = h.shape[0]
    nt = tab.shape[0] // tm
    return pl.pallas_call(
        _kvproj_kernel,
        out_shape=(jax.ShapeDtypeStruct((MLA_HEADS, m, MXU_DIM), BF16),
                   jax.ShapeDtypeStruct((MLA_HEADS, m, MXU_DIM), BF16)),
        grid=(m // tm,),
        in_specs=[
            pl.BlockSpec((tm, D_MODEL), lambda i: (i, 0)),
            _const_spec(w_ckv.shape),
            _const_spec((1, MLA_KV_LORA)),
            _const_spec(w_kv.shape),
            pl.BlockSpec((tm, LANES), lambda i: (i % nt, 0)),
        ],
        out_specs=(pl.BlockSpec((MLA_HEADS, tm, MXU_DIM), lambda i: (0, i, 0)),
                   pl.BlockSpec((MLA_HEADS, tm, MXU_DIM), lambda i: (0, i, 0))),
        compiler_params=_cparams("parallel"),
        name="mla_kv_proj",
    )(h, w_ckv, g_kv.reshape(1, -1), w_kv, tab)


MLA_KEY_CHUNK = 512


def _mla_kernel(q_ref, *refs, seg_lens):
    nseg = len(seg_lens)
    o_ref = refs[2 * nseg]
    q = q_ref[...]
    m = jnp.full((q.shape[0], 1), NEG_BIG, F32)
    acc = jnp.zeros((q.shape[0], MXU_DIM), F32)
    for s, lk in enumerate(seg_lens):
        k_ref, v_ref = refs[2 * s], refs[2 * s + 1]
        step = min(MLA_KEY_CHUNK, lk)
        for c0 in range(0, lk, step):
            sc = _dot_nt(q, k_ref[c0:c0 + step, :])
            m_new = jnp.maximum(m, sc.max(axis=-1, keepdims=True))
            p = jnp.exp2(sc - m_new).astype(BF16)
            acc = acc * jnp.exp2(m - m_new) + _dot(p, v_ref[c0:c0 + step, :])
            m = m_new
    o_ref[...] = (acc[:, :MLA_V] / acc[:, MLA_V:MLA_V + 1]).astype(o_ref.dtype)


def _mla_attention(q, segs, lq, tq=512):
    nq = lq // tq
    in_specs = [pl.BlockSpec((None, tq, MXU_DIM), lambda b, h, i: (h, b * nq + i, 0))]
    args = [q]
    for k, v, lk in segs:
        in_specs.append(pl.BlockSpec((None, lk, MXU_DIM), lambda b, h, i: (h, b, 0)))
        in_specs.append(pl.BlockSpec((None, lk, MXU_DIM), lambda b, h, i: (h, b, 0)))
        args += [k, v]
    return pl.pallas_call(
        functools.partial(_mla_kernel, seg_lens=tuple(lk for _, _, lk in segs)),
        out_shape=jax.ShapeDtypeStruct((BATCH * lq, MLA_HEADS * MLA_V), BF16),
        grid=(BATCH, MLA_HEADS, nq),
        in_specs=in_specs,
        out_specs=pl.BlockSpec((tq, MLA_V), lambda b, h, i: (b * nq + i, h)),
        compiler_params=_cparams("parallel", "parallel", "arbitrary"),
        name="mla_attention",
    )(*args)


def _dft_tables(n_len):
    n2 = 2 * n_len
    idx = np.arange(n_len, dtype=np.int64)
    ang = (2.0 * np.pi / n2) * ((idx[:, None] * idx[None, :]) % n2).astype(np.float64)
    cm = np.cos(ang)
    sf = np.sin(ang)
    sf[0, :] = 1.0 - 2.0 * (idx % 2)
    return (jnp.asarray(cm, dtype=BF16), jnp.asarray(sf, dtype=BF16),
            jnp.asarray(sf.T.copy(), dtype=BF16))


def _filter_tables(n_len):
    pos = np.arange(n_len, dtype=np.float64)
    t = pos / max(n_len - 1, 1)
    bands = np.linspace(1e-4, HY_BANDS - 1, HY_BANDS)
    ang = (2.0 * math.pi / n_len) * pos[:, None] * bands[None]
    z = np.zeros((n_len, LANES), np.float64)
    z[:, 0] = t
    z[:, 1:1 + HY_BANDS] = np.cos(ang)
    z[:, 1 + HY_BANDS:HY_EMB] = -np.sin(ang)
    deltas = np.linspace(math.log(HY_DECAY_TARGET) / HY_SLOW_DECAY_PCT,
                         math.log(HY_DECAY_TARGET) / HY_QUICK_DECAY_PCT, HY_WIDTH)
    decay = np.exp(-t[:, None] * np.abs(deltas)[None])
    return jnp.asarray(z, dtype=F32), jnp.asarray(decay, dtype=F32)


def _hy_filter_kernel(z_ref, w1_ref, b1_ref, w2_ref, b2_ref, w3_ref, b3_ref, wf_ref, wb_ref,
                      fr_ref, dec_ref, cm_ref, sf_ref, a_ref, b_ref, d_ref, hid_ref, *, n_len):
    @pl.when(pl.program_id(0) == 0)
    def _():
        fr = fr_ref[...]
        h = jnp.sin(fr * (_dot3(z_ref[...], w1_ref[...]) + b1_ref[...]))
        h = jnp.sin(fr * (_dot3(h, w2_ref[...]) + b2_ref[...]))
        hid_ref[...] = jnp.sin(fr * (_dot3(h, w3_ref[...]) + b3_ref[...]))

    h = hid_ref[...]
    dec = dec_ref[...]
    h_f = _dot3(h, wf_ref[...]) * dec
    h_b = _dot3(h, wb_ref[...]) * dec
    row = lax.broadcasted_iota(jnp.int32, h_f.shape, 0)
    h_b = jnp.where(row == 0, 0.0, h_b)
    hs = h_f + h_b
    hd = h_f - h_b
    nc = hs.shape[1]
    hs2 = jnp.concatenate(_split_bf16(hs), axis=1)
    hd2 = jnp.concatenate(_split_bf16(hd), axis=1)
    sf = sf_ref[...]
    t2 = _dot(cm_ref[...], hs2)
    g2 = _dot(sf, hd2)
    n2 = _dot(sf[0:16, :], hs2)[0:1, :]
    t_re = t2[:, :nc] + t2[:, nc:]
    g_t = g2[:, :nc] + g2[:, nc:]
    t_ny = n2[:, :nc] + n2[:, nc:]
    inv_n = 1.0 / (2 * n_len)
    first = row == 0
    a_ref[...] = jnp.where(first, inv_n, 2.0 * inv_n) * t_re
    b_ref[...] = jnp.where(first, 0.0, -2.0 * inv_n * g_t)
    d_ref[...] = jnp.where(first, inv_n * t_ny, 2.0 * inv_n * t_re)


def _hy_filters(lp, n_len, tabs):
    z, decay = tabs["filt"]
    cm, sf, _ = tabs["dft"]
    cb = LANES
    nblk = HY_WIDTH // cb
    w1 = jnp.zeros((LANES, HY_FILTER_HIDDEN), F32).at[:HY_EMB].set(lp["hy_w1"])
    row = lambda a: a.reshape(1, -1)
    small = lambda shape: pl.BlockSpec(shape, lambda j: (0,) * len(shape))
    fh = HY_FILTER_HIDDEN
    out = jax.ShapeDtypeStruct((n_len, HY_WIDTH), F32)
    return pl.pallas_call(
        functools.partial(_hy_filter_kernel, n_len=n_len),
        out_shape=(out, out, out),
        grid=(nblk,),
        in_specs=[
            small((n_len, LANES)), small((LANES, fh)), small((1, fh)), small((fh, fh)), small((1, fh)),
            small((fh, fh)), small((1, fh)),
            pl.BlockSpec((fh, cb), lambda j: (0, j)),
            pl.BlockSpec((fh, cb), lambda j: (0, nblk + j)),
            small((1, fh)),
            pl.BlockSpec((n_len, cb), lambda j: (0, j)),
            _const_spec((n_len, n_len)), _const_spec((n_len, n_len)),
        ],
        out_specs=tuple(pl.BlockSpec((n_len, cb), lambda j: (0, j)) for _ in range(3)),
        scratch_shapes=[pltpu.VMEM((n_len, fh), F32)],
        compiler_params=_cparams("arbitrary"),
        name="hyena_filters",
    )(z, w1, row(lp["hy_b1"]), lp["hy_w2"], row(lp["hy_b2"]), lp["hy_w3"], row(lp["hy_b3"]),
      lp["hy_w_filt"], lp["hy_w_filt"], row(lp["hy_freq"]), decay, cm, sf)


def _hy_pre_kernel(u0_ref, u1_ref, u2_ref, w0_ref, w1_ref, w2_ref, b0_ref, b1_ref, b2_ref,
                   x0_ref, vx_ref, vxb_ref, *, n_len):
    def sconv(u_ref, w_ref, b_ref):
        u = u_ref[...]
        w = w_ref[...]
        row = lax.broadcasted_iota(jnp.int32, u.shape, 0)
        prev = jnp.where(row == 0, 0.0, pltpu.roll(u, 1, axis=0))
        nxt = jnp.where(row == n_len - 1, 0.0, pltpu.roll(u, n_len - 1, axis=0))
        return w[0:1] * prev + w[1:2] * u + w[2:3] * nxt + b_ref[...]

    x0_ref[...] = sconv(u0_ref, w0_ref, b0_ref)
    vx = sconv(u2_ref, w2_ref, b2_ref) * sconv(u1_ref, w1_ref, b1_ref)
    vx_ref[...] = vx
    vxb_ref[...] = vx.astype(BF16)


def _hy_pre(hy, conv_w, conv_b, n_len):
    nblk = HY_WIDTH // LANES
    uspec = lambda part: pl.BlockSpec((None, n_len, LANES), lambda b, j: (b, 0, part * nblk + j))
    wspec = lambda part: pl.BlockSpec((3, LANES), lambda b, j: (0, part * nblk + j))
    bspec = lambda part: pl.BlockSpec((1, LANES), lambda b, j: (0, part * nblk + j))
    ospec = pl.BlockSpec((None, n_len, LANES), lambda b, j: (b, 0, j))
    hy3 = hy.reshape(BATCH, n_len, 3 * HY_WIDTH)
    cb = conv_b.reshape(1, -1)
    return pl.pallas_call(
        functools.partial(_hy_pre_kernel, n_len=n_len),
        out_shape=(jax.ShapeDtypeStruct((BATCH, n_len, HY_WIDTH), F32),
                   jax.ShapeDtypeStruct((BATCH, n_len, HY_WIDTH), F32),
                   jax.ShapeDtypeStruct((BATCH, n_len, HY_WIDTH), BF16)),
        grid=(BATCH, nblk),
        in_specs=[uspec(0), uspec(1), uspec(2), wspec(0), wspec(1), wspec(2), bspec(0), bspec(1), bspec(2)],
        out_specs=(ospec, ospec, ospec),
        compiler_params=_cparams("parallel", "parallel"),
        name="hyena_short_conv",
    )(hy3, hy3, hy3, conv_w, conv_w, conv_w, cb, cb, cb)


def _hy_fwd_kernel(cm_ref, sf_ref, u_ref, a_ref, b_ref, d_ref, q_ref):
    u = u_ref[...]
    ur = _dot(cm_ref[...], u)
    g = _dot(sf_ref[...], u)
    b = b_ref[...]
    q_ref[0] = (a_ref[...] * ur + b * g).astype(BF16)
    q_ref[1] = (d_ref[...] * g - b * ur).astype(BF16)


def _hy_inv_kernel(cm_ref, si_ref, q_ref, x0_ref, vx_ref, skip_ref, o_ref):
    y = _dot(cm_ref[...], q_ref[0]) + _dot(si_ref[...], q_ref[1])
    o_ref[...] = (x0_ref[...] * (y + vx_ref[...] * skip_ref[...])).astype(o_ref.dtype)


def _hy_long_conv(x0, vx, vxb, spectra, skip, n_len, tabs):
    cm, sf, si = tabs["dft"]
    a, b, d = spectra
    tf = min(512, n_len)
    nf = n_len // tf
    mat = pl.BlockSpec((tf, n_len), lambda bb, f: (f, 0))
    spec = pl.BlockSpec((tf, HY_WIDTH), lambda bb, f: (f, 0))
    q = pl.pallas_call(
        _hy_fwd_kernel,
        out_shape=jax.ShapeDtypeStruct((BATCH, 2, n_len, HY_WIDTH), BF16),
        grid=(BATCH, nf),
        in_specs=[mat, mat, pl.BlockSpec((None, n_len, HY_WIDTH), lambda bb, f: (bb, 0, 0)), spec, spec, spec],
        out_specs=pl.BlockSpec((None, 2, tf, HY_WIDTH), lambda bb, f: (bb, 0, f, 0)),
        compiler_params=_cparams("parallel", "arbitrary"),
        name="hyena_dft_fwd",
    )(cm, sf, vxb, a, b, d)
    tile = pl.BlockSpec((None, tf, HY_WIDTH), lambda bb, f: (bb, f, 0))
    return pl.pallas_call(
        _hy_inv_kernel,
        out_shape=jax.ShapeDtypeStruct((BATCH, n_len, HY_WIDTH), BF16),
        grid=(BATCH, nf),
        in_specs=[mat, mat, pl.BlockSpec((None, 2, n_len, HY_WIDTH), lambda bb, f: (bb, 0, 0, 0)),
                  tile, tile, pl.BlockSpec((1, HY_WIDTH), lambda bb, f: (0, 0))],
        out_specs=tile,
        compiler_params=_cparams("parallel", "arbitrary"),
        name="hyena_dft_inv",
    )(cm, si, q, x0, vx, skip.reshape(1, -1))


_SWA_Q = SWA_HEADS * SWA_HEAD_DIM
_SWA_KV = SWA_KV_HEADS * SWA_HEAD_DIM


def _swaproj_kernel(h_ref, w_ref, cos_ref, sin_ref, q_ref, k_ref, v_ref, *, rope):
    t = _dot(h_ref[...], w_ref[...])
    q = t[:, :_SWA_Q]
    k = t[:, 2 * _SWA_Q:2 * _SWA_Q + _SWA_KV]
    v = t[:, 2 * _SWA_Q + 2 * _SWA_KV:]
    if rope:
        cos = cos_ref[...]
        sin = sin_ref[...]
        q = q * cos + t[:, _SWA_Q:2 * _SWA_Q] * sin
        k = k * cos[:, :_SWA_KV] + t[:, 2 * _SWA_Q + _SWA_KV:2 * _SWA_Q + 2 * _SWA_KV] * sin[:, :_SWA_KV]
    q_ref[...] = (q * (SWA_HEAD_DIM ** -0.5 * math.log2(math.e))).astype(BF16)
    lo = lax.broadcasted_iota(jnp.int32, k.shape, 1) < SWA_HEAD_DIM
    for src, dst in ((k, k_ref), (v, v_ref)):
        sw = pltpu.roll(src, SWA_HEAD_DIM, axis=1)
        dst[:, 0 * LANES:1 * LANES] = jnp.where(lo, src, 0.0).astype(BF16)
        dst[:, 1 * LANES:2 * LANES] = jnp.where(lo, 0.0, sw).astype(BF16)
        dst[:, 2 * LANES:3 * LANES] = jnp.where(lo, sw, 0.0).astype(BF16)
        dst[:, 3 * LANES:4 * LANES] = jnp.where(lo, 0.0, src).astype(BF16)


def _hy_swa_kernel(h_ref, why_ref, w_ref, cos_ref, sin_ref, hy_ref, q_ref, k_ref, v_ref, *, rope):
    hy_ref[...] = _dot(h_ref[...], why_ref[...])
    _swaproj_kernel(h_ref, w_ref, cos_ref, sin_ref, q_ref, k_ref, v_ref, rope=rope)


def _swa_proj(h, w, cos_t, sin_t, rope, w_hy=None, tm=512):
    m = h.shape[0]
    nt = cos_t.shape[0] // tm
    o = jax.ShapeDtypeStruct((m, 4 * LANES), BF16)
    ospec = pl.BlockSpec((tm, 4 * LANES), lambda i: (i, 0))
    tspec = pl.BlockSpec((tm, _SWA_Q), lambda i: (i % nt, 0))
    hspec = pl.BlockSpec((tm, D_MODEL), lambda i: (i, 0))
    if w_hy is None:
        return pl.pallas_call(
            functools.partial(_swaproj_kernel, rope=rope),
            out_shape=(o, o, o),
            grid=(m // tm,),
            in_specs=[hspec, _const_spec(w.shape), tspec, tspec],
            out_specs=(ospec, ospec, ospec),
            compiler_params=_cparams("parallel"),
            name="swa_proj",
        )(h, w, cos_t, sin_t)
    n_hy = w_hy.shape[1]
    return pl.pallas_call(
        functools.partial(_hy_swa_kernel, rope=rope),
        out_shape=(jax.ShapeDtypeStruct((m, n_hy), F32), o, o, o),
        grid=(m // tm,),
        in_specs=[hspec, _const_spec(w_hy.shape), _const_spec(w.shape), tspec, tspec],
        out_specs=(pl.BlockSpec((tm, n_hy), lambda i: (i, 0)), ospec, ospec, ospec),
        compiler_params=_cparams("parallel"),
        name="hyena_swa_proj",
    )(h, w_hy, w, cos_t, sin_t)


def _swa_kernel(sink_ref, q_ref, *refs, tq, win, lk, has_lat):
    if has_lat:
        kl_ref, vl_ref, kc_ref, vc_ref, o_ref = refs
        t0 = pl.program_id(1) * tq
        start = pl.multiple_of(jnp.clip(t0 - SWA_WINDOW, 0, lk - win), SWA_WINDOW)
        qpos = t0 + lax.broadcasted_iota(jnp.int32, (tq, win), 0)
        kpos = start + lax.broadcasted_iota(jnp.int32, (tq, win), 1)
        valid = jnp.abs(qpos - kpos) <= SWA_WINDOW
        valid = jnp.concatenate([valid, valid], axis=0)
    else:
        kc_ref, vc_ref, o_ref = refs
    upper = lax.broadcasted_iota(jnp.int32, (2 * tq, 1), 0) < tq
    for g in range(SWA_KV_HEADS):
        j0 = 2 * g
        qb = jnp.concatenate([q_ref[:, j0 * LANES:(j0 + 1) * LANES],
                              q_ref[:, (j0 + 1) * LANES:(j0 + 2) * LANES]], axis=0)
        acc = None
        for par in range(2):
            c0 = (2 * g + par) * LANES
            sk = jnp.where(upper, sink_ref[2 * j0 + par], sink_ref[2 * j0 + 2 + par]) * math.log2(math.e)
            s_c = _dot_nt(qb, kc_ref[:, c0:c0 + LANES])
            m = jnp.maximum(s_c.max(axis=-1, keepdims=True), sk)
            if has_lat:
                s_l = _dot_nt(qb, kl_ref[pl.ds(start, win), c0:c0 + LANES])
                s_l = jnp.where(valid, s_l, NEG_BIG)
                m = jnp.maximum(m, s_l.max(axis=-1, keepdims=True))
            p_c = jnp.exp2(s_c - m)
            den = p_c.sum(axis=-1, keepdims=True) + jnp.exp2(sk - m)
            if has_lat:
                p_l = jnp.exp2(s_l - m)
                den = den + p_l.sum(axis=-1, keepdims=True)
            o = _dot(p_c.astype(BF16), vc_ref[:, c0:c0 + LANES])
            if has_lat:
                o = o + _dot(p_l.astype(BF16), vl_ref[pl.ds(start, win), c0:c0 + LANES])
            o = o * (1.0 / den)
            acc = o if acc is None else acc + o
        o_ref[:, j0 * LANES:(j0 + 1) * LANES] = acc[:tq].astype(o_ref.dtype)
        o_ref[:, (j0 + 1) * LANES:(j0 + 2) * LANES] = acc[tq:].astype(o_ref.dtype)


def _swa_attention(sink, q, lat, ctx, lq, tq=256):
    nq = lq // tq
    win = tq + 2 * SWA_WINDOW
    full = lambda n: pl.BlockSpec((n, 4 * LANES), lambda b, i: (b, 0))
    in_specs = [pl.BlockSpec(memory_space=pltpu.SMEM),
                pl.BlockSpec((tq, 4 * LANES), lambda b, i: (b * nq + i, 0))]
    args = [sink, q]
    if lat is not None:
        in_specs += [full(SEQ), full(SEQ)]
        args += list(lat)
    in_specs += [full(CTX_LEN), full(CTX_LEN)]
    args += list(ctx)
    return pl.pallas_call(
        functools.partial(_swa_kernel, tq=tq, win=win, lk=SEQ, has_lat=lat is not None),
        out_shape=jax.ShapeDtypeStruct((BATCH * lq, 4 * LANES), BF16),
        grid=(BATCH, nq),
        in_specs=in_specs,
        out_specs=pl.BlockSpec((tq, 4 * LANES), lambda b, i: (b * nq + i, 0)),
        compiler_params=_cparams("parallel", "arbitrary"),
        name="swa_attention",
    )(*args)


def _out_kernel(a_ref, y_ref, s_ref, w_ref, x_ref, m_ref, g_ref, xo_ref, ho_ref):
    na = MLA_HEADS * MLA_V
    acc = _dot(a_ref[...], w_ref[0:na, :])
    acc = acc + _dot(y_ref[...], w_ref[na:na + HY_WIDTH, :])
    acc = acc + _dot(s_ref[...], w_ref[na + HY_WIDTH:, :])
    x = x_ref[...] + m_ref[2:3, :] * acc
    xo_ref[...] = x
    ho_ref[...] = (_rms(x, g_ref[...]) * (1.0 + m_ref[4:5, :]) + m_ref[3:4, :]).astype(ho_ref.dtype)


def _out_proj(a, y, s, w_out, x, mods, g_ffn, groups, row0, h_dtype, tm=512):
    m = x.shape[0]
    nr = m // groups // tm
    tile = lambda n: pl.BlockSpec((tm, n), lambda g_, i: (g_ * nr + i, 0))
    return pl.pallas_call(
        _out_kernel,
        out_shape=(jax.ShapeDtypeStruct((m, D_MODEL), F32), jax.ShapeDtypeStruct((m, D_MODEL), h_dtype)),
        grid=(groups, nr),
        in_specs=[tile(a.shape[1]), tile(y.shape[1]), tile(s.shape[1]), _const_spec(w_out.shape),
                  tile(D_MODEL), pl.BlockSpec((None, 6, D_MODEL), lambda g_, i: (row0 + g_, 0, 0)),
                  pl.BlockSpec((1, D_MODEL), lambda g_, i: (0, 0))],
        out_specs=(tile(D_MODEL), tile(D_MODEL)),
        compiler_params=_cparams("parallel", "parallel"),
        name="out_proj",
    )(a, y, s, w_out, x, mods, g_ffn.reshape(1, -1))


_META_W1, _META_W2, _META_E1, _META_E2, _META_R1, _META_R2 = range(6)


def _route(h, is_first, w_ref, meta_ref, cnt_ref, carry_ref):
    @pl.when(is_first)
    def _():
        carry_ref[...] = jnp.zeros_like(carry_ref)

    logits = _dot3(h, w_ref[...])
    tm = logits.shape[0]
    lane = lax.broadcasted_iota(jnp.int32, logits.shape, 1).astype(F32)
    logits = jnp.where(lane < N_EXPERTS, logits, NEG_BIG)
    m1 = logits.max(axis=-1, keepdims=True)
    i1 = jnp.where(logits == m1, lane, float(LANES)).min(axis=-1, keepdims=True)
    rest = jnp.where(lane == i1, NEG_BIG, logits)
    m2 = rest.max(axis=-1, keepdims=True)
    i2 = jnp.where(rest == m2, lane, float(LANES)).min(axis=-1, keepdims=True)
    e2 = jnp.exp(m2 - m1)
    w1 = 1.0 / (1.0 + e2)
    hot = jnp.where((lane == i1) | (lane == i2), 1.0, 0.0)
    before = (lax.broadcasted_iota(jnp.int32, (tm, tm), 0) > lax.broadcasted_iota(jnp.int32, (tm, tm), 1))
    seen = _dot(before.astype(F32).astype(BF16), hot.astype(BF16)) + carry_ref[0:1, :]
    r1 = jnp.where(lane == i1, seen, 0.0).sum(axis=-1, keepdims=True)
    r2 = jnp.where(lane == i2, seen, 0.0).sum(axis=-1, keepdims=True)
    carry_ref[...] = carry_ref[...] + hot.sum(axis=0, keepdims=True)
    cnt_ref[...] = carry_ref[...]
    rec = jnp.zeros_like(logits)
    for k, v in ((_META_W1, w1), (_META_W2, e2 * w1), (_META_E1, i1), (_META_E2, i2), (_META_R1, r1), (_META_R2, r2)):
        rec = jnp.where(lane == float(k), v, rec)
    meta_ref[...] = rec


def _out_route_kernel(a_ref, y_ref, s_ref, w_ref, x_ref, m_ref, g_ref, wr_ref, xo_ref, ho_ref, meta_ref, cnt_ref,
                      carry_ref):
    _out_kernel(a_ref, y_ref, s_ref, w_ref, x_ref, m_ref, g_ref, xo_ref, ho_ref)
    is_first = (pl.program_id(0) == 0) & (pl.program_id(1) == 0)
    _route(ho_ref[...], is_first, wr_ref, meta_ref, cnt_ref, carry_ref)


def _out_proj_route(a, y, s, w_out, x, mods, g_ffn, groups, row0, w_router, tm=512):
    m = x.shape[0]
    nr = m // groups // tm
    tile = lambda n: pl.BlockSpec((tm, n), lambda g_, i: (g_ * nr + i, 0))
    fixed = lambda shape: pl.BlockSpec(shape, lambda g_, i: (0, 0))
    wp = jnp.zeros((D_MODEL, LANES), F32).at[:, :N_EXPERTS].set(w_router)
    return pl.pallas_call(
        _out_route_kernel,
        out_shape=(jax.ShapeDtypeStruct((m, D_MODEL), F32), jax.ShapeDtypeStruct((m, D_MODEL), F32),
                   jax.ShapeDtypeStruct((m, LANES), F32), jax.ShapeDtypeStruct((8, LANES), F32)),
        grid=(groups, nr),
        in_specs=[tile(a.shape[1]), tile(y.shape[1]), tile(s.shape[1]), _const_spec(w_out.shape),
                  tile(D_MODEL), pl.BlockSpec((None, 6, D_MODEL), lambda g_, i: (row0 + g_, 0, 0)),
                  fixed((1, D_MODEL)), fixed((D_MODEL, LANES))],
        out_specs=(tile(D_MODEL), tile(D_MODEL), tile(LANES), fixed((8, LANES))),
        scratch_shapes=[pltpu.VMEM((8, LANES), F32)],
        compiler_params=_cparams("arbitrary", "arbitrary"),
        name="out_proj_route",
    )(a, y, s, w_out, x, mods, g_ffn.reshape(1, -1), wp)


MOE_TM = 1024
MOE_BLK = 128
MOE_ROWS = (MOE_TM, 768, 512, 256)


def _moe_plan(meta, counts, m):
    n_steps = -(-2 * m // MOE_TM) + N_EXPERTS
    cnt = counts[0, :N_EXPERTS].astype(jnp.int32)
    steps_e = (cnt + MOE_TM - 1) // MOE_TM
    ends = jnp.cumsum(steps_e)
    first = ends - steps_e
    total = ends[-1]
    e = meta[:, _META_E1:_META_E2 + 1].astype(jnp.int32)
    rank = meta[:, _META_R1:_META_R2 + 1].astype(jnp.int32)
    pos = (first * MOE_TM)[e] + rank
    s_idx = jnp.arange(n_steps, dtype=jnp.int32)
    step_e = jnp.minimum(jnp.searchsorted(ends, s_idx, side="right").astype(jnp.int32), N_EXPERTS - 1)
    valid = jnp.clip(cnt[step_e] - (s_idx - first[step_e]) * MOE_TM, 0, MOE_TM)
    valid = jnp.where(s_idx < total, valid, 0)
    return pos.reshape(-1), step_e, valid, n_steps


def _row_copy(src_ref, src_row, dst_ref, dst_row, sem):
    return pltpu.make_async_copy(src_ref.at[pl.ds(src_row, 1)], dst_ref.at[pl.ds(dst_row, 1)], sem)


def _invert_kernel(pos_ref, sv_ref, src_ref):
    def clear_step(s, c):
        def clear(i, c2):
            src_ref[s * MOE_TM + i] = 0
            return c2

        return lax.fori_loop(sv_ref[s], MOE_TM, clear, c)

    def place(t, c):
        p0 = pos_ref[2 * t]
        p1 = pos_ref[2 * t + 1]
        src_ref[p0] = t
        src_ref[p1] = t
        return c

    lax.fori_loop(0, sv_ref.shape[0], clear_step, 0)
    lax.fori_loop(0, pos_ref.shape[0] // 2, place, 0, unroll=8)


def _moe_invert(pos, valid, n_rows):
    smem = pl.BlockSpec(memory_space=pltpu.SMEM)
    return pl.pallas_call(
        _invert_kernel,
        out_shape=jax.ShapeDtypeStruct((n_rows,), jnp.int32),
        in_specs=[smem, smem], out_specs=smem,
        name="moe_invert",
    )(pos, valid)


def _block_wait(src_ref, dst_ref, rows, sem):
    pltpu.make_async_copy(src_ref.at[pl.ds(0, rows)], dst_ref.at[pl.ds(0, rows)], sem).wait()


def _moe_ffn_kernel(se_ref, sv_ref, src_ref, h_ref, wg_ref, wu_ref, wd_ref, o_ref, x32_ref, xb_ref, sem):
    del se_ref
    s = pl.program_id(0)
    f = pl.program_id(1)
    valid = sv_ref[s]

    def rows_used(v):
        r = jnp.where(v > 0, MOE_ROWS[-1], 0)
        for small, big in zip(MOE_ROWS[:0:-1], MOE_ROWS[-2::-1]):
            r = jnp.where(v > small, big, r)
        return r

    def blocks(v):
        return rows_used(v) // MOE_BLK

    def gather(step):
        base = step * MOE_TM

        def issue(i, c):
            _row_copy(h_ref, src_ref[base + i], x32_ref, i, sem).start()
            return c

        lax.fori_loop(0, rows_used(sv_ref[step]), issue, 0)

    @pl.when(f == 0)
    def _():
        @pl.when(s == 0)
        def _():
            gather(0)

        def land(i, c):
            _block_wait(h_ref, x32_ref, MOE_BLK, sem)
            return c

        def to_bf16(i, c):
            r0 = pl.multiple_of(i * MOE_BLK, MOE_BLK)
            xb_ref[pl.ds(r0, MOE_BLK), :] = x32_ref[pl.ds(r0, MOE_BLK), :].astype(BF16)
            return c

        lax.fori_loop(0, blocks(valid), land, 0)
        lax.fori_loop(0, blocks(valid), to_bf16, 0)

        @pl.when(s + 1 < pl.num_programs(0))
        def _():
            gather(s + 1)

        o_ref[...] = jnp.zeros_like(o_ref)

    def compute(rows):
        xs = xb_ref[0:rows, :]
        mid = _silu(_dot(xs, wg_ref[...].astype(BF16))) * _dot(xs, wu_ref[...].astype(BF16))
        o_ref[0:rows, :] += _dot(mid.astype(BF16), wd_ref[...].astype(BF16))

    for k, rows in enumerate(MOE_ROWS):
        lower = MOE_ROWS[k + 1] if k + 1 < len(MOE_ROWS) else 0

        @pl.when((valid > lower) & (valid <= rows))
        def _():
            compute(rows)


def _moe_ffn(h, src, w_gate, w_up, w_down, step_e, valid, n_steps, tf=512):
    assert all(r % MOE_BLK == 0 for r in MOE_ROWS) and MOE_ROWS[0] == MOE_TM
    n_ff = w_gate.shape[-1]
    nf = n_ff // tf
    fidx = lambda s, f, sv: jnp.where(sv[s] > 0, f, nf - 1)
    return pl.pallas_call(
        _moe_ffn_kernel,
        out_shape=jax.ShapeDtypeStruct((n_steps * MOE_TM, D_MODEL), F32),
        grid_spec=pltpu.PrefetchScalarGridSpec(
            num_scalar_prefetch=3, grid=(n_steps, nf),
            in_specs=[
                pl.BlockSpec(memory_space=pl.ANY),
                pl.BlockSpec((None, D_MODEL, tf), lambda s, f, se, sv, sr: (se[s], 0, fidx(s, f, sv))),
                pl.BlockSpec((None, D_MODEL, tf), lambda s, f, se, sv, sr: (se[s], 0, fidx(s, f, sv))),
                pl.BlockSpec((None, tf, D_MODEL), lambda s, f, se, sv, sr: (se[s], fidx(s, f, sv), 0)),
            ],
            out_specs=pl.BlockSpec((MOE_TM, D_MODEL), lambda s, f, se, sv, sr: (s, 0), pipeline_mode=pl.Buffered(1)),
            scratch_shapes=[pltpu.VMEM((MOE_TM, D_MODEL), F32), pltpu.VMEM((MOE_TM, D_MODEL), BF16),
                            pltpu.SemaphoreType.DMA(())]),
        compiler_params=_cparams("arbitrary", "arbitrary"),
        name="moe_experts",
    )(step_e, valid, src, h, w_gate, w_up, w_down)


def _combine_kernel(pos_ref, meta_ref, x_ref, m_ref, gf_ref, ys_ref, o_ref, buf_ref, sem, *, tm, final):
    base = pl.program_id(0) * tm

    def issue(i, c):
        for slot in range(2):
            _row_copy(ys_ref, pos_ref[2 * (base + i) + slot], buf_ref.at[slot], i, sem).start()
        return c

    lax.fori_loop(0, tm, issue, 0, unroll=4)
    for slot in range(2):
        _block_wait(ys_ref, buf_ref.at[slot], tm, sem)
    meta = meta_ref[...]
    y = meta[:, _META_W1:_META_W1 + 1] * buf_ref[0] + meta[:, _META_W2:_META_W2 + 1] * buf_ref[1]
    x = x_ref[...] + m_ref[5:6, :] * y
    if final:
        x = _rms(x, gf_ref[...])
    o_ref[...] = x


def _moe_combine(pos, meta, x, mods, ys, groups, row0, g_final, tm=512):
    m = x.shape[0]
    nr = m // groups // tm
    final = g_final is not None
    gf = (g_final if final else jnp.ones((D_MODEL,), F32)).reshape(1, D_MODEL)
    return pl.pallas_call(
        functools.partial(_combine_kernel, tm=tm, final=final),
        out_shape=jax.ShapeDtypeStruct((m, D_MODEL), F32),
        grid_spec=pltpu.PrefetchScalarGridSpec(
            num_scalar_prefetch=1, grid=(m // tm,),
            in_specs=[pl.BlockSpec((tm, LANES), lambda i, p: (i, 0)),
                      pl.BlockSpec((tm, D_MODEL), lambda i, p: (i, 0)),
                      pl.BlockSpec((None, 6, D_MODEL), lambda i, p: (row0 + i // nr, 0, 0)),
                      pl.BlockSpec((1, D_MODEL), lambda i, p: (0, 0)),
                      pl.BlockSpec(memory_space=pl.ANY)],
            out_specs=pl.BlockSpec((tm, D_MODEL), lambda i, p: (i, 0)),
            scratch_shapes=[pltpu.VMEM((2, tm, D_MODEL), F32), pltpu.SemaphoreType.DMA(())]),
        compiler_params=_cparams("arbitrary"),
        name="moe_combine",
    )(pos, meta, x, mods, gf, ys)


def _moe(h, meta, counts, x, mods, groups, row0, w_gate, w_up, w_down, g_final):
    m = h.shape[0]
    pos, step_e, valid, n_steps = _moe_plan(meta, counts, m)
    src = _moe_invert(pos, valid, n_steps * MOE_TM)
    ys = _moe_ffn(h, src, w_gate, w_up, w_down, step_e, valid, n_steps)
    return _moe_combine(pos, meta, x, mods, ys, groups, row0, g_final)


def _ffn_kernel(h_ref, wg_ref, wu_ref, wd_ref, x_ref, m_ref, gn_ref, mn_ref, o_ref, *maybe_hn_ref):
    f = pl.program_id(2)

    @pl.when(f == 0)
    def _():
        o_ref[...] = jnp.zeros_like(o_ref)

    h = h_ref[...]
    mid = _silu(_dot(h, wg_ref[...])) * _dot(h, wu_ref[...])
    o_ref[...] += _dot(mid.astype(BF16), wd_ref[...])

    @pl.when(f == pl.num_programs(2) - 1)
    def _():
        x = x_ref[...] + m_ref[5:6, :] * o_ref[...]
        o_ref[...] = x
        for hn_ref in maybe_hn_ref:
            hn_ref[...] = (_rms(x, gn_ref[...]) * (1.0 + mn_ref[1:2, :]) + mn_ref[0:1, :]).astype(hn_ref.dtype)


def _ffn(h, w_gate, w_up, w_down, x, mods, groups, row0, nxt, tm, tf=512):
    m = x.shape[0]
    n_ff = w_gate.shape[1]
    nr = m // groups // tm
    g_next, mods_next = nxt if nxt is not None else (jnp.ones((D_MODEL,), F32), mods)
    once = pl.Buffered(1)
    tile = lambda n, mode=None: pl.BlockSpec((tm, n), lambda g_, i, f: (g_ * nr + i, 0), pipeline_mode=mode)
    mod_spec = pl.BlockSpec((None, 6, D_MODEL), lambda g_, i, f: (row0 + g_, 0, 0))
    out_shape = [jax.ShapeDtypeStruct((m, D_MODEL), F32)]
    out_specs = [tile(D_MODEL, once)]
    if nxt is not None:
        out_shape.append(jax.ShapeDtypeStruct((m, D_MODEL), BF16))
        out_specs.append(tile(D_MODEL, once))
    outs = pl.pallas_call(
        _ffn_kernel,
        out_shape=tuple(out_shape),
        grid=(groups, nr, n_ff // tf),
        in_specs=[tile(D_MODEL),
                  pl.BlockSpec((D_MODEL, tf), lambda g_, i, f: (0, f)),
                  pl.BlockSpec((D_MODEL, tf), lambda g_, i, f: (0, f)),
                  pl.BlockSpec((tf, D_MODEL), lambda g_, i, f: (f, 0)),
                  tile(D_MODEL, once), mod_spec,
                  pl.BlockSpec((1, D_MODEL), lambda g_, i, f: (0, 0)), mod_spec],
        out_specs=tuple(out_specs),
        compiler_params=_cparams("parallel", "parallel", "arbitrary"),
        name="swiglu_ffn",
    )(h, w_gate, w_up, w_down, x, mods, g_next.reshape(1, D_MODEL), mods_next)
    return outs if nxt is not None else (outs[0], None)


def _swap_halves(w):
    half = w.shape[-1] // 2
    return jnp.concatenate([w[..., half:], w[..., :half]], axis=-1)


def _swap_heads(w, heads, dim):
    k = w.shape[0]
    return _swap_halves(w.reshape(k, heads, dim)).reshape(k, heads * dim)


def _rope_tables():
    rows = SEQ // GRID_W
    row = np.repeat(np.arange(rows), GRID_W).astype(np.float32)
    col = np.tile(np.arange(GRID_W), rows).astype(np.float32)
    quarter = MLA_ROPE // 4
    freqs = (np.float32(ROPE_THETA) ** (-np.arange(quarter, dtype=np.float32) / quarter)).astype(np.float32)
    ang = np.concatenate([row[:, None] * freqs[None], col[:, None] * freqs[None]], axis=-1)
    cos = np.cos(ang.astype(np.float64))
    sin = np.sin(ang.astype(np.float64))
    cos2 = np.concatenate([cos, cos], axis=-1)
    sin2 = np.concatenate([-sin, sin], axis=-1)
    tab_lat = np.concatenate([cos2, sin2], axis=-1)
    tab_ctx = np.concatenate([np.ones((CTX_LEN, MLA_ROPE)), np.zeros((CTX_LEN, MLA_ROPE))], axis=-1)
    return dict(
        tab_lat=jnp.asarray(tab_lat, F32), tab_ctx=jnp.asarray(tab_ctx, F32),
        cos8=jnp.asarray(np.tile(cos2, (1, SWA_HEADS)), F32),
        sin8=jnp.asarray(np.tile(sin2, (1, SWA_HEADS)), F32),
        ones8=jnp.ones((CTX_LEN, _SWA_Q), F32), zeros8=jnp.zeros((CTX_LEN, _SWA_Q), F32),
    )


def _layer_weights(p, l):
    w_in = p["w_in"][l]
    cq = w_in[:, _O_CQ:_O_CKV]
    ckv = w_in[:, _O_CKV:_O_KPE]
    kpe = w_in[:, _O_KPE:_O_HY]
    hy = w_in[:, _O_HY:_O_SQ]
    sq = w_in[:, _O_SQ:_O_SK]
    sk = w_in[:, _O_SK:_O_SV]
    sv = w_in[:, _O_SV:]
    dq = MLA_NOPE + MLA_ROPE
    wq = p["w_q_up"][l].reshape(MLA_Q_LORA, MLA_HEADS, dq)
    wq = jnp.concatenate([wq, _swap_halves(wq[..., MLA_NOPE:])], axis=-1)
    wkv = p["w_kv_up"][l].reshape(MLA_KV_LORA, MLA_HEADS, MLA_NOPE + MLA_V)
    return dict(
        w_cq=cq.astype(BF16),
        w_ckv=jnp.concatenate([ckv, kpe, _swap_halves(kpe)], axis=-1).astype(BF16),
        w_hy=hy.astype(BF16),
        w_swa=jnp.concatenate([sq, _swap_heads(sq, SWA_HEADS, SWA_HEAD_DIM), sk,
                               _swap_heads(sk, SWA_KV_HEADS, SWA_HEAD_DIM), sv], axis=-1).astype(BF16),
        w_q=jnp.transpose(wq, (1, 0, 2)).astype(BF16),
        w_kv=jnp.transpose(wkv, (1, 0, 2)).astype(BF16),
        w_out=p["w_out"][l].astype(BF16),
        g_q=p["g_q"][l], g_kv=p["g_kv"][l],
        hy_conv_w=p["hy_conv_w"][l], hy_conv_b=p["hy_conv_b"][l],
        hy_w1=p["hy_w1"][l], hy_b1=p["hy_b1"][l], hy_w2=p["hy_w2"][l], hy_b2=p["hy_b2"][l],
        hy_w3=p["hy_w3"][l], hy_b3=p["hy_b3"][l], hy_w_filt=p["hy_w_filt"][l],
        hy_freq=p["hy_freq"][l], hy_skip=p["hy_skip"][l], swa_sink=p["swa_sink"][l],
    )


def _hyena(hy, lw, n_len, tabs):
    x0, vx, vxb = _hy_pre(hy, lw["hy_conv_w"], lw["hy_conv_b"], n_len)
    spectra = _hy_filters(lw, n_len, tabs)
    y = _hy_long_conv(x0, vx, vxb, spectra, lw["hy_skip"], n_len, tabs)
    return y.reshape(BATCH * n_len, HY_WIDTH)


def _mixer(h_lat, h_ctx, lw, rope, hy_tabs, need_ctx):
    q_l, k_l, v_l = _mla_proj(h_lat, lw, rope["tab_lat"])
    if need_ctx:
        q_c, k_c, v_c = _mla_proj(h_ctx, lw, rope["tab_ctx"], tm=CTX_LEN)
    else:
        k_c, v_c = _kv_proj(h_ctx, lw["w_ckv"], lw["g_kv"], lw["w_kv"], rope["tab_ctx"], tm=CTX_LEN)
    a_l = _mla_attention(q_l, [(k_l, v_l, SEQ), (k_c, v_c, CTX_LEN)], SEQ)
    hy_l, sq_l, kk_l, vv_l = _swa_proj(h_lat, lw["w_swa"], rope["cos8"], rope["sin8"], True, w_hy=lw["w_hy"])
    y_l = _hyena(hy_l, lw, SEQ, hy_tabs[SEQ])
    if need_ctx:
        hy_c, sq_c, kk_c, vv_c = _swa_proj(h_ctx, lw["w_swa"], rope["ones8"], rope["zeros8"], False,
                                           w_hy=lw["w_hy"], tm=CTX_LEN)
    else:
        sq_c, kk_c, vv_c = _swa_proj(h_ctx, lw["w_swa"], rope["ones8"], rope["zeros8"], False, tm=CTX_LEN)
    s_l = _swa_attention(lw["swa_sink"], sq_l, (kk_l, vv_l), (kk_c, vv_c), SEQ)
    if not need_ctx:
        return (a_l, y_l, s_l), None
    a_c = _mla_attention(q_c, [(k_c, v_c, CTX_LEN)], CTX_LEN, tq=CTX_LEN)
    y_c = _hyena(hy_c, lw, CTX_LEN, hy_tabs[CTX_LEN])
    s_c = _swa_attention(lw["swa_sink"], sq_c, None, (kk_c, vv_c), CTX_LEN, tq=CTX_LEN)
    return (a_l, y_l, s_l), (a_c, y_c, s_c)


def _forward(p):
    rope = _rope_tables()
    hy_tabs = {n: dict(dft=_dft_tables(n), filt=_filter_tables(n)) for n in (SEQ, CTX_LEN)}
    cvec = jnp.zeros((MOD_ROWS, D_MODEL), F32).at[:BATCH].set(p["c"]).at[BATCH].set(p["c_ctx"])
    mods_all = _modulation(cvec, p["w_mod"], p["b_mod"]).reshape(DEPTH, MOD_ROWS, 6, D_MODEL)

    x_lat = p["x"].reshape(BATCH * SEQ, D_MODEL)
    x_ctx = p["ctx"].reshape(BATCH * CTX_LEN, D_MODEL)
    h_in = [None, None]
    for l in range(DEPTH):
        last = l == DEPTH - 1
        mods = mods_all[l]
        lw = _layer_weights(p, l)
        h_lat = h_in[0] if h_in[0] is not None else _norm_mod(x_lat, p["g_mix"][l], mods, BATCH, 0, 0, BF16)
        h_ctx = h_in[1] if h_in[1] is not None else _norm_mod(x_ctx, p["g_mix"][l], mods, 1, BATCH, 0, BF16)
        mix_l, mix_c = _mixer(h_lat, h_ctx, lw, rope, hy_tabs, not last)
        streams = [(x_lat, mix_l, BATCH, 0, 1024)]
        if not last:
            streams.append((x_ctx, mix_c, 1, BATCH, 512))
        outs = []
        h_in = [None, None]
        i = l // 2
        dense = l % 2 == 0
        for n, (x, mix, groups, row0, ffn_tm) in enumerate(streams):
            if dense:
                x, h2 = _out_proj(*mix, lw["w_out"], x, mods, p["g_ffn"][l], groups, row0, BF16)
                nxt = None if last else (p["g_mix"][l + 1], mods_all[l + 1])
                x, h_in[n] = _ffn(h2, p["ffn_w_gate"][i].astype(BF16), p["ffn_w_up"][i].astype(BF16),
                                  p["ffn_w_down"][i].astype(BF16), x, mods, groups, row0, nxt, ffn_tm)
            else:
                x, h2, meta, counts = _out_proj_route(*mix, lw["w_out"], x, mods, p["g_ffn"][l], groups, row0,
                                                      p["moe_router"][i])
                g_final = p["g_final"] if (last and n == 0) else None
                x = _moe(h2, meta, counts, x, mods, groups, row0, p["moe_w_gate"][i], p["moe_w_up"][i],
                         p["moe_w_down"][i], g_final)
            outs.append(x)
        x_lat = outs[0]
        if not last:
            x_ctx = outs[1]
    if DEPTH % 2 == 1:
        x_lat = _norm_mod(x_lat, p["g_final"], mods_all[0], BATCH, 0, None, F32)
    return x_lat.reshape(BATCH, SEQ, D_MODEL)


def kernel(x, c, ctx, c_ctx, w_mod, b_mod, g_mix, g_ffn, w_in, g_q, w_q_up, g_kv, w_kv_up, hy_conv_w, hy_conv_b, hy_w1, hy_b1, hy_w2, hy_b2, hy_w3, hy_b3, hy_w_filt, hy_freq, hy_skip, swa_sink, w_out, ffn_w_gate, ffn_w_up, ffn_w_down, moe_router, moe_w_gate, moe_w_up, moe_w_down, g_final):
    return _forward(dict(
        x=x, c=c, ctx=ctx, c_ctx=c_ctx, w_mod=w_mod, b_mod=b_mod, g_mix=g_mix, g_ffn=g_ffn, w_in=w_in,
        g_q=g_q, w_q_up=w_q_up, g_kv=g_kv, w_kv_up=w_kv_up, hy_conv_w=hy_conv_w, hy_conv_b=hy_conv_b,
        hy_w1=hy_w1, hy_b1=hy_b1, hy_w2=hy_w2, hy_b2=hy_b2, hy_w3=hy_w3, hy_b3=hy_b3,
        hy_w_filt=hy_w_filt, hy_freq=hy_freq, hy_skip=hy_skip, swa_sink=swa_sink, w_out=w_out,
        ffn_w_gate=ffn_w_gate, ffn_w_up=ffn_w_up, ffn_w_down=ffn_w_down, moe_router=moe_router,
        moe_w_gate=moe_w_gate, moe_w_up=moe_w_up, moe_w_down=moe_w_down, g_final=g_final))
```

```python
import functools
import math

import numpy as np
import jax
import jax.numpy as jnp
from jax import lax
from jax.experimental import pallas as pl
from jax.experimental.pallas import tpu as pltpu

F32 = jnp.float32
BF16 = jnp.bfloat16

D_MODEL = 2048
BATCH = 4
SEQ = 2048
DEPTH = 2
CTX_LEN = 256
GRID_W = 64
NORM_EPS = 1e-6
ROPE_THETA = 10000.0
MLA_HEADS = 8
MLA_NOPE = 128
MLA_ROPE = 64
MLA_V = 128
MLA_Q_LORA = 768
MLA_KV_LORA = 512
HY_WIDTH = 512
HY_BANDS = 16
HY_EMB = 1 + 2 * HY_BANDS
HY_FILTER_HIDDEN = 64
HY_DECAY_TARGET = 1e-2
HY_QUICK_DECAY_PCT = 0.3
HY_SLOW_DECAY_PCT = 1.5
SWA_HEADS = 8
SWA_KV_HEADS = 2
SWA_HEAD_DIM = 64
SWA_WINDOW = 128
N_EXPERTS = 8
D_FF = 5632
D_FF_EXPERT = 7168

LANES = 128
MXU_DIM = 256
VMEM_LIMIT_BYTES = 56 * 1024 * 1024
NEG_BIG = -1e30
MOD_ROWS = 8

_O_CQ = 0
_O_CKV = _O_CQ + MLA_Q_LORA
_O_KPE = _O_CKV + MLA_KV_LORA
_O_HY = _O_KPE + MLA_ROPE
_O_SQ = _O_HY + 3 * HY_WIDTH
_O_SK = _O_SQ + SWA_HEADS * SWA_HEAD_DIM
_O_SV = _O_SK + SWA_KV_HEADS * SWA_HEAD_DIM


def _cparams(*sem):
    return pltpu.CompilerParams(dimension_semantics=sem, vmem_limit_bytes=VMEM_LIMIT_BYTES)


def _dot(a, b):
    return jnp.dot(a, b, preferred_element_type=F32)


def _dot_nt(a, b):
    return lax.dot_general(a, b, (((1,), (1,)), ((), ())), preferred_element_type=F32)


def _split_bf16(a):
    hi = a.astype(BF16)
    lo = (a - hi.astype(F32)).astype(BF16)
    return hi, lo


def _dot3(a, b):
    ah, al = _split_bf16(a)
    bh, bl = _split_bf16(b)
    return _dot(ah, bh) + (_dot(al, bh) + _dot(ah, bl))


def _silu(x):
    return x / (1.0 + jnp.exp(-x))


def _rms(x, g):
    ms = jnp.mean(x * x, axis=-1, keepdims=True)
    return x * lax.rsqrt(ms + NORM_EPS) * g


def _const_spec(shape):
    nd = len(shape)
    return pl.BlockSpec(shape, lambda *_: (0,) * nd, pipeline_mode=pl.Buffered(1))


def _mod_kernel(c_ref, w_ref, b_ref, o_ref):
    ah, al = _split_bf16(_silu(c_ref[...]))
    wh, wl = _split_bf16(w_ref[0])
    r = _dot(jnp.concatenate([ah, al], axis=0), wh)
    o_ref[0] = r[:MOD_ROWS] + (r[MOD_ROWS:] + _dot(ah, wl)) + b_ref[0]


def _modulation(cvec, w_mod, b_mod):
    tn = 1024
    n = 6 * D_MODEL
    return pl.pallas_call(
        _mod_kernel,
        out_shape=jax.ShapeDtypeStruct((DEPTH, MOD_ROWS, n), F32),
        grid=(DEPTH, n // tn),
        in_specs=[
            pl.BlockSpec((MOD_ROWS, D_MODEL), lambda l, j: (0, 0)),
            pl.BlockSpec((1, D_MODEL, tn), lambda l, j: (l, 0, j)),
            pl.BlockSpec((1, 1, tn), lambda l, j: (l, 0, j)),
        ],
        out_specs=pl.BlockSpec((1, MOD_ROWS, tn), lambda l, j: (l, 0, j)),
        compiler_params=_cparams("arbitrary", "arbitrary"),
        name="adaln_mod",
    )(cvec, w_mod, b_mod.reshape(DEPTH, 1, n))


def _norm_kernel(x_ref, g_ref, m_ref, o_ref, *, si):
    y = _rms(x_ref[...], g_ref[...])
    if si is not None:
        y = y * (1.0 + m_ref[si + 1:si + 2, :]) + m_ref[si:si + 1, :]
    o_ref[...] = y.astype(o_ref.dtype)


def _norm_mod(x, g, mods, groups, row0, si, out_dtype, tm=512):
    m = x.shape[0]
    nr = m // groups // tm
    return pl.pallas_call(
        functools.partial(_norm_kernel, si=si),
        out_shape=jax.ShapeDtypeStruct((m, D_MODEL), out_dtype),
        grid=(groups, nr),
        in_specs=[
            pl.BlockSpec((tm, D_MODEL), lambda g_, i: (g_ * nr + i, 0)),
            pl.BlockSpec((1, D_MODEL), lambda g_, i: (0, 0)),
            pl.BlockSpec((None, 6, D_MODEL), lambda g_, i: (row0 + g_, 0, 0)),
        ],
        out_specs=pl.BlockSpec((tm, D_MODEL), lambda g_, i: (g_ * nr + i, 0)),
        compiler_params=_cparams("parallel", "parallel"),
        name="norm_mod",
    )(x, g.reshape(1, D_MODEL), mods)


def _rope_halves(pe_pair, tab):
    r = pe_pair * tab
    return r + pltpu.roll(r, MLA_ROPE, axis=1)


MLA_Q_SCALE = (MLA_NOPE + MLA_ROPE) ** -0.5 * math.log2(math.e)


def _qproj_body(h_ref, wc_ref, g_ref, wq_ref, tab_ref, o_ref):
    cq = _dot(h_ref[...], wc_ref[...])
    cqn = _rms(cq, g_ref[...]).astype(BF16)
    tab = tab_ref[...] * MLA_Q_SCALE
    for hh in range(MLA_HEADS):
        r = _dot(cqn, wq_ref[hh])
        o_ref[hh, :, 0:MLA_NOPE] = (r[:, :MLA_NOPE] * MLA_Q_SCALE).astype(BF16)
        o_ref[hh, :, MLA_NOPE:] = _rope_halves(r[:, MLA_NOPE:], tab).astype(BF16)


def _mla_proj_kernel(h_ref, wcq_ref, gq_ref, wq_ref, wckv_ref, gkv_ref, wkv_ref, tab_ref, q_ref, k_ref, v_ref):
    _qproj_body(h_ref, wcq_ref, gq_ref, wq_ref, tab_ref, q_ref)
    _kvproj_kernel(h_ref, wckv_ref, gkv_ref, wkv_ref, tab_ref, k_ref, v_ref)


def _mla_proj(h, lw, tab, tm=512):
    m = h.shape[0]
    nt = tab.shape[0] // tm
    head_out = jax.ShapeDtypeStruct((MLA_HEADS, m, MXU_DIM), BF16)
    head_spec = pl.BlockSpec((MLA_HEADS, tm, MXU_DIM), lambda i: (0, i, 0))
    return pl.pallas_call(
        _mla_proj_kernel,
        out_shape=(head_out, head_out, head_out),
        grid=(m // tm,),
        in_specs=[
            pl.BlockSpec((tm, D_MODEL), lambda i: (i, 0)),
            _const_spec(lw["w_cq"].shape), _const_spec((1, MLA_Q_LORA)), _const_spec(lw["w_q"].shape),
            _const_spec(lw["w_ckv"].shape), _const_spec((1, MLA_KV_LORA)), _const_spec(lw["w_kv"].shape),
            pl.BlockSpec((tm, LANES), lambda i: (i % nt, 0)),
        ],
        out_specs=(head_spec, head_spec, head_spec),
        compiler_params=_cparams("parallel"),
        name="mla_proj",
    )(h, lw["w_cq"], lw["g_q"].reshape(1, -1), lw["w_q"], lw["w_ckv"], lw["g_kv"].reshape(1, -1), lw["w_kv"], tab)


def _kvproj_kernel(h_ref, wc_ref, g_ref, wkv_ref, tab_ref, k_ref, v_ref):
    t = _dot(h_ref[...], wc_ref[...])
    ckvn = _rms(t[:, :MLA_KV_LORA], g_ref[...]).astype(BF16)
    rot = _rope_halves(t[:, MLA_KV_LORA:], tab_ref[...])
    lane = lax.broadcasted_iota(jnp.int32, rot.shape, 1)
    kpe = jnp.where(lane < MLA_ROPE, rot, 0.0).astype(BF16)
    for hh in range(MLA_HEADS):
        r = _dot(ckvn, wkv_ref[hh])
        k_ref[hh, :, 0:MLA_NOPE] = r[:, :MLA_NOPE].astype(BF16)
        k_ref[hh, :, MLA_NOPE:] = kpe
        v_ref[hh, :, 0:MLA_V] = r[:, MLA_NOPE:].astype(BF16)
        v_ref[hh, :, MLA_V:] = jnp.ones((r.shape[0], MXU_DIM - MLA_V), BF16)


def _kv_proj(h, w_ckv, g_kv, w_kv, tab, tm=512):
    m = h.shape[0]
    nt = tab.shape[0] // tm
    return pl.pallas_call(
        _kvproj_kernel,
        out_shape=(jax.ShapeDtypeStruct((MLA_HEADS, m, MXU_DIM), BF16),
                   jax.ShapeDtypeStruct((MLA_HEADS, m, MXU_DIM), BF16)),
        grid=(m // tm,),
        in_specs=[
            pl.BlockSpec((tm, D_MODEL), lambda i: (i, 0)),
            _const_spec(w_ckv.shape),
            _const_spec((1, MLA_KV_LORA)),
            _const_spec(w_kv.shape),
            pl.BlockSpec((tm, LANES), lambda i: (i % nt, 0)),
        ],
        out_specs=(pl.BlockSpec((MLA_HEADS, tm, MXU_DIM), lambda i: (0, i, 0)),
                   pl.BlockSpec((MLA_HEADS, tm, MXU_DIM), lambda i: (0, i, 0))),
        compiler_params=_cparams("parallel"),
        name="mla_kv_proj",
    )(h, w_ckv, g_kv.reshape(1, -1), w_kv, tab)


MLA_KEY_CHUNK = 512


def _mla_kernel(q_ref, *refs, seg_lens):
    nseg = len(seg_lens)
    o_ref = refs[2 * nseg]
    q = q_ref[...]
    m = jnp.full((q.shape[0], 1), NEG_BIG, F32)
    acc = jnp.zeros((q.shape[0], MXU_DIM), F32)
    for s, lk in enumerate(seg_lens):
        k_ref, v_ref = refs[2 * s], refs[2 * s + 1]
        step = min(MLA_KEY_CHUNK, lk)
        for c0 in range(0, lk, step):
            sc = _dot_nt(q, k_ref[c0:c0 + step, :])
            m_new = jnp.maximum(m, sc.max(axis=-1, keepdims=True))
            p = jnp.exp2(sc - m_new).astype(BF16)
            acc = acc * jnp.exp2(m - m_new) + _dot(p, v_ref[c0:c0 + step, :])
            m = m_new
    o_ref[...] = (acc[:, :MLA_V] / acc[:, MLA_V:MLA_V + 1]).astype(o_ref.dtype)


def _mla_attention(q, segs, lq, tq=512):
    nq = lq // tq
    in_specs = [pl.BlockSpec((None, tq, MXU_DIM), lambda b, h, i: (h, b * nq + i, 0))]
    args = [q]
    for k, v, lk in segs:
        in_specs.append(pl.BlockSpec((None, lk, MXU_DIM), lambda b, h, i: (h, b, 0)))
        in_specs.append(pl.BlockSpec((None, lk, MXU_DIM), lambda b, h, i: (h, b, 0)))
        args += [k, v]
    return pl.pallas_call(
        functools.partial(_mla_kernel, seg_lens=tuple(lk for _, _, lk in segs)),
        out_shape=jax.ShapeDtypeStruct((BATCH * lq, MLA_HEADS * MLA_V), BF16),
        grid=(BATCH, MLA_HEADS, nq),
        in_specs=in_specs,
        out_specs=pl.BlockSpec((tq, MLA_V), lambda b, h, i: (b * nq + i, h)),
        compiler_params=_cparams("parallel", "parallel", "arbitrary"),
        name="mla_attention",
    )(*args)


def _dft_tables(n_len):
    n2 = 2 * n_len
    idx = np.arange(n_len, dtype=np.int64)
    ang = (2.0 * np.pi / n2) * ((idx[:, None] * idx[None, :]) % n2).astype(np.float64)
    cm = np.cos(ang)
    sf = np.sin(ang)
    sf[0, :] = 1.0 - 2.0 * (idx % 2)
    return (jnp.asarray(cm, dtype=BF16), jnp.asarray(sf, dtype=BF16),
            jnp.asarray(sf.T.copy(), dtype=BF16))


def _filter_tables(n_len):
    pos = np.arange(n_len, dtype=np.float64)
    t = pos / max(n_len - 1, 1)
    bands = np.linspace(1e-4, HY_BANDS - 1, HY_BANDS)
    ang = (2.0 * math.pi / n_len) * pos[:, None] * bands[None]
    z = np.zeros((n_len, LANES), np.float64)
    z[:, 0] = t
    z[:, 1:1 + HY_BANDS] = np.cos(ang)
    z[:, 1 + HY_BANDS:HY_EMB] = -np.sin(ang)
    deltas = np.linspace(math.log(HY_DECAY_TARGET) / HY_SLOW_DECAY_PCT,
                         math.log(HY_DECAY_TARGET) / HY_QUICK_DECAY_PCT, HY_WIDTH)
    decay = np.exp(-t[:, None] * np.abs(deltas)[None])
    return jnp.asarray(z, dtype=F32), jnp.asarray(decay, dtype=F32)


def _hy_filter_kernel(z_ref, w1_ref, b1_ref, w2_ref, b2_ref, w3_ref, b3_ref, wf_ref, wb_ref,
                      fr_ref, dec_ref, cm_ref, sf_ref, a_ref, b_ref, d_ref, hid_ref, *, n_len):
    @pl.when(pl.program_id(0) == 0)
    def _():
        fr = fr_ref[...]
        h = jnp.sin(fr * (_dot3(z_ref[...], w1_ref[...]) + b1_ref[...]))
        h = jnp.sin(fr * (_dot3(h, w2_ref[...]) + b2_ref[...]))
        hid_ref[...] = jnp.sin(fr * (_dot3(h, w3_ref[...]) + b3_ref[...]))

    h = hid_ref[...]
    dec = dec_ref[...]
    h_f = _dot3(h, wf_ref[...]) * dec
    h_b = _dot3(h, wb_ref[...]) * dec
    row = lax.broadcasted_iota(jnp.int32, h_f.shape, 0)
    h_b = jnp.where(row == 0, 0.0, h_b)
    hs = h_f + h_b
    hd = h_f - h_b
    nc = hs.shape[1]
    hs2 = jnp.concatenate(_split_bf16(hs), axis=1)
    hd2 = jnp.concatenate(_split_bf16(hd), axis=1)
    sf = sf_ref[...]
    t2 = _dot(cm_ref[...], hs2)
    g2 = _dot(sf, hd2)
    n2 = _dot(sf[0:16, :], hs2)[0:1, :]
    t_re = t2[:, :nc] + t2[:, nc:]
    g_t = g2[:, :nc] + g2[:, nc:]
    t_ny = n2[:, :nc] + n2[:, nc:]
    inv_n = 1.0 / (2 * n_len)
    first = row == 0
    a_ref[...] = jnp.where(first, inv_n, 2.0 * inv_n) * t_re
    b_ref[...] = jnp.where(first, 0.0, -2.0 * inv_n * g_t)
    d_ref[...] = jnp.where(first, inv_n * t_ny, 2.0 * inv_n * t_re)


def _hy_filters(lp, n_len, tabs):
    z, decay = tabs["filt"]
    cm, sf, _ = tabs["dft"]
    cb = LANES
    nblk = HY_WIDTH // cb
    w1 = jnp.zeros((LANES, HY_FILTER_HIDDEN), F32).at[:HY_EMB].set(lp["hy_w1"])
    row = lambda a: a.reshape(1, -1)
    small = lambda shape: pl.BlockSpec(shape, lambda j: (0,) * len(shape))
    fh = HY_FILTER_HIDDEN
    out = jax.ShapeDtypeStruct((n_len, HY_WIDTH), F32)
    return pl.pallas_call(
        functools.partial(_hy_filter_kernel, n_len=n_len),
        out_shape=(out, out, out),
        grid=(nblk,),
        in_specs=[
            small((n_len, LANES)), small((LANES, fh)), small((1, fh)), small((fh, fh)), small((1, fh)),
            small((fh, fh)), small((1, fh)),
            pl.BlockSpec((fh, cb), lambda j: (0, j)),
            pl.BlockSpec((fh, cb), lambda j: (0, nblk + j)),
            small((1, fh)),
            pl.BlockSpec((n_len, cb), lambda j: (0, j)),
            _const_spec((n_len, n_len)), _const_spec((n_len, n_len)),
        ],
        out_specs=tuple(pl.BlockSpec((n_len, cb), lambda j: (0, j)) for _ in range(3)),
        scratch_shapes=[pltpu.VMEM((n_len, fh), F32)],
        compiler_params=_cparams("arbitrary"),
        name="hyena_filters",
    )(z, w1, row(lp["hy_b1"]), lp["hy_w2"], row(lp["hy_b2"]), lp["hy_w3"], row(lp["hy_b3"]),
      lp["hy_w_filt"], lp["hy_w_filt"], row(lp["hy_freq"]), decay, cm, sf)


def _hy_pre_kernel(u0_ref, u1_ref, u2_ref, w0_ref, w1_ref, w2_ref, b0_ref, b1_ref, b2_ref,
                   x0_ref, vx_ref, vxb_ref, *, n_len):
    def sconv(u_ref, w_ref, b_ref):
        u = u_ref[...]
        w = w_ref[...]
        row = lax.broadcasted_iota(jnp.int32, u.shape, 0)
        prev = jnp.where(row == 0, 0.0, pltpu.roll(u, 1, axis=0))
        nxt = jnp.where(row == n_len - 1, 0.0, pltpu.roll(u, n_len - 1, axis=0))
        return w[0:1] * prev + w[1:2] * u + w[2:3] * nxt + b_ref[...]

    x0_ref[...] = sconv(u0_ref, w0_ref, b0_ref)
    vx = sconv(u2_ref, w2_ref, b2_ref) * sconv(u1_ref, w1_ref, b1_ref)
    vx_ref[...] = vx
    vxb_ref[...] = vx.astype(BF16)


def _hy_pre(hy, conv_w, conv_b, n_len):
    nblk = HY_WIDTH // LANES
    uspec = lambda part: pl.BlockSpec((None, n_len, LANES), lambda b, j: (b, 0, part * nblk + j))
    wspec = lambda part: pl.BlockSpec((3, LANES), lambda b, j: (0, part * nblk + j))
    bspec = lambda part: pl.BlockSpec((1, LANES), lambda b, j: (0, part * nblk + j))
    ospec = pl.BlockSpec((None, n_len, LANES), lambda b, j: (b, 0, j))
    hy3 = hy.reshape(BATCH, n_len, 3 * HY_WIDTH)
    cb = conv_b.reshape(1, -1)
    return pl.pallas_call(
        functools.partial(_hy_pre_kernel, n_len=n_len),
        out_shape=(jax.ShapeDtypeStruct((BATCH, n_len, HY_WIDTH), F32),
                   jax.ShapeDtypeStruct((BATCH, n_len, HY_WIDTH), F32),
                   jax.ShapeDtypeStruct((BATCH, n_len, HY_WIDTH), BF16)),
        grid=(BATCH, nblk),
        in_specs=[uspec(0), uspec(1), uspec(2), wspec(0), wspec(1), wspec(2), bspec(0), bspec(1), bspec(2)],
        out_specs=(ospec, ospec, ospec),
        compiler_params=_cparams("parallel", "parallel"),
        name="hyena_short_conv",
    )(hy3, hy3, hy3, conv_w, conv_w, conv_w, cb, cb, cb)


def _hy_fwd_kernel(cm_ref, sf_ref, u_ref, a_ref, b_ref, d_ref, q_ref):
    u = u_ref[...]
    ur = _dot(cm_ref[...], u)
    g = _dot(sf_ref[...], u)
    b = b_ref[...]
    q_ref[0] = (a_ref[...] * ur + b * g).astype(BF16)
    q_ref[1] = (d_ref[...] * g - b * ur).astype(BF16)


def _hy_inv_kernel(cm_ref, si_ref, q_ref, x0_ref, vx_ref, skip_ref, o_ref):
    y = _dot(cm_ref[...], q_ref[0]) + _dot(si_ref[...], q_ref[1])
    o_ref[...] = (x0_ref[...] * (y + vx_ref[...] * skip_ref[...])).astype(o_ref.dtype)


def _hy_long_conv(x0, vx, vxb, spectra, skip, n_len, tabs):
    cm, sf, si = tabs["dft"]
    a, b, d = spectra
    tf = min(512, n_len)
    nf = n_len // tf
    mat = pl.BlockSpec((tf, n_len), lambda bb, f: (f, 0))
    spec = pl.BlockSpec((tf, HY_WIDTH), lambda bb, f: (f, 0))
    q = pl.pallas_call(
        _hy_fwd_kernel,
        out_shape=jax.ShapeDtypeStruct((BATCH, 2, n_len, HY_WIDTH), BF16),
        grid=(BATCH, nf),
        in_specs=[mat, mat, pl.BlockSpec((None, n_len, HY_WIDTH), lambda bb, f: (bb, 0, 0)), spec, spec, spec],
        out_specs=pl.BlockSpec((None, 2, tf, HY_WIDTH), lambda bb, f: (bb, 0, f, 0)),
        compiler_params=_cparams("parallel", "arbitrary"),
        name="hyena_dft_fwd",
    )(cm, sf, vxb, a, b, d)
    tile = pl.BlockSpec((None, tf, HY_WIDTH), lambda bb, f: (bb, f, 0))
    return pl.pallas_call(
        _hy_inv_kernel,
        out_shape=jax.ShapeDtypeStruct((BATCH, n_len, HY_WIDTH), BF16),
        grid=(BATCH, nf),
        in_specs=[mat, mat, pl.BlockSpec((None, 2, n_len, HY_WIDTH), lambda bb, f: (bb, 0, 0, 0)),
                  tile, tile, pl.BlockSpec((1, HY_WIDTH), lambda bb, f: (0, 0))],
        out_specs=tile,
        compiler_params=_cparams("parallel", "arbitrary"),
        name="hyena_dft_inv",
    )(cm, si, q, x0, vx, skip.reshape(1, -1))


_SWA_Q = SWA_HEADS * SWA_HEAD_DIM
_SWA_KV = SWA_KV_HEADS * SWA_HEAD_DIM


def _swaproj_kernel(h_ref, w_ref, cos_ref, sin_ref, q_ref, k_ref, v_ref, *, rope):
    t = _dot(h_ref[...], w_ref[...])
    q = t[:, :_SWA_Q]
    k = t[:, 2 * _SWA_Q:2 * _SWA_Q + _SWA_KV]
    v = t[:, 2 * _SWA_Q + 2 * _SWA_KV:]
    if rope:
        cos = cos_ref[...]
        sin = sin_ref[...]
        q = q * cos + t[:, _SWA_Q:2 * _SWA_Q] * sin
        k = k * cos[:, :_SWA_KV] + t[:, 2 * _SWA_Q + _SWA_KV:2 * _SWA_Q + 2 * _SWA_KV] * sin[:, :_SWA_KV]
    q_ref[...] = (q * (SWA_HEAD_DIM ** -0.5 * math.log2(math.e))).astype(BF16)
    lo = lax.broadcasted_iota(jnp.int32, k.shape, 1) < SWA_HEAD_DIM
    for src, dst in ((k, k_ref), (v, v_ref)):
        sw = pltpu.roll(src, SWA_HEAD_DIM, axis=1)
        dst[:, 0 * LANES:1 * LANES] = jnp.where(lo, src, 0.0).astype(BF16)
        dst[:, 1 * LANES:2 * LANES] = jnp.where(lo, 0.0, sw).astype(BF16)
        dst[:, 2 * LANES:3 * LANES] = jnp.where(lo, sw, 0.0).astype(BF16)
        dst[:, 3 * LANES:4 * LANES] = jnp.where(lo, 0.0, src).astype(BF16)


def _hy_swa_kernel(h_ref, why_ref, w_ref, cos_ref, sin_ref, hy_ref, q_ref, k_ref, v_ref, *, rope):
    hy_ref[...] = _dot(h_ref[...], why_ref[...])
    _swaproj_kernel(h_ref, w_ref, cos_ref, sin_ref, q_ref, k_ref, v_ref, rope=rope)


def _swa_proj(h, w, cos_t, sin_t, rope, w_hy=None, tm=512):
    m = h.shape[0]
    nt = cos_t.shape[0] // tm
    o = jax.ShapeDtypeStruct((m, 4 * LANES), BF16)
    ospec = pl.BlockSpec((tm, 4 * LANES), lambda i: (i, 0))
    tspec = pl.BlockSpec((tm, _SWA_Q), lambda i: (i % nt, 0))
    hspec = pl.BlockSpec((tm, D_MODEL), lambda i: (i, 0))
    if w_hy is None:
        return pl.pallas_call(
            functools.partial(_swaproj_kernel, rope=rope),
            out_shape=(o, o, o),
            grid=(m // tm,),
            in_specs=[hspec, _const_spec(w.shape), tspec, tspec],
            out_specs=(ospec, ospec, ospec),
            compiler_params=_cparams("parallel"),
            name="swa_proj",
        )(h, w, cos_t, sin_t)
    n_hy = w_hy.shape[1]
    return pl.pallas_call(
        functools.partial(_hy_swa_kernel, rope=rope),
        out_shape=(jax.ShapeDtypeStruct((m, n_hy), F32), o, o, o),
        grid=(m // tm,),
        in_specs=[hspec, _const_spec(w_hy.shape), _const_spec(w.shape), tspec, tspec],
        out_specs=(pl.BlockSpec((tm, n_hy), lambda i: (i, 0)), ospec, ospec, ospec),
        compiler_params=_cparams("parallel"),
        name="hyena_swa_proj",
    )(h, w_hy, w, cos_t, sin_t)


def _swa_kernel(sink_ref, q_ref, *refs, tq, win, lk, has_lat):
    if has_lat:
        kl_ref, vl_ref, kc_ref, vc_ref, o_ref = refs
        t0 = pl.program_id(1) * tq
        start = pl.multiple_of(jnp.clip(t0 - SWA_WINDOW, 0, lk - win), SWA_WINDOW)
        qpos = t0 + lax.broadcasted_iota(jnp.int32, (tq, win), 0)
        kpos = start + lax.broadcasted_iota(jnp.int32, (tq, win), 1)
        valid = jnp.abs(qpos - kpos) <= SWA_WINDOW
        valid = jnp.concatenate([valid, valid], axis=0)
    else:
        kc_ref, vc_ref, o_ref = refs
    upper = lax.broadcasted_iota(jnp.int32, (2 * tq, 1), 0) < tq
    for g in range(SWA_KV_HEADS):
        j0 = 2 * g
        qb = jnp.concatenate([q_ref[:, j0 * LANES:(j0 + 1) * LANES],
                              q_ref[:, (j0 + 1) * LANES:(j0 + 2) * LANES]], axis=0)
        acc = None
        for par in range(2):
            c0 = (2 * g + par) * LANES
            sk = jnp.where(upper, sink_ref[2 * j0 + par], sink_ref[2 * j0 + 2 + par]) * math.log2(math.e)
            s_c = _dot_nt(qb, kc_ref[:, c0:c0 + LANES])
            m = jnp.maximum(s_c.max(axis=-1, keepdims=True), sk)
            if has_lat:
                s_l = _dot_nt(qb, kl_ref[pl.ds(start, win), c0:c0 + LANES])
                s_l = jnp.where(valid, s_l, NEG_BIG)
                m = jnp.maximum(m, s_l.max(axis=-1, keepdims=True))
            p_c = jnp.exp2(s_c - m)
            den = p_c.sum(axis=-1, keepdims=True) + jnp.exp2(sk - m)
            if has_lat:
                p_l = jnp.exp2(s_l - m)
                den = den + p_l.sum(axis=-1, keepdims=True)
            o = _dot(p_c.astype(BF16), vc_ref[:, c0:c0 + LANES])
            if has_lat:
                o = o + _dot(p_l.astype(BF16), vl_ref[pl.ds(start, win), c0:c0 + LANES])
            o = o * (1.0 / den)
            acc = o if acc is None else acc + o
        o_ref[:, j0 * LANES:(j0 + 1) * LANES] = acc[:tq].astype(o_ref.dtype)
        o_ref[:, (j0 + 1) * LANES:(j0 + 2) * LANES] = acc[tq:].astype(o_ref.dtype)


def _swa_attention(sink, q, lat, ctx, lq, tq=256):
    nq = lq // tq
    win = tq + 2 * SWA_WINDOW
    full = lambda n: pl.BlockSpec((n, 4 * LANES), lambda b, i: (b, 0))
    in_specs = [pl.BlockSpec(memory_space=pltpu.SMEM),
                pl.BlockSpec((tq, 4 * LANES), lambda b, i: (b * nq + i, 0))]
    args = [sink, q]
    if lat is not None:
        in_specs += [full(SEQ), full(SEQ)]
        args += list(lat)
    in_specs += [full(CTX_LEN), full(CTX_LEN)]
    args += list(ctx)
    return pl.pallas_call(
        functools.partial(_swa_kernel, tq=tq, win=win, lk=SEQ, has_lat=lat is not None),
        out_shape=jax.ShapeDtypeStruct((BATCH * lq, 4 * LANES), BF16),
        grid=(BATCH, nq),
        in_specs=in_specs,
        out_specs=pl.BlockSpec((tq, 4 * LANES), lambda b, i: (b * nq + i, 0)),
        compiler_params=_cparams("parallel", "arbitrary"),
        name="swa_attention",
    )(*args)


def _out_kernel(a_ref, y_ref, s_ref, w_ref, x_ref, m_ref, g_ref, xo_ref, ho_ref):
    na = MLA_HEADS * MLA_V
    acc = _dot(a_ref[...], w_ref[0:na, :])
    acc = acc + _dot(y_ref[...], w_ref[na:na + HY_WIDTH, :])
    acc = acc + _dot(s_ref[...], w_ref[na + HY_WIDTH:, :])
    x = x_ref[...] + m_ref[2:3, :] * acc
    xo_ref[...] = x
    ho_ref[...] = (_rms(x, g_ref[...]) * (1.0 + m_ref[4:5, :]) + m_ref[3:4, :]).astype(ho_ref.dtype)


def _out_proj(a, y, s, w_out, x, mods, g_ffn, groups, row0, h_dtype, tm=512):
    m = x.shape[0]
    nr = m // groups // tm
    tile = lambda n: pl.BlockSpec((tm, n), lambda g_, i: (g_ * nr + i, 0))
    return pl.pallas_call(
        _out_kernel,
        out_shape=(jax.ShapeDtypeStruct((m, D_MODEL), F32), jax.ShapeDtypeStruct((m, D_MODEL), h_dtype)),
        grid=(groups, nr),
        in_specs=[tile(a.shape[1]), tile(y.shape[1]), tile(s.shape[1]), _const_spec(w_out.shape),
                  tile(D_MODEL), pl.BlockSpec((None, 6, D_MODEL), lambda g_, i: (row0 + g_, 0, 0)),
                  pl.BlockSpec((1, D_MODEL), lambda g_, i: (0, 0))],
        out_specs=(tile(D_MODEL), tile(D_MODEL)),
        compiler_params=_cparams("parallel", "parallel"),
        name="out_proj",
    )(a, y, s, w_out, x, mods, g_ffn.reshape(1, -1))


_META_W1, _META_W2, _META_E1, _META_E2, _META_R1, _META_R2 = range(6)


def _route(h, is_first, w_ref, meta_ref, cnt_ref, carry_ref):
    @pl.when(is_first)
    def _():
        carry_ref[...] = jnp.zeros_like(carry_ref)

    logits = _dot3(h, w_ref[...])
    tm = logits.shape[0]
    lane = lax.broadcasted_iota(jnp.int32, logits.shape, 1).astype(F32)
    logits = jnp.where(lane < N_EXPERTS, logits, NEG_BIG)
    m1 = logits.max(axis=-1, keepdims=True)
    i1 = jnp.where(logits == m1, lane, float(LANES)).min(axis=-1, keepdims=True)
    rest = jnp.where(lane == i1, NEG_BIG, logits)
    m2 = rest.max(axis=-1, keepdims=True)
    i2 = jnp.where(rest == m2, lane, float(LANES)).min(axis=-1, keepdims=True)
    e2 = jnp.exp(m2 - m1)
    w1 = 1.0 / (1.0 + e2)
    hot = jnp.where((lane == i1) | (lane == i2), 1.0, 0.0)
    before = (lax.broadcasted_iota(jnp.int32, (tm, tm), 0) > lax.broadcasted_iota(jnp.int32, (tm, tm), 1))
    seen = _dot(before.astype(F32).astype(BF16), hot.astype(BF16)) + carry_ref[0:1, :]
    r1 = jnp.where(lane == i1, seen, 0.0).sum(axis=-1, keepdims=True)
    r2 = jnp.where(lane == i2, seen, 0.0).sum(axis=-1, keepdims=True)
    carry_ref[...] = carry_ref[...] + hot.sum(axis=0, keepdims=True)
    cnt_ref[...] = carry_ref[...]
    rec = jnp.zeros_like(logits)
    for k, v in ((_META_W1, w1), (_META_W2, e2 * w1), (_META_E1, i1), (_META_E2, i2), (_META_R1, r1), (_META_R2, r2)):
        rec = jnp.where(lane == float(k), v, rec)
    meta_ref[...] = rec


def _router_kernel(h_ref, w_ref, meta_ref, cnt_ref, carry_ref):
    _route(h_ref[...], pl.program_id(0) == 0, w_ref, meta_ref, cnt_ref, carry_ref)


def _router(h, w_router, tm=512):
    m = h.shape[0]
    wp = jnp.zeros((D_MODEL, LANES), F32).at[:, :N_EXPERTS].set(w_router)
    return pl.pallas_call(
        _router_kernel,
        out_shape=(jax.ShapeDtypeStruct((m, LANES), F32), jax.ShapeDtypeStruct((8, LANES), F32)),
        grid=(m // tm,),
        in_specs=[pl.BlockSpec((tm, D_MODEL), lambda i: (i, 0)),
                  pl.BlockSpec((D_MODEL, LANES), lambda i: (0, 0))],
        out_specs=(pl.BlockSpec((tm, LANES), lambda i: (i, 0)), pl.BlockSpec((8, LANES), lambda i: (0, 0))),
        scratch_shapes=[pltpu.VMEM((8, LANES), F32)],
        compiler_params=_cparams("arbitrary"),
        name="moe_router",
    )(h, wp)


MOE_TM = 1024
MOE_BLK = 128
MOE_ROWS = (MOE_TM, 768, 512, 256)


def _moe_plan(meta, counts, m):
    n_steps = -(-2 * m // MOE_TM) + N_EXPERTS
    cnt = counts[0, :N_EXPERTS].astype(jnp.int32)
    steps_e = (cnt + MOE_TM - 1) // MOE_TM
    ends = jnp.cumsum(steps_e)
    first = ends - steps_e
    total = ends[-1]
    e = meta[:, _META_E1:_META_E2 + 1].astype(jnp.int32)
    rank = meta[:, _META_R1:_META_R2 + 1].astype(jnp.int32)
    pos = (first * MOE_TM)[e] + rank
    s_idx = jnp.arange(n_steps, dtype=jnp.int32)
    step_e = jnp.minimum(jnp.searchsorted(ends, s_idx, side="right").astype(jnp.int32), N_EXPERTS - 1)
    valid = jnp.clip(cnt[step_e] - (s_idx - first[step_e]) * MOE_TM, 0, MOE_TM)
    valid = jnp.where(s_idx < total, valid, 0)
    return pos.reshape(-1), step_e, valid, n_steps


def _row_copy(src_ref, src_row, dst_ref, dst_row, sem):
    return pltpu.make_async_copy(src_ref.at[pl.ds(src_row, 1)], dst_ref.at[pl.ds(dst_row, 1)], sem)


def _invert_kernel(pos_ref, sv_ref, src_ref):
    def clear_step(s, c):
        def clear(i, c2):
            src_ref[s * MOE_TM + i] = 0
            return c2

        return lax.fori_loop(sv_ref[s], MOE_TM, clear, c)

    def place(t, c):
        p0 = pos_ref[2 * t]
        p1 = pos_ref[2 * t + 1]
        src_ref[p0] = t
        src_ref[p1] = t
        return c

    lax.fori_loop(0, sv_ref.shape[0], clear_step, 0)
    lax.fori_loop(0, pos_ref.shape[0] // 2, place, 0, unroll=8)


def _moe_invert(pos, valid, n_rows):
    smem = pl.BlockSpec(memory_space=pltpu.SMEM)
    return pl.pallas_call(
        _invert_kernel,
        out_shape=jax.ShapeDtypeStruct((n_rows,), jnp.int32),
        in_specs=[smem, smem], out_specs=smem,
        name="moe_invert",
    )(pos, valid)


def _block_wait(src_ref, dst_ref, rows, sem):
    pltpu.make_async_copy(src_ref.at[pl.ds(0, rows)], dst_ref.at[pl.ds(0, rows)], sem).wait()


def _moe_ffn_kernel(se_ref, sv_ref, src_ref, h_ref, wg_ref, wu_ref, wd_ref, o_ref, x32_ref, xb_ref, sem):
    del se_ref
    s = pl.program_id(0)
    f = pl.program_id(1)
    valid = sv_ref[s]

    def rows_used(v):
        r = jnp.where(v > 0, MOE_ROWS[-1], 0)
        for small, big in zip(MOE_ROWS[:0:-1], MOE_ROWS[-2::-1]):
            r = jnp.where(v > small, big, r)
        return r

    def blocks(v):
        return rows_used(v) // MOE_BLK

    def gather(step):
        base = step * MOE_TM

        def issue(b, c):
            for k in range(8):
                i = b * 8 + k
                _row_copy(h_ref, src_ref[base + i], x32_ref, i, sem).start()
            return c

        lax.fori_loop(0, rows_used(sv_ref[step]) // 8, issue, 0)

    @pl.when(f == 0)
    def _():
        @pl.when(s == 0)
        def _():
            gather(0)

        def land(i, c):
            _block_wait(h_ref, x32_ref, MOE_BLK, sem)
            return c

        def to_bf16(i, c):
            r0 = pl.multiple_of(i * MOE_BLK, MOE_BLK)
            xb_ref[pl.ds(r0, MOE_BLK), :] = x32_ref[pl.ds(r0, MOE_BLK), :].astype(BF16)
            return c

        lax.fori_loop(0, blocks(valid), land, 0)
        lax.fori_loop(0, blocks(valid), to_bf16, 0)

        @pl.when(s + 1 < pl.num_programs(0))
        def _():
            gather(s + 1)

        o_ref[...] = jnp.zeros_like(o_ref)

    def compute(rows):
        xs = xb_ref[0:rows, :]
        mid = _silu(_dot(xs, wg_ref[...].astype(BF16))) * _dot(xs, wu_ref[...].astype(BF16))
        o_ref[0:rows, :] += _dot(mid.astype(BF16), wd_ref[...].astype(BF16))

    for k, rows in enumerate(MOE_ROWS):
        lower = MOE_ROWS[k + 1] if k + 1 < len(MOE_ROWS) else 0

        @pl.when((valid > lower) & (valid <= rows))
        def _():
            compute(rows)


def _moe_ffn(h, src, w_gate, w_up, w_down, step_e, valid, n_steps, tf=512):
    assert all(r % MOE_BLK == 0 for r in MOE_ROWS) and MOE_ROWS[0] == MOE_TM
    n_ff = w_gate.shape[-1]
    nf = n_ff // tf
    fidx = lambda s, f, sv: jnp.where(sv[s] > 0, f, nf - 1)
    return pl.pallas_call(
        _moe_ffn_kernel,
        out_shape=jax.ShapeDtypeStruct((n_steps * MOE_TM, D_MODEL), F32),
        grid_spec=pltpu.PrefetchScalarGridSpec(
            num_scalar_prefetch=3, grid=(n_steps, nf),
            in_specs=[
                pl.BlockSpec(memory_space=pl.ANY),
                pl.BlockSpec((None, D_MODEL, tf), lambda s, f, se, sv, sr: (se[s], 0, fidx(s, f, sv))),
                pl.BlockSpec((None, D_MODEL, tf), lambda s, f, se, sv, sr: (se[s], 0, fidx(s, f, sv))),
                pl.BlockSpec((None, tf, D_MODEL), lambda s, f, se, sv, sr: (se[s], fidx(s, f, sv), 0)),
            ],
            out_specs=pl.BlockSpec((MOE_TM, D_MODEL), lambda s, f, se, sv, sr: (s, 0), pipeline_mode=pl.Buffered(1)),
            scratch_shapes=[pltpu.VMEM((MOE_TM, D_MODEL), F32), pltpu.VMEM((MOE_TM, D_MODEL), BF16),
                            pltpu.SemaphoreType.DMA(())]),
        compiler_params=_cparams("arbitrary", "arbitrary"),
        name="moe_experts",
    )(step_e, valid, src, h, w_gate, w_up, w_down)


def _combine_kernel(pos_ref, meta_ref, x_ref, m_ref, gf_ref, ys_ref, o_ref, buf_ref, sem, *, tm, final):
    base = pl.program_id(0) * tm

    def issue(i, c):
        for slot in range(2):
            _row_copy(ys_ref, pos_ref[2 * (base + i) + slot], buf_ref.at[slot], i, sem).start()
        return c

    lax.fori_loop(0, tm, issue, 0, unroll=4)
    for slot in range(2):
        _block_wait(ys_ref, buf_ref.at[slot], tm, sem)
    meta = meta_ref[...]
    y = meta[:, _META_W1:_META_W1 + 1] * buf_ref[0] + meta[:, _META_W2:_META_W2 + 1] * buf_ref[1]
    x = x_ref[...] + m_ref[5:6, :] * y
    if final:
        x = _rms(x, gf_ref[...])
    o_ref[...] = x


def _moe_combine(pos, meta, x, mods, ys, groups, row0, g_final, tm=512):
    m = x.shape[0]
    nr = m // groups // tm
    final = g_final is not None
    gf = (g_final if final else jnp.ones((D_MODEL,), F32)).reshape(1, D_MODEL)
    return pl.pallas_call(
        functools.partial(_combine_kernel, tm=tm, final=final),
        out_shape=jax.ShapeDtypeStruct((m, D_MODEL), F32),
        grid_spec=pltpu.PrefetchScalarGridSpec(
            num_scalar_prefetch=1, grid=(m // tm,),
            in_specs=[pl.BlockSpec((tm, LANES), lambda i, p: (i, 0)),
                      pl.BlockSpec((tm, D_MODEL), lambda i, p: (i, 0)),
                      pl.BlockSpec((None, 6, D_MODEL), lambda i, p: (row0 + i // nr, 0, 0)),
                      pl.BlockSpec((1, D_MODEL), lambda i, p: (0, 0)),
                      pl.BlockSpec(memory_space=pl.ANY)],
            out_specs=pl.BlockSpec((tm, D_MODEL), lambda i, p: (i, 0)),
            scratch_shapes=[pltpu.VMEM((2, tm, D_MODEL), F32), pltpu.SemaphoreType.DMA(())]),
        compiler_params=_cparams("arbitrary"),
        name="moe_combine",
    )(pos, meta, x, mods, gf, ys)


def _moe(h, meta, counts, x, mods, groups, row0, w_gate, w_up, w_down, g_final):
    m = h.shape[0]
    pos, step_e, valid, n_steps = _moe_plan(meta, counts, m)
    src = _moe_invert(pos, valid, n_steps * MOE_TM)
    ys = _moe_ffn(h, src, w_gate, w_up, w_down, step_e, valid, n_steps)
    return _moe_combine(pos, meta, x, mods, ys, groups, row0, g_final)


def _ffn_kernel(h_ref, wg_ref, wu_ref, wd_ref, x_ref, m_ref, gn_ref, mn_ref, o_ref, *maybe_hn_ref):
    f = pl.program_id(2)

    @pl.when(f == 0)
    def _():
        o_ref[...] = jnp.zeros_like(o_ref)

    h = h_ref[...]
    mid = _silu(_dot(h, wg_ref[...])) * _dot(h, wu_ref[...])
    o_ref[...] += _dot(mid.astype(BF16), wd_ref[...])

    @pl.when(f == pl.num_programs(2) - 1)
    def _():
        x = x_ref[...] + m_ref[5:6, :] * o_ref[...]
        o_ref[...] = x
        for hn_ref in maybe_hn_ref:
            hn_ref[...] = (_rms(x, gn_ref[...]) * (1.0 + mn_ref[1:2, :]) + mn_ref[0:1, :]).astype(hn_ref.dtype)


def _ffn(h, w_gate, w_up, w_down, x, mods, groups, row0, nxt, tm, tf=512):
    m = x.shape[0]
    n_ff = w_gate.shape[1]
    nr = m // groups // tm
    g_next, mods_next = nxt if nxt is not None else (jnp.ones((D_MODEL,), F32), mods)
    once = pl.Buffered(1)
    tile = lambda n, mode=None: pl.BlockSpec((tm, n), lambda g_, i, f: (g_ * nr + i, 0), pipeline_mode=mode)
    mod_spec = pl.BlockSpec((None, 6, D_MODEL), lambda g_, i, f: (row0 + g_, 0, 0))
    out_shape = [jax.ShapeDtypeStruct((m, D_MODEL), F32)]
    out_specs = [tile(D_MODEL, once)]
    if nxt is not None:
        out_shape.append(jax.ShapeDtypeStruct((m, D_MODEL), BF16))
        out_specs.append(tile(D_MODEL, once))
    outs = pl.pallas_call(
        _ffn_kernel,
        out_shape=tuple(out_shape),
        grid=(groups, nr, n_ff // tf),
        in_specs=[tile(D_MODEL),
                  pl.BlockSpec((D_MODEL, tf), lambda g_, i, f: (0, f)),
                  pl.BlockSpec((D_MODEL, tf), lambda g_, i, f: (0, f)),
                  pl.BlockSpec((tf, D_MODEL), lambda g_, i, f: (f, 0)),
                  tile(D_MODEL, once), mod_spec,
                  pl.BlockSpec((1, D_MODEL), lambda g_, i, f: (0, 0)), mod_spec],
        out_specs=tuple(out_specs),
        compiler_params=_cparams("parallel", "parallel", "arbitrary"),
        name="swiglu_ffn",
    )(h, w_gate, w_up, w_down, x, mods, g_next.reshape(1, D_MODEL), mods_next)
    return outs if nxt is not None else (outs[0], None)


def _swap_halves(w):
    half = w.shape[-1] // 2
    return jnp.concatenate([w[..., half:], w[..., :half]], axis=-1)


def _swap_heads(w, heads, dim):
    k = w.shape[0]
    return _swap_halves(w.reshape(k, heads, dim)).reshape(k, heads * dim)


def _rope_tables():
    rows = SEQ // GRID_W
    row = np.repeat(np.arange(rows), GRID_W).astype(np.float32)
    col = np.tile(np.arange(GRID_W), rows).astype(np.float32)
    quarter = MLA_ROPE // 4
    freqs = (np.float32(ROPE_THETA) ** (-np.arange(quarter, dtype=np.float32) / quarter)).astype(np.float32)
    ang = np.concatenate([row[:, None] * freqs[None], col[:, None] * freqs[None]], axis=-1)
    cos = np.cos(ang.astype(np.float64))
    sin = np.sin(ang.astype(np.float64))
    cos2 = np.concatenate([cos, cos], axis=-1)
    sin2 = np.concatenate([-sin, sin], axis=-1)
    tab_lat = np.concatenate([cos2, sin2], axis=-1)
    tab_ctx = np.concatenate([np.ones((CTX_LEN, MLA_ROPE)), np.zeros((CTX_LEN, MLA_ROPE))], axis=-1)
    return dict(
        tab_lat=jnp.asarray(tab_lat, F32), tab_ctx=jnp.asarray(tab_ctx, F32),
        cos8=jnp.asarray(np.tile(cos2, (1, SWA_HEADS)), F32),
        sin8=jnp.asarray(np.tile(sin2, (1, SWA_HEADS)), F32),
        ones8=jnp.ones((CTX_LEN, _SWA_Q), F32), zeros8=jnp.zeros((CTX_LEN, _SWA_Q), F32),
    )


def _layer_weights(p, l):
    w_in = p["w_in"][l]
    cq = w_in[:, _O_CQ:_O_CKV]
    ckv = w_in[:, _O_CKV:_O_KPE]
    kpe = w_in[:, _O_KPE:_O_HY]
    hy = w_in[:, _O_HY:_O_SQ]
    sq = w_in[:, _O_SQ:_O_SK]
    sk = w_in[:, _O_SK:_O_SV]
    sv = w_in[:, _O_SV:]
    dq = MLA_NOPE + MLA_ROPE
    wq = p["w_q_up"][l].reshape(MLA_Q_LORA, MLA_HEADS, dq)
    wq = jnp.concatenate([wq, _swap_halves(wq[..., MLA_NOPE:])], axis=-1)
    wkv = p["w_kv_up"][l].reshape(MLA_KV_LORA, MLA_HEADS, MLA_NOPE + MLA_V)
    return dict(
        w_cq=cq.astype(BF16),
        w_ckv=jnp.concatenate([ckv, kpe, _swap_halves(kpe)], axis=-1).astype(BF16),
        w_hy=hy.astype(BF16),
        w_swa=jnp.concatenate([sq, _swap_heads(sq, SWA_HEADS, SWA_HEAD_DIM), sk,
                               _swap_heads(sk, SWA_KV_HEADS, SWA_HEAD_DIM), sv], axis=-1).astype(BF16),
        w_q=jnp.transpose(wq, (1, 0, 2)).astype(BF16),
        w_kv=jnp.transpose(wkv, (1, 0, 2)).astype(BF16),
        w_out=p["w_out"][l].astype(BF16),
        g_q=p["g_q"][l], g_kv=p["g_kv"][l],
        hy_conv_w=p["hy_conv_w"][l], hy_conv_b=p["hy_conv_b"][l],
        hy_w1=p["hy_w1"][l], hy_b1=p["hy_b1"][l], hy_w2=p["hy_w2"][l], hy_b2=p["hy_b2"][l],
        hy_w3=p["hy_w3"][l], hy_b3=p["hy_b3"][l], hy_w_filt=p["hy_w_filt"][l],
        hy_freq=p["hy_freq"][l], hy_skip=p["hy_skip"][l], swa_sink=p["swa_sink"][l],
    )


def _hyena(hy, lw, n_len, tabs):
    x0, vx, vxb = _hy_pre(hy, lw["hy_conv_w"], lw["hy_conv_b"], n_len)
    spectra = _hy_filters(lw, n_len, tabs)
    y = _hy_long_conv(x0, vx, vxb, spectra, lw["hy_skip"], n_len, tabs)
    return y.reshape(BATCH * n_len, HY_WIDTH)


def _mixer(h_lat, h_ctx, lw, rope, hy_tabs, need_ctx):
    q_l, k_l, v_l = _mla_proj(h_lat, lw, rope["tab_lat"])
    if need_ctx:
        q_c, k_c, v_c = _mla_proj(h_ctx, lw, rope["tab_ctx"], tm=CTX_LEN)
    else:
        k_c, v_c = _kv_proj(h_ctx, lw["w_ckv"], lw["g_kv"], lw["w_kv"], rope["tab_ctx"], tm=CTX_LEN)
    a_l = _mla_attention(q_l, [(k_l, v_l, SEQ), (k_c, v_c, CTX_LEN)], SEQ)
    hy_l, sq_l, kk_l, vv_l = _swa_proj(h_lat, lw["w_swa"], rope["cos8"], rope["sin8"], True, w_hy=lw["w_hy"])
    y_l = _hyena(hy_l, lw, SEQ, hy_tabs[SEQ])
    if need_ctx:
        hy_c, sq_c, kk_c, vv_c = _swa_proj(h_ctx, lw["w_swa"], rope["ones8"], rope["zeros8"], False,
                                           w_hy=lw["w_hy"], tm=CTX_LEN)
    else:
        sq_c, kk_c, vv_c = _swa_proj(h_ctx, lw["w_swa"], rope["ones8"], rope["zeros8"], False, tm=CTX_LEN)
    s_l = _swa_attention(lw["swa_sink"], sq_l, (kk_l, vv_l), (kk_c, vv_c), SEQ)
    if not need_ctx:
        return (a_l, y_l, s_l), None
    a_c = _mla_attention(q_c, [(k_c, v_c, CTX_LEN)], CTX_LEN, tq=CTX_LEN)
    y_c = _hyena(hy_c, lw, CTX_LEN, hy_tabs[CTX_LEN])
    s_c = _swa_attention(lw["swa_sink"], sq_c, None, (kk_c, vv_c), CTX_LEN, tq=CTX_LEN)
    return (a_l, y_l, s_l), (a_c, y_c, s_c)


def _forward(p):
    rope = _rope_tables()
    hy_tabs = {n: dict(dft=_dft_tables(n), filt=_filter_tables(n)) for n in (SEQ, CTX_LEN)}
    cvec = jnp.zeros((MOD_ROWS, D_MODEL), F32).at[:BATCH].set(p["c"]).at[BATCH].set(p["c_ctx"])
    mods_all = _modulation(cvec, p["w_mod"], p["b_mod"]).reshape(DEPTH, MOD_ROWS, 6, D_MODEL)

    x_lat = p["x"].reshape(BATCH * SEQ, D_MODEL)
    x_ctx = p["ctx"].reshape(BATCH * CTX_LEN, D_MODEL)
    h_in = [None, None]
    for l in range(DEPTH):
        last = l == DEPTH - 1
        mods = mods_all[l]
        lw = _layer_weights(p, l)
        h_lat = h_in[0] if h_in[0] is not None else _norm_mod(x_lat, p["g_mix"][l], mods, BATCH, 0, 0, BF16)
        h_ctx = h_in[1] if h_in[1] is not None else _norm_mod(x_ctx, p["g_mix"][l], mods, 1, BATCH, 0, BF16)
        mix_l, mix_c = _mixer(h_lat, h_ctx, lw, rope, hy_tabs, not last)
        streams = [(x_lat, mix_l, BATCH, 0, 1024)]
        if not last:
            streams.append((x_ctx, mix_c, 1, BATCH, 512))
        outs = []
        h_in = [None, None]
        i = l // 2
        dense = l % 2 == 0
        for n, (x, mix, groups, row0, ffn_tm) in enumerate(streams):
            if dense:
                x, h2 = _out_proj(*mix, lw["w_out"], x, mods, p["g_ffn"][l], groups, row0, BF16)
                nxt = None if last else (p["g_mix"][l + 1], mods_all[l + 1])
                x, h_in[n] = _ffn(h2, p["ffn_w_gate"][i].astype(BF16), p["ffn_w_up"][i].astype(BF16),
                                  p["ffn_w_down"][i].astype(BF16), x, mods, groups, row0, nxt, ffn_tm)
            else:
                x, h2 = _out_proj(*mix, lw["w_out"], x, mods, p["g_ffn"][l], groups, row0, F32)
                meta, counts = _router(h2, p["moe_router"][i])
                g_final = p["g_final"] if (last and n == 0) else None
                x = _moe(h2, meta, counts, x, mods, groups, row0, p["moe_w_gate"][i], p["moe_w_up"][i],
                         p["moe_w_down"][i], g_final)
            outs.append(x)
        x_lat = outs[0]
        if not last:
            x_ctx = outs[1]
    if DEPTH % 2 == 1:
        x_lat = _norm_mod(x_lat, p["g_final"], mods_all[0], BATCH, 0, None, F32)
    return x_lat.reshape(BATCH, SEQ, D_MODEL)


def kernel(x, c, ctx, c_ctx, w_mod, b_mod, g_mix, g_ffn, w_in, g_q, w_q_up, g_kv, w_kv_up, hy_conv_w, hy_conv_b, hy_w1, hy_b1, hy_w2, hy_b2, hy_w3, hy_b3, hy_w_filt, hy_freq, hy_skip, swa_sink, w_out, ffn_w_gate, ffn_w_up, ffn_w_down, moe_router, moe_w_gate, moe_w_up, moe_w_down, g_final):
    return _forward(dict(
        x=x, c=c, ctx=ctx, c_ctx=c_ctx, w_mod=w_mod, b_mod=b_mod, g_mix=g_mix, g_ffn=g_ffn, w_in=w_in,
        g_q=g_q, w_q_up=w_q_up, g_kv=g_kv, w_kv_up=w_kv_up, hy_conv_w=hy_conv_w, hy_conv_b=hy_conv_b,
        hy_w1=hy_w1, hy_b1=hy_b1, hy_w2=hy_w2, hy_b2=hy_b2, hy_w3=hy_w3, hy_b3=hy_b3,
        hy_w_filt=hy_w_filt, hy_freq=hy_freq, hy_skip=hy_skip, swa_sink=swa_sink, w_out=w_out,
        ffn_w_gate=ffn_w_gate, ffn_w_up=ffn_w_up, ffn_w_down=ffn_w_down, moe_router=moe_router,
        moe_w_gate=moe_w_gate, moe_w_up=moe_w_up, moe_w_down=moe_w_down, g_final=g_final))
```

```python
import functools
import math

import numpy as np
import jax
import jax.numpy as jnp
from jax import lax
from jax.experimental import pallas as pl
from jax.experimental.pallas import tpu as pltpu

F32 = jnp.float32
BF16 = jnp.bfloat16

D_MODEL = 2048
BATCH = 4
SEQ = 2048
DEPTH = 2
CTX_LEN = 256
GRID_W = 64
NORM_EPS = 1e-6
ROPE_THETA = 10000.0
MLA_HEADS = 8
MLA_NOPE = 128
MLA_ROPE = 64
MLA_V = 128
MLA_Q_LORA = 768
MLA_KV_LORA = 512
HY_WIDTH = 512
HY_BANDS = 16
HY_EMB = 1 + 2 * HY_BANDS
HY_FILTER_HIDDEN = 64
HY_DECAY_TARGET = 1e-2
HY_QUICK_DECAY_PCT = 0.3
HY_SLOW_DECAY_PCT = 1.5
SWA_HEADS = 8
SWA_KV_HEADS = 2
SWA_HEAD_DIM = 64
SWA_WINDOW = 128
N_EXPERTS = 8
D_FF = 5632
D_FF_EXPERT = 7168

LANES = 128
MXU_DIM = 256
VMEM_LIMIT_BYTES = 56 * 1024 * 1024
NEG_BIG = -1e30
MOD_ROWS = 8

_O_CQ = 0
_O_CKV = _O_CQ + MLA_Q_LORA
_O_KPE = _O_CKV + MLA_KV_LORA
_O_HY = _O_KPE + MLA_ROPE
_O_SQ = _O_HY + 3 * HY_WIDTH
_O_SK = _O_SQ + SWA_HEADS * SWA_HEAD_DIM
_O_SV = _O_SK + SWA_KV_HEADS * SWA_HEAD_DIM


def _cparams(*sem):
    return pltpu.CompilerParams(dimension_semantics=sem, vmem_limit_bytes=VMEM_LIMIT_BYTES)


def _dot(a, b):
    return jnp.dot(a, b, preferred_element_type=F32)


def _dot_nt(a, b):
    return lax.dot_general(a, b, (((1,), (1,)), ((), ())), preferred_element_type=F32)


def _split_bf16(a):
    hi = a.astype(BF16)
    lo = (a - hi.astype(F32)).astype(BF16)
    return hi, lo


def _dot3(a, b):
    ah, al = _split_bf16(a)
    bh, bl = _split_bf16(b)
    return _dot(ah, bh) + (_dot(al, bh) + _dot(ah, bl))


def _silu(x):
    return x / (1.0 + jnp.exp(-x))


def _rms(x, g):
    ms = jnp.mean(x * x, axis=-1, keepdims=True)
    return x * lax.rsqrt(ms + NORM_EPS) * g


def _const_spec(shape):
    nd = len(shape)
    return pl.BlockSpec(shape, lambda *_: (0,) * nd, pipeline_mode=pl.Buffered(1))


def _mod_kernel(c_ref, w_ref, b_ref, o_ref):
    ah, al = _split_bf16(_silu(c_ref[...]))
    wh, wl = _split_bf16(w_ref[0])
    r = _dot(jnp.concatenate([ah, al], axis=0), wh)
    o_ref[0] = r[:MOD_ROWS] + (r[MOD_ROWS:] + _dot(ah, wl)) + b_ref[0]


def _modulation(cvec, w_mod, b_mod):
    tn = 1024
    n = 6 * D_MODEL
    return pl.pallas_call(
        _mod_kernel,
        out_shape=jax.ShapeDtypeStruct((DEPTH, MOD_ROWS, n), F32),
        grid=(DEPTH, n // tn),
        in_specs=[
            pl.BlockSpec((MOD_ROWS, D_MODEL), lambda l, j: (0, 0)),
            pl.BlockSpec((1, D_MODEL, tn), lambda l, j: (l, 0, j)),
            pl.BlockSpec((1, 1, tn), lambda l, j: (l, 0, j)),
        ],
        out_specs=pl.BlockSpec((1, MOD_ROWS, tn), lambda l, j: (l, 0, j)),
        compiler_params=_cparams("arbitrary", "arbitrary"),
        name="adaln_mod",
    )(cvec, w_mod, b_mod.reshape(DEPTH, 1, n))


def _norm_kernel(x_ref, g_ref, m_ref, o_ref, *, si):
    y = _rms(x_ref[...], g_ref[...])
    if si is not None:
        y = y * (1.0 + m_ref[si + 1:si + 2, :]) + m_ref[si:si + 1, :]
    o_ref[...] = y.astype(o_ref.dtype)


def _norm_mod(x, g, mods, groups, row0, si, out_dtype, tm=512):
    m = x.shape[0]
    nr = m // groups // tm
    return pl.pallas_call(
        functools.partial(_norm_kernel, si=si),
        out_shape=jax.ShapeDtypeStruct((m, D_MODEL), out_dtype),
        grid=(groups, nr),
        in_specs=[
            pl.BlockSpec((tm, D_MODEL), lambda g_, i: (g_ * nr + i, 0)),
            pl.BlockSpec((1, D_MODEL), lambda g_, i: (0, 0)),
            pl.BlockSpec((None, 6, D_MODEL), lambda g_, i: (row0 + g_, 0, 0)),
        ],
        out_specs=pl.BlockSpec((tm, D_MODEL), lambda g_, i: (g_ * nr + i, 0)),
        compiler_params=_cparams("parallel", "parallel"),
        name="norm_mod",
    )(x, g.reshape(1, D_MODEL), mods)


def _rope_halves(pe_pair, tab):
    r = pe_pair * tab
    return r + pltpu.roll(r, MLA_ROPE, axis=1)


MLA_Q_SCALE = (MLA_NOPE + MLA_ROPE) ** -0.5 * math.log2(math.e)


def _qproj_body(h_ref, wc_ref, g_ref, wq_ref, tab_ref, o_ref):
    cq = _dot(h_ref[...], wc_ref[...])
    cqn = _rms(cq, g_ref[...]).astype(BF16)
    tab = tab_ref[...] * MLA_Q_SCALE
    for hh in range(MLA_HEADS):
        r = _dot(cqn, wq_ref[hh])
        o_ref[hh, :, 0:MLA_NOPE] = (r[:, :MLA_NOPE] * MLA_Q_SCALE).astype(BF16)
        o_ref[hh, :, MLA_NOPE:] = _rope_halves(r[:, MLA_NOPE:], tab).astype(BF16)


def _mla_proj_kernel(h_ref, wcq_ref, gq_ref, wq_ref, wckv_ref, gkv_ref, wkv_ref, tab_ref, q_ref, k_ref, v_ref):
    _qproj_body(h_ref, wcq_ref, gq_ref, wq_ref, tab_ref, q_ref)
    _kvproj_kernel(h_ref, wckv_ref, gkv_ref, wkv_ref, tab_ref, k_ref, v_ref)


def _mla_proj(h, lw, tab, tm=512):
    m = h.shape[0]
    nt = tab.shape[0] // tm
    head_out = jax.ShapeDtypeStruct((MLA_HEADS, m, MXU_DIM), BF16)
    head_spec = pl.BlockSpec((MLA_HEADS, tm, MXU_DIM), lambda i: (0, i, 0))
    return pl.pallas_call(
        _mla_proj_kernel,
        out_shape=(head_out, head_out, head_out),
        grid=(m // tm,),
        in_specs=[
            pl.BlockSpec((tm, D_MODEL), lambda i: (i, 0)),
            _const_spec(lw["w_cq"].shape), _const_spec((1, MLA_Q_LORA)), _const_spec(lw["w_q"].shape),
            _const_spec(lw["w_ckv"].shape), _const_spec((1, MLA_KV_LORA)), _const_spec(lw["w_kv"].shape),
            pl.BlockSpec((tm, LANES), lambda i: (i % nt, 0)),
        ],
        out_specs=(head_spec, head_spec, head_spec),
        compiler_params=_cparams("parallel"),
        name="mla_proj",
    )(h, lw["w_cq"], lw["g_q"].reshape(1, -1), lw["w_q"], lw["w_ckv"], lw["g_kv"].reshape(1, -1), lw["w_kv"], tab)


def _kvproj_kernel(h_ref, wc_ref, g_ref, wkv_ref, tab_ref, k_ref, v_ref):
    t = _dot(h_ref[...], wc_ref[...])
    ckvn = _rms(t[:, :MLA_KV_LORA], g_ref[...]).astype(BF16)
    rot = _rope_halves(t[:, MLA_KV_LORA:], tab_ref[...])
    lane = lax.broadcasted_iota(jnp.int32, rot.shape, 1)
    kpe = jnp.where(lane < MLA_ROPE, rot, 0.0).astype(BF16)
    for hh in range(MLA_HEADS):
        r = _dot(ckvn, wkv_ref[hh])
        k_ref[hh, :, 0:MLA_NOPE] = r[:, :MLA_NOPE].astype(BF16)
        k_ref[hh, :, MLA_NOPE:] = kpe
        v_ref[hh, :, 0:MLA_V] = r[:, MLA_NOPE:].astype(BF16)
        v_ref[hh, :, MLA_V:] = jnp.ones((r.shape[0], MXU_DIM - MLA_V), BF16)


def _kv_proj(h, w_ckv, g_kv, w_kv, tab, tm=512):
    m = h.shape[0]
    nt = tab.shape[0] // tm
    return pl.pallas_call(
        _kvproj_kernel,
        out_shape=(jax.ShapeDtypeStruct((MLA_HEADS, m, MXU_DIM), BF16),
                   jax.ShapeDtypeStruct((MLA_HEADS, m, MXU_DIM), BF16)),
        grid=(m // tm,),
        in_specs=[
            pl.BlockSpec((tm, D_MODEL), lambda i: (i, 0)),
            _const_spec(w_ckv.shape),
            _const_spec((1, MLA_KV_LORA)),
            _const_spec(w_kv.shape),
            pl.BlockSpec((tm, LANES), lambda i: (i % nt, 0)),
        ],
        out_specs=(pl.BlockSpec((MLA_HEADS, tm, MXU_DIM), lambda i: (0, i, 0)),
                   pl.BlockSpec((MLA_HEADS, tm, MXU_DIM), lambda i: (0, i, 0))),
        compiler_params=_cparams("parallel"),
        name="mla_kv_proj",
    )(h, w_ckv, g_kv.reshape(1, -1), w_kv, tab)


MLA_KEY_CHUNK = 512


def _mla_kernel(q_ref, *refs, seg_lens):
    nseg = len(seg_lens)
    o_ref = refs[2 * nseg]
    q = q_ref[...]
    m = jnp.full((q.shape[0], 1), NEG_BIG, F32)
    acc = jnp.zeros((q.shape[0], MXU_DIM), F32)
    for s, lk in enumerate(seg_lens):
        k_ref, v_ref = refs[2 * s], refs[2 * s + 1]
        step = min(MLA_KEY_CHUNK, lk)
        for c0 in range(0, lk, step):
            sc = _dot_nt(q, k_ref[c0:c0 + step, :])
            m_new = jnp.maximum(m, sc.max(axis=-1, keepdims=True))
            p = jnp.exp2(sc - m_new).astype(BF16)
            acc = acc * jnp.exp2(m - m_new) + _dot(p, v_ref[c0:c0 + step, :])
            m = m_new
    o_ref[...] = (acc[:, :MLA_V] / acc[:, MLA_V:MLA_V + 1]).astype(o_ref.dtype)


def _mla_attention(q, segs, lq, tq=512):
    nq = lq // tq
    in_specs = [pl.BlockSpec((None, tq, MXU_DIM), lambda b, h, i: (h, b * nq + i, 0))]
    args = [q]
    for k, v, lk in segs:
        in_specs.append(pl.BlockSpec((None, lk, MXU_DIM), lambda b, h, i: (h, b, 0)))
        in_specs.append(pl.BlockSpec((None, lk, MXU_DIM), lambda b, h, i: (h, b, 0)))
        args += [k, v]
    return pl.pallas_call(
        functools.partial(_mla_kernel, seg_lens=tuple(lk for _, _, lk in segs)),
        out_shape=jax.ShapeDtypeStruct((BATCH * lq, MLA_HEADS * MLA_V), BF16),
        grid=(BATCH, MLA_HEADS, nq),
        in_specs=in_specs,
        out_specs=pl.BlockSpec((tq, MLA_V), lambda b, h, i: (b * nq + i, h)),
        compiler_params=_cparams("parallel", "parallel", "arbitrary"),
        name="mla_attention",
    )(*args)


def _dft_tables(n_len):
    n2 = 2 * n_len
    idx = np.arange(n_len, dtype=np.int64)
    ang = (2.0 * np.pi / n2) * ((idx[:, None] * idx[None, :]) % n2).astype(np.float64)
    cm = np.cos(ang)
    sf = np.sin(ang)
    sf[0, :] = 1.0 - 2.0 * (idx % 2)
    return (jnp.asarray(cm, dtype=BF16), jnp.asarray(sf, dtype=BF16),
            jnp.asarray(sf.T.copy(), dtype=BF16))


def _filter_tables(n_len):
    pos = np.arange(n_len, dtype=np.float64)
    t = pos / max(n_len - 1, 1)
    bands = np.linspace(1e-4, HY_BANDS - 1, HY_BANDS)
    ang = (2.0 * math.pi / n_len) * pos[:, None] * bands[None]
    z = np.zeros((n_len, LANES), np.float64)
    z[:, 0] = t
    z[:, 1:1 + HY_BANDS] = np.cos(ang)
    z[:, 1 + HY_BANDS:HY_EMB] = -np.sin(ang)
    deltas = np.linspace(math.log(HY_DECAY_TARGET) / HY_SLOW_DECAY_PCT,
                         math.log(HY_DECAY_TARGET) / HY_QUICK_DECAY_PCT, HY_WIDTH)
    decay = np.exp(-t[:, None] * np.abs(deltas)[None])
    return jnp.asarray(z, dtype=F32), jnp.asarray(decay, dtype=F32)


def _hy_filter_kernel(z_ref, w1_ref, b1_ref, w2_ref, b2_ref, w3_ref, b3_ref, wf_ref, wb_ref,
                      fr_ref, dec_ref, cm_ref, sf_ref, a_ref, b_ref, d_ref, hid_ref, *, n_len):
    @pl.when(pl.program_id(0) == 0)
    def _():
        fr = fr_ref[...]
        h = jnp.sin(fr * (_dot3(z_ref[...], w1_ref[...]) + b1_ref[...]))
        h = jnp.sin(fr * (_dot3(h, w2_ref[...]) + b2_ref[...]))
        hid_ref[...] = jnp.sin(fr * (_dot3(h, w3_ref[...]) + b3_ref[...]))

    h = hid_ref[...]
    dec = dec_ref[...]
    h_f = _dot3(h, wf_ref[...]) * dec
    h_b = _dot3(h, wb_ref[...]) * dec
    row = lax.broadcasted_iota(jnp.int32, h_f.shape, 0)
    h_b = jnp.where(row == 0, 0.0, h_b)
    hs = h_f + h_b
    hd = h_f - h_b
    nc = hs.shape[1]
    hs2 = jnp.concatenate(_split_bf16(hs), axis=1)
    hd2 = jnp.concatenate(_split_bf16(hd), axis=1)
    sf = sf_ref[...]
    t2 = _dot(cm_ref[...], hs2)
    g2 = _dot(sf, hd2)
    n2 = _dot(sf[0:16, :], hs2)[0:1, :]
    t_re = t2[:, :nc] + t2[:, nc:]
    g_t = g2[:, :nc] + g2[:, nc:]
    t_ny = n2[:, :nc] + n2[:, nc:]
    inv_n = 1.0 / (2 * n_len)
    first = row == 0
    a_ref[...] = jnp.where(first, inv_n, 2.0 * inv_n) * t_re
    b_ref[...] = jnp.where(first, 0.0, -2.0 * inv_n * g_t)
    d_ref[...] = jnp.where(first, inv_n * t_ny, 2.0 * inv_n * t_re)


def _hy_filters(lp, n_len, tabs):
    z, decay = tabs["filt"]
    cm, sf, _ = tabs["dft"]
    cb = LANES
    nblk = HY_WIDTH // cb
    w1 = jnp.zeros((LANES, HY_FILTER_HIDDEN), F32).at[:HY_EMB].set(lp["hy_w1"])
    row = lambda a: a.reshape(1, -1)
    small = lambda shape: pl.BlockSpec(shape, lambda j: (0,) * len(shape))
    fh = HY_FILTER_HIDDEN
    out = jax.ShapeDtypeStruct((n_len, HY_WIDTH), F32)
    return pl.pallas_call(
        functools.partial(_hy_filter_kernel, n_len=n_len),
        out_shape=(out, out, out),
        grid=(nblk,),
        in_specs=[
            small((n_len, LANES)), small((LANES, fh)), small((1, fh)), small((fh, fh)), small((1, fh)),
            small((fh, fh)), small((1, fh)),
            pl.BlockSpec((fh, cb), lambda j: (0, j)),
            pl.BlockSpec((fh, cb), lambda j: (0, nblk + j)),
            small((1, fh)),
            pl.BlockSpec((n_len, cb), lambda j: (0, j)),
            _const_spec((n_len, n_len)), _const_spec((n_len, n_len)),
        ],
        out_specs=tuple(pl.BlockSpec((n_len, cb), lambda j: (0, j)) for _ in range(3)),
        scratch_shapes=[pltpu.VMEM((n_len, fh), F32)],
        compiler_params=_cparams("arbitrary"),
        name="hyena_filters",
    )(z, w1, row(lp["hy_b1"]), lp["hy_w2"], row(lp["hy_b2"]), lp["hy_w3"], row(lp["hy_b3"]),
      lp["hy_w_filt"], lp["hy_w_filt"], row(lp["hy_freq"]), decay, cm, sf)


def _hy_pre_kernel(u0_ref, u1_ref, u2_ref, w0_ref, w1_ref, w2_ref, b0_ref, b1_ref, b2_ref,
                   x0_ref, vx_ref, vxb_ref, *, n_len):
    def sconv(u_ref, w_ref, b_ref):
        u = u_ref[...]
        w = w_ref[...]
        row = lax.broadcasted_iota(jnp.int32, u.shape, 0)
        prev = jnp.where(row == 0, 0.0, pltpu.roll(u, 1, axis=0))
        nxt = jnp.where(row == n_len - 1, 0.0, pltpu.roll(u, n_len - 1, axis=0))
        return w[0:1] * prev + w[1:2] * u + w[2:3] * nxt + b_ref[...]

    x0_ref[...] = sconv(u0_ref, w0_ref, b0_ref)
    vx = sconv(u2_ref, w2_ref, b2_ref) * sconv(u1_ref, w1_ref, b1_ref)
    vx_ref[...] = vx
    vxb_ref[...] = vx.astype(BF16)


def _hy_pre(hy, conv_w, conv_b, n_len):
    nblk = HY_WIDTH // LANES
    uspec = lambda part: pl.BlockSpec((None, n_len, LANES), lambda b, j: (b, 0, part * nblk + j))
    wspec = lambda part: pl.BlockSpec((3, LANES), lambda b, j: (0, part * nblk + j))
    bspec = lambda part: pl.BlockSpec((1, LANES), lambda b, j: (0, part * nblk + j))
    ospec = pl.BlockSpec((None, n_len, LANES), lambda b, j: (b, 0, j))
    hy3 = hy.reshape(BATCH, n_len, 3 * HY_WIDTH)
    cb = conv_b.reshape(1, -1)
    return pl.pallas_call(
        functools.partial(_hy_pre_kernel, n_len=n_len),
        out_shape=(jax.ShapeDtypeStruct((BATCH, n_len, HY_WIDTH), F32),
                   jax.ShapeDtypeStruct((BATCH, n_len, HY_WIDTH), F32),
                   jax.ShapeDtypeStruct((BATCH, n_len, HY_WIDTH), BF16)),
        grid=(BATCH, nblk),
        in_specs=[uspec(0), uspec(1), uspec(2), wspec(0), wspec(1), wspec(2), bspec(0), bspec(1), bspec(2)],
        out_specs=(ospec, ospec, ospec),
        compiler_params=_cparams("parallel", "parallel"),
        name="hyena_short_conv",
    )(hy3, hy3, hy3, conv_w, conv_w, conv_w, cb, cb, cb)


def _hy_fwd_kernel(cm_ref, sf_ref, u_ref, a_ref, b_ref, d_ref, q_ref):
    u = u_ref[...]
    ur = _dot(cm_ref[...], u)
    g = _dot(sf_ref[...], u)
    b = b_ref[...]
    q_ref[0] = (a_ref[...] * ur + b * g).astype(BF16)
    q_ref[1] = (d_ref[...] * g - b * ur).astype(BF16)


def _hy_inv_kernel(cm_ref, si_ref, q_ref, x0_ref, vx_ref, skip_ref, o_ref):
    y = _dot(cm_ref[...], q_ref[0]) + _dot(si_ref[...], q_ref[1])
    o_ref[...] = (x0_ref[...] * (y + vx_ref[...] * skip_ref[...])).astype(o_ref.dtype)


def _hy_long_conv(x0, vx, vxb, spectra, skip, n_len, tabs):
    cm, sf, si = tabs["dft"]
    a, b, d = spectra
    tf = min(1024, n_len)
    nf = n_len // tf
    mat = pl.BlockSpec((tf, n_len), lambda bb, f: (f, 0))
    spec = pl.BlockSpec((tf, HY_WIDTH), lambda bb, f: (f, 0))
    q = pl.pallas_call(
        _hy_fwd_kernel,
        out_shape=jax.ShapeDtypeStruct((BATCH, 2, n_len, HY_WIDTH), BF16),
        grid=(BATCH, nf),
        in_specs=[mat, mat, pl.BlockSpec((None, n_len, HY_WIDTH), lambda bb, f: (bb, 0, 0)), spec, spec, spec],
        out_specs=pl.BlockSpec((None, 2, tf, HY_WIDTH), lambda bb, f: (bb, 0, f, 0)),
        compiler_params=_cparams("parallel", "arbitrary"),
        name="hyena_dft_fwd",
    )(cm, sf, vxb, a, b, d)
    tile = pl.BlockSpec((None, tf, HY_WIDTH), lambda bb, f: (bb, f, 0))
    return pl.pallas_call(
        _hy_inv_kernel,
        out_shape=jax.ShapeDtypeStruct((BATCH, n_len, HY_WIDTH), BF16),
        grid=(BATCH, nf),
        in_specs=[mat, mat, pl.BlockSpec((None, 2, n_len, HY_WIDTH), lambda bb, f: (bb, 0, 0, 0)),
                  tile, tile, pl.BlockSpec((1, HY_WIDTH), lambda bb, f: (0, 0))],
        out_specs=tile,
        compiler_params=_cparams("parallel", "arbitrary"),
        name="hyena_dft_inv",
    )(cm, si, q, x0, vx, skip.reshape(1, -1))


_SWA_Q = SWA_HEADS * SWA_HEAD_DIM
_SWA_KV = SWA_KV_HEADS * SWA_HEAD_DIM


def _swaproj_kernel(h_ref, w_ref, cos_ref, sin_ref, q_ref, k_ref, v_ref, *, rope):
    t = _dot(h_ref[...], w_ref[...])
    q = t[:, :_SWA_Q]
    k = t[:, 2 * _SWA_Q:2 * _SWA_Q + _SWA_KV]
    v = t[:, 2 * _SWA_Q + 2 * _SWA_KV:]
    if rope:
        cos = cos_ref[...]
        sin = sin_ref[...]
        q = q * cos + t[:, _SWA_Q:2 * _SWA_Q] * sin
        k = k * cos[:, :_SWA_KV] + t[:, 2 * _SWA_Q + _SWA_KV:2 * _SWA_Q + 2 * _SWA_KV] * sin[:, :_SWA_KV]
    q_ref[...] = (q * (SWA_HEAD_DIM ** -0.5 * math.log2(math.e))).astype(BF16)
    lo = lax.broadcasted_iota(jnp.int32, k.shape, 1) < SWA_HEAD_DIM
    for src, dst in ((k, k_ref), (v, v_ref)):
        sw = pltpu.roll(src, SWA_HEAD_DIM, axis=1)
        dst[:, 0 * LANES:1 * LANES] = jnp.where(lo, src, 0.0).astype(BF16)
        dst[:, 1 * LANES:2 * LANES] = jnp.where(lo, 0.0, sw).astype(BF16)
        dst[:, 2 * LANES:3 * LANES] = jnp.where(lo, sw, 0.0).astype(BF16)
        dst[:, 3 * LANES:4 * LANES] = jnp.where(lo, 0.0, src).astype(BF16)


def _hy_swa_kernel(h_ref, why_ref, w_ref, cos_ref, sin_ref, hy_ref, q_ref, k_ref, v_ref, *, rope):
    hy_ref[...] = _dot(h_ref[...], why_ref[...])
    _swaproj_kernel(h_ref, w_ref, cos_ref, sin_ref, q_ref, k_ref, v_ref, rope=rope)


def _swa_proj(h, w, cos_t, sin_t, rope, w_hy=None, tm=512):
    m = h.shape[0]
    nt = cos_t.shape[0] // tm
    o = jax.ShapeDtypeStruct((m, 4 * LANES), BF16)
    ospec = pl.BlockSpec((tm, 4 * LANES), lambda i: (i, 0))
    tspec = pl.BlockSpec((tm, _SWA_Q), lambda i: (i % nt, 0))
    hspec = pl.BlockSpec((tm, D_MODEL), lambda i: (i, 0))
    if w_hy is None:
        return pl.pallas_call(
            functools.partial(_swaproj_kernel, rope=rope),
            out_shape=(o, o, o),
            grid=(m // tm,),
            in_specs=[hspec, _const_spec(w.shape), tspec, tspec],
            out_specs=(ospec, ospec, ospec),
            compiler_params=_cparams("parallel"),
            name="swa_proj",
        )(h, w, cos_t, sin_t)
    n_hy = w_hy.shape[1]
    return pl.pallas_call(
        functools.partial(_hy_swa_kernel, rope=rope),
        out_shape=(jax.ShapeDtypeStruct((m, n_hy), F32), o, o, o),
        grid=(m // tm,),
        in_specs=[hspec, _const_spec(w_hy.shape), _const_spec(w.shape), tspec, tspec],
        out_specs=(pl.BlockSpec((tm, n_hy), lambda i: (i, 0)), ospec, ospec, ospec),
        compiler_params=_cparams("parallel"),
        name="hyena_swa_proj",
    )(h, w_hy, w, cos_t, sin_t)


def _swa_kernel(sink_ref, q_ref, *refs, tq, win, lk, has_lat):
    if has_lat:
        kl_ref, vl_ref, kc_ref, vc_ref, o_ref = refs
        t0 = pl.program_id(1) * tq
        start = pl.multiple_of(jnp.clip(t0 - SWA_WINDOW, 0, lk - win), SWA_WINDOW)
        qpos = t0 + lax.broadcasted_iota(jnp.int32, (tq, win), 0)
        kpos = start + lax.broadcasted_iota(jnp.int32, (tq, win), 1)
        valid = jnp.abs(qpos - kpos) <= SWA_WINDOW
        valid = jnp.concatenate([valid, valid], axis=0)
    else:
        kc_ref, vc_ref, o_ref = refs
    upper = lax.broadcasted_iota(jnp.int32, (2 * tq, 1), 0) < tq
    for g in range(SWA_KV_HEADS):
        j0 = 2 * g
        qb = jnp.concatenate([q_ref[:, j0 * LANES:(j0 + 1) * LANES],
                              q_ref[:, (j0 + 1) * LANES:(j0 + 2) * LANES]], axis=0)
        acc = None
        for par in range(2):
            c0 = (2 * g + par) * LANES
            sk = jnp.where(upper, sink_ref[2 * j0 + par], sink_ref[2 * j0 + 2 + par]) * math.log2(math.e)
            s_c = _dot_nt(qb, kc_ref[:, c0:c0 + LANES])
            m = jnp.maximum(s_c.max(axis=-1, keepdims=True), sk)
            if has_lat:
                s_l = _dot_nt(qb, kl_ref[pl.ds(start, win), c0:c0 + LANES])
                s_l = jnp.where(valid, s_l, NEG_BIG)
                m = jnp.maximum(m, s_l.max(axis=-1, keepdims=True))
            p_c = jnp.exp2(s_c - m)
            den = p_c.sum(axis=-1, keepdims=True) + jnp.exp2(sk - m)
            if has_lat:
                p_l = jnp.exp2(s_l - m)
                den = den + p_l.sum(axis=-1, keepdims=True)
            o = _dot(p_c.astype(BF16), vc_ref[:, c0:c0 + LANES])
            if has_lat:
                o = o + _dot(p_l.astype(BF16), vl_ref[pl.ds(start, win), c0:c0 + LANES])
            o = o * (1.0 / den)
            acc = o if acc is None else acc + o
        o_ref[:, j0 * LANES:(j0 + 1) * LANES] = acc[:tq].astype(o_ref.dtype)
        o_ref[:, (j0 + 1) * LANES:(j0 + 2) * LANES] = acc[tq:].astype(o_ref.dtype)


def _swa_attention(sink, q, lat, ctx, lq, tq=256):
    nq = lq // tq
    win = tq + 2 * SWA_WINDOW
    full = lambda n: pl.BlockSpec((n, 4 * LANES), lambda b, i: (b, 0))
    in_specs = [pl.BlockSpec(memory_space=pltpu.SMEM),
                pl.BlockSpec((tq, 4 * LANES), lambda b, i: (b * nq + i, 0))]
    args = [sink, q]
    if lat is not None:
        in_specs += [full(SEQ), full(SEQ)]
        args += list(lat)
    in_specs += [full(CTX_LEN), full(CTX_LEN)]
    args += list(ctx)
    return pl.pallas_call(
        functools.partial(_swa_kernel, tq=tq, win=win, lk=SEQ, has_lat=lat is not None),
        out_shape=jax.ShapeDtypeStruct((BATCH * lq, 4 * LANES), BF16),
        grid=(BATCH, nq),
        in_specs=in_specs,
        out_specs=pl.BlockSpec((tq, 4 * LANES), lambda b, i: (b * nq + i, 0)),
        compiler_params=_cparams("parallel", "arbitrary"),
        name="swa_attention",
    )(*args)


def _out_kernel(a_ref, y_ref, s_ref, w_ref, x_ref, m_ref, g_ref, xo_ref, ho_ref):
    na = MLA_HEADS * MLA_V
    acc = _dot(a_ref[...], w_ref[0:na, :])
    acc = acc + _dot(y_ref[...], w_ref[na:na + HY_WIDTH, :])
    acc = acc + _dot(s_ref[...], w_ref[na + HY_WIDTH:, :])
    x = x_ref[...] + m_ref[2:3, :] * acc
    xo_ref[...] = x
    ho_ref[...] = (_rms(x, g_ref[...]) * (1.0 + m_ref[4:5, :]) + m_ref[3:4, :]).astype(ho_ref.dtype)


def _out_proj(a, y, s, w_out, x, mods, g_ffn, groups, row0, h_dtype, tm=512):
    m = x.shape[0]
    nr = m // groups // tm
    tile = lambda n: pl.BlockSpec((tm, n), lambda g_, i: (g_ * nr + i, 0))
    return pl.pallas_call(
        _out_kernel,
        out_shape=(jax.ShapeDtypeStruct((m, D_MODEL), F32), jax.ShapeDtypeStruct((m, D_MODEL), h_dtype)),
        grid=(groups, nr),
        in_specs=[tile(a.shape[1]), tile(y.shape[1]), tile(s.shape[1]), _const_spec(w_out.shape),
                  tile(D_MODEL), pl.BlockSpec((None, 6, D_MODEL), lambda g_, i: (row0 + g_, 0, 0)),
                  pl.BlockSpec((1, D_MODEL), lambda g_, i: (0, 0))],
        out_specs=(tile(D_MODEL), tile(D_MODEL)),
        compiler_params=_cparams("parallel", "parallel"),
        name="out_proj",
    )(a, y, s, w_out, x, mods, g_ffn.reshape(1, -1))


_META_W1, _META_W2, _META_E1, _META_E2, _META_R1, _META_R2 = range(6)


def _route(h, is_first, w_ref, meta_ref, cnt_ref, carry_ref):
    @pl.when(is_first)
    def _():
        carry_ref[...] = jnp.zeros_like(carry_ref)

    logits = _dot3(h, w_ref[...])
    tm = logits.shape[0]
    lane = lax.broadcasted_iota(jnp.int32, logits.shape, 1).astype(F32)
    logits = jnp.where(lane < N_EXPERTS, logits, NEG_BIG)
    m1 = logits.max(axis=-1, keepdims=True)
    i1 = jnp.where(logits == m1, lane, float(LANES)).min(axis=-1, keepdims=True)
    rest = jnp.where(lane == i1, NEG_BIG, logits)
    m2 = rest.max(axis=-1, keepdims=True)
    i2 = jnp.where(rest == m2, lane, float(LANES)).min(axis=-1, keepdims=True)
    e2 = jnp.exp(m2 - m1)
    w1 = 1.0 / (1.0 + e2)
    hot = jnp.where((lane == i1) | (lane == i2), 1.0, 0.0)
    before = (lax.broadcasted_iota(jnp.int32, (tm, tm), 0) > lax.broadcasted_iota(jnp.int32, (tm, tm), 1))
    seen = _dot(before.astype(F32).astype(BF16), hot.astype(BF16)) + carry_ref[0:1, :]
    r1 = jnp.where(lane == i1, seen, 0.0).sum(axis=-1, keepdims=True)
    r2 = jnp.where(lane == i2, seen, 0.0).sum(axis=-1, keepdims=True)
    carry_ref[...] = carry_ref[...] + hot.sum(axis=0, keepdims=True)
    cnt_ref[...] = carry_ref[...]
    rec = jnp.zeros_like(logits)
    for k, v in ((_META_W1, w1), (_META_W2, e2 * w1), (_META_E1, i1), (_META_E2, i2), (_META_R1, r1), (_META_R2, r2)):
        rec = jnp.where(lane == float(k), v, rec)
    meta_ref[...] = rec


def _router_kernel(h_ref, w_ref, meta_ref, cnt_ref, carry_ref):
    _route(h_ref[...], pl.program_id(0) == 0, w_ref, meta_ref, cnt_ref, carry_ref)


def _router(h, w_router, tm=512):
    m = h.shape[0]
    wp = jnp.zeros((D_MODEL, LANES), F32).at[:, :N_EXPERTS].set(w_router)
    return pl.pallas_call(
        _router_kernel,
        out_shape=(jax.ShapeDtypeStruct((m, LANES), F32), jax.ShapeDtypeStruct((8, LANES), F32)),
        grid=(m // tm,),
        in_specs=[pl.BlockSpec((tm, D_MODEL), lambda i: (i, 0)),
                  pl.BlockSpec((D_MODEL, LANES), lambda i: (0, 0))],
        out_specs=(pl.BlockSpec((tm, LANES), lambda i: (i, 0)), pl.BlockSpec((8, LANES), lambda i: (0, 0))),
        scratch_shapes=[pltpu.VMEM((8, LANES), F32)],
        compiler_params=_cparams("arbitrary"),
        name="moe_router",
    )(h, wp)


MOE_TM = 1024
MOE_BLK = 128
MOE_ROWS = (MOE_TM, 768, 512, 256)


def _moe_plan(meta, counts, m):
    n_steps = -(-2 * m // MOE_TM) + N_EXPERTS
    cnt = counts[0, :N_EXPERTS].astype(jnp.int32)
    steps_e = (cnt + MOE_TM - 1) // MOE_TM
    ends = jnp.cumsum(steps_e)
    first = ends - steps_e
    total = ends[-1]
    e = meta[:, _META_E1:_META_E2 + 1].astype(jnp.int32)
    rank = meta[:, _META_R1:_META_R2 + 1].astype(jnp.int32)
    pos = (first * MOE_TM)[e] + rank
    s_idx = jnp.arange(n_steps, dtype=jnp.int32)
    step_e = jnp.minimum(jnp.searchsorted(ends, s_idx, side="right").astype(jnp.int32), N_EXPERTS - 1)
    valid = jnp.clip(cnt[step_e] - (s_idx - first[step_e]) * MOE_TM, 0, MOE_TM)
    valid = jnp.where(s_idx < total, valid, 0)
    return pos.reshape(-1), step_e, valid, n_steps


def _row_copy(src_ref, src_row, dst_ref, dst_row, sem):
    return pltpu.make_async_copy(src_ref.at[pl.ds(src_row, 1)], dst_ref.at[pl.ds(dst_row, 1)], sem)


def _invert_kernel(pos_ref, sv_ref, src_ref):
    def clear_step(s, c):
        def clear(i, c2):
            src_ref[s * MOE_TM + i] = 0
            return c2

        return lax.fori_loop(sv_ref[s], MOE_TM, clear, c)

    def place(t, c):
        p0 = pos_ref[2 * t]
        p1 = pos_ref[2 * t + 1]
        src_ref[p0] = t
        src_ref[p1] = t
        return c

    lax.fori_loop(0, sv_ref.shape[0], clear_step, 0)
    lax.fori_loop(0, pos_ref.shape[0] // 2, place, 0, unroll=8)


def _moe_invert(pos, valid, n_rows):
    smem = pl.BlockSpec(memory_space=pltpu.SMEM)
    return pl.pallas_call(
        _invert_kernel,
        out_shape=jax.ShapeDtypeStruct((n_rows,), jnp.int32),
        in_specs=[smem, smem], out_specs=smem,
        name="moe_invert",
    )(pos, valid)


def _block_wait(src_ref, dst_ref, rows, sem):
    pltpu.make_async_copy(src_ref.at[pl.ds(0, rows)], dst_ref.at[pl.ds(0, rows)], sem).wait()


def _moe_ffn_kernel(se_ref, sv_ref, src_ref, h_ref, wg_ref, wu_ref, wd_ref, o_ref, x32_ref, xb_ref, sem):
    del se_ref
    s = pl.program_id(0)
    f = pl.program_id(1)
    valid = sv_ref[s]

    def rows_used(v):
        r = jnp.where(v > 0, MOE_ROWS[-1], 0)
        for small, big in zip(MOE_ROWS[:0:-1], MOE_ROWS[-2::-1]):
            r = jnp.where(v > small, big, r)
        return r

    def blocks(v):
        return rows_used(v) // MOE_BLK

    def gather(step):
        base = step * MOE_TM

        def issue(b, c):
            for k in range(8):
                i = b * 8 + k
                _row_copy(h_ref, src_ref[base + i], x32_ref, i, sem).start()
            return c

        lax.fori_loop(0, rows_used(sv_ref[step]) // 8, issue, 0)

    @pl.when(f == 0)
    def _():
        @pl.when(s == 0)
        def _():
            gather(0)

        def land(i, c):
            _block_wait(h_ref, x32_ref, MOE_BLK, sem)
            return c

        def to_bf16(i, c):
            r0 = pl.multiple_of(i * MOE_BLK, MOE_BLK)
            xb_ref[pl.ds(r0, MOE_BLK), :] = x32_ref[pl.ds(r0, MOE_BLK), :].astype(BF16)
            return c

        lax.fori_loop(0, blocks(valid), land, 0)
        lax.fori_loop(0, blocks(valid), to_bf16, 0)

        @pl.when(s + 1 < pl.num_programs(0))
        def _():
            gather(s + 1)

        o_ref[...] = jnp.zeros_like(o_ref)

    def compute(rows):
        xs = xb_ref[0:rows, :]
        mid = _silu(_dot(xs, wg_ref[...].astype(BF16))) * _dot(xs, wu_ref[...].astype(BF16))
        o_ref[0:rows, :] += _dot(mid.astype(BF16), wd_ref[...].astype(BF16))

    for k, rows in enumerate(MOE_ROWS):
        lower = MOE_ROWS[k + 1] if k + 1 < len(MOE_ROWS) else 0

        @pl.when((valid > lower) & (valid <= rows))
        def _():
            compute(rows)


def _moe_ffn(h, src, w_gate, w_up, w_down, step_e, valid, n_steps, tf=512):
    assert all(r % MOE_BLK == 0 for r in MOE_ROWS) and MOE_ROWS[0] == MOE_TM
    n_ff = w_gate.shape[-1]
    nf = n_ff // tf
    fidx = lambda s, f, sv: jnp.where(sv[s] > 0, f, nf - 1)
    return pl.pallas_call(
        _moe_ffn_kernel,
        out_shape=jax.ShapeDtypeStruct((n_steps * MOE_TM, D_MODEL), F32),
        grid_spec=pltpu.PrefetchScalarGridSpec(
            num_scalar_prefetch=3, grid=(n_steps, nf),
            in_specs=[
                pl.BlockSpec(memory_space=pl.ANY),
                pl.BlockSpec((None, D_MODEL, tf), lambda s, f, se, sv, sr: (se[s], 0, fidx(s, f, sv))),
                pl.BlockSpec((None, D_MODEL, tf), lambda s, f, se, sv, sr: (se[s], 0, fidx(s, f, sv))),
                pl.BlockSpec((None, tf, D_MODEL), lambda s, f, se, sv, sr: (se[s], fidx(s, f, sv), 0)),
            ],
            out_specs=pl.BlockSpec((MOE_TM, D_MODEL), lambda s, f, se, sv, sr: (s, 0), pipeline_mode=pl.Buffered(1)),
            scratch_shapes=[pltpu.VMEM((MOE_TM, D_MODEL), F32), pltpu.VMEM((MOE_TM, D_MODEL), BF16),
                            pltpu.SemaphoreType.DMA(())]),
        compiler_params=_cparams("arbitrary", "arbitrary"),
        name="moe_experts",
    )(step_e, valid, src, h, w_gate, w_up, w_down)


def _combine_kernel(pos_ref, meta_ref, x_ref, m_ref, gf_ref, ys_ref, o_ref, buf_ref, sems, *, tm, final):
    t = pl.program_id(0)

    def request(tile):
        b = tile % 2

        def issue(i, c):
            for slot in range(2):
                _row_copy(ys_ref, pos_ref[2 * (tile * tm + i) + slot], buf_ref.at[b, slot], i, sems.at[b]).start()
            return c

        lax.fori_loop(0, tm, issue, 0, unroll=4)

    @pl.when(t == 0)
    def _():
        request(0)

    @pl.when(t + 1 < pl.num_programs(0))
    def _():
        request(t + 1)

    cur = t % 2
    for slot in range(2):
        _block_wait(ys_ref, buf_ref.at[cur, slot], tm, sems.at[cur])
    meta = meta_ref[...]
    y = meta[:, _META_W1:_META_W1 + 1] * buf_ref[cur, 0] + meta[:, _META_W2:_META_W2 + 1] * buf_ref[cur, 1]
    x = x_ref[...] + m_ref[5:6, :] * y
    if final:
        x = _rms(x, gf_ref[...])
    o_ref[...] = x


def _moe_combine(pos, meta, x, mods, ys, groups, row0, g_final, tm=512):
    m = x.shape[0]
    nr = m // groups // tm
    final = g_final is not None
    gf = (g_final if final else jnp.ones((D_MODEL,), F32)).reshape(1, D_MODEL)
    return pl.pallas_call(
        functools.partial(_combine_kernel, tm=tm, final=final),
        out_shape=jax.ShapeDtypeStruct((m, D_MODEL), F32),
        grid_spec=pltpu.PrefetchScalarGridSpec(
            num_scalar_prefetch=1, grid=(m // tm,),
            in_specs=[pl.BlockSpec((tm, LANES), lambda i, p: (i, 0)),
                      pl.BlockSpec((tm, D_MODEL), lambda i, p: (i, 0)),
                      pl.BlockSpec((None, 6, D_MODEL), lambda i, p: (row0 + i // nr, 0, 0)),
                      pl.BlockSpec((1, D_MODEL), lambda i, p: (0, 0)),
                      pl.BlockSpec(memory_space=pl.ANY)],
            out_specs=pl.BlockSpec((tm, D_MODEL), lambda i, p: (i, 0)),
            scratch_shapes=[pltpu.VMEM((2, 2, tm, D_MODEL), F32), pltpu.SemaphoreType.DMA((2,))]),
        compiler_params=_cparams("arbitrary"),
        name="moe_combine",
    )(pos, meta, x, mods, gf, ys)


def _moe(h, meta, counts, x, mods, groups, row0, w_gate, w_up, w_down, g_final):
    m = h.shape[0]
    pos, step_e, valid, n_steps = _moe_plan(meta, counts, m)
    src = _moe_invert(pos, valid, n_steps * MOE_TM)
    ys = _moe_ffn(h, src, w_gate, w_up, w_down, step_e, valid, n_steps)
    return _moe_combine(pos, meta, x, mods, ys, groups, row0, g_final)


def _ffn_kernel(h_ref, wg_ref, wu_ref, wd_ref, x_ref, m_ref, gn_ref, mn_ref, o_ref, *maybe_hn_ref):
    f = pl.program_id(2)

    @pl.when(f == 0)
    def _():
        o_ref[...] = jnp.zeros_like(o_ref)

    h = h_ref[...]
    mid = _silu(_dot(h, wg_ref[...])) * _dot(h, wu_ref[...])
    o_ref[...] += _dot(mid.astype(BF16), wd_ref[...])

    @pl.when(f == pl.num_programs(2) - 1)
    def _():
        x = x_ref[...] + m_ref[5:6, :] * o_ref[...]
        o_ref[...] = x
        for hn_ref in maybe_hn_ref:
            hn_ref[...] = (_rms(x, gn_ref[...]) * (1.0 + mn_ref[1:2, :]) + mn_ref[0:1, :]).astype(hn_ref.dtype)


def _ffn(h, w_gate, w_up, w_down, x, mods, groups, row0, nxt, tm, tf=512):
    m = x.shape[0]
    n_ff = w_gate.shape[1]
    nr = m // groups // tm
    g_next, mods_next = nxt if nxt is not None else (jnp.ones((D_MODEL,), F32), mods)
    once = pl.Buffered(1)
    tile = lambda n, mode=None: pl.BlockSpec((tm, n), lambda g_, i, f: (g_ * nr + i, 0), pipeline_mode=mode)
    mod_spec = pl.BlockSpec((None, 6, D_MODEL), lambda g_, i, f: (row0 + g_, 0, 0))
    out_shape = [jax.ShapeDtypeStruct((m, D_MODEL), F32)]
    out_specs = [tile(D_MODEL, once)]
    if nxt is not None:
        out_shape.append(jax.ShapeDtypeStruct((m, D_MODEL), BF16))
        out_specs.append(tile(D_MODEL, once))
    outs = pl.pallas_call(
        _ffn_kernel,
        out_shape=tuple(out_shape),
        grid=(groups, nr, n_ff // tf),
        in_specs=[tile(D_MODEL),
                  pl.BlockSpec((D_MODEL, tf), lambda g_, i, f: (0, f)),
                  pl.BlockSpec((D_MODEL, tf), lambda g_, i, f: (0, f)),
                  pl.BlockSpec((tf, D_MODEL), lambda g_, i, f: (f, 0)),
                  tile(D_MODEL, once), mod_spec,
                  pl.BlockSpec((1, D_MODEL), lambda g_, i, f: (0, 0)), mod_spec],
        out_specs=tuple(out_specs),
        compiler_params=_cparams("parallel", "parallel", "arbitrary"),
        name="swiglu_ffn",
    )(h, w_gate, w_up, w_down, x, mods, g_next.reshape(1, D_MODEL), mods_next)
    return outs if nxt is not None else (outs[0], None)


def _swap_halves(w):
    half = w.shape[-1] // 2
    return jnp.concatenate([w[..., half:], w[..., :half]], axis=-1)


def _swap_heads(w, heads, dim):
    k = w.shape[0]
    return _swap_halves(w.reshape(k, heads, dim)).reshape(k, heads * dim)


def _rope_tables():
    rows = SEQ // GRID_W
    row = np.repeat(np.arange(rows), GRID_W).astype(np.float32)
    col = np.tile(np.arange(GRID_W), rows).astype(np.float32)
    quarter = MLA_ROPE // 4
    freqs = (np.float32(ROPE_THETA) ** (-np.arange(quarter, dtype=np.float32) / quarter)).astype(np.float32)
    ang = np.concatenate([row[:, None] * freqs[None], col[:, None] * freqs[None]], axis=-1)
    cos = np.cos(ang.astype(np.float64))
    sin = np.sin(ang.astype(np.float64))
    cos2 = np.concatenate([cos, cos], axis=-1)
    sin2 = np.concatenate([-sin, sin], axis=-1)
    tab_lat = np.concatenate([cos2, sin2], axis=-1)
    tab_ctx = np.concatenate([np.ones((CTX_LEN, MLA_ROPE)), np.zeros((CTX_LEN, MLA_ROPE))], axis=-1)
    return dict(
        tab_lat=jnp.asarray(tab_lat, F32), tab_ctx=jnp.asarray(tab_ctx, F32),
        cos8=jnp.asarray(np.tile(cos2, (1, SWA_HEADS)), F32),
        sin8=jnp.asarray(np.tile(sin2, (1, SWA_HEADS)), F32),
        ones8=jnp.ones((CTX_LEN, _SWA_Q), F32), zeros8=jnp.zeros((CTX_LEN, _SWA_Q), F32),
    )


def _layer_weights(p, l):
    w_in = p["w_in"][l]
    cq = w_in[:, _O_CQ:_O_CKV]
    ckv = w_in[:, _O_CKV:_O_KPE]
    kpe = w_in[:, _O_KPE:_O_HY]
    hy = w_in[:, _O_HY:_O_SQ]
    sq = w_in[:, _O_SQ:_O_SK]
    sk = w_in[:, _O_SK:_O_SV]
    sv = w_in[:, _O_SV:]
    dq = MLA_NOPE + MLA_ROPE
    wq = p["w_q_up"][l].reshape(MLA_Q_LORA, MLA_HEADS, dq)
    wq = jnp.concatenate([wq, _swap_halves(wq[..., MLA_NOPE:])], axis=-1)
    wkv = p["w_kv_up"][l].reshape(MLA_KV_LORA, MLA_HEADS, MLA_NOPE + MLA_V)
    return dict(
        w_cq=cq.astype(BF16),
        w_ckv=jnp.concatenate([ckv, kpe, _swap_halves(kpe)], axis=-1).astype(BF16),
        w_hy=hy.astype(BF16),
        w_swa=jnp.concatenate([sq, _swap_heads(sq, SWA_HEADS, SWA_HEAD_DIM), sk,
                               _swap_heads(sk, SWA_KV_HEADS, SWA_HEAD_DIM), sv], axis=-1).astype(BF16),
        w_q=jnp.transpose(wq, (1, 0, 2)).astype(BF16),
        w_kv=jnp.transpose(wkv, (1, 0, 2)).astype(BF16),
        w_out=p["w_out"][l].astype(BF16),
        g_q=p["g_q"][l], g_kv=p["g_kv"][l],
        hy_conv_w=p["hy_conv_w"][l], hy_conv_b=p["hy_conv_b"][l],
        hy_w1=p["hy_w1"][l], hy_b1=p["hy_b1"][l], hy_w2=p["hy_w2"][l], hy_b2=p["hy_b2"][l],
        hy_w3=p["hy_w3"][l], hy_b3=p["hy_b3"][l], hy_w_filt=p["hy_w_filt"][l],
        hy_freq=p["hy_freq"][l], hy_skip=p["hy_skip"][l], swa_sink=p["swa_sink"][l],
    )


def _hyena(hy, lw, n_len, tabs):
    x0, vx, vxb = _hy_pre(hy, lw["hy_conv_w"], lw["hy_conv_b"], n_len)
    spectra = _hy_filters(lw, n_len, tabs)
    y = _hy_long_conv(x0, vx, vxb, spectra, lw["hy_skip"], n_len, tabs)
    return y.reshape(BATCH * n_len, HY_WIDTH)


def _mixer(h_lat, h_ctx, lw, rope, hy_tabs, need_ctx):
    q_l, k_l, v_l = _mla_proj(h_lat, lw, rope["tab_lat"])
    if need_ctx:
        q_c, k_c, v_c = _mla_proj(h_ctx, lw, rope["tab_ctx"], tm=CTX_LEN)
    else:
        k_c, v_c = _kv_proj(h_ctx, lw["w_ckv"], lw["g_kv"], lw["w_kv"], rope["tab_ctx"], tm=CTX_LEN)
    a_l = _mla_attention(q_l, [(k_l, v_l, SEQ), (k_c, v_c, CTX_LEN)], SEQ)
    hy_l, sq_l, kk_l, vv_l = _swa_proj(h_lat, lw["w_swa"], rope["cos8"], rope["sin8"], True, w_hy=lw["w_hy"])
    y_l = _hyena(hy_l, lw, SEQ, hy_tabs[SEQ])
    if need_ctx:
        hy_c, sq_c, kk_c, vv_c = _swa_proj(h_ctx, lw["w_swa"], rope["ones8"], rope["zeros8"], False,
                                           w_hy=lw["w_hy"], tm=CTX_LEN)
    else:
        sq_c, kk_c, vv_c = _swa_proj(h_ctx, lw["w_swa"], rope["ones8"], rope["zeros8"], False, tm=CTX_LEN)
    s_l = _swa_attention(lw["swa_sink"], sq_l, (kk_l, vv_l), (kk_c, vv_c), SEQ)
    if not need_ctx:
        return (a_l, y_l, s_l), None
    a_c = _mla_attention(q_c, [(k_c, v_c, CTX_LEN)], CTX_LEN, tq=CTX_LEN)
    y_c = _hyena(hy_c, lw, CTX_LEN, hy_tabs[CTX_LEN])
    s_c = _swa_attention(lw["swa_sink"], sq_c, None, (kk_c, vv_c), CTX_LEN, tq=CTX_LEN)
    return (a_l, y_l, s_l), (a_c, y_c, s_c)


def _forward(p):
    rope = _rope_tables()
    hy_tabs = {n: dict(dft=_dft_tables(n), filt=_filter_tables(n)) for n in (SEQ, CTX_LEN)}
    cvec = jnp.zeros((MOD_ROWS, D_MODEL), F32).at[:BATCH].set(p["c"]).at[BATCH].set(p["c_ctx"])
    mods_all = _modulation(cvec, p["w_mod"], p["b_mod"]).reshape(DEPTH, MOD_ROWS, 6, D_MODEL)

    x_lat = p["x"].reshape(BATCH * SEQ, D_MODEL)
    x_ctx = p["ctx"].reshape(BATCH * CTX_LEN, D_MODEL)
    h_in = [None, None]
    for l in range(DEPTH):
        last = l == DEPTH - 1
        mods = mods_all[l]
        lw = _layer_weights(p, l)
        h_lat = h_in[0] if h_in[0] is not None else _norm_mod(x_lat, p["g_mix"][l], mods, BATCH, 0, 0, BF16)
        h_ctx = h_in[1] if h_in[1] is not None else _norm_mod(x_ctx, p["g_mix"][l], mods, 1, BATCH, 0, BF16)
        mix_l, mix_c = _mixer(h_lat, h_ctx, lw, rope, hy_tabs, not last)
        streams = [(x_lat, mix_l, BATCH, 0, 1024)]
        if not last:
            streams.append((x_ctx, mix_c, 1, BATCH, 512))
        outs = []
        h_in = [None, None]
        i = l // 2
        dense = l % 2 == 0
        for n, (x, mix, groups, row0, ffn_tm) in enumerate(streams):
            if dense:
                x, h2 = _out_proj(*mix, lw["w_out"], x, mods, p["g_ffn"][l], groups, row0, BF16)
                nxt = None if last else (p["g_mix"][l + 1], mods_all[l + 1])
                x, h_in[n] = _ffn(h2, p["ffn_w_gate"][i].astype(BF16), p["ffn_w_up"][i].astype(BF16),
                                  p["ffn_w_down"][i].astype(BF16), x, mods, groups, row0, nxt, ffn_tm)
            else:
                x, h2 = _out_proj(*mix, lw["w_out"], x, mods, p["g_ffn"][l], groups, row0, F32)
                meta, counts = _router(h2, p["moe_router"][i])
                g_final = p["g_final"] if (last and n == 0) else None
                x = _moe(h2, meta, counts, x, mods, groups, row0, p["moe_w_gate"][i], p["moe_w_up"][i],
                         p["moe_w_down"][i], g_final)
            outs.append(x)
        x_lat = outs[0]
        if not last:
            x_ctx = outs[1]
    if DEPTH % 2 == 1:
        x_lat = _norm_mod(x_lat, p["g_final"], mods_all[0], BATCH, 0, None, F32)
    return x_lat.reshape(BATCH, SEQ, D_MODEL)


def kernel(x, c, ctx, c_ctx, w_mod, b_mod, g_mix, g_ffn, w_in, g_q, w_q_up, g_kv, w_kv_up, hy_conv_w, hy_conv_b, hy_w1, hy_b1, hy_w2, hy_b2, hy_w3, hy_b3, hy_w_filt, hy_freq, hy_skip, swa_sink, w_out, ffn_w_gate, ffn_w_up, ffn_w_down, moe_router, moe_w_gate, moe_w_up, moe_w_down, g_final):
    return _forward(dict(
        x=x, c=c, ctx=ctx, c_ctx=c_ctx, w_mod=w_mod, b_mod=b_mod, g_mix=g_mix, g_ffn=g_ffn, w_in=w_in,
        g_q=g_q, w_q_up=w_q_up, g_kv=g_kv, w_kv_up=w_kv_up, hy_conv_w=hy_conv_w, hy_conv_b=hy_conv_b,
        hy_w1=hy_w1, hy_b1=hy_b1, hy_w2=hy_w2, hy_b2=hy_b2, hy_w3=hy_w3, hy_b3=hy_b3,
        hy_w_filt=hy_w_filt, hy_freq=hy_freq, hy_skip=hy_skip, swa_sink=swa_sink, w_out=w_out,
        ffn_w_gate=ffn_w_gate, ffn_w_up=ffn_w_up, ffn_w_down=ffn_w_down, moe_router=moe_router,
        moe_w_gate=moe_w_gate, moe_w_up=moe_w_up, moe_w_down=moe_w_down, g_final=g_final))
```

```python
import functools
import math

import numpy as np
import jax
import jax.numpy as jnp
from jax import lax
from jax.experimental import pallas as pl
from jax.experimental.pallas import tpu as pltpu

F32 = jnp.float32
BF16 = jnp.bfloat16

D_MODEL = 2048
BATCH = 4
SEQ = 2048
DEPTH = 2
CTX_LEN = 256
GRID_W = 64
NORM_EPS = 1e-6
ROPE_THETA = 10000.0
MLA_HEADS = 8
MLA_NOPE = 128
MLA_ROPE = 64
MLA_V = 128
MLA_Q_LORA = 768
MLA_KV_LORA = 512
HY_WIDTH = 512
HY_BANDS = 16
HY_EMB = 1 + 2 * HY_BANDS
HY_FILTER_HIDDEN = 64
HY_DECAY_TARGET = 1e-2
HY_QUICK_DECAY_PCT = 0.3
HY_SLOW_DECAY_PCT = 1.5
SWA_HEADS = 8
SWA_KV_HEADS = 2
SWA_HEAD_DIM = 64
SWA_WINDOW = 128
N_EXPERTS = 8
D_FF = 5632
D_FF_EXPERT = 7168

LANES = 128
MXU_DIM = 256
VMEM_LIMIT_BYTES = 56 * 1024 * 1024
NEG_BIG = -1e30
MOD_ROWS = 8

_O_CQ = 0
_O_CKV = _O_CQ + MLA_Q_LORA
_O_KPE = _O_CKV + MLA_KV_LORA
_O_HY = _O_KPE + MLA_ROPE
_O_SQ = _O_HY + 3 * HY_WIDTH
_O_SK = _O_SQ + SWA_HEADS * SWA_HEAD_DIM
_O_SV = _O_SK + SWA_KV_HEADS * SWA_HEAD_DIM


def _cparams(*sem):
    return pltpu.CompilerParams(dimension_semantics=sem, vmem_limit_bytes=VMEM_LIMIT_BYTES)


def _dot(a, b):
    return jnp.dot(a, b, preferred_element_type=F32)


def _dot_nt(a, b):
    return lax.dot_general(a, b, (((1,), (1,)), ((), ())), preferred_element_type=F32)


def _split_bf16(a):
    hi = a.astype(BF16)
    lo = (a - hi.astype(F32)).astype(BF16)
    return hi, lo


def _dot3(a, b):
    ah, al = _split_bf16(a)
    bh, bl = _split_bf16(b)
    return _dot(ah, bh) + (_dot(al, bh) + _dot(ah, bl))


def _silu(x):
    return x / (1.0 + jnp.exp(-x))


def _rms(x, g):
    ms = jnp.mean(x * x, axis=-1, keepdims=True)
    return x * lax.rsqrt(ms + NORM_EPS) * g


def _const_spec(shape):
    nd = len(shape)
    return pl.BlockSpec(shape, lambda *_: (0,) * nd, pipeline_mode=pl.Buffered(1))


def _mod_kernel(c_ref, w_ref, b_ref, o_ref):
    ah, al = _split_bf16(_silu(c_ref[...]))
    wh, wl = _split_bf16(w_ref[0])
    r = _dot(jnp.concatenate([ah, al], axis=0), wh)
    o_ref[0] = r[:MOD_ROWS] + (r[MOD_ROWS:] + _dot(ah, wl)) + b_ref[0]


def _modulation(cvec, w_mod, b_mod):
    tn = 1024
    n = 6 * D_MODEL
    return pl.pallas_call(
        _mod_kernel,
        out_shape=jax.ShapeDtypeStruct((DEPTH, MOD_ROWS, n), F32),
        grid=(DEPTH, n // tn),
        in_specs=[
            pl.BlockSpec((MOD_ROWS, D_MODEL), lambda l, j: (0, 0)),
            pl.BlockSpec((1, D_MODEL, tn), lambda l, j: (l, 0, j)),
            pl.BlockSpec((1, 1, tn), lambda l, j: (l, 0, j)),
        ],
        out_specs=pl.BlockSpec((1, MOD_ROWS, tn), lambda l, j: (l, 0, j)),
        compiler_params=_cparams("arbitrary", "arbitrary"),
        name="adaln_mod",
    )(cvec, w_mod, b_mod.reshape(DEPTH, 1, n))


def _norm_kernel(x_ref, g_ref, m_ref, o_ref, *, si):
    y = _rms(x_ref[...], g_ref[...])
    if si is not None:
        y = y * (1.0 + m_ref[si + 1:si + 2, :]) + m_ref[si:si + 1, :]
    o_ref[...] = y.astype(o_ref.dtype)


def _norm_mod(x, g, mods, groups, row0, si, out_dtype, tm=512):
    m = x.shape[0]
    nr = m // groups // tm
    return pl.pallas_call(
        functools.partial(_norm_kernel, si=si),
        out_shape=jax.ShapeDtypeStruct((m, D_MODEL), out_dtype),
        grid=(groups, nr),
        in_specs=[
            pl.BlockSpec((tm, D_MODEL), lambda g_, i: (g_ * nr + i, 0)),
            pl.BlockSpec((1, D_MODEL), lambda g_, i: (0, 0)),
            pl.BlockSpec((None, 6, D_MODEL), lambda g_, i: (row0 + g_, 0, 0)),
        ],
        out_specs=pl.BlockSpec((tm, D_MODEL), lambda g_, i: (g_ * nr + i, 0)),
        compiler_params=_cparams("parallel", "parallel"),
        name="norm_mod",
    )(x, g.reshape(1, D_MODEL), mods)


def _rope_halves(pe_pair, tab):
    r = pe_pair * tab
    return r + pltpu.roll(r, MLA_ROPE, axis=1)


MLA_Q_SCALE = (MLA_NOPE + MLA_ROPE) ** -0.5 * math.log2(math.e)


def _qproj_body(h_ref, wc_ref, g_ref, wq_ref, tab_ref, o_ref):
    cq = _dot(h_ref[...], wc_ref[...])
    cqn = _rms(cq, g_ref[...]).astype(BF16)
    tab = tab_ref[...] * MLA_Q_SCALE
    for hh in range(MLA_HEADS):
        r = _dot(cqn, wq_ref[hh])
        o_ref[hh, :, 0:MLA_NOPE] = (r[:, :MLA_NOPE] * MLA_Q_SCALE).astype(BF16)
        o_ref[hh, :, MLA_NOPE:] = _rope_halves(r[:, MLA_NOPE:], tab).astype(BF16)


def _mla_proj_kernel(h_ref, wcq_ref, gq_ref, wq_ref, wckv_ref, gkv_ref, wkv_ref, tab_ref, q_ref, k_ref, v_ref):
    _qproj_body(h_ref, wcq_ref, gq_ref, wq_ref, tab_ref, q_ref)
    _kvproj_kernel(h_ref, wckv_ref, gkv_ref, wkv_ref, tab_ref, k_ref, v_ref)


def _mla_proj(h, lw, tab, tm=512):
    m = h.shape[0]
    nt = tab.shape[0] // tm
    head_out = jax.ShapeDtypeStruct((MLA_HEADS, m, MXU_DIM), BF16)
    head_spec = pl.BlockSpec((MLA_HEADS, tm, MXU_DIM), lambda i: (0, i, 0))
    return pl.pallas_call(
        _mla_proj_kernel,
        out_shape=(head_out, head_out, head_out),
        grid=(m // tm,),
        in_specs=[
            pl.BlockSpec((tm, D_MODEL), lambda i: (i, 0)),
            _const_spec(lw["w_cq"].shape), _const_spec((1, MLA_Q_LORA)), _const_spec(lw["w_q"].shape),
            _const_spec(lw["w_ckv"].shape), _const_spec((1, MLA_KV_LORA)), _const_spec(lw["w_kv"].shape),
            pl.BlockSpec((tm, LANES), lambda i: (i % nt, 0)),
        ],
        out_specs=(head_spec, head_spec, head_spec),
        compiler_params=_cparams("parallel"),
        name="mla_proj",
    )(h, lw["w_cq"], lw["g_q"].reshape(1, -1), lw["w_q"], lw["w_ckv"], lw["g_kv"].reshape(1, -1), lw["w_kv"], tab)


def _kvproj_kernel(h_ref, wc_ref, g_ref, wkv_ref, tab_ref, k_ref, v_ref):
    t = _dot(h_ref[...], wc_ref[...])
    ckvn = _rms(t[:, :MLA_KV_LORA], g_ref[...]).astype(BF16)
    rot = _rope_halves(t[:, MLA_KV_LORA:], tab_ref[...])
    lane = lax.broadcasted_iota(jnp.int32, rot.shape, 1)
    kpe = jnp.where(lane < MLA_ROPE, rot, 0.0).astype(BF16)
    for hh in range(MLA_HEADS):
        r = _dot(ckvn, wkv_ref[hh])
        k_ref[hh, :, 0:MLA_NOPE] = r[:, :MLA_NOPE].astype(BF16)
        k_ref[hh, :, MLA_NOPE:] = kpe
        v_ref[hh, :, 0:MLA_V] = r[:, MLA_NOPE:].astype(BF16)
        v_ref[hh, :, MLA_V:] = jnp.ones((r.shape[0], MXU_DIM - MLA_V), BF16)


def _kv_proj(h, w_ckv, g_kv, w_kv, tab, tm=512):
    m = h.shape[0]
    nt = tab.shape[0] // tm
    return pl.pallas_call(
        _kvproj_kernel,
        out_shape=(jax.ShapeDtypeStruct((MLA_HEADS, m, MXU_DIM), BF16),
                   jax.ShapeDtypeStruct((MLA_HEADS, m, MXU_DIM), BF16)),
        grid=(m // tm,),
        in_specs=[
            pl.BlockSpec((tm, D_MODEL), lambda i: (i, 0)),
            _const_spec(w_ckv.shape),
            _const_spec((1, MLA_KV_LORA)),
            _const_spec(w_kv.shape),
            pl.BlockSpec((tm, LANES), lambda i: (i % nt, 0)),
        ],
        out_specs=(pl.BlockSpec((MLA_HEADS, tm, MXU_DIM), lambda i: (0, i, 0)),
                   pl.BlockSpec((MLA_HEADS, tm, MXU_DIM), lambda i: (0, i, 0))),
        compiler_params=_cparams("parallel"),
        name="mla_kv_proj",
    )(h, w_ckv, g_kv.reshape(1, -1), w_kv, tab)


MLA_KEY_CHUNK = 512


def _mla_kernel(q_ref, *refs, seg_lens):
    nseg = len(seg_lens)
    o_ref = refs[2 * nseg]
    q = q_ref[...]
    m = jnp.full((q.shape[0], 1), NEG_BIG, F32)
    acc = jnp.zeros((q.shape[0], MXU_DIM), F32)
    for s, lk in enumerate(seg_lens):
        k_ref, v_ref = refs[2 * s], refs[2 * s + 1]
        step = min(MLA_KEY_CHUNK, lk)
        for c0 in range(0, lk, step):
            sc = _dot_nt(q, k_ref[c0:c0 + step, :])
            m_new = jnp.maximum(m, sc.max(axis=-1, keepdims=True))
            p = jnp.exp2(sc - m_new).astype(BF16)
            acc = acc * jnp.exp2(m - m_new) + _dot(p, v_ref[c0:c0 + step, :])
            m = m_new
    o_ref[...] = (acc[:, :MLA_V] / acc[:, MLA_V:MLA_V + 1]).astype(o_ref.dtype)


def _mla_attention(q, segs, lq, tq=512):
    nq = lq // tq
    in_specs = [pl.BlockSpec((None, tq, MXU_DIM), lambda b, h, i: (h, b * nq + i, 0))]
    args = [q]
    for k, v, lk in segs:
        in_specs.append(pl.BlockSpec((None, lk, MXU_DIM), lambda b, h, i: (h, b, 0)))
        in_specs.append(pl.BlockSpec((None, lk, MXU_DIM), lambda b, h, i: (h, b, 0)))
        args += [k, v]
    return pl.pallas_call(
        functools.partial(_mla_kernel, seg_lens=tuple(lk for _, _, lk in segs)),
        out_shape=jax.ShapeDtypeStruct((BATCH * lq, MLA_HEADS * MLA_V), BF16),
        grid=(BATCH, MLA_HEADS, nq),
        in_specs=in_specs,
        out_specs=pl.BlockSpec((tq, MLA_V), lambda b, h, i: (b * nq + i, h)),
        compiler_params=_cparams("parallel", "parallel", "arbitrary"),
        name="mla_attention",
    )(*args)


def _dft_tables(n_len):
    n2 = 2 * n_len
    idx = np.arange(n_len, dtype=np.int64)
    ang = (2.0 * np.pi / n2) * ((idx[:, None] * idx[None, :]) % n2).astype(np.float64)
    cm = np.cos(ang)
    sf = np.sin(ang)
    sf[0, :] = 1.0 - 2.0 * (idx % 2)
    return (jnp.asarray(cm, dtype=BF16), jnp.asarray(sf, dtype=BF16),
            jnp.asarray(sf.T.copy(), dtype=BF16))


def _filter_tables(n_len):
    pos = np.arange(n_len, dtype=np.float64)
    t = pos / max(n_len - 1, 1)
    bands = np.linspace(1e-4, HY_BANDS - 1, HY_BANDS)
    ang = (2.0 * math.pi / n_len) * pos[:, None] * bands[None]
    z = np.zeros((n_len, LANES), np.float64)
    z[:, 0] = t
    z[:, 1:1 + HY_BANDS] = np.cos(ang)
    z[:, 1 + HY_BANDS:HY_EMB] = -np.sin(ang)
    deltas = np.linspace(math.log(HY_DECAY_TARGET) / HY_SLOW_DECAY_PCT,
                         math.log(HY_DECAY_TARGET) / HY_QUICK_DECAY_PCT, HY_WIDTH)
    decay = np.exp(-t[:, None] * np.abs(deltas)[None])
    return jnp.asarray(z, dtype=F32), jnp.asarray(decay, dtype=F32)


def _hy_filter_kernel(z_ref, w1_ref, b1_ref, w2_ref, b2_ref, w3_ref, b3_ref, wf_ref, wb_ref,
                      fr_ref, dec_ref, cm_ref, sf_ref, a_ref, b_ref, d_ref, hid_ref, *, n_len):
    @pl.when(pl.program_id(0) == 0)
    def _():
        fr = fr_ref[...]
        h = jnp.sin(fr * (_dot3(z_ref[...], w1_ref[...]) + b1_ref[...]))
        h = jnp.sin(fr * (_dot3(h, w2_ref[...]) + b2_ref[...]))
        hid_ref[...] = jnp.sin(fr * (_dot3(h, w3_ref[...]) + b3_ref[...]))

    h = hid_ref[...]
    dec = dec_ref[...]
    h_f = _dot3(h, wf_ref[...]) * dec
    h_b = _dot3(h, wb_ref[...]) * dec
    row = lax.broadcasted_iota(jnp.int32, h_f.shape, 0)
    h_b = jnp.where(row == 0, 0.0, h_b)
    hs = h_f + h_b
    hd = h_f - h_b
    nc = hs.shape[1]
    hs2 = jnp.concatenate(_split_bf16(hs), axis=1)
    hd2 = jnp.concatenate(_split_bf16(hd), axis=1)
    sf = sf_ref[...]
    t2 = _dot(cm_ref[...], hs2)
    g2 = _dot(sf, hd2)
    n2 = _dot(sf[0:16, :], hs2)[0:1, :]
    t_re = t2[:, :nc] + t2[:, nc:]
    g_t = g2[:, :nc] + g2[:, nc:]
    t_ny = n2[:, :nc] + n2[:, nc:]
    inv_n = 1.0 / (2 * n_len)
    first = row == 0
    a_ref[...] = jnp.where(first, inv_n, 2.0 * inv_n) * t_re
    b_ref[...] = jnp.where(first, 0.0, -2.0 * inv_n * g_t)
    d_ref[...] = jnp.where(first, inv_n * t_ny, 2.0 * inv_n * t_re)


def _hy_filters(lp, n_len, tabs):
    z, decay = tabs["filt"]
    cm, sf, _ = tabs["dft"]
    cb = LANES
    nblk = HY_WIDTH // cb
    w1 = jnp.zeros((LANES, HY_FILTER_HIDDEN), F32).at[:HY_EMB].set(lp["hy_w1"])
    row = lambda a: a.reshape(1, -1)
    small = lambda shape: pl.BlockSpec(shape, lambda j: (0,) * len(shape))
    fh = HY_FILTER_HIDDEN
    out = jax.ShapeDtypeStruct((n_len, HY_WIDTH), F32)
    return pl.pallas_call(
        functools.partial(_hy_filter_kernel, n_len=n_len),
        out_shape=(out, out, out),
        grid=(nblk,),
        in_specs=[
            small((n_len, LANES)), small((LANES, fh)), small((1, fh)), small((fh, fh)), small((1, fh)),
            small((fh, fh)), small((1, fh)),
            pl.BlockSpec((fh, cb), lambda j: (0, j)),
            pl.BlockSpec((fh, cb), lambda j: (0, nblk + j)),
            small((1, fh)),
            pl.BlockSpec((n_len, cb), lambda j: (0, j)),
            _const_spec((n_len, n_len)), _const_spec((n_len, n_len)),
        ],
        out_specs=tuple(pl.BlockSpec((n_len, cb), lambda j: (0, j)) for _ in range(3)),
        scratch_shapes=[pltpu.VMEM((n_len, fh), F32)],
        compiler_params=_cparams("arbitrary"),
        name="hyena_filters",
    )(z, w1, row(lp["hy_b1"]), lp["hy_w2"], row(lp["hy_b2"]), lp["hy_w3"], row(lp["hy_b3"]),
      lp["hy_w_filt"], lp["hy_w_filt"], row(lp["hy_freq"]), decay, cm, sf)


def _hy_pre_kernel(u0_ref, u1_ref, u2_ref, w0_ref, w1_ref, w2_ref, b0_ref, b1_ref, b2_ref,
                   x0_ref, vx_ref, vxb_ref, *, n_len):
    def sconv(u_ref, w_ref, b_ref):
        u = u_ref[...]
        w = w_ref[...]
        row = lax.broadcasted_iota(jnp.int32, u.shape, 0)
        prev = jnp.where(row == 0, 0.0, pltpu.roll(u, 1, axis=0))
        nxt = jnp.where(row == n_len - 1, 0.0, pltpu.roll(u, n_len - 1, axis=0))
        return w[0:1] * prev + w[1:2] * u + w[2:3] * nxt + b_ref[...]

    x0_ref[...] = sconv(u0_ref, w0_ref, b0_ref)
    vx = sconv(u2_ref, w2_ref, b2_ref) * sconv(u1_ref, w1_ref, b1_ref)
    vx_ref[...] = vx
    vxb_ref[...] = vx.astype(BF16)


def _hy_pre(hy, conv_w, conv_b, n_len):
    nblk = HY_WIDTH // LANES
    uspec = lambda part: pl.BlockSpec((None, n_len, LANES), lambda b, j: (b, 0, part * nblk + j))
    wspec = lambda part: pl.BlockSpec((3, LANES), lambda b, j: (0, part * nblk + j))
    bspec = lambda part: pl.BlockSpec((1, LANES), lambda b, j: (0, part * nblk + j))
    ospec = pl.BlockSpec((None, n_len, LANES), lambda b, j: (b, 0, j))
    hy3 = hy.reshape(BATCH, n_len, 3 * HY_WIDTH)
    cb = conv_b.reshape(1, -1)
    return pl.pallas_call(
        functools.partial(_hy_pre_kernel, n_len=n_len),
        out_shape=(jax.ShapeDtypeStruct((BATCH, n_len, HY_WIDTH), F32),
                   jax.ShapeDtypeStruct((BATCH, n_len, HY_WIDTH), F32),
                   jax.ShapeDtypeStruct((BATCH, n_len, HY_WIDTH), BF16)),
        grid=(BATCH, nblk),
        in_specs=[uspec(0), uspec(1), uspec(2), wspec(0), wspec(1), wspec(2), bspec(0), bspec(1), bspec(2)],
        out_specs=(ospec, ospec, ospec),
        compiler_params=_cparams("parallel", "parallel"),
        name="hyena_short_conv",
    )(hy3, hy3, hy3, conv_w, conv_w, conv_w, cb, cb, cb)


def _hy_fwd_kernel(cm_ref, sf_ref, u_ref, a_ref, b_ref, d_ref, q_ref):
    u = u_ref[...]
    ur = _dot(cm_ref[...], u)
    g = _dot(sf_ref[...], u)
    b = b_ref[...]
    q_ref[0] = (a_ref[...] * ur + b * g).astype(BF16)
    q_ref[1] = (d_ref[...] * g - b * ur).astype(BF16)


def _hy_inv_kernel(cm_ref, si_ref, q_ref, x0_ref, vx_ref, skip_ref, o_ref):
    y = _dot(cm_ref[...], q_ref[0]) + _dot(si_ref[...], q_ref[1])
    o_ref[...] = (x0_ref[...] * (y + vx_ref[...] * skip_ref[...])).astype(o_ref.dtype)


def _hy_long_conv(x0, vx, vxb, spectra, skip, n_len, tabs):
    cm, sf, si = tabs["dft"]
    a, b, d = spectra
    tf = min(1024, n_len)
    nf = n_len // tf
    mat = pl.BlockSpec((tf, n_len), lambda bb, f: (f, 0))
    spec = pl.BlockSpec((tf, HY_WIDTH), lambda bb, f: (f, 0))
    q = pl.pallas_call(
        _hy_fwd_kernel,
        out_shape=jax.ShapeDtypeStruct((BATCH, 2, n_len, HY_WIDTH), BF16),
        grid=(BATCH, nf),
        in_specs=[mat, mat, pl.BlockSpec((None, n_len, HY_WIDTH), lambda bb, f: (bb, 0, 0)), spec, spec, spec],
        out_specs=pl.BlockSpec((None, 2, tf, HY_WIDTH), lambda bb, f: (bb, 0, f, 0)),
        compiler_params=_cparams("parallel", "arbitrary"),
        name="hyena_dft_fwd",
    )(cm, sf, vxb, a, b, d)
    tile = pl.BlockSpec((None, tf, HY_WIDTH), lambda bb, f: (bb, f, 0))
    return pl.pallas_call(
        _hy_inv_kernel,
        out_shape=jax.ShapeDtypeStruct((BATCH, n_len, HY_WIDTH), BF16),
        grid=(BATCH, nf),
        in_specs=[mat, mat, pl.BlockSpec((None, 2, n_len, HY_WIDTH), lambda bb, f: (bb, 0, 0, 0)),
                  tile, tile, pl.BlockSpec((1, HY_WIDTH), lambda bb, f: (0, 0))],
        out_specs=tile,
        compiler_params=_cparams("parallel", "arbitrary"),
        name="hyena_dft_inv",
    )(cm, si, q, x0, vx, skip.reshape(1, -1))


_SWA_Q = SWA_HEADS * SWA_HEAD_DIM
_SWA_KV = SWA_KV_HEADS * SWA_HEAD_DIM


def _swaproj_kernel(h_ref, w_ref, cos_ref, sin_ref, q_ref, k_ref, v_ref, *, rope):
    t = _dot(h_ref[...], w_ref[...])
    q = t[:, :_SWA_Q]
    k = t[:, 2 * _SWA_Q:2 * _SWA_Q + _SWA_KV]
    v = t[:, 2 * _SWA_Q + 2 * _SWA_KV:]
    if rope:
        cos = cos_ref[...]
        sin = sin_ref[...]
        q = q * cos + t[:, _SWA_Q:2 * _SWA_Q] * sin
        k = k * cos[:, :_SWA_KV] + t[:, 2 * _SWA_Q + _SWA_KV:2 * _SWA_Q + 2 * _SWA_KV] * sin[:, :_SWA_KV]
    q_ref[...] = (q * (SWA_HEAD_DIM ** -0.5 * math.log2(math.e))).astype(BF16)
    lo = lax.broadcasted_iota(jnp.int32, k.shape, 1) < SWA_HEAD_DIM
    for src, dst in ((k, k_ref), (v, v_ref)):
        sw = pltpu.roll(src, SWA_HEAD_DIM, axis=1)
        dst[:, 0 * LANES:1 * LANES] = jnp.where(lo, src, 0.0).astype(BF16)
        dst[:, 1 * LANES:2 * LANES] = jnp.where(lo, 0.0, sw).astype(BF16)
        dst[:, 2 * LANES:3 * LANES] = jnp.where(lo, sw, 0.0).astype(BF16)
        dst[:, 3 * LANES:4 * LANES] = jnp.where(lo, 0.0, src).astype(BF16)


def _hy_swa_kernel(h_ref, why_ref, w_ref, cos_ref, sin_ref, hy_ref, q_ref, k_ref, v_ref, *, rope):
    hy_ref[...] = _dot(h_ref[...], why_ref[...])
    _swaproj_kernel(h_ref, w_ref, cos_ref, sin_ref, q_ref, k_ref, v_ref, rope=rope)


def _swa_proj(h, w, cos_t, sin_t, rope, w_hy=None, tm=512):
    m = h.shape[0]
    nt = cos_t.shape[0] // tm
    o = jax.ShapeDtypeStruct((m, 4 * LANES), BF16)
    ospec = pl.BlockSpec((tm, 4 * LANES), lambda i: (i, 0))
    tspec = pl.BlockSpec((tm, _SWA_Q), lambda i: (i % nt, 0))
    hspec = pl.BlockSpec((tm, D_MODEL), lambda i: (i, 0))
    if w_hy is None:
        return pl.pallas_call(
            functools.partial(_swaproj_kernel, rope=rope),
            out_shape=(o, o, o),
            grid=(m // tm,),
            in_specs=[hspec, _const_spec(w.shape), tspec, tspec],
            out_specs=(ospec, ospec, ospec),
            compiler_params=_cparams("parallel"),
            name="swa_proj",
        )(h, w, cos_t, sin_t)
    n_hy = w_hy.shape[1]
    return pl.pallas_call(
        functools.partial(_hy_swa_kernel, rope=rope),
        out_shape=(jax.ShapeDtypeStruct((m, n_hy), F32), o, o, o),
        grid=(m // tm,),
        in_specs=[hspec, _const_spec(w_hy.shape), _const_spec(w.shape), tspec, tspec],
        out_specs=(pl.BlockSpec((tm, n_hy), lambda i: (i, 0)), ospec, ospec, ospec),
        compiler_params=_cparams("parallel"),
        name="hyena_swa_proj",
    )(h, w_hy, w, cos_t, sin_t)


def _swa_kernel(sink_ref, q_ref, *refs, tq, win, lk, has_lat):
    if has_lat:
        kl_ref, vl_ref, kc_ref, vc_ref, o_ref = refs
        t0 = pl.program_id(1) * tq
        start = pl.multiple_of(jnp.clip(t0 - SWA_WINDOW, 0, lk - win), SWA_WINDOW)
        qpos = t0 + lax.broadcasted_iota(jnp.int32, (tq, win), 0)
        kpos = start + lax.broadcasted_iota(jnp.int32, (tq, win), 1)
        valid = jnp.abs(qpos - kpos) <= SWA_WINDOW
        valid = jnp.concatenate([valid, valid], axis=0)
    else:
        kc_ref, vc_ref, o_ref = refs
    upper = lax.broadcasted_iota(jnp.int32, (2 * tq, 1), 0) < tq
    for g in range(SWA_KV_HEADS):
        j0 = 2 * g
        qb = jnp.concatenate([q_ref[:, j0 * LANES:(j0 + 1) * LANES],
                              q_ref[:, (j0 + 1) * LANES:(j0 + 2) * LANES]], axis=0)
        acc = None
        for par in range(2):
            c0 = (2 * g + par) * LANES
            sk = jnp.where(upper, sink_ref[2 * j0 + par], sink_ref[2 * j0 + 2 + par]) * math.log2(math.e)
            s_c = _dot_nt(qb, kc_ref[:, c0:c0 + LANES])
            m = jnp.maximum(s_c.max(axis=-1, keepdims=True), sk)
            if has_lat:
                s_l = _dot_nt(qb, kl_ref[pl.ds(start, win), c0:c0 + LANES])
                s_l = jnp.where(valid, s_l, NEG_BIG)
                m = jnp.maximum(m, s_l.max(axis=-1, keepdims=True))
            p_c = jnp.exp2(s_c - m)
            den = p_c.sum(axis=-1, keepdims=True) + jnp.exp2(sk - m)
            if has_lat:
                p_l = jnp.exp2(s_l - m)
                den = den + p_l.sum(axis=-1, keepdims=True)
            o = _dot(p_c.astype(BF16), vc_ref[:, c0:c0 + LANES])
            if has_lat:
                o = o + _dot(p_l.astype(BF16), vl_ref[pl.ds(start, win), c0:c0 + LANES])
            o = o * (1.0 / den)
            acc = o if acc is None else acc + o
        o_ref[:, j0 * LANES:(j0 + 1) * LANES] = acc[:tq].astype(o_ref.dtype)
        o_ref[:, (j0 + 1) * LANES:(j0 + 2) * LANES] = acc[tq:].astype(o_ref.dtype)


def _swa_attention(sink, q, lat, ctx, lq, tq=256):
    nq = lq // tq
    win = tq + 2 * SWA_WINDOW
    full = lambda n: pl.BlockSpec((n, 4 * LANES), lambda b, i: (b, 0))
    in_specs = [pl.BlockSpec(memory_space=pltpu.SMEM),
                pl.BlockSpec((tq, 4 * LANES), lambda b, i: (b * nq + i, 0))]
    args = [sink, q]
    if lat is not None:
        in_specs += [full(SEQ), full(SEQ)]
        args += list(lat)
    in_specs += [full(CTX_LEN), full(CTX_LEN)]
    args += list(ctx)
    return pl.pallas_call(
        functools.partial(_swa_kernel, tq=tq, win=win, lk=SEQ, has_lat=lat is not None),
        out_shape=jax.ShapeDtypeStruct((BATCH * lq, 4 * LANES), BF16),
        grid=(BATCH, nq),
        in_specs=in_specs,
        out_specs=pl.BlockSpec((tq, 4 * LANES), lambda b, i: (b * nq + i, 0)),
        compiler_params=_cparams("parallel", "arbitrary"),
        name="swa_attention",
    )(*args)


def _out_kernel(a_ref, y_ref, s_ref, w_ref, x_ref, m_ref, g_ref, xo_ref, ho_ref):
    na = MLA_HEADS * MLA_V
    acc = _dot(a_ref[...], w_ref[0:na, :])
    acc = acc + _dot(y_ref[...], w_ref[na:na + HY_WIDTH, :])
    acc = acc + _dot(s_ref[...], w_ref[na + HY_WIDTH:, :])
    x = x_ref[...] + m_ref[2:3, :] * acc
    xo_ref[...] = x
    ho_ref[...] = (_rms(x, g_ref[...]) * (1.0 + m_ref[4:5, :]) + m_ref[3:4, :]).astype(ho_ref.dtype)


def _out_proj(a, y, s, w_out, x, mods, g_ffn, groups, row0, h_dtype, tm=512):
    m = x.shape[0]
    nr = m // groups // tm
    tile = lambda n: pl.BlockSpec((tm, n), lambda g_, i: (g_ * nr + i, 0))
    return pl.pallas_call(
        _out_kernel,
        out_shape=(jax.ShapeDtypeStruct((m, D_MODEL), F32), jax.ShapeDtypeStruct((m, D_MODEL), h_dtype)),
        grid=(groups, nr),
        in_specs=[tile(a.shape[1]), tile(y.shape[1]), tile(s.shape[1]), _const_spec(w_out.shape),
                  tile(D_MODEL), pl.BlockSpec((None, 6, D_MODEL), lambda g_, i: (row0 + g_, 0, 0)),
                  pl.BlockSpec((1, D_MODEL), lambda g_, i: (0, 0))],
        out_specs=(tile(D_MODEL), tile(D_MODEL)),
        compiler_params=_cparams("parallel", "parallel"),
        name="out_proj",
    )(a, y, s, w_out, x, mods, g_ffn.reshape(1, -1))


_META_W1, _META_W2, _META_E1, _META_E2, _META_R1, _META_R2 = range(6)


def _route(h, is_first, w_ref, meta_ref, cnt_ref, carry_ref):
    @pl.when(is_first)
    def _():
        carry_ref[...] = jnp.zeros_like(carry_ref)

    logits = _dot3(h, w_ref[...])
    tm = logits.shape[0]
    lane = lax.broadcasted_iota(jnp.int32, logits.shape, 1).astype(F32)
    logits = jnp.where(lane < N_EXPERTS, logits, NEG_BIG)
    m1 = logits.max(axis=-1, keepdims=True)
    i1 = jnp.where(logits == m1, lane, float(LANES)).min(axis=-1, keepdims=True)
    rest = jnp.where(lane == i1, NEG_BIG, logits)
    m2 = rest.max(axis=-1, keepdims=True)
    i2 = jnp.where(rest == m2, lane, float(LANES)).min(axis=-1, keepdims=True)
    e2 = jnp.exp(m2 - m1)
    w1 = 1.0 / (1.0 + e2)
    hot = jnp.where((lane == i1) | (lane == i2), 1.0, 0.0)
    before = (lax.broadcasted_iota(jnp.int32, (tm, tm), 0) > lax.broadcasted_iota(jnp.int32, (tm, tm), 1))
    seen = _dot(before.astype(F32).astype(BF16), hot.astype(BF16)) + carry_ref[0:1, :]
    r1 = jnp.where(lane == i1, seen, 0.0).sum(axis=-1, keepdims=True)
    r2 = jnp.where(lane == i2, seen, 0.0).sum(axis=-1, keepdims=True)
    carry_ref[...] = carry_ref[...] + hot.sum(axis=0, keepdims=True)
    cnt_ref[...] = carry_ref[...]
    rec = jnp.zeros_like(logits)
    for k, v in ((_META_W1, w1), (_META_W2, e2 * w1), (_META_E1, i1), (_META_E2, i2), (_META_R1, r1), (_META_R2, r2)):
        rec = jnp.where(lane == float(k), v, rec)
    meta_ref[...] = rec


def _router_kernel(h_ref, w_ref, meta_ref, cnt_ref, carry_ref):
    _route(h_ref[...], pl.program_id(0) == 0, w_ref, meta_ref, cnt_ref, carry_ref)


def _router(h, w_router, tm=512):
    m = h.shape[0]
    wp = jnp.zeros((D_MODEL, LANES), F32).at[:, :N_EXPERTS].set(w_router)
    return pl.pallas_call(
        _router_kernel,
        out_shape=(jax.ShapeDtypeStruct((m, LANES), F32), jax.ShapeDtypeStruct((8, LANES), F32)),
        grid=(m // tm,),
        in_specs=[pl.BlockSpec((tm, D_MODEL), lambda i: (i, 0)),
                  pl.BlockSpec((D_MODEL, LANES), lambda i: (0, 0))],
        out_specs=(pl.BlockSpec((tm, LANES), lambda i: (i, 0)), pl.BlockSpec((8, LANES), lambda i: (0, 0))),
        scratch_shapes=[pltpu.VMEM((8, LANES), F32)],
        compiler_params=_cparams("arbitrary"),
        name="moe_router",
    )(h, wp)


MOE_TM = 1024
MOE_BLK = 128
MOE_ROWS = (MOE_TM, 768, 512, 256)


def _moe_plan(meta, counts, m):
    n_steps = -(-2 * m // MOE_TM) + N_EXPERTS
    cnt = counts[0, :N_EXPERTS].astype(jnp.int32)
    steps_e = (cnt + MOE_TM - 1) // MOE_TM
    ends = jnp.cumsum(steps_e)
    first = ends - steps_e
    total = ends[-1]
    e = meta[:, _META_E1:_META_E2 + 1].astype(jnp.int32)
    rank = meta[:, _META_R1:_META_R2 + 1].astype(jnp.int32)
    pos = (first * MOE_TM)[e] + rank
    s_idx = jnp.arange(n_steps, dtype=jnp.int32)
    step_e = jnp.minimum(jnp.searchsorted(ends, s_idx, side="right").astype(jnp.int32), N_EXPERTS - 1)
    valid = jnp.clip(cnt[step_e] - (s_idx - first[step_e]) * MOE_TM, 0, MOE_TM)
    valid = jnp.where(s_idx < total, valid, 0)
    return pos.reshape(-1), step_e, valid, n_steps


def _row_copy(src_ref, src_row, dst_ref, dst_row, sem):
    return pltpu.make_async_copy(src_ref.at[pl.ds(src_row, 1)], dst_ref.at[pl.ds(dst_row, 1)], sem)


def _invert_kernel(pos_ref, sv_ref, src_ref):
    def clear_step(s, c):
        def clear(i, c2):
            src_ref[s * MOE_TM + i] = 0
            return c2

        return lax.fori_loop(sv_ref[s], MOE_TM, clear, c)

    def place(t, c):
        p0 = pos_ref[2 * t]
        p1 = pos_ref[2 * t + 1]
        src_ref[p0] = t
        src_ref[p1] = t
        return c

    lax.fori_loop(0, sv_ref.shape[0], clear_step, 0)
    lax.fori_loop(0, pos_ref.shape[0] // 2, place, 0, unroll=8)


def _moe_invert(pos, valid, n_rows):
    smem = pl.BlockSpec(memory_space=pltpu.SMEM)
    return pl.pallas_call(
        _invert_kernel,
        out_shape=jax.ShapeDtypeStruct((n_rows,), jnp.int32),
        in_specs=[smem, smem], out_specs=smem,
        name="moe_invert",
    )(pos, valid)


def _block_wait(src_ref, dst_ref, rows, sem):
    pltpu.make_async_copy(src_ref.at[pl.ds(0, rows)], dst_ref.at[pl.ds(0, rows)], sem).wait()


def _moe_ffn_kernel(se_ref, sv_ref, src_ref, h_ref, wg_ref, wu_ref, wd_ref, o_ref, x32_ref, xb_ref, sem):
    del se_ref
    s = pl.program_id(0)
    f = pl.program_id(1)
    valid = sv_ref[s]

    def rows_used(v):
        r = jnp.where(v > 0, MOE_ROWS[-1], 0)
        for small, big in zip(MOE_ROWS[:0:-1], MOE_ROWS[-2::-1]):
            r = jnp.where(v > small, big, r)
        return r

    def blocks(v):
        return rows_used(v) // MOE_BLK

    def gather(step):
        base = step * MOE_TM

        def issue(b, c):
            for k in range(8):
                i = b * 8 + k
                _row_copy(h_ref, src_ref[base + i], x32_ref, i, sem).start()
            return c

        lax.fori_loop(0, rows_used(sv_ref[step]) // 8, issue, 0)

    @pl.when(f == 0)
    def _():
        @pl.when(s == 0)
        def _():
            gather(0)

        def land(i, c):
            _block_wait(h_ref, x32_ref, MOE_BLK, sem)
            return c

        def to_bf16(i, c):
            r0 = pl.multiple_of(i * MOE_BLK, MOE_BLK)
            xb_ref[pl.ds(r0, MOE_BLK), :] = x32_ref[pl.ds(r0, MOE_BLK), :].astype(BF16)
            return c

        lax.fori_loop(0, blocks(valid), land, 0)
        lax.fori_loop(0, blocks(valid), to_bf16, 0)

        @pl.when(s + 1 < pl.num_programs(0))
        def _():
            gather(s + 1)

        o_ref[...] = jnp.zeros_like(o_ref)

    def compute(rows):
        xs = xb_ref[0:rows, :]
        mid = _silu(_dot(xs, wg_ref[...].astype(BF16))) * _dot(xs, wu_ref[...].astype(BF16))
        o_ref[0:rows, :] += _dot(mid.astype(BF16), wd_ref[...].astype(BF16))

    for k, rows in enumerate(MOE_ROWS):
        lower = MOE_ROWS[k + 1] if k + 1 < len(MOE_ROWS) else 0

        @pl.when((valid > lower) & (valid <= rows))
        def _():
            compute(rows)


def _moe_ffn(h, src, w_gate, w_up, w_down, step_e, valid, n_steps, tf=512):
    assert all(r % MOE_BLK == 0 for r in MOE_ROWS) and MOE_ROWS[0] == MOE_TM
    n_ff = w_gate.shape[-1]
    nf = n_ff // tf
    fidx = lambda s, f, sv: jnp.where(sv[s] > 0, f, nf - 1)
    return pl.pallas_call(
        _moe_ffn_kernel,
        out_shape=jax.ShapeDtypeStruct((n_steps * MOE_TM, D_MODEL), F32),
        grid_spec=pltpu.PrefetchScalarGridSpec(
            num_scalar_prefetch=3, grid=(n_steps, nf),
            in_specs=[
                pl.BlockSpec(memory_space=pl.ANY),
                pl.BlockSpec((None, D_MODEL, tf), lambda s, f, se, sv, sr: (se[s], 0, fidx(s, f, sv))),
                pl.BlockSpec((None, D_MODEL, tf), lambda s, f, se, sv, sr: (se[s], 0, fidx(s, f, sv))),
                pl.BlockSpec((None, tf, D_MODEL), lambda s, f, se, sv, sr: (se[s], fidx(s, f, sv), 0)),
            ],
            out_specs=pl.BlockSpec((MOE_TM, D_MODEL), lambda s, f, se, sv, sr: (s, 0), pipeline_mode=pl.Buffered(1)),
            scratch_shapes=[pltpu.VMEM((MOE_TM, D_MODEL), F32), pltpu.VMEM((MOE_TM, D_MODEL), BF16),
                            pltpu.SemaphoreType.DMA(())]),
        compiler_params=_cparams("arbitrary", "arbitrary"),
        name="moe_experts",
    )(step_e, valid, src, h, w_gate, w_up, w_down)


def _combine_kernel(pos_ref, meta_ref, x_ref, m_ref, gf_ref, ys_ref, o_ref, buf_ref, sems, *, tm, final):
    t = pl.program_id(0)

    def request(tile):
        b = tile % 2

        def issue(i, c):
            for slot in range(2):
                _row_copy(ys_ref, pos_ref[2 * (tile * tm + i) + slot], buf_ref.at[b, slot], i, sems.at[b]).start()
            return c

        lax.fori_loop(0, tm, issue, 0, unroll=4)

    @pl.when(t == 0)
    def _():
        request(0)

    @pl.when(t + 1 < pl.num_programs(0))
    def _():
        request(t + 1)

    cur = t % 2
    for slot in range(2):
        _block_wait(ys_ref, buf_ref.at[cur, slot], tm, sems.at[cur])
    meta = meta_ref[...]
    y = meta[:, _META_W1:_META_W1 + 1] * buf_ref[cur, 0] + meta[:, _META_W2:_META_W2 + 1] * buf_ref[cur, 1]
    x = x_ref[...] + m_ref[5:6, :] * y
    if final:
        x = _rms(x, gf_ref[...])
    o_ref[...] = x


def _moe_combine(pos, meta, x, mods, ys, groups, row0, g_final, tm=512):
    m = x.shape[0]
    nr = m // groups // tm
    final = g_final is not None
    gf = (g_final if final else jnp.ones((D_MODEL,), F32)).reshape(1, D_MODEL)
    return pl.pallas_call(
        functools.partial(_combine_kernel, tm=tm, final=final),
        out_shape=jax.ShapeDtypeStruct((m, D_MODEL), F32),
        grid_spec=pltpu.PrefetchScalarGridSpec(
            num_scalar_prefetch=1, grid=(m // tm,),
            in_specs=[pl.BlockSpec((tm, LANES), lambda i, p: (i, 0)),
                      pl.BlockSpec((tm, D_MODEL), lambda i, p: (i, 0)),
                      pl.BlockSpec((None, 6, D_MODEL), lambda i, p: (row0 + i // nr, 0, 0)),
                      pl.BlockSpec((1, D_MODEL), lambda i, p: (0, 0)),
                      pl.BlockSpec(memory_space=pl.ANY)],
            out_specs=pl.BlockSpec((tm, D_MODEL), lambda i, p: (i, 0)),
            scratch_shapes=[pltpu.VMEM((2, 2, tm, D_MODEL), F32), pltpu.SemaphoreType.DMA((2,))]),
        compiler_params=_cparams("arbitrary"),
        name="moe_combine",
    )(pos, meta, x, mods, gf, ys)


def _moe(h, meta, counts, x, mods, groups, row0, w_gate, w_up, w_down, g_final):
    m = h.shape[0]
    pos, step_e, valid, n_steps = _moe_plan(meta, counts, m)
    src = _moe_invert(pos, valid, n_steps * MOE_TM)
    ys = _moe_ffn(h, src, w_gate, w_up, w_down, step_e, valid, n_steps)
    return _moe_combine(pos, meta, x, mods, ys, groups, row0, g_final)


def _ffn_kernel(h_ref, wg_ref, wu_ref, wd_ref, x_ref, m_ref, gn_ref, mn_ref, o_ref, *maybe_hn_ref):
    f = pl.program_id(2)

    @pl.when(f == 0)
    def _():
        o_ref[...] = jnp.zeros_like(o_ref)

    h = h_ref[...]
    mid = _silu(_dot(h, wg_ref[...])) * _dot(h, wu_ref[...])
    o_ref[...] += _dot(mid.astype(BF16), wd_ref[...])

    @pl.when(f == pl.num_programs(2) - 1)
    def _():
        x = x_ref[...] + m_ref[5:6, :] * o_ref[...]
        o_ref[...] = x
        for hn_ref in maybe_hn_ref:
            hn_ref[...] = (_rms(x, gn_ref[...]) * (1.0 + mn_ref[1:2, :]) + mn_ref[0:1, :]).astype(hn_ref.dtype)


def _ffn(h, w_gate, w_up, w_down, x, mods, groups, row0, nxt, tm, tf=512):
    m = x.shape[0]
    n_ff = w_gate.shape[1]
    nr = m // groups // tm
    g_next, mods_next = nxt if nxt is not None else (jnp.ones((D_MODEL,), F32), mods)
    once = pl.Buffered(1)
    tile = lambda n, mode=None: pl.BlockSpec((tm, n), lambda g_, i, f: (g_ * nr + i, 0), pipeline_mode=mode)
    mod_spec = pl.BlockSpec((None, 6, D_MODEL), lambda g_, i, f: (row0 + g_, 0, 0))
    out_shape = [jax.ShapeDtypeStruct((m, D_MODEL), F32)]
    out_specs = [tile(D_MODEL, once)]
    if nxt is not None:
        out_shape.append(jax.ShapeDtypeStruct((m, D_MODEL), BF16))
        out_specs.append(tile(D_MODEL, once))
    outs = pl.pallas_call(
        _ffn_kernel,
        out_shape=tuple(out_shape),
        grid=(groups, nr, n_ff // tf),
        in_specs=[tile(D_MODEL),
                  pl.BlockSpec((D_MODEL, tf), lambda g_, i, f: (0, f)),
                  pl.BlockSpec((D_MODEL, tf), lambda g_, i, f: (0, f)),
                  pl.BlockSpec((tf, D_MODEL), lambda g_, i, f: (f, 0)),
                  tile(D_MODEL, once), mod_spec,
                  pl.BlockSpec((1, D_MODEL), lambda g_, i, f: (0, 0)), mod_spec],
        out_specs=tuple(out_specs),
        compiler_params=_cparams("parallel", "parallel", "arbitrary"),
        name="swiglu_ffn",
    )(h, w_gate, w_up, w_down, x, mods, g_next.reshape(1, D_MODEL), mods_next)
    return outs if nxt is not None else (outs[0], None)


def _swap_halves(w):
    half = w.shape[-1] // 2
    return jnp.concatenate([w[..., half:], w[..., :half]], axis=-1)


def _swap_heads(w, heads, dim):
    k = w.shape[0]
    return _swap_halves(w.reshape(k, heads, dim)).reshape(k, heads * dim)


def _rope_tables():
    rows = SEQ // GRID_W
    row = np.repeat(np.arange(rows), GRID_W).astype(np.float32)
    col = np.tile(np.arange(GRID_W), rows).astype(np.float32)
    quarter = MLA_ROPE // 4
    freqs = (np.float32(ROPE_THETA) ** (-np.arange(quarter, dtype=np.float32) / quarter)).astype(np.float32)
    ang = np.concatenate([row[:, None] * freqs[None], col[:, None] * freqs[None]], axis=-1)
    cos = np.cos(ang.astype(np.float64))
    sin = np.sin(ang.astype(np.float64))
    cos2 = np.concatenate([cos, cos], axis=-1)
    sin2 = np.concatenate([-sin, sin], axis=-1)
    tab_lat = np.concatenate([cos2, sin2], axis=-1)
    tab_ctx = np.concatenate([np.ones((CTX_LEN, MLA_ROPE)), np.zeros((CTX_LEN, MLA_ROPE))], axis=-1)
    return dict(
        tab_lat=jnp.asarray(tab_lat, F32), tab_ctx=jnp.asarray(tab_ctx, F32),
        cos8=jnp.asarray(np.tile(cos2, (1, SWA_HEADS)), F32),
        sin8=jnp.asarray(np.tile(sin2, (1, SWA_HEADS)), F32),
        ones8=jnp.ones((CTX_LEN, _SWA_Q), F32), zeros8=jnp.zeros((CTX_LEN, _SWA_Q), F32),
    )


def _layer_weights(p, l):
    w_in = p["w_in"][l]
    cq = w_in[:, _O_CQ:_O_CKV]
    ckv = w_in[:, _O_CKV:_O_KPE]
    kpe = w_in[:, _O_KPE:_O_HY]
    hy = w_in[:, _O_HY:_O_SQ]
    sq = w_in[:, _O_SQ:_O_SK]
    sk = w_in[:, _O_SK:_O_SV]
    sv = w_in[:, _O_SV:]
    dq = MLA_NOPE + MLA_ROPE
    wq = p["w_q_up"][l].reshape(MLA_Q_LORA, MLA_HEADS, dq)
    wq = jnp.concatenate([wq, _swap_halves(wq[..., MLA_NOPE:])], axis=-1)
    wkv = p["w_kv_up"][l].reshape(MLA_KV_LORA, MLA_HEADS, MLA_NOPE + MLA_V)
    return dict(
        w_cq=cq.astype(BF16),
        w_ckv=jnp.concatenate([ckv, kpe, _swap_halves(kpe)], axis=-1).astype(BF16),
        w_hy=hy.astype(BF16),
        w_swa=jnp.concatenate([sq, _swap_heads(sq, SWA_HEADS, SWA_HEAD_DIM), sk,
                               _swap_heads(sk, SWA_KV_HEADS, SWA_HEAD_DIM), sv], axis=-1).astype(BF16),
        w_q=jnp.transpose(wq, (1, 0, 2)).astype(BF16),
        w_kv=jnp.transpose(wkv, (1, 0, 2)).astype(BF16),
        w_out=p["w_out"][l].astype(BF16),
        g_q=p["g_q"][l], g_kv=p["g_kv"][l],
        hy_conv_w=p["hy_conv_w"][l], hy_conv_b=p["hy_conv_b"][l],
        hy_w1=p["hy_w1"][l], hy_b1=p["hy_b1"][l], hy_w2=p["hy_w2"][l], hy_b2=p["hy_b2"][l],
        hy_w3=p["hy_w3"][l], hy_b3=p["hy_b3"][l], hy_w_filt=p["hy_w_filt"][l],
        hy_freq=p["hy_freq"][l], hy_skip=p["hy_skip"][l], swa_sink=p["swa_sink"][l],
    )


def _hyena(hy, lw, n_len, tabs):
    x0, vx, vxb = _hy_pre(hy, lw["hy_conv_w"], lw["hy_conv_b"], n_len)
    spectra = _hy_filters(lw, n_len, tabs)
    y = _hy_long_conv(x0, vx, vxb, spectra, lw["hy_skip"], n_len, tabs)
    return y.reshape(BATCH * n_len, HY_WIDTH)


def _mixer(h_lat, h_ctx, lw, rope, hy_tabs, need_ctx):
    q_l, k_l, v_l = _mla_proj(h_lat, lw, rope["tab_lat"], tm=1024)
    if need_ctx:
        q_c, k_c, v_c = _mla_proj(h_ctx, lw, rope["tab_ctx"], tm=CTX_LEN)
    else:
        k_c, v_c = _kv_proj(h_ctx, lw["w_ckv"], lw["g_kv"], lw["w_kv"], rope["tab_ctx"], tm=CTX_LEN)
    a_l = _mla_attention(q_l, [(k_l, v_l, SEQ), (k_c, v_c, CTX_LEN)], SEQ)
    hy_l, sq_l, kk_l, vv_l = _swa_proj(h_lat, lw["w_swa"], rope["cos8"], rope["sin8"], True, w_hy=lw["w_hy"],
                                       tm=1024)
    y_l = _hyena(hy_l, lw, SEQ, hy_tabs[SEQ])
    if need_ctx:
        hy_c, sq_c, kk_c, vv_c = _swa_proj(h_ctx, lw["w_swa"], rope["ones8"], rope["zeros8"], False,
                                           w_hy=lw["w_hy"], tm=CTX_LEN)
    else:
        sq_c, kk_c, vv_c = _swa_proj(h_ctx, lw["w_swa"], rope["ones8"], rope["zeros8"], False, tm=CTX_LEN)
    s_l = _swa_attention(lw["swa_sink"], sq_l, (kk_l, vv_l), (kk_c, vv_c), SEQ)
    if not need_ctx:
        return (a_l, y_l, s_l), None
    a_c = _mla_attention(q_c, [(k_c, v_c, CTX_LEN)], CTX_LEN, tq=CTX_LEN)
    y_c = _hyena(hy_c, lw, CTX_LEN, hy_tabs[CTX_LEN])
    s_c = _swa_attention(lw["swa_sink"], sq_c, None, (kk_c, vv_c), CTX_LEN, tq=CTX_LEN)
    return (a_l, y_l, s_l), (a_c, y_c, s_c)


def _forward(p):
    rope = _rope_tables()
    hy_tabs = {n: dict(dft=_dft_tables(n), filt=_filter_tables(n)) for n in (SEQ, CTX_LEN)}
    cvec = jnp.zeros((MOD_ROWS, D_MODEL), F32).at[:BATCH].set(p["c"]).at[BATCH].set(p["c_ctx"])
    mods_all = _modulation(cvec, p["w_mod"], p["b_mod"]).reshape(DEPTH, MOD_ROWS, 6, D_MODEL)

    x_lat = p["x"].reshape(BATCH * SEQ, D_MODEL)
    x_ctx = p["ctx"].reshape(BATCH * CTX_LEN, D_MODEL)
    h_in = [None, None]
    for l in range(DEPTH):
        last = l == DEPTH - 1
        mods = mods_all[l]
        lw = _layer_weights(p, l)
        h_lat = h_in[0] if h_in[0] is not None else _norm_mod(x_lat, p["g_mix"][l], mods, BATCH, 0, 0, BF16)
        h_ctx = h_in[1] if h_in[1] is not None else _norm_mod(x_ctx, p["g_mix"][l], mods, 1, BATCH, 0, BF16)
        mix_l, mix_c = _mixer(h_lat, h_ctx, lw, rope, hy_tabs, not last)
        streams = [(x_lat, mix_l, BATCH, 0, 1024)]
        if not last:
            streams.append((x_ctx, mix_c, 1, BATCH, 512))
        outs = []
        h_in = [None, None]
        i = l // 2
        dense = l % 2 == 0
        for n, (x, mix, groups, row0, ffn_tm) in enumerate(streams):
            if dense:
                x, h2 = _out_proj(*mix, lw["w_out"], x, mods, p["g_ffn"][l], groups, row0, BF16)
                nxt = None if last else (p["g_mix"][l + 1], mods_all[l + 1])
                x, h_in[n] = _ffn(h2, p["ffn_w_gate"][i].astype(BF16), p["ffn_w_up"][i].astype(BF16),
                                  p["ffn_w_down"][i].astype(BF16), x, mods, groups, row0, nxt, ffn_tm)
            else:
                x, h2 = _out_proj(*mix, lw["w_out"], x, mods, p["g_ffn"][l], groups, row0, F32)
                meta, counts = _router(h2, p["moe_router"][i])
                g_final = p["g_final"] if (last and n == 0) else None
                x = _moe(h2, meta, counts, x, mods, groups, row0, p["moe_w_gate"][i], p["moe_w_up"][i],
                         p["moe_w_down"][i], g_final)
            outs.append(x)
        x_lat = outs[0]
        if not last:
            x_ctx = outs[1]
    if DEPTH % 2 == 1:
        x_lat = _norm_mod(x_lat, p["g_final"], mods_all[0], BATCH, 0, None, F32)
    return x_lat.reshape(BATCH, SEQ, D_MODEL)


def kernel(x, c, ctx, c_ctx, w_mod, b_mod, g_mix, g_ffn, w_in, g_q, w_q_up, g_kv, w_kv_up, hy_conv_w, hy_conv_b, hy_w1, hy_b1, hy_w2, hy_b2, hy_w3, hy_b3, hy_w_filt, hy_freq, hy_skip, swa_sink, w_out, ffn_w_gate, ffn_w_up, ffn_w_down, moe_router, moe_w_gate, moe_w_up, moe_w_down, g_final):
    return _forward(dict(
        x=x, c=c, ctx=ctx, c_ctx=c_ctx, w_mod=w_mod, b_mod=b_mod, g_mix=g_mix, g_ffn=g_ffn, w_in=w_in,
        g_q=g_q, w_q_up=w_q_up, g_kv=g_kv, w_kv_up=w_kv_up, hy_conv_w=hy_conv_w, hy_conv_b=hy_conv_b,
        hy_w1=hy_w1, hy_b1=hy_b1, hy_w2=hy_w2, hy_b2=hy_b2, hy_w3=hy_w3, hy_b3=hy_b3,
        hy_w_filt=hy_w_filt, hy_freq=hy_freq, hy_skip=hy_skip, swa_sink=swa_sink, w_out=w_out,
        ffn_w_gate=ffn_w_gate, ffn_w_up=ffn_w_up, ffn_w_down=ffn_w_down, moe_router=moe_router,
        moe_w_gate=moe_w_gate, moe_w_up=moe_w_up, moe_w_down=moe_w_down, g_final=g_final))
```

```python
import functools
import math

import numpy as np
import jax
import jax.numpy as jnp
from jax import lax
from jax.experimental import pallas as pl
from jax.experimental.pallas import tpu as pltpu

F32 = jnp.float32
BF16 = jnp.bfloat16

D_MODEL = 2048
BATCH = 4
SEQ = 2048
DEPTH = 2
CTX_LEN = 256
GRID_W = 64
NORM_EPS = 1e-6
ROPE_THETA = 10000.0
MLA_HEADS = 8
MLA_NOPE = 128
MLA_ROPE = 64
MLA_V = 128
MLA_Q_LORA = 768
MLA_KV_LORA = 512
HY_WIDTH = 512
HY_BANDS = 16
HY_EMB = 1 + 2 * HY_BANDS
HY_FILTER_HIDDEN = 64
HY_DECAY_TARGET = 1e-2
HY_QUICK_DECAY_PCT = 0.3
HY_SLOW_DECAY_PCT = 1.5
SWA_HEADS = 8
SWA_KV_HEADS = 2
SWA_HEAD_DIM = 64
SWA_WINDOW = 128
N_EXPERTS = 8
D_FF = 5632
D_FF_EXPERT = 7168

LANES = 128
MXU_DIM = 256
VMEM_LIMIT_BYTES = 56 * 1024 * 1024
NEG_BIG = -1e30
MOD_ROWS = 8

_O_CQ = 0
_O_CKV = _O_CQ + MLA_Q_LORA
_O_KPE = _O_CKV + MLA_KV_LORA
_O_HY = _O_KPE + MLA_ROPE
_O_SQ = _O_HY + 3 * HY_WIDTH
_O_SK = _O_SQ + SWA_HEADS * SWA_HEAD_DIM
_O_SV = _O_SK + SWA_KV_HEADS * SWA_HEAD_DIM


def _cparams(*sem):
    return pltpu.CompilerParams(dimension_semantics=sem, vmem_limit_bytes=VMEM_LIMIT_BYTES)


def _dot(a, b):
    return jnp.dot(a, b, preferred_element_type=F32)


def _dot_nt(a, b):
    return lax.dot_general(a, b, (((1,), (1,)), ((), ())), preferred_element_type=F32)


def _split_bf16(a):
    hi = a.astype(BF16)
    lo = (a - hi.astype(F32)).astype(BF16)
    return hi, lo


def _dot3(a, b):
    ah, al = _split_bf16(a)
    bh, bl = _split_bf16(b)
    return _dot(ah, bh) + (_dot(al, bh) + _dot(ah, bl))


def _silu(x):
    return x / (1.0 + jnp.exp(-x))


def _rms(x, g):
    ms = jnp.mean(x * x, axis=-1, keepdims=True)
    return x * lax.rsqrt(ms + NORM_EPS) * g


def _const_spec(shape):
    nd = len(shape)
    return pl.BlockSpec(shape, lambda *_: (0,) * nd, pipeline_mode=pl.Buffered(1))


def _mod_kernel(c_ref, w_ref, b_ref, o_ref):
    ah, al = _split_bf16(_silu(c_ref[...]))
    wh, wl = _split_bf16(w_ref[0])
    r = _dot(jnp.concatenate([ah, al], axis=0), wh)
    o_ref[0] = r[:MOD_ROWS] + (r[MOD_ROWS:] + _dot(ah, wl)) + b_ref[0]


def _modulation(cvec, w_mod, b_mod):
    tn = 1024
    n = 6 * D_MODEL
    return pl.pallas_call(
        _mod_kernel,
        out_shape=jax.ShapeDtypeStruct((DEPTH, MOD_ROWS, n), F32),
        grid=(DEPTH, n // tn),
        in_specs=[
            pl.BlockSpec((MOD_ROWS, D_MODEL), lambda l, j: (0, 0)),
            pl.BlockSpec((1, D_MODEL, tn), lambda l, j: (l, 0, j)),
            pl.BlockSpec((1, 1, tn), lambda l, j: (l, 0, j)),
        ],
        out_specs=pl.BlockSpec((1, MOD_ROWS, tn), lambda l, j: (l, 0, j)),
        compiler_params=_cparams("arbitrary", "arbitrary"),
        name="adaln_mod",
    )(cvec, w_mod, b_mod.reshape(DEPTH, 1, n))


def _norm_kernel(x_ref, g_ref, m_ref, o_ref, *, si):
    y = _rms(x_ref[...], g_ref[...])
    if si is not None:
        y = y * (1.0 + m_ref[si + 1:si + 2, :]) + m_ref[si:si + 1, :]
    o_ref[...] = y.astype(o_ref.dtype)


def _norm_mod(x, g, mods, groups, row0, si, out_dtype, tm=512):
    m = x.shape[0]
    nr = m // groups // tm
    return pl.pallas_call(
        functools.partial(_norm_kernel, si=si),
        out_shape=jax.ShapeDtypeStruct((m, D_MODEL), out_dtype),
        grid=(groups, nr),
        in_specs=[
            pl.BlockSpec((tm, D_MODEL), lambda g_, i: (g_ * nr + i, 0)),
            pl.BlockSpec((1, D_MODEL), lambda g_, i: (0, 0)),
            pl.BlockSpec((None, 6, D_MODEL), lambda g_, i: (row0 + g_, 0, 0)),
        ],
        out_specs=pl.BlockSpec((tm, D_MODEL), lambda g_, i: (g_ * nr + i, 0)),
        compiler_params=_cparams("parallel", "parallel"),
        name="norm_mod",
    )(x, g.reshape(1, D_MODEL), mods)


def _rope_halves(pe_pair, tab):
    r = pe_pair * tab
    return r + pltpu.roll(r, MLA_ROPE, axis=1)


MLA_Q_SCALE = (MLA_NOPE + MLA_ROPE) ** -0.5 * math.log2(math.e)


def _qproj_body(h_ref, wc_ref, g_ref, wq_ref, tab_ref, o_ref):
    cq = _dot(h_ref[...], wc_ref[...])
    cqn = _rms(cq, g_ref[...]).astype(BF16)
    tab = tab_ref[...] * MLA_Q_SCALE
    for hh in range(MLA_HEADS):
        r = _dot(cqn, wq_ref[hh])
        o_ref[hh, :, 0:MLA_NOPE] = (r[:, :MLA_NOPE] * MLA_Q_SCALE).astype(BF16)
        o_ref[hh, :, MLA_NOPE:] = _rope_halves(r[:, MLA_NOPE:], tab).astype(BF16)


def _mla_proj_kernel(h_ref, wcq_ref, gq_ref, wq_ref, wckv_ref, gkv_ref, wkv_ref, tab_ref, q_ref, k_ref, v_ref):
    _qproj_body(h_ref, wcq_ref, gq_ref, wq_ref, tab_ref, q_ref)
    _kvproj_kernel(h_ref, wckv_ref, gkv_ref, wkv_ref, tab_ref, k_ref, v_ref)


def _mla_proj(h, lw, tab, tm=512):
    m = h.shape[0]
    nt = tab.shape[0] // tm
    head_out = jax.ShapeDtypeStruct((MLA_HEADS, m, MXU_DIM), BF16)
    head_spec = pl.BlockSpec((MLA_HEADS, tm, MXU_DIM), lambda i: (0, i, 0))
    return pl.pallas_call(
        _mla_proj_kernel,
        out_shape=(head_out, head_out, head_out),
        grid=(m // tm,),
        in_specs=[
            pl.BlockSpec((tm, D_MODEL), lambda i: (i, 0)),
            _const_spec(lw["w_cq"].shape), _const_spec((1, MLA_Q_LORA)), _const_spec(lw["w_q"].shape),
            _const_spec(lw["w_ckv"].shape), _const_spec((1, MLA_KV_LORA)), _const_spec(lw["w_kv"].shape),
            pl.BlockSpec((tm, LANES), lambda i: (i % nt, 0)),
        ],
        out_specs=(head_spec, head_spec, head_spec),
        compiler_params=_cparams("parallel"),
        name="mla_proj",
    )(h, lw["w_cq"], lw["g_q"].reshape(1, -1), lw["w_q"], lw["w_ckv"], lw["g_kv"].reshape(1, -1), lw["w_kv"], tab)


def _kvproj_kernel(h_ref, wc_ref, g_ref, wkv_ref, tab_ref, k_ref, v_ref):
    t = _dot(h_ref[...], wc_ref[...])
    ckvn = _rms(t[:, :MLA_KV_LORA], g_ref[...]).astype(BF16)
    rot = _rope_halves(t[:, MLA_KV_LORA:], tab_ref[...])
    lane = lax.broadcasted_iota(jnp.int32, rot.shape, 1)
    kpe = jnp.where(lane < MLA_ROPE, rot, 0.0).astype(BF16)
    for hh in range(MLA_HEADS):
        r = _dot(ckvn, wkv_ref[hh])
        k_ref[hh, :, 0:MLA_NOPE] = r[:, :MLA_NOPE].astype(BF16)
        k_ref[hh, :, MLA_NOPE:] = kpe
        v_ref[hh, :, 0:MLA_V] = r[:, MLA_NOPE:].astype(BF16)
        v_ref[hh, :, MLA_V:] = jnp.ones((r.shape[0], MXU_DIM - MLA_V), BF16)


def _kv_proj(h, w_ckv, g_kv, w_kv, tab, tm=512):
    m = h.shape[0]
    nt = tab.shape[0] // tm
    return pl.pallas_call(
        _kvproj_kernel,
        out_shape=(jax.ShapeDtypeStruct((MLA_HEADS, m, MXU_DIM), BF16),
                   jax.ShapeDtypeStruct((MLA_HEADS, m, MXU_DIM), BF16)),
        grid=(m // tm,),
        in_specs=[
            pl.BlockSpec((tm, D_MODEL), lambda i: (i, 0)),
            _const_spec(w_ckv.shape),
            _const_spec((1, MLA_KV_LORA)),
            _const_spec(w_kv.shape),
            pl.BlockSpec((tm, LANES), lambda i: (i % nt, 0)),
        ],
        out_specs=(pl.BlockSpec((MLA_HEADS, tm, MXU_DIM), lambda i: (0, i, 0)),
                   pl.BlockSpec((MLA_HEADS, tm, MXU_DIM), lambda i: (0, i, 0))),
        compiler_params=_cparams("parallel"),
        name="mla_kv_proj",
    )(h, w_ckv, g_kv.reshape(1, -1), w_kv, tab)


MLA_KEY_CHUNK = 512


def _mla_kernel(q_ref, *refs, seg_lens):
    nseg = len(seg_lens)
    o_ref = refs[2 * nseg]
    q = q_ref[...]
    m = jnp.full((q.shape[0], 1), NEG_BIG, F32)
    acc = jnp.zeros((q.shape[0], MXU_DIM), F32)
    for s, lk in enumerate(seg_lens):
        k_ref, v_ref = refs[2 * s], refs[2 * s + 1]
        step = min(MLA_KEY_CHUNK, lk)
        for c0 in range(0, lk, step):
            sc = _dot_nt(q, k_ref[c0:c0 + step, :])
            m_new = jnp.maximum(m, sc.max(axis=-1, keepdims=True))
            p = jnp.exp2(sc - m_new).astype(BF16)
            acc = acc * jnp.exp2(m - m_new) + _dot(p, v_ref[c0:c0 + step, :])
            m = m_new
    o_ref[...] = (acc[:, :MLA_V] / acc[:, MLA_V:MLA_V + 1]).astype(o_ref.dtype)


def _mla_attention(q, segs, lq, tq=512):
    nq = lq // tq
    in_specs = [pl.BlockSpec((None, tq, MXU_DIM), lambda b, h, i: (h, b * nq + i, 0))]
    args = [q]
    for k, v, lk in segs:
        in_specs.append(pl.BlockSpec((None, lk, MXU_DIM), lambda b, h, i: (h, b, 0)))
        in_specs.append(pl.BlockSpec((None, lk, MXU_DIM), lambda b, h, i: (h, b, 0)))
        args += [k, v]
    return pl.pallas_call(
        functools.partial(_mla_kernel, seg_lens=tuple(lk for _, _, lk in segs)),
        out_shape=jax.ShapeDtypeStruct((BATCH * lq, MLA_HEADS * MLA_V), BF16),
        grid=(BATCH, MLA_HEADS, nq),
        in_specs=in_specs,
        out_specs=pl.BlockSpec((tq, MLA_V), lambda b, h, i: (b * nq + i, h)),
        compiler_params=_cparams("parallel", "parallel", "arbitrary"),
        name="mla_attention",
    )(*args)


def _dft_tables(n_len):
    n2 = 2 * n_len
    idx = np.arange(n_len, dtype=np.int64)
    ang = (2.0 * np.pi / n2) * ((idx[:, None] * idx[None, :]) % n2).astype(np.float64)
    cm = np.cos(ang)
    sf = np.sin(ang)
    sf[0, :] = 1.0 - 2.0 * (idx % 2)
    return (jnp.asarray(cm, dtype=BF16), jnp.asarray(sf, dtype=BF16),
            jnp.asarray(sf.T.copy(), dtype=BF16))


def _filter_tables(n_len):
    pos = np.arange(n_len, dtype=np.float64)
    t = pos / max(n_len - 1, 1)
    bands = np.linspace(1e-4, HY_BANDS - 1, HY_BANDS)
    ang = (2.0 * math.pi / n_len) * pos[:, None] * bands[None]
    z = np.zeros((n_len, LANES), np.float64)
    z[:, 0] = t
    z[:, 1:1 + HY_BANDS] = np.cos(ang)
    z[:, 1 + HY_BANDS:HY_EMB] = -np.sin(ang)
    deltas = np.linspace(math.log(HY_DECAY_TARGET) / HY_SLOW_DECAY_PCT,
                         math.log(HY_DECAY_TARGET) / HY_QUICK_DECAY_PCT, HY_WIDTH)
    decay = np.exp(-t[:, None] * np.abs(deltas)[None])
    return jnp.asarray(z, dtype=F32), jnp.asarray(decay, dtype=F32)


def _hy_filter_kernel(z_ref, w1_ref, b1_ref, w2_ref, b2_ref, w3_ref, b3_ref, wf_ref, wb_ref,
                      fr_ref, dec_ref, cm_ref, sf_ref, a_ref, b_ref, d_ref, hid_ref, *, n_len):
    @pl.when(pl.program_id(0) == 0)
    def _():
        fr = fr_ref[...]
        h = jnp.sin(fr * (_dot3(z_ref[...], w1_ref[...]) + b1_ref[...]))
        h = jnp.sin(fr * (_dot3(h, w2_ref[...]) + b2_ref[...]))
        hid_ref[...] = jnp.sin(fr * (_dot3(h, w3_ref[...]) + b3_ref[...]))

    h = hid_ref[...]
    dec = dec_ref[...]
    h_f = _dot3(h, wf_ref[...]) * dec
    h_b = _dot3(h, wb_ref[...]) * dec
    row = lax.broadcasted_iota(jnp.int32, h_f.shape, 0)
    h_b = jnp.where(row == 0, 0.0, h_b)
    hs = h_f + h_b
    hd = h_f - h_b
    nc = hs.shape[1]
    hs2 = jnp.concatenate(_split_bf16(hs), axis=1)
    hd2 = jnp.concatenate(_split_bf16(hd), axis=1)
    sf = sf_ref[...]
    t2 = _dot(cm_ref[...], hs2)
    g2 = _dot(sf, hd2)
    n2 = _dot(sf[0:16, :], hs2)[0:1, :]
    t_re = t2[:, :nc] + t2[:, nc:]
    g_t = g2[:, :nc] + g2[:, nc:]
    t_ny = n2[:, :nc] + n2[:, nc:]
    inv_n = 1.0 / (2 * n_len)
    first = row == 0
    a_ref[...] = jnp.where(first, inv_n, 2.0 * inv_n) * t_re
    b_ref[...] = jnp.where(first, 0.0, -2.0 * inv_n * g_t)
    d_ref[...] = jnp.where(first, inv_n * t_ny, 2.0 * inv_n * t_re)


def _hy_filters(lp, n_len, tabs):
    z, decay = tabs["filt"]
    cm, sf, _ = tabs["dft"]
    cb = LANES
    nblk = HY_WIDTH // cb
    w1 = jnp.zeros((LANES, HY_FILTER_HIDDEN), F32).at[:HY_EMB].set(lp["hy_w1"])
    row = lambda a: a.reshape(1, -1)
    small = lambda shape: pl.BlockSpec(shape, lambda j: (0,) * len(shape))
    fh = HY_FILTER_HIDDEN
    out = jax.ShapeDtypeStruct((n_len, HY_WIDTH), F32)
    return pl.pallas_call(
        functools.partial(_hy_filter_kernel, n_len=n_len),
        out_shape=(out, out, out),
        grid=(nblk,),
        in_specs=[
            small((n_len, LANES)), small((LANES, fh)), small((1, fh)), small((fh, fh)), small((1, fh)),
            small((fh, fh)), small((1, fh)),
            pl.BlockSpec((fh, cb), lambda j: (0, j)),
            pl.BlockSpec((fh, cb), lambda j: (0, nblk + j)),
            small((1, fh)),
            pl.BlockSpec((n_len, cb), lambda j: (0, j)),
            _const_spec((n_len, n_len)), _const_spec((n_len, n_len)),
        ],
        out_specs=tuple(pl.BlockSpec((n_len, cb), lambda j: (0, j)) for _ in range(3)),
        scratch_shapes=[pltpu.VMEM((n_len, fh), F32)],
        compiler_params=_cparams("arbitrary"),
        name="hyena_filters",
    )(z, w1, row(lp["hy_b1"]), lp["hy_w2"], row(lp["hy_b2"]), lp["hy_w3"], row(lp["hy_b3"]),
      lp["hy_w_filt"], lp["hy_w_filt"], row(lp["hy_freq"]), decay, cm, sf)


def _hy_pre_kernel(u0_ref, u1_ref, u2_ref, w0_ref, w1_ref, w2_ref, b0_ref, b1_ref, b2_ref,
                   x0_ref, vx_ref, vxb_ref, *, n_len):
    def sconv(u_ref, w_ref, b_ref):
        u = u_ref[...]
        w = w_ref[...]
        row = lax.broadcasted_iota(jnp.int32, u.shape, 0)
        prev = jnp.where(row == 0, 0.0, pltpu.roll(u, 1, axis=0))
        nxt = jnp.where(row == n_len - 1, 0.0, pltpu.roll(u, n_len - 1, axis=0))
        return w[0:1] * prev + w[1:2] * u + w[2:3] * nxt + b_ref[...]

    x0_ref[...] = sconv(u0_ref, w0_ref, b0_ref)
    vx = sconv(u2_ref, w2_ref, b2_ref) * sconv(u1_ref, w1_ref, b1_ref)
    vx_ref[...] = vx
    vxb_ref[...] = vx.astype(BF16)


def _hy_pre(hy, conv_w, conv_b, n_len):
    nblk = HY_WIDTH // LANES
    uspec = lambda part: pl.BlockSpec((None, n_len, LANES), lambda b, j: (b, 0, part * nblk + j))
    wspec = lambda part: pl.BlockSpec((3, LANES), lambda b, j: (0, part * nblk + j))
    bspec = lambda part: pl.BlockSpec((1, LANES), lambda b, j: (0, part * nblk + j))
    ospec = pl.BlockSpec((None, n_len, LANES), lambda b, j: (b, 0, j))
    hy3 = hy.reshape(BATCH, n_len, 3 * HY_WIDTH)
    cb = conv_b.reshape(1, -1)
    return pl.pallas_call(
        functools.partial(_hy_pre_kernel, n_len=n_len),
        out_shape=(jax.ShapeDtypeStruct((BATCH, n_len, HY_WIDTH), F32),
                   jax.ShapeDtypeStruct((BATCH, n_len, HY_WIDTH), F32),
                   jax.ShapeDtypeStruct((BATCH, n_len, HY_WIDTH), BF16)),
        grid=(BATCH, nblk),
        in_specs=[uspec(0), uspec(1), uspec(2), wspec(0), wspec(1), wspec(2), bspec(0), bspec(1), bspec(2)],
        out_specs=(ospec, ospec, ospec),
        compiler_params=_cparams("parallel", "parallel"),
        name="hyena_short_conv",
    )(hy3, hy3, hy3, conv_w, conv_w, conv_w, cb, cb, cb)


def _hy_fwd_kernel(cm_ref, sf_ref, u_ref, a_ref, b_ref, d_ref, q_ref):
    u = u_ref[...]
    ur = _dot(cm_ref[...], u)
    g = _dot(sf_ref[...], u)
    b = b_ref[...]
    q_ref[0] = (a_ref[...] * ur + b * g).astype(BF16)
    q_ref[1] = (d_ref[...] * g - b * ur).astype(BF16)


def _hy_inv_kernel(cm_ref, si_ref, q_ref, x0_ref, vx_ref, skip_ref, o_ref):
    y = _dot(cm_ref[...], q_ref[0]) + _dot(si_ref[...], q_ref[1])
    o_ref[...] = (x0_ref[...] * (y + vx_ref[...] * skip_ref[...])).astype(o_ref.dtype)


def _hy_long_conv(x0, vx, vxb, spectra, skip, n_len, tabs):
    cm, sf, si = tabs["dft"]
    a, b, d = spectra
    tf = min(1024, n_len)
    nf = n_len // tf
    mat = pl.BlockSpec((tf, n_len), lambda bb, f: (f, 0))
    spec = pl.BlockSpec((tf, HY_WIDTH), lambda bb, f: (f, 0))
    q = pl.pallas_call(
        _hy_fwd_kernel,
        out_shape=jax.ShapeDtypeStruct((BATCH, 2, n_len, HY_WIDTH), BF16),
        grid=(BATCH, nf),
        in_specs=[mat, mat, pl.BlockSpec((None, n_len, HY_WIDTH), lambda bb, f: (bb, 0, 0)), spec, spec, spec],
        out_specs=pl.BlockSpec((None, 2, tf, HY_WIDTH), lambda bb, f: (bb, 0, f, 0)),
        compiler_params=_cparams("parallel", "arbitrary"),
        name="hyena_dft_fwd",
    )(cm, sf, vxb, a, b, d)
    tile = pl.BlockSpec((None, tf, HY_WIDTH), lambda bb, f: (bb, f, 0))
    return pl.pallas_call(
        _hy_inv_kernel,
        out_shape=jax.ShapeDtypeStruct((BATCH, n_len, HY_WIDTH), BF16),
        grid=(BATCH, nf),
        in_specs=[mat, mat, pl.BlockSpec((None, 2, n_len, HY_WIDTH), lambda bb, f: (bb, 0, 0, 0)),
                  tile, tile, pl.BlockSpec((1, HY_WIDTH), lambda bb, f: (0, 0))],
        out_specs=tile,
        compiler_params=_cparams("parallel", "arbitrary"),
        name="hyena_dft_inv",
    )(cm, si, q, x0, vx, skip.reshape(1, -1))


_SWA_Q = SWA_HEADS * SWA_HEAD_DIM
_SWA_KV = SWA_KV_HEADS * SWA_HEAD_DIM


def _swaproj_kernel(h_ref, w_ref, cos_ref, sin_ref, q_ref, k_ref, v_ref, *, rope):
    t = _dot(h_ref[...], w_ref[...])
    q = t[:, :_SWA_Q]
    k = t[:, 2 * _SWA_Q:2 * _SWA_Q + _SWA_KV]
    v = t[:, 2 * _SWA_Q + 2 * _SWA_KV:]
    if rope:
        cos = cos_ref[...]
        sin = sin_ref[...]
        q = q * cos + t[:, _SWA_Q:2 * _SWA_Q] * sin
        k = k * cos[:, :_SWA_KV] + t[:, 2 * _SWA_Q + _SWA_KV:2 * _SWA_Q + 2 * _SWA_KV] * sin[:, :_SWA_KV]
    q_ref[...] = (q * (SWA_HEAD_DIM ** -0.5 * math.log2(math.e))).astype(BF16)
    lo = lax.broadcasted_iota(jnp.int32, k.shape, 1) < SWA_HEAD_DIM
    for src, dst in ((k, k_ref), (v, v_ref)):
        sw = pltpu.roll(src, SWA_HEAD_DIM, axis=1)
        dst[:, 0 * LANES:1 * LANES] = jnp.where(lo, src, 0.0).astype(BF16)
        dst[:, 1 * LANES:2 * LANES] = jnp.where(lo, 0.0, sw).astype(BF16)
        dst[:, 2 * LANES:3 * LANES] = jnp.where(lo, sw, 0.0).astype(BF16)
        dst[:, 3 * LANES:4 * LANES] = jnp.where(lo, 0.0, src).astype(BF16)


def _hy_swa_kernel(h_ref, why_ref, w_ref, cos_ref, sin_ref, hy_ref, q_ref, k_ref, v_ref, *, rope):
    hy_ref[...] = _dot(h_ref[...], why_ref[...])
    _swaproj_kernel(h_ref, w_ref, cos_ref, sin_ref, q_ref, k_ref, v_ref, rope=rope)


def _swa_proj(h, w, cos_t, sin_t, rope, w_hy=None, tm=512):
    m = h.shape[0]
    nt = cos_t.shape[0] // tm
    o = jax.ShapeDtypeStruct((m, 4 * LANES), BF16)
    ospec = pl.BlockSpec((tm, 4 * LANES), lambda i: (i, 0))
    tspec = pl.BlockSpec((tm, _SWA_Q), lambda i: (i % nt, 0))
    hspec = pl.BlockSpec((tm, D_MODEL), lambda i: (i, 0))
    if w_hy is None:
        return pl.pallas_call(
            functools.partial(_swaproj_kernel, rope=rope),
            out_shape=(o, o, o),
            grid=(m // tm,),
            in_specs=[hspec, _const_spec(w.shape), tspec, tspec],
            out_specs=(ospec, ospec, ospec),
            compiler_params=_cparams("parallel"),
            name="swa_proj",
        )(h, w, cos_t, sin_t)
    n_hy = w_hy.shape[1]
    return pl.pallas_call(
        functools.partial(_hy_swa_kernel, rope=rope),
        out_shape=(jax.ShapeDtypeStruct((m, n_hy), F32), o, o, o),
        grid=(m // tm,),
        in_specs=[hspec, _const_spec(w_hy.shape), _const_spec(w.shape), tspec, tspec],
        out_specs=(pl.BlockSpec((tm, n_hy), lambda i: (i, 0)), ospec, ospec, ospec),
        compiler_params=_cparams("parallel"),
        name="hyena_swa_proj",
    )(h, w_hy, w, cos_t, sin_t)


def _swa_kernel(sink_ref, q_ref, *refs, tq, win, lk, has_lat):
    if has_lat:
        kl_ref, vl_ref, kc_ref, vc_ref, o_ref = refs
        t0 = pl.program_id(1) * tq
        start = pl.multiple_of(jnp.clip(t0 - SWA_WINDOW, 0, lk - win), SWA_WINDOW)
        qpos = t0 + lax.broadcasted_iota(jnp.int32, (tq, win), 0)
        kpos = start + lax.broadcasted_iota(jnp.int32, (tq, win), 1)
        valid = jnp.abs(qpos - kpos) <= SWA_WINDOW
        valid = jnp.concatenate([valid, valid], axis=0)
    else:
        kc_ref, vc_ref, o_ref = refs
    upper = lax.broadcasted_iota(jnp.int32, (2 * tq, 1), 0) < tq
    for g in range(SWA_KV_HEADS):
        j0 = 2 * g
        qb = jnp.concatenate([q_ref[:, j0 * LANES:(j0 + 1) * LANES],
                              q_ref[:, (j0 + 1) * LANES:(j0 + 2) * LANES]], axis=0)
        acc = None
        for par in range(2):
            c0 = (2 * g + par) * LANES
            sk = jnp.where(upper, sink_ref[2 * j0 + par], sink_ref[2 * j0 + 2 + par]) * math.log2(math.e)
            s_c = _dot_nt(qb, kc_ref[:, c0:c0 + LANES])
            m = jnp.maximum(s_c.max(axis=-1, keepdims=True), sk)
            if has_lat:
                s_l = _dot_nt(qb, kl_ref[pl.ds(start, win), c0:c0 + LANES])
                s_l = jnp.where(valid, s_l, NEG_BIG)
                m = jnp.maximum(m, s_l.max(axis=-1, keepdims=True))
            p_c = jnp.exp2(s_c - m)
            den = p_c.sum(axis=-1, keepdims=True) + jnp.exp2(sk - m)
            if has_lat:
                p_l = jnp.exp2(s_l - m)
                den = den + p_l.sum(axis=-1, keepdims=True)
            o = _dot(p_c.astype(BF16), vc_ref[:, c0:c0 + LANES])
            if has_lat:
                o = o + _dot(p_l.astype(BF16), vl_ref[pl.ds(start, win), c0:c0 + LANES])
            o = o * (1.0 / den)
            acc = o if acc is None else acc + o
        o_ref[:, j0 * LANES:(j0 + 1) * LANES] = acc[:tq].astype(o_ref.dtype)
        o_ref[:, (j0 + 1) * LANES:(j0 + 2) * LANES] = acc[tq:].astype(o_ref.dtype)


def _swa_attention(sink, q, lat, ctx, lq, tq=256):
    nq = lq // tq
    win = tq + 2 * SWA_WINDOW
    full = lambda n: pl.BlockSpec((n, 4 * LANES), lambda b, i: (b, 0))
    in_specs = [pl.BlockSpec(memory_space=pltpu.SMEM),
                pl.BlockSpec((tq, 4 * LANES), lambda b, i: (b * nq + i, 0))]
    args = [sink, q]
    if lat is not None:
        in_specs += [full(SEQ), full(SEQ)]
        args += list(lat)
    in_specs += [full(CTX_LEN), full(CTX_LEN)]
    args += list(ctx)
    return pl.pallas_call(
        functools.partial(_swa_kernel, tq=tq, win=win, lk=SEQ, has_lat=lat is not None),
        out_shape=jax.ShapeDtypeStruct((BATCH * lq, 4 * LANES), BF16),
        grid=(BATCH, nq),
        in_specs=in_specs,
        out_specs=pl.BlockSpec((tq, 4 * LANES), lambda b, i: (b * nq + i, 0)),
        compiler_params=_cparams("parallel", "arbitrary"),
        name="swa_attention",
    )(*args)


def _out_kernel(a_ref, y_ref, s_ref, w_ref, x_ref, m_ref, g_ref, xo_ref, ho_ref):
    na = MLA_HEADS * MLA_V
    acc = _dot(a_ref[...], w_ref[0:na, :])
    acc = acc + _dot(y_ref[...], w_ref[na:na + HY_WIDTH, :])
    acc = acc + _dot(s_ref[...], w_ref[na + HY_WIDTH:, :])
    x = x_ref[...] + m_ref[2:3, :] * acc
    xo_ref[...] = x
    ho_ref[...] = (_rms(x, g_ref[...]) * (1.0 + m_ref[4:5, :]) + m_ref[3:4, :]).astype(ho_ref.dtype)


def _out_proj(a, y, s, w_out, x, mods, g_ffn, groups, row0, h_dtype, tm=512):
    m = x.shape[0]
    nr = m // groups // tm
    tile = lambda n: pl.BlockSpec((tm, n), lambda g_, i: (g_ * nr + i, 0))
    return pl.pallas_call(
        _out_kernel,
        out_shape=(jax.ShapeDtypeStruct((m, D_MODEL), F32), jax.ShapeDtypeStruct((m, D_MODEL), h_dtype)),
        grid=(groups, nr),
        in_specs=[tile(a.shape[1]), tile(y.shape[1]), tile(s.shape[1]), _const_spec(w_out.shape),
                  tile(D_MODEL), pl.BlockSpec((None, 6, D_MODEL), lambda g_, i: (row0 + g_, 0, 0)),
                  pl.BlockSpec((1, D_MODEL), lambda g_, i: (0, 0))],
        out_specs=(tile(D_MODEL), tile(D_MODEL)),
        compiler_params=_cparams("parallel", "parallel"),
        name="out_proj",
    )(a, y, s, w_out, x, mods, g_ffn.reshape(1, -1))


_META_W1, _META_W2, _META_E1, _META_E2, _META_R1, _META_R2 = range(6)


def _route(h, is_first, w_ref, meta_ref, cnt_ref, carry_ref):
    @pl.when(is_first)
    def _():
        carry_ref[...] = jnp.zeros_like(carry_ref)

    logits = _dot3(h, w_ref[...])
    tm = logits.shape[0]
    lane = lax.broadcasted_iota(jnp.int32, logits.shape, 1).astype(F32)
    logits = jnp.where(lane < N_EXPERTS, logits, NEG_BIG)
    m1 = logits.max(axis=-1, keepdims=True)
    i1 = jnp.where(logits == m1, lane, float(LANES)).min(axis=-1, keepdims=True)
    rest = jnp.where(lane == i1, NEG_BIG, logits)
    m2 = rest.max(axis=-1, keepdims=True)
    i2 = jnp.where(rest == m2, lane, float(LANES)).min(axis=-1, keepdims=True)
    e2 = jnp.exp(m2 - m1)
    w1 = 1.0 / (1.0 + e2)
    hot = jnp.where((lane == i1) | (lane == i2), 1.0, 0.0)
    before = (lax.broadcasted_iota(jnp.int32, (tm, tm), 0) > lax.broadcasted_iota(jnp.int32, (tm, tm), 1))
    seen = _dot(before.astype(F32).astype(BF16), hot.astype(BF16)) + carry_ref[0:1, :]
    r1 = jnp.where(lane == i1, seen, 0.0).sum(axis=-1, keepdims=True)
    r2 = jnp.where(lane == i2, seen, 0.0).sum(axis=-1, keepdims=True)
    carry_ref[...] = carry_ref[...] + hot.sum(axis=0, keepdims=True)
    cnt_ref[...] = carry_ref[...]
    rec = jnp.zeros_like(logits)
    for k, v in ((_META_W1, w1), (_META_W2, e2 * w1), (_META_E1, i1), (_META_E2, i2), (_META_R1, r1), (_META_R2, r2)):
        rec = jnp.where(lane == float(k), v, rec)
    meta_ref[...] = rec


def _router_kernel(h_ref, w_ref, meta_ref, cnt_ref, carry_ref):
    _route(h_ref[...], pl.program_id(0) == 0, w_ref, meta_ref, cnt_ref, carry_ref)


def _router(h, w_router, tm=512):
    m = h.shape[0]
    wp = jnp.zeros((D_MODEL, LANES), F32).at[:, :N_EXPERTS].set(w_router)
    return pl.pallas_call(
        _router_kernel,
        out_shape=(jax.ShapeDtypeStruct((m, LANES), F32), jax.ShapeDtypeStruct((8, LANES), F32)),
        grid=(m // tm,),
        in_specs=[pl.BlockSpec((tm, D_MODEL), lambda i: (i, 0)),
                  pl.BlockSpec((D_MODEL, LANES), lambda i: (0, 0))],
        out_specs=(pl.BlockSpec((tm, LANES), lambda i: (i, 0)), pl.BlockSpec((8, LANES), lambda i: (0, 0))),
        scratch_shapes=[pltpu.VMEM((8, LANES), F32)],
        compiler_params=_cparams("arbitrary"),
        name="moe_router",
    )(h, wp)


MOE_TM = 1024
MOE_BLK = 128
MOE_ROWS = (MOE_TM, 768, 512, 256)


def _moe_plan(meta, counts, m):
    n_steps = -(-2 * m // MOE_TM) + N_EXPERTS
    cnt = counts[0, :N_EXPERTS].astype(jnp.int32)
    steps_e = (cnt + MOE_TM - 1) // MOE_TM
    ends = jnp.cumsum(steps_e)
    first = ends - steps_e
    total = ends[-1]
    e = meta[:, _META_E1:_META_E2 + 1].astype(jnp.int32)
    rank = meta[:, _META_R1:_META_R2 + 1].astype(jnp.int32)
    pos = (first * MOE_TM)[e] + rank
    s_idx = jnp.arange(n_steps, dtype=jnp.int32)
    step_e = jnp.minimum(jnp.searchsorted(ends, s_idx, side="right").astype(jnp.int32), N_EXPERTS - 1)
    valid = jnp.clip(cnt[step_e] - (s_idx - first[step_e]) * MOE_TM, 0, MOE_TM)
    valid = jnp.where(s_idx < total, valid, 0)
    return pos.reshape(-1), step_e, valid, n_steps


def _row_copy(src_ref, src_row, dst_ref, dst_row, sem):
    return pltpu.make_async_copy(src_ref.at[pl.ds(src_row, 1)], dst_ref.at[pl.ds(dst_row, 1)], sem)


def _invert_kernel(pos_ref, sv_ref, src_ref):
    def clear_step(s, c):
        def clear(i, c2):
            src_ref[s * MOE_TM + i] = 0
            return c2

        return lax.fori_loop(sv_ref[s], MOE_TM, clear, c)

    def place(t, c):
        p0 = pos_ref[2 * t]
        p1 = pos_ref[2 * t + 1]
        src_ref[p0] = t
        src_ref[p1] = t
        return c

    lax.fori_loop(0, sv_ref.shape[0], clear_step, 0)
    lax.fori_loop(0, pos_ref.shape[0] // 2, place, 0, unroll=8)


def _moe_invert(pos, valid, n_rows):
    smem = pl.BlockSpec(memory_space=pltpu.SMEM)
    return pl.pallas_call(
        _invert_kernel,
        out_shape=jax.ShapeDtypeStruct((n_rows,), jnp.int32),
        in_specs=[smem, smem], out_specs=smem,
        name="moe_invert",
    )(pos, valid)


def _block_wait(src_ref, dst_ref, rows, sem):
    pltpu.make_async_copy(src_ref.at[pl.ds(0, rows)], dst_ref.at[pl.ds(0, rows)], sem).wait()


def _moe_ffn_kernel(se_ref, sv_ref, src_ref, h_ref, wg_ref, wu_ref, wd_ref, o_ref, x32_ref, xb_ref, sem):
    del se_ref
    s = pl.program_id(0)
    f = pl.program_id(1)
    valid = sv_ref[s]

    def rows_used(v):
        r = jnp.where(v > 0, MOE_ROWS[-1], 0)
        for small, big in zip(MOE_ROWS[:0:-1], MOE_ROWS[-2::-1]):
            r = jnp.where(v > small, big, r)
        return r

    def blocks(v):
        return rows_used(v) // MOE_BLK

    def gather(step):
        base = step * MOE_TM

        def issue(b, c):
            for k in range(8):
                i = b * 8 + k
                _row_copy(h_ref, src_ref[base + i], x32_ref, i, sem).start()
            return c

        lax.fori_loop(0, rows_used(sv_ref[step]) // 8, issue, 0)

    @pl.when(f == 0)
    def _():
        @pl.when(s == 0)
        def _():
            gather(0)

        def land(i, c):
            _block_wait(h_ref, x32_ref, MOE_BLK, sem)
            return c

        def to_bf16(i, c):
            r0 = pl.multiple_of(i * MOE_BLK, MOE_BLK)
            xb_ref[pl.ds(r0, MOE_BLK), :] = x32_ref[pl.ds(r0, MOE_BLK), :].astype(BF16)
            return c

        lax.fori_loop(0, blocks(valid), land, 0)
        lax.fori_loop(0, blocks(valid), to_bf16, 0)

        @pl.when(s + 1 < pl.num_programs(0))
        def _():
            gather(s + 1)

        o_ref[...] = jnp.zeros_like(o_ref)

    def compute(rows):
        xs = xb_ref[0:rows, :]
        mid = _silu(_dot(xs, wg_ref[...].astype(BF16))) * _dot(xs, wu_ref[...].astype(BF16))
        o_ref[0:rows, :] += _dot(mid.astype(BF16), wd_ref[...].astype(BF16))

    for k, rows in enumerate(MOE_ROWS):
        lower = MOE_ROWS[k + 1] if k + 1 < len(MOE_ROWS) else 0

        @pl.when((valid > lower) & (valid <= rows))
        def _():
            compute(rows)


def _moe_ffn(h, src, w_gate, w_up, w_down, step_e, valid, n_steps, tf=512):
    assert all(r % MOE_BLK == 0 for r in MOE_ROWS) and MOE_ROWS[0] == MOE_TM
    n_ff = w_gate.shape[-1]
    nf = n_ff // tf
    fidx = lambda s, f, sv: jnp.where(sv[s] > 0, f, nf - 1)
    return pl.pallas_call(
        _moe_ffn_kernel,
        out_shape=jax.ShapeDtypeStruct((n_steps * MOE_TM, D_MODEL), F32),
        grid_spec=pltpu.PrefetchScalarGridSpec(
            num_scalar_prefetch=3, grid=(n_steps, nf),
            in_specs=[
                pl.BlockSpec(memory_space=pl.ANY),
                pl.BlockSpec((None, D_MODEL, tf), lambda s, f, se, sv, sr: (se[s], 0, fidx(s, f, sv))),
                pl.BlockSpec((None, D_MODEL, tf), lambda s, f, se, sv, sr: (se[s], 0, fidx(s, f, sv))),
                pl.BlockSpec((None, tf, D_MODEL), lambda s, f, se, sv, sr: (se[s], fidx(s, f, sv), 0)),
            ],
            out_specs=pl.BlockSpec((MOE_TM, D_MODEL), lambda s, f, se, sv, sr: (s, 0), pipeline_mode=pl.Buffered(1)),
            scratch_shapes=[pltpu.VMEM((MOE_TM, D_MODEL), F32), pltpu.VMEM((MOE_TM, D_MODEL), BF16),
                            pltpu.SemaphoreType.DMA(())]),
        compiler_params=_cparams("arbitrary", "arbitrary"),
        name="moe_experts",
    )(step_e, valid, src, h, w_gate, w_up, w_down)


def _combine_kernel(pos_ref, meta_ref, x_ref, m_ref, gf_ref, ys_ref, o_ref, buf_ref, sems, *, tm, final):
    t = pl.program_id(0)

    def request(tile):
        b = tile % 2

        def issue(i, c):
            for slot in range(2):
                _row_copy(ys_ref, pos_ref[2 * (tile * tm + i) + slot], buf_ref.at[b, slot], i, sems.at[b]).start()
            return c

        lax.fori_loop(0, tm, issue, 0, unroll=4)

    @pl.when(t == 0)
    def _():
        request(0)

    @pl.when(t + 1 < pl.num_programs(0))
    def _():
        request(t + 1)

    cur = t % 2
    for slot in range(2):
        _block_wait(ys_ref, buf_ref.at[cur, slot], tm, sems.at[cur])
    meta = meta_ref[...]
    y = meta[:, _META_W1:_META_W1 + 1] * buf_ref[cur, 0] + meta[:, _META_W2:_META_W2 + 1] * buf_ref[cur, 1]
    x = x_ref[...] + m_ref[5:6, :] * y
    if final:
        x = _rms(x, gf_ref[...])
    o_ref[...] = x


def _moe_combine(pos, meta, x, mods, ys, groups, row0, g_final, tm=512):
    m = x.shape[0]
    nr = m // groups // tm
    final = g_final is not None
    gf = (g_final if final else jnp.ones((D_MODEL,), F32)).reshape(1, D_MODEL)
    return pl.pallas_call(
        functools.partial(_combine_kernel, tm=tm, final=final),
        out_shape=jax.ShapeDtypeStruct((m, D_MODEL), F32),
        grid_spec=pltpu.PrefetchScalarGridSpec(
            num_scalar_prefetch=1, grid=(m // tm,),
            in_specs=[pl.BlockSpec((tm, LANES), lambda i, p: (i, 0)),
                      pl.BlockSpec((tm, D_MODEL), lambda i, p: (i, 0)),
                      pl.BlockSpec((None, 6, D_MODEL), lambda i, p: (row0 + i // nr, 0, 0)),
                      pl.BlockSpec((1, D_MODEL), lambda i, p: (0, 0)),
                      pl.BlockSpec(memory_space=pl.ANY)],
            out_specs=pl.BlockSpec((tm, D_MODEL), lambda i, p: (i, 0)),
            scratch_shapes=[pltpu.VMEM((2, 2, tm, D_MODEL), F32), pltpu.SemaphoreType.DMA((2,))]),
        compiler_params=_cparams("arbitrary"),
        name="moe_combine",
    )(pos, meta, x, mods, gf, ys)


def _moe(h, meta, counts, x, mods, groups, row0, w_gate, w_up, w_down, g_final):
    m = h.shape[0]
    pos, step_e, valid, n_steps = _moe_plan(meta, counts, m)
    src = _moe_invert(pos, valid, n_steps * MOE_TM)
    ys = _moe_ffn(h, src, w_gate, w_up, w_down, step_e, valid, n_steps)
    return _moe_combine(pos, meta, x, mods, ys, groups, row0, g_final)


def _ffn_kernel(h_ref, wg_ref, wu_ref, wd_ref, x_ref, m_ref, gn_ref, mn_ref, o_ref, *maybe_hn_ref):
    f = pl.program_id(2)

    @pl.when(f == 0)
    def _():
        o_ref[...] = jnp.zeros_like(o_ref)

    h = h_ref[...]
    mid = _silu(_dot(h, wg_ref[...])) * _dot(h, wu_ref[...])
    o_ref[...] += _dot(mid.astype(BF16), wd_ref[...])

    @pl.when(f == pl.num_programs(2) - 1)
    def _():
        x = x_ref[...] + m_ref[5:6, :] * o_ref[...]
        o_ref[...] = x
        for hn_ref in maybe_hn_ref:
            hn_ref[...] = (_rms(x, gn_ref[...]) * (1.0 + mn_ref[1:2, :]) + mn_ref[0:1, :]).astype(hn_ref.dtype)


def _ffn(h, w_gate, w_up, w_down, x, mods, groups, row0, nxt, tm, tf=512):
    m = x.shape[0]
    n_ff = w_gate.shape[1]
    nr = m // groups // tm
    g_next, mods_next = nxt if nxt is not None else (jnp.ones((D_MODEL,), F32), mods)
    once = pl.Buffered(1)
    tile = lambda n, mode=None: pl.BlockSpec((tm, n), lambda g_, i, f: (g_ * nr + i, 0), pipeline_mode=mode)
    mod_spec = pl.BlockSpec((None, 6, D_MODEL), lambda g_, i, f: (row0 + g_, 0, 0))
    out_shape = [jax.ShapeDtypeStruct((m, D_MODEL), F32)]
    out_specs = [tile(D_MODEL, once)]
    if nxt is not None:
        out_shape.append(jax.ShapeDtypeStruct((m, D_MODEL), BF16))
        out_specs.append(tile(D_MODEL, once))
    outs = pl.pallas_call(
        _ffn_kernel,
        out_shape=tuple(out_shape),
        grid=(groups, nr, n_ff // tf),
        in_specs=[tile(D_MODEL),
                  pl.BlockSpec((D_MODEL, tf), lambda g_, i, f: (0, f)),
                  pl.BlockSpec((D_MODEL, tf), lambda g_, i, f: (0, f)),
                  pl.BlockSpec((tf, D_MODEL), lambda g_, i, f: (f, 0)),
                  tile(D_MODEL, once), mod_spec,
                  pl.BlockSpec((1, D_MODEL), lambda g_, i, f: (0, 0)), mod_spec],
        out_specs=tuple(out_specs),
        compiler_params=_cparams("parallel", "parallel", "arbitrary"),
        name="swiglu_ffn",
    )(h, w_gate, w_up, w_down, x, mods, g_next.reshape(1, D_MODEL), mods_next)
    return outs if nxt is not None else (outs[0], None)


def _swap_halves(w):
    half = w.shape[-1] // 2
    return jnp.concatenate([w[..., half:], w[..., :half]], axis=-1)


def _swap_heads(w, heads, dim):
    k = w.shape[0]
    return _swap_halves(w.reshape(k, heads, dim)).reshape(k, heads * dim)


def _rope_tables():
    rows = SEQ // GRID_W
    row = np.repeat(np.arange(rows), GRID_W).astype(np.float32)
    col = np.tile(np.arange(GRID_W), rows).astype(np.float32)
    quarter = MLA_ROPE // 4
    freqs = (np.float32(ROPE_THETA) ** (-np.arange(quarter, dtype=np.float32) / quarter)).astype(np.float32)
    ang = np.concatenate([row[:, None] * freqs[None], col[:, None] * freqs[None]], axis=-1)
    cos = np.cos(ang.astype(np.float64))
    sin = np.sin(ang.astype(np.float64))
    cos2 = np.concatenate([cos, cos], axis=-1)
    sin2 = np.concatenate([-sin, sin], axis=-1)
    tab_lat = np.concatenate([cos2, sin2], axis=-1)
    tab_ctx = np.concatenate([np.ones((CTX_LEN, MLA_ROPE)), np.zeros((CTX_LEN, MLA_ROPE))], axis=-1)
    return dict(
        tab_lat=jnp.asarray(tab_lat, F32), tab_ctx=jnp.asarray(tab_ctx, F32),
        cos8=jnp.asarray(np.tile(cos2, (1, SWA_HEADS)), F32),
        sin8=jnp.asarray(np.tile(sin2, (1, SWA_HEADS)), F32),
        ones8=jnp.ones((CTX_LEN, _SWA_Q), F32), zeros8=jnp.zeros((CTX_LEN, _SWA_Q), F32),
    )


def _layer_weights(p, l):
    w_in = p["w_in"][l]
    cq = w_in[:, _O_CQ:_O_CKV]
    ckv = w_in[:, _O_CKV:_O_KPE]
    kpe = w_in[:, _O_KPE:_O_HY]
    hy = w_in[:, _O_HY:_O_SQ]
    sq = w_in[:, _O_SQ:_O_SK]
    sk = w_in[:, _O_SK:_O_SV]
    sv = w_in[:, _O_SV:]
    dq = MLA_NOPE + MLA_ROPE
    wq = p["w_q_up"][l].reshape(MLA_Q_LORA, MLA_HEADS, dq)
    wq = jnp.concatenate([wq, _swap_halves(wq[..., MLA_NOPE:])], axis=-1)
    wkv = p["w_kv_up"][l].reshape(MLA_KV_LORA, MLA_HEADS, MLA_NOPE + MLA_V)
    return dict(
        w_cq=cq.astype(BF16),
        w_ckv=jnp.concatenate([ckv, kpe, _swap_halves(kpe)], axis=-1).astype(BF16),
        w_hy=hy.astype(BF16),
        w_swa=jnp.concatenate([sq, _swap_heads(sq, SWA_HEADS, SWA_HEAD_DIM), sk,
                               _swap_heads(sk, SWA_KV_HEADS, SWA_HEAD_DIM), sv], axis=-1).astype(BF16),
        w_q=jnp.transpose(wq, (1, 0, 2)).astype(BF16),
        w_kv=jnp.transpose(wkv, (1, 0, 2)).astype(BF16),
        w_out=p["w_out"][l].astype(BF16),
        g_q=p["g_q"][l], g_kv=p["g_kv"][l],
        hy_conv_w=p["hy_conv_w"][l], hy_conv_b=p["hy_conv_b"][l],
        hy_w1=p["hy_w1"][l], hy_b1=p["hy_b1"][l], hy_w2=p["hy_w2"][l], hy_b2=p["hy_b2"][l],
        hy_w3=p["hy_w3"][l], hy_b3=p["hy_b3"][l], hy_w_filt=p["hy_w_filt"][l],
        hy_freq=p["hy_freq"][l], hy_skip=p["hy_skip"][l], swa_sink=p["swa_sink"][l],
    )


def _hyena(hy, lw, n_len, tabs):
    x0, vx, vxb = _hy_pre(hy, lw["hy_conv_w"], lw["hy_conv_b"], n_len)
    spectra = _hy_filters(lw, n_len, tabs)
    y = _hy_long_conv(x0, vx, vxb, spectra, lw["hy_skip"], n_len, tabs)
    return y.reshape(BATCH * n_len, HY_WIDTH)


def _mixer(h_lat, h_ctx, lw, rope, hy_tabs, need_ctx):
    q_l, k_l, v_l = _mla_proj(h_lat, lw, rope["tab_lat"], tm=1024)
    if need_ctx:
        q_c, k_c, v_c = _mla_proj(h_ctx, lw, rope["tab_ctx"], tm=CTX_LEN)
    else:
        k_c, v_c = _kv_proj(h_ctx, lw["w_ckv"], lw["g_kv"], lw["w_kv"], rope["tab_ctx"], tm=CTX_LEN)
    a_l = _mla_attention(q_l, [(k_l, v_l, SEQ), (k_c, v_c, CTX_LEN)], SEQ, tq=1024)
    hy_l, sq_l, kk_l, vv_l = _swa_proj(h_lat, lw["w_swa"], rope["cos8"], rope["sin8"], True, w_hy=lw["w_hy"],
                                       tm=1024)
    y_l = _hyena(hy_l, lw, SEQ, hy_tabs[SEQ])
    if need_ctx:
        hy_c, sq_c, kk_c, vv_c = _swa_proj(h_ctx, lw["w_swa"], rope["ones8"], rope["zeros8"], False,
                                           w_hy=lw["w_hy"], tm=CTX_LEN)
    else:
        sq_c, kk_c, vv_c = _swa_proj(h_ctx, lw["w_swa"], rope["ones8"], rope["zeros8"], False, tm=CTX_LEN)
    s_l = _swa_attention(lw["swa_sink"], sq_l, (kk_l, vv_l), (kk_c, vv_c), SEQ)
    if not need_ctx:
        return (a_l, y_l, s_l), None
    a_c = _mla_attention(q_c, [(k_c, v_c, CTX_LEN)], CTX_LEN, tq=CTX_LEN)
    y_c = _hyena(hy_c, lw, CTX_LEN, hy_tabs[CTX_LEN])
    s_c = _swa_attention(lw["swa_sink"], sq_c, None, (kk_c, vv_c), CTX_LEN, tq=CTX_LEN)
    return (a_l, y_l, s_l), (a_c, y_c, s_c)


def _forward(p):
    rope = _rope_tables()
    hy_tabs = {n: dict(dft=_dft_tables(n), filt=_filter_tables(n)) for n in (SEQ, CTX_LEN)}
    cvec = jnp.zeros((MOD_ROWS, D_MODEL), F32).at[:BATCH].set(p["c"]).at[BATCH].set(p["c_ctx"])
    mods_all = _modulation(cvec, p["w_mod"], p["b_mod"]).reshape(DEPTH, MOD_ROWS, 6, D_MODEL)

    x_lat = p["x"].reshape(BATCH * SEQ, D_MODEL)
    x_ctx = p["ctx"].reshape(BATCH * CTX_LEN, D_MODEL)
    h_in = [None, None]
    for l in range(DEPTH):
        last = l == DEPTH - 1
        mods = mods_all[l]
        lw = _layer_weights(p, l)
        h_lat = h_in[0] if h_in[0] is not None else _norm_mod(x_lat, p["g_mix"][l], mods, BATCH, 0, 0, BF16)
        h_ctx = h_in[1] if h_in[1] is not None else _norm_mod(x_ctx, p["g_mix"][l], mods, 1, BATCH, 0, BF16)
        mix_l, mix_c = _mixer(h_lat, h_ctx, lw, rope, hy_tabs, not last)
        streams = [(x_lat, mix_l, BATCH, 0, 1024)]
        if not last:
            streams.append((x_ctx, mix_c, 1, BATCH, 512))
        outs = []
        h_in = [None, None]
        i = l // 2
        dense = l % 2 == 0
        for n, (x, mix, groups, row0, ffn_tm) in enumerate(streams):
            if dense:
                x, h2 = _out_proj(*mix, lw["w_out"], x, mods, p["g_ffn"][l], groups, row0, BF16)
                nxt = None if last else (p["g_mix"][l + 1], mods_all[l + 1])
                x, h_in[n] = _ffn(h2, p["ffn_w_gate"][i].astype(BF16), p["ffn_w_up"][i].astype(BF16),
                                  p["ffn_w_down"][i].astype(BF16), x, mods, groups, row0, nxt, ffn_tm)
            else:
                x, h2 = _out_proj(*mix, lw["w_out"], x, mods, p["g_ffn"][l], groups, row0, F32)
                meta, counts = _router(h2, p["moe_router"][i])
                g_final = p["g_final"] if (last and n == 0) else None
                x = _moe(h2, meta, counts, x, mods, groups, row0, p["moe_w_gate"][i], p["moe_w_up"][i],
                         p["moe_w_down"][i], g_final)
            outs.append(x)
        x_lat = outs[0]
        if not last:
            x_ctx = outs[1]
    if DEPTH % 2 == 1:
        x_lat = _norm_mod(x_lat, p["g_final"], mods_all[0], BATCH, 0, None, F32)
    return x_lat.reshape(BATCH, SEQ, D_MODEL)


def kernel(x, c, ctx, c_ctx, w_mod, b_mod, g_mix, g_ffn, w_in, g_q, w_q_up, g_kv, w_kv_up, hy_conv_w, hy_conv_b, hy_w1, hy_b1, hy_w2, hy_b2, hy_w3, hy_b3, hy_w_filt, hy_freq, hy_skip, swa_sink, w_out, ffn_w_gate, ffn_w_up, ffn_w_down, moe_router, moe_w_gate, moe_w_up, moe_w_down, g_final):
    return _forward(dict(
        x=x, c=c, ctx=ctx, c_ctx=c_ctx, w_mod=w_mod, b_mod=b_mod, g_mix=g_mix, g_ffn=g_ffn, w_in=w_in,
        g_q=g_q, w_q_up=w_q_up, g_kv=g_kv, w_kv_up=w_kv_up, hy_conv_w=hy_conv_w, hy_conv_b=hy_conv_b,
        hy_w1=hy_w1, hy_b1=hy_b1, hy_w2=hy_w2, hy_b2=hy_b2, hy_w3=hy_w3, hy_b3=hy_b3,
        hy_w_filt=hy_w_filt, hy_freq=hy_freq, hy_skip=hy_skip, swa_sink=swa_sink, w_out=w_out,
        ffn_w_gate=ffn_w_gate, ffn_w_up=ffn_w_up, ffn_w_down=ffn_w_down, moe_router=moe_router,
        moe_w_gate=moe_w_gate, moe_w_up=moe_w_up, moe_w_down=moe_w_down, g_final=g_final))
```

```python
import functools
import math

import numpy as np
import jax
import jax.numpy as jnp
from jax import lax
from jax.experimental import pallas as pl
from jax.experimental.pallas import tpu as pltpu

F32 = jnp.float32
BF16 = jnp.bfloat16

D_MODEL = 2048
BATCH = 4
SEQ = 2048
DEPTH = 2
CTX_LEN = 256
GRID_W = 64
NORM_EPS = 1e-6
ROPE_THETA = 10000.0
MLA_HEADS = 8
MLA_NOPE = 128
MLA_ROPE = 64
MLA_V = 128
MLA_Q_LORA = 768
MLA_KV_LORA = 512
HY_WIDTH = 512
HY_BANDS = 16
HY_EMB = 1 + 2 * HY_BANDS
HY_FILTER_HIDDEN = 64
HY_DECAY_TARGET = 1e-2
HY_QUICK_DECAY_PCT = 0.3
HY_SLOW_DECAY_PCT = 1.5
SWA_HEADS = 8
SWA_KV_HEADS = 2
SWA_HEAD_DIM = 64
SWA_WINDOW = 128
N_EXPERTS = 8
D_FF = 5632
D_FF_EXPERT = 7168

LANES = 128
MXU_DIM = 256
VMEM_LIMIT_BYTES = 56 * 1024 * 1024
NEG_BIG = -1e30
MOD_ROWS = 8

_O_CQ = 0
_O_CKV = _O_CQ + MLA_Q_LORA
_O_KPE = _O_CKV + MLA_KV_LORA
_O_HY = _O_KPE + MLA_ROPE
_O_SQ = _O_HY + 3 * HY_WIDTH
_O_SK = _O_SQ + SWA_HEADS * SWA_HEAD_DIM
_O_SV = _O_SK + SWA_KV_HEADS * SWA_HEAD_DIM


def _cparams(*sem):
    return pltpu.CompilerParams(dimension_semantics=sem, vmem_limit_bytes=VMEM_LIMIT_BYTES)


def _dot(a, b):
    return jnp.dot(a, b, preferred_element_type=F32)


def _dot_nt(a, b):
    return lax.dot_general(a, b, (((1,), (1,)), ((), ())), preferred_element_type=F32)


def _split_bf16(a):
    hi = a.astype(BF16)
    lo = (a - hi.astype(F32)).astype(BF16)
    return hi, lo


def _dot3(a, b):
    ah, al = _split_bf16(a)
    bh, bl = _split_bf16(b)
    return _dot(ah, bh) + (_dot(al, bh) + _dot(ah, bl))


def _silu(x):
    return x / (1.0 + jnp.exp(-x))


def _rms(x, g):
    ms = jnp.mean(x * x, axis=-1, keepdims=True)
    return x * lax.rsqrt(ms + NORM_EPS) * g


def _const_spec(shape):
    nd = len(shape)
    return pl.BlockSpec(shape, lambda *_: (0,) * nd, pipeline_mode=pl.Buffered(1))


def _mod_kernel(c_ref, w_ref, b_ref, o_ref):
    ah, al = _split_bf16(_silu(c_ref[...]))
    wh, wl = _split_bf16(w_ref[0])
    r = _dot(jnp.concatenate([ah, al], axis=0), wh)
    o_ref[0] = r[:MOD_ROWS] + (r[MOD_ROWS:] + _dot(ah, wl)) + b_ref[0]


def _modulation(cvec, w_mod, b_mod):
    tn = 1024
    n = 6 * D_MODEL
    return pl.pallas_call(
        _mod_kernel,
        out_shape=jax.ShapeDtypeStruct((DEPTH, MOD_ROWS, n), F32),
        grid=(DEPTH, n // tn),
        in_specs=[
            pl.BlockSpec((MOD_ROWS, D_MODEL), lambda l, j: (0, 0)),
            pl.BlockSpec((1, D_MODEL, tn), lambda l, j: (l, 0, j)),
            pl.BlockSpec((1, 1, tn), lambda l, j: (l, 0, j)),
        ],
        out_specs=pl.BlockSpec((1, MOD_ROWS, tn), lambda l, j: (l, 0, j)),
        compiler_params=_cparams("arbitrary", "arbitrary"),
        name="adaln_mod",
    )(cvec, w_mod, b_mod.reshape(DEPTH, 1, n))


def _norm_kernel(x_ref, g_ref, m_ref, o_ref, *, si):
    y = _rms(x_ref[...], g_ref[...])
    if si is not None:
        y = y * (1.0 + m_ref[si + 1:si + 2, :]) + m_ref[si:si + 1, :]
    o_ref[...] = y.astype(o_ref.dtype)


def _norm_mod(x, g, mods, groups, row0, si, out_dtype, tm=512):
    m = x.shape[0]
    nr = m // groups // tm
    return pl.pallas_call(
        functools.partial(_norm_kernel, si=si),
        out_shape=jax.ShapeDtypeStruct((m, D_MODEL), out_dtype),
        grid=(groups, nr),
        in_specs=[
            pl.BlockSpec((tm, D_MODEL), lambda g_, i: (g_ * nr + i, 0)),
            pl.BlockSpec((1, D_MODEL), lambda g_, i: (0, 0)),
            pl.BlockSpec((None, 6, D_MODEL), lambda g_, i: (row0 + g_, 0, 0)),
        ],
        out_specs=pl.BlockSpec((tm, D_MODEL), lambda g_, i: (g_ * nr + i, 0)),
        compiler_params=_cparams("parallel", "parallel"),
        name="norm_mod",
    )(x, g.reshape(1, D_MODEL), mods)


def _rope_halves(pe_pair, tab):
    r = pe_pair * tab
    return r + pltpu.roll(r, MLA_ROPE, axis=1)


MLA_Q_SCALE = (MLA_NOPE + MLA_ROPE) ** -0.5 * math.log2(math.e)


def _qproj_body(h_ref, wc_ref, g_ref, wq_ref, tab_ref, o_ref):
    cq = _dot(h_ref[...], wc_ref[...])
    cqn = _rms(cq, g_ref[...]).astype(BF16)
    tab = tab_ref[...] * MLA_Q_SCALE
    for hh in range(MLA_HEADS):
        r = _dot(cqn, wq_ref[hh])
        o_ref[hh, :, 0:MLA_NOPE] = (r[:, :MLA_NOPE] * MLA_Q_SCALE).astype(BF16)
        o_ref[hh, :, MLA_NOPE:] = _rope_halves(r[:, MLA_NOPE:], tab).astype(BF16)


def _mla_proj_kernel(h_ref, wcq_ref, gq_ref, wq_ref, wckv_ref, gkv_ref, wkv_ref, tab_ref, q_ref, k_ref, v_ref):
    _qproj_body(h_ref, wcq_ref, gq_ref, wq_ref, tab_ref, q_ref)
    _kvproj_kernel(h_ref, wckv_ref, gkv_ref, wkv_ref, tab_ref, k_ref, v_ref)


def _mla_proj(h, lw, tab, tm=512):
    m = h.shape[0]
    nt = tab.shape[0] // tm
    head_out = jax.ShapeDtypeStruct((MLA_HEADS, m, MXU_DIM), BF16)
    head_spec = pl.BlockSpec((MLA_HEADS, tm, MXU_DIM), lambda i: (0, i, 0))
    return pl.pallas_call(
        _mla_proj_kernel,
        out_shape=(head_out, head_out, head_out),
        grid=(m // tm,),
        in_specs=[
            pl.BlockSpec((tm, D_MODEL), lambda i: (i, 0)),
            _const_spec(lw["w_cq"].shape), _const_spec((1, MLA_Q_LORA)), _const_spec(lw["w_q"].shape),
            _const_spec(lw["w_ckv"].shape), _const_spec((1, MLA_KV_LORA)), _const_spec(lw["w_kv"].shape),
            pl.BlockSpec((tm, LANES), lambda i: (i % nt, 0)),
        ],
        out_specs=(head_spec, head_spec, head_spec),
        compiler_params=_cparams("parallel"),
        name="mla_proj",
    )(h, lw["w_cq"], lw["g_q"].reshape(1, -1), lw["w_q"], lw["w_ckv"], lw["g_kv"].reshape(1, -1), lw["w_kv"], tab)


def _kvproj_kernel(h_ref, wc_ref, g_ref, wkv_ref, tab_ref, k_ref, v_ref):
    t = _dot(h_ref[...], wc_ref[...])
    ckvn = _rms(t[:, :MLA_KV_LORA], g_ref[...]).astype(BF16)
    rot = _rope_halves(t[:, MLA_KV_LORA:], tab_ref[...])
    lane = lax.broadcasted_iota(jnp.int32, rot.shape, 1)
    kpe = jnp.where(lane < MLA_ROPE, rot, 0.0).astype(BF16)
    for hh in range(MLA_HEADS):
        r = _dot(ckvn, wkv_ref[hh])
        k_ref[hh, :, 0:MLA_NOPE] = r[:, :MLA_NOPE].astype(BF16)
        k_ref[hh, :, MLA_NOPE:] = kpe
        v_ref[hh, :, 0:MLA_V] = r[:, MLA_NOPE:].astype(BF16)
        v_ref[hh, :, MLA_V:] = jnp.ones((r.shape[0], MXU_DIM - MLA_V), BF16)


def _kv_proj(h, w_ckv, g_kv, w_kv, tab, tm=512):
    m = h.shape[0]
    nt = tab.shape[0] // tm
    return pl.pallas_call(
        _kvproj_kernel,
        out_shape=(jax.ShapeDtypeStruct((MLA_HEADS, m, MXU_DIM), BF16),
                   jax.ShapeDtypeStruct((MLA_HEADS, m, MXU_DIM), BF16)),
        grid=(m // tm,),
        in_specs=[
            pl.BlockSpec((tm, D_MODEL), lambda i: (i, 0)),
            _const_spec(w_ckv.shape),
            _const_spec((1, MLA_KV_LORA)),
            _const_spec(w_kv.shape),
            pl.BlockSpec((tm, LANES), lambda i: (i % nt, 0)),
        ],
        out_specs=(pl.BlockSpec((MLA_HEADS, tm, MXU_DIM), lambda i: (0, i, 0)),
                   pl.BlockSpec((MLA_HEADS, tm, MXU_DIM), lambda i: (0, i, 0))),
        compiler_params=_cparams("parallel"),
        name="mla_kv_proj",
    )(h, w_ckv, g_kv.reshape(1, -1), w_kv, tab)


MLA_KEY_CHUNK = 512


def _mla_kernel(q_ref, *refs, seg_lens):
    nseg = len(seg_lens)
    o_ref = refs[2 * nseg]
    q = q_ref[...]
    m = jnp.full((q.shape[0], 1), NEG_BIG, F32)
    acc = jnp.zeros((q.shape[0], MXU_DIM), F32)
    for s, lk in enumerate(seg_lens):
        k_ref, v_ref = refs[2 * s], refs[2 * s + 1]
        step = min(MLA_KEY_CHUNK, lk)
        for c0 in range(0, lk, step):
            sc = _dot_nt(q, k_ref[c0:c0 + step, :])
            m_new = jnp.maximum(m, sc.max(axis=-1, keepdims=True))
            p = jnp.exp2(sc - m_new).astype(BF16)
            acc = acc * jnp.exp2(m - m_new) + _dot(p, v_ref[c0:c0 + step, :])
            m = m_new
    o_ref[...] = (acc[:, :MLA_V] / acc[:, MLA_V:MLA_V + 1]).astype(o_ref.dtype)


def _mla_attention(q, segs, lq, tq=512):
    nq = lq // tq
    in_specs = [pl.BlockSpec((None, tq, MXU_DIM), lambda b, h, i: (h, b * nq + i, 0))]
    args = [q]
    for k, v, lk in segs:
        in_specs.append(pl.BlockSpec((None, lk, MXU_DIM), lambda b, h, i: (h, b, 0)))
        in_specs.append(pl.BlockSpec((None, lk, MXU_DIM), lambda b, h, i: (h, b, 0)))
        args += [k, v]
    return pl.pallas_call(
        functools.partial(_mla_kernel, seg_lens=tuple(lk for _, _, lk in segs)),
        out_shape=jax.ShapeDtypeStruct((BATCH * lq, MLA_HEADS * MLA_V), BF16),
        grid=(BATCH, MLA_HEADS, nq),
        in_specs=in_specs,
        out_specs=pl.BlockSpec((tq, MLA_V), lambda b, h, i: (b * nq + i, h)),
        compiler_params=_cparams("parallel", "parallel", "arbitrary"),
        name="mla_attention",
    )(*args)


def _dft_tables(n_len):
    n2 = 2 * n_len
    idx = np.arange(n_len, dtype=np.int64)
    ang = (2.0 * np.pi / n2) * ((idx[:, None] * idx[None, :]) % n2).astype(np.float64)
    cm = np.cos(ang)
    sf = np.sin(ang)
    sf[0, :] = 1.0 - 2.0 * (idx % 2)
    return (jnp.asarray(cm, dtype=BF16), jnp.asarray(sf, dtype=BF16),
            jnp.asarray(sf.T.copy(), dtype=BF16))


def _filter_tables(n_len):
    pos = np.arange(n_len, dtype=np.float64)
    t = pos / max(n_len - 1, 1)
    bands = np.linspace(1e-4, HY_BANDS - 1, HY_BANDS)
    ang = (2.0 * math.pi / n_len) * pos[:, None] * bands[None]
    z = np.zeros((n_len, LANES), np.float64)
    z[:, 0] = t
    z[:, 1:1 + HY_BANDS] = np.cos(ang)
    z[:, 1 + HY_BANDS:HY_EMB] = -np.sin(ang)
    deltas = np.linspace(math.log(HY_DECAY_TARGET) / HY_SLOW_DECAY_PCT,
                         math.log(HY_DECAY_TARGET) / HY_QUICK_DECAY_PCT, HY_WIDTH)
    decay = np.exp(-t[:, None] * np.abs(deltas)[None])
    return jnp.asarray(z, dtype=F32), jnp.asarray(decay, dtype=F32)


def _hy_filter_kernel(z_ref, w1_ref, b1_ref, w2_ref, b2_ref, w3_ref, b3_ref, wf_ref, wb_ref,
                      fr_ref, dec_ref, cm_ref, sf_ref, a_ref, b_ref, d_ref, hid_ref, *, n_len):
    @pl.when(pl.program_id(0) == 0)
    def _():
        fr = fr_ref[...]
        h = jnp.sin(fr * (_dot3(z_ref[...], w1_ref[...]) + b1_ref[...]))
        h = jnp.sin(fr * (_dot3(h, w2_ref[...]) + b2_ref[...]))
        hid_ref[...] = jnp.sin(fr * (_dot3(h, w3_ref[...]) + b3_ref[...]))

    h = hid_ref[...]
    dec = dec_ref[...]
    h_f = _dot3(h, wf_ref[...]) * dec
    h_b = _dot3(h, wb_ref[...]) * dec
    row = lax.broadcasted_iota(jnp.int32, h_f.shape, 0)
    h_b = jnp.where(row == 0, 0.0, h_b)
    hs = h_f + h_b
    hd = h_f - h_b
    nc = hs.shape[1]
    hs2 = jnp.concatenate(_split_bf16(hs), axis=1)
    hd2 = jnp.concatenate(_split_bf16(hd), axis=1)
    sf = sf_ref[...]
    t2 = _dot(cm_ref[...], hs2)
    g2 = _dot(sf, hd2)
    n2 = _dot(sf[0:16, :], hs2)[0:1, :]
    t_re = t2[:, :nc] + t2[:, nc:]
    g_t = g2[:, :nc] + g2[:, nc:]
    t_ny = n2[:, :nc] + n2[:, nc:]
    inv_n = 1.0 / (2 * n_len)
    first = row == 0
    a_ref[...] = jnp.where(first, inv_n, 2.0 * inv_n) * t_re
    b_ref[...] = jnp.where(first, 0.0, -2.0 * inv_n * g_t)
    d_ref[...] = jnp.where(first, inv_n * t_ny, 2.0 * inv_n * t_re)


def _hy_filters(lp, n_len, tabs):
    z, decay = tabs["filt"]
    cm, sf, _ = tabs["dft"]
    cb = LANES
    nblk = HY_WIDTH // cb
    w1 = jnp.zeros((LANES, HY_FILTER_HIDDEN), F32).at[:HY_EMB].set(lp["hy_w1"])
    row = lambda a: a.reshape(1, -1)
    small = lambda shape: pl.BlockSpec(shape, lambda j: (0,) * len(shape))
    fh = HY_FILTER_HIDDEN
    out = jax.ShapeDtypeStruct((n_len, HY_WIDTH), F32)
    return pl.pallas_call(
        functools.partial(_hy_filter_kernel, n_len=n_len),
        out_shape=(out, out, out),
        grid=(nblk,),
        in_specs=[
            small((n_len, LANES)), small((LANES, fh)), small((1, fh)), small((fh, fh)), small((1, fh)),
            small((fh, fh)), small((1, fh)),
            pl.BlockSpec((fh, cb), lambda j: (0, j)),
            pl.BlockSpec((fh, cb), lambda j: (0, nblk + j)),
            small((1, fh)),
            pl.BlockSpec((n_len, cb), lambda j: (0, j)),
            _const_spec((n_len, n_len)), _const_spec((n_len, n_len)),
        ],
        out_specs=tuple(pl.BlockSpec((n_len, cb), lambda j: (0, j)) for _ in range(3)),
        scratch_shapes=[pltpu.VMEM((n_len, fh), F32)],
        compiler_params=_cparams("arbitrary"),
        name="hyena_filters",
    )(z, w1, row(lp["hy_b1"]), lp["hy_w2"], row(lp["hy_b2"]), lp["hy_w3"], row(lp["hy_b3"]),
      lp["hy_w_filt"], lp["hy_w_filt"], row(lp["hy_freq"]), decay, cm, sf)


def _hy_pre_kernel(u0_ref, u1_ref, u2_ref, w0_ref, w1_ref, w2_ref, b0_ref, b1_ref, b2_ref,
                   x0_ref, vx_ref, vxb_ref, *, n_len):
    def sconv(u_ref, w_ref, b_ref):
        u = u_ref[...]
        w = w_ref[...]
        row = lax.broadcasted_iota(jnp.int32, u.shape, 0)
        prev = jnp.where(row == 0, 0.0, pltpu.roll(u, 1, axis=0))
        nxt = jnp.where(row == n_len - 1, 0.0, pltpu.roll(u, n_len - 1, axis=0))
        return w[0:1] * prev + w[1:2] * u + w[2:3] * nxt + b_ref[...]

    x0_ref[...] = sconv(u0_ref, w0_ref, b0_ref)
    vx = sconv(u2_ref, w2_ref, b2_ref) * sconv(u1_ref, w1_ref, b1_ref)
    vx_ref[...] = vx
    vxb_ref[...] = vx.astype(BF16)


def _hy_pre(hy, conv_w, conv_b, n_len):
    nblk = HY_WIDTH // LANES
    uspec = lambda part: pl.BlockSpec((None, n_len, LANES), lambda b, j: (b, 0, part * nblk + j))
    wspec = lambda part: pl.BlockSpec((3, LANES), lambda b, j: (0, part * nblk + j))
    bspec = lambda part: pl.BlockSpec((1, LANES), lambda b, j: (0, part * nblk + j))
    ospec = pl.BlockSpec((None, n_len, LANES), lambda b, j: (b, 0, j))
    hy3 = hy.reshape(BATCH, n_len, 3 * HY_WIDTH)
    cb = conv_b.reshape(1, -1)
    return pl.pallas_call(
        functools.partial(_hy_pre_kernel, n_len=n_len),
        out_shape=(jax.ShapeDtypeStruct((BATCH, n_len, HY_WIDTH), F32),
                   jax.ShapeDtypeStruct((BATCH, n_len, HY_WIDTH), F32),
                   jax.ShapeDtypeStruct((BATCH, n_len, HY_WIDTH), BF16)),
        grid=(BATCH, nblk),
        in_specs=[uspec(0), uspec(1), uspec(2), wspec(0), wspec(1), wspec(2), bspec(0), bspec(1), bspec(2)],
        out_specs=(ospec, ospec, ospec),
        compiler_params=_cparams("parallel", "parallel"),
        name="hyena_short_conv",
    )(hy3, hy3, hy3, conv_w, conv_w, conv_w, cb, cb, cb)


def _hy_fwd_kernel(cm_ref, sf_ref, u_ref, a_ref, b_ref, d_ref, q_ref):
    u = u_ref[...]
    ur = _dot(cm_ref[...], u)
    g = _dot(sf_ref[...], u)
    b = b_ref[...]
    q_ref[0] = (a_ref[...] * ur + b * g).astype(BF16)
    q_ref[1] = (d_ref[...] * g - b * ur).astype(BF16)


def _hy_inv_kernel(cm_ref, si_ref, q_ref, x0_ref, vx_ref, skip_ref, o_ref):
    y = _dot(cm_ref[...], q_ref[0]) + _dot(si_ref[...], q_ref[1])
    o_ref[...] = (x0_ref[...] * (y + vx_ref[...] * skip_ref[...])).astype(o_ref.dtype)


def _hy_long_conv(x0, vx, vxb, spectra, skip, n_len, tabs):
    cm, sf, si = tabs["dft"]
    a, b, d = spectra
    tf = min(1024, n_len)
    nf = n_len // tf
    mat = pl.BlockSpec((tf, n_len), lambda bb, f: (f, 0))
    spec = pl.BlockSpec((tf, HY_WIDTH), lambda bb, f: (f, 0))
    q = pl.pallas_call(
        _hy_fwd_kernel,
        out_shape=jax.ShapeDtypeStruct((BATCH, 2, n_len, HY_WIDTH), BF16),
        grid=(BATCH, nf),
        in_specs=[mat, mat, pl.BlockSpec((None, n_len, HY_WIDTH), lambda bb, f: (bb, 0, 0)), spec, spec, spec],
        out_specs=pl.BlockSpec((None, 2, tf, HY_WIDTH), lambda bb, f: (bb, 0, f, 0)),
        compiler_params=_cparams("parallel", "arbitrary"),
        name="hyena_dft_fwd",
    )(cm, sf, vxb, a, b, d)
    tile = pl.BlockSpec((None, tf, HY_WIDTH), lambda bb, f: (bb, f, 0))
    return pl.pallas_call(
        _hy_inv_kernel,
        out_shape=jax.ShapeDtypeStruct((BATCH, n_len, HY_WIDTH), BF16),
        grid=(BATCH, nf),
        in_specs=[mat, mat, pl.BlockSpec((None, 2, n_len, HY_WIDTH), lambda bb, f: (bb, 0, 0, 0)),
                  tile, tile, pl.BlockSpec((1, HY_WIDTH), lambda bb, f: (0, 0))],
        out_specs=tile,
        compiler_params=_cparams("parallel", "arbitrary"),
        name="hyena_dft_inv",
    )(cm, si, q, x0, vx, skip.reshape(1, -1))


_SWA_Q = SWA_HEADS * SWA_HEAD_DIM
_SWA_KV = SWA_KV_HEADS * SWA_HEAD_DIM


def _swaproj_kernel(h_ref, w_ref, cos_ref, sin_ref, q_ref, k_ref, v_ref, *, rope):
    t = _dot(h_ref[...], w_ref[...])
    q = t[:, :_SWA_Q]
    k = t[:, 2 * _SWA_Q:2 * _SWA_Q + _SWA_KV]
    v = t[:, 2 * _SWA_Q + 2 * _SWA_KV:]
    if rope:
        cos = cos_ref[...]
        sin = sin_ref[...]
        q = q * cos + t[:, _SWA_Q:2 * _SWA_Q] * sin
        k = k * cos[:, :_SWA_KV] + t[:, 2 * _SWA_Q + _SWA_KV:2 * _SWA_Q + 2 * _SWA_KV] * sin[:, :_SWA_KV]
    q_ref[...] = (q * (SWA_HEAD_DIM ** -0.5 * math.log2(math.e))).astype(BF16)
    lo = lax.broadcasted_iota(jnp.int32, k.shape, 1) < SWA_HEAD_DIM
    for src, dst in ((k, k_ref), (v, v_ref)):
        sw = pltpu.roll(src, SWA_HEAD_DIM, axis=1)
        dst[:, 0 * LANES:1 * LANES] = jnp.where(lo, src, 0.0).astype(BF16)
        dst[:, 1 * LANES:2 * LANES] = jnp.where(lo, 0.0, sw).astype(BF16)
        dst[:, 2 * LANES:3 * LANES] = jnp.where(lo, sw, 0.0).astype(BF16)
        dst[:, 3 * LANES:4 * LANES] = jnp.where(lo, 0.0, src).astype(BF16)


def _hy_swa_kernel(h_ref, why_ref, w_ref, cos_ref, sin_ref, hy_ref, q_ref, k_ref, v_ref, *, rope):
    hy_ref[...] = _dot(h_ref[...], why_ref[...])
    _swaproj_kernel(h_ref, w_ref, cos_ref, sin_ref, q_ref, k_ref, v_ref, rope=rope)


def _swa_proj(h, w, cos_t, sin_t, rope, w_hy=None, tm=512):
    m = h.shape[0]
    nt = cos_t.shape[0] // tm
    o = jax.ShapeDtypeStruct((m, 4 * LANES), BF16)
    ospec = pl.BlockSpec((tm, 4 * LANES), lambda i: (i, 0))
    tspec = pl.BlockSpec((tm, _SWA_Q), lambda i: (i % nt, 0))
    hspec = pl.BlockSpec((tm, D_MODEL), lambda i: (i, 0))
    if w_hy is None:
        return pl.pallas_call(
            functools.partial(_swaproj_kernel, rope=rope),
            out_shape=(o, o, o),
            grid=(m // tm,),
            in_specs=[hspec, _const_spec(w.shape), tspec, tspec],
            out_specs=(ospec, ospec, ospec),
            compiler_params=_cparams("parallel"),
            name="swa_proj",
        )(h, w, cos_t, sin_t)
    n_hy = w_hy.shape[1]
    return pl.pallas_call(
        functools.partial(_hy_swa_kernel, rope=rope),
        out_shape=(jax.ShapeDtypeStruct((m, n_hy), F32), o, o, o),
        grid=(m // tm,),
        in_specs=[hspec, _const_spec(w_hy.shape), _const_spec(w.shape), tspec, tspec],
        out_specs=(pl.BlockSpec((tm, n_hy), lambda i: (i, 0)), ospec, ospec, ospec),
        compiler_params=_cparams("parallel"),
        name="hyena_swa_proj",
    )(h, w_hy, w, cos_t, sin_t)


def _swa_kernel(sink_ref, q_ref, *refs, tq, win, lk, has_lat):
    if has_lat:
        kl_ref, vl_ref, kc_ref, vc_ref, o_ref = refs
        t0 = pl.program_id(1) * tq
        start = pl.multiple_of(jnp.clip(t0 - SWA_WINDOW, 0, lk - win), SWA_WINDOW)
        qpos = t0 + lax.broadcasted_iota(jnp.int32, (tq, win), 0)
        kpos = start + lax.broadcasted_iota(jnp.int32, (tq, win), 1)
        valid = jnp.abs(qpos - kpos) <= SWA_WINDOW
        valid = jnp.concatenate([valid, valid], axis=0)
    else:
        kc_ref, vc_ref, o_ref = refs
    upper = lax.broadcasted_iota(jnp.int32, (2 * tq, 1), 0) < tq
    for g in range(SWA_KV_HEADS):
        j0 = 2 * g
        qb = jnp.concatenate([q_ref[:, j0 * LANES:(j0 + 1) * LANES],
                              q_ref[:, (j0 + 1) * LANES:(j0 + 2) * LANES]], axis=0)
        acc = None
        for par in range(2):
            c0 = (2 * g + par) * LANES
            sk = jnp.where(upper, sink_ref[2 * j0 + par], sink_ref[2 * j0 + 2 + par]) * math.log2(math.e)
            s_c = _dot_nt(qb, kc_ref[:, c0:c0 + LANES])
            m = jnp.maximum(s_c.max(axis=-1, keepdims=True), sk)
            if has_lat:
                s_l = _dot_nt(qb, kl_ref[pl.ds(start, win), c0:c0 + LANES])
                s_l = jnp.where(valid, s_l, NEG_BIG)
                m = jnp.maximum(m, s_l.max(axis=-1, keepdims=True))
            p_c = jnp.exp2(s_c - m)
            den = p_c.sum(axis=-1, keepdims=True) + jnp.exp2(sk - m)
            if has_lat:
                p_l = jnp.exp2(s_l - m)
                den = den + p_l.sum(axis=-1, keepdims=True)
            o = _dot(p_c.astype(BF16), vc_ref[:, c0:c0 + LANES])
            if has_lat:
                o = o + _dot(p_l.astype(BF16), vl_ref[pl.ds(start, win), c0:c0 + LANES])
            o = o * (1.0 / den)
            acc = o if acc is None else acc + o
        o_ref[:, j0 * LANES:(j0 + 1) * LANES] = acc[:tq].astype(o_ref.dtype)
        o_ref[:, (j0 + 1) * LANES:(j0 + 2) * LANES] = acc[tq:].astype(o_ref.dtype)


def _swa_attention(sink, q, lat, ctx, lq, tq=256):
    nq = lq // tq
    win = tq + 2 * SWA_WINDOW
    full = lambda n: pl.BlockSpec((n, 4 * LANES), lambda b, i: (b, 0))
    in_specs = [pl.BlockSpec(memory_space=pltpu.SMEM),
                pl.BlockSpec((tq, 4 * LANES), lambda b, i: (b * nq + i, 0))]
    args = [sink, q]
    if lat is not None:
        in_specs += [full(SEQ), full(SEQ)]
        args += list(lat)
    in_specs += [full(CTX_LEN), full(CTX_LEN)]
    args += list(ctx)
    return pl.pallas_call(
        functools.partial(_swa_kernel, tq=tq, win=win, lk=SEQ, has_lat=lat is not None),
        out_shape=jax.ShapeDtypeStruct((BATCH * lq, 4 * LANES), BF16),
        grid=(BATCH, nq),
        in_specs=in_specs,
        out_specs=pl.BlockSpec((tq, 4 * LANES), lambda b, i: (b * nq + i, 0)),
        compiler_params=_cparams("parallel", "arbitrary"),
        name="swa_attention",
    )(*args)


def _out_kernel(a_ref, y_ref, s_ref, w_ref, x_ref, m_ref, g_ref, xo_ref, ho_ref):
    na = MLA_HEADS * MLA_V
    acc = _dot(a_ref[...], w_ref[0:na, :])
    acc = acc + _dot(y_ref[...], w_ref[na:na + HY_WIDTH, :])
    acc = acc + _dot(s_ref[...], w_ref[na + HY_WIDTH:, :])
    x = x_ref[...] + m_ref[2:3, :] * acc
    xo_ref[...] = x
    ho_ref[...] = (_rms(x, g_ref[...]) * (1.0 + m_ref[4:5, :]) + m_ref[3:4, :]).astype(ho_ref.dtype)


def _out_proj(a, y, s, w_out, x, mods, g_ffn, groups, row0, h_dtype, tm=512):
    m = x.shape[0]
    nr = m // groups // tm
    tile = lambda n: pl.BlockSpec((tm, n), lambda g_, i: (g_ * nr + i, 0))
    return pl.pallas_call(
        _out_kernel,
        out_shape=(jax.ShapeDtypeStruct((m, D_MODEL), F32), jax.ShapeDtypeStruct((m, D_MODEL), h_dtype)),
        grid=(groups, nr),
        in_specs=[tile(a.shape[1]), tile(y.shape[1]), tile(s.shape[1]), _const_spec(w_out.shape),
                  tile(D_MODEL), pl.BlockSpec((None, 6, D_MODEL), lambda g_, i: (row0 + g_, 0, 0)),
                  pl.BlockSpec((1, D_MODEL), lambda g_, i: (0, 0))],
        out_specs=(tile(D_MODEL), tile(D_MODEL)),
        compiler_params=_cparams("parallel", "parallel"),
        name="out_proj",
    )(a, y, s, w_out, x, mods, g_ffn.reshape(1, -1))


_META_W1, _META_W2, _META_E1, _META_E2, _META_R1, _META_R2 = range(6)


def _route(h, is_first, w_ref, meta_ref, cnt_ref, carry_ref):
    @pl.when(is_first)
    def _():
        carry_ref[...] = jnp.zeros_like(carry_ref)

    logits = _dot3(h, w_ref[...])
    tm = logits.shape[0]
    lane = lax.broadcasted_iota(jnp.int32, logits.shape, 1).astype(F32)
    logits = jnp.where(lane < N_EXPERTS, logits, NEG_BIG)
    m1 = logits.max(axis=-1, keepdims=True)
    i1 = jnp.where(logits == m1, lane, float(LANES)).min(axis=-1, keepdims=True)
    rest = jnp.where(lane == i1, NEG_BIG, logits)
    m2 = rest.max(axis=-1, keepdims=True)
    i2 = jnp.where(rest == m2, lane, float(LANES)).min(axis=-1, keepdims=True)
    e2 = jnp.exp(m2 - m1)
    w1 = 1.0 / (1.0 + e2)
    hot = jnp.where((lane == i1) | (lane == i2), 1.0, 0.0)
    before = (lax.broadcasted_iota(jnp.int32, (tm, tm), 0) > lax.broadcasted_iota(jnp.int32, (tm, tm), 1))
    seen = _dot(before.astype(F32).astype(BF16), hot.astype(BF16)) + carry_ref[0:1, :]
    r1 = jnp.where(lane == i1, seen, 0.0).sum(axis=-1, keepdims=True)
    r2 = jnp.where(lane == i2, seen, 0.0).sum(axis=-1, keepdims=True)
    carry_ref[...] = carry_ref[...] + hot.sum(axis=0, keepdims=True)
    cnt_ref[...] = carry_ref[...]
    rec = jnp.zeros_like(logits)
    for k, v in ((_META_W1, w1), (_META_W2, e2 * w1), (_META_E1, i1), (_META_E2, i2), (_META_R1, r1), (_META_R2, r2)):
        rec = jnp.where(lane == float(k), v, rec)
    meta_ref[...] = rec


def _router_kernel(h_ref, w_ref, meta_ref, cnt_ref, carry_ref):
    _route(h_ref[...], pl.program_id(0) == 0, w_ref, meta_ref, cnt_ref, carry_ref)


def _router(h, w_router, tm=512):
    m = h.shape[0]
    wp = jnp.zeros((D_MODEL, LANES), F32).at[:, :N_EXPERTS].set(w_router)
    return pl.pallas_call(
        _router_kernel,
        out_shape=(jax.ShapeDtypeStruct((m, LANES), F32), jax.ShapeDtypeStruct((8, LANES), F32)),
        grid=(m // tm,),
        in_specs=[pl.BlockSpec((tm, D_MODEL), lambda i: (i, 0)),
                  pl.BlockSpec((D_MODEL, LANES), lambda i: (0, 0))],
        out_specs=(pl.BlockSpec((tm, LANES), lambda i: (i, 0)), pl.BlockSpec((8, LANES), lambda i: (0, 0))),
        scratch_shapes=[pltpu.VMEM((8, LANES), F32)],
        compiler_params=_cparams("arbitrary"),
        name="moe_router",
    )(h, wp)


MOE_TM = 1024
MOE_BLK = 128
MOE_ROWS = (MOE_TM, 768, 512, 256)


def _moe_plan(meta, counts, m):
    n_steps = -(-2 * m // MOE_TM) + N_EXPERTS
    cnt = counts[0, :N_EXPERTS].astype(jnp.int32)
    steps_e = (cnt + MOE_TM - 1) // MOE_TM
    ends = jnp.cumsum(steps_e)
    first = ends - steps_e
    total = ends[-1]
    e = meta[:, _META_E1:_META_E2 + 1].astype(jnp.int32)
    rank = meta[:, _META_R1:_META_R2 + 1].astype(jnp.int32)
    pos = (first * MOE_TM)[e] + rank
    s_idx = jnp.arange(n_steps, dtype=jnp.int32)
    step_e = jnp.minimum(jnp.searchsorted(ends, s_idx, side="right").astype(jnp.int32), N_EXPERTS - 1)
    valid = jnp.clip(cnt[step_e] - (s_idx - first[step_e]) * MOE_TM, 0, MOE_TM)
    valid = jnp.where(s_idx < total, valid, 0)
    return pos.reshape(-1), step_e, valid, n_steps


def _row_copy(src_ref, src_row, dst_ref, dst_row, sem):
    return pltpu.make_async_copy(src_ref.at[pl.ds(src_row, 1)], dst_ref.at[pl.ds(dst_row, 1)], sem)


def _invert_kernel(pos_ref, sv_ref, src_ref):
    def clear_step(s, c):
        def clear(b, c2):
            for k in range(8):
                src_ref[s * MOE_TM + b * 8 + k] = 0
            return c2

        return lax.fori_loop(sv_ref[s] // 8, MOE_TM // 8, clear, c)

    def place(t, c):
        p0 = pos_ref[2 * t]
        p1 = pos_ref[2 * t + 1]
        src_ref[p0] = t
        src_ref[p1] = t
        return c

    lax.fori_loop(0, sv_ref.shape[0], clear_step, 0)
    lax.fori_loop(0, pos_ref.shape[0] // 2, place, 0, unroll=8)


def _moe_invert(pos, valid, n_rows):
    smem = pl.BlockSpec(memory_space=pltpu.SMEM)
    return pl.pallas_call(
        _invert_kernel,
        out_shape=jax.ShapeDtypeStruct((n_rows,), jnp.int32),
        in_specs=[smem, smem], out_specs=smem,
        name="moe_invert",
    )(pos, valid)


def _block_wait(src_ref, dst_ref, rows, sem):
    pltpu.make_async_copy(src_ref.at[pl.ds(0, rows)], dst_ref.at[pl.ds(0, rows)], sem).wait()


def _moe_ffn_kernel(se_ref, sv_ref, src_ref, h_ref, wg_ref, wu_ref, wd_ref, o_ref, x32_ref, xb_ref, sem):
    del se_ref
    s = pl.program_id(0)
    f = pl.program_id(1)
    valid = sv_ref[s]

    def rows_used(v):
        r = jnp.where(v > 0, MOE_ROWS[-1], 0)
        for small, big in zip(MOE_ROWS[:0:-1], MOE_ROWS[-2::-1]):
            r = jnp.where(v > small, big, r)
        return r

    def blocks(v):
        return rows_used(v) // MOE_BLK

    def gather(step):
        base = step * MOE_TM

        def issue(b, c):
            for k in range(8):
                i = b * 8 + k
                _row_copy(h_ref, src_ref[base + i], x32_ref, i, sem).start()
            return c

        lax.fori_loop(0, rows_used(sv_ref[step]) // 8, issue, 0)

    @pl.when(f == 0)
    def _():
        @pl.when(s == 0)
        def _():
            gather(0)

        def land(i, c):
            _block_wait(h_ref, x32_ref, MOE_BLK, sem)
            return c

        def to_bf16(i, c):
            r0 = pl.multiple_of(i * MOE_BLK, MOE_BLK)
            xb_ref[pl.ds(r0, MOE_BLK), :] = x32_ref[pl.ds(r0, MOE_BLK), :].astype(BF16)
            return c

        lax.fori_loop(0, blocks(valid), land, 0)
        lax.fori_loop(0, blocks(valid), to_bf16, 0)

        @pl.when(s + 1 < pl.num_programs(0))
        def _():
            gather(s + 1)

        o_ref[...] = jnp.zeros_like(o_ref)

    def compute(rows):
        xs = xb_ref[0:rows, :]
        mid = _silu(_dot(xs, wg_ref[...].astype(BF16))) * _dot(xs, wu_ref[...].astype(BF16))
        o_ref[0:rows, :] += _dot(mid.astype(BF16), wd_ref[...].astype(BF16))

    for k, rows in enumerate(MOE_ROWS):
        lower = MOE_ROWS[k + 1] if k + 1 < len(MOE_ROWS) else 0

        @pl.when((valid > lower) & (valid <= rows))
        def _():
            compute(rows)


def _moe_ffn(h, src, w_gate, w_up, w_down, step_e, valid, n_steps, tf=512):
    assert all(r % MOE_BLK == 0 for r in MOE_ROWS) and MOE_ROWS[0] == MOE_TM
    n_ff = w_gate.shape[-1]
    nf = n_ff // tf
    fidx = lambda s, f, sv: jnp.where(sv[s] > 0, f, nf - 1)
    return pl.pallas_call(
        _moe_ffn_kernel,
        out_shape=jax.ShapeDtypeStruct((n_steps * MOE_TM, D_MODEL), F32),
        grid_spec=pltpu.PrefetchScalarGridSpec(
            num_scalar_prefetch=3, grid=(n_steps, nf),
            in_specs=[
                pl.BlockSpec(memory_space=pl.ANY),
                pl.BlockSpec((None, D_MODEL, tf), lambda s, f, se, sv, sr: (se[s], 0, fidx(s, f, sv))),
                pl.BlockSpec((None, D_MODEL, tf), lambda s, f, se, sv, sr: (se[s], 0, fidx(s, f, sv))),
                pl.BlockSpec((None, tf, D_MODEL), lambda s, f, se, sv, sr: (se[s], fidx(s, f, sv), 0)),
            ],
            out_specs=pl.BlockSpec((MOE_TM, D_MODEL), lambda s, f, se, sv, sr: (s, 0), pipeline_mode=pl.Buffered(1)),
            scratch_shapes=[pltpu.VMEM((MOE_TM, D_MODEL), F32), pltpu.VMEM((MOE_TM, D_MODEL), BF16),
                            pltpu.SemaphoreType.DMA(())]),
        compiler_params=_cparams("arbitrary", "arbitrary"),
        name="moe_experts",
    )(step_e, valid, src, h, w_gate, w_up, w_down)


def _combine_kernel(pos_ref, meta_ref, x_ref, m_ref, gf_ref, ys_ref, o_ref, buf_ref, sems, *, tm, final):
    t = pl.program_id(0)

    def request(tile):
        b = tile % 2

        def issue(i, c):
            for slot in range(2):
                _row_copy(ys_ref, pos_ref[2 * (tile * tm + i) + slot], buf_ref.at[b, slot], i, sems.at[b]).start()
            return c

        lax.fori_loop(0, tm, issue, 0, unroll=4)

    @pl.when(t == 0)
    def _():
        request(0)

    @pl.when(t + 1 < pl.num_programs(0))
    def _():
        request(t + 1)

    cur = t % 2
    for slot in range(2):
        _block_wait(ys_ref, buf_ref.at[cur, slot], tm, sems.at[cur])
    meta = meta_ref[...]
    y = meta[:, _META_W1:_META_W1 + 1] * buf_ref[cur, 0] + meta[:, _META_W2:_META_W2 + 1] * buf_ref[cur, 1]
    x = x_ref[...] + m_ref[5:6, :] * y
    if final:
        x = _rms(x, gf_ref[...])
    o_ref[...] = x


def _moe_combine(pos, meta, x, mods, ys, groups, row0, g_final, tm=512):
    m = x.shape[0]
    nr = m // groups // tm
    final = g_final is not None
    gf = (g_final if final else jnp.ones((D_MODEL,), F32)).reshape(1, D_MODEL)
    return pl.pallas_call(
        functools.partial(_combine_kernel, tm=tm, final=final),
        out_shape=jax.ShapeDtypeStruct((m, D_MODEL), F32),
        grid_spec=pltpu.PrefetchScalarGridSpec(
            num_scalar_prefetch=1, grid=(m // tm,),
            in_specs=[pl.BlockSpec((tm, LANES), lambda i, p: (i, 0)),
                      pl.BlockSpec((tm, D_MODEL), lambda i, p: (i, 0)),
                      pl.BlockSpec((None, 6, D_MODEL), lambda i, p: (row0 + i // nr, 0, 0)),
                      pl.BlockSpec((1, D_MODEL), lambda i, p: (0, 0)),
                      pl.BlockSpec(memory_space=pl.ANY)],
            out_specs=pl.BlockSpec((tm, D_MODEL), lambda i, p: (i, 0)),
            scratch_shapes=[pltpu.VMEM((2, 2, tm, D_MODEL), F32), pltpu.SemaphoreType.DMA((2,))]),
        compiler_params=_cparams("arbitrary"),
        name="moe_combine",
    )(pos, meta, x, mods, gf, ys)


def _moe(h, meta, counts, x, mods, groups, row0, w_gate, w_up, w_down, g_final):
    m = h.shape[0]
    pos, step_e, valid, n_steps = _moe_plan(meta, counts, m)
    src = _moe_invert(pos, valid, n_steps * MOE_TM)
    ys = _moe_ffn(h, src, w_gate, w_up, w_down, step_e, valid, n_steps)
    return _moe_combine(pos, meta, x, mods, ys, groups, row0, g_final)


def _ffn_kernel(h_ref, wg_ref, wu_ref, wd_ref, x_ref, m_ref, gn_ref, mn_ref, o_ref, *maybe_hn_ref):
    f = pl.program_id(2)

    @pl.when(f == 0)
    def _():
        o_ref[...] = jnp.zeros_like(o_ref)

    h = h_ref[...]
    mid = _silu(_dot(h, wg_ref[...])) * _dot(h, wu_ref[...])
    o_ref[...] += _dot(mid.astype(BF16), wd_ref[...])

    @pl.when(f == pl.num_programs(2) - 1)
    def _():
        x = x_ref[...] + m_ref[5:6, :] * o_ref[...]
        o_ref[...] = x
        for hn_ref in maybe_hn_ref:
            hn_ref[...] = (_rms(x, gn_ref[...]) * (1.0 + mn_ref[1:2, :]) + mn_ref[0:1, :]).astype(hn_ref.dtype)


def _ffn(h, w_gate, w_up, w_down, x, mods, groups, row0, nxt, tm, tf=512):
    m = x.shape[0]
    n_ff = w_gate.shape[1]
    nr = m // groups // tm
    g_next, mods_next = nxt if nxt is not None else (jnp.ones((D_MODEL,), F32), mods)
    once = pl.Buffered(1)
    tile = lambda n, mode=None: pl.BlockSpec((tm, n), lambda g_, i, f: (g_ * nr + i, 0), pipeline_mode=mode)
    mod_spec = pl.BlockSpec((None, 6, D_MODEL), lambda g_, i, f: (row0 + g_, 0, 0))
    out_shape = [jax.ShapeDtypeStruct((m, D_MODEL), F32)]
    out_specs = [tile(D_MODEL, once)]
    if nxt is not None:
        out_shape.append(jax.ShapeDtypeStruct((m, D_MODEL), BF16))
        out_specs.append(tile(D_MODEL, once))
    outs = pl.pallas_call(
        _ffn_kernel,
        out_shape=tuple(out_shape),
        grid=(groups, nr, n_ff // tf),
        in_specs=[tile(D_MODEL),
                  pl.BlockSpec((D_MODEL, tf), lambda g_, i, f: (0, f)),
                  pl.BlockSpec((D_MODEL, tf), lambda g_, i, f: (0, f)),
                  pl.BlockSpec((tf, D_MODEL), lambda g_, i, f: (f, 0)),
                  tile(D_MODEL, once), mod_spec,
                  pl.BlockSpec((1, D_MODEL), lambda g_, i, f: (0, 0)), mod_spec],
        out_specs=tuple(out_specs),
        compiler_params=_cparams("parallel", "parallel", "arbitrary"),
        name="swiglu_ffn",
    )(h, w_gate, w_up, w_down, x, mods, g_next.reshape(1, D_MODEL), mods_next)
    return outs if nxt is not None else (outs[0], None)


def _swap_halves(w):
    half = w.shape[-1] // 2
    return jnp.concatenate([w[..., half:], w[..., :half]], axis=-1)


def _swap_heads(w, heads, dim):
    k = w.shape[0]
    return _swap_halves(w.reshape(k, heads, dim)).reshape(k, heads * dim)


def _rope_tables():
    rows = SEQ // GRID_W
    row = np.repeat(np.arange(rows), GRID_W).astype(np.float32)
    col = np.tile(np.arange(GRID_W), rows).astype(np.float32)
    quarter = MLA_ROPE // 4
    freqs = (np.float32(ROPE_THETA) ** (-np.arange(quarter, dtype=np.float32) / quarter)).astype(np.float32)
    ang = np.concatenate([row[:, None] * freqs[None], col[:, None] * freqs[None]], axis=-1)
    cos = np.cos(ang.astype(np.float64))
    sin = np.sin(ang.astype(np.float64))
    cos2 = np.concatenate([cos, cos], axis=-1)
    sin2 = np.concatenate([-sin, sin], axis=-1)
    tab_lat = np.concatenate([cos2, sin2], axis=-1)
    tab_ctx = np.concatenate([np.ones((CTX_LEN, MLA_ROPE)), np.zeros((CTX_LEN, MLA_ROPE))], axis=-1)
    return dict(
        tab_lat=jnp.asarray(tab_lat, F32), tab_ctx=jnp.asarray(tab_ctx, F32),
        cos8=jnp.asarray(np.tile(cos2, (1, SWA_HEADS)), F32),
        sin8=jnp.asarray(np.tile(sin2, (1, SWA_HEADS)), F32),
        ones8=jnp.ones((CTX_LEN, _SWA_Q), F32), zeros8=jnp.zeros((CTX_LEN, _SWA_Q), F32),
    )


def _layer_weights(p, l):
    w_in = p["w_in"][l]
    cq = w_in[:, _O_CQ:_O_CKV]
    ckv = w_in[:, _O_CKV:_O_KPE]
    kpe = w_in[:, _O_KPE:_O_HY]
    hy = w_in[:, _O_HY:_O_SQ]
    sq = w_in[:, _O_SQ:_O_SK]
    sk = w_in[:, _O_SK:_O_SV]
    sv = w_in[:, _O_SV:]
    dq = MLA_NOPE + MLA_ROPE
    wq = p["w_q_up"][l].reshape(MLA_Q_LORA, MLA_HEADS, dq)
    wq = jnp.concatenate([wq, _swap_halves(wq[..., MLA_NOPE:])], axis=-1)
    wkv = p["w_kv_up"][l].reshape(MLA_KV_LORA, MLA_HEADS, MLA_NOPE + MLA_V)
    return dict(
        w_cq=cq.astype(BF16),
        w_ckv=jnp.concatenate([ckv, kpe, _swap_halves(kpe)], axis=-1).astype(BF16),
        w_hy=hy.astype(BF16),
        w_swa=jnp.concatenate([sq, _swap_heads(sq, SWA_HEADS, SWA_HEAD_DIM), sk,
                               _swap_heads(sk, SWA_KV_HEADS, SWA_HEAD_DIM), sv], axis=-1).astype(BF16),
        w_q=jnp.transpose(wq, (1, 0, 2)).astype(BF16),
        w_kv=jnp.transpose(wkv, (1, 0, 2)).astype(BF16),
        w_out=p["w_out"][l].astype(BF16),
        g_q=p["g_q"][l], g_kv=p["g_kv"][l],
        hy_conv_w=p["hy_conv_w"][l], hy_conv_b=p["hy_conv_b"][l],
        hy_w1=p["hy_w1"][l], hy_b1=p["hy_b1"][l], hy_w2=p["hy_w2"][l], hy_b2=p["hy_b2"][l],
        hy_w3=p["hy_w3"][l], hy_b3=p["hy_b3"][l], hy_w_filt=p["hy_w_filt"][l],
        hy_freq=p["hy_freq"][l], hy_skip=p["hy_skip"][l], swa_sink=p["swa_sink"][l],
    )


def _hyena(hy, lw, n_len, tabs):
    x0, vx, vxb = _hy_pre(hy, lw["hy_conv_w"], lw["hy_conv_b"], n_len)
    spectra = _hy_filters(lw, n_len, tabs)
    y = _hy_long_conv(x0, vx, vxb, spectra, lw["hy_skip"], n_len, tabs)
    return y.reshape(BATCH * n_len, HY_WIDTH)


def _mixer(h_lat, h_ctx, lw, rope, hy_tabs, need_ctx):
    q_l, k_l, v_l = _mla_proj(h_lat, lw, rope["tab_lat"], tm=1024)
    if need_ctx:
        q_c, k_c, v_c = _mla_proj(h_ctx, lw, rope["tab_ctx"], tm=CTX_LEN)
    else:
        k_c, v_c = _kv_proj(h_ctx, lw["w_ckv"], lw["g_kv"], lw["w_kv"], rope["tab_ctx"], tm=CTX_LEN)
    a_l = _mla_attention(q_l, [(k_l, v_l, SEQ), (k_c, v_c, CTX_LEN)], SEQ, tq=2048)
    hy_l, sq_l, kk_l, vv_l = _swa_proj(h_lat, lw["w_swa"], rope["cos8"], rope["sin8"], True, w_hy=lw["w_hy"],
                                       tm=1024)
    y_l = _hyena(hy_l, lw, SEQ, hy_tabs[SEQ])
    if need_ctx:
        hy_c, sq_c, kk_c, vv_c = _swa_proj(h_ctx, lw["w_swa"], rope["ones8"], rope["zeros8"], False,
                                           w_hy=lw["w_hy"], tm=CTX_LEN)
    else:
        sq_c, kk_c, vv_c = _swa_proj(h_ctx, lw["w_swa"], rope["ones8"], rope["zeros8"], False, tm=CTX_LEN)
    s_l = _swa_attention(lw["swa_sink"], sq_l, (kk_l, vv_l), (kk_c, vv_c), SEQ)
    if not need_ctx:
        return (a_l, y_l, s_l), None
    a_c = _mla_attention(q_c, [(k_c, v_c, CTX_LEN)], CTX_LEN, tq=CTX_LEN)
    y_c = _hyena(hy_c, lw, CTX_LEN, hy_tabs[CTX_LEN])
    s_c = _swa_attention(lw["swa_sink"], sq_c, None, (kk_c, vv_c), CTX_LEN, tq=CTX_LEN)
    return (a_l, y_l, s_l), (a_c, y_c, s_c)


def _forward(p):
    rope = _rope_tables()
    hy_tabs = {n: dict(dft=_dft_tables(n), filt=_filter_tables(n)) for n in (SEQ, CTX_LEN)}
    cvec = jnp.zeros((MOD_ROWS, D_MODEL), F32).at[:BATCH].set(p["c"]).at[BATCH].set(p["c_ctx"])
    mods_all = _modulation(cvec, p["w_mod"], p["b_mod"]).reshape(DEPTH, MOD_ROWS, 6, D_MODEL)

    x_lat = p["x"].reshape(BATCH * SEQ, D_MODEL)
    x_ctx = p["ctx"].reshape(BATCH * CTX_LEN, D_MODEL)
    h_in = [None, None]
    for l in range(DEPTH):
        last = l == DEPTH - 1
        mods = mods_all[l]
        lw = _layer_weights(p, l)
        h_lat = h_in[0] if h_in[0] is not None else _norm_mod(x_lat, p["g_mix"][l], mods, BATCH, 0, 0, BF16)
        h_ctx = h_in[1] if h_in[1] is not None else _norm_mod(x_ctx, p["g_mix"][l], mods, 1, BATCH, 0, BF16)
        mix_l, mix_c = _mixer(h_lat, h_ctx, lw, rope, hy_tabs, not last)
        streams = [(x_lat, mix_l, BATCH, 0, 1024)]
        if not last:
            streams.append((x_ctx, mix_c, 1, BATCH, 512))
        outs = []
        h_in = [None, None]
        i = l // 2
        dense = l % 2 == 0
        for n, (x, mix, groups, row0, ffn_tm) in enumerate(streams):
            if dense:
                x, h2 = _out_proj(*mix, lw["w_out"], x, mods, p["g_ffn"][l], groups, row0, BF16)
                nxt = None if last else (p["g_mix"][l + 1], mods_all[l + 1])
                x, h_in[n] = _ffn(h2, p["ffn_w_gate"][i].astype(BF16), p["ffn_w_up"][i].astype(BF16),
                                  p["ffn_w_down"][i].astype(BF16), x, mods, groups, row0, nxt, ffn_tm)
            else:
                x, h2 = _out_proj(*mix, lw["w_out"], x, mods, p["g_ffn"][l], groups, row0, F32)
                meta, counts = _router(h2, p["moe_router"][i])
                g_final = p["g_final"] if (last and n == 0) else None
                x = _moe(h2, meta, counts, x, mods, groups, row0, p["moe_w_gate"][i], p["moe_w_up"][i],
                         p["moe_w_down"][i], g_final)
            outs.append(x)
        x_lat = outs[0]
        if not last:
            x_ctx = outs[1]
    if DEPTH % 2 == 1:
        x_lat = _norm_mod(x_lat, p["g_final"], mods_all[0], BATCH, 0, None, F32)
    return x_lat.reshape(BATCH, SEQ, D_MODEL)


def kernel(x, c, ctx, c_ctx, w_mod, b_mod, g_mix, g_ffn, w_in, g_q, w_q_up, g_kv, w_kv_up, hy_conv_w, hy_conv_b, hy_w1, hy_b1, hy_w2, hy_b2, hy_w3, hy_b3, hy_w_filt, hy_freq, hy_skip, swa_sink, w_out, ffn_w_gate, ffn_w_up, ffn_w_down, moe_router, moe_w_gate, moe_w_up, moe_w_down, g_final):
    return _forward(dict(
        x=x, c=c, ctx=ctx, c_ctx=c_ctx, w_mod=w_mod, b_mod=b_mod, g_mix=g_mix, g_ffn=g_ffn, w_in=w_in,
        g_q=g_q, w_q_up=w_q_up, g_kv=g_kv, w_kv_up=w_kv_up, hy_conv_w=hy_conv_w, hy_conv_b=hy_conv_b,
        hy_w1=hy_w1, hy_b1=hy_b1, hy_w2=hy_w2, hy_b2=hy_b2, hy_w3=hy_w3, hy_b3=hy_b3,
        hy_w_filt=hy_w_filt, hy_freq=hy_freq, hy_skip=hy_skip, swa_sink=swa_sink, w_out=w_out,
        ffn_w_gate=ffn_w_gate, ffn_w_up=ffn_w_up, ffn_w_down=ffn_w_down, moe_router=moe_router,
        moe_w_gate=moe_w_gate, moe_w_up=moe_w_up, moe_w_down=moe_w_down, g_final=g_final))
```

```python
import functools
import math

import numpy as np
import jax
import jax.numpy as jnp
from jax import lax
from jax.experimental import pallas as pl
from jax.experimental.pallas import tpu as pltpu

F32 = jnp.float32
BF16 = jnp.bfloat16

D_MODEL = 2048
BATCH = 4
SEQ = 2048
DEPTH = 2
CTX_LEN = 256
GRID_W = 64
NORM_EPS = 1e-6
ROPE_THETA = 10000.0
MLA_HEADS = 8
MLA_NOPE = 128
MLA_ROPE = 64
MLA_V = 128
MLA_Q_LORA = 768
MLA_KV_LORA = 512
HY_WIDTH = 512
HY_BANDS = 16
HY_EMB = 1 + 2 * HY_BANDS
HY_FILTER_HIDDEN = 64
HY_DECAY_TARGET = 1e-2
HY_QUICK_DECAY_PCT = 0.3
HY_SLOW_DECAY_PCT = 1.5
SWA_HEADS = 8
SWA_KV_HEADS = 2
SWA_HEAD_DIM = 64
SWA_WINDOW = 128
N_EXPERTS = 8
D_FF = 5632
D_FF_EXPERT = 7168

LANES = 128
MXU_DIM = 256
VMEM_LIMIT_BYTES = 56 * 1024 * 1024
NEG_BIG = -1e30
MOD_ROWS = 8

_O_CQ = 0
_O_CKV = _O_CQ + MLA_Q_LORA
_O_KPE = _O_CKV + MLA_KV_LORA
_O_HY = _O_KPE + MLA_ROPE
_O_SQ = _O_HY + 3 * HY_WIDTH
_O_SK = _O_SQ + SWA_HEADS * SWA_HEAD_DIM
_O_SV = _O_SK + SWA_KV_HEADS * SWA_HEAD_DIM


def _cparams(*sem):
    return pltpu.CompilerParams(dimension_semantics=sem, vmem_limit_bytes=VMEM_LIMIT_BYTES)


def _dot(a, b):
    return jnp.dot(a, b, preferred_element_type=F32)


def _dot_nt(a, b):
    return lax.dot_general(a, b, (((1,), (1,)), ((), ())), preferred_element_type=F32)


def _split_bf16(a):
    hi = a.astype(BF16)
    lo = (a - hi.astype(F32)).astype(BF16)
    return hi, lo


def _dot3(a, b):
    ah, al = _split_bf16(a)
    bh, bl = _split_bf16(b)
    return _dot(ah, bh) + (_dot(al, bh) + _dot(ah, bl))


def _silu(x):
    return x / (1.0 + jnp.exp(-x))


def _rms(x, g):
    ms = jnp.mean(x * x, axis=-1, keepdims=True)
    return x * lax.rsqrt(ms + NORM_EPS) * g


def _const_spec(shape):
    nd = len(shape)
    return pl.BlockSpec(shape, lambda *_: (0,) * nd, pipeline_mode=pl.Buffered(1))


def _mod_kernel(c_ref, w_ref, b_ref, o_ref):
    ah, al = _split_bf16(_silu(c_ref[...]))
    wh, wl = _split_bf16(w_ref[0])
    r = _dot(jnp.concatenate([ah, al], axis=0), wh)
    o_ref[0] = r[:MOD_ROWS] + (r[MOD_ROWS:] + _dot(ah, wl)) + b_ref[0]


def _modulation(cvec, w_mod, b_mod):
    tn = 1024
    n = 6 * D_MODEL
    return pl.pallas_call(
        _mod_kernel,
        out_shape=jax.ShapeDtypeStruct((DEPTH, MOD_ROWS, n), F32),
        grid=(DEPTH, n // tn),
        in_specs=[
            pl.BlockSpec((MOD_ROWS, D_MODEL), lambda l, j: (0, 0)),
            pl.BlockSpec((1, D_MODEL, tn), lambda l, j: (l, 0, j)),
            pl.BlockSpec((1, 1, tn), lambda l, j: (l, 0, j)),
        ],
        out_specs=pl.BlockSpec((1, MOD_ROWS, tn), lambda l, j: (l, 0, j)),
        compiler_params=_cparams("arbitrary", "arbitrary"),
        name="adaln_mod",
    )(cvec, w_mod, b_mod.reshape(DEPTH, 1, n))


def _norm_kernel(x_ref, g_ref, m_ref, o_ref, *, si):
    y = _rms(x_ref[...], g_ref[...])
    if si is not None:
        y = y * (1.0 + m_ref[si + 1:si + 2, :]) + m_ref[si:si + 1, :]
    o_ref[...] = y.astype(o_ref.dtype)


def _norm_mod(x, g, mods, groups, row0, si, out_dtype, tm=512):
    m = x.shape[0]
    nr = m // groups // tm
    return pl.pallas_call(
        functools.partial(_norm_kernel, si=si),
        out_shape=jax.ShapeDtypeStruct((m, D_MODEL), out_dtype),
        grid=(groups, nr),
        in_specs=[
            pl.BlockSpec((tm, D_MODEL), lambda g_, i: (g_ * nr + i, 0)),
            pl.BlockSpec((1, D_MODEL), lambda g_, i: (0, 0)),
            pl.BlockSpec((None, 6, D_MODEL), lambda g_, i: (row0 + g_, 0, 0)),
        ],
        out_specs=pl.BlockSpec((tm, D_MODEL), lambda g_, i: (g_ * nr + i, 0)),
        compiler_params=_cparams("parallel", "parallel"),
        name="norm_mod",
    )(x, g.reshape(1, D_MODEL), mods)


def _rope_halves(pe_pair, tab):
    r = pe_pair * tab
    return r + pltpu.roll(r, MLA_ROPE, axis=1)


MLA_Q_SCALE = (MLA_NOPE + MLA_ROPE) ** -0.5 * math.log2(math.e)


def _qproj_body(h_ref, wc_ref, g_ref, wq_ref, tab_ref, o_ref):
    cq = _dot(h_ref[...], wc_ref[...])
    cqn = _rms(cq, g_ref[...]).astype(BF16)
    tab = tab_ref[...] * MLA_Q_SCALE
    for hh in range(MLA_HEADS):
        r = _dot(cqn, wq_ref[hh])
        o_ref[hh, :, 0:MLA_NOPE] = (r[:, :MLA_NOPE] * MLA_Q_SCALE).astype(BF16)
        o_ref[hh, :, MLA_NOPE:] = _rope_halves(r[:, MLA_NOPE:], tab).astype(BF16)


def _mla_proj_kernel(h_ref, wcq_ref, gq_ref, wq_ref, wckv_ref, gkv_ref, wkv_ref, tab_ref, q_ref, k_ref, v_ref):
    _qproj_body(h_ref, wcq_ref, gq_ref, wq_ref, tab_ref, q_ref)
    _kvproj_kernel(h_ref, wckv_ref, gkv_ref, wkv_ref, tab_ref, k_ref, v_ref)


def _mla_proj(h, lw, tab, tm=512):
    m = h.shape[0]
    nt = tab.shape[0] // tm
    head_out = jax.ShapeDtypeStruct((MLA_HEADS, m, MXU_DIM), BF16)
    head_spec = pl.BlockSpec((MLA_HEADS, tm, MXU_DIM), lambda i: (0, i, 0))
    return pl.pallas_call(
        _mla_proj_kernel,
        out_shape=(head_out, head_out, head_out),
        grid=(m // tm,),
        in_specs=[
            pl.BlockSpec((tm, D_MODEL), lambda i: (i, 0)),
            _const_spec(lw["w_cq"].shape), _const_spec((1, MLA_Q_LORA)), _const_spec(lw["w_q"].shape),
            _const_spec(lw["w_ckv"].shape), _const_spec((1, MLA_KV_LORA)), _const_spec(lw["w_kv"].shape),
            pl.BlockSpec((tm, LANES), lambda i: (i % nt, 0)),
        ],
        out_specs=(head_spec, head_spec, head_spec),
        compiler_params=_cparams("parallel"),
        name="mla_proj",
    )(h, lw["w_cq"], lw["g_q"].reshape(1, -1), lw["w_q"], lw["w_ckv"], lw["g_kv"].reshape(1, -1), lw["w_kv"], tab)


def _kvproj_kernel(h_ref, wc_ref, g_ref, wkv_ref, tab_ref, k_ref, v_ref):
    t = _dot(h_ref[...], wc_ref[...])
    ckvn = _rms(t[:, :MLA_KV_LORA], g_ref[...]).astype(BF16)
    rot = _rope_halves(t[:, MLA_KV_LORA:], tab_ref[...])
    lane = lax.broadcasted_iota(jnp.int32, rot.shape, 1)
    kpe = jnp.where(lane < MLA_ROPE, rot, 0.0).astype(BF16)
    for hh in range(MLA_HEADS):
        r = _dot(ckvn, wkv_ref[hh])
        k_ref[hh, :, 0:MLA_NOPE] = r[:, :MLA_NOPE].astype(BF16)
        k_ref[hh, :, MLA_NOPE:] = kpe
        v_ref[hh, :, 0:MLA_V] = r[:, MLA_NOPE:].astype(BF16)
        v_ref[hh, :, MLA_V:] = jnp.ones((r.shape[0], MXU_DIM - MLA_V), BF16)


def _kv_proj(h, w_ckv, g_kv, w_kv, tab, tm=512):
    m = h.shape[0]
    nt = tab.shape[0] // tm
    return pl.pallas_call(
        _kvproj_kernel,
        out_shape=(jax.ShapeDtypeStruct((MLA_HEADS, m, MXU_DIM), BF16),
                   jax.ShapeDtypeStruct((MLA_HEADS, m, MXU_DIM), BF16)),
        grid=(m // tm,),
        in_specs=[
            pl.BlockSpec((tm, D_MODEL), lambda i: (i, 0)),
            _const_spec(w_ckv.shape),
            _const_spec((1, MLA_KV_LORA)),
            _const_spec(w_kv.shape),
            pl.BlockSpec((tm, LANES), lambda i: (i % nt, 0)),
        ],
        out_specs=(pl.BlockSpec((MLA_HEADS, tm, MXU_DIM), lambda i: (0, i, 0)),
                   pl.BlockSpec((MLA_HEADS, tm, MXU_DIM), lambda i: (0, i, 0))),
        compiler_params=_cparams("parallel"),
        name="mla_kv_proj",
    )(h, w_ckv, g_kv.reshape(1, -1), w_kv, tab)


MLA_KEY_CHUNK = 512


def _mla_kernel(q_ref, *refs, seg_lens):
    nseg = len(seg_lens)
    o_ref = refs[2 * nseg]
    q = q_ref[...]
    m = jnp.full((q.shape[0], 1), NEG_BIG, F32)
    acc = jnp.zeros((q.shape[0], MXU_DIM), F32)
    for s, lk in enumerate(seg_lens):
        k_ref, v_ref = refs[2 * s], refs[2 * s + 1]
        step = min(MLA_KEY_CHUNK, lk)
        for c0 in range(0, lk, step):
            sc = _dot_nt(q, k_ref[c0:c0 + step, :])
            m_new = jnp.maximum(m, sc.max(axis=-1, keepdims=True))
            p = jnp.exp2(sc - m_new).astype(BF16)
            acc = acc * jnp.exp2(m - m_new) + _dot(p, v_ref[c0:c0 + step, :])
            m = m_new
    o_ref[...] = (acc[:, :MLA_V] / acc[:, MLA_V:MLA_V + 1]).astype(o_ref.dtype)


def _mla_attention(q, segs, lq, tq=512):
    nq = lq // tq
    in_specs = [pl.BlockSpec((None, tq, MXU_DIM), lambda b, h, i: (h, b * nq + i, 0))]
    args = [q]
    for k, v, lk in segs:
        in_specs.append(pl.BlockSpec((None, lk, MXU_DIM), lambda b, h, i: (h, b, 0)))
        in_specs.append(pl.BlockSpec((None, lk, MXU_DIM), lambda b, h, i: (h, b, 0)))
        args += [k, v]
    return pl.pallas_call(
        functools.partial(_mla_kernel, seg_lens=tuple(lk for _, _, lk in segs)),
        out_shape=jax.ShapeDtypeStruct((BATCH * lq, MLA_HEADS * MLA_V), BF16),
        grid=(BATCH, MLA_HEADS, nq),
        in_specs=in_specs,
        out_specs=pl.BlockSpec((tq, MLA_V), lambda b, h, i: (b * nq + i, h)),
        compiler_params=_cparams("parallel", "parallel", "arbitrary"),
        name="mla_attention",
    )(*args)


def _dft_tables(n_len):
    n2 = 2 * n_len
    idx = np.arange(n_len, dtype=np.int64)
    ang = (2.0 * np.pi / n2) * ((idx[:, None] * idx[None, :]) % n2).astype(np.float64)
    cm = np.cos(ang)
    sf = np.sin(ang)
    sf[0, :] = 1.0 - 2.0 * (idx % 2)
    return (jnp.asarray(cm, dtype=BF16), jnp.asarray(sf, dtype=BF16),
            jnp.asarray(sf.T.copy(), dtype=BF16))


def _filter_tables(n_len):
    pos = np.arange(n_len, dtype=np.float64)
    t = pos / max(n_len - 1, 1)
    bands = np.linspace(1e-4, HY_BANDS - 1, HY_BANDS)
    ang = (2.0 * math.pi / n_len) * pos[:, None] * bands[None]
    z = np.zeros((n_len, LANES), np.float64)
    z[:, 0] = t
    z[:, 1:1 + HY_BANDS] = np.cos(ang)
    z[:, 1 + HY_BANDS:HY_EMB] = -np.sin(ang)
    deltas = np.linspace(math.log(HY_DECAY_TARGET) / HY_SLOW_DECAY_PCT,
                         math.log(HY_DECAY_TARGET) / HY_QUICK_DECAY_PCT, HY_WIDTH)
    decay = np.exp(-t[:, None] * np.abs(deltas)[None])
    return jnp.asarray(z, dtype=F32), jnp.asarray(decay, dtype=F32)


def _hy_filter_kernel(z_ref, w1_ref, b1_ref, w2_ref, b2_ref, w3_ref, b3_ref, wf_ref, wb_ref,
                      fr_ref, dec_ref, cm_ref, sf_ref, a_ref, b_ref, d_ref, hid_ref, *, n_len):
    @pl.when(pl.program_id(0) == 0)
    def _():
        fr = fr_ref[...]
        h = jnp.sin(fr * (_dot3(z_ref[...], w1_ref[...]) + b1_ref[...]))
        h = jnp.sin(fr * (_dot3(h, w2_ref[...]) + b2_ref[...]))
        hid_ref[...] = jnp.sin(fr * (_dot3(h, w3_ref[...]) + b3_ref[...]))

    h = hid_ref[...]
    dec = dec_ref[...]
    h_f = _dot3(h, wf_ref[...]) * dec
    h_b = _dot3(h, wb_ref[...]) * dec
    row = lax.broadcasted_iota(jnp.int32, h_f.shape, 0)
    h_b = jnp.where(row == 0, 0.0, h_b)
    hs = h_f + h_b
    hd = h_f - h_b
    nc = hs.shape[1]
    hs2 = jnp.concatenate(_split_bf16(hs), axis=1)
    hd2 = jnp.concatenate(_split_bf16(hd), axis=1)
    sf = sf_ref[...]
    t2 = _dot(cm_ref[...], hs2)
    g2 = _dot(sf, hd2)
    n2 = _dot(sf[0:16, :], hs2)[0:1, :]
    t_re = t2[:, :nc] + t2[:, nc:]
    g_t = g2[:, :nc] + g2[:, nc:]
    t_ny = n2[:, :nc] + n2[:, nc:]
    inv_n = 1.0 / (2 * n_len)
    first = row == 0
    a_ref[...] = jnp.where(first, inv_n, 2.0 * inv_n) * t_re
    b_ref[...] = jnp.where(first, 0.0, -2.0 * inv_n * g_t)
    d_ref[...] = jnp.where(first, inv_n * t_ny, 2.0 * inv_n * t_re)


def _hy_filters(lp, n_len, tabs):
    z, decay = tabs["filt"]
    cm, sf, _ = tabs["dft"]
    cb = LANES
    nblk = HY_WIDTH // cb
    w1 = jnp.zeros((LANES, HY_FILTER_HIDDEN), F32).at[:HY_EMB].set(lp["hy_w1"])
    row = lambda a: a.reshape(1, -1)
    small = lambda shape: pl.BlockSpec(shape, lambda j: (0,) * len(shape))
    fh = HY_FILTER_HIDDEN
    out = jax.ShapeDtypeStruct((n_len, HY_WIDTH), F32)
    return pl.pallas_call(
        functools.partial(_hy_filter_kernel, n_len=n_len),
        out_shape=(out, out, out),
        grid=(nblk,),
        in_specs=[
            small((n_len, LANES)), small((LANES, fh)), small((1, fh)), small((fh, fh)), small((1, fh)),
            small((fh, fh)), small((1, fh)),
            pl.BlockSpec((fh, cb), lambda j: (0, j)),
            pl.BlockSpec((fh, cb), lambda j: (0, nblk + j)),
            small((1, fh)),
            pl.BlockSpec((n_len, cb), lambda j: (0, j)),
            _const_spec((n_len, n_len)), _const_spec((n_len, n_len)),
        ],
        out_specs=tuple(pl.BlockSpec((n_len, cb), lambda j: (0, j)) for _ in range(3)),
        scratch_shapes=[pltpu.VMEM((n_len, fh), F32)],
        compiler_params=_cparams("arbitrary"),
        name="hyena_filters",
    )(z, w1, row(lp["hy_b1"]), lp["hy_w2"], row(lp["hy_b2"]), lp["hy_w3"], row(lp["hy_b3"]),
      lp["hy_w_filt"], lp["hy_w_filt"], row(lp["hy_freq"]), decay, cm, sf)


def _hy_pre_kernel(u0_ref, u1_ref, u2_ref, w0_ref, w1_ref, w2_ref, b0_ref, b1_ref, b2_ref,
                   x0_ref, vx_ref, vxb_ref, *, n_len):
    def sconv(u_ref, w_ref, b_ref):
        u = u_ref[...]
        w = w_ref[...]
        row = lax.broadcasted_iota(jnp.int32, u.shape, 0)
        prev = jnp.where(row == 0, 0.0, pltpu.roll(u, 1, axis=0))
        nxt = jnp.where(row == n_len - 1, 0.0, pltpu.roll(u, n_len - 1, axis=0))
        return w[0:1] * prev + w[1:2] * u + w[2:3] * nxt + b_ref[...]

    x0_ref[...] = sconv(u0_ref, w0_ref, b0_ref)
    vx = sconv(u2_ref, w2_ref, b2_ref) * sconv(u1_ref, w1_ref, b1_ref)
    vx_ref[...] = vx
    vxb_ref[...] = vx.astype(BF16)


def _hy_pre(hy, conv_w, conv_b, n_len):
    nblk = HY_WIDTH // LANES
    uspec = lambda part: pl.BlockSpec((None, n_len, LANES), lambda b, j: (b, 0, part * nblk + j))
    wspec = lambda part: pl.BlockSpec((3, LANES), lambda b, j: (0, part * nblk + j))
    bspec = lambda part: pl.BlockSpec((1, LANES), lambda b, j: (0, part * nblk + j))
    ospec = pl.BlockSpec((None, n_len, LANES), lambda b, j: (b, 0, j))
    hy3 = hy.reshape(BATCH, n_len, 3 * HY_WIDTH)
    cb = conv_b.reshape(1, -1)
    return pl.pallas_call(
        functools.partial(_hy_pre_kernel, n_len=n_len),
        out_shape=(jax.ShapeDtypeStruct((BATCH, n_len, HY_WIDTH), F32),
                   jax.ShapeDtypeStruct((BATCH, n_len, HY_WIDTH), F32),
                   jax.ShapeDtypeStruct((BATCH, n_len, HY_WIDTH), BF16)),
        grid=(BATCH, nblk),
        in_specs=[uspec(0), uspec(1), uspec(2), wspec(0), wspec(1), wspec(2), bspec(0), bspec(1), bspec(2)],
        out_specs=(ospec, ospec, ospec),
        compiler_params=_cparams("parallel", "parallel"),
        name="hyena_short_conv",
    )(hy3, hy3, hy3, conv_w, conv_w, conv_w, cb, cb, cb)


def _hy_fwd_kernel(cm_ref, sf_ref, u_ref, a_ref, b_ref, d_ref, q_ref):
    u = u_ref[...]
    ur = _dot(cm_ref[...], u)
    g = _dot(sf_ref[...], u)
    b = b_ref[...]
    q_ref[0] = (a_ref[...] * ur + b * g).astype(BF16)
    q_ref[1] = (d_ref[...] * g - b * ur).astype(BF16)


def _hy_inv_kernel(cm_ref, si_ref, q_ref, x0_ref, vx_ref, skip_ref, o_ref):
    y = _dot(cm_ref[...], q_ref[0]) + _dot(si_ref[...], q_ref[1])
    o_ref[...] = (x0_ref[...] * (y + vx_ref[...] * skip_ref[...])).astype(o_ref.dtype)


def _hy_long_conv(x0, vx, vxb, spectra, skip, n_len, tabs):
    cm, sf, si = tabs["dft"]
    a, b, d = spectra
    tf = min(1024, n_len)
    nf = n_len // tf
    mat = pl.BlockSpec((tf, n_len), lambda bb, f: (f, 0))
    spec = pl.BlockSpec((tf, HY_WIDTH), lambda bb, f: (f, 0))
    q = pl.pallas_call(
        _hy_fwd_kernel,
        out_shape=jax.ShapeDtypeStruct((BATCH, 2, n_len, HY_WIDTH), BF16),
        grid=(BATCH, nf),
        in_specs=[mat, mat, pl.BlockSpec((None, n_len, HY_WIDTH), lambda bb, f: (bb, 0, 0)), spec, spec, spec],
        out_specs=pl.BlockSpec((None, 2, tf, HY_WIDTH), lambda bb, f: (bb, 0, f, 0)),
        compiler_params=_cparams("parallel", "arbitrary"),
        name="hyena_dft_fwd",
    )(cm, sf, vxb, a, b, d)
    tile = pl.BlockSpec((None, tf, HY_WIDTH), lambda bb, f: (bb, f, 0))
    return pl.pallas_call(
        _hy_inv_kernel,
        out_shape=jax.ShapeDtypeStruct((BATCH, n_len, HY_WIDTH), BF16),
        grid=(BATCH, nf),
        in_specs=[mat, mat, pl.BlockSpec((None, 2, n_len, HY_WIDTH), lambda bb, f: (bb, 0, 0, 0)),
                  tile, tile, pl.BlockSpec((1, HY_WIDTH), lambda bb, f: (0, 0))],
        out_specs=tile,
        compiler_params=_cparams("parallel", "arbitrary"),
        name="hyena_dft_inv",
    )(cm, si, q, x0, vx, skip.reshape(1, -1))


_SWA_Q = SWA_HEADS * SWA_HEAD_DIM
_SWA_KV = SWA_KV_HEADS * SWA_HEAD_DIM


def _swaproj_kernel(h_ref, w_ref, cos_ref, sin_ref, q_ref, k_ref, v_ref, *, rope):
    t = _dot(h_ref[...], w_ref[...])
    q = t[:, :_SWA_Q]
    k = t[:, 2 * _SWA_Q:2 * _SWA_Q + _SWA_KV]
    v = t[:, 2 * _SWA_Q + 2 * _SWA_KV:]
    if rope:
        cos = cos_ref[...]
        sin = sin_ref[...]
        q = q * cos + t[:, _SWA_Q:2 * _SWA_Q] * sin
        k = k * cos[:, :_SWA_KV] + t[:, 2 * _SWA_Q + _SWA_KV:2 * _SWA_Q + 2 * _SWA_KV] * sin[:, :_SWA_KV]
    q_ref[...] = (q * (SWA_HEAD_DIM ** -0.5 * math.log2(math.e))).astype(BF16)
    lo = lax.broadcasted_iota(jnp.int32, k.shape, 1) < SWA_HEAD_DIM
    for src, dst in ((k, k_ref), (v, v_ref)):
        sw = pltpu.roll(src, SWA_HEAD_DIM, axis=1)
        dst[:, 0 * LANES:1 * LANES] = jnp.where(lo, src, 0.0).astype(BF16)
        dst[:, 1 * LANES:2 * LANES] = jnp.where(lo, 0.0, sw).astype(BF16)
        dst[:, 2 * LANES:3 * LANES] = jnp.where(lo, sw, 0.0).astype(BF16)
        dst[:, 3 * LANES:4 * LANES] = jnp.where(lo, 0.0, src).astype(BF16)


def _hy_swa_kernel(h_ref, why_ref, w_ref, cos_ref, sin_ref, hy_ref, q_ref, k_ref, v_ref, *, rope):
    hy_ref[...] = _dot(h_ref[...], why_ref[...])
    _swaproj_kernel(h_ref, w_ref, cos_ref, sin_ref, q_ref, k_ref, v_ref, rope=rope)


def _swa_proj(h, w, cos_t, sin_t, rope, w_hy=None, tm=512):
    m = h.shape[0]
    nt = cos_t.shape[0] // tm
    o = jax.ShapeDtypeStruct((m, 4 * LANES), BF16)
    ospec = pl.BlockSpec((tm, 4 * LANES), lambda i: (i, 0))
    tspec = pl.BlockSpec((tm, _SWA_Q), lambda i: (i % nt, 0))
    hspec = pl.BlockSpec((tm, D_MODEL), lambda i: (i, 0))
    if w_hy is None:
        return pl.pallas_call(
            functools.partial(_swaproj_kernel, rope=rope),
            out_shape=(o, o, o),
            grid=(m // tm,),
            in_specs=[hspec, _const_spec(w.shape), tspec, tspec],
            out_specs=(ospec, ospec, ospec),
            compiler_params=_cparams("parallel"),
            name="swa_proj",
        )(h, w, cos_t, sin_t)
    n_hy = w_hy.shape[1]
    return pl.pallas_call(
        functools.partial(_hy_swa_kernel, rope=rope),
        out_shape=(jax.ShapeDtypeStruct((m, n_hy), F32), o, o, o),
        grid=(m // tm,),
        in_specs=[hspec, _const_spec(w_hy.shape), _const_spec(w.shape), tspec, tspec],
        out_specs=(pl.BlockSpec((tm, n_hy), lambda i: (i, 0)), ospec, ospec, ospec),
        compiler_params=_cparams("parallel"),
        name="hyena_swa_proj",
    )(h, w_hy, w, cos_t, sin_t)


def _swa_kernel(sink_ref, q_ref, *refs, tq, win, lk, has_lat):
    if has_lat:
        kl_ref, vl_ref, kc_ref, vc_ref, o_ref = refs
        t0 = pl.program_id(1) * tq
        start = pl.multiple_of(jnp.clip(t0 - SWA_WINDOW, 0, lk - win), SWA_WINDOW)
        qpos = t0 + lax.broadcasted_iota(jnp.int32, (tq, win), 0)
        kpos = start + lax.broadcasted_iota(jnp.int32, (tq, win), 1)
        valid = jnp.abs(qpos - kpos) <= SWA_WINDOW
        valid = jnp.concatenate([valid, valid], axis=0)
    else:
        kc_ref, vc_ref, o_ref = refs
    upper = lax.broadcasted_iota(jnp.int32, (2 * tq, 1), 0) < tq
    for g in range(SWA_KV_HEADS):
        j0 = 2 * g
        qb = jnp.concatenate([q_ref[:, j0 * LANES:(j0 + 1) * LANES],
                              q_ref[:, (j0 + 1) * LANES:(j0 + 2) * LANES]], axis=0)
        acc = None
        for par in range(2):
            c0 = (2 * g + par) * LANES
            sk = jnp.where(upper, sink_ref[2 * j0 + par], sink_ref[2 * j0 + 2 + par]) * math.log2(math.e)
            s_c = _dot_nt(qb, kc_ref[:, c0:c0 + LANES])
            m = jnp.maximum(s_c.max(axis=-1, keepdims=True), sk)
            if has_lat:
                s_l = _dot_nt(qb, kl_ref[pl.ds(start, win), c0:c0 + LANES])
                s_l = jnp.where(valid, s_l, NEG_BIG)
                m = jnp.maximum(m, s_l.max(axis=-1, keepdims=True))
            p_c = jnp.exp2(s_c - m)
            den = p_c.sum(axis=-1, keepdims=True) + jnp.exp2(sk - m)
            if has_lat:
                p_l = jnp.exp2(s_l - m)
                den = den + p_l.sum(axis=-1, keepdims=True)
            o = _dot(p_c.astype(BF16), vc_ref[:, c0:c0 + LANES])
            if has_lat:
                o = o + _dot(p_l.astype(BF16), vl_ref[pl.ds(start, win), c0:c0 + LANES])
            o = o * (1.0 / den)
            acc = o if acc is None else acc + o
        o_ref[:, j0 * LANES:(j0 + 1) * LANES] = acc[:tq].astype(o_ref.dtype)
        o_ref[:, (j0 + 1) * LANES:(j0 + 2) * LANES] = acc[tq:].astype(o_ref.dtype)


def _swa_attention(sink, q, lat, ctx, lq, tq=256):
    nq = lq // tq
    win = tq + 2 * SWA_WINDOW
    full = lambda n: pl.BlockSpec((n, 4 * LANES), lambda b, i: (b, 0))
    in_specs = [pl.BlockSpec(memory_space=pltpu.SMEM),
                pl.BlockSpec((tq, 4 * LANES), lambda b, i: (b * nq + i, 0))]
    args = [sink, q]
    if lat is not None:
        in_specs += [full(SEQ), full(SEQ)]
        args += list(lat)
    in_specs += [full(CTX_LEN), full(CTX_LEN)]
    args += list(ctx)
    return pl.pallas_call(
        functools.partial(_swa_kernel, tq=tq, win=win, lk=SEQ, has_lat=lat is not None),
        out_shape=jax.ShapeDtypeStruct((BATCH * lq, 4 * LANES), BF16),
        grid=(BATCH, nq),
        in_specs=in_specs,
        out_specs=pl.BlockSpec((tq, 4 * LANES), lambda b, i: (b * nq + i, 0)),
        compiler_params=_cparams("parallel", "arbitrary"),
        name="swa_attention",
    )(*args)


def _out_kernel(a_ref, y_ref, s_ref, w_ref, x_ref, m_ref, g_ref, xo_ref, ho_ref):
    na = MLA_HEADS * MLA_V
    acc = _dot(a_ref[...], w_ref[0:na, :])
    acc = acc + _dot(y_ref[...], w_ref[na:na + HY_WIDTH, :])
    acc = acc + _dot(s_ref[...], w_ref[na + HY_WIDTH:, :])
    x = x_ref[...] + m_ref[2:3, :] * acc
    xo_ref[...] = x
    ho_ref[...] = (_rms(x, g_ref[...]) * (1.0 + m_ref[4:5, :]) + m_ref[3:4, :]).astype(ho_ref.dtype)


def _out_proj(a, y, s, w_out, x, mods, g_ffn, groups, row0, h_dtype, tm=512):
    m = x.shape[0]
    nr = m // groups // tm
    tile = lambda n: pl.BlockSpec((tm, n), lambda g_, i: (g_ * nr + i, 0))
    return pl.pallas_call(
        _out_kernel,
        out_shape=(jax.ShapeDtypeStruct((m, D_MODEL), F32), jax.ShapeDtypeStruct((m, D_MODEL), h_dtype)),
        grid=(groups, nr),
        in_specs=[tile(a.shape[1]), tile(y.shape[1]), tile(s.shape[1]), _const_spec(w_out.shape),
                  tile(D_MODEL), pl.BlockSpec((None, 6, D_MODEL), lambda g_, i: (row0 + g_, 0, 0)),
                  pl.BlockSpec((1, D_MODEL), lambda g_, i: (0, 0))],
        out_specs=(tile(D_MODEL), tile(D_MODEL)),
        compiler_params=_cparams("parallel", "parallel"),
        name="out_proj",
    )(a, y, s, w_out, x, mods, g_ffn.reshape(1, -1))


_META_W1, _META_W2, _META_E1, _META_E2, _META_R1, _META_R2 = range(6)


def _route(h, is_first, w_ref, meta_ref, cnt_ref, carry_ref):
    @pl.when(is_first)
    def _():
        carry_ref[...] = jnp.zeros_like(carry_ref)

    logits = _dot3(h, w_ref[...])
    tm = logits.shape[0]
    lane = lax.broadcasted_iota(jnp.int32, logits.shape, 1).astype(F32)
    logits = jnp.where(lane < N_EXPERTS, logits, NEG_BIG)
    m1 = logits.max(axis=-1, keepdims=True)
    i1 = jnp.where(logits == m1, lane, float(LANES)).min(axis=-1, keepdims=True)
    rest = jnp.where(lane == i1, NEG_BIG, logits)
    m2 = rest.max(axis=-1, keepdims=True)
    i2 = jnp.where(rest == m2, lane, float(LANES)).min(axis=-1, keepdims=True)
    e2 = jnp.exp(m2 - m1)
    w1 = 1.0 / (1.0 + e2)
    hot = jnp.where((lane == i1) | (lane == i2), 1.0, 0.0)
    before = (lax.broadcasted_iota(jnp.int32, (tm, tm), 0) > lax.broadcasted_iota(jnp.int32, (tm, tm), 1))
    seen = _dot(before.astype(F32).astype(BF16), hot.astype(BF16)) + carry_ref[0:1, :]
    r1 = jnp.where(lane == i1, seen, 0.0).sum(axis=-1, keepdims=True)
    r2 = jnp.where(lane == i2, seen, 0.0).sum(axis=-1, keepdims=True)
    carry_ref[...] = carry_ref[...] + hot.sum(axis=0, keepdims=True)
    cnt_ref[...] = carry_ref[...]
    rec = jnp.zeros_like(logits)
    for k, v in ((_META_W1, w1), (_META_W2, e2 * w1), (_META_E1, i1), (_META_E2, i2), (_META_R1, r1), (_META_R2, r2)):
        rec = jnp.where(lane == float(k), v, rec)
    meta_ref[...] = rec


def _router_kernel(h_ref, w_ref, meta_ref, cnt_ref, carry_ref):
    _route(h_ref[...], pl.program_id(0) == 0, w_ref, meta_ref, cnt_ref, carry_ref)


def _router(h, w_router, tm=512):
    m = h.shape[0]
    wp = jnp.zeros((D_MODEL, LANES), F32).at[:, :N_EXPERTS].set(w_router)
    return pl.pallas_call(
        _router_kernel,
        out_shape=(jax.ShapeDtypeStruct((m, LANES), F32), jax.ShapeDtypeStruct((8, LANES), F32)),
        grid=(m // tm,),
        in_specs=[pl.BlockSpec((tm, D_MODEL), lambda i: (i, 0)),
                  pl.BlockSpec((D_MODEL, LANES), lambda i: (0, 0))],
        out_specs=(pl.BlockSpec((tm, LANES), lambda i: (i, 0)), pl.BlockSpec((8, LANES), lambda i: (0, 0))),
        scratch_shapes=[pltpu.VMEM((8, LANES), F32)],
        compiler_params=_cparams("arbitrary"),
        name="moe_router",
    )(h, wp)


MOE_TM = 1024
MOE_BLK = 128
MOE_ROWS = (MOE_TM, 768, 512, 256)


def _moe_plan(meta, counts, m):
    n_steps = -(-2 * m // MOE_TM) + N_EXPERTS
    cnt = counts[0, :N_EXPERTS].astype(jnp.int32)
    steps_e = (cnt + MOE_TM - 1) // MOE_TM
    ends = jnp.cumsum(steps_e)
    first = ends - steps_e
    total = ends[-1]
    e = meta[:, _META_E1:_META_E2 + 1].astype(jnp.int32)
    rank = meta[:, _META_R1:_META_R2 + 1].astype(jnp.int32)
    pos = (first * MOE_TM)[e] + rank
    s_idx = jnp.arange(n_steps, dtype=jnp.int32)
    step_e = jnp.minimum(jnp.searchsorted(ends, s_idx, side="right").astype(jnp.int32), N_EXPERTS - 1)
    valid = jnp.clip(cnt[step_e] - (s_idx - first[step_e]) * MOE_TM, 0, MOE_TM)
    valid = jnp.where(s_idx < total, valid, 0)
    return pos.reshape(-1), step_e, valid, n_steps


def _row_copy(src_ref, src_row, dst_ref, dst_row, sem):
    return pltpu.make_async_copy(src_ref.at[pl.ds(src_row, 1)], dst_ref.at[pl.ds(dst_row, 1)], sem)


def _invert_kernel(pos_ref, sv_ref, src_ref):
    def clear_step(s, c):
        def clear(b, c2):
            for k in range(8):
                src_ref[s * MOE_TM + b * 8 + k] = 0
            return c2

        return lax.fori_loop(sv_ref[s] // 8, MOE_TM // 8, clear, c)

    def place(t, c):
        p0 = pos_ref[2 * t]
        p1 = pos_ref[2 * t + 1]
        src_ref[p0] = t
        src_ref[p1] = t
        return c

    lax.fori_loop(0, sv_ref.shape[0], clear_step, 0)
    lax.fori_loop(0, pos_ref.shape[0] // 2, place, 0, unroll=16)


def _moe_invert(pos, valid, n_rows):
    smem = pl.BlockSpec(memory_space=pltpu.SMEM)
    return pl.pallas_call(
        _invert_kernel,
        out_shape=jax.ShapeDtypeStruct((n_rows,), jnp.int32),
        in_specs=[smem, smem], out_specs=smem,
        name="moe_invert",
    )(pos, valid)


def _block_wait(src_ref, dst_ref, rows, sem):
    pltpu.make_async_copy(src_ref.at[pl.ds(0, rows)], dst_ref.at[pl.ds(0, rows)], sem).wait()


def _moe_ffn_kernel(se_ref, sv_ref, src_ref, h_ref, wg_ref, wu_ref, wd_ref, o_ref, x32_ref, xb_ref, sem):
    del se_ref
    s = pl.program_id(0)
    f = pl.program_id(1)
    valid = sv_ref[s]

    def rows_used(v):
        r = jnp.where(v > 0, MOE_ROWS[-1], 0)
        for small, big in zip(MOE_ROWS[:0:-1], MOE_ROWS[-2::-1]):
            r = jnp.where(v > small, big, r)
        return r

    def blocks(v):
        return rows_used(v) // MOE_BLK

    def gather(step):
        base = step * MOE_TM

        def issue(b, c):
            for k in range(16):
                i = b * 16 + k
                _row_copy(h_ref, src_ref[base + i], x32_ref, i, sem).start()
            return c

        lax.fori_loop(0, rows_used(sv_ref[step]) // 16, issue, 0)

    @pl.when(f == 0)
    def _():
        @pl.when(s == 0)
        def _():
            gather(0)

        def land(i, c):
            _block_wait(h_ref, x32_ref, MOE_BLK, sem)
            return c

        def to_bf16(i, c):
            r0 = pl.multiple_of(i * MOE_BLK, MOE_BLK)
            xb_ref[pl.ds(r0, MOE_BLK), :] = x32_ref[pl.ds(r0, MOE_BLK), :].astype(BF16)
            return c

        lax.fori_loop(0, blocks(valid), land, 0)
        lax.fori_loop(0, blocks(valid), to_bf16, 0)

        @pl.when(s + 1 < pl.num_programs(0))
        def _():
            gather(s + 1)

        o_ref[...] = jnp.zeros_like(o_ref)

    def compute(rows):
        xs = xb_ref[0:rows, :]
        mid = _silu(_dot(xs, wg_ref[...].astype(BF16))) * _dot(xs, wu_ref[...].astype(BF16))
        o_ref[0:rows, :] += _dot(mid.astype(BF16), wd_ref[...].astype(BF16))

    for k, rows in enumerate(MOE_ROWS):
        lower = MOE_ROWS[k + 1] if k + 1 < len(MOE_ROWS) else 0

        @pl.when((valid > lower) & (valid <= rows))
        def _():
            compute(rows)


def _moe_ffn(h, src, w_gate, w_up, w_down, step_e, valid, n_steps, tf=512):
    assert all(r % MOE_BLK == 0 for r in MOE_ROWS) and MOE_ROWS[0] == MOE_TM
    n_ff = w_gate.shape[-1]
    nf = n_ff // tf
    fidx = lambda s, f, sv: jnp.where(sv[s] > 0, f, nf - 1)
    return pl.pallas_call(
        _moe_ffn_kernel,
        out_shape=jax.ShapeDtypeStruct((n_steps * MOE_TM, D_MODEL), F32),
        grid_spec=pltpu.PrefetchScalarGridSpec(
            num_scalar_prefetch=3, grid=(n_steps, nf),
            in_specs=[
                pl.BlockSpec(memory_space=pl.ANY),
                pl.BlockSpec((None, D_MODEL, tf), lambda s, f, se, sv, sr: (se[s], 0, fidx(s, f, sv))),
                pl.BlockSpec((None, D_MODEL, tf), lambda s, f, se, sv, sr: (se[s], 0, fidx(s, f, sv))),
                pl.BlockSpec((None, tf, D_MODEL), lambda s, f, se, sv, sr: (se[s], fidx(s, f, sv), 0)),
            ],
            out_specs=pl.BlockSpec((MOE_TM, D_MODEL), lambda s, f, se, sv, sr: (s, 0), pipeline_mode=pl.Buffered(1)),
            scratch_shapes=[pltpu.VMEM((MOE_TM, D_MODEL), F32), pltpu.VMEM((MOE_TM, D_MODEL), BF16),
                            pltpu.SemaphoreType.DMA(())]),
        compiler_params=_cparams("arbitrary", "arbitrary"),
        name="moe_experts",
    )(step_e, valid, src, h, w_gate, w_up, w_down)


def _combine_kernel(pos_ref, meta_ref, x_ref, m_ref, gf_ref, ys_ref, o_ref, buf_ref, sems, *, tm, final):
    t = pl.program_id(0)

    def request(tile):
        b = tile % 2

        def issue(i, c):
            for slot in range(2):
                _row_copy(ys_ref, pos_ref[2 * (tile * tm + i) + slot], buf_ref.at[b, slot], i, sems.at[b]).start()
            return c

        lax.fori_loop(0, tm, issue, 0, unroll=8)

    @pl.when(t == 0)
    def _():
        request(0)

    @pl.when(t + 1 < pl.num_programs(0))
    def _():
        request(t + 1)

    cur = t % 2
    for slot in range(2):
        _block_wait(ys_ref, buf_ref.at[cur, slot], tm, sems.at[cur])
    meta = meta_ref[...]
    y = meta[:, _META_W1:_META_W1 + 1] * buf_ref[cur, 0] + meta[:, _META_W2:_META_W2 + 1] * buf_ref[cur, 1]
    x = x_ref[...] + m_ref[5:6, :] * y
    if final:
        x = _rms(x, gf_ref[...])
    o_ref[...] = x


def _moe_combine(pos, meta, x, mods, ys, groups, row0, g_final, tm=512):
    m = x.shape[0]
    nr = m // groups // tm
    final = g_final is not None
    gf = (g_final if final else jnp.ones((D_MODEL,), F32)).reshape(1, D_MODEL)
    return pl.pallas_call(
        functools.partial(_combine_kernel, tm=tm, final=final),
        out_shape=jax.ShapeDtypeStruct((m, D_MODEL), F32),
        grid_spec=pltpu.PrefetchScalarGridSpec(
            num_scalar_prefetch=1, grid=(m // tm,),
            in_specs=[pl.BlockSpec((tm, LANES), lambda i, p: (i, 0)),
                      pl.BlockSpec((tm, D_MODEL), lambda i, p: (i, 0)),
                      pl.BlockSpec((None, 6, D_MODEL), lambda i, p: (row0 + i // nr, 0, 0)),
                      pl.BlockSpec((1, D_MODEL), lambda i, p: (0, 0)),
                      pl.BlockSpec(memory_space=pl.ANY)],
            out_specs=pl.BlockSpec((tm, D_MODEL), lambda i, p: (i, 0)),
            scratch_shapes=[pltpu.VMEM((2, 2, tm, D_MODEL), F32), pltpu.SemaphoreType.DMA((2,))]),
        compiler_params=_cparams("arbitrary"),
        name="moe_combine",
    )(pos, meta, x, mods, gf, ys)


def _moe(h, meta, counts, x, mods, groups, row0, w_gate, w_up, w_down, g_final):
    m = h.shape[0]
    pos, step_e, valid, n_steps = _moe_plan(meta, counts, m)
    src = _moe_invert(pos, valid, n_steps * MOE_TM)
    ys = _moe_ffn(h, src, w_gate, w_up, w_down, step_e, valid, n_steps)
    return _moe_combine(pos, meta, x, mods, ys, groups, row0, g_final)


def _ffn_kernel(h_ref, wg_ref, wu_ref, wd_ref, x_ref, m_ref, gn_ref, mn_ref, o_ref, *maybe_hn_ref):
    f = pl.program_id(2)

    @pl.when(f == 0)
    def _():
        o_ref[...] = jnp.zeros_like(o_ref)

    h = h_ref[...]
    mid = _silu(_dot(h, wg_ref[...])) * _dot(h, wu_ref[...])
    o_ref[...] += _dot(mid.astype(BF16), wd_ref[...])

    @pl.when(f == pl.num_programs(2) - 1)
    def _():
        x = x_ref[...] + m_ref[5:6, :] * o_ref[...]
        o_ref[...] = x
        for hn_ref in maybe_hn_ref:
            hn_ref[...] = (_rms(x, gn_ref[...]) * (1.0 + mn_ref[1:2, :]) + mn_ref[0:1, :]).astype(hn_ref.dtype)


def _ffn(h, w_gate, w_up, w_down, x, mods, groups, row0, nxt, tm, tf=512):
    m = x.shape[0]
    n_ff = w_gate.shape[1]
    nr = m // groups // tm
    g_next, mods_next = nxt if nxt is not None else (jnp.ones((D_MODEL,), F32), mods)
    once = pl.Buffered(1)
    tile = lambda n, mode=None: pl.BlockSpec((tm, n), lambda g_, i, f: (g_ * nr + i, 0), pipeline_mode=mode)
    mod_spec = pl.BlockSpec((None, 6, D_MODEL), lambda g_, i, f: (row0 + g_, 0, 0))
    out_shape = [jax.ShapeDtypeStruct((m, D_MODEL), F32)]
    out_specs = [tile(D_MODEL, once)]
    if nxt is not None:
        out_shape.append(jax.ShapeDtypeStruct((m, D_MODEL), BF16))
        out_specs.append(tile(D_MODEL, once))
    outs = pl.pallas_call(
        _ffn_kernel,
        out_shape=tuple(out_shape),
        grid=(groups, nr, n_ff // tf),
        in_specs=[tile(D_MODEL),
                  pl.BlockSpec((D_MODEL, tf), lambda g_, i, f: (0, f)),
                  pl.BlockSpec((D_MODEL, tf), lambda g_, i, f: (0, f)),
                  pl.BlockSpec((tf, D_MODEL), lambda g_, i, f: (f, 0)),
                  tile(D_MODEL, once), mod_spec,
                  pl.BlockSpec((1, D_MODEL), lambda g_, i, f: (0, 0)), mod_spec],
        out_specs=tuple(out_specs),
        compiler_params=_cparams("parallel", "parallel", "arbitrary"),
        name="swiglu_ffn",
    )(h, w_gate, w_up, w_down, x, mods, g_next.reshape(1, D_MODEL), mods_next)
    return outs if nxt is not None else (outs[0], None)


def _swap_halves(w):
    half = w.shape[-1] // 2
    return jnp.concatenate([w[..., half:], w[..., :half]], axis=-1)


def _swap_heads(w, heads, dim):
    k = w.shape[0]
    return _swap_halves(w.reshape(k, heads, dim)).reshape(k, heads * dim)


def _rope_tables():
    rows = SEQ // GRID_W
    row = np.repeat(np.arange(rows), GRID_W).astype(np.float32)
    col = np.tile(np.arange(GRID_W), rows).astype(np.float32)
    quarter = MLA_ROPE // 4
    freqs = (np.float32(ROPE_THETA) ** (-np.arange(quarter, dtype=np.float32) / quarter)).astype(np.float32)
    ang = np.concatenate([row[:, None] * freqs[None], col[:, None] * freqs[None]], axis=-1)
    cos = np.cos(ang.astype(np.float64))
    sin = np.sin(ang.astype(np.float64))
    cos2 = np.concatenate([cos, cos], axis=-1)
    sin2 = np.concatenate([-sin, sin], axis=-1)
    tab_lat = np.concatenate([cos2, sin2], axis=-1)
    tab_ctx = np.concatenate([np.ones((CTX_LEN, MLA_ROPE)), np.zeros((CTX_LEN, MLA_ROPE))], axis=-1)
    return dict(
        tab_lat=jnp.asarray(tab_lat, F32), tab_ctx=jnp.asarray(tab_ctx, F32),
        cos8=jnp.asarray(np.tile(cos2, (1, SWA_HEADS)), F32),
        sin8=jnp.asarray(np.tile(sin2, (1, SWA_HEADS)), F32),
        ones8=jnp.ones((CTX_LEN, _SWA_Q), F32), zeros8=jnp.zeros((CTX_LEN, _SWA_Q), F32),
    )


def _layer_weights(p, l):
    w_in = p["w_in"][l]
    cq = w_in[:, _O_CQ:_O_CKV]
    ckv = w_in[:, _O_CKV:_O_KPE]
    kpe = w_in[:, _O_KPE:_O_HY]
    hy = w_in[:, _O_HY:_O_SQ]
    sq = w_in[:, _O_SQ:_O_SK]
    sk = w_in[:, _O_SK:_O_SV]
    sv = w_in[:, _O_SV:]
    dq = MLA_NOPE + MLA_ROPE
    wq = p["w_q_up"][l].reshape(MLA_Q_LORA, MLA_HEADS, dq)
    wq = jnp.concatenate([wq, _swap_halves(wq[..., MLA_NOPE:])], axis=-1)
    wkv = p["w_kv_up"][l].reshape(MLA_KV_LORA, MLA_HEADS, MLA_NOPE + MLA_V)
    return dict(
        w_cq=cq.astype(BF16),
        w_ckv=jnp.concatenate([ckv, kpe, _swap_halves(kpe)], axis=-1).astype(BF16),
        w_hy=hy.astype(BF16),
        w_swa=jnp.concatenate([sq, _swap_heads(sq, SWA_HEADS, SWA_HEAD_DIM), sk,
                               _swap_heads(sk, SWA_KV_HEADS, SWA_HEAD_DIM), sv], axis=-1).astype(BF16),
        w_q=jnp.transpose(wq, (1, 0, 2)).astype(BF16),
        w_kv=jnp.transpose(wkv, (1, 0, 2)).astype(BF16),
        w_out=p["w_out"][l].astype(BF16),
        g_q=p["g_q"][l], g_kv=p["g_kv"][l],
        hy_conv_w=p["hy_conv_w"][l], hy_conv_b=p["hy_conv_b"][l],
        hy_w1=p["hy_w1"][l], hy_b1=p["hy_b1"][l], hy_w2=p["hy_w2"][l], hy_b2=p["hy_b2"][l],
        hy_w3=p["hy_w3"][l], hy_b3=p["hy_b3"][l], hy_w_filt=p["hy_w_filt"][l],
        hy_freq=p["hy_freq"][l], hy_skip=p["hy_skip"][l], swa_sink=p["swa_sink"][l],
    )


def _hyena(hy, lw, n_len, tabs):
    x0, vx, vxb = _hy_pre(hy, lw["hy_conv_w"], lw["hy_conv_b"], n_len)
    spectra = _hy_filters(lw, n_len, tabs)
    y = _hy_long_conv(x0, vx, vxb, spectra, lw["hy_skip"], n_len, tabs)
    return y.reshape(BATCH * n_len, HY_WIDTH)


def _mixer(h_lat, h_ctx, lw, rope, hy_tabs, need_ctx):
    q_l, k_l, v_l = _mla_proj(h_lat, lw, rope["tab_lat"], tm=1024)
    if need_ctx:
        q_c, k_c, v_c = _mla_proj(h_ctx, lw, rope["tab_ctx"], tm=CTX_LEN)
    else:
        k_c, v_c = _kv_proj(h_ctx, lw["w_ckv"], lw["g_kv"], lw["w_kv"], rope["tab_ctx"], tm=CTX_LEN)
    a_l = _mla_attention(q_l, [(k_l, v_l, SEQ), (k_c, v_c, CTX_LEN)], SEQ, tq=2048)
    hy_l, sq_l, kk_l, vv_l = _swa_proj(h_lat, lw["w_swa"], rope["cos8"], rope["sin8"], True, w_hy=lw["w_hy"],
                                       tm=1024)
    y_l = _hyena(hy_l, lw, SEQ, hy_tabs[SEQ])
    if need_ctx:
        hy_c, sq_c, kk_c, vv_c = _swa_proj(h_ctx, lw["w_swa"], rope["ones8"], rope["zeros8"], False,
                                           w_hy=lw["w_hy"], tm=CTX_LEN)
    else:
        sq_c, kk_c, vv_c = _swa_proj(h_ctx, lw["w_swa"], rope["ones8"], rope["zeros8"], False, tm=CTX_LEN)
    s_l = _swa_attention(lw["swa_sink"], sq_l, (kk_l, vv_l), (kk_c, vv_c), SEQ)
    if not need_ctx:
        return (a_l, y_l, s_l), None
    a_c = _mla_attention(q_c, [(k_c, v_c, CTX_LEN)], CTX_LEN, tq=CTX_LEN)
    y_c = _hyena(hy_c, lw, CTX_LEN, hy_tabs[CTX_LEN])
    s_c = _swa_attention(lw["swa_sink"], sq_c, None, (kk_c, vv_c), CTX_LEN, tq=CTX_LEN)
    return (a_l, y_l, s_l), (a_c, y_c, s_c)


def _forward(p):
    rope = _rope_tables()
    hy_tabs = {n: dict(dft=_dft_tables(n), filt=_filter_tables(n)) for n in (SEQ, CTX_LEN)}
    cvec = jnp.zeros((MOD_ROWS, D_MODEL), F32).at[:BATCH].set(p["c"]).at[BATCH].set(p["c_ctx"])
    mods_all = _modulation(cvec, p["w_mod"], p["b_mod"]).reshape(DEPTH, MOD_ROWS, 6, D_MODEL)

    x_lat = p["x"].reshape(BATCH * SEQ, D_MODEL)
    x_ctx = p["ctx"].reshape(BATCH * CTX_LEN, D_MODEL)
    h_in = [None, None]
    for l in range(DEPTH):
        last = l == DEPTH - 1
        mods = mods_all[l]
        lw = _layer_weights(p, l)
        h_lat = h_in[0] if h_in[0] is not None else _norm_mod(x_lat, p["g_mix"][l], mods, BATCH, 0, 0, BF16)
        h_ctx = h_in[1] if h_in[1] is not None else _norm_mod(x_ctx, p["g_mix"][l], mods, 1, BATCH, 0, BF16)
        mix_l, mix_c = _mixer(h_lat, h_ctx, lw, rope, hy_tabs, not last)
        streams = [(x_lat, mix_l, BATCH, 0, 1024)]
        if not last:
            streams.append((x_ctx, mix_c, 1, BATCH, 512))
        outs = []
        h_in = [None, None]
        i = l // 2
        dense = l % 2 == 0
        for n, (x, mix, groups, row0, ffn_tm) in enumerate(streams):
            if dense:
                x, h2 = _out_proj(*mix, lw["w_out"], x, mods, p["g_ffn"][l], groups, row0, BF16)
                nxt = None if last else (p["g_mix"][l + 1], mods_all[l + 1])
                x, h_in[n] = _ffn(h2, p["ffn_w_gate"][i].astype(BF16), p["ffn_w_up"][i].astype(BF16),
                                  p["ffn_w_down"][i].astype(BF16), x, mods, groups, row0, nxt, ffn_tm)
            else:
                x, h2 = _out_proj(*mix, lw["w_out"], x, mods, p["g_ffn"][l], groups, row0, F32)
                meta, counts = _router(h2, p["moe_router"][i])
                g_final = p["g_final"] if (last and n == 0) else None
                x = _moe(h2, meta, counts, x, mods, groups, row0, p["moe_w_gate"][i], p["moe_w_up"][i],
                         p["moe_w_down"][i], g_final)
            outs.append(x)
        x_lat = outs[0]
        if not last:
            x_ctx = outs[1]
    if DEPTH % 2 == 1:
        x_lat = _norm_mod(x_lat, p["g_final"], mods_all[0], BATCH, 0, None, F32)
    return x_lat.reshape(BATCH, SEQ, D_MODEL)


def kernel(x, c, ctx, c_ctx, w_mod, b_mod, g_mix, g_ffn, w_in, g_q, w_q_up, g_kv, w_kv_up, hy_conv_w, hy_conv_b, hy_w1, hy_b1, hy_w2, hy_b2, hy_w3, hy_b3, hy_w_filt, hy_freq, hy_skip, swa_sink, w_out, ffn_w_gate, ffn_w_up, ffn_w_down, moe_router, moe_w_gate, moe_w_up, moe_w_down, g_final):
    return _forward(dict(
        x=x, c=c, ctx=ctx, c_ctx=c_ctx, w_mod=w_mod, b_mod=b_mod, g_mix=g_mix, g_ffn=g_ffn, w_in=w_in,
        g_q=g_q, w_q_up=w_q_up, g_kv=g_kv, w_kv_up=w_kv_up, hy_conv_w=hy_conv_w, hy_conv_b=hy_conv_b,
        hy_w1=hy_w1, hy_b1=hy_b1, hy_w2=hy_w2, hy_b2=hy_b2, hy_w3=hy_w3, hy_b3=hy_b3,
        hy_w_filt=hy_w_filt, hy_freq=hy_freq, hy_skip=hy_skip, swa_sink=swa_sink, w_out=w_out,
        ffn_w_gate=ffn_w_gate, ffn_w_up=ffn_w_up, ffn_w_down=ffn_w_down, moe_router=moe_router,
        moe_w_gate=moe_w_gate, moe_w_up=moe_w_up, moe_w_down=moe_w_down, g_final=g_final))
```

```python
import functools
import math

import numpy as np
import jax
import jax.numpy as jnp
from jax import lax
from jax.experimental import pallas as pl
from jax.experimental.pallas import tpu as pltpu

F32 = jnp.float32
BF16 = jnp.bfloat16

D_MODEL = 2048
BATCH = 4
SEQ = 2048
DEPTH = 2
CTX_LEN = 256
GRID_W = 64
NORM_EPS = 1e-6
ROPE_THETA = 10000.0
MLA_HEADS = 8
MLA_NOPE = 128
MLA_ROPE = 64
MLA_V = 128
MLA_Q_LORA = 768
MLA_KV_LORA = 512
HY_WIDTH = 512
HY_BANDS = 16
HY_EMB = 1 + 2 * HY_BANDS
HY_FILTER_HIDDEN = 64
HY_DECAY_TARGET = 1e-2
HY_QUICK_DECAY_PCT = 0.3
HY_SLOW_DECAY_PCT = 1.5
SWA_HEADS = 8
SWA_KV_HEADS = 2
SWA_HEAD_DIM = 64
SWA_WINDOW = 128
N_EXPERTS = 8
D_FF = 5632
D_FF_EXPERT = 7168

LANES = 128
MXU_DIM = 256
VMEM_LIMIT_BYTES = 56 * 1024 * 1024
NEG_BIG = -1e30
MOD_ROWS = 8

ROW_TILE = 512
LATENT_ROW_TILE = 1024
FF_TILE = 512

_O_CQ = 0
_O_CKV = _O_CQ + MLA_Q_LORA
_O_KPE = _O_CKV + MLA_KV_LORA
_O_HY = _O_KPE + MLA_ROPE
_O_SQ = _O_HY + 3 * HY_WIDTH
_O_SK = _O_SQ + SWA_HEADS * SWA_HEAD_DIM
_O_SV = _O_SK + SWA_KV_HEADS * SWA_HEAD_DIM


def _cparams(*sem):
    return pltpu.CompilerParams(dimension_semantics=sem, vmem_limit_bytes=VMEM_LIMIT_BYTES)


def _dot(a, b):
    return jnp.dot(a, b, preferred_element_type=F32)


def _dot_nt(a, b):
    return lax.dot_general(a, b, (((1,), (1,)), ((), ())), preferred_element_type=F32)


def _split_bf16(a):
    hi = a.astype(BF16)
    lo = (a - hi.astype(F32)).astype(BF16)
    return hi, lo


def _dot3(a, b):
    ah, al = _split_bf16(a)
    bh, bl = _split_bf16(b)
    return _dot(ah, bh) + (_dot(al, bh) + _dot(ah, bl))


def _silu(x):
    return x / (1.0 + jnp.exp(-x))


def _rms(x, g):
    ms = jnp.mean(x * x, axis=-1, keepdims=True)
    return x * lax.rsqrt(ms + NORM_EPS) * g


def _const_spec(shape):
    nd = len(shape)
    return pl.BlockSpec(shape, lambda *_: (0,) * nd, pipeline_mode=pl.Buffered(1))


def _mod_kernel(c_ref, w_ref, b_ref, o_ref):
    ah, al = _split_bf16(_silu(c_ref[...]))
    wh, wl = _split_bf16(w_ref[0])
    r = _dot(jnp.concatenate([ah, al], axis=0), wh)
    o_ref[0] = r[:MOD_ROWS] + (r[MOD_ROWS:] + _dot(ah, wl)) + b_ref[0]


def _modulation(cvec, w_mod, b_mod):
    tn = 1024
    n = 6 * D_MODEL
    return pl.pallas_call(
        _mod_kernel,
        out_shape=jax.ShapeDtypeStruct((DEPTH, MOD_ROWS, n), F32),
        grid=(DEPTH, n // tn),
        in_specs=[
            pl.BlockSpec((MOD_ROWS, D_MODEL), lambda l, j: (0, 0)),
            pl.BlockSpec((1, D_MODEL, tn), lambda l, j: (l, 0, j)),
            pl.BlockSpec((1, 1, tn), lambda l, j: (l, 0, j)),
        ],
        out_specs=pl.BlockSpec((1, MOD_ROWS, tn), lambda l, j: (l, 0, j)),
        compiler_params=_cparams("arbitrary", "arbitrary"),
        name="adaln_mod",
    )(cvec, w_mod, b_mod.reshape(DEPTH, 1, n))


def _norm_kernel(x_ref, g_ref, m_ref, o_ref, *, si):
    y = _rms(x_ref[...], g_ref[...])
    if si is not None:
        y = y * (1.0 + m_ref[si + 1:si + 2, :]) + m_ref[si:si + 1, :]
    o_ref[...] = y.astype(o_ref.dtype)


def _norm_mod(x, g, mods, groups, row0, si, out_dtype, tm=ROW_TILE):
    m = x.shape[0]
    nr = m // groups // tm
    return pl.pallas_call(
        functools.partial(_norm_kernel, si=si),
        out_shape=jax.ShapeDtypeStruct((m, D_MODEL), out_dtype),
        grid=(groups, nr),
        in_specs=[
            pl.BlockSpec((tm, D_MODEL), lambda g_, i: (g_ * nr + i, 0)),
            pl.BlockSpec((1, D_MODEL), lambda g_, i: (0, 0)),
            pl.BlockSpec((None, 6, D_MODEL), lambda g_, i: (row0 + g_, 0, 0)),
        ],
        out_specs=pl.BlockSpec((tm, D_MODEL), lambda g_, i: (g_ * nr + i, 0)),
        compiler_params=_cparams("parallel", "parallel"),
        name="norm_mod",
    )(x, g.reshape(1, D_MODEL), mods)


def _rope_halves(pe_pair, tab):
    r = pe_pair * tab
    return r + pltpu.roll(r, MLA_ROPE, axis=1)


MLA_Q_SCALE = (MLA_NOPE + MLA_ROPE) ** -0.5 * math.log2(math.e)


def _qproj_body(h_ref, wc_ref, g_ref, wq_ref, tab_ref, o_ref):
    cq = _dot(h_ref[...], wc_ref[...])
    cqn = _rms(cq, g_ref[...]).astype(BF16)
    tab = tab_ref[...] * MLA_Q_SCALE
    for hh in range(MLA_HEADS):
        r = _dot(cqn, wq_ref[hh])
        o_ref[hh, :, 0:MLA_NOPE] = (r[:, :MLA_NOPE] * MLA_Q_SCALE).astype(BF16)
        o_ref[hh, :, MLA_NOPE:] = _rope_halves(r[:, MLA_NOPE:], tab).astype(BF16)


def _mla_proj_kernel(h_ref, wcq_ref, gq_ref, wq_ref, wckv_ref, gkv_ref, wkv_ref, tab_ref, q_ref, k_ref, v_ref):
    _qproj_body(h_ref, wcq_ref, gq_ref, wq_ref, tab_ref, q_ref)
    _kvproj_kernel(h_ref, wckv_ref, gkv_ref, wkv_ref, tab_ref, k_ref, v_ref)


def _mla_proj(h, lw, tab, tm=ROW_TILE):
    m = h.shape[0]
    nt = tab.shape[0] // tm
    head_out = jax.ShapeDtypeStruct((MLA_HEADS, m, MXU_DIM), BF16)
    head_spec = pl.BlockSpec((MLA_HEADS, tm, MXU_DIM), lambda i: (0, i, 0))
    return pl.pallas_call(
        _mla_proj_kernel,
        out_shape=(head_out, head_out, head_out),
        grid=(m // tm,),
        in_specs=[
            pl.BlockSpec((tm, D_MODEL), lambda i: (i, 0)),
            _const_spec(lw["w_cq"].shape), _const_spec((1, MLA_Q_LORA)), _const_spec(lw["w_q"].shape),
            _const_spec(lw["w_ckv"].shape), _const_spec((1, MLA_KV_LORA)), _const_spec(lw["w_kv"].shape),
            pl.BlockSpec((tm, LANES), lambda i: (i % nt, 0)),
        ],
        out_specs=(head_spec, head_spec, head_spec),
        compiler_params=_cparams("parallel"),
        name="mla_proj",
    )(h, lw["w_cq"], lw["g_q"].reshape(1, -1), lw["w_q"], lw["w_ckv"], lw["g_kv"].reshape(1, -1), lw["w_kv"], tab)


def _kvproj_kernel(h_ref, wc_ref, g_ref, wkv_ref, tab_ref, k_ref, v_ref):
    t = _dot(h_ref[...], wc_ref[...])
    ckvn = _rms(t[:, :MLA_KV_LORA], g_ref[...]).astype(BF16)
    rot = _rope_halves(t[:, MLA_KV_LORA:], tab_ref[...])
    lane = lax.broadcasted_iota(jnp.int32, rot.shape, 1)
    kpe = jnp.where(lane < MLA_ROPE, rot, 0.0).astype(BF16)
    for hh in range(MLA_HEADS):
        r = _dot(ckvn, wkv_ref[hh])
        k_ref[hh, :, 0:MLA_NOPE] = r[:, :MLA_NOPE].astype(BF16)
        k_ref[hh, :, MLA_NOPE:] = kpe
        v_ref[hh, :, 0:MLA_V] = r[:, MLA_NOPE:].astype(BF16)
        v_ref[hh, :, MLA_V:] = jnp.ones((r.shape[0], MXU_DIM - MLA_V), BF16)


def _kv_proj(h, w_ckv, g_kv, w_kv, tab, tm=ROW_TILE):
    m = h.shape[0]
    nt = tab.shape[0] // tm
    return pl.pallas_call(
        _kvproj_kernel,
        out_shape=(jax.ShapeDtypeStruct((MLA_HEADS, m, MXU_DIM), BF16),
                   jax.ShapeDtypeStruct((MLA_HEADS, m, MXU_DIM), BF16)),
        grid=(m // tm,),
        in_specs=[
            pl.BlockSpec((tm, D_MODEL), lambda i: (i, 0)),
            _const_spec(w_ckv.shape),
            _const_spec((1, MLA_KV_LORA)),
            _const_spec(w_kv.shape),
            pl.BlockSpec((tm, LANES), lambda i: (i % nt, 0)),
        ],
        out_specs=(pl.BlockSpec((MLA_HEADS, tm, MXU_DIM), lambda i: (0, i, 0)),
                   pl.BlockSpec((MLA_HEADS, tm, MXU_DIM), lambda i: (0, i, 0))),
        compiler_params=_cparams("parallel"),
        name="mla_kv_proj",
    )(h, w_ckv, g_kv.reshape(1, -1), w_kv, tab)


MLA_KEY_CHUNK = 512


def _mla_kernel(q_ref, *refs, seg_lens):
    nseg = len(seg_lens)
    o_ref = refs[2 * nseg]
    q = q_ref[...]
    m = jnp.full((q.shape[0], 1), NEG_BIG, F32)
    acc = jnp.zeros((q.shape[0], MXU_DIM), F32)
    for s, lk in enumerate(seg_lens):
        k_ref, v_ref = refs[2 * s], refs[2 * s + 1]
        step = min(MLA_KEY_CHUNK, lk)
        for c0 in range(0, lk, step):
            sc = _dot_nt(q, k_ref[c0:c0 + step, :])
            m_new = jnp.maximum(m, sc.max(axis=-1, keepdims=True))
            p = jnp.exp2(sc - m_new).astype(BF16)
            acc = acc * jnp.exp2(m - m_new) + _dot(p, v_ref[c0:c0 + step, :])
            m = m_new
    o_ref[...] = (acc[:, :MLA_V] / acc[:, MLA_V:MLA_V + 1]).astype(o_ref.dtype)


def _mla_attention(q, segs, lq, tq=ROW_TILE):
    nq = lq // tq
    in_specs = [pl.BlockSpec((None, tq, MXU_DIM), lambda b, h, i: (h, b * nq + i, 0))]
    args = [q]
    for k, v, lk in segs:
        in_specs.append(pl.BlockSpec((None, lk, MXU_DIM), lambda b, h, i: (h, b, 0)))
        in_specs.append(pl.BlockSpec((None, lk, MXU_DIM), lambda b, h, i: (h, b, 0)))
        args += [k, v]
    return pl.pallas_call(
        functools.partial(_mla_kernel, seg_lens=tuple(lk for _, _, lk in segs)),
        out_shape=jax.ShapeDtypeStruct((BATCH * lq, MLA_HEADS * MLA_V), BF16),
        grid=(BATCH, MLA_HEADS, nq),
        in_specs=in_specs,
        out_specs=pl.BlockSpec((tq, MLA_V), lambda b, h, i: (b * nq + i, h)),
        compiler_params=_cparams("parallel", "parallel", "arbitrary"),
        name="mla_attention",
    )(*args)


def _dft_tables(n_len):
    n2 = 2 * n_len
    idx = np.arange(n_len, dtype=np.int64)
    ang = (2.0 * np.pi / n2) * ((idx[:, None] * idx[None, :]) % n2).astype(np.float64)
    cm = np.cos(ang)
    sf = np.sin(ang)
    sf[0, :] = 1.0 - 2.0 * (idx % 2)
    return (jnp.asarray(cm, dtype=BF16), jnp.asarray(sf, dtype=BF16),
            jnp.asarray(sf.T.copy(), dtype=BF16))


def _filter_tables(n_len):
    pos = np.arange(n_len, dtype=np.float64)
    t = pos / max(n_len - 1, 1)
    bands = np.linspace(1e-4, HY_BANDS - 1, HY_BANDS)
    ang = (2.0 * math.pi / n_len) * pos[:, None] * bands[None]
    z = np.zeros((n_len, LANES), np.float64)
    z[:, 0] = t
    z[:, 1:1 + HY_BANDS] = np.cos(ang)
    z[:, 1 + HY_BANDS:HY_EMB] = -np.sin(ang)
    deltas = np.linspace(math.log(HY_DECAY_TARGET) / HY_SLOW_DECAY_PCT,
                         math.log(HY_DECAY_TARGET) / HY_QUICK_DECAY_PCT, HY_WIDTH)
    decay = np.exp(-t[:, None] * np.abs(deltas)[None])
    return jnp.asarray(z, dtype=F32), jnp.asarray(decay, dtype=F32)


def _hy_filter_kernel(z_ref, w1_ref, b1_ref, w2_ref, b2_ref, w3_ref, b3_ref, wf_ref, wb_ref,
                      fr_ref, dec_ref, cm_ref, sf_ref, a_ref, b_ref, d_ref, hid_ref, *, n_len):
    @pl.when(pl.program_id(0) == 0)
    def _():
        fr = fr_ref[...]
        h = jnp.sin(fr * (_dot3(z_ref[...], w1_ref[...]) + b1_ref[...]))
        h = jnp.sin(fr * (_dot3(h, w2_ref[...]) + b2_ref[...]))
        hid_ref[...] = jnp.sin(fr * (_dot3(h, w3_ref[...]) + b3_ref[...]))

    h = hid_ref[...]
    dec = dec_ref[...]
    h_f = _dot3(h, wf_ref[...]) * dec
    h_b = _dot3(h, wb_ref[...]) * dec
    row = lax.broadcasted_iota(jnp.int32, h_f.shape, 0)
    h_b = jnp.where(row == 0, 0.0, h_b)
    hs = h_f + h_b
    hd = h_f - h_b
    nc = hs.shape[1]
    hs2 = jnp.concatenate(_split_bf16(hs), axis=1)
    hd2 = jnp.concatenate(_split_bf16(hd), axis=1)
    sf = sf_ref[...]
    t2 = _dot(cm_ref[...], hs2)
    g2 = _dot(sf, hd2)
    n2 = _dot(sf[0:16, :], hs2)[0:1, :]
    t_re = t2[:, :nc] + t2[:, nc:]
    g_t = g2[:, :nc] + g2[:, nc:]
    t_ny = n2[:, :nc] + n2[:, nc:]
    inv_n = 1.0 / (2 * n_len)
    first = row == 0
    a_ref[...] = jnp.where(first, inv_n, 2.0 * inv_n) * t_re
    b_ref[...] = jnp.where(first, 0.0, -2.0 * inv_n * g_t)
    d_ref[...] = jnp.where(first, inv_n * t_ny, 2.0 * inv_n * t_re)


def _hy_filters(lp, n_len, tabs):
    z, decay = tabs["filt"]
    cm, sf, _ = tabs["dft"]
    cb = LANES
    nblk = HY_WIDTH // cb
    w1 = jnp.zeros((LANES, HY_FILTER_HIDDEN), F32).at[:HY_EMB].set(lp["hy_w1"])
    row = lambda a: a.reshape(1, -1)
    small = lambda shape: pl.BlockSpec(shape, lambda j: (0,) * len(shape))
    fh = HY_FILTER_HIDDEN
    out = jax.ShapeDtypeStruct((n_len, HY_WIDTH), F32)
    return pl.pallas_call(
        functools.partial(_hy_filter_kernel, n_len=n_len),
        out_shape=(out, out, out),
        grid=(nblk,),
        in_specs=[
            small((n_len, LANES)), small((LANES, fh)), small((1, fh)), small((fh, fh)), small((1, fh)),
            small((fh, fh)), small((1, fh)),
            pl.BlockSpec((fh, cb), lambda j: (0, j)),
            pl.BlockSpec((fh, cb), lambda j: (0, nblk + j)),
            small((1, fh)),
            pl.BlockSpec((n_len, cb), lambda j: (0, j)),
            _const_spec((n_len, n_len)), _const_spec((n_len, n_len)),
        ],
        out_specs=tuple(pl.BlockSpec((n_len, cb), lambda j: (0, j)) for _ in range(3)),
        scratch_shapes=[pltpu.VMEM((n_len, fh), F32)],
        compiler_params=_cparams("arbitrary"),
        name="hyena_filters",
    )(z, w1, row(lp["hy_b1"]), lp["hy_w2"], row(lp["hy_b2"]), lp["hy_w3"], row(lp["hy_b3"]),
      lp["hy_w_filt"], lp["hy_w_filt"], row(lp["hy_freq"]), decay, cm, sf)


def _hy_pre_kernel(u0_ref, u1_ref, u2_ref, w0_ref, w1_ref, w2_ref, b0_ref, b1_ref, b2_ref,
                   x0_ref, vx_ref, vxb_ref, *, n_len):
    def sconv(u_ref, w_ref, b_ref):
        u = u_ref[...]
        w = w_ref[...]
        row = lax.broadcasted_iota(jnp.int32, u.shape, 0)
        prev = jnp.where(row == 0, 0.0, pltpu.roll(u, 1, axis=0))
        nxt = jnp.where(row == n_len - 1, 0.0, pltpu.roll(u, n_len - 1, axis=0))
        return w[0:1] * prev + w[1:2] * u + w[2:3] * nxt + b_ref[...]

    x0_ref[...] = sconv(u0_ref, w0_ref, b0_ref)
    vx = sconv(u2_ref, w2_ref, b2_ref) * sconv(u1_ref, w1_ref, b1_ref)
    vx_ref[...] = vx
    vxb_ref[...] = vx.astype(BF16)


def _hy_pre(hy, conv_w, conv_b, n_len):
    nblk = HY_WIDTH // LANES
    uspec = lambda part: pl.BlockSpec((None, n_len, LANES), lambda b, j: (b, 0, part * nblk + j))
    wspec = lambda part: pl.BlockSpec((3, LANES), lambda b, j: (0, part * nblk + j))
    bspec = lambda part: pl.BlockSpec((1, LANES), lambda b, j: (0, part * nblk + j))
    ospec = pl.BlockSpec((None, n_len, LANES), lambda b, j: (b, 0, j))
    hy3 = hy.reshape(BATCH, n_len, 3 * HY_WIDTH)
    cb = conv_b.reshape(1, -1)
    return pl.pallas_call(
        functools.partial(_hy_pre_kernel, n_len=n_len),
        out_shape=(jax.ShapeDtypeStruct((BATCH, n_len, HY_WIDTH), F32),
                   jax.ShapeDtypeStruct((BATCH, n_len, HY_WIDTH), F32),
                   jax.ShapeDtypeStruct((BATCH, n_len, HY_WIDTH), BF16)),
        grid=(BATCH, nblk),
        in_specs=[uspec(0), uspec(1), uspec(2), wspec(0), wspec(1), wspec(2), bspec(0), bspec(1), bspec(2)],
        out_specs=(ospec, ospec, ospec),
        compiler_params=_cparams("parallel", "parallel"),
        name="hyena_short_conv",
    )(hy3, hy3, hy3, conv_w, conv_w, conv_w, cb, cb, cb)


def _hy_fwd_kernel(cm_ref, sf_ref, u_ref, a_ref, b_ref, d_ref, q_ref):
    u = u_ref[...]
    ur = _dot(cm_ref[...], u)
    g = _dot(sf_ref[...], u)
    b = b_ref[...]
    q_ref[0] = (a_ref[...] * ur + b * g).astype(BF16)
    q_ref[1] = (d_ref[...] * g - b * ur).astype(BF16)


def _hy_inv_kernel(cm_ref, si_ref, q_ref, x0_ref, vx_ref, skip_ref, o_ref):
    y = _dot(cm_ref[...], q_ref[0]) + _dot(si_ref[...], q_ref[1])
    o_ref[...] = (x0_ref[...] * (y + vx_ref[...] * skip_ref[...])).astype(o_ref.dtype)


def _hy_long_conv(x0, vx, vxb, spectra, skip, n_len, tabs):
    cm, sf, si = tabs["dft"]
    a, b, d = spectra
    tf = min(1024, n_len)
    nf = n_len // tf
    mat = pl.BlockSpec((tf, n_len), lambda bb, f: (f, 0))
    spec = pl.BlockSpec((tf, HY_WIDTH), lambda bb, f: (f, 0))
    q = pl.pallas_call(
        _hy_fwd_kernel,
        out_shape=jax.ShapeDtypeStruct((BATCH, 2, n_len, HY_WIDTH), BF16),
        grid=(BATCH, nf),
        in_specs=[mat, mat, pl.BlockSpec((None, n_len, HY_WIDTH), lambda bb, f: (bb, 0, 0)), spec, spec, spec],
        out_specs=pl.BlockSpec((None, 2, tf, HY_WIDTH), lambda bb, f: (bb, 0, f, 0)),
        compiler_params=_cparams("parallel", "arbitrary"),
        name="hyena_dft_fwd",
    )(cm, sf, vxb, a, b, d)
    tile = pl.BlockSpec((None, tf, HY_WIDTH), lambda bb, f: (bb, f, 0))
    return pl.pallas_call(
        _hy_inv_kernel,
        out_shape=jax.ShapeDtypeStruct((BATCH, n_len, HY_WIDTH), BF16),
        grid=(BATCH, nf),
        in_specs=[mat, mat, pl.BlockSpec((None, 2, n_len, HY_WIDTH), lambda bb, f: (bb, 0, 0, 0)),
                  tile, tile, pl.BlockSpec((1, HY_WIDTH), lambda bb, f: (0, 0))],
        out_specs=tile,
        compiler_params=_cparams("parallel", "arbitrary"),
        name="hyena_dft_inv",
    )(cm, si, q, x0, vx, skip.reshape(1, -1))


_SWA_Q = SWA_HEADS * SWA_HEAD_DIM
_SWA_KV = SWA_KV_HEADS * SWA_HEAD_DIM


def _swaproj_kernel(h_ref, w_ref, cos_ref, sin_ref, q_ref, k_ref, v_ref, *, rope):
    t = _dot(h_ref[...], w_ref[...])
    q = t[:, :_SWA_Q]
    k = t[:, 2 * _SWA_Q:2 * _SWA_Q + _SWA_KV]
    v = t[:, 2 * _SWA_Q + 2 * _SWA_KV:]
    if rope:
        cos = cos_ref[...]
        sin = sin_ref[...]
        q = q * cos + t[:, _SWA_Q:2 * _SWA_Q] * sin
        k = k * cos[:, :_SWA_KV] + t[:, 2 * _SWA_Q + _SWA_KV:2 * _SWA_Q + 2 * _SWA_KV] * sin[:, :_SWA_KV]
    q_ref[...] = (q * (SWA_HEAD_DIM ** -0.5 * math.log2(math.e))).astype(BF16)
    lo = lax.broadcasted_iota(jnp.int32, k.shape, 1) < SWA_HEAD_DIM
    for src, dst in ((k, k_ref), (v, v_ref)):
        sw = pltpu.roll(src, SWA_HEAD_DIM, axis=1)
        dst[:, 0 * LANES:1 * LANES] = jnp.where(lo, src, 0.0).astype(BF16)
        dst[:, 1 * LANES:2 * LANES] = jnp.where(lo, 0.0, sw).astype(BF16)
        dst[:, 2 * LANES:3 * LANES] = jnp.where(lo, sw, 0.0).astype(BF16)
        dst[:, 3 * LANES:4 * LANES] = jnp.where(lo, 0.0, src).astype(BF16)


def _hy_swa_kernel(h_ref, why_ref, w_ref, cos_ref, sin_ref, hy_ref, q_ref, k_ref, v_ref, *, rope):
    hy_ref[...] = _dot(h_ref[...], why_ref[...])
    _swaproj_kernel(h_ref, w_ref, cos_ref, sin_ref, q_ref, k_ref, v_ref, rope=rope)


def _swa_proj(h, w, cos_t, sin_t, rope, w_hy=None, tm=ROW_TILE):
    m = h.shape[0]
    nt = cos_t.shape[0] // tm
    o = jax.ShapeDtypeStruct((m, 4 * LANES), BF16)
    ospec = pl.BlockSpec((tm, 4 * LANES), lambda i: (i, 0))
    tspec = pl.BlockSpec((tm, _SWA_Q), lambda i: (i % nt, 0))
    hspec = pl.BlockSpec((tm, D_MODEL), lambda i: (i, 0))
    if w_hy is None:
        return pl.pallas_call(
            functools.partial(_swaproj_kernel, rope=rope),
            out_shape=(o, o, o),
            grid=(m // tm,),
            in_specs=[hspec, _const_spec(w.shape), tspec, tspec],
            out_specs=(ospec, ospec, ospec),
            compiler_params=_cparams("parallel"),
            name="swa_proj",
        )(h, w, cos_t, sin_t)
    n_hy = w_hy.shape[1]
    return pl.pallas_call(
        functools.partial(_hy_swa_kernel, rope=rope),
        out_shape=(jax.ShapeDtypeStruct((m, n_hy), F32), o, o, o),
        grid=(m // tm,),
        in_specs=[hspec, _const_spec(w_hy.shape), _const_spec(w.shape), tspec, tspec],
        out_specs=(pl.BlockSpec((tm, n_hy), lambda i: (i, 0)), ospec, ospec, ospec),
        compiler_params=_cparams("parallel"),
        name="hyena_swa_proj",
    )(h, w_hy, w, cos_t, sin_t)


def _swa_kernel(sink_ref, q_ref, *refs, tq, win, lk, has_lat):
    if has_lat:
        kl_ref, vl_ref, kc_ref, vc_ref, o_ref = refs
        t0 = pl.program_id(1) * tq
        start = pl.multiple_of(jnp.clip(t0 - SWA_WINDOW, 0, lk - win), SWA_WINDOW)
        qpos = t0 + lax.broadcasted_iota(jnp.int32, (tq, win), 0)
        kpos = start + lax.broadcasted_iota(jnp.int32, (tq, win), 1)
        valid = jnp.abs(qpos - kpos) <= SWA_WINDOW
        valid = jnp.concatenate([valid, valid], axis=0)
    else:
        kc_ref, vc_ref, o_ref = refs
    upper = lax.broadcasted_iota(jnp.int32, (2 * tq, 1), 0) < tq
    for g in range(SWA_KV_HEADS):
        j0 = 2 * g
        qb = jnp.concatenate([q_ref[:, j0 * LANES:(j0 + 1) * LANES],
                              q_ref[:, (j0 + 1) * LANES:(j0 + 2) * LANES]], axis=0)
        acc = None
        for par in range(2):
            c0 = (2 * g + par) * LANES
            sk = jnp.where(upper, sink_ref[2 * j0 + par], sink_ref[2 * j0 + 2 + par]) * math.log2(math.e)
            s_c = _dot_nt(qb, kc_ref[:, c0:c0 + LANES])
            m = jnp.maximum(s_c.max(axis=-1, keepdims=True), sk)
            if has_lat:
                s_l = _dot_nt(qb, kl_ref[pl.ds(start, win), c0:c0 + LANES])
                s_l = jnp.where(valid, s_l, NEG_BIG)
                m = jnp.maximum(m, s_l.max(axis=-1, keepdims=True))
            p_c = jnp.exp2(s_c - m)
            den = p_c.sum(axis=-1, keepdims=True) + jnp.exp2(sk - m)
            if has_lat:
                p_l = jnp.exp2(s_l - m)
                den = den + p_l.sum(axis=-1, keepdims=True)
            o = _dot(p_c.astype(BF16), vc_ref[:, c0:c0 + LANES])
            if has_lat:
                o = o + _dot(p_l.astype(BF16), vl_ref[pl.ds(start, win), c0:c0 + LANES])
            o = o * (1.0 / den)
            acc = o if acc is None else acc + o
        o_ref[:, j0 * LANES:(j0 + 1) * LANES] = acc[:tq].astype(o_ref.dtype)
        o_ref[:, (j0 + 1) * LANES:(j0 + 2) * LANES] = acc[tq:].astype(o_ref.dtype)


def _swa_attention(sink, q, lat, ctx, lq, tq=2 * SWA_WINDOW):
    nq = lq // tq
    win = tq + 2 * SWA_WINDOW
    full = lambda n: pl.BlockSpec((n, 4 * LANES), lambda b, i: (b, 0))
    in_specs = [pl.BlockSpec(memory_space=pltpu.SMEM),
                pl.BlockSpec((tq, 4 * LANES), lambda b, i: (b * nq + i, 0))]
    args = [sink, q]
    if lat is not None:
        in_specs += [full(SEQ), full(SEQ)]
        args += list(lat)
    in_specs += [full(CTX_LEN), full(CTX_LEN)]
    args += list(ctx)
    return pl.pallas_call(
        functools.partial(_swa_kernel, tq=tq, win=win, lk=SEQ, has_lat=lat is not None),
        out_shape=jax.ShapeDtypeStruct((BATCH * lq, 4 * LANES), BF16),
        grid=(BATCH, nq),
        in_specs=in_specs,
        out_specs=pl.BlockSpec((tq, 4 * LANES), lambda b, i: (b * nq + i, 0)),
        compiler_params=_cparams("parallel", "arbitrary"),
        name="swa_attention",
    )(*args)


def _out_kernel(a_ref, y_ref, s_ref, w_ref, x_ref, m_ref, g_ref, xo_ref, ho_ref):
    na = MLA_HEADS * MLA_V
    acc = _dot(a_ref[...], w_ref[0:na, :])
    acc = acc + _dot(y_ref[...], w_ref[na:na + HY_WIDTH, :])
    acc = acc + _dot(s_ref[...], w_ref[na + HY_WIDTH:, :])
    x = x_ref[...] + m_ref[2:3, :] * acc
    xo_ref[...] = x
    ho_ref[...] = (_rms(x, g_ref[...]) * (1.0 + m_ref[4:5, :]) + m_ref[3:4, :]).astype(ho_ref.dtype)


def _out_proj(a, y, s, w_out, x, mods, g_ffn, groups, row0, h_dtype, tm=ROW_TILE):
    m = x.shape[0]
    nr = m // groups // tm
    tile = lambda n: pl.BlockSpec((tm, n), lambda g_, i: (g_ * nr + i, 0))
    return pl.pallas_call(
        _out_kernel,
        out_shape=(jax.ShapeDtypeStruct((m, D_MODEL), F32), jax.ShapeDtypeStruct((m, D_MODEL), h_dtype)),
        grid=(groups, nr),
        in_specs=[tile(a.shape[1]), tile(y.shape[1]), tile(s.shape[1]), _const_spec(w_out.shape),
                  tile(D_MODEL), pl.BlockSpec((None, 6, D_MODEL), lambda g_, i: (row0 + g_, 0, 0)),
                  pl.BlockSpec((1, D_MODEL), lambda g_, i: (0, 0))],
        out_specs=(tile(D_MODEL), tile(D_MODEL)),
        compiler_params=_cparams("parallel", "parallel"),
        name="out_proj",
    )(a, y, s, w_out, x, mods, g_ffn.reshape(1, -1))


_META_W1, _META_W2, _META_E1, _META_E2, _META_R1, _META_R2 = range(6)


def _route(h, is_first, w_ref, meta_ref, cnt_ref, carry_ref):
    @pl.when(is_first)
    def _():
        carry_ref[...] = jnp.zeros_like(carry_ref)

    logits = _dot3(h, w_ref[...])
    tm = logits.shape[0]
    lane = lax.broadcasted_iota(jnp.int32, logits.shape, 1).astype(F32)
    logits = jnp.where(lane < N_EXPERTS, logits, NEG_BIG)
    m1 = logits.max(axis=-1, keepdims=True)
    i1 = jnp.where(logits == m1, lane, float(LANES)).min(axis=-1, keepdims=True)
    rest = jnp.where(lane == i1, NEG_BIG, logits)
    m2 = rest.max(axis=-1, keepdims=True)
    i2 = jnp.where(rest == m2, lane, float(LANES)).min(axis=-1, keepdims=True)
    e2 = jnp.exp(m2 - m1)
    w1 = 1.0 / (1.0 + e2)
    hot = jnp.where((lane == i1) | (lane == i2), 1.0, 0.0)
    before = (lax.broadcasted_iota(jnp.int32, (tm, tm), 0) > lax.broadcasted_iota(jnp.int32, (tm, tm), 1))
    seen = _dot(before.astype(F32).astype(BF16), hot.astype(BF16)) + carry_ref[0:1, :]
    r1 = jnp.where(lane == i1, seen, 0.0).sum(axis=-1, keepdims=True)
    r2 = jnp.where(lane == i2, seen, 0.0).sum(axis=-1, keepdims=True)
    carry_ref[...] = carry_ref[...] + hot.sum(axis=0, keepdims=True)
    cnt_ref[...] = carry_ref[...]
    rec = jnp.zeros_like(logits)
    for k, v in ((_META_W1, w1), (_META_W2, e2 * w1), (_META_E1, i1), (_META_E2, i2), (_META_R1, r1), (_META_R2, r2)):
        rec = jnp.where(lane == float(k), v, rec)
    meta_ref[...] = rec


def _router_kernel(h_ref, w_ref, meta_ref, cnt_ref, carry_ref):
    _route(h_ref[...], pl.program_id(0) == 0, w_ref, meta_ref, cnt_ref, carry_ref)


def _router(h, w_router, tm=ROW_TILE):
    m = h.shape[0]
    wp = jnp.zeros((D_MODEL, LANES), F32).at[:, :N_EXPERTS].set(w_router)
    return pl.pallas_call(
        _router_kernel,
        out_shape=(jax.ShapeDtypeStruct((m, LANES), F32), jax.ShapeDtypeStruct((8, LANES), F32)),
        grid=(m // tm,),
        in_specs=[pl.BlockSpec((tm, D_MODEL), lambda i: (i, 0)),
                  pl.BlockSpec((D_MODEL, LANES), lambda i: (0, 0))],
        out_specs=(pl.BlockSpec((tm, LANES), lambda i: (i, 0)), pl.BlockSpec((8, LANES), lambda i: (0, 0))),
        scratch_shapes=[pltpu.VMEM((8, LANES), F32)],
        compiler_params=_cparams("arbitrary"),
        name="moe_router",
    )(h, wp)


MOE_TM = 1024
MOE_BLK = 128
MOE_ROWS = (MOE_TM, 768, 512, 256)


def _moe_plan(meta, counts, m):
    n_steps = -(-2 * m // MOE_TM) + N_EXPERTS
    cnt = counts[0, :N_EXPERTS].astype(jnp.int32)
    steps_e = (cnt + MOE_TM - 1) // MOE_TM
    ends = jnp.cumsum(steps_e)
    first = ends - steps_e
    total = ends[-1]
    e = meta[:, _META_E1:_META_E2 + 1].astype(jnp.int32)
    rank = meta[:, _META_R1:_META_R2 + 1].astype(jnp.int32)
    pos = (first * MOE_TM)[e] + rank
    s_idx = jnp.arange(n_steps, dtype=jnp.int32)
    step_e = jnp.minimum(jnp.searchsorted(ends, s_idx, side="right").astype(jnp.int32), N_EXPERTS - 1)
    valid = jnp.clip(cnt[step_e] - (s_idx - first[step_e]) * MOE_TM, 0, MOE_TM)
    valid = jnp.where(s_idx < total, valid, 0)
    return pos.reshape(-1), step_e, valid, n_steps


def _row_copy(src_ref, src_row, dst_ref, dst_row, sem):
    return pltpu.make_async_copy(src_ref.at[pl.ds(src_row, 1)], dst_ref.at[pl.ds(dst_row, 1)], sem)


def _invert_kernel(pos_ref, sv_ref, src_ref):
    def clear_step(s, c):
        def clear(b, c2):
            for k in range(8):
                src_ref[s * MOE_TM + b * 8 + k] = 0
            return c2

        return lax.fori_loop(sv_ref[s] // 8, MOE_TM // 8, clear, c)

    def place(t, c):
        p0 = pos_ref[2 * t]
        p1 = pos_ref[2 * t + 1]
        src_ref[p0] = t
        src_ref[p1] = t
        return c

    lax.fori_loop(0, sv_ref.shape[0], clear_step, 0)
    lax.fori_loop(0, pos_ref.shape[0] // 2, place, 0, unroll=16)


def _moe_invert(pos, valid, n_rows):
    smem = pl.BlockSpec(memory_space=pltpu.SMEM)
    return pl.pallas_call(
        _invert_kernel,
        out_shape=jax.ShapeDtypeStruct((n_rows,), jnp.int32),
        in_specs=[smem, smem], out_specs=smem,
        name="moe_invert",
    )(pos, valid)


def _block_wait(src_ref, dst_ref, rows, sem):
    pltpu.make_async_copy(src_ref.at[pl.ds(0, rows)], dst_ref.at[pl.ds(0, rows)], sem).wait()


def _moe_ffn_kernel(se_ref, sv_ref, src_ref, h_ref, wg_ref, wu_ref, wd_ref, o_ref, x32_ref, xb_ref, sem):
    del se_ref
    s = pl.program_id(0)
    f = pl.program_id(1)
    valid = sv_ref[s]

    def rows_used(v):
        r = jnp.where(v > 0, MOE_ROWS[-1], 0)
        for small, big in zip(MOE_ROWS[:0:-1], MOE_ROWS[-2::-1]):
            r = jnp.where(v > small, big, r)
        return r

    def blocks(v):
        return rows_used(v) // MOE_BLK

    def gather(step):
        base = step * MOE_TM

        def issue(b, c):
            for k in range(16):
                i = b * 16 + k
                _row_copy(h_ref, src_ref[base + i], x32_ref, i, sem).start()
            return c

        lax.fori_loop(0, rows_used(sv_ref[step]) // 16, issue, 0)

    @pl.when(f == 0)
    def _():
        @pl.when(s == 0)
        def _():
            gather(0)

        def land(i, c):
            _block_wait(h_ref, x32_ref, MOE_BLK, sem)
            return c

        def to_bf16(i, c):
            r0 = pl.multiple_of(i * MOE_BLK, MOE_BLK)
            xb_ref[pl.ds(r0, MOE_BLK), :] = x32_ref[pl.ds(r0, MOE_BLK), :].astype(BF16)
            return c

        lax.fori_loop(0, blocks(valid), land, 0)
        lax.fori_loop(0, blocks(valid), to_bf16, 0)

        @pl.when(s + 1 < pl.num_programs(0))
        def _():
            gather(s + 1)

        o_ref[...] = jnp.zeros_like(o_ref)

    def compute(rows):
        xs = xb_ref[0:rows, :]
        mid = _silu(_dot(xs, wg_ref[...].astype(BF16))) * _dot(xs, wu_ref[...].astype(BF16))
        o_ref[0:rows, :] += _dot(mid.astype(BF16), wd_ref[...].astype(BF16))

    for k, rows in enumerate(MOE_ROWS):
        lower = MOE_ROWS[k + 1] if k + 1 < len(MOE_ROWS) else 0

        @pl.when((valid > lower) & (valid <= rows))
        def _():
            compute(rows)


def _moe_ffn(h, src, w_gate, w_up, w_down, step_e, valid, n_steps, tf=FF_TILE):
    assert all(r % MOE_BLK == 0 for r in MOE_ROWS) and MOE_ROWS[0] == MOE_TM
    n_ff = w_gate.shape[-1]
    nf = n_ff // tf
    fidx = lambda s, f, sv: jnp.where(sv[s] > 0, f, nf - 1)
    return pl.pallas_call(
        _moe_ffn_kernel,
        out_shape=jax.ShapeDtypeStruct((n_steps * MOE_TM, D_MODEL), F32),
        grid_spec=pltpu.PrefetchScalarGridSpec(
            num_scalar_prefetch=3, grid=(n_steps, nf),
            in_specs=[
                pl.BlockSpec(memory_space=pl.ANY),
                pl.BlockSpec((None, D_MODEL, tf), lambda s, f, se, sv, sr: (se[s], 0, fidx(s, f, sv))),
                pl.BlockSpec((None, D_MODEL, tf), lambda s, f, se, sv, sr: (se[s], 0, fidx(s, f, sv))),
                pl.BlockSpec((None, tf, D_MODEL), lambda s, f, se, sv, sr: (se[s], fidx(s, f, sv), 0)),
            ],
            out_specs=pl.BlockSpec((MOE_TM, D_MODEL), lambda s, f, se, sv, sr: (s, 0), pipeline_mode=pl.Buffered(1)),
            scratch_shapes=[pltpu.VMEM((MOE_TM, D_MODEL), F32), pltpu.VMEM((MOE_TM, D_MODEL), BF16),
                            pltpu.SemaphoreType.DMA(())]),
        compiler_params=_cparams("arbitrary", "arbitrary"),
        name="moe_experts",
    )(step_e, valid, src, h, w_gate, w_up, w_down)


def _combine_kernel(pos_ref, meta_ref, x_ref, m_ref, gf_ref, ys_ref, o_ref, buf_ref, sems, *, tm, final):
    t = pl.program_id(0)

    def request(tile):
        b = tile % 2

        def issue(i, c):
            for slot in range(2):
                _row_copy(ys_ref, pos_ref[2 * (tile * tm + i) + slot], buf_ref.at[b, slot], i, sems.at[b]).start()
            return c

        lax.fori_loop(0, tm, issue, 0, unroll=8)

    @pl.when(t == 0)
    def _():
        request(0)

    @pl.when(t + 1 < pl.num_programs(0))
    def _():
        request(t + 1)

    cur = t % 2
    for slot in range(2):
        _block_wait(ys_ref, buf_ref.at[cur, slot], tm, sems.at[cur])
    meta = meta_ref[...]
    y = meta[:, _META_W1:_META_W1 + 1] * buf_ref[cur, 0] + meta[:, _META_W2:_META_W2 + 1] * buf_ref[cur, 1]
    x = x_ref[...] + m_ref[5:6, :] * y
    if final:
        x = _rms(x, gf_ref[...])
    o_ref[...] = x


def _moe_combine(pos, meta, x, mods, ys, groups, row0, g_final, tm=ROW_TILE):
    m = x.shape[0]
    nr = m // groups // tm
    final = g_final is not None
    gf = (g_final if final else jnp.ones((D_MODEL,), F32)).reshape(1, D_MODEL)
    return pl.pallas_call(
        functools.partial(_combine_kernel, tm=tm, final=final),
        out_shape=jax.ShapeDtypeStruct((m, D_MODEL), F32),
        grid_spec=pltpu.PrefetchScalarGridSpec(
            num_scalar_prefetch=1, grid=(m // tm,),
            in_specs=[pl.BlockSpec((tm, LANES), lambda i, p: (i, 0)),
                      pl.BlockSpec((tm, D_MODEL), lambda i, p: (i, 0)),
                      pl.BlockSpec((None, 6, D_MODEL), lambda i, p: (row0 + i // nr, 0, 0)),
                      pl.BlockSpec((1, D_MODEL), lambda i, p: (0, 0)),
                      pl.BlockSpec(memory_space=pl.ANY)],
            out_specs=pl.BlockSpec((tm, D_MODEL), lambda i, p: (i, 0)),
            scratch_shapes=[pltpu.VMEM((2, 2, tm, D_MODEL), F32), pltpu.SemaphoreType.DMA((2,))]),
        compiler_params=_cparams("arbitrary"),
        name="moe_combine",
    )(pos, meta, x, mods, gf, ys)


def _moe(h, meta, counts, x, mods, groups, row0, w_gate, w_up, w_down, g_final):
    m = h.shape[0]
    pos, step_e, valid, n_steps = _moe_plan(meta, counts, m)
    src = _moe_invert(pos, valid, n_steps * MOE_TM)
    ys = _moe_ffn(h, src, w_gate, w_up, w_down, step_e, valid, n_steps)
    return _moe_combine(pos, meta, x, mods, ys, groups, row0, g_final)


def _ffn_kernel(h_ref, wg_ref, wu_ref, wd_ref, x_ref, m_ref, gn_ref, mn_ref, o_ref, *maybe_hn_ref):
    f = pl.program_id(2)

    @pl.when(f == 0)
    def _():
        o_ref[...] = jnp.zeros_like(o_ref)

    h = h_ref[...]
    mid = _silu(_dot(h, wg_ref[...])) * _dot(h, wu_ref[...])
    o_ref[...] += _dot(mid.astype(BF16), wd_ref[...])

    @pl.when(f == pl.num_programs(2) - 1)
    def _():
        x = x_ref[...] + m_ref[5:6, :] * o_ref[...]
        o_ref[...] = x
        for hn_ref in maybe_hn_ref:
            hn_ref[...] = (_rms(x, gn_ref[...]) * (1.0 + mn_ref[1:2, :]) + mn_ref[0:1, :]).astype(hn_ref.dtype)


def _ffn(h, w_gate, w_up, w_down, x, mods, groups, row0, nxt, tm, tf=FF_TILE):
    m = x.shape[0]
    n_ff = w_gate.shape[1]
    nr = m // groups // tm
    g_next, mods_next = nxt if nxt is not None else (jnp.ones((D_MODEL,), F32), mods)
    once = pl.Buffered(1)
    tile = lambda n, mode=None: pl.BlockSpec((tm, n), lambda g_, i, f: (g_ * nr + i, 0), pipeline_mode=mode)
    mod_spec = pl.BlockSpec((None, 6, D_MODEL), lambda g_, i, f: (row0 + g_, 0, 0))
    out_shape = [jax.ShapeDtypeStruct((m, D_MODEL), F32)]
    out_specs = [tile(D_MODEL, once)]
    if nxt is not None:
        out_shape.append(jax.ShapeDtypeStruct((m, D_MODEL), BF16))
        out_specs.append(tile(D_MODEL, once))
    outs = pl.pallas_call(
        _ffn_kernel,
        out_shape=tuple(out_shape),
        grid=(groups, nr, n_ff // tf),
        in_specs=[tile(D_MODEL),
                  pl.BlockSpec((D_MODEL, tf), lambda g_, i, f: (0, f)),
                  pl.BlockSpec((D_MODEL, tf), lambda g_, i, f: (0, f)),
                  pl.BlockSpec((tf, D_MODEL), lambda g_, i, f: (f, 0)),
                  tile(D_MODEL, once), mod_spec,
                  pl.BlockSpec((1, D_MODEL), lambda g_, i, f: (0, 0)), mod_spec],
        out_specs=tuple(out_specs),
        compiler_params=_cparams("parallel", "parallel", "arbitrary"),
        name="swiglu_ffn",
    )(h, w_gate, w_up, w_down, x, mods, g_next.reshape(1, D_MODEL), mods_next)
    return outs if nxt is not None else (outs[0], None)


def _swap_halves(w):
    half = w.shape[-1] // 2
    return jnp.concatenate([w[..., half:], w[..., :half]], axis=-1)


def _swap_heads(w, heads, dim):
    k = w.shape[0]
    return _swap_halves(w.reshape(k, heads, dim)).reshape(k, heads * dim)


def _rope_tables():
    rows = SEQ // GRID_W
    row = np.repeat(np.arange(rows), GRID_W).astype(np.float32)
    col = np.tile(np.arange(GRID_W), rows).astype(np.float32)
    quarter = MLA_ROPE // 4
    freqs = (np.float32(ROPE_THETA) ** (-np.arange(quarter, dtype=np.float32) / quarter)).astype(np.float32)
    ang = np.concatenate([row[:, None] * freqs[None], col[:, None] * freqs[None]], axis=-1)
    cos = np.cos(ang.astype(np.float64))
    sin = np.sin(ang.astype(np.float64))
    cos2 = np.concatenate([cos, cos], axis=-1)
    sin2 = np.concatenate([-sin, sin], axis=-1)
    tab_lat = np.concatenate([cos2, sin2], axis=-1)
    tab_ctx = np.concatenate([np.ones((CTX_LEN, MLA_ROPE)), np.zeros((CTX_LEN, MLA_ROPE))], axis=-1)
    return dict(
        tab_lat=jnp.asarray(tab_lat, F32), tab_ctx=jnp.asarray(tab_ctx, F32),
        cos8=jnp.asarray(np.tile(cos2, (1, SWA_HEADS)), F32),
        sin8=jnp.asarray(np.tile(sin2, (1, SWA_HEADS)), F32),
        ones8=jnp.ones((CTX_LEN, _SWA_Q), F32), zeros8=jnp.zeros((CTX_LEN, _SWA_Q), F32),
    )


def _layer_weights(p, l):
    w_in = p["w_in"][l]
    cq = w_in[:, _O_CQ:_O_CKV]
    ckv = w_in[:, _O_CKV:_O_KPE]
    kpe = w_in[:, _O_KPE:_O_HY]
    hy = w_in[:, _O_HY:_O_SQ]
    sq = w_in[:, _O_SQ:_O_SK]
    sk = w_in[:, _O_SK:_O_SV]
    sv = w_in[:, _O_SV:]
    dq = MLA_NOPE + MLA_ROPE
    wq = p["w_q_up"][l].reshape(MLA_Q_LORA, MLA_HEADS, dq)
    wq = jnp.concatenate([wq, _swap_halves(wq[..., MLA_NOPE:])], axis=-1)
    wkv = p["w_kv_up"][l].reshape(MLA_KV_LORA, MLA_HEADS, MLA_NOPE + MLA_V)
    return dict(
        w_cq=cq.astype(BF16),
        w_ckv=jnp.concatenate([ckv, kpe, _swap_halves(kpe)], axis=-1).astype(BF16),
        w_hy=hy.astype(BF16),
        w_swa=jnp.concatenate([sq, _swap_heads(sq, SWA_HEADS, SWA_HEAD_DIM), sk,
                               _swap_heads(sk, SWA_KV_HEADS, SWA_HEAD_DIM), sv], axis=-1).astype(BF16),
        w_q=jnp.transpose(wq, (1, 0, 2)).astype(BF16),
        w_kv=jnp.transpose(wkv, (1, 0, 2)).astype(BF16),
        w_out=p["w_out"][l].astype(BF16),
        g_q=p["g_q"][l], g_kv=p["g_kv"][l],
        hy_conv_w=p["hy_conv_w"][l], hy_conv_b=p["hy_conv_b"][l],
        hy_w1=p["hy_w1"][l], hy_b1=p["hy_b1"][l], hy_w2=p["hy_w2"][l], hy_b2=p["hy_b2"][l],
        hy_w3=p["hy_w3"][l], hy_b3=p["hy_b3"][l], hy_w_filt=p["hy_w_filt"][l],
        hy_freq=p["hy_freq"][l], hy_skip=p["hy_skip"][l], swa_sink=p["swa_sink"][l],
    )


def _hyena(hy, lw, n_len, tabs):
    x0, vx, vxb = _hy_pre(hy, lw["hy_conv_w"], lw["hy_conv_b"], n_len)
    spectra = _hy_filters(lw, n_len, tabs)
    y = _hy_long_conv(x0, vx, vxb, spectra, lw["hy_skip"], n_len, tabs)
    return y.reshape(BATCH * n_len, HY_WIDTH)


def _mixer(h_lat, h_ctx, lw, rope, hy_tabs, need_ctx):
    q_l, k_l, v_l = _mla_proj(h_lat, lw, rope["tab_lat"], tm=LATENT_ROW_TILE)
    if need_ctx:
        q_c, k_c, v_c = _mla_proj(h_ctx, lw, rope["tab_ctx"], tm=CTX_LEN)
    else:
        k_c, v_c = _kv_proj(h_ctx, lw["w_ckv"], lw["g_kv"], lw["w_kv"], rope["tab_ctx"], tm=CTX_LEN)
    a_l = _mla_attention(q_l, [(k_l, v_l, SEQ), (k_c, v_c, CTX_LEN)], SEQ, tq=SEQ)
    hy_l, sq_l, kk_l, vv_l = _swa_proj(h_lat, lw["w_swa"], rope["cos8"], rope["sin8"], True, w_hy=lw["w_hy"],
                                       tm=LATENT_ROW_TILE)
    y_l = _hyena(hy_l, lw, SEQ, hy_tabs[SEQ])
    if need_ctx:
        hy_c, sq_c, kk_c, vv_c = _swa_proj(h_ctx, lw["w_swa"], rope["ones8"], rope["zeros8"], False,
                                           w_hy=lw["w_hy"], tm=CTX_LEN)
    else:
        sq_c, kk_c, vv_c = _swa_proj(h_ctx, lw["w_swa"], rope["ones8"], rope["zeros8"], False, tm=CTX_LEN)
    s_l = _swa_attention(lw["swa_sink"], sq_l, (kk_l, vv_l), (kk_c, vv_c), SEQ)
    if not need_ctx:
        return (a_l, y_l, s_l), None
    a_c = _mla_attention(q_c, [(k_c, v_c, CTX_LEN)], CTX_LEN, tq=CTX_LEN)
    y_c = _hyena(hy_c, lw, CTX_LEN, hy_tabs[CTX_LEN])
    s_c = _swa_attention(lw["swa_sink"], sq_c, None, (kk_c, vv_c), CTX_LEN, tq=CTX_LEN)
    return (a_l, y_l, s_l), (a_c, y_c, s_c)


def _forward(p):
    rope = _rope_tables()
    hy_tabs = {n: dict(dft=_dft_tables(n), filt=_filter_tables(n)) for n in (SEQ, CTX_LEN)}
    cvec = jnp.zeros((MOD_ROWS, D_MODEL), F32).at[:BATCH].set(p["c"]).at[BATCH].set(p["c_ctx"])
    mods_all = _modulation(cvec, p["w_mod"], p["b_mod"]).reshape(DEPTH, MOD_ROWS, 6, D_MODEL)

    x_lat = p["x"].reshape(BATCH * SEQ, D_MODEL)
    x_ctx = p["ctx"].reshape(BATCH * CTX_LEN, D_MODEL)
    h_in = [None, None]
    for l in range(DEPTH):
        last = l == DEPTH - 1
        mods = mods_all[l]
        lw = _layer_weights(p, l)
        h_lat = h_in[0] if h_in[0] is not None else _norm_mod(x_lat, p["g_mix"][l], mods, BATCH, 0, 0, BF16)
        h_ctx = h_in[1] if h_in[1] is not None else _norm_mod(x_ctx, p["g_mix"][l], mods, 1, BATCH, 0, BF16)
        mix_l, mix_c = _mixer(h_lat, h_ctx, lw, rope, hy_tabs, not last)
        streams = [(x_lat, mix_l, BATCH, 0, LATENT_ROW_TILE)]
        if not last:
            streams.append((x_ctx, mix_c, 1, BATCH, ROW_TILE))
        outs = []
        h_in = [None, None]
        i = l // 2
        dense = l % 2 == 0
        for n, (x, mix, groups, row0, ffn_tm) in enumerate(streams):
            if dense:
                x, h2 = _out_proj(*mix, lw["w_out"], x, mods, p["g_ffn"][l], groups, row0, BF16)
                nxt = None if last else (p["g_mix"][l + 1], mods_all[l + 1])
                x, h_in[n] = _ffn(h2, p["ffn_w_gate"][i].astype(BF16), p["ffn_w_up"][i].astype(BF16),
                                  p["ffn_w_down"][i].astype(BF16), x, mods, groups, row0, nxt, ffn_tm)
            else:
                x, h2 = _out_proj(*mix, lw["w_out"], x, mods, p["g_ffn"][l], groups, row0, F32)
                meta, counts = _router(h2, p["moe_router"][i])
                g_final = p["g_final"] if (last and n == 0) else None
                x = _moe(h2, meta, counts, x, mods, groups, row0, p["moe_w_gate"][i], p["moe_w_up"][i],
                         p["moe_w_down"][i], g_final)
            outs.append(x)
        x_lat = outs[0]
        if not last:
            x_ctx = outs[1]
    if DEPTH % 2 == 1:
        x_lat = _norm_mod(x_lat, p["g_final"], mods_all[0], BATCH, 0, None, F32)
    return x_lat.reshape(BATCH, SEQ, D_MODEL)


def kernel(x, c, ctx, c_ctx, w_mod, b_mod, g_mix, g_ffn, w_in, g_q, w_q_up, g_kv, w_kv_up, hy_conv_w, hy_conv_b, hy_w1, hy_b1, hy_w2, hy_b2, hy_w3, hy_b3, hy_w_filt, hy_freq, hy_skip, swa_sink, w_out, ffn_w_gate, ffn_w_up, ffn_w_down, moe_router, moe_w_gate, moe_w_up, moe_w_down, g_final):
    return _forward(dict(
        x=x, c=c, ctx=ctx, c_ctx=c_ctx, w_mod=w_mod, b_mod=b_mod, g_mix=g_mix, g_ffn=g_ffn, w_in=w_in,
        g_q=g_q, w_q_up=w_q_up, g_kv=g_kv, w_kv_up=w_kv_up, hy_conv_w=hy_conv_w, hy_conv_b=hy_conv_b,
        hy_w1=hy_w1, hy_b1=hy_b1, hy_w2=hy_w2, hy_b2=hy_b2, hy_w3=hy_w3, hy_b3=hy_b3,
        hy_w_filt=hy_w_filt, hy_freq=hy_freq, hy_skip=hy_skip, swa_sink=swa_sink, w_out=w_out,
        ffn_w_gate=ffn_w_gate, ffn_w_up=ffn_w_up, ffn_w_down=ffn_w_down, moe_router=moe_router,
        moe_w_gate=moe_w_gate, moe_w_up=moe_w_up, moe_w_down=moe_w_down, g_final=g_final))
```

```python
import functools
import math

import numpy as np
import jax
import jax.numpy as jnp
from jax import lax
from jax.experimental import pallas as pl
from jax.experimental.pallas import tpu as pltpu

F32 = jnp.float32
BF16 = jnp.bfloat16

D_MODEL = 2048
BATCH = 4
SEQ = 2048
DEPTH = 2
CTX_LEN = 256
GRID_W = 64
NORM_EPS = 1e-6
ROPE_THETA = 10000.0
MLA_HEADS = 8
MLA_NOPE = 128
MLA_ROPE = 64
MLA_V = 128
MLA_Q_LORA = 768
MLA_KV_LORA = 512
HY_WIDTH = 512
HY_BANDS = 16
HY_EMB = 1 + 2 * HY_BANDS
HY_FILTER_HIDDEN = 64
HY_DECAY_TARGET = 1e-2
HY_QUICK_DECAY_PCT = 0.3
HY_SLOW_DECAY_PCT = 1.5
SWA_HEADS = 8
SWA_KV_HEADS = 2
SWA_HEAD_DIM = 64
SWA_WINDOW = 128
N_EXPERTS = 8
D_FF = 5632
D_FF_EXPERT = 7168

LANES = 128
MXU_DIM = 256
VMEM_LIMIT_BYTES = 56 * 1024 * 1024
NEG_BIG = -1e30
MOD_ROWS = 8

ROW_TILE = 512
LATENT_ROW_TILE = 1024
FF_TILE = 512

_O_CQ = 0
_O_CKV = _O_CQ + MLA_Q_LORA
_O_KPE = _O_CKV + MLA_KV_LORA
_O_HY = _O_KPE + MLA_ROPE
_O_SQ = _O_HY + 3 * HY_WIDTH
_O_SK = _O_SQ + SWA_HEADS * SWA_HEAD_DIM
_O_SV = _O_SK + SWA_KV_HEADS * SWA_HEAD_DIM


def _cparams(*sem):
    return pltpu.CompilerParams(dimension_semantics=sem, vmem_limit_bytes=VMEM_LIMIT_BYTES)


def _dot(a, b):
    return jnp.dot(a, b, preferred_element_type=F32)


def _dot_nt(a, b):
    return lax.dot_general(a, b, (((1,), (1,)), ((), ())), preferred_element_type=F32)


def _split_bf16(a):
    hi = a.astype(BF16)
    lo = (a - hi.astype(F32)).astype(BF16)
    return hi, lo


def _dot3(a, b):
    ah, al = _split_bf16(a)
    bh, bl = _split_bf16(b)
    return _dot(ah, bh) + (_dot(al, bh) + _dot(ah, bl))


def _silu(x):
    return x / (1.0 + jnp.exp(-x))


def _rms(x, g):
    ms = jnp.mean(x * x, axis=-1, keepdims=True)
    return x * lax.rsqrt(ms + NORM_EPS) * g


def _const_spec(shape):
    nd = len(shape)
    return pl.BlockSpec(shape, lambda *_: (0,) * nd, pipeline_mode=pl.Buffered(1))


def _mod_kernel(c_ref, w_ref, b_ref, o_ref):
    ah, al = _split_bf16(_silu(c_ref[...]))
    wh, wl = _split_bf16(w_ref[0])
    r = _dot(jnp.concatenate([ah, al], axis=0), wh)
    o_ref[0] = r[:MOD_ROWS] + (r[MOD_ROWS:] + _dot(ah, wl)) + b_ref[0]


def _modulation(cvec, w_mod, b_mod):
    tn = 1024
    n = 6 * D_MODEL
    return pl.pallas_call(
        _mod_kernel,
        out_shape=jax.ShapeDtypeStruct((DEPTH, MOD_ROWS, n), F32),
        grid=(DEPTH, n // tn),
        in_specs=[
            pl.BlockSpec((MOD_ROWS, D_MODEL), lambda l, j: (0, 0)),
            pl.BlockSpec((1, D_MODEL, tn), lambda l, j: (l, 0, j)),
            pl.BlockSpec((1, 1, tn), lambda l, j: (l, 0, j)),
        ],
        out_specs=pl.BlockSpec((1, MOD_ROWS, tn), lambda l, j: (l, 0, j)),
        compiler_params=_cparams("arbitrary", "arbitrary"),
        name="adaln_mod",
    )(cvec, w_mod, b_mod.reshape(DEPTH, 1, n))


def _norm_kernel(x_ref, g_ref, m_ref, o_ref, *, si):
    y = _rms(x_ref[...], g_ref[...])
    if si is not None:
        y = y * (1.0 + m_ref[si + 1:si + 2, :]) + m_ref[si:si + 1, :]
    o_ref[...] = y.astype(o_ref.dtype)


def _norm_mod(x, g, mods, groups, row0, si, out_dtype, tm=ROW_TILE):
    m = x.shape[0]
    nr = m // groups // tm
    return pl.pallas_call(
        functools.partial(_norm_kernel, si=si),
        out_shape=jax.ShapeDtypeStruct((m, D_MODEL), out_dtype),
        grid=(groups, nr),
        in_specs=[
            pl.BlockSpec((tm, D_MODEL), lambda g_, i: (g_ * nr + i, 0)),
            pl.BlockSpec((1, D_MODEL), lambda g_, i: (0, 0)),
            pl.BlockSpec((None, 6, D_MODEL), lambda g_, i: (row0 + g_, 0, 0)),
        ],
        out_specs=pl.BlockSpec((tm, D_MODEL), lambda g_, i: (g_ * nr + i, 0)),
        compiler_params=_cparams("parallel", "parallel"),
        name="norm_mod",
    )(x, g.reshape(1, D_MODEL), mods)


def _rope_halves(pe_pair, tab):
    r = pe_pair * tab
    return r + pltpu.roll(r, MLA_ROPE, axis=1)


MLA_Q_SCALE = (MLA_NOPE + MLA_ROPE) ** -0.5 * math.log2(math.e)


def _qproj_body(h_ref, wc_ref, g_ref, wq_ref, tab_ref, o_ref):
    cq = _dot(h_ref[...], wc_ref[...])
    cqn = _rms(cq, g_ref[...]).astype(BF16)
    tab = tab_ref[...] * MLA_Q_SCALE
    for hh in range(MLA_HEADS):
        r = _dot(cqn, wq_ref[hh])
        o_ref[hh, :, 0:MLA_NOPE] = (r[:, :MLA_NOPE] * MLA_Q_SCALE).astype(BF16)
        o_ref[hh, :, MLA_NOPE:] = _rope_halves(r[:, MLA_NOPE:], tab).astype(BF16)


def _mla_proj_kernel(h_ref, wcq_ref, gq_ref, wq_ref, wckv_ref, gkv_ref, wkv_ref, tab_ref, q_ref, k_ref, v_ref):
    _qproj_body(h_ref, wcq_ref, gq_ref, wq_ref, tab_ref, q_ref)
    _kvproj_kernel(h_ref, wckv_ref, gkv_ref, wkv_ref, tab_ref, k_ref, v_ref)


def _mla_proj(h, lw, tab, tm=ROW_TILE):
    m = h.shape[0]
    nt = tab.shape[0] // tm
    head_out = jax.ShapeDtypeStruct((MLA_HEADS, m, MXU_DIM), BF16)
    head_spec = pl.BlockSpec((MLA_HEADS, tm, MXU_DIM), lambda i: (0, i, 0))
    return pl.pallas_call(
        _mla_proj_kernel,
        out_shape=(head_out, head_out, head_out),
        grid=(m // tm,),
        in_specs=[
            pl.BlockSpec((tm, D_MODEL), lambda i: (i, 0)),
            _const_spec(lw["w_cq"].shape), _const_spec((1, MLA_Q_LORA)), _const_spec(lw["w_q"].shape),
            _const_spec(lw["w_ckv"].shape), _const_spec((1, MLA_KV_LORA)), _const_spec(lw["w_kv"].shape),
            pl.BlockSpec((tm, LANES), lambda i: (i % nt, 0)),
        ],
        out_specs=(head_spec, head_spec, head_spec),
        compiler_params=_cparams("parallel"),
        name="mla_proj",
    )(h, lw["w_cq"], lw["g_q"].reshape(1, -1), lw["w_q"], lw["w_ckv"], lw["g_kv"].reshape(1, -1), lw["w_kv"], tab)


def _kvproj_kernel(h_ref, wc_ref, g_ref, wkv_ref, tab_ref, k_ref, v_ref):
    t = _dot(h_ref[...], wc_ref[...])
    ckvn = _rms(t[:, :MLA_KV_LORA], g_ref[...]).astype(BF16)
    rot = _rope_halves(t[:, MLA_KV_LORA:], tab_ref[...])
    lane = lax.broadcasted_iota(jnp.int32, rot.shape, 1)
    kpe = jnp.where(lane < MLA_ROPE, rot, 0.0).astype(BF16)
    for hh in range(MLA_HEADS):
        r = _dot(ckvn, wkv_ref[hh])
        k_ref[hh, :, 0:MLA_NOPE] = r[:, :MLA_NOPE].astype(BF16)
        k_ref[hh, :, MLA_NOPE:] = kpe
        v_ref[hh, :, 0:MLA_V] = r[:, MLA_NOPE:].astype(BF16)
        v_ref[hh, :, MLA_V:] = jnp.ones((r.shape[0], MXU_DIM - MLA_V), BF16)


def _kv_proj(h, w_ckv, g_kv, w_kv, tab, tm=ROW_TILE):
    m = h.shape[0]
    nt = tab.shape[0] // tm
    return pl.pallas_call(
        _kvproj_kernel,
        out_shape=(jax.ShapeDtypeStruct((MLA_HEADS, m, MXU_DIM), BF16),
                   jax.ShapeDtypeStruct((MLA_HEADS, m, MXU_DIM), BF16)),
        grid=(m // tm,),
        in_specs=[
            pl.BlockSpec((tm, D_MODEL), lambda i: (i, 0)),
            _const_spec(w_ckv.shape),
            _const_spec((1, MLA_KV_LORA)),
            _const_spec(w_kv.shape),
            pl.BlockSpec((tm, LANES), lambda i: (i % nt, 0)),
        ],
        out_specs=(pl.BlockSpec((MLA_HEADS, tm, MXU_DIM), lambda i: (0, i, 0)),
                   pl.BlockSpec((MLA_HEADS, tm, MXU_DIM), lambda i: (0, i, 0))),
        compiler_params=_cparams("parallel"),
        name="mla_kv_proj",
    )(h, w_ckv, g_kv.reshape(1, -1), w_kv, tab)


MLA_KEY_CHUNK = 512


def _mla_kernel(q_ref, *refs, seg_lens):
    nseg = len(seg_lens)
    o_ref = refs[2 * nseg]
    q = q_ref[...]
    m = jnp.full((q.shape[0], 1), NEG_BIG, F32)
    acc = jnp.zeros((q.shape[0], MXU_DIM), F32)
    for s, lk in enumerate(seg_lens):
        k_ref, v_ref = refs[2 * s], refs[2 * s + 1]
        step = min(MLA_KEY_CHUNK, lk)
        for c0 in range(0, lk, step):
            sc = _dot_nt(q, k_ref[c0:c0 + step, :])
            m_new = jnp.maximum(m, sc.max(axis=-1, keepdims=True))
            p = jnp.exp2(sc - m_new).astype(BF16)
            acc = acc * jnp.exp2(m - m_new) + _dot(p, v_ref[c0:c0 + step, :])
            m = m_new
    o_ref[...] = (acc[:, :MLA_V] / acc[:, MLA_V:MLA_V + 1]).astype(o_ref.dtype)


def _mla_attention(q, segs, lq, tq=ROW_TILE):
    nq = lq // tq
    in_specs = [pl.BlockSpec((None, tq, MXU_DIM), lambda b, h, i: (h, b * nq + i, 0))]
    args = [q]
    for k, v, lk in segs:
        in_specs.append(pl.BlockSpec((None, lk, MXU_DIM), lambda b, h, i: (h, b, 0)))
        in_specs.append(pl.BlockSpec((None, lk, MXU_DIM), lambda b, h, i: (h, b, 0)))
        args += [k, v]
    return pl.pallas_call(
        functools.partial(_mla_kernel, seg_lens=tuple(lk for _, _, lk in segs)),
        out_shape=jax.ShapeDtypeStruct((BATCH * lq, MLA_HEADS * MLA_V), BF16),
        grid=(BATCH, MLA_HEADS, nq),
        in_specs=in_specs,
        out_specs=pl.BlockSpec((tq, MLA_V), lambda b, h, i: (b * nq + i, h)),
        compiler_params=_cparams("parallel", "parallel", "arbitrary"),
        name="mla_attention",
    )(*args)


def _dft_tables(n_len):
    n2 = 2 * n_len
    idx = np.arange(n_len, dtype=np.int64)
    ang = (2.0 * np.pi / n2) * ((idx[:, None] * idx[None, :]) % n2).astype(np.float64)
    cm = np.cos(ang)
    sf = np.sin(ang)
    sf[0, :] = 1.0 - 2.0 * (idx % 2)
    return (jnp.asarray(cm, dtype=BF16), jnp.asarray(sf, dtype=BF16),
            jnp.asarray(sf.T.copy(), dtype=BF16))


def _filter_tables(n_len):
    pos = np.arange(n_len, dtype=np.float64)
    t = pos / max(n_len - 1, 1)
    bands = np.linspace(1e-4, HY_BANDS - 1, HY_BANDS)
    ang = (2.0 * math.pi / n_len) * pos[:, None] * bands[None]
    z = np.zeros((n_len, LANES), np.float64)
    z[:, 0] = t
    z[:, 1:1 + HY_BANDS] = np.cos(ang)
    z[:, 1 + HY_BANDS:HY_EMB] = -np.sin(ang)
    deltas = np.linspace(math.log(HY_DECAY_TARGET) / HY_SLOW_DECAY_PCT,
                         math.log(HY_DECAY_TARGET) / HY_QUICK_DECAY_PCT, HY_WIDTH)
    decay = np.exp(-t[:, None] * np.abs(deltas)[None])
    return jnp.asarray(z, dtype=F32), jnp.asarray(decay, dtype=F32)


def _hy_filter_kernel(z_ref, w1_ref, b1_ref, w2_ref, b2_ref, w3_ref, b3_ref, wf_ref, wb_ref,
                      fr_ref, dec_ref, cm_ref, sf_ref, a_ref, b_ref, d_ref, hid_ref, *, n_len):
    @pl.when(pl.program_id(0) == 0)
    def _():
        fr = fr_ref[...]
        h = jnp.sin(fr * (_dot3(z_ref[...], w1_ref[...]) + b1_ref[...]))
        h = jnp.sin(fr * (_dot3(h, w2_ref[...]) + b2_ref[...]))
        hid_ref[...] = jnp.sin(fr * (_dot3(h, w3_ref[...]) + b3_ref[...]))

    h = hid_ref[...]
    dec = dec_ref[...]
    h_f = _dot3(h, wf_ref[...]) * dec
    h_b = _dot3(h, wb_ref[...]) * dec
    row = lax.broadcasted_iota(jnp.int32, h_f.shape, 0)
    h_b = jnp.where(row == 0, 0.0, h_b)
    hs = h_f + h_b
    hd = h_f - h_b
    nc = hs.shape[1]
    hs2 = jnp.concatenate(_split_bf16(hs), axis=1)
    hd2 = jnp.concatenate(_split_bf16(hd), axis=1)
    sf = sf_ref[...]
    t2 = _dot(cm_ref[...], hs2)
    g2 = _dot(sf, hd2)
    n2 = _dot(sf[0:16, :], hs2)[0:1, :]
    t_re = t2[:, :nc] + t2[:, nc:]
    g_t = g2[:, :nc] + g2[:, nc:]
    t_ny = n2[:, :nc] + n2[:, nc:]
    inv_n = 1.0 / (2 * n_len)
    first = row == 0
    a_ref[...] = jnp.where(first, inv_n, 2.0 * inv_n) * t_re
    b_ref[...] = jnp.where(first, 0.0, -2.0 * inv_n * g_t)
    d_ref[...] = jnp.where(first, inv_n * t_ny, 2.0 * inv_n * t_re)


def _hy_filters(lp, n_len, tabs):
    z, decay = tabs["filt"]
    cm, sf, _ = tabs["dft"]
    cb = LANES
    nblk = HY_WIDTH // cb
    w1 = jnp.zeros((LANES, HY_FILTER_HIDDEN), F32).at[:HY_EMB].set(lp["hy_w1"])
    row = lambda a: a.reshape(1, -1)
    small = lambda shape: pl.BlockSpec(shape, lambda j: (0,) * len(shape))
    fh = HY_FILTER_HIDDEN
    out = jax.ShapeDtypeStruct((n_len, HY_WIDTH), F32)
    return pl.pallas_call(
        functools.partial(_hy_filter_kernel, n_len=n_len),
        out_shape=(out, out, out),
        grid=(nblk,),
        in_specs=[
            small((n_len, LANES)), small((LANES, fh)), small((1, fh)), small((fh, fh)), small((1, fh)),
            small((fh, fh)), small((1, fh)),
            pl.BlockSpec((fh, cb), lambda j: (0, j)),
            pl.BlockSpec((fh, cb), lambda j: (0, nblk + j)),
            small((1, fh)),
            pl.BlockSpec((n_len, cb), lambda j: (0, j)),
            _const_spec((n_len, n_len)), _const_spec((n_len, n_len)),
        ],
        out_specs=tuple(pl.BlockSpec((n_len, cb), lambda j: (0, j)) for _ in range(3)),
        scratch_shapes=[pltpu.VMEM((n_len, fh), F32)],
        compiler_params=_cparams("arbitrary"),
        name="hyena_filters",
    )(z, w1, row(lp["hy_b1"]), lp["hy_w2"], row(lp["hy_b2"]), lp["hy_w3"], row(lp["hy_b3"]),
      lp["hy_w_filt"], lp["hy_w_filt"], row(lp["hy_freq"]), decay, cm, sf)


def _hy_pre_kernel(u0_ref, u1_ref, u2_ref, w0_ref, w1_ref, w2_ref, b0_ref, b1_ref, b2_ref,
                   x0_ref, vx_ref, vxb_ref, *, n_len):
    def sconv(u_ref, w_ref, b_ref):
        u = u_ref[...]
        w = w_ref[...]
        row = lax.broadcasted_iota(jnp.int32, u.shape, 0)
        prev = jnp.where(row == 0, 0.0, pltpu.roll(u, 1, axis=0))
        nxt = jnp.where(row == n_len - 1, 0.0, pltpu.roll(u, n_len - 1, axis=0))
        return w[0:1] * prev + w[1:2] * u + w[2:3] * nxt + b_ref[...]

    x0_ref[...] = sconv(u0_ref, w0_ref, b0_ref)
    vx = sconv(u2_ref, w2_ref, b2_ref) * sconv(u1_ref, w1_ref, b1_ref)
    vx_ref[...] = vx
    vxb_ref[...] = vx.astype(BF16)


def _hy_pre(hy, conv_w, conv_b, n_len):
    nblk = HY_WIDTH // LANES
    uspec = lambda part: pl.BlockSpec((None, n_len, LANES), lambda b, j: (b, 0, part * nblk + j))
    wspec = lambda part: pl.BlockSpec((3, LANES), lambda b, j: (0, part * nblk + j))
    bspec = lambda part: pl.BlockSpec((1, LANES), lambda b, j: (0, part * nblk + j))
    ospec = pl.BlockSpec((None, n_len, LANES), lambda b, j: (b, 0, j))
    hy3 = hy.reshape(BATCH, n_len, 3 * HY_WIDTH)
    cb = conv_b.reshape(1, -1)
    return pl.pallas_call(
        functools.partial(_hy_pre_kernel, n_len=n_len),
        out_shape=(jax.ShapeDtypeStruct((BATCH, n_len, HY_WIDTH), F32),
                   jax.ShapeDtypeStruct((BATCH, n_len, HY_WIDTH), F32),
                   jax.ShapeDtypeStruct((BATCH, n_len, HY_WIDTH), BF16)),
        grid=(BATCH, nblk),
        in_specs=[uspec(0), uspec(1), uspec(2), wspec(0), wspec(1), wspec(2), bspec(0), bspec(1), bspec(2)],
        out_specs=(ospec, ospec, ospec),
        compiler_params=_cparams("parallel", "parallel"),
        name="hyena_short_conv",
    )(hy3, hy3, hy3, conv_w, conv_w, conv_w, cb, cb, cb)


def _hy_fwd_kernel(cm_ref, sf_ref, u_ref, a_ref, b_ref, d_ref, q_ref):
    u = u_ref[...]
    ur = _dot(cm_ref[...], u)
    g = _dot(sf_ref[...], u)
    b = b_ref[...]
    q_ref[0] = (a_ref[...] * ur + b * g).astype(BF16)
    q_ref[1] = (d_ref[...] * g - b * ur).astype(BF16)


def _hy_inv_kernel(cm_ref, si_ref, q_ref, x0_ref, vx_ref, skip_ref, o_ref):
    y = _dot(cm_ref[...], q_ref[0]) + _dot(si_ref[...], q_ref[1])
    o_ref[...] = (x0_ref[...] * (y + vx_ref[...] * skip_ref[...])).astype(o_ref.dtype)


def _hy_long_conv(x0, vx, vxb, spectra, skip, n_len, tabs):
    cm, sf, si = tabs["dft"]
    a, b, d = spectra
    tf = min(1024, n_len)
    nf = n_len // tf
    mat = pl.BlockSpec((tf, n_len), lambda bb, f: (f, 0))
    spec = pl.BlockSpec((tf, HY_WIDTH), lambda bb, f: (f, 0))
    q = pl.pallas_call(
        _hy_fwd_kernel,
        out_shape=jax.ShapeDtypeStruct((BATCH, 2, n_len, HY_WIDTH), BF16),
        grid=(BATCH, nf),
        in_specs=[mat, mat, pl.BlockSpec((None, n_len, HY_WIDTH), lambda bb, f: (bb, 0, 0)), spec, spec, spec],
        out_specs=pl.BlockSpec((None, 2, tf, HY_WIDTH), lambda bb, f: (bb, 0, f, 0)),
        compiler_params=_cparams("parallel", "arbitrary"),
        name="hyena_dft_fwd",
    )(cm, sf, vxb, a, b, d)
    tile = pl.BlockSpec((None, tf, HY_WIDTH), lambda bb, f: (bb, f, 0))
    return pl.pallas_call(
        _hy_inv_kernel,
        out_shape=jax.ShapeDtypeStruct((BATCH, n_len, HY_WIDTH), BF16),
        grid=(BATCH, nf),
        in_specs=[mat, mat, pl.BlockSpec((None, 2, n_len, HY_WIDTH), lambda bb, f: (bb, 0, 0, 0)),
                  tile, tile, pl.BlockSpec((1, HY_WIDTH), lambda bb, f: (0, 0))],
        out_specs=tile,
        compiler_params=_cparams("parallel", "arbitrary"),
        name="hyena_dft_inv",
    )(cm, si, q, x0, vx, skip.reshape(1, -1))


_SWA_Q = SWA_HEADS * SWA_HEAD_DIM
_SWA_KV = SWA_KV_HEADS * SWA_HEAD_DIM


def _swaproj_kernel(h_ref, w_ref, cos_ref, sin_ref, q_ref, k_ref, v_ref, *, rope):
    t = _dot(h_ref[...], w_ref[...])
    q = t[:, :_SWA_Q]
    k = t[:, 2 * _SWA_Q:2 * _SWA_Q + _SWA_KV]
    v = t[:, 2 * _SWA_Q + 2 * _SWA_KV:]
    if rope:
        cos = cos_ref[...]
        sin = sin_ref[...]
        q = q * cos + t[:, _SWA_Q:2 * _SWA_Q] * sin
        k = k * cos[:, :_SWA_KV] + t[:, 2 * _SWA_Q + _SWA_KV:2 * _SWA_Q + 2 * _SWA_KV] * sin[:, :_SWA_KV]
    q_ref[...] = (q * (SWA_HEAD_DIM ** -0.5 * math.log2(math.e))).astype(BF16)
    lo = lax.broadcasted_iota(jnp.int32, k.shape, 1) < SWA_HEAD_DIM
    for src, dst in ((k, k_ref), (v, v_ref)):
        sw = pltpu.roll(src, SWA_HEAD_DIM, axis=1)
        dst[:, 0 * LANES:1 * LANES] = jnp.where(lo, src, 0.0).astype(BF16)
        dst[:, 1 * LANES:2 * LANES] = jnp.where(lo, 0.0, sw).astype(BF16)
        dst[:, 2 * LANES:3 * LANES] = jnp.where(lo, sw, 0.0).astype(BF16)
        dst[:, 3 * LANES:4 * LANES] = jnp.where(lo, 0.0, src).astype(BF16)


def _hy_swa_kernel(h_ref, why_ref, w_ref, cos_ref, sin_ref, hy_ref, q_ref, k_ref, v_ref, *, rope):
    hy_ref[...] = _dot(h_ref[...], why_ref[...])
    _swaproj_kernel(h_ref, w_ref, cos_ref, sin_ref, q_ref, k_ref, v_ref, rope=rope)


def _swa_proj(h, w, cos_t, sin_t, rope, w_hy=None, tm=ROW_TILE):
    m = h.shape[0]
    nt = cos_t.shape[0] // tm
    o = jax.ShapeDtypeStruct((m, 4 * LANES), BF16)
    ospec = pl.BlockSpec((tm, 4 * LANES), lambda i: (i, 0))
    tspec = pl.BlockSpec((tm, _SWA_Q), lambda i: (i % nt, 0))
    hspec = pl.BlockSpec((tm, D_MODEL), lambda i: (i, 0))
    if w_hy is None:
        return pl.pallas_call(
            functools.partial(_swaproj_kernel, rope=rope),
            out_shape=(o, o, o),
            grid=(m // tm,),
            in_specs=[hspec, _const_spec(w.shape), tspec, tspec],
            out_specs=(ospec, ospec, ospec),
            compiler_params=_cparams("parallel"),
            name="swa_proj",
        )(h, w, cos_t, sin_t)
    n_hy = w_hy.shape[1]
    return pl.pallas_call(
        functools.partial(_hy_swa_kernel, rope=rope),
        out_shape=(jax.ShapeDtypeStruct((m, n_hy), F32), o, o, o),
        grid=(m // tm,),
        in_specs=[hspec, _const_spec(w_hy.shape), _const_spec(w.shape), tspec, tspec],
        out_specs=(pl.BlockSpec((tm, n_hy), lambda i: (i, 0)), ospec, ospec, ospec),
        compiler_params=_cparams("parallel"),
        name="hyena_swa_proj",
    )(h, w_hy, w, cos_t, sin_t)


def _swa_kernel(sink_ref, q_ref, *refs, tq, win, lk, has_lat):
    if has_lat:
        kl_ref, vl_ref, kc_ref, vc_ref, o_ref = refs
        t0 = pl.program_id(1) * tq
        start = pl.multiple_of(jnp.clip(t0 - SWA_WINDOW, 0, lk - win), SWA_WINDOW)
        qpos = t0 + lax.broadcasted_iota(jnp.int32, (tq, win), 0)
        kpos = start + lax.broadcasted_iota(jnp.int32, (tq, win), 1)
        valid = jnp.abs(qpos - kpos) <= SWA_WINDOW
        valid = jnp.concatenate([valid, valid], axis=0)
    else:
        kc_ref, vc_ref, o_ref = refs
    upper = lax.broadcasted_iota(jnp.int32, (2 * tq, 1), 0) < tq
    for g in range(SWA_KV_HEADS):
        j0 = 2 * g
        qb = jnp.concatenate([q_ref[:, j0 * LANES:(j0 + 1) * LANES],
                              q_ref[:, (j0 + 1) * LANES:(j0 + 2) * LANES]], axis=0)
        acc = None
        for par in range(2):
            c0 = (2 * g + par) * LANES
            sk = jnp.where(upper, sink_ref[2 * j0 + par], sink_ref[2 * j0 + 2 + par]) * math.log2(math.e)
            s_c = _dot_nt(qb, kc_ref[:, c0:c0 + LANES])
            m = jnp.maximum(s_c.max(axis=-1, keepdims=True), sk)
            if has_lat:
                s_l = _dot_nt(qb, kl_ref[pl.ds(start, win), c0:c0 + LANES])
                s_l = jnp.where(valid, s_l, NEG_BIG)
                m = jnp.maximum(m, s_l.max(axis=-1, keepdims=True))
            p_c = jnp.exp2(s_c - m)
            den = p_c.sum(axis=-1, keepdims=True) + jnp.exp2(sk - m)
            if has_lat:
                p_l = jnp.exp2(s_l - m)
                den = den + p_l.sum(axis=-1, keepdims=True)
            o = _dot(p_c.astype(BF16), vc_ref[:, c0:c0 + LANES])
            if has_lat:
                o = o + _dot(p_l.astype(BF16), vl_ref[pl.ds(start, win), c0:c0 + LANES])
            o = o * (1.0 / den)
            acc = o if acc is None else acc + o
        o_ref[:, j0 * LANES:(j0 + 1) * LANES] = acc[:tq].astype(o_ref.dtype)
        o_ref[:, (j0 + 1) * LANES:(j0 + 2) * LANES] = acc[tq:].astype(o_ref.dtype)


def _swa_attention(sink, q, lat, ctx, lq, tq=2 * SWA_WINDOW):
    nq = lq // tq
    win = tq + 2 * SWA_WINDOW
    full = lambda n: pl.BlockSpec((n, 4 * LANES), lambda b, i: (b, 0))
    in_specs = [pl.BlockSpec(memory_space=pltpu.SMEM),
                pl.BlockSpec((tq, 4 * LANES), lambda b, i: (b * nq + i, 0))]
    args = [sink, q]
    if lat is not None:
        in_specs += [full(SEQ), full(SEQ)]
        args += list(lat)
    in_specs += [full(CTX_LEN), full(CTX_LEN)]
    args += list(ctx)
    return pl.pallas_call(
        functools.partial(_swa_kernel, tq=tq, win=win, lk=SEQ, has_lat=lat is not None),
        out_shape=jax.ShapeDtypeStruct((BATCH * lq, 4 * LANES), BF16),
        grid=(BATCH, nq),
        in_specs=in_specs,
        out_specs=pl.BlockSpec((tq, 4 * LANES), lambda b, i: (b * nq + i, 0)),
        compiler_params=_cparams("parallel", "arbitrary"),
        name="swa_attention",
    )(*args)


def _out_kernel(a_ref, y_ref, s_ref, w_ref, x_ref, m_ref, g_ref, xo_ref, ho_ref):
    na = MLA_HEADS * MLA_V
    acc = _dot(a_ref[...], w_ref[0:na, :])
    acc = acc + _dot(y_ref[...], w_ref[na:na + HY_WIDTH, :])
    acc = acc + _dot(s_ref[...], w_ref[na + HY_WIDTH:, :])
    x = x_ref[...] + m_ref[2:3, :] * acc
    xo_ref[...] = x
    ho_ref[...] = (_rms(x, g_ref[...]) * (1.0 + m_ref[4:5, :]) + m_ref[3:4, :]).astype(ho_ref.dtype)


def _out_proj(a, y, s, w_out, x, mods, g_ffn, groups, row0, h_dtype, tm=ROW_TILE):
    m = x.shape[0]
    nr = m // groups // tm
    tile = lambda n: pl.BlockSpec((tm, n), lambda g_, i: (g_ * nr + i, 0))
    return pl.pallas_call(
        _out_kernel,
        out_shape=(jax.ShapeDtypeStruct((m, D_MODEL), F32), jax.ShapeDtypeStruct((m, D_MODEL), h_dtype)),
        grid=(groups, nr),
        in_specs=[tile(a.shape[1]), tile(y.shape[1]), tile(s.shape[1]), _const_spec(w_out.shape),
                  tile(D_MODEL), pl.BlockSpec((None, 6, D_MODEL), lambda g_, i: (row0 + g_, 0, 0)),
                  pl.BlockSpec((1, D_MODEL), lambda g_, i: (0, 0))],
        out_specs=(tile(D_MODEL), tile(D_MODEL)),
        compiler_params=_cparams("parallel", "parallel"),
        name="out_proj",
    )(a, y, s, w_out, x, mods, g_ffn.reshape(1, -1))


_META_W1, _META_W2, _META_E1, _META_E2, _META_R1, _META_R2 = range(6)


def _route(h, is_first, w_ref, meta_ref, cnt_ref, carry_ref):
    @pl.when(is_first)
    def _():
        carry_ref[...] = jnp.zeros_like(carry_ref)

    logits = _dot3(h, w_ref[...])
    tm = logits.shape[0]
    lane = lax.broadcasted_iota(jnp.int32, logits.shape, 1).astype(F32)
    logits = jnp.where(lane < N_EXPERTS, logits, NEG_BIG)
    m1 = logits.max(axis=-1, keepdims=True)
    i1 = jnp.where(logits == m1, lane, float(LANES)).min(axis=-1, keepdims=True)
    rest = jnp.where(lane == i1, NEG_BIG, logits)
    m2 = rest.max(axis=-1, keepdims=True)
    i2 = jnp.where(rest == m2, lane, float(LANES)).min(axis=-1, keepdims=True)
    e2 = jnp.exp(m2 - m1)
    w1 = 1.0 / (1.0 + e2)
    hot = jnp.where((lane == i1) | (lane == i2), 1.0, 0.0)
    before = (lax.broadcasted_iota(jnp.int32, (tm, tm), 0) > lax.broadcasted_iota(jnp.int32, (tm, tm), 1))
    seen = _dot(before.astype(F32).astype(BF16), hot.astype(BF16)) + carry_ref[0:1, :]
    r1 = jnp.where(lane == i1, seen, 0.0).sum(axis=-1, keepdims=True)
    r2 = jnp.where(lane == i2, seen, 0.0).sum(axis=-1, keepdims=True)
    carry_ref[...] = carry_ref[...] + hot.sum(axis=0, keepdims=True)
    cnt_ref[...] = carry_ref[...]
    rec = jnp.zeros_like(logits)
    for k, v in ((_META_W1, w1), (_META_W2, e2 * w1), (_META_E1, i1), (_META_E2, i2), (_META_R1, r1), (_META_R2, r2)):
        rec = jnp.where(lane == float(k), v, rec)
    meta_ref[...] = rec


def _router_kernel(h_ref, w_ref, meta_ref, cnt_ref, carry_ref):
    _route(h_ref[...], pl.program_id(0) == 0, w_ref, meta_ref, cnt_ref, carry_ref)


def _router(h, w_router, tm=ROW_TILE):
    m = h.shape[0]
    wp = jnp.zeros((D_MODEL, LANES), F32).at[:, :N_EXPERTS].set(w_router)
    return pl.pallas_call(
        _router_kernel,
        out_shape=(jax.ShapeDtypeStruct((m, LANES), F32), jax.ShapeDtypeStruct((8, LANES), F32)),
        grid=(m // tm,),
        in_specs=[pl.BlockSpec((tm, D_MODEL), lambda i: (i, 0)),
                  pl.BlockSpec((D_MODEL, LANES), lambda i: (0, 0))],
        out_specs=(pl.BlockSpec((tm, LANES), lambda i: (i, 0)), pl.BlockSpec((8, LANES), lambda i: (0, 0))),
        scratch_shapes=[pltpu.VMEM((8, LANES), F32)],
        compiler_params=_cparams("arbitrary"),
        name="moe_router",
    )(h, wp)


MOE_TM = 1024
MOE_BLK = 128
MOE_ROWS = (MOE_TM, 768, 512, 256)


def _moe_plan(meta, counts, m):
    n_steps = -(-2 * m // MOE_TM) + N_EXPERTS
    cnt = counts[0, :N_EXPERTS].astype(jnp.int32)
    steps_e = (cnt + MOE_TM - 1) // MOE_TM
    ends = jnp.cumsum(steps_e)
    first = ends - steps_e
    total = ends[-1]
    e = meta[:, _META_E1:_META_E2 + 1].astype(jnp.int32)
    rank = meta[:, _META_R1:_META_R2 + 1].astype(jnp.int32)
    pos = (first * MOE_TM)[e] + rank
    s_idx = jnp.arange(n_steps, dtype=jnp.int32)
    step_e = jnp.minimum(jnp.searchsorted(ends, s_idx, side="right").astype(jnp.int32), N_EXPERTS - 1)
    valid = jnp.clip(cnt[step_e] - (s_idx - first[step_e]) * MOE_TM, 0, MOE_TM)
    valid = jnp.where(s_idx < total, valid, 0)
    return pos.reshape(-1), step_e, valid, n_steps


def _row_copy(src_ref, src_row, dst_ref, dst_row, sem):
    return pltpu.make_async_copy(src_ref.at[pl.ds(src_row, 1)], dst_ref.at[pl.ds(dst_row, 1)], sem)


def _invert_kernel(pos_ref, sv_ref, src_ref):
    def clear_step(s, c):
        def clear(b, c2):
            for k in range(8):
                src_ref[s * MOE_TM + b * 8 + k] = 0
            return c2

        return lax.fori_loop(sv_ref[s] // 8, MOE_TM // 8, clear, c)

    def place(t, c):
        p0 = pos_ref[2 * t]
        p1 = pos_ref[2 * t + 1]
        src_ref[p0] = t
        src_ref[p1] = t
        return c

    lax.fori_loop(0, sv_ref.shape[0], clear_step, 0)
    lax.fori_loop(0, pos_ref.shape[0] // 2, place, 0, unroll=16)


def _moe_invert(pos, valid, n_rows):
    smem = pl.BlockSpec(memory_space=pltpu.SMEM)
    return pl.pallas_call(
        _invert_kernel,
        out_shape=jax.ShapeDtypeStruct((n_rows,), jnp.int32),
        in_specs=[smem, smem], out_specs=smem,
        name="moe_invert",
    )(pos, valid)


def _block_wait(src_ref, dst_ref, rows, sem):
    pltpu.make_async_copy(src_ref.at[pl.ds(0, rows)], dst_ref.at[pl.ds(0, rows)], sem).wait()


def _moe_ffn_kernel(se_ref, sv_ref, src_ref, h_ref, wg_ref, wu_ref, wd_ref, o_ref, x32_ref, xb_ref, sem):
    del se_ref
    s = pl.program_id(0)
    f = pl.program_id(1)
    valid = sv_ref[s]

    def rows_used(v):
        r = jnp.where(v > 0, MOE_ROWS[-1], 0)
        for small, big in zip(MOE_ROWS[:0:-1], MOE_ROWS[-2::-1]):
            r = jnp.where(v > small, big, r)
        return r

    def blocks(v):
        return rows_used(v) // MOE_BLK

    def gather(step):
        base = step * MOE_TM

        def issue(b, c):
            for k in range(16):
                i = b * 16 + k
                _row_copy(h_ref, src_ref[base + i], x32_ref, i, sem).start()
            return c

        lax.fori_loop(0, rows_used(sv_ref[step]) // 16, issue, 0)

    @pl.when(f == 0)
    def _():
        @pl.when(s == 0)
        def _():
            gather(0)

        def land(i, c):
            _block_wait(h_ref, x32_ref, MOE_BLK, sem)
            return c

        def to_bf16(i, c):
            r0 = pl.multiple_of(i * MOE_BLK, MOE_BLK)
            xb_ref[pl.ds(r0, MOE_BLK), :] = x32_ref[pl.ds(r0, MOE_BLK), :].astype(BF16)
            return c

        lax.fori_loop(0, blocks(valid), land, 0)
        lax.fori_loop(0, blocks(valid), to_bf16, 0)

        @pl.when(s + 1 < pl.num_programs(0))
        def _():
            gather(s + 1)

        o_ref[...] = jnp.zeros_like(o_ref)

    def compute(rows):
        xs = xb_ref[0:rows, :]
        mid = _silu(_dot(xs, wg_ref[...].astype(BF16))) * _dot(xs, wu_ref[...].astype(BF16))
        o_ref[0:rows, :] += _dot(mid.astype(BF16), wd_ref[...].astype(BF16))

    for k, rows in enumerate(MOE_ROWS):
        lower = MOE_ROWS[k + 1] if k + 1 < len(MOE_ROWS) else 0

        @pl.when((valid > lower) & (valid <= rows))
        def _():
            compute(rows)


def _moe_ffn(h, src, w_gate, w_up, w_down, step_e, valid, n_steps, tf=FF_TILE):
    assert all(r % MOE_BLK == 0 for r in MOE_ROWS) and MOE_ROWS[0] == MOE_TM
    n_ff = w_gate.shape[-1]
    nf = n_ff // tf
    fidx = lambda s, f, sv: jnp.where(sv[s] > 0, f, nf - 1)
    return pl.pallas_call(
        _moe_ffn_kernel,
        out_shape=jax.ShapeDtypeStruct((n_steps * MOE_TM, D_MODEL), F32),
        grid_spec=pltpu.PrefetchScalarGridSpec(
            num_scalar_prefetch=3, grid=(n_steps, nf),
            in_specs=[
                pl.BlockSpec(memory_space=pl.ANY),
                pl.BlockSpec((None, D_MODEL, tf), lambda s, f, se, sv, sr: (se[s], 0, fidx(s, f, sv))),
                pl.BlockSpec((None, D_MODEL, tf), lambda s, f, se, sv, sr: (se[s], 0, fidx(s, f, sv))),
                pl.BlockSpec((None, tf, D_MODEL), lambda s, f, se, sv, sr: (se[s], fidx(s, f, sv), 0)),
            ],
            out_specs=pl.BlockSpec((MOE_TM, D_MODEL), lambda s, f, se, sv, sr: (s, 0), pipeline_mode=pl.Buffered(1)),
            scratch_shapes=[pltpu.VMEM((MOE_TM, D_MODEL), F32), pltpu.VMEM((MOE_TM, D_MODEL), BF16),
                            pltpu.SemaphoreType.DMA(())]),
        compiler_params=_cparams("arbitrary", "arbitrary"),
        name="moe_experts",
    )(step_e, valid, src, h, w_gate, w_up, w_down)


def _combine_kernel(pos_ref, meta_ref, x_ref, m_ref, gf_ref, ys_ref, o_ref, buf_ref, sems, *, tm, final):
    t = pl.program_id(0)

    def request(tile):
        b = tile % 2

        def issue(i, c):
            for slot in range(2):
                _row_copy(ys_ref, pos_ref[2 * (tile * tm + i) + slot], buf_ref.at[b, slot], i, sems.at[b]).start()
            return c

        lax.fori_loop(0, tm, issue, 0, unroll=8)

    @pl.when(t == 0)
    def _():
        request(0)

    @pl.when(t + 1 < pl.num_programs(0))
    def _():
        request(t + 1)

    cur = t % 2
    for slot in range(2):
        _block_wait(ys_ref, buf_ref.at[cur, slot], tm, sems.at[cur])
    meta = meta_ref[...]
    y = meta[:, _META_W1:_META_W1 + 1] * buf_ref[cur, 0] + meta[:, _META_W2:_META_W2 + 1] * buf_ref[cur, 1]
    x = x_ref[...] + m_ref[5:6, :] * y
    if final:
        x = _rms(x, gf_ref[...])
    o_ref[...] = x


def _moe_combine(pos, meta, x, mods, ys, groups, row0, g_final, tm=ROW_TILE):
    m = x.shape[0]
    nr = m // groups // tm
    final = g_final is not None
    gf = (g_final if final else jnp.ones((D_MODEL,), F32)).reshape(1, D_MODEL)
    return pl.pallas_call(
        functools.partial(_combine_kernel, tm=tm, final=final),
        out_shape=jax.ShapeDtypeStruct((m, D_MODEL), F32),
        grid_spec=pltpu.PrefetchScalarGridSpec(
            num_scalar_prefetch=1, grid=(m // tm,),
            in_specs=[pl.BlockSpec((tm, LANES), lambda i, p: (i, 0)),
                      pl.BlockSpec((tm, D_MODEL), lambda i, p: (i, 0)),
                      pl.BlockSpec((None, 6, D_MODEL), lambda i, p: (row0 + i // nr, 0, 0)),
                      pl.BlockSpec((1, D_MODEL), lambda i, p: (0, 0)),
                      pl.BlockSpec(memory_space=pl.ANY)],
            out_specs=pl.BlockSpec((tm, D_MODEL), lambda i, p: (i, 0)),
            scratch_shapes=[pltpu.VMEM((2, 2, tm, D_MODEL), F32), pltpu.SemaphoreType.DMA((2,))]),
        compiler_params=_cparams("arbitrary"),
        name="moe_combine",
    )(pos, meta, x, mods, gf, ys)


def _moe(h, meta, counts, x, mods, groups, row0, w_gate, w_up, w_down, g_final):
    m = h.shape[0]
    pos, step_e, valid, n_steps = _moe_plan(meta, counts, m)
    src = _moe_invert(pos, valid, n_steps * MOE_TM)
    ys = _moe_ffn(h, src, w_gate, w_up, w_down, step_e, valid, n_steps)
    return _moe_combine(pos, meta, x, mods, ys, groups, row0, g_final)


def _ffn_kernel(h_ref, wg_ref, wu_ref, wd_ref, x_ref, m_ref, gn_ref, mn_ref, o_ref, *maybe_hn_ref):
    f = pl.program_id(2)

    @pl.when(f == 0)
    def _():
        o_ref[...] = jnp.zeros_like(o_ref)

    h = h_ref[...]
    mid = _silu(_dot(h, wg_ref[...])) * _dot(h, wu_ref[...])
    o_ref[...] += _dot(mid.astype(BF16), wd_ref[...])

    @pl.when(f == pl.num_programs(2) - 1)
    def _():
        x = x_ref[...] + m_ref[5:6, :] * o_ref[...]
        o_ref[...] = x
        for hn_ref in maybe_hn_ref:
            hn_ref[...] = (_rms(x, gn_ref[...]) * (1.0 + mn_ref[1:2, :]) + mn_ref[0:1, :]).astype(hn_ref.dtype)


def _ffn(h, w_gate, w_up, w_down, x, mods, groups, row0, nxt, tm, tf=FF_TILE):
    m = x.shape[0]
    n_ff = w_gate.shape[1]
    nr = m // groups // tm
    g_next, mods_next = nxt if nxt is not None else (jnp.ones((D_MODEL,), F32), mods)
    once = pl.Buffered(1)
    tile = lambda n, mode=None: pl.BlockSpec((tm, n), lambda g_, i, f: (g_ * nr + i, 0), pipeline_mode=mode)
    mod_spec = pl.BlockSpec((None, 6, D_MODEL), lambda g_, i, f: (row0 + g_, 0, 0))
    out_shape = [jax.ShapeDtypeStruct((m, D_MODEL), F32)]
    out_specs = [tile(D_MODEL, once)]
    if nxt is not None:
        out_shape.append(jax.ShapeDtypeStruct((m, D_MODEL), BF16))
        out_specs.append(tile(D_MODEL, once))
    outs = pl.pallas_call(
        _ffn_kernel,
        out_shape=tuple(out_shape),
        grid=(groups, nr, n_ff // tf),
        in_specs=[tile(D_MODEL),
                  pl.BlockSpec((D_MODEL, tf), lambda g_, i, f: (0, f)),
                  pl.BlockSpec((D_MODEL, tf), lambda g_, i, f: (0, f)),
                  pl.BlockSpec((tf, D_MODEL), lambda g_, i, f: (f, 0)),
                  tile(D_MODEL, once), mod_spec,
                  pl.BlockSpec((1, D_MODEL), lambda g_, i, f: (0, 0)), mod_spec],
        out_specs=tuple(out_specs),
        compiler_params=_cparams("parallel", "parallel", "arbitrary"),
        name="swiglu_ffn",
    )(h, w_gate, w_up, w_down, x, mods, g_next.reshape(1, D_MODEL), mods_next)
    return outs if nxt is not None else (outs[0], None)


def _swap_halves(w):
    half = w.shape[-1] // 2
    return jnp.concatenate([w[..., half:], w[..., :half]], axis=-1)


def _swap_heads(w, heads, dim):
    k = w.shape[0]
    return _swap_halves(w.reshape(k, heads, dim)).reshape(k, heads * dim)


def _rope_tables():
    rows = SEQ // GRID_W
    row = np.repeat(np.arange(rows), GRID_W).astype(np.float32)
    col = np.tile(np.arange(GRID_W), rows).astype(np.float32)
    quarter = MLA_ROPE // 4
    freqs = (np.float32(ROPE_THETA) ** (-np.arange(quarter, dtype=np.float32) / quarter)).astype(np.float32)
    ang = np.concatenate([row[:, None] * freqs[None], col[:, None] * freqs[None]], axis=-1)
    cos = np.cos(ang.astype(np.float64))
    sin = np.sin(ang.astype(np.float64))
    cos2 = np.concatenate([cos, cos], axis=-1)
    sin2 = np.concatenate([-sin, sin], axis=-1)
    tab_lat = np.concatenate([cos2, sin2], axis=-1)
    tab_ctx = np.concatenate([np.ones((CTX_LEN, MLA_ROPE)), np.zeros((CTX_LEN, MLA_ROPE))], axis=-1)
    return dict(
        tab_lat=jnp.asarray(tab_lat, F32), tab_ctx=jnp.asarray(tab_ctx, F32),
        cos8=jnp.asarray(np.tile(cos2, (1, SWA_HEADS)), F32),
        sin8=jnp.asarray(np.tile(sin2, (1, SWA_HEADS)), F32),
        ones8=jnp.ones((CTX_LEN, _SWA_Q), F32), zeros8=jnp.zeros((CTX_LEN, _SWA_Q), F32),
    )


def _layer_weights(p, l):
    w_in = p["w_in"][l]
    cq = w_in[:, _O_CQ:_O_CKV]
    ckv = w_in[:, _O_CKV:_O_KPE]
    kpe = w_in[:, _O_KPE:_O_HY]
    hy = w_in[:, _O_HY:_O_SQ]
    sq = w_in[:, _O_SQ:_O_SK]
    sk = w_in[:, _O_SK:_O_SV]
    sv = w_in[:, _O_SV:]
    dq = MLA_NOPE + MLA_ROPE
    wq = p["w_q_up"][l].reshape(MLA_Q_LORA, MLA_HEADS, dq)
    wq = jnp.concatenate([wq, _swap_halves(wq[..., MLA_NOPE:])], axis=-1)
    wkv = p["w_kv_up"][l].reshape(MLA_KV_LORA, MLA_HEADS, MLA_NOPE + MLA_V)
    return dict(
        w_cq=cq.astype(BF16),
        w_ckv=jnp.concatenate([ckv, kpe, _swap_halves(kpe)], axis=-1).astype(BF16),
        w_hy=hy.astype(BF16),
        w_swa=jnp.concatenate([sq, _swap_heads(sq, SWA_HEADS, SWA_HEAD_DIM), sk,
                               _swap_heads(sk, SWA_KV_HEADS, SWA_HEAD_DIM), sv], axis=-1).astype(BF16),
        w_q=jnp.transpose(wq, (1, 0, 2)).astype(BF16),
        w_kv=jnp.transpose(wkv, (1, 0, 2)).astype(BF16),
        w_out=p["w_out"][l].astype(BF16),
        g_q=p["g_q"][l], g_kv=p["g_kv"][l],
        hy_conv_w=p["hy_conv_w"][l], hy_conv_b=p["hy_conv_b"][l],
        hy_w1=p["hy_w1"][l], hy_b1=p["hy_b1"][l], hy_w2=p["hy_w2"][l], hy_b2=p["hy_b2"][l],
        hy_w3=p["hy_w3"][l], hy_b3=p["hy_b3"][l], hy_w_filt=p["hy_w_filt"][l],
        hy_freq=p["hy_freq"][l], hy_skip=p["hy_skip"][l], swa_sink=p["swa_sink"][l],
    )


def _hyena(hy, lw, n_len, tabs):
    x0, vx, vxb = _hy_pre(hy, lw["hy_conv_w"], lw["hy_conv_b"], n_len)
    spectra = _hy_filters(lw, n_len, tabs)
    y = _hy_long_conv(x0, vx, vxb, spectra, lw["hy_skip"], n_len, tabs)
    return y.reshape(BATCH * n_len, HY_WIDTH)


def _mixer(h_lat, h_ctx, lw, rope, hy_tabs, need_ctx):
    q_l, k_l, v_l = _mla_proj(h_lat, lw, rope["tab_lat"], tm=LATENT_ROW_TILE)
    if need_ctx:
        q_c, k_c, v_c = _mla_proj(h_ctx, lw, rope["tab_ctx"], tm=CTX_LEN)
    else:
        k_c, v_c = _kv_proj(h_ctx, lw["w_ckv"], lw["g_kv"], lw["w_kv"], rope["tab_ctx"], tm=CTX_LEN)
    a_l = _mla_attention(q_l, [(k_l, v_l, SEQ), (k_c, v_c, CTX_LEN)], SEQ, tq=SEQ)
    hy_l, sq_l, kk_l, vv_l = _swa_proj(h_lat, lw["w_swa"], rope["cos8"], rope["sin8"], True, w_hy=lw["w_hy"],
                                       tm=LATENT_ROW_TILE)
    y_l = _hyena(hy_l, lw, SEQ, hy_tabs[SEQ])
    if need_ctx:
        hy_c, sq_c, kk_c, vv_c = _swa_proj(h_ctx, lw["w_swa"], rope["ones8"], rope["zeros8"], False,
                                           w_hy=lw["w_hy"], tm=CTX_LEN)
    else:
        sq_c, kk_c, vv_c = _swa_proj(h_ctx, lw["w_swa"], rope["ones8"], rope["zeros8"], False, tm=CTX_LEN)
    s_l = _swa_attention(lw["swa_sink"], sq_l, (kk_l, vv_l), (kk_c, vv_c), SEQ)
    if not need_ctx:
        return (a_l, y_l, s_l), None
    a_c = _mla_attention(q_c, [(k_c, v_c, CTX_LEN)], CTX_LEN, tq=CTX_LEN)
    y_c = _hyena(hy_c, lw, CTX_LEN, hy_tabs[CTX_LEN])
    s_c = _swa_attention(lw["swa_sink"], sq_c, None, (kk_c, vv_c), CTX_LEN, tq=CTX_LEN)
    return (a_l, y_l, s_l), (a_c, y_c, s_c)


def _forward(p):
    rope = _rope_tables()
    hy_tabs = {n: dict(dft=_dft_tables(n), filt=_filter_tables(n)) for n in (SEQ, CTX_LEN)}
    cvec = jnp.zeros((MOD_ROWS, D_MODEL), F32).at[:BATCH].set(p["c"]).at[BATCH].set(p["c_ctx"])
    mods_all = _modulation(cvec, p["w_mod"], p["b_mod"]).reshape(DEPTH, MOD_ROWS, 6, D_MODEL)

    x_lat = p["x"].reshape(BATCH * SEQ, D_MODEL)
    x_ctx = p["ctx"].reshape(BATCH * CTX_LEN, D_MODEL)
    h_in = [None, None]
    for l in range(DEPTH):
        last = l == DEPTH - 1
        mods = mods_all[l]
        lw = _layer_weights(p, l)
        h_lat = h_in[0] if h_in[0] is not None else _norm_mod(x_lat, p["g_mix"][l], mods, BATCH, 0, 0, BF16)
        h_ctx = h_in[1] if h_in[1] is not None else _norm_mod(x_ctx, p["g_mix"][l], mods, 1, BATCH, 0, BF16)
        mix_l, mix_c = _mixer(h_lat, h_ctx, lw, rope, hy_tabs, not last)
        streams = [(x_lat, mix_l, BATCH, 0, LATENT_ROW_TILE)]
        if not last:
            streams.append((x_ctx, mix_c, 1, BATCH, LATENT_ROW_TILE))
        outs = []
        h_in = [None, None]
        i = l // 2
        dense = l % 2 == 0
        for n, (x, mix, groups, row0, ffn_tm) in enumerate(streams):
            if dense:
                x, h2 = _out_proj(*mix, lw["w_out"], x, mods, p["g_ffn"][l], groups, row0, BF16)
                nxt = None if last else (p["g_mix"][l + 1], mods_all[l + 1])
                x, h_in[n] = _ffn(h2, p["ffn_w_gate"][i].astype(BF16), p["ffn_w_up"][i].astype(BF16),
                                  p["ffn_w_down"][i].astype(BF16), x, mods, groups, row0, nxt, ffn_tm)
            else:
                x, h2 = _out_proj(*mix, lw["w_out"], x, mods, p["g_ffn"][l], groups, row0, F32)
                meta, counts = _router(h2, p["moe_router"][i])
                g_final = p["g_final"] if (last and n == 0) else None
                x = _moe(h2, meta, counts, x, mods, groups, row0, p["moe_w_gate"][i], p["moe_w_up"][i],
                         p["moe_w_down"][i], g_final)
            outs.append(x)
        x_lat = outs[0]
        if not last:
            x_ctx = outs[1]
    if DEPTH % 2 == 1:
        x_lat = _norm_mod(x_lat, p["g_final"], mods_all[0], BATCH, 0, None, F32)
    return x_lat.reshape(BATCH, SEQ, D_MODEL)


def kernel(x, c, ctx, c_ctx, w_mod, b_mod, g_mix, g_ffn, w_in, g_q, w_q_up, g_kv, w_kv_up, hy_conv_w, hy_conv_b, hy_w1, hy_b1, hy_w2, hy_b2, hy_w3, hy_b3, hy_w_filt, hy_freq, hy_skip, swa_sink, w_out, ffn_w_gate, ffn_w_up, ffn_w_down, moe_router, moe_w_gate, moe_w_up, moe_w_down, g_final):
    return _forward(dict(
        x=x, c=c, ctx=ctx, c_ctx=c_ctx, w_mod=w_mod, b_mod=b_mod, g_mix=g_mix, g_ffn=g_ffn, w_in=w_in,
        g_q=g_q, w_q_up=w_q_up, g_kv=g_kv, w_kv_up=w_kv_up, hy_conv_w=hy_conv_w, hy_conv_b=hy_conv_b,
        hy_w1=hy_w1, hy_b1=hy_b1, hy_w2=hy_w2, hy_b2=hy_b2, hy_w3=hy_w3, hy_b3=hy_b3,
        hy_w_filt=hy_w_filt, hy_freq=hy_freq, hy_skip=hy_skip, swa_sink=swa_sink, w_out=w_out,
        ffn_w_gate=ffn_w_gate, ffn_w_up=ffn_w_up, ffn_w_down=ffn_w_down, moe_router=moe_router,
        moe_w_gate=moe_w_gate, moe_w_up=moe_w_up, moe_w_down=moe_w_down, g_final=g_final))
```
